```python
import jax, jax.numpy as jnp
from jax import lax
import numpy as np

D_MODEL = 1024
BATCH = 16
SEQ = 4096
DEPTH = 2

EXPAND = 2
D_INNER = EXPAND * D_MODEL
EPS = 1e-6

CONV_WIDTH = D_INNER // 2
ATTN_WIDTH = D_INNER - CONV_WIDTH
SB_HEAD_DIM = 128
SB_HEADS = ATTN_WIDTH // SB_HEAD_DIM
CONF_KERNEL = 31
Q_BLOCK = 128
EVEN_SPLITS = [CONV_WIDTH, 2 * CONV_WIDTH, 3 * CONV_WIDTH,
               3 * CONV_WIDTH + ATTN_WIDTH, 3 * CONV_WIDTH + 2 * ATTN_WIDTH,
               3 * CONV_WIDTH + 3 * ATTN_WIDTH]
IN_EVEN = 3 * CONV_WIDTH + 4 * ATTN_WIDTH

SSM_HEAD_DIM = 64
SSM_HEADS = D_INNER // SSM_HEAD_DIM
SSM_GROUPS = 4
SSM_STATE = 128
SSM_CONV = 4
SSM_CHUNK = 128
XBC_WIDTH = D_INNER + 2 * SSM_GROUPS * SSM_STATE
IN_ODD = D_INNER + XBC_WIDTH + SSM_HEADS
DT_MIN = 0.001
DT_MAX = 0.1

N_EVEN = (DEPTH + 1) // 2
N_ODD = DEPTH // 2

kernel_name = "hybrid_conformer_stickbreak_ssd"


def rmsnorm(x, w):
    xf = x.astype(jnp.float32)
    y = xf * lax.rsqrt(jnp.mean(xf * xf, axis=-1, keepdims=True) + EPS)
    return (y * w.astype(jnp.float32)).astype(x.dtype)


def layernorm(x, w, b):
    xf = x.astype(jnp.float32)
    mu = jnp.mean(xf, axis=-1, keepdims=True)
    xc = xf - mu
    var = jnp.mean(xc * xc, axis=-1, keepdims=True)
    return xc * lax.rsqrt(var + EPS) * w.astype(jnp.float32) + b.astype(jnp.float32)


def causal_dwconv(x, w, b):
    k_width = w.shape[0]
    out = lax.conv_general_dilated(
        x, w[:, None, :].astype(x.dtype), window_strides=(1,),
        padding=((k_width - 1, 0),), dimension_numbers=('NWC', 'WIO', 'NWC'),
        feature_group_count=x.shape[-1])
    return out + b.astype(x.dtype)


def stick_breaking_attention(q, k, v):
    S = q.shape[1]
    dh = q.shape[-1]
    qf = q.astype(jnp.float32) * (dh ** -0.5)
    kf = k.astype(jnp.float32)
    vf = v.astype(jnp.float32)
    outs = []
    for i in range(S // Q_BLOCK):
        t0 = i * Q_BLOCK
        kl = t0 + Q_BLOCK
        z = jnp.einsum('bthd,bshd->bhts', qf[:, t0:kl], kf[:, :kl])
        t_idx = t0 + jnp.arange(Q_BLOCK)
        s_idx = jnp.arange(kl)
        mask = s_idx[None, :] < t_idx[:, None]
        log_keep = jnp.where(mask, jax.nn.log_sigmoid(-z), 0.0)
        later = lax.cumsum(log_keep, axis=3, reverse=True) - log_keep
        wts = jnp.where(mask, jnp.exp(jax.nn.log_sigmoid(z) + later), 0.0)
        outs.append(jnp.einsum('bhts,bshd->bthd', wts, vf[:, :kl]))
    return jnp.concatenate(outs, axis=1).astype(q.dtype)


def ssd_scan(x, dt, a, bm, cm):
    bsz, S, H, P = x.shape
    G, N = bm.shape[2], bm.shape[3]
    R = H // G
    L = SSM_CHUNK
    nc = S // L
    xs = (x * dt[..., None]).reshape(bsz, nc, L, G, R, P).transpose(1, 0, 2, 3, 4, 5)
    la = (dt * a).reshape(bsz, nc, L, G, R).transpose(1, 0, 3, 4, 2)
    bc = bm.reshape(bsz, nc, L, G, N).transpose(1, 0, 2, 3, 4)
    cc = cm.reshape(bsz, nc, L, G, N).transpose(1, 0, 2, 3, 4)
    causal = jnp.tril(jnp.ones((L, L), dtype=bool))

    def step(state, inp):
        xck, ack, bck, cck = inp
        cs = jnp.cumsum(ack, axis=-1)
        seg = cs[..., :, None] - cs[..., None, :]
        decay = jnp.exp(jnp.where(causal, seg, -jnp.inf))
        cb = jnp.einsum('blgn,bsgn->bgls', cck, bck)
        y = jnp.einsum('bgls,bgrls,bsgrp->blgrp', cb, decay, xck)
        y = y + jnp.einsum('blgn,bgrpn,bgrl->blgrp', cck, state, jnp.exp(cs))
        tail = jnp.exp(cs[..., -1:] - cs)
        new_state = (state * jnp.exp(cs[..., -1])[..., None, None]
                     + jnp.einsum('bsgn,bgrs,bsgrp->bgrpn', bck, tail, xck))
        return new_state, y

    init = jnp.zeros((bsz, G, R, P, N), jnp.float32)
    _, ys = lax.scan(step, init, (xs, la, bc, cc))
    return ys.transpose(1, 0, 2, 3, 4, 5).reshape(bsz, S, H, P)


def conv_attn_mixer(h, w_in, dw_w, dw_b, ln_w, ln_b, w_out):
    bsz, S, _ = h.shape
    proj = h @ w_in
    glu_a, glu_b, gate_c, q, k, v, gate_a = jnp.split(proj, EVEN_SPLITS, axis=-1)
    u = glu_a * jax.nn.sigmoid(glu_b)
    u = causal_dwconv(u, dw_w, dw_b)
    u = jax.nn.silu(layernorm(u, ln_w, ln_b))
    y_conv = (u * jax.nn.silu(gate_c.astype(jnp.float32))).astype(h.dtype)
    shp = (bsz, S, SB_HEADS, SB_HEAD_DIM)
    o = stick_breaking_attention(q.reshape(shp), k.reshape(shp), v.reshape(shp))
    y_attn = (o.reshape(bsz, S, ATTN_WIDTH).astype(jnp.float32)
              * jax.nn.silu(gate_a.astype(jnp.float32))).astype(h.dtype)
    return jnp.concatenate([y_conv, y_attn], axis=-1) @ w_out


def mamba2_mixer(h, w_in, conv_w, conv_b, dt_bias, a_log, d_skip, norm_w, w_out):
    bsz, S, _ = h.shape
    gn = SSM_GROUPS * SSM_STATE
    proj = h @ w_in
    z, xbc, dt = jnp.split(proj, [D_INNER, D_INNER + XBC_WIDTH], axis=-1)
    xbc = jax.nn.silu(causal_dwconv(xbc, conv_w, conv_b))
    xs, bm, cm = jnp.split(xbc, [D_INNER, D_INNER + gn], axis=-1)
    xs = xs.astype(jnp.float32).reshape(bsz, S, SSM_HEADS, SSM_HEAD_DIM)
    bm = bm.astype(jnp.float32).reshape(bsz, S, SSM_GROUPS, SSM_STATE)
    cm = cm.astype(jnp.float32).reshape(bsz, S, SSM_GROUPS, SSM_STATE)
    dt = jax.nn.softplus(dt.astype(jnp.float32) + dt_bias.astype(jnp.float32))
    a = -jnp.exp(a_log.astype(jnp.float32))
    y = ssd_scan(xs, dt, a, bm, cm)
    y = y + d_skip.astype(jnp.float32)[:, None] * xs
    y = y.reshape(bsz, S, D_INNER) * jax.nn.silu(z.astype(jnp.float32))
    yg = y.reshape(bsz, S, SSM_GROUPS, D_INNER // SSM_GROUPS)
    yg = yg * lax.rsqrt(jnp.mean(yg * yg, axis=-1, keepdims=True) + EPS)
    y = yg.reshape(bsz, S, D_INNER) * norm_w.astype(jnp.float32)
    return y.astype(h.dtype) @ w_out


def _fwd_setup_inputs(seed: int = 0) -> dict:
    key = jax.random.key(seed)
    ks = jax.random.split(key, 20)
    f32 = jnp.float32
    nrm = lambda k, shp, s: jax.random.normal(k, shp, f32) * s
    x = jax.random.normal(ks[0], (BATCH, SEQ, D_MODEL), f32)
    ev_norm_w = 1.0 + nrm(ks[1], (N_EVEN, D_MODEL), 0.02)
    ev_w_in = nrm(ks[2], (N_EVEN, D_MODEL, IN_EVEN), D_MODEL ** -0.5)
    ev_dw_w = nrm(ks[3], (N_EVEN, CONF_KERNEL, CONV_WIDTH), CONF_KERNEL ** -0.5)
    ev_dw_b = nrm(ks[4], (N_EVEN, CONV_WIDTH), 0.02)
    ev_ln_w = 1.0 + nrm(ks[5], (N_EVEN, CONV_WIDTH), 0.02)
    ev_ln_b = nrm(ks[6], (N_EVEN, CONV_WIDTH), 0.02)
    ev_w_out = nrm(ks[7], (N_EVEN, D_INNER, D_MODEL), D_INNER ** -0.5)
    od_norm_w = 1.0 + nrm(ks[8], (N_ODD, D_MODEL), 0.02)
    od_w_in = nrm(ks[9], (N_ODD, D_MODEL, IN_ODD), D_MODEL ** -0.5)
    od_conv_w = nrm(ks[10], (N_ODD, SSM_CONV, XBC_WIDTH), SSM_CONV ** -0.5)
    od_conv_b = nrm(ks[11], (N_ODD, XBC_WIDTH), 0.02)
    u = jax.random.uniform(ks[12], (N_ODD, SSM_HEADS), f32)
    dt0 = jnp.exp(u * (np.log(DT_MAX) - np.log(DT_MIN)) + np.log(DT_MIN))
    od_dt_bias = dt0 + jnp.log(-jnp.expm1(-dt0))
    od_a_log = jnp.log(jax.random.uniform(ks[13], (N_ODD, SSM_HEADS), f32, 1.0, 16.0))
    od_d = 1.0 + nrm(ks[14], (N_ODD, SSM_HEADS), 0.1)
    od_gnorm_w = 1.0 + nrm(ks[15], (N_ODD, D_INNER), 0.02)
    od_w_out = nrm(ks[16], (N_ODD, D_INNER, D_MODEL), D_INNER ** -0.5)
    final_norm_w = 1.0 + nrm(ks[17], (D_MODEL,), 0.02)
    return {"x": x, "ev_norm_w": ev_norm_w, "ev_w_in": ev_w_in, "ev_dw_w": ev_dw_w,
            "ev_dw_b": ev_dw_b, "ev_ln_w": ev_ln_w, "ev_ln_b": ev_ln_b, "ev_w_out": ev_w_out,
            "od_norm_w": od_norm_w, "od_w_in": od_w_in, "od_conv_w": od_conv_w,
            "od_conv_b": od_conv_b, "od_dt_bias": od_dt_bias, "od_a_log": od_a_log,
            "od_d": od_d, "od_gnorm_w": od_gnorm_w, "od_w_out": od_w_out,
            "final_norm_w": final_norm_w}


def _fwd_reference(x, ev_norm_w, ev_w_in, ev_dw_w, ev_dw_b, ev_ln_w, ev_ln_b, ev_w_out,
              od_norm_w, od_w_in, od_conv_w, od_conv_b, od_dt_bias, od_a_log, od_d,
              od_gnorm_w, od_w_out, final_norm_w):
    h = x
    for layer in range(DEPTH):
        i = layer // 2
        if layer % 2 == 0:
            h = h + conv_attn_mixer(rmsnorm(h, ev_norm_w[i]), ev_w_in[i], ev_dw_w[i],
                                    ev_dw_b[i], ev_ln_w[i], ev_ln_b[i], ev_w_out[i])
        else:
            h = h + mamba2_mixer(rmsnorm(h, od_norm_w[i]), od_w_in[i], od_conv_w[i],
                                 od_conv_b[i], od_dt_bias[i], od_a_log[i], od_d[i],
                                 od_gnorm_w[i], od_w_out[i])
    return rmsnorm(h, final_norm_w)


import jax as _jax
import jax.numpy as _jnp

TWIN_FORMAT = 'train_step'
FWD_PARAMS = ['x', 'ev_norm_w', 'ev_w_in', 'ev_dw_w', 'ev_dw_b', 'ev_ln_w', 'ev_ln_b', 'ev_w_out', 'od_norm_w', 'od_w_in', 'od_conv_w', 'od_conv_b', 'od_dt_bias', 'od_a_log', 'od_d', 'od_gnorm_w', 'od_w_out', 'final_norm_w']
TWIN_WEIGHTS = ['ev_norm_w', 'ev_w_in', 'ev_dw_w', 'ev_dw_b', 'ev_ln_w', 'ev_ln_b', 'ev_w_out', 'od_norm_w', 'od_w_in', 'od_conv_w', 'od_conv_b', 'od_dt_bias', 'od_a_log', 'od_d', 'od_gnorm_w', 'od_w_out', 'final_norm_w']
TWIN_DIFF_INPUT = 'x'
TWIN_INPUTS = ['x', 'ev_norm_w', 'ev_w_in', 'ev_dw_w', 'ev_dw_b', 'ev_ln_w', 'ev_ln_b', 'ev_w_out', 'od_norm_w', 'od_w_in', 'od_conv_w', 'od_conv_b', 'od_dt_bias', 'od_a_log', 'od_d', 'od_gnorm_w', 'od_w_out', 'final_norm_w', 'loss_target', 'm_ev_norm_w', 'm_ev_w_in', 'm_ev_dw_w', 'm_ev_dw_b', 'm_ev_ln_w', 'm_ev_ln_b', 'm_ev_w_out', 'm_od_norm_w', 'm_od_w_in', 'm_od_conv_w', 'm_od_conv_b', 'm_od_dt_bias', 'm_od_a_log', 'm_od_d', 'm_od_gnorm_w', 'm_od_w_out', 'm_final_norm_w', 'v_ev_norm_w', 'v_ev_w_in', 'v_ev_dw_w', 'v_ev_dw_b', 'v_ev_ln_w', 'v_ev_ln_b', 'v_ev_w_out', 'v_od_norm_w', 'v_od_w_in', 'v_od_conv_w', 'v_od_conv_b', 'v_od_dt_bias', 'v_od_a_log', 'v_od_d', 'v_od_gnorm_w', 'v_od_w_out', 'v_final_norm_w']
TWIN_OUTPUTS = ['loss', 'grad_x', 'grad_ev_norm_w', 'grad_ev_w_in', 'grad_ev_dw_w', 'grad_ev_dw_b', 'grad_ev_ln_w', 'grad_ev_ln_b', 'grad_ev_w_out', 'grad_od_norm_w', 'grad_od_w_in', 'grad_od_conv_w', 'grad_od_conv_b', 'grad_od_dt_bias', 'grad_od_a_log', 'grad_od_d', 'grad_od_gnorm_w', 'grad_od_w_out', 'grad_final_norm_w', 'delta_ev_norm_w', 'delta_ev_w_in', 'delta_ev_dw_w', 'delta_ev_dw_b', 'delta_ev_ln_w', 'delta_ev_ln_b', 'delta_ev_w_out', 'delta_od_norm_w', 'delta_od_w_in', 'delta_od_conv_w', 'delta_od_conv_b', 'delta_od_dt_bias', 'delta_od_a_log', 'delta_od_d', 'delta_od_gnorm_w', 'delta_od_w_out', 'delta_final_norm_w', 'new_m_ev_norm_w', 'new_m_ev_w_in', 'new_m_ev_dw_w', 'new_m_ev_dw_b', 'new_m_ev_ln_w', 'new_m_ev_ln_b', 'new_m_ev_w_out', 'new_m_od_norm_w', 'new_m_od_w_in', 'new_m_od_conv_w', 'new_m_od_conv_b', 'new_m_od_dt_bias', 'new_m_od_a_log', 'new_m_od_d', 'new_m_od_gnorm_w', 'new_m_od_w_out', 'new_m_final_norm_w', 'new_v_ev_norm_w', 'new_v_ev_w_in', 'new_v_ev_dw_w', 'new_v_ev_dw_b', 'new_v_ev_ln_w', 'new_v_ev_ln_b', 'new_v_ev_w_out', 'new_v_od_norm_w', 'new_v_od_w_in', 'new_v_od_conv_w', 'new_v_od_conv_b', 'new_v_od_dt_bias', 'new_v_od_a_log', 'new_v_od_d', 'new_v_od_gnorm_w', 'new_v_od_w_out', 'new_v_final_norm_w']
TWIN_LEAF_KINDS = {'loss': 'loss', 'grad_x': 'grad_x', 'grad_ev_norm_w': 'grad_w', 'grad_ev_w_in': 'grad_w', 'grad_ev_dw_w': 'grad_w', 'grad_ev_dw_b': 'grad_w', 'grad_ev_ln_w': 'grad_w', 'grad_ev_ln_b': 'grad_w', 'grad_ev_w_out': 'grad_w', 'grad_od_norm_w': 'grad_w', 'grad_od_w_in': 'grad_w', 'grad_od_conv_w': 'grad_w', 'grad_od_conv_b': 'grad_w', 'grad_od_dt_bias': 'grad_w', 'grad_od_a_log': 'grad_w', 'grad_od_d': 'grad_w', 'grad_od_gnorm_w': 'grad_w', 'grad_od_w_out': 'grad_w', 'grad_final_norm_w': 'grad_w', 'delta_ev_norm_w': 'delta_w', 'delta_ev_w_in': 'delta_w', 'delta_ev_dw_w': 'delta_w', 'delta_ev_dw_b': 'delta_w', 'delta_ev_ln_w': 'delta_w', 'delta_ev_ln_b': 'delta_w', 'delta_ev_w_out': 'delta_w', 'delta_od_norm_w': 'delta_w', 'delta_od_w_in': 'delta_w', 'delta_od_conv_w': 'delta_w', 'delta_od_conv_b': 'delta_w', 'delta_od_dt_bias': 'delta_w', 'delta_od_a_log': 'delta_w', 'delta_od_d': 'delta_w', 'delta_od_gnorm_w': 'delta_w', 'delta_od_w_out': 'delta_w', 'delta_final_norm_w': 'delta_w', 'new_m_ev_norm_w': 'new_m', 'new_m_ev_w_in': 'new_m', 'new_m_ev_dw_w': 'new_m', 'new_m_ev_dw_b': 'new_m', 'new_m_ev_ln_w': 'new_m', 'new_m_ev_ln_b': 'new_m', 'new_m_ev_w_out': 'new_m', 'new_m_od_norm_w': 'new_m', 'new_m_od_w_in': 'new_m', 'new_m_od_conv_w': 'new_m', 'new_m_od_conv_b': 'new_m', 'new_m_od_dt_bias': 'new_m', 'new_m_od_a_log': 'new_m', 'new_m_od_d': 'new_m', 'new_m_od_gnorm_w': 'new_m', 'new_m_od_w_out': 'new_m', 'new_m_final_norm_w': 'new_m', 'new_v_ev_norm_w': 'new_v', 'new_v_ev_w_in': 'new_v', 'new_v_ev_dw_w': 'new_v', 'new_v_ev_dw_b': 'new_v', 'new_v_ev_ln_w': 'new_v', 'new_v_ev_ln_b': 'new_v', 'new_v_ev_w_out': 'new_v', 'new_v_od_norm_w': 'new_v', 'new_v_od_w_in': 'new_v', 'new_v_od_conv_w': 'new_v', 'new_v_od_conv_b': 'new_v', 'new_v_od_dt_bias': 'new_v', 'new_v_od_a_log': 'new_v', 'new_v_od_d': 'new_v', 'new_v_od_gnorm_w': 'new_v', 'new_v_od_w_out': 'new_v', 'new_v_final_norm_w': 'new_v'}


def _forward(args):
    return _fwd_reference(*[args[k] for k in FWD_PARAMS])


def _output_shape():
    out = _jax.eval_shape(lambda: _forward(_fwd_setup_inputs(0)))
    return out.shape, out.dtype

N_MICROBATCH = 1
ADAM_LR = 0.001
ADAM_B1 = 0.9
ADAM_B2 = 0.999
ADAM_EPS = 1e-08
ADAM_WD = 0.01
ADAM_STEP = 10
PER_EXAMPLE_BATCH_AXIS = {'x': 0, 'loss_target': 0}
SHARED_INPUTS = []
_WEIGHT_DTYPES = {'ev_norm_w': _jnp.float32, 'ev_w_in': _jnp.float32, 'ev_dw_w': _jnp.float32, 'ev_dw_b': _jnp.float32, 'ev_ln_w': _jnp.float32, 'ev_ln_b': _jnp.float32, 'ev_w_out': _jnp.float32, 'od_norm_w': _jnp.float32, 'od_w_in': _jnp.float32, 'od_conv_w': _jnp.float32, 'od_conv_b': _jnp.float32, 'od_dt_bias': _jnp.float32, 'od_a_log': _jnp.float32, 'od_d': _jnp.float32, 'od_gnorm_w': _jnp.float32, 'od_w_out': _jnp.float32, 'final_norm_w': _jnp.float32}
MOMENT_SCALE = {'ev_norm_w': 1.771002e-01, 'ev_w_in': 6.418269e-02, 'ev_dw_w': 7.767465e-02, 'ev_dw_b': 1.655457e-01, 'ev_ln_w': 9.691046e-02, 'ev_ln_b': 8.169254e-02, 'ev_w_out': 1.098150e-01, 'od_norm_w': 2.575162e-01, 'od_w_in': 1.147242e-01, 'od_conv_w': 1.055670e-01, 'od_conv_b': 1.397038e-01, 'od_dt_bias': 2.623685e-01, 'od_a_log': 7.575561e-01, 'od_d': 8.126851e-01, 'od_gnorm_w': 1.254135e-01, 'od_w_out': 1.720091e-01, 'final_norm_w': 6.395402e+01}


def _to_microbatches(a, axis):
    t = _jnp.moveaxis(a, axis, 0)
    t = t.reshape((N_MICROBATCH, t.shape[0] // N_MICROBATCH) + t.shape[1:])
    return _jnp.moveaxis(t, 1, axis + 1)


def setup_inputs(seed: int = 0) -> dict:
    inp = _fwd_setup_inputs(seed)
    key = _jax.random.fold_in(_jax.random.key(seed), 7919)
    shape, _ = _output_shape()
    out = dict(inp)
    out["loss_target"] = _jax.random.normal(_jax.random.fold_in(key, 0), shape, _jnp.float32)
    for i, name in enumerate(TWIN_WEIGHTS):
        w = inp[name].astype(_jnp.float32)
        if MOMENT_SCALE is None:
            s = _jnp.sqrt(_jnp.mean(_jnp.square(w)) + 1e-30)
        else:
            s = MOMENT_SCALE[name]
        km, kv = _jax.random.split(_jax.random.fold_in(key, i + 1))
        out[name] = w
        out["m_" + name] = s * _jax.random.normal(km, w.shape, _jnp.float32)
        out["v_" + name] = (s * s) * _jax.random.uniform(kv, w.shape, _jnp.float32, 0.5, 1.5)
    if N_MICROBATCH > 1:
        for name, axis in PER_EXAMPLE_BATCH_AXIS.items():
            out[name] = _to_microbatches(out[name], axis)
    return {'x': out['x'], 'ev_norm_w': out['ev_norm_w'], 'ev_w_in': out['ev_w_in'], 'ev_dw_w': out['ev_dw_w'], 'ev_dw_b': out['ev_dw_b'], 'ev_ln_w': out['ev_ln_w'], 'ev_ln_b': out['ev_ln_b'], 'ev_w_out': out['ev_w_out'], 'od_norm_w': out['od_norm_w'], 'od_w_in': out['od_w_in'], 'od_conv_w': out['od_conv_w'], 'od_conv_b': out['od_conv_b'], 'od_dt_bias': out['od_dt_bias'], 'od_a_log': out['od_a_log'], 'od_d': out['od_d'], 'od_gnorm_w': out['od_gnorm_w'], 'od_w_out': out['od_w_out'], 'final_norm_w': out['final_norm_w'], 'loss_target': out['loss_target'], 'm_ev_norm_w': out['m_ev_norm_w'], 'm_ev_w_in': out['m_ev_w_in'], 'm_ev_dw_w': out['m_ev_dw_w'], 'm_ev_dw_b': out['m_ev_dw_b'], 'm_ev_ln_w': out['m_ev_ln_w'], 'm_ev_ln_b': out['m_ev_ln_b'], 'm_ev_w_out': out['m_ev_w_out'], 'm_od_norm_w': out['m_od_norm_w'], 'm_od_w_in': out['m_od_w_in'], 'm_od_conv_w': out['m_od_conv_w'], 'm_od_conv_b': out['m_od_conv_b'], 'm_od_dt_bias': out['m_od_dt_bias'], 'm_od_a_log': out['m_od_a_log'], 'm_od_d': out['m_od_d'], 'm_od_gnorm_w': out['m_od_gnorm_w'], 'm_od_w_out': out['m_od_w_out'], 'm_final_norm_w': out['m_final_norm_w'], 'v_ev_norm_w': out['v_ev_norm_w'], 'v_ev_w_in': out['v_ev_w_in'], 'v_ev_dw_w': out['v_ev_dw_w'], 'v_ev_dw_b': out['v_ev_dw_b'], 'v_ev_ln_w': out['v_ev_ln_w'], 'v_ev_ln_b': out['v_ev_ln_b'], 'v_ev_w_out': out['v_ev_w_out'], 'v_od_norm_w': out['v_od_norm_w'], 'v_od_w_in': out['v_od_w_in'], 'v_od_conv_w': out['v_od_conv_w'], 'v_od_conv_b': out['v_od_conv_b'], 'v_od_dt_bias': out['v_od_dt_bias'], 'v_od_a_log': out['v_od_a_log'], 'v_od_d': out['v_od_d'], 'v_od_gnorm_w': out['v_od_gnorm_w'], 'v_od_w_out': out['v_od_w_out'], 'v_final_norm_w': out['v_final_norm_w']}


def _loss(weights, diff, rest, loss_target):
    with _jax.named_scope("forward"):
        args = {**rest, TWIN_DIFF_INPUT: diff, **{k: w.astype(_WEIGHT_DTYPES[k]) for k, w in weights.items()}}
        y = _forward(args)
    with _jax.named_scope("loss_head"):
        err = _jnp.square(y.astype(_jnp.float32) - loss_target)
        return 0.5 * _jnp.sum(_jnp.mean(err, axis=-1)) if err.ndim else 0.5 * err


def _adamw(w, g, m, v):
    m = ADAM_B1 * m + (1.0 - ADAM_B1) * g
    v = ADAM_B2 * v + (1.0 - ADAM_B2) * _jnp.square(g)
    m_hat = m / (1.0 - ADAM_B1 ** ADAM_STEP)
    v_hat = v / (1.0 - ADAM_B2 ** ADAM_STEP)
    delta = -ADAM_LR * (m_hat / (_jnp.sqrt(v_hat) + ADAM_EPS) + ADAM_WD * w)
    return delta, m, v


def reference(x, ev_norm_w, ev_w_in, ev_dw_w, ev_dw_b, ev_ln_w, ev_ln_b, ev_w_out, od_norm_w, od_w_in, od_conv_w, od_conv_b, od_dt_bias, od_a_log, od_d, od_gnorm_w, od_w_out, final_norm_w, loss_target, m_ev_norm_w, m_ev_w_in, m_ev_dw_w, m_ev_dw_b, m_ev_ln_w, m_ev_ln_b, m_ev_w_out, m_od_norm_w, m_od_w_in, m_od_conv_w, m_od_conv_b, m_od_dt_bias, m_od_a_log, m_od_d, m_od_gnorm_w, m_od_w_out, m_final_norm_w, v_ev_norm_w, v_ev_w_in, v_ev_dw_w, v_ev_dw_b, v_ev_ln_w, v_ev_ln_b, v_ev_w_out, v_od_norm_w, v_od_w_in, v_od_conv_w, v_od_conv_b, v_od_dt_bias, v_od_a_log, v_od_d, v_od_gnorm_w, v_od_w_out, v_final_norm_w):
    given = dict(x=x, ev_norm_w=ev_norm_w, ev_w_in=ev_w_in, ev_dw_w=ev_dw_w, ev_dw_b=ev_dw_b, ev_ln_w=ev_ln_w, ev_ln_b=ev_ln_b, ev_w_out=ev_w_out, od_norm_w=od_norm_w, od_w_in=od_w_in, od_conv_w=od_conv_w, od_conv_b=od_conv_b, od_dt_bias=od_dt_bias, od_a_log=od_a_log, od_d=od_d, od_gnorm_w=od_gnorm_w, od_w_out=od_w_out, final_norm_w=final_norm_w, loss_target=loss_target, m_ev_norm_w=m_ev_norm_w, m_ev_w_in=m_ev_w_in, m_ev_dw_w=m_ev_dw_w, m_ev_dw_b=m_ev_dw_b, m_ev_ln_w=m_ev_ln_w, m_ev_ln_b=m_ev_ln_b, m_ev_w_out=m_ev_w_out, m_od_norm_w=m_od_norm_w, m_od_w_in=m_od_w_in, m_od_conv_w=m_od_conv_w, m_od_conv_b=m_od_conv_b, m_od_dt_bias=m_od_dt_bias, m_od_a_log=m_od_a_log, m_od_d=m_od_d, m_od_gnorm_w=m_od_gnorm_w, m_od_w_out=m_od_w_out, m_final_norm_w=m_final_norm_w, v_ev_norm_w=v_ev_norm_w, v_ev_w_in=v_ev_w_in, v_ev_dw_w=v_ev_dw_w, v_ev_dw_b=v_ev_dw_b, v_ev_ln_w=v_ev_ln_w, v_ev_ln_b=v_ev_ln_b, v_ev_w_out=v_ev_w_out, v_od_norm_w=v_od_norm_w, v_od_w_in=v_od_w_in, v_od_conv_w=v_od_conv_w, v_od_conv_b=v_od_conv_b, v_od_dt_bias=v_od_dt_bias, v_od_a_log=v_od_a_log, v_od_d=v_od_d, v_od_gnorm_w=v_od_gnorm_w, v_od_w_out=v_od_w_out, v_final_norm_w=v_final_norm_w)
    weights = {n: given[n] for n in TWIN_WEIGHTS}
    shared = {n: given[n] for n in SHARED_INPUTS}
    per_example = {n: given[n] for n in ['x']}
    grad_fn = _jax.value_and_grad(_loss, argnums=(0, 1))

    def one_microbatch(ex, loss_target):
        ex = dict(ex)
        diff = ex.pop(TWIN_DIFF_INPUT)
        return grad_fn(weights, diff, {**shared, **ex}, loss_target)

    if N_MICROBATCH == 1:
        loss, (grad_w, grad_x) = one_microbatch(per_example, given["loss_target"])
    else:
        def body(carry, xs):
            loss_sum, grad_sum = carry
            l_k, (gw_k, gx_k) = one_microbatch(xs[0], xs[1])
            with _jax.named_scope("update"):
                return (loss_sum + l_k, _jax.tree.map(_jnp.add, grad_sum, gw_k)), gx_k

        init = (_jnp.zeros((), _jnp.float32), _jax.tree.map(_jnp.zeros_like, weights))
        (loss, grad_w), grad_x = _jax.lax.scan(body, init, (per_example, given["loss_target"]))
    with _jax.named_scope("update"):
        delta_w, new_m, new_v = {}, {}, {}
        for n in TWIN_WEIGHTS:
            delta_w[n], new_m[n], new_v[n] = _adamw(weights[n], grad_w[n], given["m_" + n], given["v_" + n])
    return (loss, grad_x, *[grad_w[n] for n in TWIN_WEIGHTS], *[delta_w[n] for n in TWIN_WEIGHTS],
            *[new_m[n] for n in TWIN_WEIGHTS], *[new_v[n] for n in TWIN_WEIGHTS])
```

```python
import jax
import jax.numpy as jnp
from jax import lax
from jax.experimental import pallas as pl
from jax.experimental.pallas import tpu as pltpu

F32 = jnp.float32
BF16 = jnp.bfloat16

D_MODEL = 1024
CONV_W = 1024
ATT_W = 1024
HEAD_DIM = 128
N_HEADS = 8
CONF_K = 31
IN_EVEN = 7168
D_INNER = 2048
SSM_P = 64
SSM_H = 32
SSM_G = 4
SSM_R = SSM_H // SSM_G
SSM_N = 128
SSM_K = 4
CHUNK = 128
XBC = D_INNER + 2 * SSM_G * SSM_N
IN_ODD = D_INNER + XBC + SSM_H
IN_ODD_PAD = 5376
EPS = 1e-6
QB = 128
NEG_CUT = -100.0

ADAM_LR = 0.001
ADAM_B1 = 0.9
ADAM_B2 = 0.999
ADAM_EPS = 1e-08
ADAM_WD = 0.01
ADAM_STEP = 10

LANE = 128
VMEM_LIMIT = 56 * 1024 * 1024
MESH = pl.DeviceIdType.MESH

NN = (((1,), (0,)), ((), ()))
NT = (((1,), (1,)), ((), ()))
TN = (((0,), (0,)), ((), ()))


def _pallas(body, **kw):
    return pl.pallas_call(body, **kw)


def _params(n_axes):
    return pltpu.CompilerParams(dimension_semantics=("arbitrary",) * n_axes, vmem_limit_bytes=VMEM_LIMIT)


def _dot(a, b, dims=NN):
    return lax.dot_general(a.astype(BF16), b.astype(BF16), dims, preferred_element_type=F32)


def _parts(x):
    h = x.astype(BF16)
    r = x - h.astype(F32)
    m = r.astype(BF16)
    l = (r - m.astype(F32)).astype(BF16)
    return (h, m, l)


def _dotx(x, e01, dims=NN):
    acc = None
    for p in _parts(x):
        t = lax.dot_general(p, e01, dims, preferred_element_type=F32)
        acc = t if acc is None else acc + t
    return acc


def _xdot(e01, x, dims=NN):
    acc = None
    for p in _parts(x):
        t = lax.dot_general(e01, p, dims, preferred_element_type=F32)
        acc = t if acc is None else acc + t
    return acc


def _sigmoid(x):
    return 1.0 / (1.0 + jnp.exp(-x))


def _dsilu(x, s):
    return s * (1.0 + x * (1.0 - s))


def _matmul(a, b, *, mode, out_dtype, bm, bn, bk, name, residual=None):
    if mode == "nn":
        (m, k), n = a.shape, b.shape[1]
        a_spec = pl.BlockSpec((bm, bk), lambda i, j, kk: (i, kk))
        b_spec = pl.BlockSpec((bk, bn), lambda i, j, kk: (kk, j))
        dims = NN
    elif mode == "nt":
        (m, k), n = a.shape, b.shape[0]
        a_spec = pl.BlockSpec((bm, bk), lambda i, j, kk: (i, kk))
        b_spec = pl.BlockSpec((bn, bk), lambda i, j, kk: (j, kk))
        dims = NT
    else:
        (k, m), n = a.shape, b.shape[1]
        a_spec = pl.BlockSpec((bk, bm), lambda i, j, kk: (kk, i))
        b_spec = pl.BlockSpec((bk, bn), lambda i, j, kk: (kk, j))
        dims = TN
    assert m % bm == 0 and n % bn == 0 and k % bk == 0, (name, m, n, k)
    nk = k // bk
    has_res = residual is not None

    def body(*refs):
        if has_res:
            a_ref, b_ref, r_ref, o_ref, acc_ref = refs
        else:
            a_ref, b_ref, o_ref, acc_ref = refs
        kk = pl.program_id(2)

        @pl.when(kk == 0)
        def _():
            acc_ref[...] = jnp.zeros_like(acc_ref)

        acc_ref[...] += _dot(a_ref[...], b_ref[...], dims)

        @pl.when(kk == nk - 1)
        def _():
            r = acc_ref[...]
            if has_res:
                r = r + r_ref[...]
            o_ref[...] = r.astype(out_dtype)

    in_specs = [a_spec, b_spec]
    args = [a, b]
    if has_res:
        in_specs.append(pl.BlockSpec((bm, bn), lambda i, j, kk: (i, j)))
        args.append(residual)
    return _pallas(
        body, name=name, grid=(m // bm, n // bn, nk), in_specs=in_specs,
        out_specs=pl.BlockSpec((bm, bn), lambda i, j, kk: (i, j)),
        out_shape=jax.ShapeDtypeStruct((m, n), out_dtype),
        scratch_shapes=[pltpu.VMEM((bm, bn), F32)], compiler_params=_params(3),
    )(*args)


def _rms_fwd(x, w, *, name, tm=512):
    t, d = x.shape

    def body(x_ref, w_ref, o_ref):
        xv = x_ref[...]
        r = lax.rsqrt(jnp.mean(xv * xv, axis=1, keepdims=True) + EPS)
        o_ref[...] = (xv * r * w_ref[...]).astype(BF16)

    return _pallas(
        body, name=name, grid=(t // tm,),
        in_specs=[pl.BlockSpec((tm, d), lambda i: (i, 0)), pl.BlockSpec((1, d), lambda i: (0, 0))],
        out_specs=pl.BlockSpec((tm, d), lambda i: (i, 0)),
        out_shape=jax.ShapeDtypeStruct((t, d), BF16), compiler_params=_params(1),
    )(x, w)


def _rms_bwd(dn, x, w, dres, *, name, tm=512):
    t, d = x.shape

    def body(dn_ref, x_ref, w_ref, dr_ref, dx_ref, dw_ref):
        i = pl.program_id(0)
        xv = x_ref[...]
        r = lax.rsqrt(jnp.mean(xv * xv, axis=1, keepdims=True) + EPS)
        xh = xv * r
        dy = dn_ref[...].astype(F32)
        g = dy * w_ref[...]
        dx_ref[...] = dr_ref[...] + r * (g - xh * jnp.mean(g * xh, axis=1, keepdims=True))

        @pl.when(i == 0)
        def _():
            dw_ref[...] = jnp.zeros_like(dw_ref)

        dw_ref[...] += jnp.sum(dy * xh, axis=0, keepdims=True)

    row = pl.BlockSpec((tm, d), lambda i: (i, 0))
    vec = pl.BlockSpec((1, d), lambda i: (0, 0))
    return _pallas(
        body, name=name, grid=(t // tm,), in_specs=[row, row, vec, row], out_specs=[row, vec],
        out_shape=[jax.ShapeDtypeStruct((t, d), F32), jax.ShapeDtypeStruct((1, d), F32)],
        compiler_params=_params(1),
    )(dn, x, w, dres)


def _final_loss(h, w, target, *, tm=512):
    t, d = h.shape

    def body(h_ref, w_ref, t_ref, dh_ref, dw_ref, loss_ref):
        i = pl.program_id(0)
        xv = h_ref[...]
        r = lax.rsqrt(jnp.mean(xv * xv, axis=1, keepdims=True) + EPS)
        xh = xv * r
        wv = w_ref[...]
        err = xh * wv - t_ref[...]
        dy = err * (1.0 / d)
        g = dy * wv
        dh_ref[...] = r * (g - xh * jnp.mean(g * xh, axis=1, keepdims=True))

        @pl.when(i == 0)
        def _():
            dw_ref[...] = jnp.zeros_like(dw_ref)
            loss_ref[...] = jnp.zeros_like(loss_ref)

        dw_ref[...] += jnp.sum(dy * xh, axis=0, keepdims=True)
        part = jnp.sum(jnp.sum(err * err, axis=1, keepdims=True), axis=0, keepdims=True)
        loss_ref[...] += part * (0.5 / d)

    row = pl.BlockSpec((tm, d), lambda i: (i, 0))
    vec = pl.BlockSpec((1, d), lambda i: (0, 0))
    return _pallas(
        body, name="final_loss", grid=(t // tm,), in_specs=[row, vec, row],
        out_specs=[row, vec, pl.BlockSpec((1, LANE), lambda i: (0, 0))],
        out_shape=[jax.ShapeDtypeStruct((t, d), F32), jax.ShapeDtypeStruct((1, d), F32),
                   jax.ShapeDtypeStruct((1, LANE), F32)],
        compiler_params=_params(1),
    )(h, w, target)


HALO = 32


def _conf_fwd(proj, dw_w, dw_b, ln_w, ln_b, seq, *, tm=256):
    t = proj.shape[0]
    c = CONV_W
    tps = seq // tm
    hb = tm // HALO

    def body(a_ref, b_ref, g_ref, ha_ref, hb_ref, w_ref, wb_ref, lw_ref, lb_ref, y_ref, ext_ref):
        i = pl.program_id(0)
        keep = jnp.where(i % tps == 0, 0.0, 1.0)
        ext_ref[0:HALO, :] = ha_ref[...] * _sigmoid(hb_ref[...]) * keep
        ext_ref[HALO:HALO + tm, :] = a_ref[...] * _sigmoid(b_ref[...])
        acc = jnp.zeros((tm, c), F32) + wb_ref[...]
        for k in range(CONF_K):
            acc = acc + w_ref[k:k + 1, :] * ext_ref[2 + k:2 + k + tm, :]
        mu = jnp.mean(acc, axis=1, keepdims=True)
        xc = acc - mu
        rs = lax.rsqrt(jnp.mean(xc * xc, axis=1, keepdims=True) + EPS)
        u3 = xc * rs * lw_ref[...] + lb_ref[...]
        gv = g_ref[...]
        y_ref[...] = (u3 * _sigmoid(u3) * gv * _sigmoid(gv)).astype(BF16)

    def col(j):
        return pl.BlockSpec((tm, c), lambda i: (i, j))

    def prev(j):
        return pl.BlockSpec((HALO, c), lambda i: (jnp.maximum(i * hb - 1, 0), j))

    vec = pl.BlockSpec((1, c), lambda i: (0, 0))
    return _pallas(
        body, name="conf_fwd", grid=(t // tm,),
        in_specs=[col(0), col(1), col(2), prev(0), prev(1),
                  pl.BlockSpec((HALO, c), lambda i: (0, 0)), vec, vec, vec],
        out_specs=pl.BlockSpec((tm, c), lambda i: (i, 0)),
        out_shape=jax.ShapeDtypeStruct((t, c), BF16),
        scratch_shapes=[pltpu.VMEM((tm + HALO, c), F32)], compiler_params=_params(1),
    )(proj, proj, proj, proj, proj, dw_w, dw_b, ln_w, ln_b)


def _conf_bwd(proj, dycat, dw_w, dw_b, ln_w, ln_b, seq, *, tm=256):
    t = proj.shape[0]
    c = CONV_W
    tps = seq // tm
    hb = tm // HALO
    nhb = t // HALO
    te = tm + HALO

    def body(a_ref, b_ref, g_ref, pa_ref, pb_ref, na_ref, nb_ref, ng_ref, dy_ref, ndy_ref,
             w_ref, wb_ref, lw_ref, lb_ref,
             dp_ref, dww_ref, dwb_ref, dlw_ref, dlb_ref, ext_ref, du2_ref):
        i = pl.program_id(0)
        first = i % tps == 0
        last = i % tps == tps - 1
        sb = _sigmoid(b_ref[...])
        av = a_ref[...]
        ext_ref[0:HALO, :] = pa_ref[...] * _sigmoid(pb_ref[...]) * jnp.where(first, 0.0, 1.0)
        ext_ref[HALO:HALO + tm, :] = av * sb
        ext_ref[HALO + tm:HALO + te, :] = na_ref[...] * _sigmoid(nb_ref[...])
        acc = jnp.zeros((te, c), F32) + wb_ref[...]
        for k in range(CONF_K):
            acc = acc + w_ref[k:k + 1, :] * ext_ref[2 + k:2 + k + te, :]
        mu = jnp.mean(acc, axis=1, keepdims=True)
        xc = acc - mu
        rs = lax.rsqrt(jnp.mean(xc * xc, axis=1, keepdims=True) + EPS)
        xh = xc * rs
        lw = lw_ref[...]
        u3 = xh * lw + lb_ref[...]
        s3 = _sigmoid(u3)
        u4 = u3 * s3
        gv = jnp.concatenate([g_ref[...], ng_ref[...]], axis=0)
        dy = jnp.concatenate([dy_ref[...], ndy_ref[...]], axis=0)
        sg = _sigmoid(gv)
        dgc = dy * u4 * _dsilu(gv, sg)
        du3 = dy * gv * sg * _dsilu(u3, s3)
        dxh = du3 * lw
        du2 = rs * (dxh - jnp.mean(dxh, axis=1, keepdims=True)
                    - xh * jnp.mean(dxh * xh, axis=1, keepdims=True))
        rows = lax.broadcasted_iota(jnp.int32, (te, 1), 0)
        du2 = jnp.where(jnp.logical_and(last, rows >= tm), 0.0, du2)
        du2_ref[...] = du2
        du1 = jnp.zeros((tm, c), F32)
        for k in range(CONF_K):
            du1 = du1 + w_ref[k:k + 1, :] * du2_ref[CONF_K - 1 - k:CONF_K - 1 - k + tm, :]
        dp_ref[:, 0:c] = (du1 * sb).astype(BF16)
        dp_ref[:, c:2 * c] = (du1 * av * sb * (1.0 - sb)).astype(BF16)
        dp_ref[:, 2 * c:3 * c] = dgc[0:tm, :].astype(BF16)

        @pl.when(i == 0)
        def _():
            dww_ref[...] = jnp.zeros_like(dww_ref)
            dwb_ref[...] = jnp.zeros_like(dwb_ref)
            dlw_ref[...] = jnp.zeros_like(dlw_ref)
            dlb_ref[...] = jnp.zeros_like(dlb_ref)

        du2t = du2_ref[0:tm, :]
        for k in range(CONF_K):
            dww_ref[k:k + 1, :] += jnp.sum(du2t * ext_ref[2 + k:2 + k + tm, :], axis=0, keepdims=True)
        dwb_ref[...] += jnp.sum(du2t, axis=0, keepdims=True)
        dlw_ref[...] += jnp.sum(du3[0:tm, :] * xh[0:tm, :], axis=0, keepdims=True)
        dlb_ref[...] += jnp.sum(du3[0:tm, :], axis=0, keepdims=True)

    def col(j):
        return pl.BlockSpec((tm, c), lambda i: (i, j))

    def prev(j):
        return pl.BlockSpec((HALO, c), lambda i: (jnp.maximum(i * hb - 1, 0), j))

    def nxt(j):
        return pl.BlockSpec((HALO, c), lambda i: (jnp.minimum((i + 1) * hb, nhb - 1), j))

    vec = pl.BlockSpec((1, c), lambda i: (0, 0))
    wsp = pl.BlockSpec((HALO, c), lambda i: (0, 0))
    return _pallas(
        body, name="conf_bwd", grid=(t // tm,),
        in_specs=[col(0), col(1), col(2), prev(0), prev(1), nxt(0), nxt(1), nxt(2), col(0), nxt(0),
                  wsp, vec, vec, vec],
        out_specs=[pl.BlockSpec((tm, 3 * c), lambda i: (i, 0)), wsp, vec, vec, vec],
        out_shape=[jax.ShapeDtypeStruct((t, 3 * c), BF16), jax.ShapeDtypeStruct((HALO, c), F32),
                   jax.ShapeDtypeStruct((1, c), F32), jax.ShapeDtypeStruct((1, c), F32),
                   jax.ShapeDtypeStruct((1, c), F32)],
        scratch_shapes=[pltpu.VMEM((tm + 2 * HALO, c), F32), pltpu.VMEM((te, c), F32)],
        compiler_params=_params(1),
    )(proj, proj, proj, proj, proj, proj, proj, proj, dycat, dycat, dw_w, dw_b, ln_w, ln_b)


Q_COL = 3 * CONV_W // HEAD_DIM
K_COL = Q_COL + N_HEADS
V_COL = K_COL + N_HEADS
GA_COL = V_COL + N_HEADS


def _sb_block(qs, kj, diag, causal, ustrict, carry):
    z = _dot(qs, kj, NT)
    sp = jnp.log(1.0 + jnp.exp(-jnp.abs(z)))
    ls = jnp.minimum(z, 0.0) - sp
    lk = ls - z
    if diag:
        lk = jnp.where(causal, lk, 0.0)
    later = carry + _dotx(lk, ustrict)
    w = jnp.exp(ls + later)
    if diag:
        w = jnp.where(causal, w, 0.0)
    return ls, lk, w


def _sba_fwd(proj, nb, seq):
    t = proj.shape[0]
    nq = seq // QB
    scale = HEAD_DIM ** -0.5

    def body(q_ref, k_ref, v_ref, g_ref, o_ref, y_ref):
        i = pl.program_id(2)
        qs = (q_ref[...] * scale).astype(BF16)
        tt = lax.broadcasted_iota(jnp.int32, (QB, QB), 0)
        ss = lax.broadcasted_iota(jnp.int32, (QB, QB), 1)
        causal = ss < tt
        ustrict = jnp.where(tt > ss, 1.0, 0.0).astype(BF16)

        def blk(j, carry, acc, diag):
            start = pl.multiple_of(j * QB, QB)
            kj = k_ref[pl.ds(start, QB), :]
            vj = v_ref[pl.ds(start, QB), :]
            _, lk, w = _sb_block(qs, kj, diag, causal, ustrict, carry)
            return carry + jnp.sum(lk, axis=1, keepdims=True), acc + _dot(w, vj)

        carry, acc = blk(i, jnp.zeros((QB, 1), F32), jnp.zeros((QB, HEAD_DIM), F32), True)

        def cond(st):
            return jnp.logical_and(st[0] >= 0, jnp.max(st[1]) > NEG_CUT)

        def step(st):
            c2, a2 = blk(st[0], st[1], st[2], False)
            return st[0] - 1, c2, a2

        _, _, acc = lax.while_loop(cond, step, (i - 1, carry, acc))
        o_ref[...] = acc
        gv = g_ref[...]
        y_ref[...] = (acc * gv * _sigmoid(gv)).astype(BF16)

    def tile(c0):
        return pl.BlockSpec((QB, HEAD_DIM), lambda b, h, i: (b * nq + i, c0 + h))

    def whole(c0):
        return pl.BlockSpec((seq, HEAD_DIM), lambda b, h, i: (b, c0 + h))

    return _pallas(
        body, name="sba_fwd", grid=(nb, N_HEADS, nq),
        in_specs=[tile(Q_COL), whole(K_COL), whole(V_COL), tile(GA_COL)],
        out_specs=[tile(0), tile(0)],
        out_shape=[jax.ShapeDtypeStruct((t, ATT_W), F32), jax.ShapeDtypeStruct((t, ATT_W), BF16)],
        compiler_params=_params(3),
    )(proj, proj, proj, proj)


def _sba_bwd(proj, o, dycat, nb, seq):
    t = proj.shape[0]
    nq = seq // QB
    scale = HEAD_DIM ** -0.5

    def body(q_ref, k_ref, v_ref, g_ref, o_ref, dy_ref, dq_ref, dk_ref, dv_ref, dg_ref, e_ref, sp_ref):
        i = pl.program_id(2)

        @pl.when(i == 0)
        def _():
            dk_ref[...] = jnp.zeros_like(dk_ref)
            dv_ref[...] = jnp.zeros_like(dv_ref)

        qs = (q_ref[...] * scale).astype(BF16)
        gv = g_ref[...]
        sg = _sigmoid(gv)
        ov = o_ref[...]
        dy = dy_ref[...]
        do = (dy * gv * sg).astype(BF16)
        dg_ref[...] = (dy * ov * _dsilu(gv, sg)).astype(BF16)
        tt = lax.broadcasted_iota(jnp.int32, (QB, QB), 0)
        ss = lax.broadcasted_iota(jnp.int32, (QB, QB), 1)
        causal = ss < tt
        ustrict = jnp.where(tt > ss, 1.0, 0.0).astype(BF16)
        lstrict = jnp.where(tt < ss, 1.0, 0.0).astype(BF16)

        def near(j, carry, diag):
            start = pl.multiple_of(j * QB, QB)
            kj = k_ref[pl.ds(start, QB), :]
            vj = v_ref[pl.ds(start, QB), :]
            ls, lk, w = _sb_block(qs, kj, diag, causal, ustrict, carry)
            e_ref[j] = w * _dot(do, vj, NT)
            sp_ref[j] = jnp.exp(ls)
            dv_ref[pl.ds(start, QB), :] += _dot(w, do, TN)
            return carry + jnp.sum(lk, axis=1, keepdims=True)

        carry = near(i, jnp.zeros((QB, 1), F32), True)

        def cond(st):
            return jnp.logical_and(st[0] >= 0, jnp.max(st[1]) > NEG_CUT)

        def step(st):
            return st[0] - 1, near(st[0], st[1], False)

        jlast, _ = lax.while_loop(cond, step, (i - 1, carry))

        def far(j, pre, dq, diag):
            start = pl.multiple_of(j * QB, QB)
            kj = k_ref[pl.ds(start, QB), :]
            e = e_ref[j]
            spj = sp_ref[j]
            gsum = pre + _dotx(e, lstrict)
            dz = e * (1.0 - spj) - gsum * spj
            if diag:
                dz = jnp.where(causal, dz, 0.0)
            dzb = dz.astype(BF16)
            dk_ref[pl.ds(start, QB), :] += _dot(dzb, qs, TN)
            return pre + jnp.sum(e, axis=1, keepdims=True), dq + _dot(dzb, kj)

        def fstep(j, st):
            return far(j, st[0], st[1], False)

        pre, dq = lax.fori_loop(jlast + 1, i, fstep,
                                (jnp.zeros((QB, 1), F32), jnp.zeros((QB, HEAD_DIM), F32)))
        _, dq = far(i, pre, dq, True)
        dq_ref[...] = (dq * scale).astype(BF16)

    def tile(c0):
        return pl.BlockSpec((QB, HEAD_DIM), lambda b, h, i: (b * nq + i, c0 + h))

    def whole(c0):
        return pl.BlockSpec((seq, HEAD_DIM), lambda b, h, i: (b, c0 + h))

    return _pallas(
        body, name="sba_bwd", grid=(nb, N_HEADS, nq),
        in_specs=[tile(Q_COL), whole(K_COL), whole(V_COL), tile(GA_COL), tile(0),
                  tile(CONV_W // HEAD_DIM)],
        out_specs=[tile(0), whole(0), whole(0), tile(0)],
        out_shape=[jax.ShapeDtypeStruct((t, ATT_W), BF16), jax.ShapeDtypeStruct((t, ATT_W), F32),
                   jax.ShapeDtypeStruct((t, ATT_W), F32), jax.ShapeDtypeStruct((t, ATT_W), BF16)],
        scratch_shapes=[pltpu.VMEM((nq, QB, QB), F32), pltpu.VMEM((nq, QB, QB), F32)],
        compiler_params=_params(3),
    )(proj, proj, proj, proj, o, dycat)


CT = 512
PH = 8
Z_BLK = 0
XBC_BLK = D_INNER // CT
DT_BLK = (D_INNER + XBC) // LANE


def _softplus(x):
    return jnp.maximum(x, 0.0) + jnp.log(1.0 + jnp.exp(-jnp.abs(x)))


def _dt_fwd(proj, dt_bias, *, tm=512):
    t = proj.shape[0]

    def body(p_ref, b_ref, o_ref):
        o_ref[...] = _softplus(p_ref[...] + b_ref[...])

    return _pallas(
        body, name="dt_fwd", grid=(t // tm,),
        in_specs=[pl.BlockSpec((tm, LANE), lambda i: (i, DT_BLK)), pl.BlockSpec((1, LANE), lambda i: (0, 0))],
        out_specs=pl.BlockSpec((tm, LANE), lambda i: (i, 0)),
        out_shape=jax.ShapeDtypeStruct((t, LANE), F32), compiler_params=_params(1),
    )(proj, dt_bias)


def _dt_bwd(proj, dt_bias, ddt, *, tm=512):
    t = proj.shape[0]

    def body(p_ref, b_ref, d_ref, o_ref, db_ref):
        i = pl.program_id(0)
        lanes = lax.broadcasted_iota(jnp.int32, (tm, LANE), 1)
        dr = jnp.where(lanes < SSM_H, d_ref[...] * _sigmoid(p_ref[...] + b_ref[...]), 0.0)
        o_ref[...] = dr

        @pl.when(i == 0)
        def _():
            db_ref[...] = jnp.zeros_like(db_ref)

        db_ref[...] += jnp.sum(dr, axis=0, keepdims=True)

    vec = pl.BlockSpec((1, LANE), lambda i: (0, 0))
    row = pl.BlockSpec((tm, LANE), lambda i: (i, 0))
    return _pallas(
        body, name="dt_bwd", grid=(t // tm,),
        in_specs=[pl.BlockSpec((tm, LANE), lambda i: (i, DT_BLK)), vec, row],
        out_specs=[row, vec],
        out_shape=[jax.ShapeDtypeStruct((t, LANE), F32), jax.ShapeDtypeStruct((1, LANE), F32)],
        compiler_params=_params(1),
    )(proj, dt_bias, ddt)


def _xconv_fwd(proj, conv_w, conv_b, seq, *, tm=512):
    t = proj.shape[0]
    tps = seq // tm
    hb = tm // PH

    def body(x_ref, h_ref, w_ref, b_ref, o_ref, ext_ref):
        i = pl.program_id(1)
        ext_ref[0:PH, :] = h_ref[...] * jnp.where(i % tps == 0, 0.0, 1.0)
        ext_ref[PH:PH + tm, :] = x_ref[...]
        acc = jnp.zeros((tm, CT), F32) + b_ref[...]
        for k in range(SSM_K):
            acc = acc + w_ref[k:k + 1, :] * ext_ref[PH - SSM_K + 1 + k:PH - SSM_K + 1 + k + tm, :]
        o_ref[...] = acc * _sigmoid(acc)

    return _pallas(
        body, name="xconv_fwd", grid=(XBC // CT, t // tm),
        in_specs=[pl.BlockSpec((tm, CT), lambda j, i: (i, XBC_BLK + j)),
                  pl.BlockSpec((PH, CT), lambda j, i: (jnp.maximum(i * hb - 1, 0), XBC_BLK + j)),
                  pl.BlockSpec((PH, CT), lambda j, i: (0, j)),
                  pl.BlockSpec((1, CT), lambda j, i: (0, j))],
        out_specs=pl.BlockSpec((tm, CT), lambda j, i: (i, j)),
        out_shape=jax.ShapeDtypeStruct((t, XBC), F32),
        scratch_shapes=[pltpu.VMEM((tm + PH, CT), F32)], compiler_params=_params(2),
    )(proj, proj, conv_w, conv_b)


def _xconv_bwd(proj, dxc, conv_w, conv_b, seq, *, tm=512):
    t = proj.shape[0]
    tps = seq // tm
    hb = tm // PH
    nhb = t // PH
    te = tm + PH

    def body(x_ref, p_ref, n_ref, d_ref, nd_ref, w_ref, b_ref, dx_ref, dw_ref, db_ref, ext_ref, dv_ref):
        i = pl.program_id(1)
        first = i % tps == 0
        last = i % tps == tps - 1
        ext_ref[0:PH, :] = p_ref[...] * jnp.where(first, 0.0, 1.0)
        ext_ref[PH:PH + tm, :] = x_ref[...]
        ext_ref[PH + tm:PH + te, :] = n_ref[...]
        acc = jnp.zeros((te, CT), F32) + b_ref[...]
        for k in range(SSM_K):
            acc = acc + w_ref[k:k + 1, :] * ext_ref[PH - SSM_K + 1 + k:PH - SSM_K + 1 + k + te, :]
        sv = _sigmoid(acc)
        dy = jnp.concatenate([d_ref[...], nd_ref[...]], axis=0)
        dv = dy * _dsilu(acc, sv)
        rows = lax.broadcasted_iota(jnp.int32, (te, 1), 0)
        dv_ref[...] = jnp.where(jnp.logical_and(last, rows >= tm), 0.0, dv)
        dx = jnp.zeros((tm, CT), F32)
        for k in range(SSM_K):
            dx = dx + w_ref[k:k + 1, :] * dv_ref[SSM_K - 1 - k:SSM_K - 1 - k + tm, :]
        dx_ref[...] = dx.astype(BF16)

        @pl.when(i == 0)
        def _():
            dw_ref[...] = jnp.zeros_like(dw_ref)
            db_ref[...] = jnp.zeros_like(db_ref)

        dvt = dv_ref[0:tm, :]
        for k in range(SSM_K):
            dw_ref[k:k + 1, :] += jnp.sum(
                dvt * ext_ref[PH - SSM_K + 1 + k:PH - SSM_K + 1 + k + tm, :], axis=0, keepdims=True)
        db_ref[...] += jnp.sum(dvt, axis=0, keepdims=True)

    return _pallas(
        body, name="xconv_bwd", grid=(XBC // CT, t // tm),
        in_specs=[pl.BlockSpec((tm, CT), lambda j, i: (i, XBC_BLK + j)),
                  pl.BlockSpec((PH, CT), lambda j, i: (jnp.maximum(i * hb - 1, 0), XBC_BLK + j)),
                  pl.BlockSpec((PH, CT), lambda j, i: (jnp.minimum((i + 1) * hb, nhb - 1), XBC_BLK + j)),
                  pl.BlockSpec((tm, CT), lambda j, i: (i, j)),
                  pl.BlockSpec((PH, CT), lambda j, i: (jnp.minimum((i + 1) * hb, nhb - 1), j)),
                  pl.BlockSpec((PH, CT), lambda j, i: (0, j)),
                  pl.BlockSpec((1, CT), lambda j, i: (0, j))],
        out_specs=[pl.BlockSpec((tm, CT), lambda j, i: (i, j)),
                   pl.BlockSpec((PH, CT), lambda j, i: (0, j)),
                   pl.BlockSpec((1, CT), lambda j, i: (0, j))],
        out_shape=[jax.ShapeDtypeStruct((t, XBC), BF16), jax.ShapeDtypeStruct((PH, XBC), F32),
                   jax.ShapeDtypeStruct((1, XBC), F32)],
        scratch_shapes=[pltpu.VMEM((tm + 2 * PH, CT), F32), pltpu.VMEM((te, CT), F32)],
        compiler_params=_params(2),
    )(proj, proj, proj, dxc, dxc, conv_w, conv_b)


def _ssd_common(xbc, dt, alog, ex):
    L = CHUNK
    a = -jnp.exp(alog)
    la = dt * a
    li = lax.broadcasted_iota(jnp.int32, (L, L), 0)
    si = lax.broadcasted_iota(jnp.int32, (L, L), 1)
    lower = si <= li
    tri = jnp.where(lower, 1.0, 0.0).astype(BF16)
    cs = _xdot(tri, la)
    cst = _dotx(la, tri, (((0,), (1,)), ((), ())))
    csl = cs[L - 1:L, :]
    ecs_x = _dotx(jnp.exp(cs)[:, 0:SSM_H], ex)
    tail_x = _dotx(jnp.exp(csl - cs)[:, 0:SSM_H], ex)
    dt_x = _dotx(dt[:, 0:SSM_H], ex)
    return a, la, lower, tri, cs, cst, ecs_x, tail_x, dt_x


def _ssd_fwd(xbc_c, dt, a_log, ex, nb, seq):
    t = xbc_c.shape[0]
    L = CHUNK
    nc = seq // L
    GW = SSM_R * SSM_P

    def body(x_ref, dt_ref, al_ref, ex_ref, y_ref, st_ref, state):
        c = pl.program_id(1)

        @pl.when(c == 0)
        def _():
            state[...] = jnp.zeros_like(state)

        st_ref[0] = state[...]
        xbc = x_ref[...]
        _, _, lower, _, cs, cst, ecs_x, tail_x, dt_x = _ssd_common(xbc, dt_ref[...], al_ref[...], ex_ref[...])
        xd = xbc[:, 0:D_INNER] * dt_x
        xdb = xd.astype(BF16)
        xt = (xd * tail_x).astype(BF16)
        el_x = ecs_x[L - 1:L, :]
        for g in range(SSM_G):
            bg = xbc[:, D_INNER + g * SSM_N:D_INNER + (g + 1) * SSM_N].astype(BF16)
            cg = xbc[:, D_INNER + (SSM_G + g) * SSM_N:D_INNER + (SSM_G + g + 1) * SSM_N].astype(BF16)
            cb = _dot(cg, bg, NT)
            sg = state[:, g * GW:(g + 1) * GW]
            ys = _dot(cg, sg) * ecs_x[:, g * GW:(g + 1) * GW]
            for r in range(SSM_R):
                h = g * SSM_R + r
                seg = cs[:, h:h + 1] - cst[h:h + 1, :]
                dec = jnp.exp(jnp.where(lower, seg, -1e30))
                yh = _dot(cb * dec, xdb[:, h * SSM_P:(h + 1) * SSM_P])
                y_ref[:, h * SSM_P:(h + 1) * SSM_P] = yh + ys[:, r * SSM_P:(r + 1) * SSM_P]
            state[:, g * GW:(g + 1) * GW] = sg * el_x[:, g * GW:(g + 1) * GW] + _dot(bg, xt[:, g * GW:(g + 1) * GW], TN)

    return _pallas(
        body, name="ssd_fwd", grid=(nb, nc),
        in_specs=[pl.BlockSpec((L, XBC), lambda b, c: (b * nc + c, 0)),
                  pl.BlockSpec((L, LANE), lambda b, c: (b * nc + c, 0)),
                  pl.BlockSpec((1, LANE), lambda b, c: (0, 0)),
                  pl.BlockSpec((SSM_H, D_INNER), lambda b, c: (0, 0))],
        out_specs=[pl.BlockSpec((L, D_INNER), lambda b, c: (b * nc + c, 0)),
                   pl.BlockSpec((1, SSM_N, D_INNER), lambda b, c: (b * nc + c, 0, 0))],
        out_shape=[jax.ShapeDtypeStruct((t, D_INNER), F32),
                   jax.ShapeDtypeStruct((nb * nc, SSM_N, D_INNER), F32)],
        scratch_shapes=[pltpu.VMEM((SSM_N, D_INNER), F32)], compiler_params=_params(2),
    )(xbc_c, dt, a_log, ex)


def _ssd_bwd(xbc_c, dt, a_log, ex, ext, states, dy, dskip, nb, seq):
    t = xbc_c.shape[0]
    L = CHUNK
    nc = seq // L
    GW = SSM_R * SSM_P

    def body(x_ref, dt_ref, al_ref, ex_ref, ext_ref, st_ref, dy_ref, sk_ref, dx_ref, ddt_ref, da_ref,
             dstate, dxd, yd, lastv):
        b = pl.program_id(0)
        c = pl.program_id(1)

        @pl.when(c == 0)
        def _():
            dstate[...] = jnp.zeros_like(dstate)

        @pl.when(jnp.logical_and(b == 0, c == 0))
        def _():
            da_ref[...] = jnp.zeros_like(da_ref)

        xbc = x_ref[...]
        dtv = dt_ref[...]
        ex_t = ext_ref[...]
        a, la, lower, tri, cs, cst, ecs_x, tail_x, dt_x = _ssd_common(xbc, dtv, al_ref[...], ex_ref[...])
        xs = xbc[:, 0:D_INNER]
        xd = xs * dt_x
        xdb = xd.astype(BF16)
        dyv = dy_ref[...]
        dyb = dyv.astype(BF16)
        dys = dyv * ecs_x
        xt = xd * tail_x
        el_x = ecs_x[L - 1:L, :]
        lane = lax.broadcasted_iota(jnp.int32, (L, LANE), 1)
        sub = lax.broadcasted_iota(jnp.int32, (LANE, L), 0)
        row_part = jnp.zeros((L, LANE), F32)
        col_part = jnp.zeros((LANE, L), F32)
        for g in range(SSM_G):
            gs = slice(g * GW, (g + 1) * GW)
            bcol = slice(D_INNER + g * SSM_N, D_INNER + (g + 1) * SSM_N)
            ccol = slice(D_INNER + (SSM_G + g) * SSM_N, D_INNER + (SSM_G + g + 1) * SSM_N)
            bg = xbc[:, bcol].astype(BF16)
            cg = xbc[:, ccol].astype(BF16)
            cb = _dot(cg, bg, NT)
            sg = st_ref[0, :, gs]
            dsg = dstate[:, gs]
            dc = _dot(dys[:, gs], sg, NT)
            db = _dot(xt[:, gs], dsg, NT)
            dx_state = tail_x[:, gs] * _dot(bg, dsg)
            yd[:, gs] = dys[:, gs] * _dot(cg, sg) - xd[:, gs] * dx_state
            s_out = sg * el_x[:, gs] + _dot(bg, xt[:, gs], TN)
            lastv[:, gs] = jnp.broadcast_to(jnp.sum(dsg * s_out, axis=0, keepdims=True), (8, GW))
            dcb = jnp.zeros((L, L), F32)
            for r in range(SSM_R):
                h = g * SSM_R + r
                hs = slice(h * SSM_P, (h + 1) * SSM_P)
                seg = cs[:, h:h + 1] - cst[h:h + 1, :]
                dec = jnp.exp(jnp.where(lower, seg, -1e30))
                m = cb * dec
                dm = _dot(dyb[:, hs], xdb[:, hs], NT)
                dcb = dcb + dm * dec
                e = dm * m
                row_part = row_part + jnp.where(lane == h, jnp.sum(e, axis=1, keepdims=True), 0.0)
                col_part = col_part + jnp.where(sub == h, jnp.sum(e, axis=0, keepdims=True), 0.0)
                dxd[:, hs] = _dot(m, dyb[:, hs], TN) + dx_state[:, r * SSM_P:(r + 1) * SSM_P]
            dx_ref[:, bcol] = db + _dot(dcb, cg, TN)
            dx_ref[:, ccol] = dc + _dot(dcb, bg)
            dstate[:, gs] = dsg * el_x[:, gs] + _dot(cg, dys[:, gs], TN)
        dxv = dxd[...]
        dx_ref[:, 0:D_INNER] = dxv * dt_x + sk_ref[...]
        ddt_x = _dotx(dxv * xs, ex_t)
        yst = _dotx(yd[...], ex_t)
        lst = _dotx(lastv[...], ex_t)[0:1, :]
        rows = lax.broadcasted_iota(jnp.int32, (L, LANE), 0)
        dcs = row_part - col_part.T + yst + jnp.where(rows == L - 1, lst, 0.0)
        li = lax.broadcasted_iota(jnp.int32, (L, L), 0)
        si = lax.broadcasted_iota(jnp.int32, (L, L), 1)
        upper = jnp.where(si >= li, 1.0, 0.0).astype(BF16)
        dla = _xdot(upper, dcs)
        ddt_ref[...] = dla * a + ddt_x
        da_ref[...] += jnp.sum(dla * dtv, axis=0, keepdims=True)

    def row(w):
        return pl.BlockSpec((L, w), lambda b, c: (b * nc + nc - 1 - c, 0))

    return _pallas(
        body, name="ssd_bwd", grid=(nb, nc),
        in_specs=[row(XBC), row(LANE), pl.BlockSpec((1, LANE), lambda b, c: (0, 0)),
                  pl.BlockSpec((SSM_H, D_INNER), lambda b, c: (0, 0)),
                  pl.BlockSpec((D_INNER, LANE), lambda b, c: (0, 0)),
                  pl.BlockSpec((1, SSM_N, D_INNER), lambda b, c: (b * nc + nc - 1 - c, 0, 0)),
                  row(D_INNER), row(D_INNER)],
        out_specs=[row(XBC), row(LANE), pl.BlockSpec((1, LANE), lambda b, c: (0, 0))],
        out_shape=[jax.ShapeDtypeStruct((t, XBC), F32), jax.ShapeDtypeStruct((t, LANE), F32),
                   jax.ShapeDtypeStruct((1, LANE), F32)],
        scratch_shapes=[pltpu.VMEM((SSM_N, D_INNER), F32), pltpu.VMEM((L, D_INNER), F32),
                        pltpu.VMEM((L, D_INNER), F32), pltpu.VMEM((8, D_INNER), F32)],
        compiler_params=_params(2),
    )(xbc_c, dt, a_log, ex, ext, states, dy, dskip)


def _group_rms(y2):
    gw = D_INNER // SSM_G
    parts = []
    for g in range(SSM_G):
        v = y2[:, g * gw:(g + 1) * gw]
        r = lax.rsqrt(jnp.mean(v * v, axis=1, keepdims=True) + EPS)
        parts.append(jnp.broadcast_to(r, v.shape))
    return jnp.concatenate(parts, axis=1)


def _gate_fwd(y, xbc_c, proj, d_x, gn_w, *, tm=256):
    t = y.shape[0]

    def body(y_ref, x_ref, z_ref, d_ref, w_ref, o_ref):
        y1 = y_ref[...] + d_ref[...] * x_ref[...]
        zv = z_ref[...]
        y2 = y1 * zv * _sigmoid(zv)
        o_ref[...] = (y2 * _group_rms(y2) * w_ref[...]).astype(BF16)

    row = pl.BlockSpec((tm, D_INNER), lambda i: (i, 0))
    vec = pl.BlockSpec((1, D_INNER), lambda i: (0, 0))
    return _pallas(
        body, name="gate_fwd", grid=(t // tm,), in_specs=[row, row, row, vec, vec], out_specs=row,
        out_shape=jax.ShapeDtypeStruct((t, D_INNER), BF16), compiler_params=_params(1),
    )(y, xbc_c, proj, d_x, gn_w)


def _gate_bwd(dyg, y, xbc_c, proj, d_x, gn_w, *, tm=256):
    t = y.shape[0]
    gw = D_INNER // SSM_G

    def body(dg_ref, y_ref, x_ref, z_ref, d_ref, w_ref, dy_ref, dsk_ref, dz_ref, dw_ref, dd_ref):
        i = pl.program_id(0)
        xv = x_ref[...]
        dxv = d_ref[...]
        y1 = y_ref[...] + dxv * xv
        zv = z_ref[...]
        sz = _sigmoid(zv)
        y2 = y1 * zv * sz
        rr = _group_rms(y2)
        xh = y2 * rr
        dg = dg_ref[...]
        gq = dg * w_ref[...]
        prod = gq * xh
        means = []
        for g in range(SSM_G):
            mg = jnp.mean(prod[:, g * gw:(g + 1) * gw], axis=1, keepdims=True)
            means.append(jnp.broadcast_to(mg, (tm, gw)))
        dy2 = rr * (gq - xh * jnp.concatenate(means, axis=1))
        dy1 = dy2 * zv * sz
        dy_ref[...] = dy1
        dsk_ref[...] = dy1 * dxv
        dz_ref[...] = (dy2 * y1 * _dsilu(zv, sz)).astype(BF16)

        @pl.when(i == 0)
        def _():
            dw_ref[...] = jnp.zeros_like(dw_ref)
            dd_ref[...] = jnp.zeros_like(dd_ref)

        dw_ref[...] += jnp.sum(dg * xh, axis=0, keepdims=True)
        dd_ref[...] += jnp.sum(dy1 * xv, axis=0, keepdims=True)

    row = pl.BlockSpec((tm, D_INNER), lambda i: (i, 0))
    vec = pl.BlockSpec((1, D_INNER), lambda i: (0, 0))
    return _pallas(
        body, name="gate_bwd", grid=(t // tm,), in_specs=[row, row, row, row, vec, vec],
        out_specs=[row, row, row, vec, vec],
        out_shape=[jax.ShapeDtypeStruct((t, D_INNER), F32), jax.ShapeDtypeStruct((t, D_INNER), F32),
                   jax.ShapeDtypeStruct((t, D_INNER), BF16), jax.ShapeDtypeStruct((1, D_INNER), F32),
                   jax.ShapeDtypeStruct((1, D_INNER), F32)],
        compiler_params=_params(1),
    )(dyg, y, xbc_c, proj, d_x, gn_w)


ANY = pl.BlockSpec(memory_space=pl.ANY)


def _gather_shards(big, small):
    def body(big_ref, small_ref, obig_ref, osmall_ref, send_sems, recv_sems, local_sems):
        x, y, c = lax.axis_index("x"), lax.axis_index("y"), lax.axis_index("c")
        me = 2 * x + y
        peers = [(1 - x, y), (x, 1 - y), (1 - x, 1 - y)]
        own = [pltpu.make_async_copy(big_ref, obig_ref.at[me], local_sems.at[0]),
               pltpu.make_async_copy(small_ref, osmall_ref.at[me], local_sems.at[1])]
        for cp in own:
            cp.start()

        def copies(k, slot, to):
            return [pltpu.make_async_remote_copy(src_ref=big_ref, dst_ref=obig_ref.at[slot],
                                                 send_sem=send_sems.at[2 * k], recv_sem=recv_sems.at[2 * k],
                                                 device_id=to, device_id_type=MESH),
                    pltpu.make_async_remote_copy(src_ref=small_ref, dst_ref=osmall_ref.at[slot],
                                                 send_sem=send_sems.at[2 * k + 1], recv_sem=recv_sems.at[2 * k + 1],
                                                 device_id=to, device_id_type=MESH)]

        sends = []
        for k, (px, py) in enumerate(peers):
            sends += copies(k, me, (px, py, c))
        for cp in sends:
            cp.start()
        for k, (px, py) in enumerate(peers):
            for cp in copies(k, 2 * px + py, (px, py, c)):
                cp.wait_recv()
        for cp in sends:
            cp.wait_send()
        for cp in own:
            cp.wait()

    n = 4
    return _pallas(
        body, name="gather_shards", in_specs=[ANY, ANY], out_specs=[ANY, ANY],
        out_shape=[jax.ShapeDtypeStruct((n,) + big.shape, big.dtype),
                   jax.ShapeDtypeStruct((n,) + small.shape, small.dtype)],
        scratch_shapes=[pltpu.SemaphoreType.DMA((6,)), pltpu.SemaphoreType.DMA((6,)),
                        pltpu.SemaphoreType.DMA((2,))],
    )(big, small)


def _exchange_grads(gbig, gsmall, grep):
    def body(big_ref, small_ref, rep_ref, obig_ref, osmall_ref, orep_ref, send_sems, recv_sems, local_sems):
        x, y, c = lax.axis_index("x"), lax.axis_index("y"), lax.axis_index("c")
        me = 4 * x + 2 * y + c
        chip = 2 * x + y
        own = [pltpu.make_async_copy(big_ref.at[chip], obig_ref.at[me], local_sems.at[0]),
               pltpu.make_async_copy(small_ref.at[chip], osmall_ref.at[me], local_sems.at[1]),
               pltpu.make_async_copy(rep_ref, orep_ref.at[me], local_sems.at[2])]
        for cp in own:
            cp.start()

        def copies(k, src_chip_slot, slot, to):
            mk = pltpu.make_async_remote_copy
            return [mk(src_ref=big_ref.at[src_chip_slot], dst_ref=obig_ref.at[slot], send_sem=send_sems.at[3 * k],
                       recv_sem=recv_sems.at[3 * k], device_id=to, device_id_type=MESH),
                    mk(src_ref=small_ref.at[src_chip_slot], dst_ref=osmall_ref.at[slot], send_sem=send_sems.at[3 * k + 1],
                       recv_sem=recv_sems.at[3 * k + 1], device_id=to, device_id_type=MESH),
                    mk(src_ref=rep_ref, dst_ref=orep_ref.at[slot], send_sem=send_sems.at[3 * k + 2],
                       recv_sem=recv_sems.at[3 * k + 2], device_id=to, device_id_type=MESH)]

        peers = []
        for k in range(7):
            fx, fy, fc = ((k + 1) >> 2) & 1, ((k + 1) >> 1) & 1, (k + 1) & 1
            peers.append((1 - x if fx else x, 1 - y if fy else y, 1 - c if fc else c))
        sends = []
        for k, (px, py, pc) in enumerate(peers):
            sends += copies(k, 2 * px + py, me, (px, py, pc))
        for cp in sends:
            cp.start()
        for k, (px, py, pc) in enumerate(peers):
            for cp in copies(k, chip, 4 * px + 2 * py + pc, (px, py, pc)):
                cp.wait_recv()
        for cp in sends:
            cp.wait_send()
        for cp in own:
            cp.wait()

    n = 8
    return _pallas(
        body, name="exchange_grads", in_specs=[ANY, ANY, ANY], out_specs=[ANY, ANY, ANY],
        out_shape=[jax.ShapeDtypeStruct((n,) + gbig.shape[1:], gbig.dtype),
                   jax.ShapeDtypeStruct((n,) + gsmall.shape[1:], gsmall.dtype),
                   jax.ShapeDtypeStruct((n,) + grep.shape, grep.dtype)],
        scratch_shapes=[pltpu.SemaphoreType.DMA((21,)), pltpu.SemaphoreType.DMA((21,)),
                        pltpu.SemaphoreType.DMA((3,))],
    )(gbig, gsmall, grep)


def _adamw(slots, w, m, v, *, name, tr):
    nd, rows, _ = slots.shape
    c1 = 1.0 - ADAM_B1 ** ADAM_STEP
    c2 = 1.0 - ADAM_B2 ** ADAM_STEP

    def body(s_ref, w_ref, m_ref, v_ref, g_ref, d_ref, nm_ref, nv_ref):
        g = s_ref[0]
        for d in range(1, nd):
            g = g + s_ref[d]
        m2 = ADAM_B1 * m_ref[...] + (1.0 - ADAM_B1) * g
        v2 = ADAM_B2 * v_ref[...] + (1.0 - ADAM_B2) * (g * g)
        g_ref[...] = g
        nm_ref[...] = m2
        nv_ref[...] = v2
        d_ref[...] = -ADAM_LR * ((m2 / c1) / (jnp.sqrt(v2 / c2) + ADAM_EPS) + ADAM_WD * w_ref[...])

    row = pl.BlockSpec((tr, LANE), lambda i: (i, 0))
    out = jax.ShapeDtypeStruct((rows, LANE), F32)
    return _pallas(
        body, name=name, grid=(rows // tr,),
        in_specs=[pl.BlockSpec((nd, tr, LANE), lambda i: (0, i, 0)), row, row, row],
        out_specs=[row, row, row, row], out_shape=[out, out, out, out], compiler_params=_params(1),
    )(slots, w, m, v)


def _rows(a):
    return a.reshape(-1, LANE)


def _pad_rows(a, mult):
    pad = (-a.shape[0]) % mult
    return jnp.pad(a, ((0, pad), (0, 0))) if pad else a


def _pack(parts, mult):
    return _pad_rows(jnp.concatenate([_rows(p) for p in parts], axis=0), mult)


def _unpack(slab, shapes):
    out, r0 = [], 0
    for shp in shapes:
        n = 1
        for s in shp:
            n *= s
        r = n // LANE
        out.append(slab[r0:r0 + r].reshape(shp))
        r0 += r
    return out


def _pack_rep(vecs, scal):
    srow = jnp.concatenate([s.reshape(-1) for s in scal] + [jnp.zeros((LANE - 3 * SSM_H,), F32)]).reshape(1, LANE)
    return _pad_rows(jnp.concatenate([_rows(vv) for vv in vecs] + [srow], axis=0), 8)


def _unpack_rep(slab, vec_shapes, scal_shape):
    vecs, r0 = [], 0
    for shp in vec_shapes:
        vecs.append(slab[r0:r0 + 8].reshape(shp))
        r0 += 8
    srow = slab[r0]
    scal = [srow[i * SSM_H:(i + 1) * SSM_H].reshape(scal_shape) for i in range(3)]
    return vecs, scal


def kernel(x, ev_norm_w, ev_w_in, ev_dw_w, ev_dw_b, ev_ln_w, ev_ln_b, ev_w_out, od_norm_w, od_w_in, od_conv_w, od_conv_b, od_dt_bias, od_a_log, od_d, od_gnorm_w, od_w_out, final_norm_w, loss_target, m_ev_norm_w, m_ev_w_in, m_ev_dw_w, m_ev_dw_b, m_ev_ln_w, m_ev_ln_b, m_ev_w_out, m_od_norm_w, m_od_w_in, m_od_conv_w, m_od_conv_b, m_od_dt_bias, m_od_a_log, m_od_d, m_od_gnorm_w, m_od_w_out, m_final_norm_w, v_ev_norm_w, v_ev_w_in, v_ev_dw_w, v_ev_dw_b, v_ev_ln_w, v_ev_ln_b, v_ev_w_out, v_od_norm_w, v_od_w_in, v_od_conv_w, v_od_conv_b, v_od_dt_bias, v_od_a_log, v_od_d, v_od_gnorm_w, v_od_w_out, v_final_norm_w):
    nb, seq, d = x.shape
    t = nb * seq
    nchip = 4
    xf = x.reshape(t, d)
    tgt = loss_target.reshape(t, d)

    big_w = [ev_w_in[0], od_w_in[0], ev_w_out[0], od_w_out[0]]
    big_shapes = [a.shape for a in big_w]
    small_w = [ev_dw_w[0], od_norm_w[0], od_conv_w[0], od_conv_b[0], od_gnorm_w[0]]
    small_shapes = [a.shape for a in small_w]
    gath_big, gath_small = _gather_shards(_pack(big_w, 16).astype(BF16), _pack(small_w, 8))
    per_chip = [_unpack(gath_big[p], big_shapes) + _unpack(gath_small[p], small_shapes) for p in range(nchip)]

    def cat(idx, axis):
        return jnp.concatenate([per_chip[p][idx] for p in range(nchip)], axis=axis)

    w_in0 = cat(0, 1)
    w_in1 = jnp.pad(cat(1, 1), ((0, 0), (0, IN_ODD_PAD - IN_ODD)))
    w_out0 = cat(2, 0)
    w_out1 = cat(3, 0)
    dw_w = jnp.pad(cat(4, 1), ((0, HALO - CONF_K), (0, 0)))
    n1_w = cat(5, 0).reshape(1, d)
    conv_w = jnp.pad(cat(6, 1), ((0, PH - SSM_K), (0, 0)))
    conv_b = cat(7, 0).reshape(1, XBC)
    gn_w = cat(8, 0).reshape(1, D_INNER)

    def lanes(a):
        return jnp.pad(a.reshape(1, -1), ((0, 0), (0, LANE - a.size)))

    dt_bias, a_log = lanes(od_dt_bias), lanes(od_a_log)
    d_x = jnp.repeat(od_d.reshape(-1), SSM_P).reshape(1, D_INNER)
    hid = lax.broadcasted_iota(jnp.int32, (SSM_H, D_INNER), 1) // SSM_P
    ex = (hid == lax.broadcasted_iota(jnp.int32, (SSM_H, D_INNER), 0)).astype(BF16)
    ex_t = jnp.pad(ex.T, ((0, 0), (0, LANE - SSM_H)))
    fn_w = final_norm_w.reshape(1, d)

    n0 = _rms_fwd(xf, ev_norm_w, name="rms_fwd0")
    proj0 = _matmul(n0, w_in0, mode="nn", out_dtype=F32, bm=512, bn=1024, bk=d, name="in_proj0")
    y_conv = _conf_fwd(proj0, dw_w, ev_dw_b, ev_ln_w, ev_ln_b, seq)
    o_att, y_att = _sba_fwd(proj0, nb, seq)
    ycat0 = jnp.concatenate([y_conv, y_att], axis=1)
    h1 = _matmul(ycat0, w_out0, mode="nn", out_dtype=F32, bm=512, bn=d, bk=D_INNER, name="out_proj0", residual=xf)
    n1 = _rms_fwd(h1, n1_w, name="rms_fwd1")
    proj1 = _matmul(n1, w_in1, mode="nn", out_dtype=F32, bm=512, bn=768, bk=d, name="in_proj1")
    xbc_c = _xconv_fwd(proj1, conv_w, conv_b, seq)
    dt = _dt_fwd(proj1, dt_bias)
    y_ssd, states = _ssd_fwd(xbc_c, dt, a_log, ex, nb, seq)
    yg = _gate_fwd(y_ssd, xbc_c, proj1, d_x, gn_w)
    h2 = _matmul(yg, w_out1, mode="nn", out_dtype=F32, bm=512, bn=d, bk=D_INNER, name="out_proj1", residual=h1)
    dh2, g_fn, loss_part = _final_loss(h2, fn_w, tgt)

    dyg = _matmul(dh2, w_out1, mode="nt", out_dtype=F32, bm=512, bn=1024, bk=d, name="d_out_proj1")
    g_w_out1 = _matmul(yg, dh2, mode="tn", out_dtype=F32, bm=1024, bn=d, bk=512, name="dw_out_proj1")
    dy_ssd, dskip, dz, g_gn, g_dx = _gate_bwd(dyg, y_ssd, xbc_c, proj1, d_x, gn_w)
    dxbc_c, ddt, g_a = _ssd_bwd(xbc_c, dt, a_log, ex, ex_t, states, dy_ssd, dskip, nb, seq)
    dxbc, g_conv_w, g_conv_b = _xconv_bwd(proj1, dxbc_c, conv_w, conv_b, seq)
    ddt_raw, g_dt_bias = _dt_bwd(proj1, dt_bias, ddt)
    dproj1 = jnp.concatenate([dz, dxbc, ddt_raw.astype(BF16),
                              jnp.zeros((t, IN_ODD_PAD - IN_ODD - (LANE - SSM_H)), BF16)], axis=1)
    dn1 = _matmul(dproj1, w_in1, mode="nt", out_dtype=F32, bm=512, bn=d, bk=1792, name="d_in_proj1")
    g_w_in1 = _matmul(n1, dproj1, mode="tn", out_dtype=F32, bm=d, bn=768, bk=512, name="dw_in_proj1")
    dh1, g_n1 = _rms_bwd(dn1, h1, n1_w, dh2, name="rms_bwd1")

    dycat0 = _matmul(dh1, w_out0, mode="nt", out_dtype=F32, bm=512, bn=1024, bk=d, name="d_out_proj0")
    g_w_out0 = _matmul(ycat0, dh1, mode="tn", out_dtype=F32, bm=1024, bn=d, bk=512, name="dw_out_proj0")
    dq, dk, dv, dga = _sba_bwd(proj0, o_att, dycat0, nb, seq)
    dpc, g_dw_w, g_dw_b, g_ln_w, g_ln_b = _conf_bwd(proj0, dycat0, dw_w, ev_dw_b, ev_ln_w, ev_ln_b, seq)
    dproj0 = jnp.concatenate([dpc, dq, dk.astype(BF16), dv.astype(BF16), dga], axis=1)
    dn0 = _matmul(dproj0, w_in0, mode="nt", out_dtype=F32, bm=512, bn=d, bk=1792, name="d_in_proj0")
    g_w_in0 = _matmul(n0, dproj0, mode="tn", out_dtype=F32, bm=d, bn=1024, bk=512, name="dw_in_proj0")
    grad_x, g_n0 = _rms_bwd(dn0, xf, ev_norm_w, dh1, name="rms_bwd0")

    g_dw_w = g_dw_w[0:CONF_K]
    g_conv_w = g_conv_w[0:SSM_K]
    g_w_in1 = g_w_in1[:, 0:IN_ODD]
    a_neg = -jnp.exp(od_a_log.reshape(-1))
    g_a_log = g_a[0, 0:SSM_H] * a_neg
    g_d = g_dx.reshape(SSM_H, SSM_P).sum(axis=1)

    def chip_slab_big(p):
        c0, c1, r = IN_EVEN // nchip, IN_ODD // nchip, D_INNER // nchip
        return _pack([g_w_in0[:, p * c0:(p + 1) * c0], g_w_in1[:, p * c1:(p + 1) * c1],
                      g_w_out0[p * r:(p + 1) * r], g_w_out1[p * r:(p + 1) * r]], 16)

    def chip_slab_small(p):
        c0, c1, c2, c3 = CONV_W // nchip, d // nchip, XBC // nchip, D_INNER // nchip
        return _pack([g_dw_w[:, p * c0:(p + 1) * c0], g_n1[0, p * c1:(p + 1) * c1],
                      g_conv_w[:, p * c2:(p + 1) * c2], g_conv_b[0, p * c2:(p + 1) * c2],
                      g_gn[0, p * c3:(p + 1) * c3]], 8)

    gbig = jnp.stack([chip_slab_big(p) for p in range(nchip)])
    gsmall = jnp.stack([chip_slab_small(p) for p in range(nchip)])
    rep_vec_shapes = [ev_norm_w.shape, ev_dw_b.shape, ev_ln_w.shape, ev_ln_b.shape, final_norm_w.shape]
    grep = _pack_rep([g_n0, g_dw_b, g_ln_w, g_ln_b, g_fn], [g_dt_bias[0, 0:SSM_H], g_a_log, g_d])
    sbig, ssmall, srep = _exchange_grads(gbig, gsmall, grep)

    def upd(slots, ws, ms, vs, packer, name, tr):
        return _adamw(slots, packer(ws), packer(ms), packer(vs), name=name, tr=tr)

    big_m = [m_ev_w_in[0], m_od_w_in[0], m_ev_w_out[0], m_od_w_out[0]]
    big_v = [v_ev_w_in[0], v_od_w_in[0], v_ev_w_out[0], v_od_w_out[0]]
    rb = sbig.shape[1]
    tr_big = max(r for r in range(8, 2049, 8) if rb % r == 0)
    out_big = upd(sbig, big_w, big_m, big_v, lambda a: _pack(a, 16), "adamw_big", tr_big)
    small_m = [m_ev_dw_w[0], m_od_norm_w[0], m_od_conv_w[0], m_od_conv_b[0], m_od_gnorm_w[0]]
    small_v = [v_ev_dw_w[0], v_od_norm_w[0], v_od_conv_w[0], v_od_conv_b[0], v_od_gnorm_w[0]]
    out_small = upd(ssmall, small_w, small_m, small_v, lambda a: _pack(a, 8), "adamw_small", ssmall.shape[1])

    def rep_pack(a):
        return _pack_rep(a[0:5], a[5:8])

    rep_w = [ev_norm_w, ev_dw_b, ev_ln_w, ev_ln_b, final_norm_w, od_dt_bias, od_a_log, od_d]
    rep_m = [m_ev_norm_w, m_ev_dw_b, m_ev_ln_w, m_ev_ln_b, m_final_norm_w, m_od_dt_bias, m_od_a_log, m_od_d]
    rep_v = [v_ev_norm_w, v_ev_dw_b, v_ev_ln_w, v_ev_ln_b, v_final_norm_w, v_od_dt_bias, v_od_a_log, v_od_d]
    out_rep = upd(srep, rep_w, rep_m, rep_v, rep_pack, "adamw_rep", srep.shape[1])

    results = []
    for kind in range(4):
        bw = [a.reshape((1,) + a.shape) for a in _unpack(out_big[kind], big_shapes)]
        sw = _unpack(out_small[kind], small_shapes)
        vecs, scal = _unpack_rep(out_rep[kind], rep_vec_shapes, od_dt_bias.shape)
        results.append([
            vecs[0], bw[0], sw[0].reshape(ev_dw_w.shape), vecs[1], vecs[2], vecs[3], bw[2],
            sw[1].reshape(od_norm_w.shape), bw[1], sw[2].reshape(od_conv_w.shape), sw[3].reshape(od_conv_b.shape),
            scal[0], scal[1], scal[2], sw[4].reshape(od_gnorm_w.shape), bw[3], vecs[4]])
    loss = lax.psum(loss_part[0, 0], ("x", "y", "c"))
    return (loss, grad_x.reshape(x.shape), *results[0], *results[1], *results[2], *results[3])
```

```python
import jax
import jax.numpy as jnp
from jax import lax
from jax.experimental import pallas as pl
from jax.experimental.pallas import tpu as pltpu

F32 = jnp.float32
BF16 = jnp.bfloat16

D_MODEL = 1024
CONV_W = 1024
ATT_W = 1024
HEAD_DIM = 128
N_HEADS = 8
CONF_K = 31
IN_EVEN = 7168
D_INNER = 2048
SSM_P = 64
SSM_H = 32
SSM_G = 4
SSM_R = SSM_H // SSM_G
SSM_N = 128
SSM_K = 4
CHUNK = 128
XBC = D_INNER + 2 * SSM_G * SSM_N
IN_ODD = D_INNER + XBC + SSM_H
IN_ODD_PAD = 5376
EPS = 1e-6
QB = 128
NEG_CUT = -100.0

ADAM_LR = 0.001
ADAM_B1 = 0.9
ADAM_B2 = 0.999
ADAM_EPS = 1e-08
ADAM_WD = 0.01
ADAM_STEP = 10

LANE = 128
VMEM_LIMIT = 56 * 1024 * 1024
MESH = pl.DeviceIdType.MESH

NN = (((1,), (0,)), ((), ()))
NT = (((1,), (1,)), ((), ()))
TN = (((0,), (0,)), ((), ()))


def _pallas(body, **kw):
    return pl.pallas_call(body, **kw)


def _params(n_axes):
    return pltpu.CompilerParams(dimension_semantics=("arbitrary",) * n_axes, vmem_limit_bytes=VMEM_LIMIT)


def _dot(a, b, dims=NN):
    return lax.dot_general(a.astype(BF16), b.astype(BF16), dims, preferred_element_type=F32)


def _parts(x):
    h = x.astype(BF16)
    r = x - h.astype(F32)
    m = r.astype(BF16)
    l = (r - m.astype(F32)).astype(BF16)
    return (h, m, l)


def _dotx(x, e01, dims=NN):
    acc = None
    for p in _parts(x):
        t = lax.dot_general(p, e01, dims, preferred_element_type=F32)
        acc = t if acc is None else acc + t
    return acc


def _xdot(e01, x, dims=NN):
    acc = None
    for p in _parts(x):
        t = lax.dot_general(e01, p, dims, preferred_element_type=F32)
        acc = t if acc is None else acc + t
    return acc


def _sigmoid(x):
    return 1.0 / (1.0 + jnp.exp(-x))


def _dsilu(x, s):
    return s * (1.0 + x * (1.0 - s))


def _matmul(a, b, *, mode, out_dtype, bm, bn, bk, name, residual=None):
    if mode == "nn":
        (m, k), n = a.shape, b.shape[1]
        a_spec = pl.BlockSpec((bm, bk), lambda i, j, kk: (i, kk))
        b_spec = pl.BlockSpec((bk, bn), lambda i, j, kk: (kk, j))
        dims = NN
    elif mode == "nt":
        (m, k), n = a.shape, b.shape[0]
        a_spec = pl.BlockSpec((bm, bk), lambda i, j, kk: (i, kk))
        b_spec = pl.BlockSpec((bn, bk), lambda i, j, kk: (j, kk))
        dims = NT
    else:
        (k, m), n = a.shape, b.shape[1]
        a_spec = pl.BlockSpec((bk, bm), lambda i, j, kk: (kk, i))
        b_spec = pl.BlockSpec((bk, bn), lambda i, j, kk: (kk, j))
        dims = TN
    assert m % bm == 0 and n % bn == 0 and k % bk == 0, (name, m, n, k)
    nk = k // bk
    has_res = residual is not None

    def body(*refs):
        if has_res:
            a_ref, b_ref, r_ref, o_ref, acc_ref = refs
        else:
            a_ref, b_ref, o_ref, acc_ref = refs
        kk = pl.program_id(2)

        @pl.when(kk == 0)
        def _():
            acc_ref[...] = jnp.zeros_like(acc_ref)

        acc_ref[...] += _dot(a_ref[...], b_ref[...], dims)

        @pl.when(kk == nk - 1)
        def _():
            r = acc_ref[...]
            if has_res:
                r = r + r_ref[...]
            o_ref[...] = r.astype(out_dtype)

    in_specs = [a_spec, b_spec]
    args = [a, b]
    if has_res:
        in_specs.append(pl.BlockSpec((bm, bn), lambda i, j, kk: (i, j)))
        args.append(residual)
    return _pallas(
        body, name=name, grid=(m // bm, n // bn, nk), in_specs=in_specs,
        out_specs=pl.BlockSpec((bm, bn), lambda i, j, kk: (i, j)),
        out_shape=jax.ShapeDtypeStruct((m, n), out_dtype),
        scratch_shapes=[pltpu.VMEM((bm, bn), F32)], compiler_params=_params(3),
    )(*args)


def _rms_fwd(x, w, *, name, tm=512):
    t, d = x.shape

    def body(x_ref, w_ref, o_ref):
        xv = x_ref[...]
        r = lax.rsqrt(jnp.mean(xv * xv, axis=1, keepdims=True) + EPS)
        o_ref[...] = (xv * r * w_ref[...]).astype(BF16)

    return _pallas(
        body, name=name, grid=(t // tm,),
        in_specs=[pl.BlockSpec((tm, d), lambda i: (i, 0)), pl.BlockSpec((1, d), lambda i: (0, 0))],
        out_specs=pl.BlockSpec((tm, d), lambda i: (i, 0)),
        out_shape=jax.ShapeDtypeStruct((t, d), BF16), compiler_params=_params(1),
    )(x, w)


def _rms_bwd(dn, x, w, dres, *, name, tm=512):
    t, d = x.shape

    def body(dn_ref, x_ref, w_ref, dr_ref, dx_ref, dw_ref):
        i = pl.program_id(0)
        xv = x_ref[...]
        r = lax.rsqrt(jnp.mean(xv * xv, axis=1, keepdims=True) + EPS)
        xh = xv * r
        dy = dn_ref[...].astype(F32)
        g = dy * w_ref[...]
        dx_ref[...] = dr_ref[...] + r * (g - xh * jnp.mean(g * xh, axis=1, keepdims=True))

        @pl.when(i == 0)
        def _():
            dw_ref[...] = jnp.zeros_like(dw_ref)

        dw_ref[...] += jnp.sum(dy * xh, axis=0, keepdims=True)

    row = pl.BlockSpec((tm, d), lambda i: (i, 0))
    vec = pl.BlockSpec((1, d), lambda i: (0, 0))
    return _pallas(
        body, name=name, grid=(t // tm,), in_specs=[row, row, vec, row], out_specs=[row, vec],
        out_shape=[jax.ShapeDtypeStruct((t, d), F32), jax.ShapeDtypeStruct((1, d), F32)],
        compiler_params=_params(1),
    )(dn, x, w, dres)


def _final_loss(h, w, target, *, tm=512):
    t, d = h.shape

    def body(h_ref, w_ref, t_ref, dh_ref, dw_ref, loss_ref):
        i = pl.program_id(0)
        xv = h_ref[...]
        r = lax.rsqrt(jnp.mean(xv * xv, axis=1, keepdims=True) + EPS)
        xh = xv * r
        wv = w_ref[...]
        err = xh * wv - t_ref[...]
        dy = err * (1.0 / d)
        g = dy * wv
        dh_ref[...] = r * (g - xh * jnp.mean(g * xh, axis=1, keepdims=True))

        @pl.when(i == 0)
        def _():
            dw_ref[...] = jnp.zeros_like(dw_ref)
            loss_ref[...] = jnp.zeros_like(loss_ref)

        dw_ref[...] += jnp.sum(dy * xh, axis=0, keepdims=True)
        part = jnp.sum(jnp.sum(err * err, axis=1, keepdims=True), axis=0, keepdims=True)
        loss_ref[...] += part * (0.5 / d)

    row = pl.BlockSpec((tm, d), lambda i: (i, 0))
    vec = pl.BlockSpec((1, d), lambda i: (0, 0))
    return _pallas(
        body, name="final_loss", grid=(t // tm,), in_specs=[row, vec, row],
        out_specs=[row, vec, pl.BlockSpec((1, LANE), lambda i: (0, 0))],
        out_shape=[jax.ShapeDtypeStruct((t, d), F32), jax.ShapeDtypeStruct((1, d), F32),
                   jax.ShapeDtypeStruct((1, LANE), F32)],
        compiler_params=_params(1),
    )(h, w, target)


HALO = 32


def _conf_fwd(proj, dw_w, dw_b, ln_w, ln_b, seq, *, tm=256):
    t = proj.shape[0]
    c = CONV_W
    tps = seq // tm
    hb = tm // HALO

    def body(a_ref, b_ref, g_ref, ha_ref, hb_ref, w_ref, wb_ref, lw_ref, lb_ref, y_ref, ext_ref):
        i = pl.program_id(0)
        keep = jnp.where(i % tps == 0, 0.0, 1.0)
        ext_ref[0:HALO, :] = ha_ref[...] * _sigmoid(hb_ref[...]) * keep
        ext_ref[HALO:HALO + tm, :] = a_ref[...] * _sigmoid(b_ref[...])
        acc = jnp.zeros((tm, c), F32) + wb_ref[...]
        for k in range(CONF_K):
            acc = acc + w_ref[k:k + 1, :] * ext_ref[2 + k:2 + k + tm, :]
        mu = jnp.mean(acc, axis=1, keepdims=True)
        xc = acc - mu
        rs = lax.rsqrt(jnp.mean(xc * xc, axis=1, keepdims=True) + EPS)
        u3 = xc * rs * lw_ref[...] + lb_ref[...]
        gv = g_ref[...]
        y_ref[...] = (u3 * _sigmoid(u3) * gv * _sigmoid(gv)).astype(BF16)

    def col(j):
        return pl.BlockSpec((tm, c), lambda i: (i, j))

    def prev(j):
        return pl.BlockSpec((HALO, c), lambda i: (jnp.maximum(i * hb - 1, 0), j))

    vec = pl.BlockSpec((1, c), lambda i: (0, 0))
    return _pallas(
        body, name="conf_fwd", grid=(t // tm,),
        in_specs=[col(0), col(1), col(2), prev(0), prev(1),
                  pl.BlockSpec((HALO, c), lambda i: (0, 0)), vec, vec, vec],
        out_specs=pl.BlockSpec((tm, c), lambda i: (i, 0)),
        out_shape=jax.ShapeDtypeStruct((t, c), BF16),
        scratch_shapes=[pltpu.VMEM((tm + HALO, c), F32)], compiler_params=_params(1),
    )(proj, proj, proj, proj, proj, dw_w, dw_b, ln_w, ln_b)


def _conf_bwd(proj, dycat, dw_w, dw_b, ln_w, ln_b, seq, *, tm=256):
    t = proj.shape[0]
    c = CONV_W
    tps = seq // tm
    hb = tm // HALO
    nhb = t // HALO
    te = tm + HALO

    def body(a_ref, b_ref, g_ref, pa_ref, pb_ref, na_ref, nb_ref, ng_ref, dy_ref, ndy_ref,
             w_ref, wb_ref, lw_ref, lb_ref,
             dp_ref, dww_ref, dwb_ref, dlw_ref, dlb_ref, ext_ref, du2_ref):
        i = pl.program_id(0)
        first = i % tps == 0
        last = i % tps == tps - 1
        sb = _sigmoid(b_ref[...])
        av = a_ref[...]
        ext_ref[0:HALO, :] = pa_ref[...] * _sigmoid(pb_ref[...]) * jnp.where(first, 0.0, 1.0)
        ext_ref[HALO:HALO + tm, :] = av * sb
        ext_ref[HALO + tm:HALO + te, :] = na_ref[...] * _sigmoid(nb_ref[...])
        acc = jnp.zeros((te, c), F32) + wb_ref[...]
        for k in range(CONF_K):
            acc = acc + w_ref[k:k + 1, :] * ext_ref[2 + k:2 + k + te, :]
        mu = jnp.mean(acc, axis=1, keepdims=True)
        xc = acc - mu
        rs = lax.rsqrt(jnp.mean(xc * xc, axis=1, keepdims=True) + EPS)
        xh = xc * rs
        lw = lw_ref[...]
        u3 = xh * lw + lb_ref[...]
        s3 = _sigmoid(u3)
        u4 = u3 * s3
        gv = jnp.concatenate([g_ref[...], ng_ref[...]], axis=0)
        dy = jnp.concatenate([dy_ref[...], ndy_ref[...]], axis=0)
        sg = _sigmoid(gv)
        dgc = dy * u4 * _dsilu(gv, sg)
        du3 = dy * gv * sg * _dsilu(u3, s3)
        dxh = du3 * lw
        du2 = rs * (dxh - jnp.mean(dxh, axis=1, keepdims=True)
                    - xh * jnp.mean(dxh * xh, axis=1, keepdims=True))
        rows = lax.broadcasted_iota(jnp.int32, (te, 1), 0)
        du2 = jnp.where(jnp.logical_and(last, rows >= tm), 0.0, du2)
        du2_ref[...] = du2
        du1 = jnp.zeros((tm, c), F32)
        for k in range(CONF_K):
            du1 = du1 + w_ref[k:k + 1, :] * du2_ref[CONF_K - 1 - k:CONF_K - 1 - k + tm, :]
        dp_ref[:, 0:c] = (du1 * sb).astype(BF16)
        dp_ref[:, c:2 * c] = (du1 * av * sb * (1.0 - sb)).astype(BF16)
        dp_ref[:, 2 * c:3 * c] = dgc[0:tm, :].astype(BF16)

        @pl.when(i == 0)
        def _():
            dww_ref[...] = jnp.zeros_like(dww_ref)
            dwb_ref[...] = jnp.zeros_like(dwb_ref)
            dlw_ref[...] = jnp.zeros_like(dlw_ref)
            dlb_ref[...] = jnp.zeros_like(dlb_ref)

        du2t = du2_ref[0:tm, :]
        for k in range(CONF_K):
            dww_ref[k:k + 1, :] += jnp.sum(du2t * ext_ref[2 + k:2 + k + tm, :], axis=0, keepdims=True)
        dwb_ref[...] += jnp.sum(du2t, axis=0, keepdims=True)
        dlw_ref[...] += jnp.sum(du3[0:tm, :] * xh[0:tm, :], axis=0, keepdims=True)
        dlb_ref[...] += jnp.sum(du3[0:tm, :], axis=0, keepdims=True)

    def col(j):
        return pl.BlockSpec((tm, c), lambda i: (i, j))

    def prev(j):
        return pl.BlockSpec((HALO, c), lambda i: (jnp.maximum(i * hb - 1, 0), j))

    def nxt(j):
        return pl.BlockSpec((HALO, c), lambda i: (jnp.minimum((i + 1) * hb, nhb - 1), j))

    vec = pl.BlockSpec((1, c), lambda i: (0, 0))
    wsp = pl.BlockSpec((HALO, c), lambda i: (0, 0))
    return _pallas(
        body, name="conf_bwd", grid=(t // tm,),
        in_specs=[col(0), col(1), col(2), prev(0), prev(1), nxt(0), nxt(1), nxt(2), col(0), nxt(0),
                  wsp, vec, vec, vec],
        out_specs=[pl.BlockSpec((tm, 3 * c), lambda i: (i, 0)), wsp, vec, vec, vec],
        out_shape=[jax.ShapeDtypeStruct((t, 3 * c), BF16), jax.ShapeDtypeStruct((HALO, c), F32),
                   jax.ShapeDtypeStruct((1, c), F32), jax.ShapeDtypeStruct((1, c), F32),
                   jax.ShapeDtypeStruct((1, c), F32)],
        scratch_shapes=[pltpu.VMEM((tm + 2 * HALO, c), F32), pltpu.VMEM((te, c), F32)],
        compiler_params=_params(1),
    )(proj, proj, proj, proj, proj, proj, proj, proj, dycat, dycat, dw_w, dw_b, ln_w, ln_b)


Q_COL = 3 * CONV_W // HEAD_DIM
K_COL = Q_COL + N_HEADS
V_COL = K_COL + N_HEADS
GA_COL = V_COL + N_HEADS


SBA_TQ = 256
SBA_WK = 5 * QB


def _sb_window(qs, kw, ws, limit, t0, carry):
    tq, wk = qs.shape[0], kw.shape[0]
    z = _dot(qs, kw, NT)
    sg = ws + lax.broadcasted_iota(jnp.int32, (tq, wk), 1)
    tg = t0 + lax.broadcasted_iota(jnp.int32, (tq, wk), 0)
    mask = sg < jnp.minimum(tg, limit)
    sp = jnp.log(1.0 + jnp.exp(-jnp.abs(z)))
    ls = jnp.minimum(z, 0.0) - sp
    lk = jnp.where(mask, ls - z, 0.0)
    jj = lax.broadcasted_iota(jnp.int32, (QB, QB), 0)
    ss = lax.broadcasted_iota(jnp.int32, (QB, QB), 1)
    ustrict = jnp.where(jj > ss, 1.0, 0.0).astype(BF16)
    laters = [None] * (wk // QB)
    for ch in reversed(range(wk // QB)):
        lkc = lk[:, ch * QB:(ch + 1) * QB]
        laters[ch] = carry + _dotx(lkc, ustrict)
        carry = carry + jnp.sum(lkc, axis=1, keepdims=True)
    w = jnp.where(mask, jnp.exp(ls + jnp.concatenate(laters, axis=1)), 0.0)
    return mask, ls, w, carry


def _sba_fwd(proj, nb, seq, *, tq=SBA_TQ, wk=SBA_WK):
    t = proj.shape[0]
    wk = min(wk, seq)
    nq = seq // tq
    scale = HEAD_DIM ** -0.5

    def body(q_ref, k_ref, v_ref, g_ref, o_ref, y_ref):
        i = pl.program_id(2)
        t0 = i * tq
        qs = (q_ref[...] * scale).astype(BF16)

        def window(ws, limit, carry, acc):
            ws = pl.multiple_of(ws, QB)
            _, _, w, carry = _sb_window(qs, k_ref[pl.ds(ws, wk), :], ws, limit, t0, carry)
            return carry, acc + _dot(w, v_ref[pl.ds(ws, wk), :])

        ws0 = jnp.maximum(t0 + tq - wk, 0)
        carry, acc = window(ws0, seq, jnp.zeros((tq, 1), F32), jnp.zeros((tq, HEAD_DIM), F32))

        def cond(st):
            return jnp.logical_and(st[0] > 0, jnp.max(st[1]) > NEG_CUT)

        def step(st):
            c2, a2 = window(jnp.maximum(st[0] - wk, 0), st[0], st[1], st[2])
            return jnp.maximum(st[0] - wk, 0), c2, a2

        _, _, acc = lax.while_loop(cond, step, (ws0, carry, acc))
        o_ref[...] = acc
        gv = g_ref[...]
        y_ref[...] = (acc * gv * _sigmoid(gv)).astype(BF16)

    def tile(c0):
        return pl.BlockSpec((tq, HEAD_DIM), lambda b, h, i: (b * nq + i, c0 + h))

    def whole(c0):
        return pl.BlockSpec((seq, HEAD_DIM), lambda b, h, i: (b, c0 + h))

    return _pallas(
        body, name="sba_fwd", grid=(nb, N_HEADS, nq),
        in_specs=[tile(Q_COL), whole(K_COL), whole(V_COL), tile(GA_COL)],
        out_specs=[tile(0), tile(0)],
        out_shape=[jax.ShapeDtypeStruct((t, ATT_W), F32), jax.ShapeDtypeStruct((t, ATT_W), BF16)],
        compiler_params=_params(3),
    )(proj, proj, proj, proj)


def _sba_bwd(proj, o, dycat, nb, seq, *, tq=SBA_TQ, wk=SBA_WK):
    t = proj.shape[0]
    wk = min(wk, seq)
    nq = seq // tq
    nwin = -(-seq // wk) + 1
    nch = wk // QB
    scale = HEAD_DIM ** -0.5

    def body(q_ref, k_ref, v_ref, g_ref, o_ref, dy_ref, dq_ref, dk_ref, dv_ref, dg_ref, e_ref, sp_ref):
        i = pl.program_id(2)
        t0 = i * tq

        @pl.when(i == 0)
        def _():
            dk_ref[...] = jnp.zeros_like(dk_ref)
            dv_ref[...] = jnp.zeros_like(dv_ref)

        qs = (q_ref[...] * scale).astype(BF16)
        gv = g_ref[...]
        sg = _sigmoid(gv)
        dy = dy_ref[...]
        do = (dy * gv * sg).astype(BF16)
        dg_ref[...] = (dy * o_ref[...] * _dsilu(gv, sg)).astype(BF16)

        def start_of(n):
            return pl.multiple_of(jnp.maximum(t0 + tq - (n + 1) * wk, 0), QB)

        def limit_of(n):
            return jnp.where(n == 0, seq, jnp.maximum(t0 + tq - n * wk, 0))

        def near(n, carry):
            ws = start_of(n)
            _, ls, w, carry = _sb_window(qs, k_ref[pl.ds(ws, wk), :], ws, limit_of(n), t0, carry)
            e_ref[n] = w * _dot(do, v_ref[pl.ds(ws, wk), :], NT)
            sp_ref[n] = jnp.exp(ls)
            dv_ref[pl.ds(ws, wk), :] += _dot(w, do, TN)
            return carry

        carry = near(0, jnp.zeros((tq, 1), F32))

        def cond(st):
            return jnp.logical_and(start_of(st[0] - 1) > 0, jnp.max(st[1]) > NEG_CUT)

        def step(st):
            return st[0] + 1, near(st[0], st[1])

        nvis, _ = lax.while_loop(cond, step, (1, carry))

        jj = lax.broadcasted_iota(jnp.int32, (QB, QB), 0)
        ss = lax.broadcasted_iota(jnp.int32, (QB, QB), 1)
        lstrict = jnp.where(jj < ss, 1.0, 0.0).astype(BF16)

        def far(r, st):
            pre, dq = st
            n = nvis - 1 - r
            ws = start_of(n)
            e = e_ref[n]
            spn = sp_ref[n]
            gs = []
            for ch in range(nch):
                ec = e[:, ch * QB:(ch + 1) * QB]
                gs.append(pre + _dotx(ec, lstrict))
                pre = pre + jnp.sum(ec, axis=1, keepdims=True)
            sgl = ws + lax.broadcasted_iota(jnp.int32, (tq, wk), 1)
            tgl = t0 + lax.broadcasted_iota(jnp.int32, (tq, wk), 0)
            mask = sgl < jnp.minimum(tgl, limit_of(n))
            dz = jnp.where(mask, e * (1.0 - spn) - jnp.concatenate(gs, axis=1) * spn, 0.0).astype(BF16)
            dk_ref[pl.ds(ws, wk), :] += _dot(dz, qs, TN)
            return pre, dq + _dot(dz, k_ref[pl.ds(ws, wk), :])

        _, dq = lax.fori_loop(0, nvis, far, (jnp.zeros((tq, 1), F32), jnp.zeros((tq, HEAD_DIM), F32)))
        dq_ref[...] = (dq * scale).astype(BF16)

    def tile(c0):
        return pl.BlockSpec((tq, HEAD_DIM), lambda b, h, i: (b * nq + i, c0 + h))

    def whole(c0):
        return pl.BlockSpec((seq, HEAD_DIM), lambda b, h, i: (b, c0 + h))

    return _pallas(
        body, name="sba_bwd", grid=(nb, N_HEADS, nq),
        in_specs=[tile(Q_COL), whole(K_COL), whole(V_COL), tile(GA_COL), tile(0),
                  tile(CONV_W // HEAD_DIM)],
        out_specs=[tile(0), whole(0), whole(0), tile(0)],
        out_shape=[jax.ShapeDtypeStruct((t, ATT_W), BF16), jax.ShapeDtypeStruct((t, ATT_W), F32),
                   jax.ShapeDtypeStruct((t, ATT_W), F32), jax.ShapeDtypeStruct((t, ATT_W), BF16)],
        scratch_shapes=[pltpu.VMEM((nwin, tq, wk), F32), pltpu.VMEM((nwin, tq, wk), F32)],
        compiler_params=_params(3),
    )(proj, proj, proj, proj, o, dycat)


CT = 512
PH = 8
Z_BLK = 0
XBC_BLK = D_INNER // CT
DT_BLK = (D_INNER + XBC) // LANE


def _softplus(x):
    return jnp.maximum(x, 0.0) + jnp.log(1.0 + jnp.exp(-jnp.abs(x)))


def _dt_fwd(proj, dt_bias, *, tm=512):
    t = proj.shape[0]

    def body(p_ref, b_ref, o_ref):
        o_ref[...] = _softplus(p_ref[...] + b_ref[...])

    return _pallas(
        body, name="dt_fwd", grid=(t // tm,),
        in_specs=[pl.BlockSpec((tm, LANE), lambda i: (i, DT_BLK)), pl.BlockSpec((1, LANE), lambda i: (0, 0))],
        out_specs=pl.BlockSpec((tm, LANE), lambda i: (i, 0)),
        out_shape=jax.ShapeDtypeStruct((t, LANE), F32), compiler_params=_params(1),
    )(proj, dt_bias)


def _dt_bwd(proj, dt_bias, ddt, *, tm=512):
    t = proj.shape[0]

    def body(p_ref, b_ref, d_ref, o_ref, db_ref):
        i = pl.program_id(0)
        lanes = lax.broadcasted_iota(jnp.int32, (tm, LANE), 1)
        dr = jnp.where(lanes < SSM_H, d_ref[...] * _sigmoid(p_ref[...] + b_ref[...]), 0.0)
        o_ref[...] = dr

        @pl.when(i == 0)
        def _():
            db_ref[...] = jnp.zeros_like(db_ref)

        db_ref[...] += jnp.sum(dr, axis=0, keepdims=True)

    vec = pl.BlockSpec((1, LANE), lambda i: (0, 0))
    row = pl.BlockSpec((tm, LANE), lambda i: (i, 0))
    return _pallas(
        body, name="dt_bwd", grid=(t // tm,),
        in_specs=[pl.BlockSpec((tm, LANE), lambda i: (i, DT_BLK)), vec, row],
        out_specs=[row, vec],
        out_shape=[jax.ShapeDtypeStruct((t, LANE), F32), jax.ShapeDtypeStruct((1, LANE), F32)],
        compiler_params=_params(1),
    )(proj, dt_bias, ddt)


def _xconv_fwd(proj, conv_w, conv_b, seq, *, tm=512):
    t = proj.shape[0]
    tps = seq // tm
    hb = tm // PH

    def body(x_ref, h_ref, w_ref, b_ref, o_ref, ext_ref):
        i = pl.program_id(1)
        ext_ref[0:PH, :] = h_ref[...] * jnp.where(i % tps == 0, 0.0, 1.0)
        ext_ref[PH:PH + tm, :] = x_ref[...]
        acc = jnp.zeros((tm, CT), F32) + b_ref[...]
        for k in range(SSM_K):
            acc = acc + w_ref[k:k + 1, :] * ext_ref[PH - SSM_K + 1 + k:PH - SSM_K + 1 + k + tm, :]
        o_ref[...] = acc * _sigmoid(acc)

    return _pallas(
        body, name="xconv_fwd", grid=(XBC // CT, t // tm),
        in_specs=[pl.BlockSpec((tm, CT), lambda j, i: (i, XBC_BLK + j)),
                  pl.BlockSpec((PH, CT), lambda j, i: (jnp.maximum(i * hb - 1, 0), XBC_BLK + j)),
                  pl.BlockSpec((PH, CT), lambda j, i: (0, j)),
                  pl.BlockSpec((1, CT), lambda j, i: (0, j))],
        out_specs=pl.BlockSpec((tm, CT), lambda j, i: (i, j)),
        out_shape=jax.ShapeDtypeStruct((t, XBC), F32),
        scratch_shapes=[pltpu.VMEM((tm + PH, CT), F32)], compiler_params=_params(2),
    )(proj, proj, conv_w, conv_b)


def _xconv_bwd(proj, dxc, conv_w, conv_b, seq, *, tm=512):
    t = proj.shape[0]
    tps = seq // tm
    hb = tm // PH
    nhb = t // PH
    te = tm + PH

    def body(x_ref, p_ref, n_ref, d_ref, nd_ref, w_ref, b_ref, dx_ref, dw_ref, db_ref, ext_ref, dv_ref):
        i = pl.program_id(1)
        first = i % tps == 0
        last = i % tps == tps - 1
        ext_ref[0:PH, :] = p_ref[...] * jnp.where(first, 0.0, 1.0)
        ext_ref[PH:PH + tm, :] = x_ref[...]
        ext_ref[PH + tm:PH + te, :] = n_ref[...]
        acc = jnp.zeros((te, CT), F32) + b_ref[...]
        for k in range(SSM_K):
            acc = acc + w_ref[k:k + 1, :] * ext_ref[PH - SSM_K + 1 + k:PH - SSM_K + 1 + k + te, :]
        sv = _sigmoid(acc)
        dy = jnp.concatenate([d_ref[...], nd_ref[...]], axis=0)
        dv = dy * _dsilu(acc, sv)
        rows = lax.broadcasted_iota(jnp.int32, (te, 1), 0)
        dv_ref[...] = jnp.where(jnp.logical_and(last, rows >= tm), 0.0, dv)
        dx = jnp.zeros((tm, CT), F32)
        for k in range(SSM_K):
            dx = dx + w_ref[k:k + 1, :] * dv_ref[SSM_K - 1 - k:SSM_K - 1 - k + tm, :]
        dx_ref[...] = dx.astype(BF16)

        @pl.when(i == 0)
        def _():
            dw_ref[...] = jnp.zeros_like(dw_ref)
            db_ref[...] = jnp.zeros_like(db_ref)

        dvt = dv_ref[0:tm, :]
        for k in range(SSM_K):
            dw_ref[k:k + 1, :] += jnp.sum(
                dvt * ext_ref[PH - SSM_K + 1 + k:PH - SSM_K + 1 + k + tm, :], axis=0, keepdims=True)
        db_ref[...] += jnp.sum(dvt, axis=0, keepdims=True)

    return _pallas(
        body, name="xconv_bwd", grid=(XBC // CT, t // tm),
        in_specs=[pl.BlockSpec((tm, CT), lambda j, i: (i, XBC_BLK + j)),
                  pl.BlockSpec((PH, CT), lambda j, i: (jnp.maximum(i * hb - 1, 0), XBC_BLK + j)),
                  pl.BlockSpec((PH, CT), lambda j, i: (jnp.minimum((i + 1) * hb, nhb - 1), XBC_BLK + j)),
                  pl.BlockSpec((tm, CT), lambda j, i: (i, j)),
                  pl.BlockSpec((PH, CT), lambda j, i: (jnp.minimum((i + 1) * hb, nhb - 1), j)),
                  pl.BlockSpec((PH, CT), lambda j, i: (0, j)),
                  pl.BlockSpec((1, CT), lambda j, i: (0, j))],
        out_specs=[pl.BlockSpec((tm, CT), lambda j, i: (i, j)),
                   pl.BlockSpec((PH, CT), lambda j, i: (0, j)),
                   pl.BlockSpec((1, CT), lambda j, i: (0, j))],
        out_shape=[jax.ShapeDtypeStruct((t, XBC), BF16), jax.ShapeDtypeStruct((PH, XBC), F32),
                   jax.ShapeDtypeStruct((1, XBC), F32)],
        scratch_shapes=[pltpu.VMEM((tm + 2 * PH, CT), F32), pltpu.VMEM((te, CT), F32)],
        compiler_params=_params(2),
    )(proj, proj, proj, dxc, dxc, conv_w, conv_b)


def _ssd_common(xbc, dt, alog, ex):
    L = CHUNK
    a = -jnp.exp(alog)
    la = dt * a
    li = lax.broadcasted_iota(jnp.int32, (L, L), 0)
    si = lax.broadcasted_iota(jnp.int32, (L, L), 1)
    lower = si <= li
    tri = jnp.where(lower, 1.0, 0.0).astype(BF16)
    cs = _xdot(tri, la)
    cst = _dotx(la, tri, (((0,), (1,)), ((), ())))
    csl = cs[L - 1:L, :]
    ecs_x = _dotx(jnp.exp(cs)[:, 0:SSM_H], ex)
    tail_x = _dotx(jnp.exp(csl - cs)[:, 0:SSM_H], ex)
    dt_x = _dotx(dt[:, 0:SSM_H], ex)
    return a, la, lower, tri, cs, cst, ecs_x, tail_x, dt_x


def _ssd_fwd(xbc_c, dt, a_log, ex, nb, seq):
    t = xbc_c.shape[0]
    L = CHUNK
    nc = seq // L
    GW = SSM_R * SSM_P

    def body(x_ref, dt_ref, al_ref, ex_ref, y_ref, st_ref, state):
        c = pl.program_id(1)

        @pl.when(c == 0)
        def _():
            state[...] = jnp.zeros_like(state)

        st_ref[0] = state[...]
        xbc = x_ref[...]
        _, _, lower, _, cs, cst, ecs_x, tail_x, dt_x = _ssd_common(xbc, dt_ref[...], al_ref[...], ex_ref[...])
        xd = xbc[:, 0:D_INNER] * dt_x
        xdb = xd.astype(BF16)
        xt = (xd * tail_x).astype(BF16)
        el_x = ecs_x[L - 1:L, :]
        for g in range(SSM_G):
            bg = xbc[:, D_INNER + g * SSM_N:D_INNER + (g + 1) * SSM_N].astype(BF16)
            cg = xbc[:, D_INNER + (SSM_G + g) * SSM_N:D_INNER + (SSM_G + g + 1) * SSM_N].astype(BF16)
            cb = _dot(cg, bg, NT)
            sg = state[:, g * GW:(g + 1) * GW]
            ys = _dot(cg, sg) * ecs_x[:, g * GW:(g + 1) * GW]
            for r in range(SSM_R):
                h = g * SSM_R + r
                seg = cs[:, h:h + 1] - cst[h:h + 1, :]
                dec = jnp.exp(jnp.where(lower, seg, -1e30))
                yh = _dot(cb * dec, xdb[:, h * SSM_P:(h + 1) * SSM_P])
                y_ref[:, h * SSM_P:(h + 1) * SSM_P] = yh + ys[:, r * SSM_P:(r + 1) * SSM_P]
            state[:, g * GW:(g + 1) * GW] = sg * el_x[:, g * GW:(g + 1) * GW] + _dot(bg, xt[:, g * GW:(g + 1) * GW], TN)

    return _pallas(
        body, name="ssd_fwd", grid=(nb, nc),
        in_specs=[pl.BlockSpec((L, XBC), lambda b, c: (b * nc + c, 0)),
                  pl.BlockSpec((L, LANE), lambda b, c: (b * nc + c, 0)),
                  pl.BlockSpec((1, LANE), lambda b, c: (0, 0)),
                  pl.BlockSpec((SSM_H, D_INNER), lambda b, c: (0, 0))],
        out_specs=[pl.BlockSpec((L, D_INNER), lambda b, c: (b * nc + c, 0)),
                   pl.BlockSpec((1, SSM_N, D_INNER), lambda b, c: (b * nc + c, 0, 0))],
        out_shape=[jax.ShapeDtypeStruct((t, D_INNER), F32),
                   jax.ShapeDtypeStruct((nb * nc, SSM_N, D_INNER), F32)],
        scratch_shapes=[pltpu.VMEM((SSM_N, D_INNER), F32)], compiler_params=_params(2),
    )(xbc_c, dt, a_log, ex)


def _ssd_bwd(xbc_c, dt, a_log, ex, ext, states, dy, dskip, nb, seq):
    t = xbc_c.shape[0]
    L = CHUNK
    nc = seq // L
    GW = SSM_R * SSM_P

    def body(x_ref, dt_ref, al_ref, ex_ref, ext_ref, st_ref, dy_ref, sk_ref, dx_ref, ddt_ref, da_ref,
             dstate, dxd, yd, lastv):
        b = pl.program_id(0)
        c = pl.program_id(1)

        @pl.when(c == 0)
        def _():
            dstate[...] = jnp.zeros_like(dstate)

        @pl.when(jnp.logical_and(b == 0, c == 0))
        def _():
            da_ref[...] = jnp.zeros_like(da_ref)

        xbc = x_ref[...]
        dtv = dt_ref[...]
        ex_t = ext_ref[...]
        a, la, lower, tri, cs, cst, ecs_x, tail_x, dt_x = _ssd_common(xbc, dtv, al_ref[...], ex_ref[...])
        xs = xbc[:, 0:D_INNER]
        xd = xs * dt_x
        xdb = xd.astype(BF16)
        dyv = dy_ref[...]
        dyb = dyv.astype(BF16)
        dys = dyv * ecs_x
        xt = xd * tail_x
        el_x = ecs_x[L - 1:L, :]
        lane = lax.broadcasted_iota(jnp.int32, (L, LANE), 1)
        sub = lax.broadcasted_iota(jnp.int32, (LANE, L), 0)
        row_part = jnp.zeros((L, LANE), F32)
        col_part = jnp.zeros((LANE, L), F32)
        for g in range(SSM_G):
            gs = slice(g * GW, (g + 1) * GW)
            bcol = slice(D_INNER + g * SSM_N, D_INNER + (g + 1) * SSM_N)
            ccol = slice(D_INNER + (SSM_G + g) * SSM_N, D_INNER + (SSM_G + g + 1) * SSM_N)
            bg = xbc[:, bcol].astype(BF16)
            cg = xbc[:, ccol].astype(BF16)
            cb = _dot(cg, bg, NT)
            sg = st_ref[0, :, gs]
            dsg = dstate[:, gs]
            dc = _dot(dys[:, gs], sg, NT)
            db = _dot(xt[:, gs], dsg, NT)
            dx_state = tail_x[:, gs] * _dot(bg, dsg)
            yd[:, gs] = dys[:, gs] * _dot(cg, sg) - xd[:, gs] * dx_state
            s_out = sg * el_x[:, gs] + _dot(bg, xt[:, gs], TN)
            lastv[:, gs] = jnp.broadcast_to(jnp.sum(dsg * s_out, axis=0, keepdims=True), (8, GW))
            dcb = jnp.zeros((L, L), F32)
            for r in range(SSM_R):
                h = g * SSM_R + r
                hs = slice(h * SSM_P, (h + 1) * SSM_P)
                seg = cs[:, h:h + 1] - cst[h:h + 1, :]
                dec = jnp.exp(jnp.where(lower, seg, -1e30))
                m = cb * dec
                dm = _dot(dyb[:, hs], xdb[:, hs], NT)
                dcb = dcb + dm * dec
                e = dm * m
                row_part = row_part + jnp.where(lane == h, jnp.sum(e, axis=1, keepdims=True), 0.0)
                col_part = col_part + jnp.where(sub == h, jnp.sum(e, axis=0, keepdims=True), 0.0)
                dxd[:, hs] = _dot(m, dyb[:, hs], TN) + dx_state[:, r * SSM_P:(r + 1) * SSM_P]
            dx_ref[:, bcol] = db + _dot(dcb, cg, TN)
            dx_ref[:, ccol] = dc + _dot(dcb, bg)
            dstate[:, gs] = dsg * el_x[:, gs] + _dot(cg, dys[:, gs], TN)
        dxv = dxd[...]
        dx_ref[:, 0:D_INNER] = dxv * dt_x + sk_ref[...]
        ddt_x = _dotx(dxv * xs, ex_t)
        yst = _dotx(yd[...], ex_t)
        lst = _dotx(lastv[...], ex_t)[0:1, :]
        rows = lax.broadcasted_iota(jnp.int32, (L, LANE), 0)
        dcs = row_part - col_part.T + yst + jnp.where(rows == L - 1, lst, 0.0)
        li = lax.broadcasted_iota(jnp.int32, (L, L), 0)
        si = lax.broadcasted_iota(jnp.int32, (L, L), 1)
        upper = jnp.where(si >= li, 1.0, 0.0).astype(BF16)
        dla = _xdot(upper, dcs)
        ddt_ref[...] = dla * a + ddt_x
        da_ref[...] += jnp.sum(dla * dtv, axis=0, keepdims=True)

    def row(w):
        return pl.BlockSpec((L, w), lambda b, c: (b * nc + nc - 1 - c, 0))

    return _pallas(
        body, name="ssd_bwd", grid=(nb, nc),
        in_specs=[row(XBC), row(LANE), pl.BlockSpec((1, LANE), lambda b, c: (0, 0)),
                  pl.BlockSpec((SSM_H, D_INNER), lambda b, c: (0, 0)),
                  pl.BlockSpec((D_INNER, LANE), lambda b, c: (0, 0)),
                  pl.BlockSpec((1, SSM_N, D_INNER), lambda b, c: (b * nc + nc - 1 - c, 0, 0)),
                  row(D_INNER), row(D_INNER)],
        out_specs=[row(XBC), row(LANE), pl.BlockSpec((1, LANE), lambda b, c: (0, 0))],
        out_shape=[jax.ShapeDtypeStruct((t, XBC), F32), jax.ShapeDtypeStruct((t, LANE), F32),
                   jax.ShapeDtypeStruct((1, LANE), F32)],
        scratch_shapes=[pltpu.VMEM((SSM_N, D_INNER), F32), pltpu.VMEM((L, D_INNER), F32),
                        pltpu.VMEM((L, D_INNER), F32), pltpu.VMEM((8, D_INNER), F32)],
        compiler_params=_params(2),
    )(xbc_c, dt, a_log, ex, ext, states, dy, dskip)


def _group_rms(y2):
    gw = D_INNER // SSM_G
    parts = []
    for g in range(SSM_G):
        v = y2[:, g * gw:(g + 1) * gw]
        r = lax.rsqrt(jnp.mean(v * v, axis=1, keepdims=True) + EPS)
        parts.append(jnp.broadcast_to(r, v.shape))
    return jnp.concatenate(parts, axis=1)


def _gate_fwd(y, xbc_c, proj, d_x, gn_w, *, tm=256):
    t = y.shape[0]

    def body(y_ref, x_ref, z_ref, d_ref, w_ref, o_ref):
        y1 = y_ref[...] + d_ref[...] * x_ref[...]
        zv = z_ref[...]
        y2 = y1 * zv * _sigmoid(zv)
        o_ref[...] = (y2 * _group_rms(y2) * w_ref[...]).astype(BF16)

    row = pl.BlockSpec((tm, D_INNER), lambda i: (i, 0))
    vec = pl.BlockSpec((1, D_INNER), lambda i: (0, 0))
    return _pallas(
        body, name="gate_fwd", grid=(t // tm,), in_specs=[row, row, row, vec, vec], out_specs=row,
        out_shape=jax.ShapeDtypeStruct((t, D_INNER), BF16), compiler_params=_params(1),
    )(y, xbc_c, proj, d_x, gn_w)


def _gate_bwd(dyg, y, xbc_c, proj, d_x, gn_w, *, tm=256):
    t = y.shape[0]
    gw = D_INNER // SSM_G

    def body(dg_ref, y_ref, x_ref, z_ref, d_ref, w_ref, dy_ref, dsk_ref, dz_ref, dw_ref, dd_ref):
        i = pl.program_id(0)
        xv = x_ref[...]
        dxv = d_ref[...]
        y1 = y_ref[...] + dxv * xv
        zv = z_ref[...]
        sz = _sigmoid(zv)
        y2 = y1 * zv * sz
        rr = _group_rms(y2)
        xh = y2 * rr
        dg = dg_ref[...]
        gq = dg * w_ref[...]
        prod = gq * xh
        means = []
        for g in range(SSM_G):
            mg = jnp.mean(prod[:, g * gw:(g + 1) * gw], axis=1, keepdims=True)
            means.append(jnp.broadcast_to(mg, (tm, gw)))
        dy2 = rr * (gq - xh * jnp.concatenate(means, axis=1))
        dy1 = dy2 * zv * sz
        dy_ref[...] = dy1
        dsk_ref[...] = dy1 * dxv
        dz_ref[...] = (dy2 * y1 * _dsilu(zv, sz)).astype(BF16)

        @pl.when(i == 0)
        def _():
            dw_ref[...] = jnp.zeros_like(dw_ref)
            dd_ref[...] = jnp.zeros_like(dd_ref)

        dw_ref[...] += jnp.sum(dg * xh, axis=0, keepdims=True)
        dd_ref[...] += jnp.sum(dy1 * xv, axis=0, keepdims=True)

    row = pl.BlockSpec((tm, D_INNER), lambda i: (i, 0))
    vec = pl.BlockSpec((1, D_INNER), lambda i: (0, 0))
    return _pallas(
        body, name="gate_bwd", grid=(t // tm,), in_specs=[row, row, row, row, vec, vec],
        out_specs=[row, row, row, vec, vec],
        out_shape=[jax.ShapeDtypeStruct((t, D_INNER), F32), jax.ShapeDtypeStruct((t, D_INNER), F32),
                   jax.ShapeDtypeStruct((t, D_INNER), BF16), jax.ShapeDtypeStruct((1, D_INNER), F32),
                   jax.ShapeDtypeStruct((1, D_INNER), F32)],
        compiler_params=_params(1),
    )(dyg, y, xbc_c, proj, d_x, gn_w)


ANY = pl.BlockSpec(memory_space=pl.ANY)


def _remote(src, dst, sems, k, to):
    send_sems, recv_sems = sems
    return pltpu.make_async_remote_copy(src_ref=src, dst_ref=dst, send_sem=send_sems.at[k], recv_sem=recv_sems.at[k],
                                        device_id=to, device_id_type=MESH)


def _gather_shards(w_in0, w_in1, w_out0, w_out1, small):
    nchip = 4
    dm, n0 = w_in0.shape
    n1 = w_in1.shape[1]
    ro = w_out0.shape[0]
    hr, ho = dm // 2, ro // 2

    def body(a0, a1, b0, b1, sm, o0, o1, p0, p1, osm, ici_s, ici_r, d2d_s, d2d_r, local_sems):
        x, y, c = lax.axis_index("x"), lax.axis_index("y"), lax.axis_index("c")
        me = 2 * x + y
        sib = (x, y, 1 - c)
        peers = [(1 - x, y), (x, 1 - y), (1 - x, 1 - y)]

        def region(chip, half):
            col = pl.multiple_of(chip * n0, LANE)
            return [o0.at[pl.ds(half * hr, hr), pl.ds(col, n0)], o1.at[chip, pl.ds(half * hr, hr), :],
                    p0.at[pl.ds(chip * ro + half * ho, ho), :], p1.at[pl.ds(chip * ro + half * ho, ho), :]]

        own = [pltpu.make_async_copy(a0, o0.at[:, pl.ds(pl.multiple_of(me * n0, LANE), n0)], local_sems.at[0]),
               pltpu.make_async_copy(a1, o1.at[me], local_sems.at[1]),
               pltpu.make_async_copy(b0, p0.at[pl.ds(me * ro, ro), :], local_sems.at[2]),
               pltpu.make_async_copy(b1, p1.at[pl.ds(me * ro, ro), :], local_sems.at[3]),
               pltpu.make_async_copy(sm, osm.at[me], local_sems.at[4])]
        for cp in own:
            cp.start()
        halves = [a0.at[pl.ds(c * hr, hr), :], a1.at[pl.ds(c * hr, hr), :],
                  b0.at[pl.ds(c * ho, ho), :], b1.at[pl.ds(c * ho, ho), :]]
        sends = []
        for k, (px, py) in enumerate(peers):
            to = (px, py, c)
            for j, (s, d) in enumerate(zip(halves, region(me, c))):
                sends.append(_remote(s, d, (ici_s, ici_r), 5 * k + j, to))
            sends.append(_remote(sm, osm.at[me], (ici_s, ici_r), 5 * k + 4, to))
        for cp in sends:
            cp.start()
        for k, (px, py) in enumerate(peers):
            q = 2 * px + py
            for j, d in enumerate(region(q, c)):
                _remote(d, d, (ici_s, ici_r), 5 * k + j, (px, py, c)).wait_recv()
                fwd = _remote(d, d, (d2d_s, d2d_r), 4 * k + j, sib)
                fwd.start()
                sends.append(fwd)
            _remote(sm, osm.at[q], (ici_s, ici_r), 5 * k + 4, (px, py, c)).wait_recv()
        for k, (px, py) in enumerate(peers):
            for j, d in enumerate(region(2 * px + py, 1 - c)):
                _remote(d, d, (d2d_s, d2d_r), 4 * k + j, sib).wait_recv()
        for cp in sends:
            cp.wait_send()
        for cp in own:
            cp.wait()

    return _pallas(
        body, name="gather_shards", in_specs=[ANY] * 5, out_specs=[ANY] * 5,
        out_shape=[jax.ShapeDtypeStruct((dm, nchip * n0), w_in0.dtype),
                   jax.ShapeDtypeStruct((nchip, dm, n1), w_in1.dtype),
                   jax.ShapeDtypeStruct((nchip * ro, w_out0.shape[1]), w_out0.dtype),
                   jax.ShapeDtypeStruct((nchip * ro, w_out1.shape[1]), w_out1.dtype),
                   jax.ShapeDtypeStruct((nchip,) + small.shape, small.dtype)],
        scratch_shapes=[pltpu.SemaphoreType.DMA((15,)), pltpu.SemaphoreType.DMA((15,)),
                        pltpu.SemaphoreType.DMA((12,)), pltpu.SemaphoreType.DMA((12,)),
                        pltpu.SemaphoreType.DMA((5,))],
    )(w_in0, w_in1, w_out0, w_out1, small)


def _pair_exchange(g_in0, g_in1, g_out0, g_out1, gsmall, grep):
    dm = g_in0.shape[0]
    hr = dm // 2
    hc = g_out0.shape[1] // 2

    def body(a0, a1, b0, b1, sm, rp, q0, q1, r0, r1, osm, orp, pair_s, pair_r, send_sems, recv_sems, local_sems):
        x, y, c = lax.axis_index("x"), lax.axis_index("y"), lax.axis_index("c")
        me = 4 * x + 2 * y + c
        chip = 2 * x + y
        sib = (x, y, 1 - c)
        rows = pl.ds(pl.multiple_of((1 - c) * hr, 8), hr)
        cols = pl.ds(pl.multiple_of((1 - c) * hc, LANE), hc)
        pair = [_remote(a0.at[rows, :], q0, (pair_s, pair_r), 0, sib),
                _remote(a1.at[:, rows, :], q1, (pair_s, pair_r), 1, sib),
                _remote(b0.at[:, cols], r0, (pair_s, pair_r), 2, sib),
                _remote(b1.at[:, cols], r1, (pair_s, pair_r), 3, sib)]
        for cp in pair:
            cp.start()
        own = [pltpu.make_async_copy(sm.at[chip], osm.at[me], local_sems.at[0]),
               pltpu.make_async_copy(rp, orp.at[me], local_sems.at[1])]
        for cp in own:
            cp.start()
        peers = []
        for k in range(7):
            fx, fy, fc = ((k + 1) >> 2) & 1, ((k + 1) >> 1) & 1, (k + 1) & 1
            peers.append((1 - x if fx else x, 1 - y if fy else y, 1 - c if fc else c))
        sends = []
        for k, (px, py, pc) in enumerate(peers):
            sends.append(_remote(sm.at[2 * px + py], osm.at[me], (send_sems, recv_sems), 2 * k, (px, py, pc)))
            sends.append(_remote(rp, orp.at[me], (send_sems, recv_sems), 2 * k + 1, (px, py, pc)))
        for cp in sends:
            cp.start()
        for k, (px, py, pc) in enumerate(peers):
            slot = 4 * px + 2 * py + pc
            _remote(sm.at[chip], osm.at[slot], (send_sems, recv_sems), 2 * k, (px, py, pc)).wait_recv()
            _remote(rp, orp.at[slot], (send_sems, recv_sems), 2 * k + 1, (px, py, pc)).wait_recv()
        for cp in pair:
            cp.wait_recv()
        for cp in pair + sends:
            cp.wait_send()
        for cp in own:
            cp.wait()

    return _pallas(
        body, name="pair_exchange", in_specs=[ANY] * 6, out_specs=[ANY] * 6,
        out_shape=[jax.ShapeDtypeStruct((hr, g_in0.shape[1]), F32),
                   jax.ShapeDtypeStruct((g_in1.shape[0], hr, g_in1.shape[2]), F32),
                   jax.ShapeDtypeStruct((g_out0.shape[0], hc), F32),
                   jax.ShapeDtypeStruct((g_out1.shape[0], hc), F32),
                   jax.ShapeDtypeStruct((8,) + gsmall.shape[1:], F32),
                   jax.ShapeDtypeStruct((8,) + grep.shape, F32)],
        scratch_shapes=[pltpu.SemaphoreType.DMA((4,)), pltpu.SemaphoreType.DMA((4,)),
                        pltpu.SemaphoreType.DMA((14,)), pltpu.SemaphoreType.DMA((14,)),
                        pltpu.SemaphoreType.DMA((2,))],
    )(g_in0, g_in1, g_out0, g_out1, gsmall, grep)


def _core_index():
    return lax.axis_index("c").astype(jnp.int32).reshape(1)


def _half_add(full, other, *, axis, block, name):
    nd = full.ndim
    nblk = other.shape[axis] // block[axis]
    grid = tuple(other.shape[d] // block[d] for d in range(nd))

    def body(c_ref, f_ref, o_ref, out_ref):
        out_ref[...] = (f_ref[...] + o_ref[...]).astype(BF16)

    def full_map(*idx):
        ids, c_ref = list(idx[:nd]), idx[nd]
        ids[axis] = ids[axis] + c_ref[0] * nblk
        return tuple(ids)

    def plain_map(*idx):
        return tuple(idx[:nd])

    return _pallas(
        body, name=name,
        grid_spec=pltpu.PrefetchScalarGridSpec(
            num_scalar_prefetch=1, grid=grid,
            in_specs=[pl.BlockSpec(block, full_map), pl.BlockSpec(block, plain_map)],
            out_specs=pl.BlockSpec(block, plain_map)),
        out_shape=jax.ShapeDtypeStruct(other.shape, BF16), compiler_params=_params(nd),
    )(_core_index(), full, other)


def _chip_exchange(s_in0, s_in1, s_out0, s_out1):
    nchip = 4
    n0 = s_in0.shape[1] // nchip
    ro = s_out0.shape[0] // nchip

    def body(a0, a1, b0, b1, l0, l1, m0, m1, send_sems, recv_sems, local_sems):
        x, y, c = lax.axis_index("x"), lax.axis_index("y"), lax.axis_index("c")
        me = 2 * x + y
        peers = [(1 - x, y), (x, 1 - y), (1 - x, 1 - y)]

        def pieces(chip):
            return [a0.at[:, pl.ds(pl.multiple_of(chip * n0, LANE), n0)], a1.at[chip],
                    b0.at[pl.ds(chip * ro, ro), :], b1.at[pl.ds(chip * ro, ro), :]]

        def slots(chip):
            return [l0.at[chip], l1.at[chip], m0.at[chip], m1.at[chip]]

        own = [pltpu.make_async_copy(s, d, local_sems.at[j]) for j, (s, d) in enumerate(zip(pieces(me), slots(me)))]
        for cp in own:
            cp.start()
        sends = []
        for k, (px, py) in enumerate(peers):
            for j, (s, d) in enumerate(zip(pieces(2 * px + py), slots(me))):
                sends.append(_remote(s, d, (send_sems, recv_sems), 4 * k + j, (px, py, c)))
        for cp in sends:
            cp.start()
        for k, (px, py) in enumerate(peers):
            for j, (s, d) in enumerate(zip(pieces(me), slots(2 * px + py))):
                _remote(s, d, (send_sems, recv_sems), 4 * k + j, (px, py, c)).wait_recv()
        for cp in sends:
            cp.wait_send()
        for cp in own:
            cp.wait()

    return _pallas(
        body, name="chip_exchange", in_specs=[ANY] * 4, out_specs=[ANY] * 4,
        out_shape=[jax.ShapeDtypeStruct((nchip, s_in0.shape[0], n0), BF16),
                   jax.ShapeDtypeStruct((nchip,) + s_in1.shape[1:], BF16),
                   jax.ShapeDtypeStruct((nchip, ro, s_out0.shape[1]), BF16),
                   jax.ShapeDtypeStruct((nchip, ro, s_out1.shape[1]), BF16)],
        scratch_shapes=[pltpu.SemaphoreType.DMA((12,)), pltpu.SemaphoreType.DMA((12,)),
                        pltpu.SemaphoreType.DMA((4,))],
    )(s_in0, s_in1, s_out0, s_out1)


def _chip_sum(slots, *, block, name):
    nchip = slots.shape[0]
    shape = slots.shape[1:]
    grid = (shape[0] // block[0], shape[1] // block[1])

    def body(s_ref, o_ref):
        acc = s_ref[0].astype(F32)
        for q in range(1, nchip):
            acc = acc + s_ref[q].astype(F32)
        o_ref[...] = acc

    return _pallas(
        body, name=name, grid=grid,
        in_specs=[pl.BlockSpec((nchip,) + block, lambda i, j: (0, i, j))],
        out_specs=pl.BlockSpec(block, lambda i, j: (i, j)),
        out_shape=jax.ShapeDtypeStruct(shape, F32), compiler_params=_params(2),
    )(slots)


def _pair_share(r_in0, r_in1, r_out0, r_out1):
    hr = r_in0.shape[0]
    hc = r_out0.shape[1]

    def body(a0, a1, b0, b1, g0, g1, h0, h1, send_sems, recv_sems, local_sems):
        x, y, c = lax.axis_index("x"), lax.axis_index("y"), lax.axis_index("c")
        sib = (x, y, 1 - c)

        def halves(k):
            rows = pl.ds(pl.multiple_of(k * hr, 8), hr)
            cols = pl.ds(pl.multiple_of(k * hc, LANE), hc)
            return [g0.at[rows, :], g1.at[rows, :], h0.at[:, cols], h1.at[:, cols]]

        srcs = [a0, a1, b0, b1]
        own = [pltpu.make_async_copy(s, d, local_sems.at[j]) for j, (s, d) in enumerate(zip(srcs, halves(c)))]
        sends = [_remote(s, d, (send_sems, recv_sems), j, sib) for j, (s, d) in enumerate(zip(srcs, halves(c)))]
        for cp in own + sends:
            cp.start()
        for j, (s, d) in enumerate(zip(srcs, halves(1 - c))):
            _remote(s, d, (send_sems, recv_sems), j, sib).wait_recv()
        for cp in sends:
            cp.wait_send()
        for cp in own:
            cp.wait()

    return _pallas(
        body, name="pair_share", in_specs=[ANY] * 4, out_specs=[ANY] * 4,
        out_shape=[jax.ShapeDtypeStruct((2 * hr, r_in0.shape[1]), F32),
                   jax.ShapeDtypeStruct((2 * hr, r_in1.shape[1]), F32),
                   jax.ShapeDtypeStruct((r_out0.shape[0], 2 * hc), F32),
                   jax.ShapeDtypeStruct((r_out1.shape[0], 2 * hc), F32)],
        scratch_shapes=[pltpu.SemaphoreType.DMA((4,)), pltpu.SemaphoreType.DMA((4,)),
                        pltpu.SemaphoreType.DMA((4,))],
    )(r_in0, r_in1, r_out0, r_out1)


def _adam_math(g, w, m, v):
    c1 = 1.0 - ADAM_B1 ** ADAM_STEP
    c2 = 1.0 - ADAM_B2 ** ADAM_STEP
    m2 = ADAM_B1 * m + (1.0 - ADAM_B1) * g
    v2 = ADAM_B2 * v + (1.0 - ADAM_B2) * (g * g)
    delta = -ADAM_LR * ((m2 / c1) / (jnp.sqrt(v2 / c2) + ADAM_EPS) + ADAM_WD * w)
    return delta, m2, v2


def _adamw_nat(g, w, m, v, *, name, tr):
    rows, cw = w.shape

    def body(g_ref, w_ref, m_ref, v_ref, go_ref, d_ref, nm_ref, nv_ref):
        gv = g_ref[...][:, 0:cw]
        delta, m2, v2 = _adam_math(gv, w_ref[...], m_ref[...], v_ref[...])
        go_ref[...] = gv
        d_ref[...] = delta
        nm_ref[...] = m2
        nv_ref[...] = v2

    row = pl.BlockSpec((tr, cw), lambda i: (i, 0))
    out = jax.ShapeDtypeStruct((rows, cw), F32)
    return _pallas(
        body, name=name, grid=(rows // tr,),
        in_specs=[pl.BlockSpec((tr, g.shape[1]), lambda i: (i, 0)), row, row, row],
        out_specs=[row, row, row, row], out_shape=[out, out, out, out], compiler_params=_params(1),
    )(g, w, m, v)


def _adamw(slots, w, m, v, *, name, tr):
    nd, rows, _ = slots.shape
    c1 = 1.0 - ADAM_B1 ** ADAM_STEP
    c2 = 1.0 - ADAM_B2 ** ADAM_STEP

    def body(s_ref, w_ref, m_ref, v_ref, g_ref, d_ref, nm_ref, nv_ref):
        g = s_ref[0]
        for d in range(1, nd):
            g = g + s_ref[d]
        m2 = ADAM_B1 * m_ref[...] + (1.0 - ADAM_B1) * g
        v2 = ADAM_B2 * v_ref[...] + (1.0 - ADAM_B2) * (g * g)
        g_ref[...] = g
        nm_ref[...] = m2
        nv_ref[...] = v2
        d_ref[...] = -ADAM_LR * ((m2 / c1) / (jnp.sqrt(v2 / c2) + ADAM_EPS) + ADAM_WD * w_ref[...])

    row = pl.BlockSpec((tr, LANE), lambda i: (i, 0))
    out = jax.ShapeDtypeStruct((rows, LANE), F32)
    return _pallas(
        body, name=name, grid=(rows // tr,),
        in_specs=[pl.BlockSpec((nd, tr, LANE), lambda i: (0, i, 0)), row, row, row],
        out_specs=[row, row, row, row], out_shape=[out, out, out, out], compiler_params=_params(1),
    )(slots, w, m, v)


def _rows(a):
    return a.reshape(-1, LANE)


def _pad_rows(a, mult):
    pad = (-a.shape[0]) % mult
    return jnp.pad(a, ((0, pad), (0, 0))) if pad else a


def _pack(parts, mult):
    return _pad_rows(jnp.concatenate([_rows(p) for p in parts], axis=0), mult)


def _unpack(slab, shapes):
    out, r0 = [], 0
    for shp in shapes:
        n = 1
        for s in shp:
            n *= s
        r = n // LANE
        out.append(slab[r0:r0 + r].reshape(shp))
        r0 += r
    return out


def _pack_rep(vecs, scal):
    srow = jnp.concatenate([s.reshape(-1) for s in scal] + [jnp.zeros((LANE - 3 * SSM_H,), F32)]).reshape(1, LANE)
    return _pad_rows(jnp.concatenate([_rows(vv) for vv in vecs] + [srow], axis=0), 8)


def _unpack_rep(slab, vec_shapes, scal_shape):
    vecs, r0 = [], 0
    for shp in vec_shapes:
        vecs.append(slab[r0:r0 + 8].reshape(shp))
        r0 += 8
    srow = slab[r0]
    scal = [srow[i * SSM_H:(i + 1) * SSM_H].reshape(scal_shape) for i in range(3)]
    return vecs, scal


def kernel(x, ev_norm_w, ev_w_in, ev_dw_w, ev_dw_b, ev_ln_w, ev_ln_b, ev_w_out, od_norm_w, od_w_in, od_conv_w, od_conv_b, od_dt_bias, od_a_log, od_d, od_gnorm_w, od_w_out, final_norm_w, loss_target, m_ev_norm_w, m_ev_w_in, m_ev_dw_w, m_ev_dw_b, m_ev_ln_w, m_ev_ln_b, m_ev_w_out, m_od_norm_w, m_od_w_in, m_od_conv_w, m_od_conv_b, m_od_dt_bias, m_od_a_log, m_od_d, m_od_gnorm_w, m_od_w_out, m_final_norm_w, v_ev_norm_w, v_ev_w_in, v_ev_dw_w, v_ev_dw_b, v_ev_ln_w, v_ev_ln_b, v_ev_w_out, v_od_norm_w, v_od_w_in, v_od_conv_w, v_od_conv_b, v_od_dt_bias, v_od_a_log, v_od_d, v_od_gnorm_w, v_od_w_out, v_final_norm_w):
    nb, seq, d = x.shape
    t = nb * seq
    nchip = 4
    xf = x.reshape(t, d)
    tgt = loss_target.reshape(t, d)

    big_w = [ev_w_in[0], od_w_in[0], ev_w_out[0], od_w_out[0]]
    small_w = [ev_dw_w[0], od_norm_w[0], od_conv_w[0], od_conv_b[0], od_gnorm_w[0]]
    small_shapes = [a.shape for a in small_w]
    w_in0, w_in1g, w_out0, w_out1, gath_small = _gather_shards(*[a.astype(BF16) for a in big_w], _pack(small_w, 8))
    per_chip = [_unpack(gath_small[p], small_shapes) for p in range(nchip)]

    def cat(idx, axis):
        return jnp.concatenate([per_chip[p][idx] for p in range(nchip)], axis=axis)

    w_in1 = jnp.pad(jnp.concatenate([w_in1g[p] for p in range(nchip)], axis=1),
                    ((0, 0), (0, IN_ODD_PAD - IN_ODD)))
    dw_w = jnp.pad(cat(0, 1), ((0, HALO - CONF_K), (0, 0)))
    n1_w = cat(1, 0).reshape(1, d)
    conv_w = jnp.pad(cat(2, 1), ((0, PH - SSM_K), (0, 0)))
    conv_b = cat(3, 0).reshape(1, XBC)
    gn_w = cat(4, 0).reshape(1, D_INNER)

    def lanes(a):
        return jnp.pad(a.reshape(1, -1), ((0, 0), (0, LANE - a.size)))

    dt_bias, a_log = lanes(od_dt_bias), lanes(od_a_log)
    d_x = jnp.repeat(od_d.reshape(-1), SSM_P).reshape(1, D_INNER)
    hid = lax.broadcasted_iota(jnp.int32, (SSM_H, D_INNER), 1) // SSM_P
    ex = (hid == lax.broadcasted_iota(jnp.int32, (SSM_H, D_INNER), 0)).astype(BF16)
    ex_t = jnp.pad(ex.T, ((0, 0), (0, LANE - SSM_H)))
    fn_w = final_norm_w.reshape(1, d)

    n0 = _rms_fwd(xf, ev_norm_w, name="rms_fwd0")
    proj0 = _matmul(n0, w_in0, mode="nn", out_dtype=F32, bm=512, bn=1024, bk=d, name="in_proj0")
    y_conv = _conf_fwd(proj0, dw_w, ev_dw_b, ev_ln_w, ev_ln_b, seq)
    o_att, y_att = _sba_fwd(proj0, nb, seq)
    ycat0 = jnp.concatenate([y_conv, y_att], axis=1)
    h1 = _matmul(ycat0, w_out0, mode="nn", out_dtype=F32, bm=512, bn=d, bk=D_INNER, name="out_proj0", residual=xf)
    n1 = _rms_fwd(h1, n1_w, name="rms_fwd1")
    proj1 = _matmul(n1, w_in1, mode="nn", out_dtype=F32, bm=512, bn=768, bk=d, name="in_proj1")
    xbc_c = _xconv_fwd(proj1, conv_w, conv_b, seq)
    dt = _dt_fwd(proj1, dt_bias)
    y_ssd, states = _ssd_fwd(xbc_c, dt, a_log, ex, nb, seq)
    yg = _gate_fwd(y_ssd, xbc_c, proj1, d_x, gn_w)
    h2 = _matmul(yg, w_out1, mode="nn", out_dtype=F32, bm=512, bn=d, bk=D_INNER, name="out_proj1", residual=h1)
    dh2, g_fn, loss_part = _final_loss(h2, fn_w, tgt)

    dyg = _matmul(dh2, w_out1, mode="nt", out_dtype=F32, bm=512, bn=1024, bk=d, name="d_out_proj1")
    g_w_out1 = _matmul(yg, dh2, mode="tn", out_dtype=F32, bm=1024, bn=d, bk=512, name="dw_out_proj1")
    dy_ssd, dskip, dz, g_gn, g_dx = _gate_bwd(dyg, y_ssd, xbc_c, proj1, d_x, gn_w)
    dxbc_c, ddt, g_a = _ssd_bwd(xbc_c, dt, a_log, ex, ex_t, states, dy_ssd, dskip, nb, seq)
    dxbc, g_conv_w, g_conv_b = _xconv_bwd(proj1, dxbc_c, conv_w, conv_b, seq)
    ddt_raw, g_dt_bias = _dt_bwd(proj1, dt_bias, ddt)
    dproj1 = jnp.concatenate([dz, dxbc, ddt_raw.astype(BF16),
                              jnp.zeros((t, IN_ODD_PAD - IN_ODD - (LANE - SSM_H)), BF16)], axis=1)
    dn1 = _matmul(dproj1, w_in1, mode="nt", out_dtype=F32, bm=512, bn=d, bk=1792, name="d_in_proj1")
    g_w_in1 = _matmul(n1, dproj1, mode="tn", out_dtype=F32, bm=d, bn=768, bk=512, name="dw_in_proj1")
    dh1, g_n1 = _rms_bwd(dn1, h1, n1_w, dh2, name="rms_bwd1")

    dycat0 = _matmul(dh1, w_out0, mode="nt", out_dtype=F32, bm=512, bn=1024, bk=d, name="d_out_proj0")
    g_w_out0 = _matmul(ycat0, dh1, mode="tn", out_dtype=F32, bm=1024, bn=d, bk=512, name="dw_out_proj0")
    dq, dk, dv, dga = _sba_bwd(proj0, o_att, dycat0, nb, seq)
    dpc, g_dw_w, g_dw_b, g_ln_w, g_ln_b = _conf_bwd(proj0, dycat0, dw_w, ev_dw_b, ev_ln_w, ev_ln_b, seq)
    dproj0 = jnp.concatenate([dpc, dq, dk.astype(BF16), dv.astype(BF16), dga], axis=1)
    dn0 = _matmul(dproj0, w_in0, mode="nt", out_dtype=F32, bm=512, bn=d, bk=1792, name="d_in_proj0")
    g_w_in0 = _matmul(n0, dproj0, mode="tn", out_dtype=F32, bm=d, bn=1024, bk=512, name="dw_in_proj0")
    grad_x, g_n0 = _rms_bwd(dn0, xf, ev_norm_w, dh1, name="rms_bwd0")

    g_dw_w = g_dw_w[0:CONF_K]
    g_conv_w = g_conv_w[0:SSM_K]
    a_neg = -jnp.exp(od_a_log.reshape(-1))
    g_a_log = g_a[0, 0:SSM_H] * a_neg
    g_d = g_dx.reshape(SSM_H, SSM_P).sum(axis=1)
    n1 = IN_ODD // nchip
    n1p = -(-n1 // LANE) * LANE
    g_w_in1c = jnp.stack([jnp.pad(g_w_in1[:, p * n1:(p + 1) * n1], ((0, 0), (0, n1p - n1))) for p in range(nchip)])

    def chip_slab_small(p):
        c0, c1, c2, c3 = CONV_W // nchip, d // nchip, XBC // nchip, D_INNER // nchip
        return _pack([g_dw_w[:, p * c0:(p + 1) * c0], g_n1[0, p * c1:(p + 1) * c1],
                      g_conv_w[:, p * c2:(p + 1) * c2], g_conv_b[0, p * c2:(p + 1) * c2],
                      g_gn[0, p * c3:(p + 1) * c3]], 8)

    gsmall = jnp.stack([chip_slab_small(p) for p in range(nchip)])
    rep_vec_shapes = [ev_norm_w.shape, ev_dw_b.shape, ev_ln_w.shape, ev_ln_b.shape, final_norm_w.shape]
    grep = _pack_rep([g_n0, g_dw_b, g_ln_w, g_ln_b, g_fn], [g_dt_bias[0, 0:SSM_H], g_a_log, g_d])

    q_in0, q_in1, q_out0, q_out1, ssmall, srep = _pair_exchange(g_w_in0, g_w_in1c, g_w_out0, g_w_out1, gsmall, grep)
    s_in0 = _half_add(g_w_in0, q_in0, axis=0, block=(128, IN_EVEN), name="half_add_in0")
    s_in1 = _half_add(g_w_in1c, q_in1, axis=1, block=(1, 256, n1p), name="half_add_in1")
    s_out0 = _half_add(g_w_out0, q_out0, axis=1, block=(512, d // 2), name="half_add_out0")
    s_out1 = _half_add(g_w_out1, q_out1, axis=1, block=(512, d // 2), name="half_add_out1")
    l_in0, l_in1, l_out0, l_out1 = _chip_exchange(s_in0, s_in1, s_out0, s_out1)
    r_in0 = _chip_sum(l_in0, block=(128, IN_EVEN // nchip), name="chip_sum_in0")
    r_in1 = _chip_sum(l_in1, block=(256, n1p), name="chip_sum_in1")
    r_out0 = _chip_sum(l_out0, block=(512, d // 2), name="chip_sum_out0")
    r_out1 = _chip_sum(l_out1, block=(512, d // 2), name="chip_sum_out1")
    big_g = _pair_share(r_in0, r_in1, r_out0, r_out1)

    big_m = [m_ev_w_in[0], m_od_w_in[0], m_ev_w_out[0], m_od_w_out[0]]
    big_v = [v_ev_w_in[0], v_od_w_in[0], v_ev_w_out[0], v_od_w_out[0]]
    big_names = ["adamw_in0", "adamw_in1", "adamw_out0", "adamw_out1"]
    out_bigs = [_adamw_nat(g, w, m, v, name=nm, tr=128)
                for g, w, m, v, nm in zip(big_g, big_w, big_m, big_v, big_names)]

    def upd(slots, ws, ms, vs, packer, name, tr):
        return _adamw(slots, packer(ws), packer(ms), packer(vs), name=name, tr=tr)

    small_m = [m_ev_dw_w[0], m_od_norm_w[0], m_od_conv_w[0], m_od_conv_b[0], m_od_gnorm_w[0]]
    small_v = [v_ev_dw_w[0], v_od_norm_w[0], v_od_conv_w[0], v_od_conv_b[0], v_od_gnorm_w[0]]
    out_small = upd(ssmall, small_w, small_m, small_v, lambda a: _pack(a, 8), "adamw_small", ssmall.shape[1])

    def rep_pack(a):
        return _pack_rep(a[0:5], a[5:8])

    rep_w = [ev_norm_w, ev_dw_b, ev_ln_w, ev_ln_b, final_norm_w, od_dt_bias, od_a_log, od_d]
    rep_m = [m_ev_norm_w, m_ev_dw_b, m_ev_ln_w, m_ev_ln_b, m_final_norm_w, m_od_dt_bias, m_od_a_log, m_od_d]
    rep_v = [v_ev_norm_w, v_ev_dw_b, v_ev_ln_w, v_ev_ln_b, v_final_norm_w, v_od_dt_bias, v_od_a_log, v_od_d]
    out_rep = upd(srep, rep_w, rep_m, rep_v, rep_pack, "adamw_rep", srep.shape[1])

    results = []
    for kind in range(4):
        bw = [o[kind].reshape((1,) + o[kind].shape) for o in out_bigs]
        sw = _unpack(out_small[kind], small_shapes)
        vecs, scal = _unpack_rep(out_rep[kind], rep_vec_shapes, od_dt_bias.shape)
        results.append([
            vecs[0], bw[0], sw[0].reshape(ev_dw_w.shape), vecs[1], vecs[2], vecs[3], bw[2],
            sw[1].reshape(od_norm_w.shape), bw[1], sw[2].reshape(od_conv_w.shape), sw[3].reshape(od_conv_b.shape),
            scal[0], scal[1], scal[2], sw[4].reshape(od_gnorm_w.shape), bw[3], vecs[4]])
    loss = lax.psum(loss_part[0, 0], ("x", "y", "c"))
    return (loss, grad_x.reshape(x.shape), *results[0], *results[1], *results[2], *results[3])
```

```python
import jax
import jax.numpy as jnp
from jax import lax
from jax.experimental import pallas as pl
from jax.experimental.pallas import tpu as pltpu

F32 = jnp.float32
BF16 = jnp.bfloat16

D_MODEL = 1024
CONV_W = 1024
ATT_W = 1024
HEAD_DIM = 128
N_HEADS = 8
CONF_K = 31
IN_EVEN = 7168
D_INNER = 2048
SSM_P = 64
SSM_H = 32
SSM_G = 4
SSM_R = SSM_H // SSM_G
SSM_N = 128
SSM_K = 4
CHUNK = 128
XBC = D_INNER + 2 * SSM_G * SSM_N
IN_ODD = D_INNER + XBC + SSM_H
IN_ODD_PAD = 5376
EPS = 1e-6
QB = 128
NEG_CUT = -100.0

ADAM_LR = 0.001
ADAM_B1 = 0.9
ADAM_B2 = 0.999
ADAM_EPS = 1e-08
ADAM_WD = 0.01
ADAM_STEP = 10

LANE = 128
VMEM_LIMIT = 56 * 1024 * 1024
MESH = pl.DeviceIdType.MESH

NN = (((1,), (0,)), ((), ()))
NT = (((1,), (1,)), ((), ()))
TN = (((0,), (0,)), ((), ()))


def _pallas(body, **kw):
    return pl.pallas_call(body, **kw)


def _params(n_axes):
    return pltpu.CompilerParams(dimension_semantics=("arbitrary",) * n_axes, vmem_limit_bytes=VMEM_LIMIT)


def _dot(a, b, dims=NN):
    return lax.dot_general(a.astype(BF16), b.astype(BF16), dims, preferred_element_type=F32)


def _parts(x):
    h = x.astype(BF16)
    r = x - h.astype(F32)
    m = r.astype(BF16)
    l = (r - m.astype(F32)).astype(BF16)
    return (h, m, l)


def _dotx(x, e01, dims=NN):
    acc = None
    for p in _parts(x):
        t = lax.dot_general(p, e01, dims, preferred_element_type=F32)
        acc = t if acc is None else acc + t
    return acc


def _dotx2(x, e01, dims=NN):
    h = x.astype(BF16)
    l = (x - h.astype(F32)).astype(BF16)
    return (lax.dot_general(h, e01, dims, preferred_element_type=F32)
            + lax.dot_general(l, e01, dims, preferred_element_type=F32))


def _xdot(e01, x, dims=NN):
    acc = None
    for p in _parts(x):
        t = lax.dot_general(e01, p, dims, preferred_element_type=F32)
        acc = t if acc is None else acc + t
    return acc


def _sigmoid(x):
    return 1.0 / (1.0 + jnp.exp(-x))


def _dsilu(x, s):
    return s * (1.0 + x * (1.0 - s))


def _matmul(a, b, *, mode, out_dtype, bm, bn, bk, name, residual=None):
    if mode == "nn":
        (m, k), n = a.shape, b.shape[1]
        a_spec = pl.BlockSpec((bm, bk), lambda i, j, kk: (i, kk))
        b_spec = pl.BlockSpec((bk, bn), lambda i, j, kk: (kk, j))
        dims = NN
    elif mode == "nt":
        (m, k), n = a.shape, b.shape[0]
        a_spec = pl.BlockSpec((bm, bk), lambda i, j, kk: (i, kk))
        b_spec = pl.BlockSpec((bn, bk), lambda i, j, kk: (j, kk))
        dims = NT
    else:
        (k, m), n = a.shape, b.shape[1]
        a_spec = pl.BlockSpec((bk, bm), lambda i, j, kk: (kk, i))
        b_spec = pl.BlockSpec((bk, bn), lambda i, j, kk: (kk, j))
        dims = TN
    assert m % bm == 0 and n % bn == 0 and k % bk == 0, (name, m, n, k)
    nk = k // bk
    has_res = residual is not None

    def body(*refs):
        if has_res:
            a_ref, b_ref, r_ref, o_ref, acc_ref = refs
        else:
            a_ref, b_ref, o_ref, acc_ref = refs
        kk = pl.program_id(2)

        @pl.when(kk == 0)
        def _():
            acc_ref[...] = jnp.zeros_like(acc_ref)

        acc_ref[...] += _dot(a_ref[...], b_ref[...], dims)

        @pl.when(kk == nk - 1)
        def _():
            r = acc_ref[...]
            if has_res:
                r = r + r_ref[...]
            o_ref[...] = r.astype(out_dtype)

    in_specs = [a_spec, b_spec]
    args = [a, b]
    if has_res:
        in_specs.append(pl.BlockSpec((bm, bn), lambda i, j, kk: (i, j)))
        args.append(residual)
    return _pallas(
        body, name=name, grid=(m // bm, n // bn, nk), in_specs=in_specs,
        out_specs=pl.BlockSpec((bm, bn), lambda i, j, kk: (i, j)),
        out_shape=jax.ShapeDtypeStruct((m, n), out_dtype),
        scratch_shapes=[pltpu.VMEM((bm, bn), F32)], compiler_params=_params(3),
    )(*args)


def _rms_fwd(x, w, *, name, tm=512):
    t, d = x.shape

    def body(x_ref, w_ref, o_ref):
        xv = x_ref[...]
        r = lax.rsqrt(jnp.mean(xv * xv, axis=1, keepdims=True) + EPS)
        o_ref[...] = (xv * r * w_ref[...]).astype(BF16)

    return _pallas(
        body, name=name, grid=(t // tm,),
        in_specs=[pl.BlockSpec((tm, d), lambda i: (i, 0)), pl.BlockSpec((1, d), lambda i: (0, 0))],
        out_specs=pl.BlockSpec((tm, d), lambda i: (i, 0)),
        out_shape=jax.ShapeDtypeStruct((t, d), BF16), compiler_params=_params(1),
    )(x, w)


def _rms_bwd(dn, x, w, dres, *, name, tm=512):
    t, d = x.shape

    def body(dn_ref, x_ref, w_ref, dr_ref, dx_ref, dw_ref):
        i = pl.program_id(0)
        xv = x_ref[...]
        r = lax.rsqrt(jnp.mean(xv * xv, axis=1, keepdims=True) + EPS)
        xh = xv * r
        dy = dn_ref[...].astype(F32)
        g = dy * w_ref[...]
        dx_ref[...] = dr_ref[...] + r * (g - xh * jnp.mean(g * xh, axis=1, keepdims=True))

        @pl.when(i == 0)
        def _():
            dw_ref[...] = jnp.zeros_like(dw_ref)

        dw_ref[...] += jnp.sum(dy * xh, axis=0, keepdims=True)

    row = pl.BlockSpec((tm, d), lambda i: (i, 0))
    vec = pl.BlockSpec((1, d), lambda i: (0, 0))
    return _pallas(
        body, name=name, grid=(t // tm,), in_specs=[row, row, vec, row], out_specs=[row, vec],
        out_shape=[jax.ShapeDtypeStruct((t, d), F32), jax.ShapeDtypeStruct((1, d), F32)],
        compiler_params=_params(1),
    )(dn, x, w, dres)


def _final_loss(h, w, target, *, tm=512):
    t, d = h.shape

    def body(h_ref, w_ref, t_ref, dh_ref, dw_ref, loss_ref):
        i = pl.program_id(0)
        xv = h_ref[...]
        r = lax.rsqrt(jnp.mean(xv * xv, axis=1, keepdims=True) + EPS)
        xh = xv * r
        wv = w_ref[...]
        err = xh * wv - t_ref[...]
        dy = err * (1.0 / d)
        g = dy * wv
        dh_ref[...] = r * (g - xh * jnp.mean(g * xh, axis=1, keepdims=True))

        @pl.when(i == 0)
        def _():
            dw_ref[...] = jnp.zeros_like(dw_ref)
            loss_ref[...] = jnp.zeros_like(loss_ref)

        dw_ref[...] += jnp.sum(dy * xh, axis=0, keepdims=True)
        part = jnp.sum(jnp.sum(err * err, axis=1, keepdims=True), axis=0, keepdims=True)
        loss_ref[...] += part * (0.5 / d)

    row = pl.BlockSpec((tm, d), lambda i: (i, 0))
    vec = pl.BlockSpec((1, d), lambda i: (0, 0))
    return _pallas(
        body, name="final_loss", grid=(t // tm,), in_specs=[row, vec, row],
        out_specs=[row, vec, pl.BlockSpec((1, LANE), lambda i: (0, 0))],
        out_shape=[jax.ShapeDtypeStruct((t, d), F32), jax.ShapeDtypeStruct((1, d), F32),
                   jax.ShapeDtypeStruct((1, LANE), F32)],
        compiler_params=_params(1),
    )(h, w, target)


HALO = 32


SUB = 8
RC = 16


def _make_shifts(sh_ref, rows, shifts=tuple(range(1, SUB))):
    for s in shifts:
        sh_ref[s, 0:rows, :] = sh_ref[0, s:s + rows, :]


def _shifted(sh_ref, r0, j, rows):
    return sh_ref[j % SUB, pl.ds(r0 + (j - j % SUB), rows), :]


def _conf_fwd(proj, dw_w, dw_b, ln_w, ln_b, seq, *, tm=256):
    t = proj.shape[0]
    c = CONV_W
    tps = seq // tm
    hb = tm // HALO

    def body(a_ref, b_ref, g_ref, ha_ref, hb_ref, w_ref, wb_ref, lw_ref, lb_ref, y_ref, u2_ref, sh_ref):
        i = pl.program_id(0)
        keep = jnp.where(i % tps == 0, 0.0, 1.0)
        sh_ref[0, 0:HALO, :] = ha_ref[...] * _sigmoid(hb_ref[...]) * keep
        sh_ref[0, HALO:HALO + tm, :] = a_ref[...] * _sigmoid(b_ref[...])
        _make_shifts(sh_ref, tm + HALO - SUB)

        def chunk(ci, carry):
            r0 = pl.multiple_of(ci * RC, RC)
            acc = jnp.zeros((RC, c), F32) + wb_ref[...]
            for k in range(CONF_K):
                acc = acc + w_ref[k:k + 1, :] * _shifted(sh_ref, r0, HALO - CONF_K + 1 + k, RC)
            u2_ref[pl.ds(r0, RC), :] = acc
            mu = jnp.mean(acc, axis=1, keepdims=True)
            xc = acc - mu
            rs = lax.rsqrt(jnp.mean(xc * xc, axis=1, keepdims=True) + EPS)
            u3 = xc * rs * lw_ref[...] + lb_ref[...]
            gv = g_ref[pl.ds(r0, RC), :]
            y_ref[pl.ds(r0, RC), :] = (u3 * _sigmoid(u3) * gv * _sigmoid(gv)).astype(BF16)
            return carry

        lax.fori_loop(0, tm // RC, chunk, 0)

    def col(j):
        return pl.BlockSpec((tm, c), lambda i: (i, j))

    def prev(j):
        return pl.BlockSpec((HALO, c), lambda i: (jnp.maximum(i * hb - 1, 0), j))

    vec = pl.BlockSpec((1, c), lambda i: (0, 0))
    return _pallas(
        body, name="conf_fwd", grid=(t // tm,),
        in_specs=[col(0), col(1), col(2), prev(0), prev(1),
                  pl.BlockSpec((HALO, c), lambda i: (0, 0)), vec, vec, vec],
        out_specs=[pl.BlockSpec((tm, c), lambda i: (i, 0)), pl.BlockSpec((tm, c), lambda i: (i, 0))],
        out_shape=[jax.ShapeDtypeStruct((t, c), BF16), jax.ShapeDtypeStruct((t, c), F32)],
        scratch_shapes=[pltpu.VMEM((SUB, tm + HALO, c), F32)], compiler_params=_params(1),
    )(proj, proj, proj, proj, proj, dw_w, dw_b, ln_w, ln_b)


def _conf_bwd(proj, u2, dycat, dw_w, ln_w, ln_b, seq, *, tm=256):
    t = proj.shape[0]
    c = CONV_W
    tps = seq // tm
    hb = tm // HALO
    nhb = t // HALO

    def fold(v):
        out = v[0:SUB]
        for q in range(1, RC // SUB):
            out = out + v[q * SUB:(q + 1) * SUB]
        return out

    def body(a_ref, b_ref, g_ref, pa_ref, pb_ref, ng_ref, u2_ref, nu2_ref, dy_ref, ndy_ref,
             w_ref, lw_ref, lb_ref,
             dp_ref, dww_ref, dwb_ref, dlw_ref, dlb_ref, su_ref, sd_ref):
        i = pl.program_id(0)
        first = i % tps == 0
        last = i % tps == tps - 1

        @pl.when(i == 0)
        def _():
            dww_ref[...] = jnp.zeros_like(dww_ref)
            dwb_ref[...] = jnp.zeros_like(dwb_ref)
            dlw_ref[...] = jnp.zeros_like(dlw_ref)
            dlb_ref[...] = jnp.zeros_like(dlb_ref)

        su_ref[0, 0:HALO, :] = pa_ref[...] * _sigmoid(pb_ref[...]) * jnp.where(first, 0.0, 1.0)
        su_ref[0, HALO:HALO + tm, :] = a_ref[...] * _sigmoid(b_ref[...])
        _make_shifts(su_ref, tm + HALO - SUB)

        def ln_back(u2c, gv, dy):
            mu = jnp.mean(u2c, axis=1, keepdims=True)
            xc = u2c - mu
            rs = lax.rsqrt(jnp.mean(xc * xc, axis=1, keepdims=True) + EPS)
            xh = xc * rs
            lw = lw_ref[...]
            u3 = xh * lw + lb_ref[...]
            s3 = _sigmoid(u3)
            sg = _sigmoid(gv)
            dgc = dy * (u3 * s3) * _dsilu(gv, sg)
            du3 = dy * gv * sg * _dsilu(u3, s3)
            dxh = du3 * lw
            du2 = rs * (dxh - jnp.mean(dxh, axis=1, keepdims=True)
                        - xh * jnp.mean(dxh * xh, axis=1, keepdims=True))
            return du2, dgc, du3, xh

        def tile_chunk(ci, carry):
            r0 = pl.multiple_of(ci * RC, RC)
            rows = pl.ds(r0, RC)
            du2, dgc, du3, xh = ln_back(u2_ref[rows, :], g_ref[rows, :], dy_ref[rows, :])
            sd_ref[0, rows, :] = du2
            dp_ref[rows, 2 * c:3 * c] = dgc.astype(BF16)
            dwb_ref[...] += fold(du2)
            dlw_ref[...] += fold(du3 * xh)
            dlb_ref[...] += fold(du3)
            return carry

        lax.fori_loop(0, tm // RC, tile_chunk, 0)
        live = jnp.where(last, 0.0, 1.0)
        for ci in range(HALO // RC):
            rows = slice(ci * RC, (ci + 1) * RC)
            du2, _, _, _ = ln_back(nu2_ref[rows, :], ng_ref[rows, :], ndy_ref[rows, :])
            sd_ref[0, tm + ci * RC:tm + (ci + 1) * RC, :] = du2 * live
        _make_shifts(sd_ref, tm + HALO - SUB)

        def tap_chunk(ci, carry):
            r0 = pl.multiple_of(ci * RC, RC)
            rows = pl.ds(r0, RC)
            du1 = jnp.zeros((RC, c), F32)
            for k in range(CONF_K):
                du1 = du1 + w_ref[k:k + 1, :] * _shifted(sd_ref, r0, CONF_K - 1 - k, RC)
            sb = _sigmoid(b_ref[rows, :])
            dp_ref[rows, 0:c] = (du1 * sb).astype(BF16)
            dp_ref[rows, c:2 * c] = (du1 * a_ref[rows, :] * sb * (1.0 - sb)).astype(BF16)
            du2 = sd_ref[0, rows, :]
            for k in range(CONF_K):
                dww_ref[k * SUB:(k + 1) * SUB, :] += fold(du2 * _shifted(su_ref, r0, HALO - CONF_K + 1 + k, RC))
            return carry

        lax.fori_loop(0, tm // RC, tap_chunk, 0)

    def col(j):
        return pl.BlockSpec((tm, c), lambda i: (i, j))

    def prev(j):
        return pl.BlockSpec((HALO, c), lambda i: (jnp.maximum(i * hb - 1, 0), j))

    def nxt(j):
        return pl.BlockSpec((HALO, c), lambda i: (jnp.minimum((i + 1) * hb, nhb - 1), j))

    vec = pl.BlockSpec((1, c), lambda i: (0, 0))
    acc = pl.BlockSpec((SUB, c), lambda i: (0, 0))
    return _pallas(
        body, name="conf_bwd", grid=(t // tm,),
        in_specs=[col(0), col(1), col(2), prev(0), prev(1), nxt(2), col(0), nxt(0), col(0), nxt(0),
                  pl.BlockSpec((HALO, c), lambda i: (0, 0)), vec, vec],
        out_specs=[pl.BlockSpec((tm, 3 * c), lambda i: (i, 0)),
                   pl.BlockSpec((HALO * SUB, c), lambda i: (0, 0)), acc, acc, acc],
        out_shape=[jax.ShapeDtypeStruct((t, 3 * c), BF16), jax.ShapeDtypeStruct((HALO * SUB, c), F32),
                   jax.ShapeDtypeStruct((SUB, c), F32), jax.ShapeDtypeStruct((SUB, c), F32),
                   jax.ShapeDtypeStruct((SUB, c), F32)],
        scratch_shapes=[pltpu.VMEM((SUB, tm + HALO, c), F32), pltpu.VMEM((SUB, tm + HALO, c), F32)],
        compiler_params=_params(1),
    )(proj, proj, proj, proj, proj, proj, u2, u2, dycat, dycat, dw_w, ln_w, ln_b)


Q_COL = 3 * CONV_W // HEAD_DIM
K_COL = Q_COL + N_HEADS
V_COL = K_COL + N_HEADS
GA_COL = V_COL + N_HEADS


SBA_TQ = 256
SBA_WK = 4 * QB


def _sb_window(qs, kw, ws, limit, t0, carry):
    tq, wk = qs.shape[0], kw.shape[0]
    z = _dot(qs, kw, NT)
    sg = ws + lax.broadcasted_iota(jnp.int32, (tq, wk), 1)
    tg = t0 + lax.broadcasted_iota(jnp.int32, (tq, wk), 0)
    mask = sg < jnp.minimum(tg, limit)
    sp = jnp.log(1.0 + jnp.exp(-jnp.abs(z)))
    ls = jnp.minimum(z, 0.0) - sp
    lk = jnp.where(mask, ls - z, 0.0)
    jj = lax.broadcasted_iota(jnp.int32, (QB, QB), 0)
    ss = lax.broadcasted_iota(jnp.int32, (QB, QB), 1)
    ustrict = jnp.where(jj > ss, 1.0, 0.0).astype(BF16)
    laters = [None] * (wk // QB)
    for ch in reversed(range(wk // QB)):
        lkc = lk[:, ch * QB:(ch + 1) * QB]
        laters[ch] = carry + _dotx2(lkc, ustrict)
        carry = carry + jnp.sum(lkc, axis=1, keepdims=True)
    w = jnp.where(mask, jnp.exp(ls + jnp.concatenate(laters, axis=1)), 0.0)
    return mask, ls, w, carry


def _sba_fwd(proj, nb, seq, *, tq=SBA_TQ, wk=SBA_WK):
    t = proj.shape[0]
    wk = min(wk, seq)
    nq = seq // tq
    scale = HEAD_DIM ** -0.5

    def body(q_ref, k_ref, v_ref, g_ref, o_ref, y_ref):
        i = pl.program_id(2)
        t0 = i * tq
        qs = (q_ref[...] * scale).astype(BF16)

        def window(ws, limit, carry, acc):
            ws = pl.multiple_of(ws, QB)
            _, _, w, carry = _sb_window(qs, k_ref[pl.ds(ws, wk), :], ws, limit, t0, carry)
            return carry, acc + _dot(w, v_ref[pl.ds(ws, wk), :])

        ws0 = jnp.maximum(t0 + tq - wk, 0)
        carry, acc = window(ws0, seq, jnp.zeros((tq, 1), F32), jnp.zeros((tq, HEAD_DIM), F32))

        def cond(st):
            return jnp.logical_and(st[0] > 0, jnp.max(st[1]) > NEG_CUT)

        def step(st):
            c2, a2 = window(jnp.maximum(st[0] - wk, 0), st[0], st[1], st[2])
            return jnp.maximum(st[0] - wk, 0), c2, a2

        _, _, acc = lax.while_loop(cond, step, (ws0, carry, acc))
        o_ref[...] = acc
        gv = g_ref[...]
        y_ref[...] = (acc * gv * _sigmoid(gv)).astype(BF16)

    def tile(c0):
        return pl.BlockSpec((tq, HEAD_DIM), lambda b, h, i: (b * nq + i, c0 + h))

    def whole(c0):
        return pl.BlockSpec((seq, HEAD_DIM), lambda b, h, i: (b, c0 + h))

    return _pallas(
        body, name="sba_fwd", grid=(nb, N_HEADS, nq),
        in_specs=[tile(Q_COL), whole(K_COL), whole(V_COL), tile(GA_COL)],
        out_specs=[tile(0), tile(0)],
        out_shape=[jax.ShapeDtypeStruct((t, ATT_W), F32), jax.ShapeDtypeStruct((t, ATT_W), BF16)],
        compiler_params=_params(3),
    )(proj, proj, proj, proj)


def _sba_bwd(proj, o, dycat, nb, seq, *, tq=SBA_TQ, wk=SBA_WK):
    t = proj.shape[0]
    wk = min(wk, seq)
    nq = seq // tq
    nwin = -(-seq // wk) + 1
    nch = wk // QB
    scale = HEAD_DIM ** -0.5

    def body(q_ref, k_ref, v_ref, g_ref, o_ref, dy_ref, dq_ref, dk_ref, dv_ref, dg_ref, e_ref, sp_ref):
        i = pl.program_id(2)
        t0 = i * tq

        @pl.when(i == 0)
        def _():
            dk_ref[...] = jnp.zeros_like(dk_ref)
            dv_ref[...] = jnp.zeros_like(dv_ref)

        qs = (q_ref[...] * scale).astype(BF16)
        gv = g_ref[...]
        sg = _sigmoid(gv)
        dy = dy_ref[...]
        do = (dy * gv * sg).astype(BF16)
        dg_ref[...] = (dy * o_ref[...] * _dsilu(gv, sg)).astype(BF16)

        def start_of(n):
            return pl.multiple_of(jnp.maximum(t0 + tq - (n + 1) * wk, 0), QB)

        def limit_of(n):
            return jnp.where(n == 0, seq, jnp.maximum(t0 + tq - n * wk, 0))

        def near(n, carry):
            ws = start_of(n)
            _, ls, w, carry = _sb_window(qs, k_ref[pl.ds(ws, wk), :], ws, limit_of(n), t0, carry)
            e_ref[n] = w * _dot(do, v_ref[pl.ds(ws, wk), :], NT)
            sp_ref[n] = jnp.exp(ls)
            dv_ref[pl.ds(ws, wk), :] += _dot(w, do, TN)
            return carry

        carry = near(0, jnp.zeros((tq, 1), F32))

        def cond(st):
            return jnp.logical_and(start_of(st[0] - 1) > 0, jnp.max(st[1]) > NEG_CUT)

        def step(st):
            return st[0] + 1, near(st[0], st[1])

        nvis, _ = lax.while_loop(cond, step, (1, carry))

        jj = lax.broadcasted_iota(jnp.int32, (QB, QB), 0)
        ss = lax.broadcasted_iota(jnp.int32, (QB, QB), 1)
        lstrict = jnp.where(jj < ss, 1.0, 0.0).astype(BF16)

        def far(r, st):
            pre, dq = st
            n = nvis - 1 - r
            ws = start_of(n)
            e = e_ref[n]
            spn = sp_ref[n]
            gs = []
            for ch in range(nch):
                ec = e[:, ch * QB:(ch + 1) * QB]
                gs.append(pre + _dotx2(ec, lstrict))
                pre = pre + jnp.sum(ec, axis=1, keepdims=True)
            sgl = ws + lax.broadcasted_iota(jnp.int32, (tq, wk), 1)
            tgl = t0 + lax.broadcasted_iota(jnp.int32, (tq, wk), 0)
            mask = sgl < jnp.minimum(tgl, limit_of(n))
            dz = jnp.where(mask, e * (1.0 - spn) - jnp.concatenate(gs, axis=1) * spn, 0.0).astype(BF16)
            dk_ref[pl.ds(ws, wk), :] += _dot(dz, qs, TN)
            return pre, dq + _dot(dz, k_ref[pl.ds(ws, wk), :])

        _, dq = lax.fori_loop(0, nvis, far, (jnp.zeros((tq, 1), F32), jnp.zeros((tq, HEAD_DIM), F32)))
        dq_ref[...] = (dq * scale).astype(BF16)

    def tile(c0):
        return pl.BlockSpec((tq, HEAD_DIM), lambda b, h, i: (b * nq + i, c0 + h))

    def whole(c0):
        return pl.BlockSpec((seq, HEAD_DIM), lambda b, h, i: (b, c0 + h))

    return _pallas(
        body, name="sba_bwd", grid=(nb, N_HEADS, nq),
        in_specs=[tile(Q_COL), whole(K_COL), whole(V_COL), tile(GA_COL), tile(0),
                  tile(CONV_W // HEAD_DIM)],
        out_specs=[tile(0), whole(0), whole(0), tile(0)],
        out_shape=[jax.ShapeDtypeStruct((t, ATT_W), BF16), jax.ShapeDtypeStruct((t, ATT_W), F32),
                   jax.ShapeDtypeStruct((t, ATT_W), F32), jax.ShapeDtypeStruct((t, ATT_W), BF16)],
        scratch_shapes=[pltpu.VMEM((nwin, tq, wk), F32), pltpu.VMEM((nwin, tq, wk), F32)],
        compiler_params=_params(3),
    )(proj, proj, proj, proj, o, dycat)


CT = 512
PH = 8
XRC = 32
X_SHIFTS = tuple(s for s in range(PH - SSM_K + 1, PH))
D_SHIFTS = tuple(range(1, SSM_K))
Z_BLK = 0
XBC_BLK = D_INNER // CT
DT_BLK = (D_INNER + XBC) // LANE


def _softplus(x):
    return jnp.maximum(x, 0.0) + jnp.log(1.0 + jnp.exp(-jnp.abs(x)))


def _dt_fwd(proj, dt_bias, *, tm=512):
    t = proj.shape[0]

    def body(p_ref, b_ref, o_ref):
        o_ref[...] = _softplus(p_ref[...] + b_ref[...])

    return _pallas(
        body, name="dt_fwd", grid=(t // tm,),
        in_specs=[pl.BlockSpec((tm, LANE), lambda i: (i, DT_BLK)), pl.BlockSpec((1, LANE), lambda i: (0, 0))],
        out_specs=pl.BlockSpec((tm, LANE), lambda i: (i, 0)),
        out_shape=jax.ShapeDtypeStruct((t, LANE), F32), compiler_params=_params(1),
    )(proj, dt_bias)


def _dt_bwd(proj, dt_bias, ddt, *, tm=512):
    t = proj.shape[0]

    def body(p_ref, b_ref, d_ref, o_ref, db_ref):
        i = pl.program_id(0)
        lanes = lax.broadcasted_iota(jnp.int32, (tm, LANE), 1)
        dr = jnp.where(lanes < SSM_H, d_ref[...] * _sigmoid(p_ref[...] + b_ref[...]), 0.0)
        o_ref[...] = dr

        @pl.when(i == 0)
        def _():
            db_ref[...] = jnp.zeros_like(db_ref)

        db_ref[...] += jnp.sum(dr, axis=0, keepdims=True)

    vec = pl.BlockSpec((1, LANE), lambda i: (0, 0))
    row = pl.BlockSpec((tm, LANE), lambda i: (i, 0))
    return _pallas(
        body, name="dt_bwd", grid=(t // tm,),
        in_specs=[pl.BlockSpec((tm, LANE), lambda i: (i, DT_BLK)), vec, row],
        out_specs=[row, vec],
        out_shape=[jax.ShapeDtypeStruct((t, LANE), F32), jax.ShapeDtypeStruct((1, LANE), F32)],
        compiler_params=_params(1),
    )(proj, dt_bias, ddt)


def _xconv_fwd(proj, conv_w, conv_b, seq, *, tm=512):
    t = proj.shape[0]
    tps = seq // tm
    hb = tm // PH

    def body(x_ref, h_ref, w_ref, b_ref, o_ref, sh_ref):
        i = pl.program_id(1)
        sh_ref[0, 0:PH, :] = h_ref[...] * jnp.where(i % tps == 0, 0.0, 1.0)
        sh_ref[0, PH:PH + tm, :] = x_ref[...]
        _make_shifts(sh_ref, tm, X_SHIFTS)

        def chunk(ci, carry):
            r0 = pl.multiple_of(ci * XRC, XRC)
            acc = jnp.zeros((XRC, CT), F32) + b_ref[...]
            for k in range(SSM_K):
                acc = acc + w_ref[k:k + 1, :] * _shifted(sh_ref, r0, PH - SSM_K + 1 + k, XRC)
            o_ref[pl.ds(r0, XRC), :] = acc * _sigmoid(acc)
            return carry

        lax.fori_loop(0, tm // XRC, chunk, 0)

    return _pallas(
        body, name="xconv_fwd", grid=(XBC // CT, t // tm),
        in_specs=[pl.BlockSpec((tm, CT), lambda j, i: (i, XBC_BLK + j)),
                  pl.BlockSpec((PH, CT), lambda j, i: (jnp.maximum(i * hb - 1, 0), XBC_BLK + j)),
                  pl.BlockSpec((PH, CT), lambda j, i: (0, j)),
                  pl.BlockSpec((1, CT), lambda j, i: (0, j))],
        out_specs=pl.BlockSpec((tm, CT), lambda j, i: (i, j)),
        out_shape=jax.ShapeDtypeStruct((t, XBC), F32),
        scratch_shapes=[pltpu.VMEM((SUB, tm + PH, CT), F32)], compiler_params=_params(2),
    )(proj, proj, conv_w, conv_b)


def _xconv_bwd(proj, dxc, conv_w, conv_b, seq, *, tm=512):
    t = proj.shape[0]
    tps = seq // tm
    hb = tm // PH
    nhb = t // PH
    te = tm + PH

    def fold(v):
        out = v[0:SUB]
        for q in range(1, v.shape[0] // SUB):
            out = out + v[q * SUB:(q + 1) * SUB]
        return out

    def body(x_ref, p_ref, n_ref, d_ref, nd_ref, w_ref, b_ref, dx_ref, dw_ref, db_ref, sx_ref, sd_ref):
        i = pl.program_id(1)
        first = i % tps == 0
        last = i % tps == tps - 1

        @pl.when(i == 0)
        def _():
            dw_ref[...] = jnp.zeros_like(dw_ref)
            db_ref[...] = jnp.zeros_like(db_ref)

        sx_ref[0, 0:PH, :] = p_ref[...] * jnp.where(first, 0.0, 1.0)
        sx_ref[0, PH:PH + tm, :] = x_ref[...]
        sx_ref[0, PH + tm:PH + te, :] = n_ref[...]
        _make_shifts(sx_ref, te, X_SHIFTS)

        def dv_of(r0, rows, dy):
            acc = jnp.zeros((rows, CT), F32) + b_ref[...]
            for k in range(SSM_K):
                acc = acc + w_ref[k:k + 1, :] * _shifted(sx_ref, r0, PH - SSM_K + 1 + k, rows)
            return dy * _dsilu(acc, _sigmoid(acc))

        def dv_chunk(ci, carry):
            r0 = pl.multiple_of(ci * XRC, XRC)
            dv = dv_of(r0, XRC, d_ref[pl.ds(r0, XRC), :])
            sd_ref[0, pl.ds(r0, XRC), :] = dv
            db_ref[...] += fold(dv)
            return carry

        lax.fori_loop(0, tm // XRC, dv_chunk, 0)
        sd_ref[0, tm:te, :] = dv_of(tm, PH, nd_ref[...]) * jnp.where(last, 0.0, 1.0)
        _make_shifts(sd_ref, tm, D_SHIFTS)

        def tap_chunk(ci, carry):
            r0 = pl.multiple_of(ci * XRC, XRC)
            dx = jnp.zeros((XRC, CT), F32)
            for k in range(SSM_K):
                dx = dx + w_ref[k:k + 1, :] * _shifted(sd_ref, r0, SSM_K - 1 - k, XRC)
            dx_ref[pl.ds(r0, XRC), :] = dx.astype(BF16)
            dv = sd_ref[0, pl.ds(r0, XRC), :]
            for k in range(SSM_K):
                dw_ref[k * SUB:(k + 1) * SUB, :] += fold(dv * _shifted(sx_ref, r0, PH - SSM_K + 1 + k, XRC))
            return carry

        lax.fori_loop(0, tm // XRC, tap_chunk, 0)

    return _pallas(
        body, name="xconv_bwd", grid=(XBC // CT, t // tm),
        in_specs=[pl.BlockSpec((tm, CT), lambda j, i: (i, XBC_BLK + j)),
                  pl.BlockSpec((PH, CT), lambda j, i: (jnp.maximum(i * hb - 1, 0), XBC_BLK + j)),
                  pl.BlockSpec((PH, CT), lambda j, i: (jnp.minimum((i + 1) * hb, nhb - 1), XBC_BLK + j)),
                  pl.BlockSpec((tm, CT), lambda j, i: (i, j)),
                  pl.BlockSpec((PH, CT), lambda j, i: (jnp.minimum((i + 1) * hb, nhb - 1), j)),
                  pl.BlockSpec((PH, CT), lambda j, i: (0, j)),
                  pl.BlockSpec((1, CT), lambda j, i: (0, j))],
        out_specs=[pl.BlockSpec((tm, CT), lambda j, i: (i, j)),
                   pl.BlockSpec((PH * SUB, CT), lambda j, i: (0, j)),
                   pl.BlockSpec((SUB, CT), lambda j, i: (0, j))],
        out_shape=[jax.ShapeDtypeStruct((t, XBC), BF16), jax.ShapeDtypeStruct((PH * SUB, XBC), F32),
                   jax.ShapeDtypeStruct((SUB, XBC), F32)],
        scratch_shapes=[pltpu.VMEM((SUB, tm + 2 * PH, CT), F32), pltpu.VMEM((SUB, te, CT), F32)],
        compiler_params=_params(2),
    )(proj, proj, proj, dxc, dxc, conv_w, conv_b)


def _ssd_common(xbc, dt, alog, ex):
    L = CHUNK
    a = -jnp.exp(alog)
    la = dt * a
    li = lax.broadcasted_iota(jnp.int32, (L, L), 0)
    si = lax.broadcasted_iota(jnp.int32, (L, L), 1)
    lower = si <= li
    tri = jnp.where(lower, 1.0, 0.0).astype(BF16)
    cs = _xdot(tri, la)
    cst = _dotx(la, tri, (((0,), (1,)), ((), ())))
    csl = cs[L - 1:L, :]
    ecs_x = _dotx(jnp.exp(cs)[:, 0:SSM_H], ex)
    tail_x = _dotx(jnp.exp(csl - cs)[:, 0:SSM_H], ex)
    dt_x = _dotx(dt[:, 0:SSM_H], ex)
    return a, la, lower, tri, cs, cst, ecs_x, tail_x, dt_x


def _ssd_fwd(xbc_c, dt, a_log, ex, nb, seq):
    t = xbc_c.shape[0]
    L = CHUNK
    nc = seq // L
    GW = SSM_R * SSM_P

    def body(x_ref, dt_ref, al_ref, ex_ref, y_ref, st_ref, state):
        c = pl.program_id(1)

        @pl.when(c == 0)
        def _():
            state[...] = jnp.zeros_like(state)

        st_ref[0] = state[...]
        xbc = x_ref[...]
        _, _, lower, _, cs, cst, ecs_x, tail_x, dt_x = _ssd_common(xbc, dt_ref[...], al_ref[...], ex_ref[...])
        xd = xbc[:, 0:D_INNER] * dt_x
        xdb = xd.astype(BF16)
        xt = (xd * tail_x).astype(BF16)
        el_x = ecs_x[L - 1:L, :]
        for g in range(SSM_G):
            bg = xbc[:, D_INNER + g * SSM_N:D_INNER + (g + 1) * SSM_N].astype(BF16)
            cg = xbc[:, D_INNER + (SSM_G + g) * SSM_N:D_INNER + (SSM_G + g + 1) * SSM_N].astype(BF16)
            cb = _dot(cg, bg, NT)
            sg = state[:, g * GW:(g + 1) * GW]
            ys = _dot(cg, sg) * ecs_x[:, g * GW:(g + 1) * GW]
            for r in range(SSM_R):
                h = g * SSM_R + r
                seg = cs[:, h:h + 1] - cst[h:h + 1, :]
                dec = jnp.exp(jnp.where(lower, seg, -1e30))
                yh = _dot(cb * dec, xdb[:, h * SSM_P:(h + 1) * SSM_P])
                y_ref[:, h * SSM_P:(h + 1) * SSM_P] = yh + ys[:, r * SSM_P:(r + 1) * SSM_P]
            state[:, g * GW:(g + 1) * GW] = sg * el_x[:, g * GW:(g + 1) * GW] + _dot(bg, xt[:, g * GW:(g + 1) * GW], TN)

    return _pallas(
        body, name="ssd_fwd", grid=(nb, nc),
        in_specs=[pl.BlockSpec((L, XBC), lambda b, c: (b * nc + c, 0)),
                  pl.BlockSpec((L, LANE), lambda b, c: (b * nc + c, 0)),
                  pl.BlockSpec((1, LANE), lambda b, c: (0, 0)),
                  pl.BlockSpec((SSM_H, D_INNER), lambda b, c: (0, 0))],
        out_specs=[pl.BlockSpec((L, D_INNER), lambda b, c: (b * nc + c, 0)),
                   pl.BlockSpec((1, SSM_N, D_INNER), lambda b, c: (b * nc + c, 0, 0))],
        out_shape=[jax.ShapeDtypeStruct((t, D_INNER), F32),
                   jax.ShapeDtypeStruct((nb * nc, SSM_N, D_INNER), F32)],
        scratch_shapes=[pltpu.VMEM((SSM_N, D_INNER), F32)], compiler_params=_params(2),
    )(xbc_c, dt, a_log, ex)


def _ssd_bwd(xbc_c, dt, a_log, ex, ext, states, dy, dskip, nb, seq):
    t = xbc_c.shape[0]
    L = CHUNK
    nc = seq // L
    GW = SSM_R * SSM_P

    def body(x_ref, dt_ref, al_ref, ex_ref, ext_ref, st_ref, dy_ref, sk_ref, dx_ref, ddt_ref, da_ref,
             dstate, dxd, yd, lastv):
        b = pl.program_id(0)
        c = pl.program_id(1)

        @pl.when(c == 0)
        def _():
            dstate[...] = jnp.zeros_like(dstate)

        @pl.when(jnp.logical_and(b == 0, c == 0))
        def _():
            da_ref[...] = jnp.zeros_like(da_ref)

        xbc = x_ref[...]
        dtv = dt_ref[...]
        ex_t = ext_ref[...]
        a, la, lower, tri, cs, cst, ecs_x, tail_x, dt_x = _ssd_common(xbc, dtv, al_ref[...], ex_ref[...])
        xs = xbc[:, 0:D_INNER]
        xd = xs * dt_x
        xdb = xd.astype(BF16)
        dyv = dy_ref[...]
        dyb = dyv.astype(BF16)
        dys = dyv * ecs_x
        xt = xd * tail_x
        el_x = ecs_x[L - 1:L, :]
        lane = lax.broadcasted_iota(jnp.int32, (L, LANE), 1)
        sub = lax.broadcasted_iota(jnp.int32, (LANE, L), 0)
        row_part = jnp.zeros((L, LANE), F32)
        col_part = jnp.zeros((LANE, L), F32)
        for g in range(SSM_G):
            gs = slice(g * GW, (g + 1) * GW)
            bcol = slice(D_INNER + g * SSM_N, D_INNER + (g + 1) * SSM_N)
            ccol = slice(D_INNER + (SSM_G + g) * SSM_N, D_INNER + (SSM_G + g + 1) * SSM_N)
            bg = xbc[:, bcol].astype(BF16)
            cg = xbc[:, ccol].astype(BF16)
            cb = _dot(cg, bg, NT)
            sg = st_ref[0, :, gs]
            dsg = dstate[:, gs]
            dc = _dot(dys[:, gs], sg, NT)
            db = _dot(xt[:, gs], dsg, NT)
            dx_state = tail_x[:, gs] * _dot(bg, dsg)
            yd[:, gs] = dys[:, gs] * _dot(cg, sg) - xd[:, gs] * dx_state
            s_out = sg * el_x[:, gs] + _dot(bg, xt[:, gs], TN)
            lastv[:, gs] = jnp.broadcast_to(jnp.sum(dsg * s_out, axis=0, keepdims=True), (8, GW))
            dcb = jnp.zeros((L, L), F32)
            for r in range(SSM_R):
                h = g * SSM_R + r
                hs = slice(h * SSM_P, (h + 1) * SSM_P)
                seg = cs[:, h:h + 1] - cst[h:h + 1, :]
                dec = jnp.exp(jnp.where(lower, seg, -1e30))
                m = cb * dec
                dm = _dot(dyb[:, hs], xdb[:, hs], NT)
                dcb = dcb + dm * dec
                e = dm * m
                row_part = row_part + jnp.where(lane == h, jnp.sum(e, axis=1, keepdims=True), 0.0)
                col_part = col_part + jnp.where(sub == h, jnp.sum(e, axis=0, keepdims=True), 0.0)
                dxd[:, hs] = _dot(m, dyb[:, hs], TN) + dx_state[:, r * SSM_P:(r + 1) * SSM_P]
            dx_ref[:, bcol] = db + _dot(dcb, cg, TN)
            dx_ref[:, ccol] = dc + _dot(dcb, bg)
            dstate[:, gs] = dsg * el_x[:, gs] + _dot(cg, dys[:, gs], TN)
        dxv = dxd[...]
        dx_ref[:, 0:D_INNER] = dxv * dt_x + sk_ref[...]
        ddt_x = _dotx(dxv * xs, ex_t)
        yst = _dotx(yd[...], ex_t)
        lst = _dotx(lastv[...], ex_t)[0:1, :]
        rows = lax.broadcasted_iota(jnp.int32, (L, LANE), 0)
        dcs = row_part - col_part.T + yst + jnp.where(rows == L - 1, lst, 0.0)
        li = lax.broadcasted_iota(jnp.int32, (L, L), 0)
        si = lax.broadcasted_iota(jnp.int32, (L, L), 1)
        upper = jnp.where(si >= li, 1.0, 0.0).astype(BF16)
        dla = _xdot(upper, dcs)
        ddt_ref[...] = dla * a + ddt_x
        da_ref[...] += jnp.sum(dla * dtv, axis=0, keepdims=True)

    def row(w):
        return pl.BlockSpec((L, w), lambda b, c: (b * nc + nc - 1 - c, 0))

    return _pallas(
        body, name="ssd_bwd", grid=(nb, nc),
        in_specs=[row(XBC), row(LANE), pl.BlockSpec((1, LANE), lambda b, c: (0, 0)),
                  pl.BlockSpec((SSM_H, D_INNER), lambda b, c: (0, 0)),
                  pl.BlockSpec((D_INNER, LANE), lambda b, c: (0, 0)),
                  pl.BlockSpec((1, SSM_N, D_INNER), lambda b, c: (b * nc + nc - 1 - c, 0, 0)),
                  row(D_INNER), row(D_INNER)],
        out_specs=[row(XBC), row(LANE), pl.BlockSpec((1, LANE), lambda b, c: (0, 0))],
        out_shape=[jax.ShapeDtypeStruct((t, XBC), F32), jax.ShapeDtypeStruct((t, LANE), F32),
                   jax.ShapeDtypeStruct((1, LANE), F32)],
        scratch_shapes=[pltpu.VMEM((SSM_N, D_INNER), F32), pltpu.VMEM((L, D_INNER), F32),
                        pltpu.VMEM((L, D_INNER), F32), pltpu.VMEM((8, D_INNER), F32)],
        compiler_params=_params(2),
    )(xbc_c, dt, a_log, ex, ext, states, dy, dskip)


def _group_rms(y2):
    gw = D_INNER // SSM_G
    parts = []
    for g in range(SSM_G):
        v = y2[:, g * gw:(g + 1) * gw]
        r = lax.rsqrt(jnp.mean(v * v, axis=1, keepdims=True) + EPS)
        parts.append(jnp.broadcast_to(r, v.shape))
    return jnp.concatenate(parts, axis=1)


def _gate_fwd(y, xbc_c, proj, d_x, gn_w, *, tm=256):
    t = y.shape[0]

    def body(y_ref, x_ref, z_ref, d_ref, w_ref, o_ref):
        y1 = y_ref[...] + d_ref[...] * x_ref[...]
        zv = z_ref[...]
        y2 = y1 * zv * _sigmoid(zv)
        o_ref[...] = (y2 * _group_rms(y2) * w_ref[...]).astype(BF16)

    row = pl.BlockSpec((tm, D_INNER), lambda i: (i, 0))
    vec = pl.BlockSpec((1, D_INNER), lambda i: (0, 0))
    return _pallas(
        body, name="gate_fwd", grid=(t // tm,), in_specs=[row, row, row, vec, vec], out_specs=row,
        out_shape=jax.ShapeDtypeStruct((t, D_INNER), BF16), compiler_params=_params(1),
    )(y, xbc_c, proj, d_x, gn_w)


def _gate_bwd(dyg, y, xbc_c, proj, d_x, gn_w, *, tm=256):
    t = y.shape[0]
    gw = D_INNER // SSM_G

    def body(dg_ref, y_ref, x_ref, z_ref, d_ref, w_ref, dy_ref, dsk_ref, dz_ref, dw_ref, dd_ref):
        i = pl.program_id(0)
        xv = x_ref[...]
        dxv = d_ref[...]
        y1 = y_ref[...] + dxv * xv
        zv = z_ref[...]
        sz = _sigmoid(zv)
        y2 = y1 * zv * sz
        rr = _group_rms(y2)
        xh = y2 * rr
        dg = dg_ref[...]
        gq = dg * w_ref[...]
        prod = gq * xh
        means = []
        for g in range(SSM_G):
            mg = jnp.mean(prod[:, g * gw:(g + 1) * gw], axis=1, keepdims=True)
            means.append(jnp.broadcast_to(mg, (tm, gw)))
        dy2 = rr * (gq - xh * jnp.concatenate(means, axis=1))
        dy1 = dy2 * zv * sz
        dy_ref[...] = dy1
        dsk_ref[...] = dy1 * dxv
        dz_ref[...] = (dy2 * y1 * _dsilu(zv, sz)).astype(BF16)

        @pl.when(i == 0)
        def _():
            dw_ref[...] = jnp.zeros_like(dw_ref)
            dd_ref[...] = jnp.zeros_like(dd_ref)

        dw_ref[...] += jnp.sum(dg * xh, axis=0, keepdims=True)
        dd_ref[...] += jnp.sum(dy1 * xv, axis=0, keepdims=True)

    row = pl.BlockSpec((tm, D_INNER), lambda i: (i, 0))
    vec = pl.BlockSpec((1, D_INNER), lambda i: (0, 0))
    return _pallas(
        body, name="gate_bwd", grid=(t // tm,), in_specs=[row, row, row, row, vec, vec],
        out_specs=[row, row, row, vec, vec],
        out_shape=[jax.ShapeDtypeStruct((t, D_INNER), F32), jax.ShapeDtypeStruct((t, D_INNER), F32),
                   jax.ShapeDtypeStruct((t, D_INNER), BF16), jax.ShapeDtypeStruct((1, D_INNER), F32),
                   jax.ShapeDtypeStruct((1, D_INNER), F32)],
        compiler_params=_params(1),
    )(dyg, y, xbc_c, proj, d_x, gn_w)


ANY = pl.BlockSpec(memory_space=pl.ANY)


def _remote(src, dst, sems, k, to):
    send_sems, recv_sems = sems
    return pltpu.make_async_remote_copy(src_ref=src, dst_ref=dst, send_sem=send_sems.at[k], recv_sem=recv_sems.at[k],
                                        device_id=to, device_id_type=MESH)


def _gather_shards(w_in0, w_in1, w_out0, w_out1, small):
    nchip = 4
    dm, n0 = w_in0.shape
    n1 = w_in1.shape[1]
    ro = w_out0.shape[0]
    hr, ho = dm // 2, ro // 2

    def body(a0, a1, b0, b1, sm, o0, o1, p0, p1, osm, ici_s, ici_r, d2d_s, d2d_r):
        x, y, c = lax.axis_index("x"), lax.axis_index("y"), lax.axis_index("c")
        me = 2 * x + y
        sib = (x, y, 1 - c)
        peers = [(1 - x, y), (x, 1 - y), (1 - x, 1 - y)]

        def region(chip, half):
            col = pl.multiple_of(chip * n0, LANE)
            return [o0.at[pl.ds(half * hr, hr), pl.ds(col, n0)], o1.at[chip, pl.ds(half * hr, hr), :],
                    p0.at[pl.ds(chip * ro + half * ho, ho), :], p1.at[pl.ds(chip * ro + half * ho, ho), :]]

        halves = [a0.at[pl.ds(c * hr, hr), :], a1.at[pl.ds(c * hr, hr), :],
                  b0.at[pl.ds(c * ho, ho), :], b1.at[pl.ds(c * ho, ho), :]]
        sends = []
        for k, (px, py) in enumerate(peers):
            to = (px, py, c)
            for j, (s, d) in enumerate(zip(halves, region(me, c))):
                sends.append(_remote(s, d, (ici_s, ici_r), 5 * k + j, to))
            sends.append(_remote(sm, osm.at[me], (ici_s, ici_r), 5 * k + 4, to))
        for cp in sends:
            cp.start()
        for k, (px, py) in enumerate(peers):
            q = 2 * px + py
            for j, d in enumerate(region(q, c)):
                _remote(d, d, (ici_s, ici_r), 5 * k + j, (px, py, c)).wait_recv()
                fwd = _remote(d, d, (d2d_s, d2d_r), 4 * k + j, sib)
                fwd.start()
                sends.append(fwd)
            _remote(sm, osm.at[q], (ici_s, ici_r), 5 * k + 4, (px, py, c)).wait_recv()
        for k, (px, py) in enumerate(peers):
            for j, d in enumerate(region(2 * px + py, 1 - c)):
                _remote(d, d, (d2d_s, d2d_r), 4 * k + j, sib).wait_recv()
        for cp in sends:
            cp.wait_send()

    return _pallas(
        body, name="gather_shards", in_specs=[ANY] * 5, out_specs=[ANY] * 5,
        out_shape=[jax.ShapeDtypeStruct((dm, nchip * n0), w_in0.dtype),
                   jax.ShapeDtypeStruct((nchip, dm, n1), w_in1.dtype),
                   jax.ShapeDtypeStruct((nchip * ro, w_out0.shape[1]), w_out0.dtype),
                   jax.ShapeDtypeStruct((nchip * ro, w_out1.shape[1]), w_out1.dtype),
                   jax.ShapeDtypeStruct((nchip,) + small.shape, small.dtype)],
        scratch_shapes=[pltpu.SemaphoreType.DMA((15,)), pltpu.SemaphoreType.DMA((15,)),
                        pltpu.SemaphoreType.DMA((12,)), pltpu.SemaphoreType.DMA((12,))],
    )(w_in0, w_in1, w_out0, w_out1, small)


def _pair_exchange(g_in0, g_in1, g_out0, g_out1, gsmall, grep):
    dm = g_in0.shape[0]
    hr = dm // 2
    ho = g_out0.shape[1] // 2

    def body(a0, a1, b0, b1, sm, rp, q0, q1, r0, r1, osm, orp, pair_s, pair_r, send_sems, recv_sems, local_sems):
        x, y, c = lax.axis_index("x"), lax.axis_index("y"), lax.axis_index("c")
        me = 4 * x + 2 * y + c
        chip = 2 * x + y
        sib = (x, y, 1 - c)
        rows = pl.ds(pl.multiple_of((1 - c) * hr, 8), hr)
        orows = pl.ds(pl.multiple_of((1 - c) * ho, 8), ho)
        pair = [_remote(a0.at[rows, :], q0, (pair_s, pair_r), 0, sib),
                _remote(a1.at[:, rows, :], q1, (pair_s, pair_r), 1, sib),
                _remote(b0.at[:, orows, :], r0, (pair_s, pair_r), 2, sib),
                _remote(b1.at[:, orows, :], r1, (pair_s, pair_r), 3, sib)]
        for cp in pair:
            cp.start()
        own = [pltpu.make_async_copy(sm.at[chip], osm.at[me], local_sems.at[0]),
               pltpu.make_async_copy(rp, orp.at[me], local_sems.at[1])]
        for cp in own:
            cp.start()
        peers = []
        for k in range(7):
            fx, fy, fc = ((k + 1) >> 2) & 1, ((k + 1) >> 1) & 1, (k + 1) & 1
            peers.append((1 - x if fx else x, 1 - y if fy else y, 1 - c if fc else c))
        sends = []
        for k, (px, py, pc) in enumerate(peers):
            sends.append(_remote(sm.at[2 * px + py], osm.at[me], (send_sems, recv_sems), 2 * k, (px, py, pc)))
            sends.append(_remote(rp, orp.at[me], (send_sems, recv_sems), 2 * k + 1, (px, py, pc)))
        for cp in sends:
            cp.start()
        for k, (px, py, pc) in enumerate(peers):
            slot = 4 * px + 2 * py + pc
            _remote(sm.at[chip], osm.at[slot], (send_sems, recv_sems), 2 * k, (px, py, pc)).wait_recv()
            _remote(rp, orp.at[slot], (send_sems, recv_sems), 2 * k + 1, (px, py, pc)).wait_recv()
        for cp in pair:
            cp.wait_recv()
        for cp in pair + sends:
            cp.wait_send()
        for cp in own:
            cp.wait()

    return _pallas(
        body, name="pair_exchange", in_specs=[ANY] * 6, out_specs=[ANY] * 6,
        out_shape=[jax.ShapeDtypeStruct((hr, g_in0.shape[1]), F32),
                   jax.ShapeDtypeStruct((g_in1.shape[0], hr, g_in1.shape[2]), F32),
                   jax.ShapeDtypeStruct((g_out0.shape[0], ho, g_out0.shape[2]), F32),
                   jax.ShapeDtypeStruct((g_out1.shape[0], ho, g_out1.shape[2]), F32),
                   jax.ShapeDtypeStruct((8,) + gsmall.shape[1:], F32),
                   jax.ShapeDtypeStruct((8,) + grep.shape, F32)],
        scratch_shapes=[pltpu.SemaphoreType.DMA((4,)), pltpu.SemaphoreType.DMA((4,)),
                        pltpu.SemaphoreType.DMA((14,)), pltpu.SemaphoreType.DMA((14,)),
                        pltpu.SemaphoreType.DMA((2,))],
    )(g_in0, g_in1, g_out0, g_out1, gsmall, grep)


def _core_index():
    return lax.axis_index("c").astype(jnp.int32).reshape(1)


def _half_add(full, other, *, axis, block, name):
    nd = full.ndim
    nblk = other.shape[axis] // block[axis]
    grid = tuple(other.shape[d] // block[d] for d in range(nd))

    def body(c_ref, f_ref, o_ref, out_ref):
        out_ref[...] = (f_ref[...] + o_ref[...]).astype(BF16)

    def full_map(*idx):
        ids, c_ref = list(idx[:nd]), idx[nd]
        ids[axis] = ids[axis] + c_ref[0] * nblk
        return tuple(ids)

    def plain_map(*idx):
        return tuple(idx[:nd])

    return _pallas(
        body, name=name,
        grid_spec=pltpu.PrefetchScalarGridSpec(
            num_scalar_prefetch=1, grid=grid,
            in_specs=[pl.BlockSpec(block, full_map), pl.BlockSpec(block, plain_map)],
            out_specs=pl.BlockSpec(block, plain_map)),
        out_shape=jax.ShapeDtypeStruct(other.shape, BF16), compiler_params=_params(nd),
    )(_core_index(), full, other)


def _chip_exchange(s_in0, s_in1, s_out0, s_out1):
    npeer = 3
    n0 = s_in0.shape[1] // 4

    def body(a0, a1, b0, b1, l0, l1, m0, m1, send_sems, recv_sems):
        x, y, c = lax.axis_index("x"), lax.axis_index("y"), lax.axis_index("c")
        me = 2 * x + y
        peers = [(1 - x, y), (x, 1 - y), (1 - x, 1 - y)]

        def pieces(chip):
            return [a0.at[:, pl.ds(pl.multiple_of(chip * n0, LANE), n0)], a1.at[chip], b0.at[chip], b1.at[chip]]

        def slots(k):
            return [l0.at[k], l1.at[k], m0.at[k], m1.at[k]]

        sends = []
        for k, (px, py) in enumerate(peers):
            for j, (s, d) in enumerate(zip(pieces(2 * px + py), slots(k))):
                sends.append(_remote(s, d, (send_sems, recv_sems), 4 * k + j, (px, py, c)))
        for cp in sends:
            cp.start()
        for k, (px, py) in enumerate(peers):
            for j, (s, d) in enumerate(zip(pieces(me), slots(k))):
                _remote(s, d, (send_sems, recv_sems), 4 * k + j, (px, py, c)).wait_recv()
        for cp in sends:
            cp.wait_send()

    return _pallas(
        body, name="chip_exchange", in_specs=[ANY] * 4, out_specs=[ANY] * 4,
        out_shape=[jax.ShapeDtypeStruct((npeer, s_in0.shape[0], n0), BF16),
                   jax.ShapeDtypeStruct((npeer,) + s_in1.shape[1:], BF16),
                   jax.ShapeDtypeStruct((npeer,) + s_out0.shape[1:], BF16),
                   jax.ShapeDtypeStruct((npeer,) + s_out1.shape[1:], BF16)],
        scratch_shapes=[pltpu.SemaphoreType.DMA((12,)), pltpu.SemaphoreType.DMA((12,))],
    )(s_in0, s_in1, s_out0, s_out1)


def _chip_index():
    return (2 * lax.axis_index("x") + lax.axis_index("y")).astype(jnp.int32).reshape(1)


def _chip_sum(own, slots, *, own_block, own_map, block, name):
    npeer = slots.shape[0]
    shape = slots.shape[1:]
    grid = (shape[0] // block[0], shape[1] // block[1])

    def body(p_ref, own_ref, s_ref, o_ref):
        acc = own_ref[...].reshape(block).astype(F32)
        for q in range(npeer):
            acc = acc + s_ref[q].astype(F32)
        o_ref[...] = acc

    return _pallas(
        body, name=name,
        grid_spec=pltpu.PrefetchScalarGridSpec(
            num_scalar_prefetch=1, grid=grid,
            in_specs=[pl.BlockSpec(own_block, own_map),
                      pl.BlockSpec((npeer,) + block, lambda i, j, p: (0, i, j))],
            out_specs=pl.BlockSpec(block, lambda i, j, p: (i, j))),
        out_shape=jax.ShapeDtypeStruct(shape, F32), compiler_params=_params(2),
    )(_chip_index(), own, slots)


def _pair_share(r_in0, r_in1, r_out0, r_out1):
    def body(a0, a1, b0, b1, g0, g1, h0, h1, send_sems, recv_sems):
        x, y, c = lax.axis_index("x"), lax.axis_index("y"), lax.axis_index("c")
        sib = (x, y, 1 - c)
        sends = [_remote(s, d, (send_sems, recv_sems), j, sib)
                 for j, (s, d) in enumerate(zip([a0, a1, b0, b1], [g0, g1, h0, h1]))]
        for cp in sends:
            cp.start()
        for cp in sends:
            cp.wait()

    return _pallas(
        body, name="pair_share", in_specs=[ANY] * 4, out_specs=[ANY] * 4,
        out_shape=[jax.ShapeDtypeStruct(r.shape, F32) for r in (r_in0, r_in1, r_out0, r_out1)],
        scratch_shapes=[pltpu.SemaphoreType.DMA((4,)), pltpu.SemaphoreType.DMA((4,))],
    )(r_in0, r_in1, r_out0, r_out1)


def _adam_math(g, w, m, v):
    c1 = 1.0 - ADAM_B1 ** ADAM_STEP
    c2 = 1.0 - ADAM_B2 ** ADAM_STEP
    m2 = ADAM_B1 * m + (1.0 - ADAM_B1) * g
    v2 = ADAM_B2 * v + (1.0 - ADAM_B2) * (g * g)
    delta = -ADAM_LR * ((m2 / c1) / (jnp.sqrt(v2 / c2) + ADAM_EPS) + ADAM_WD * w)
    return delta, m2, v2


def _adamw_nat(g_mine, g_sib, w, m, v, *, name, tr):
    rows, cw = w.shape
    nt = g_mine.shape[0] // tr

    def body(c_ref, gm_ref, gs_ref, w_ref, m_ref, v_ref, go_ref, d_ref, nm_ref, nv_ref):
        mine = pl.program_id(0) // nt == c_ref[0]
        gv = jnp.where(mine, gm_ref[...], gs_ref[...])[:, 0:cw]
        delta, m2, v2 = _adam_math(gv, w_ref[...], m_ref[...], v_ref[...])
        go_ref[...] = gv
        d_ref[...] = delta
        nm_ref[...] = m2
        nv_ref[...] = v2

    def mine_map(i, c_ref):
        return (jnp.where(i // nt == c_ref[0], i % nt, 0), 0)

    def sib_map(i, c_ref):
        return (jnp.where(i // nt == c_ref[0], 0, i % nt), 0)

    row = pl.BlockSpec((tr, cw), lambda i, c_ref: (i, 0))
    gspec = (tr, g_mine.shape[1])
    out = jax.ShapeDtypeStruct((rows, cw), F32)
    return _pallas(
        body, name=name,
        grid_spec=pltpu.PrefetchScalarGridSpec(
            num_scalar_prefetch=1, grid=(rows // tr,),
            in_specs=[pl.BlockSpec(gspec, mine_map), pl.BlockSpec(gspec, sib_map), row, row, row],
            out_specs=[row, row, row, row]),
        out_shape=[out, out, out, out], compiler_params=_params(1),
    )(_core_index(), g_mine, g_sib, w, m, v)


def _adamw(slots, w, m, v, *, name, tr):
    nd, rows, _ = slots.shape
    c1 = 1.0 - ADAM_B1 ** ADAM_STEP
    c2 = 1.0 - ADAM_B2 ** ADAM_STEP

    def body(s_ref, w_ref, m_ref, v_ref, g_ref, d_ref, nm_ref, nv_ref):
        g = s_ref[0]
        for d in range(1, nd):
            g = g + s_ref[d]
        m2 = ADAM_B1 * m_ref[...] + (1.0 - ADAM_B1) * g
        v2 = ADAM_B2 * v_ref[...] + (1.0 - ADAM_B2) * (g * g)
        g_ref[...] = g
        nm_ref[...] = m2
        nv_ref[...] = v2
        d_ref[...] = -ADAM_LR * ((m2 / c1) / (jnp.sqrt(v2 / c2) + ADAM_EPS) + ADAM_WD * w_ref[...])

    row = pl.BlockSpec((tr, LANE), lambda i: (i, 0))
    out = jax.ShapeDtypeStruct((rows, LANE), F32)
    return _pallas(
        body, name=name, grid=(rows // tr,),
        in_specs=[pl.BlockSpec((nd, tr, LANE), lambda i: (0, i, 0)), row, row, row],
        out_specs=[row, row, row, row], out_shape=[out, out, out, out], compiler_params=_params(1),
    )(slots, w, m, v)


def _rows(a):
    return a.reshape(-1, LANE)


def _pad_rows(a, mult):
    pad = (-a.shape[0]) % mult
    return jnp.pad(a, ((0, pad), (0, 0))) if pad else a


def _pack(parts, mult):
    return _pad_rows(jnp.concatenate([_rows(p) for p in parts], axis=0), mult)


def _unpack(slab, shapes):
    out, r0 = [], 0
    for shp in shapes:
        n = 1
        for s in shp:
            n *= s
        r = n // LANE
        out.append(slab[r0:r0 + r].reshape(shp))
        r0 += r
    return out


def _pack_rep(vecs, scal):
    srow = jnp.concatenate([s.reshape(-1) for s in scal] + [jnp.zeros((LANE - 3 * SSM_H,), F32)]).reshape(1, LANE)
    return _pad_rows(jnp.concatenate([_rows(vv) for vv in vecs] + [srow], axis=0), 8)


def _unpack_rep(slab, vec_shapes, scal_shape):
    vecs, r0 = [], 0
    for shp in vec_shapes:
        vecs.append(slab[r0:r0 + 8].reshape(shp))
        r0 += 8
    srow = slab[r0]
    scal = [srow[i * SSM_H:(i + 1) * SSM_H].reshape(scal_shape) for i in range(3)]
    return vecs, scal


def kernel(x, ev_norm_w, ev_w_in, ev_dw_w, ev_dw_b, ev_ln_w, ev_ln_b, ev_w_out, od_norm_w, od_w_in, od_conv_w, od_conv_b, od_dt_bias, od_a_log, od_d, od_gnorm_w, od_w_out, final_norm_w, loss_target, m_ev_norm_w, m_ev_w_in, m_ev_dw_w, m_ev_dw_b, m_ev_ln_w, m_ev_ln_b, m_ev_w_out, m_od_norm_w, m_od_w_in, m_od_conv_w, m_od_conv_b, m_od_dt_bias, m_od_a_log, m_od_d, m_od_gnorm_w, m_od_w_out, m_final_norm_w, v_ev_norm_w, v_ev_w_in, v_ev_dw_w, v_ev_dw_b, v_ev_ln_w, v_ev_ln_b, v_ev_w_out, v_od_norm_w, v_od_w_in, v_od_conv_w, v_od_conv_b, v_od_dt_bias, v_od_a_log, v_od_d, v_od_gnorm_w, v_od_w_out, v_final_norm_w):
    nb, seq, d = x.shape
    t = nb * seq
    nchip = 4
    xf = x.reshape(t, d)
    tgt = loss_target.reshape(t, d)

    big_w = [ev_w_in[0], od_w_in[0], ev_w_out[0], od_w_out[0]]
    small_w = [ev_dw_w[0], od_norm_w[0], od_conv_w[0], od_conv_b[0], od_gnorm_w[0]]
    small_shapes = [a.shape for a in small_w]
    big_b = [a.astype(BF16) for a in big_w]
    small_slab = _pack(small_w, 8)
    w_in0, w_in1g, w_out0, w_out1, gath_small = _gather_shards(*big_b, small_slab)
    chip = 2 * lax.axis_index("x") + lax.axis_index("y")
    w_in0 = lax.dynamic_update_slice(w_in0, big_b[0], (0, chip * big_b[0].shape[1]))
    w_in1g = lax.dynamic_update_slice(w_in1g, big_b[1][None], (chip, 0, 0))
    w_out0 = lax.dynamic_update_slice(w_out0, big_b[2], (chip * big_b[2].shape[0], 0))
    w_out1 = lax.dynamic_update_slice(w_out1, big_b[3], (chip * big_b[3].shape[0], 0))
    gath_small = lax.dynamic_update_slice(gath_small, small_slab[None], (chip, 0, 0))
    per_chip = [_unpack(gath_small[p], small_shapes) for p in range(nchip)]

    def cat(idx, axis):
        return jnp.concatenate([per_chip[p][idx] for p in range(nchip)], axis=axis)

    w_in1 = jnp.pad(jnp.concatenate([w_in1g[p] for p in range(nchip)], axis=1),
                    ((0, 0), (0, IN_ODD_PAD - IN_ODD)))
    dw_w = jnp.pad(cat(0, 1), ((0, HALO - CONF_K), (0, 0)))
    n1_w = cat(1, 0).reshape(1, d)
    conv_w = jnp.pad(cat(2, 1), ((0, PH - SSM_K), (0, 0)))
    conv_b = cat(3, 0).reshape(1, XBC)
    gn_w = cat(4, 0).reshape(1, D_INNER)

    def lanes(a):
        return jnp.pad(a.reshape(1, -1), ((0, 0), (0, LANE - a.size)))

    dt_bias, a_log = lanes(od_dt_bias), lanes(od_a_log)
    d_x = jnp.repeat(od_d.reshape(-1), SSM_P).reshape(1, D_INNER)
    hid = lax.broadcasted_iota(jnp.int32, (SSM_H, D_INNER), 1) // SSM_P
    ex = (hid == lax.broadcasted_iota(jnp.int32, (SSM_H, D_INNER), 0)).astype(BF16)
    ex_t = jnp.pad(ex.T, ((0, 0), (0, LANE - SSM_H)))
    fn_w = final_norm_w.reshape(1, d)

    n0 = _rms_fwd(xf, ev_norm_w, name="rms_fwd0")
    proj0 = _matmul(n0, w_in0, mode="nn", out_dtype=F32, bm=512, bn=1024, bk=d, name="in_proj0")
    y_conv, u2 = _conf_fwd(proj0, dw_w, ev_dw_b, ev_ln_w, ev_ln_b, seq)
    o_att, y_att = _sba_fwd(proj0, nb, seq)
    ycat0 = jnp.concatenate([y_conv, y_att], axis=1)
    h1 = _matmul(ycat0, w_out0, mode="nn", out_dtype=F32, bm=512, bn=d, bk=D_INNER, name="out_proj0", residual=xf)
    n1 = _rms_fwd(h1, n1_w, name="rms_fwd1")
    proj1 = _matmul(n1, w_in1, mode="nn", out_dtype=F32, bm=512, bn=768, bk=d, name="in_proj1")
    xbc_c = _xconv_fwd(proj1, conv_w, conv_b, seq)
    dt = _dt_fwd(proj1, dt_bias)
    y_ssd, states = _ssd_fwd(xbc_c, dt, a_log, ex, nb, seq)
    yg = _gate_fwd(y_ssd, xbc_c, proj1, d_x, gn_w)
    h2 = _matmul(yg, w_out1, mode="nn", out_dtype=F32, bm=512, bn=d, bk=D_INNER, name="out_proj1", residual=h1)
    dh2, g_fn, loss_part = _final_loss(h2, fn_w, tgt)

    dyg = _matmul(dh2, w_out1, mode="nt", out_dtype=F32, bm=512, bn=1024, bk=d, name="d_out_proj1")
    g_w_out1 = _matmul(yg, dh2, mode="tn", out_dtype=F32, bm=1024, bn=d, bk=512, name="dw_out_proj1")
    dy_ssd, dskip, dz, g_gn, g_dx = _gate_bwd(dyg, y_ssd, xbc_c, proj1, d_x, gn_w)
    dxbc_c, ddt, g_a = _ssd_bwd(xbc_c, dt, a_log, ex, ex_t, states, dy_ssd, dskip, nb, seq)
    dxbc, g_conv_w, g_conv_b = _xconv_bwd(proj1, dxbc_c, conv_w, conv_b, seq)
    ddt_raw, g_dt_bias = _dt_bwd(proj1, dt_bias, ddt)
    dproj1 = jnp.concatenate([dz, dxbc, ddt_raw.astype(BF16),
                              jnp.zeros((t, IN_ODD_PAD - IN_ODD - (LANE - SSM_H)), BF16)], axis=1)
    dn1 = _matmul(dproj1, w_in1, mode="nt", out_dtype=F32, bm=512, bn=d, bk=1792, name="d_in_proj1")
    g_w_in1 = _matmul(n1, dproj1, mode="tn", out_dtype=F32, bm=d, bn=768, bk=512, name="dw_in_proj1")
    dh1, g_n1 = _rms_bwd(dn1, h1, n1_w, dh2, name="rms_bwd1")

    dycat0 = _matmul(dh1, w_out0, mode="nt", out_dtype=F32, bm=512, bn=1024, bk=d, name="d_out_proj0")
    g_w_out0 = _matmul(ycat0, dh1, mode="tn", out_dtype=F32, bm=1024, bn=d, bk=512, name="dw_out_proj0")
    dq, dk, dv, dga = _sba_bwd(proj0, o_att, dycat0, nb, seq)
    dpc, g_dw_w, g_dw_b, g_ln_w, g_ln_b = _conf_bwd(proj0, u2, dycat0, dw_w, ev_ln_w, ev_ln_b, seq)
    dproj0 = jnp.concatenate([dpc, dq, dk.astype(BF16), dv.astype(BF16), dga], axis=1)
    dn0 = _matmul(dproj0, w_in0, mode="nt", out_dtype=F32, bm=512, bn=d, bk=1792, name="d_in_proj0")
    g_w_in0 = _matmul(n0, dproj0, mode="tn", out_dtype=F32, bm=d, bn=1024, bk=512, name="dw_in_proj0")
    grad_x, g_n0 = _rms_bwd(dn0, xf, ev_norm_w, dh1, name="rms_bwd0")

    g_dw_w = g_dw_w.reshape(HALO, SUB, CONV_W).sum(axis=1)[0:CONF_K]
    g_dw_b, g_ln_w, g_ln_b = (a.sum(axis=0, keepdims=True) for a in (g_dw_b, g_ln_w, g_ln_b))
    g_conv_w = g_conv_w.reshape(PH, SUB, XBC).sum(axis=1)[0:SSM_K]
    g_conv_b = g_conv_b.sum(axis=0, keepdims=True)
    a_neg = -jnp.exp(od_a_log.reshape(-1))
    g_a_log = g_a[0, 0:SSM_H] * a_neg
    g_d = g_dx.reshape(SSM_H, SSM_P).sum(axis=1)
    n1 = IN_ODD // nchip
    n1p = -(-n1 // LANE) * LANE
    g_w_in1c = jnp.stack([jnp.pad(g_w_in1[:, p * n1:(p + 1) * n1], ((0, 0), (0, n1p - n1))) for p in range(nchip)])

    def chip_slab_small(p):
        c0, c1, c2, c3 = CONV_W // nchip, d // nchip, XBC // nchip, D_INNER // nchip
        return _pack([g_dw_w[:, p * c0:(p + 1) * c0], g_n1[0, p * c1:(p + 1) * c1],
                      g_conv_w[:, p * c2:(p + 1) * c2], g_conv_b[0, p * c2:(p + 1) * c2],
                      g_gn[0, p * c3:(p + 1) * c3]], 8)

    gsmall = jnp.stack([chip_slab_small(p) for p in range(nchip)])
    rep_vec_shapes = [ev_norm_w.shape, ev_dw_b.shape, ev_ln_w.shape, ev_ln_b.shape, final_norm_w.shape]
    grep = _pack_rep([g_n0, g_dw_b, g_ln_w, g_ln_b, g_fn], [g_dt_bias[0, 0:SSM_H], g_a_log, g_d])

    ro = D_INNER // nchip
    g_w_out0c = g_w_out0.reshape(nchip, ro, d)
    g_w_out1c = g_w_out1.reshape(nchip, ro, d)
    q_in0, q_in1, q_out0, q_out1, ssmall, srep = _pair_exchange(g_w_in0, g_w_in1c, g_w_out0c, g_w_out1c, gsmall, grep)
    s_in0 = _half_add(g_w_in0, q_in0, axis=0, block=(128, IN_EVEN), name="half_add_in0")
    s_in1 = _half_add(g_w_in1c, q_in1, axis=1, block=(1, 256, n1p), name="half_add_in1")
    s_out0 = _half_add(g_w_out0c, q_out0, axis=1, block=(1, ro // 2, d), name="half_add_out0")
    s_out1 = _half_add(g_w_out1c, q_out1, axis=1, block=(1, ro // 2, d), name="half_add_out1")
    l_in0, l_in1, l_out0, l_out1 = _chip_exchange(s_in0, s_in1, s_out0, s_out1)
    r_in0 = _chip_sum(s_in0, l_in0, own_block=(128, IN_EVEN // nchip), own_map=lambda i, j, p: (i, p[0]),
                      block=(128, IN_EVEN // nchip), name="chip_sum_in0")
    r_in1 = _chip_sum(s_in1, l_in1, own_block=(1, 256, n1p), own_map=lambda i, j, p: (p[0], i, 0),
                      block=(256, n1p), name="chip_sum_in1")
    r_out0 = _chip_sum(s_out0, l_out0, own_block=(1, ro // 2, d), own_map=lambda i, j, p: (p[0], 0, 0),
                       block=(ro // 2, d), name="chip_sum_out0")
    r_out1 = _chip_sum(s_out1, l_out1, own_block=(1, ro // 2, d), own_map=lambda i, j, p: (p[0], 0, 0),
                       block=(ro // 2, d), name="chip_sum_out1")
    big_r = [r_in0, r_in1, r_out0, r_out1]
    big_q = _pair_share(*big_r)

    big_m = [m_ev_w_in[0], m_od_w_in[0], m_ev_w_out[0], m_od_w_out[0]]
    big_v = [v_ev_w_in[0], v_od_w_in[0], v_ev_w_out[0], v_od_w_out[0]]
    big_names = ["adamw_in0", "adamw_in1", "adamw_out0", "adamw_out1"]
    out_bigs = [_adamw_nat(gm, gs, w, m, v, name=nm, tr=128)
                for gm, gs, w, m, v, nm in zip(big_r, big_q, big_w, big_m, big_v, big_names)]

    def upd(slots, ws, ms, vs, packer, name, tr):
        return _adamw(slots, packer(ws), packer(ms), packer(vs), name=name, tr=tr)

    small_m = [m_ev_dw_w[0], m_od_norm_w[0], m_od_conv_w[0], m_od_conv_b[0], m_od_gnorm_w[0]]
    small_v = [v_ev_dw_w[0], v_od_norm_w[0], v_od_conv_w[0], v_od_conv_b[0], v_od_gnorm_w[0]]
    out_small = upd(ssmall, small_w, small_m, small_v, lambda a: _pack(a, 8), "adamw_small", ssmall.shape[1])

    def rep_pack(a):
        return _pack_rep(a[0:5], a[5:8])

    rep_w = [ev_norm_w, ev_dw_b, ev_ln_w, ev_ln_b, final_norm_w, od_dt_bias, od_a_log, od_d]
    rep_m = [m_ev_norm_w, m_ev_dw_b, m_ev_ln_w, m_ev_ln_b, m_final_norm_w, m_od_dt_bias, m_od_a_log, m_od_d]
    rep_v = [v_ev_norm_w, v_ev_dw_b, v_ev_ln_w, v_ev_ln_b, v_final_norm_w, v_od_dt_bias, v_od_a_log, v_od_d]
    out_rep = upd(srep, rep_w, rep_m, rep_v, rep_pack, "adamw_rep", srep.shape[1])

    results = []
    for kind in range(4):
        bw = [o[kind].reshape((1,) + o[kind].shape) for o in out_bigs]
        sw = _unpack(out_small[kind], small_shapes)
        vecs, scal = _unpack_rep(out_rep[kind], rep_vec_shapes, od_dt_bias.shape)
        results.append([
            vecs[0], bw[0], sw[0].reshape(ev_dw_w.shape), vecs[1], vecs[2], vecs[3], bw[2],
            sw[1].reshape(od_norm_w.shape), bw[1], sw[2].reshape(od_conv_w.shape), sw[3].reshape(od_conv_b.shape),
            scal[0], scal[1], scal[2], sw[4].reshape(od_gnorm_w.shape), bw[3], vecs[4]])
    loss = lax.psum(loss_part[0, 0], ("x", "y", "c"))
    return (loss, grad_x.reshape(x.shape), *results[0], *results[1], *results[2], *results[3])
```

```python
import jax
import jax.numpy as jnp
from jax import lax
from jax.experimental import pallas as pl
from jax.experimental.pallas import tpu as pltpu

F32 = jnp.float32
BF16 = jnp.bfloat16

D_MODEL = 1024
CONV_W = 1024
ATT_W = 1024
HEAD_DIM = 128
N_HEADS = 8
CONF_K = 31
IN_EVEN = 7168
D_INNER = 2048
SSM_P = 64
SSM_H = 32
SSM_G = 4
SSM_R = SSM_H // SSM_G
SSM_N = 128
SSM_K = 4
CHUNK = 128
XBC = D_INNER + 2 * SSM_G * SSM_N
IN_ODD = D_INNER + XBC + SSM_H
IN_ODD_PAD = 5376
EPS = 1e-6
QB = 128
NEG_CUT = -100.0

ADAM_LR = 0.001
ADAM_B1 = 0.9
ADAM_B2 = 0.999
ADAM_EPS = 1e-08
ADAM_WD = 0.01
ADAM_STEP = 10

LANE = 128
VMEM_LIMIT = 56 * 1024 * 1024
MESH = pl.DeviceIdType.MESH

NN = (((1,), (0,)), ((), ()))
NT = (((1,), (1,)), ((), ()))
TN = (((0,), (0,)), ((), ()))


def _pallas(body, **kw):
    return pl.pallas_call(body, **kw)


def _params(n_axes):
    return pltpu.CompilerParams(dimension_semantics=("arbitrary",) * n_axes, vmem_limit_bytes=VMEM_LIMIT)


def _dot(a, b, dims=NN):
    return lax.dot_general(a.astype(BF16), b.astype(BF16), dims, preferred_element_type=F32)


def _parts(x):
    h = x.astype(BF16)
    r = x - h.astype(F32)
    m = r.astype(BF16)
    l = (r - m.astype(F32)).astype(BF16)
    return (h, m, l)


def _dotx(x, e01, dims=NN):
    acc = None
    for p in _parts(x):
        t = lax.dot_general(p, e01, dims, preferred_element_type=F32)
        acc = t if acc is None else acc + t
    return acc


def _dotx2(x, e01, dims=NN):
    h = x.astype(BF16)
    l = (x - h.astype(F32)).astype(BF16)
    return (lax.dot_general(h, e01, dims, preferred_element_type=F32)
            + lax.dot_general(l, e01, dims, preferred_element_type=F32))


def _xdot(e01, x, dims=NN):
    acc = None
    for p in _parts(x):
        t = lax.dot_general(e01, p, dims, preferred_element_type=F32)
        acc = t if acc is None else acc + t
    return acc


def _f32(x):
    return x.astype(F32)


def _sigmoid(x):
    return 1.0 / (1.0 + jnp.exp(-x))


def _dsilu(x, s):
    return s * (1.0 + x * (1.0 - s))


def _matmul(a, b, *, mode, out_dtype, bm, bn, bk, name, residual=None, n_major=False):
    if mode == "nn":
        (m, k), n = a.shape, b.shape[1]
        a_blk, a_map = (bm, bk), lambda i, j, kk: (i, kk)
        b_blk, b_map = (bk, bn), lambda i, j, kk: (kk, j)
        dims = NN
    elif mode == "nt":
        (m, k), n = a.shape, b.shape[0]
        a_blk, a_map = (bm, bk), lambda i, j, kk: (i, kk)
        b_blk, b_map = (bn, bk), lambda i, j, kk: (j, kk)
        dims = NT
    else:
        (k, m), n = a.shape, b.shape[1]
        a_blk, a_map = (bk, bm), lambda i, j, kk: (kk, i)
        b_blk, b_map = (bk, bn), lambda i, j, kk: (kk, j)
        dims = TN
    bm, bn, bk = min(bm, m), min(bn, n), min(bk, k)
    if mode != "nn":
        a_blk = (bm, bk) if mode == "nt" else (bk, bm)
        b_blk = (bn, bk) if mode == "nt" else (bk, bn)
    else:
        a_blk, b_blk = (bm, bk), (bk, bn)
    assert m % bm == 0 and n % bn == 0 and k % bk == 0, (name, m, n, k)
    nk = k // bk
    has_res = residual is not None

    def order(f):
        return (lambda j, i, kk: f(i, j, kk)) if n_major else f

    def body(*refs):
        a_ref, b_ref = refs[0], refs[1]
        r_ref = refs[2] if has_res else None
        o_ref = refs[2 + has_res]

        def finish(r):
            if has_res:
                r = r + r_ref[...]
            o_ref[...] = r.astype(out_dtype)

        if nk == 1:
            finish(_dot(a_ref[...], b_ref[...], dims))
            return
        acc_ref = refs[3 + has_res]
        kk = pl.program_id(2)

        @pl.when(kk == 0)
        def _():
            acc_ref[...] = jnp.zeros_like(acc_ref)

        acc_ref[...] += _dot(a_ref[...], b_ref[...], dims)

        @pl.when(kk == nk - 1)
        def _():
            finish(acc_ref[...])

    in_specs = [pl.BlockSpec(a_blk, order(a_map)), pl.BlockSpec(b_blk, order(b_map))]
    args = [a, b]
    out_map = order(lambda i, j, kk: (i, j))
    if has_res:
        in_specs.append(pl.BlockSpec((bm, bn), out_map))
        args.append(residual)
    grid = (n // bn, m // bm, nk) if n_major else (m // bm, n // bn, nk)
    return _pallas(
        body, name=name, grid=grid, in_specs=in_specs,
        out_specs=pl.BlockSpec((bm, bn), out_map),
        out_shape=jax.ShapeDtypeStruct((m, n), out_dtype),
        scratch_shapes=[pltpu.VMEM((bm, bn), F32)] if nk > 1 else [], compiler_params=_params(3),
    )(*args)


def _rms_fwd(x, w, *, name, tm=512):
    t, d = x.shape

    def body(x_ref, w_ref, o_ref):
        xv = x_ref[...]
        r = lax.rsqrt(jnp.mean(xv * xv, axis=1, keepdims=True) + EPS)
        o_ref[...] = (xv * r * w_ref[...]).astype(BF16)

    return _pallas(
        body, name=name, grid=(t // tm,),
        in_specs=[pl.BlockSpec((tm, d), lambda i: (i, 0)), pl.BlockSpec((1, d), lambda i: (0, 0))],
        out_specs=pl.BlockSpec((tm, d), lambda i: (i, 0)),
        out_shape=jax.ShapeDtypeStruct((t, d), BF16), compiler_params=_params(1),
    )(x, w)


def _rms_bwd(dn, x, w, dres, *, name, tm=512):
    t, d = x.shape

    def body(dn_ref, x_ref, w_ref, dr_ref, dx_ref, dw_ref):
        i = pl.program_id(0)
        xv = x_ref[...]
        r = lax.rsqrt(jnp.mean(xv * xv, axis=1, keepdims=True) + EPS)
        xh = xv * r
        dy = dn_ref[...].astype(F32)
        g = dy * w_ref[...]
        dx_ref[...] = dr_ref[...] + r * (g - xh * jnp.mean(g * xh, axis=1, keepdims=True))

        @pl.when(i == 0)
        def _():
            dw_ref[...] = jnp.zeros_like(dw_ref)

        dw_ref[...] += jnp.sum(dy * xh, axis=0, keepdims=True)

    row = pl.BlockSpec((tm, d), lambda i: (i, 0))
    vec = pl.BlockSpec((1, d), lambda i: (0, 0))
    return _pallas(
        body, name=name, grid=(t // tm,), in_specs=[row, row, vec, row], out_specs=[row, vec],
        out_shape=[jax.ShapeDtypeStruct((t, d), F32), jax.ShapeDtypeStruct((1, d), F32)],
        compiler_params=_params(1),
    )(dn, x, w, dres)


def _final_loss(h, w, target, *, tm=512):
    t, d = h.shape

    def body(h_ref, w_ref, t_ref, dh_ref, dw_ref, loss_ref):
        i = pl.program_id(0)
        xv = h_ref[...]
        r = lax.rsqrt(jnp.mean(xv * xv, axis=1, keepdims=True) + EPS)
        xh = xv * r
        wv = w_ref[...]
        err = xh * wv - t_ref[...]
        dy = err * (1.0 / d)
        g = dy * wv
        dh_ref[...] = r * (g - xh * jnp.mean(g * xh, axis=1, keepdims=True))

        @pl.when(i == 0)
        def _():
            dw_ref[...] = jnp.zeros_like(dw_ref)
            loss_ref[...] = jnp.zeros_like(loss_ref)

        dw_ref[...] += jnp.sum(dy * xh, axis=0, keepdims=True)
        part = jnp.sum(jnp.sum(err * err, axis=1, keepdims=True), axis=0, keepdims=True)
        loss_ref[...] += part * (0.5 / d)

    row = pl.BlockSpec((tm, d), lambda i: (i, 0))
    vec = pl.BlockSpec((1, d), lambda i: (0, 0))
    return _pallas(
        body, name="final_loss", grid=(t // tm,), in_specs=[row, vec, row],
        out_specs=[row, vec, pl.BlockSpec((1, LANE), lambda i: (0, 0))],
        out_shape=[jax.ShapeDtypeStruct((t, d), F32), jax.ShapeDtypeStruct((1, d), F32),
                   jax.ShapeDtypeStruct((1, LANE), F32)],
        compiler_params=_params(1),
    )(h, w, target)


HALO = 32


SUB = 8
RC = 16


def _make_shifts(sh_ref, rows, shifts=tuple(range(1, SUB))):
    for s in shifts:
        sh_ref[s, 0:rows, :] = sh_ref[0, s:s + rows, :]


def _shifted(sh_ref, r0, j, rows):
    return sh_ref[j % SUB, pl.ds(r0 + (j - j % SUB), rows), :]


def _conf_fwd(proj, dw_w, dw_b, ln_w, ln_b, seq, *, tm=256):
    t = proj.shape[0]
    c = CONV_W
    tps = seq // tm
    hb = tm // HALO

    def body(a_ref, b_ref, g_ref, ha_ref, hb_ref, w_ref, wb_ref, lw_ref, lb_ref, y_ref, u2_ref, sh_ref):
        i = pl.program_id(0)
        keep = jnp.where(i % tps == 0, 0.0, 1.0)
        sh_ref[0, 0:HALO, :] = _f32(ha_ref[...]) * _sigmoid(_f32(hb_ref[...])) * keep
        sh_ref[0, HALO:HALO + tm, :] = _f32(a_ref[...]) * _sigmoid(_f32(b_ref[...]))
        _make_shifts(sh_ref, tm + HALO - SUB)

        def chunk(ci, carry):
            r0 = pl.multiple_of(ci * RC, RC)
            acc = jnp.zeros((RC, c), F32) + wb_ref[...]
            for k in range(CONF_K):
                acc = acc + w_ref[k:k + 1, :] * _shifted(sh_ref, r0, HALO - CONF_K + 1 + k, RC)
            u2_ref[pl.ds(r0, RC), :] = acc
            mu = jnp.mean(acc, axis=1, keepdims=True)
            xc = acc - mu
            rs = lax.rsqrt(jnp.mean(xc * xc, axis=1, keepdims=True) + EPS)
            u3 = xc * rs * lw_ref[...] + lb_ref[...]
            gv = _f32(g_ref[pl.ds(r0, RC), :])
            y_ref[pl.ds(r0, RC), :] = (u3 * _sigmoid(u3) * gv * _sigmoid(gv)).astype(BF16)
            return carry

        lax.fori_loop(0, tm // RC, chunk, 0)

    def col(j):
        return pl.BlockSpec((tm, c), lambda i: (i, j))

    def prev(j):
        return pl.BlockSpec((HALO, c), lambda i: (jnp.maximum(i * hb - 1, 0), j))

    vec = pl.BlockSpec((1, c), lambda i: (0, 0))
    return _pallas(
        body, name="conf_fwd", grid=(t // tm,),
        in_specs=[col(0), col(1), col(2), prev(0), prev(1),
                  pl.BlockSpec((HALO, c), lambda i: (0, 0)), vec, vec, vec],
        out_specs=[pl.BlockSpec((tm, c), lambda i: (i, 0)), pl.BlockSpec((tm, c), lambda i: (i, 0))],
        out_shape=[jax.ShapeDtypeStruct((t, c), BF16), jax.ShapeDtypeStruct((t, c), F32)],
        scratch_shapes=[pltpu.VMEM((SUB, tm + HALO, c), F32)], compiler_params=_params(1),
    )(proj, proj, proj, proj, proj, dw_w, dw_b, ln_w, ln_b)


def _conf_bwd(proj, u2, dycat, dw_w, ln_w, ln_b, seq, *, tm=256):
    t = proj.shape[0]
    c = CONV_W
    tps = seq // tm
    hb = tm // HALO
    nhb = t // HALO

    def fold(v):
        out = v[0:SUB]
        for q in range(1, RC // SUB):
            out = out + v[q * SUB:(q + 1) * SUB]
        return out

    def body(a_ref, b_ref, g_ref, pa_ref, pb_ref, ng_ref, u2_ref, nu2_ref, dy_ref, ndy_ref,
             w_ref, lw_ref, lb_ref,
             dp_ref, dww_ref, dwb_ref, dlw_ref, dlb_ref, su_ref, sd_ref):
        i = pl.program_id(0)
        first = i % tps == 0
        last = i % tps == tps - 1

        @pl.when(i == 0)
        def _():
            dww_ref[...] = jnp.zeros_like(dww_ref)
            dwb_ref[...] = jnp.zeros_like(dwb_ref)
            dlw_ref[...] = jnp.zeros_like(dlw_ref)
            dlb_ref[...] = jnp.zeros_like(dlb_ref)

        su_ref[0, 0:HALO, :] = _f32(pa_ref[...]) * _sigmoid(_f32(pb_ref[...])) * jnp.where(first, 0.0, 1.0)
        su_ref[0, HALO:HALO + tm, :] = _f32(a_ref[...]) * _sigmoid(_f32(b_ref[...]))
        _make_shifts(su_ref, tm + HALO - SUB)

        def ln_back(u2c, gv, dy):
            mu = jnp.mean(u2c, axis=1, keepdims=True)
            xc = u2c - mu
            rs = lax.rsqrt(jnp.mean(xc * xc, axis=1, keepdims=True) + EPS)
            xh = xc * rs
            lw = lw_ref[...]
            u3 = xh * lw + lb_ref[...]
            s3 = _sigmoid(u3)
            sg = _sigmoid(gv)
            dgc = dy * (u3 * s3) * _dsilu(gv, sg)
            du3 = dy * gv * sg * _dsilu(u3, s3)
            dxh = du3 * lw
            du2 = rs * (dxh - jnp.mean(dxh, axis=1, keepdims=True)
                        - xh * jnp.mean(dxh * xh, axis=1, keepdims=True))
            return du2, dgc, du3, xh

        def tile_chunk(ci, carry):
            r0 = pl.multiple_of(ci * RC, RC)
            rows = pl.ds(r0, RC)
            du2, dgc, du3, xh = ln_back(u2_ref[rows, :], _f32(g_ref[rows, :]), _f32(dy_ref[rows, :]))
            sd_ref[0, rows, :] = du2
            dp_ref[rows, 2 * c:3 * c] = dgc.astype(BF16)
            dwb_ref[...] += fold(du2)
            dlw_ref[...] += fold(du3 * xh)
            dlb_ref[...] += fold(du3)
            return carry

        lax.fori_loop(0, tm // RC, tile_chunk, 0)
        live = jnp.where(last, 0.0, 1.0)
        for ci in range(HALO // RC):
            rows = slice(ci * RC, (ci + 1) * RC)
            du2, _, _, _ = ln_back(nu2_ref[rows, :], _f32(ng_ref[rows, :]), _f32(ndy_ref[rows, :]))
            sd_ref[0, tm + ci * RC:tm + (ci + 1) * RC, :] = du2 * live
        _make_shifts(sd_ref, tm + HALO - SUB)

        def tap_chunk(ci, carry):
            r0 = pl.multiple_of(ci * RC, RC)
            rows = pl.ds(r0, RC)
            du1 = jnp.zeros((RC, c), F32)
            for k in range(CONF_K):
                du1 = du1 + w_ref[k:k + 1, :] * _shifted(sd_ref, r0, CONF_K - 1 - k, RC)
            sb = _sigmoid(_f32(b_ref[rows, :]))
            dp_ref[rows, 0:c] = (du1 * sb).astype(BF16)
            dp_ref[rows, c:2 * c] = (du1 * _f32(a_ref[rows, :]) * sb * (1.0 - sb)).astype(BF16)
            du2 = sd_ref[0, rows, :]
            for k in range(CONF_K):
                dww_ref[k * SUB:(k + 1) * SUB, :] += fold(du2 * _shifted(su_ref, r0, HALO - CONF_K + 1 + k, RC))
            return carry

        lax.fori_loop(0, tm // RC, tap_chunk, 0)

    def col(j):
        return pl.BlockSpec((tm, c), lambda i: (i, j))

    def prev(j):
        return pl.BlockSpec((HALO, c), lambda i: (jnp.maximum(i * hb - 1, 0), j))

    def nxt(j):
        return pl.BlockSpec((HALO, c), lambda i: (jnp.minimum((i + 1) * hb, nhb - 1), j))

    vec = pl.BlockSpec((1, c), lambda i: (0, 0))
    acc = pl.BlockSpec((SUB, c), lambda i: (0, 0))
    return _pallas(
        body, name="conf_bwd", grid=(t // tm,),
        in_specs=[col(0), col(1), col(2), prev(0), prev(1), nxt(2), col(0), nxt(0), col(0), nxt(0),
                  pl.BlockSpec((HALO, c), lambda i: (0, 0)), vec, vec],
        out_specs=[pl.BlockSpec((tm, 3 * c), lambda i: (i, 0)),
                   pl.BlockSpec((HALO * SUB, c), lambda i: (0, 0)), acc, acc, acc],
        out_shape=[jax.ShapeDtypeStruct((t, 3 * c), BF16), jax.ShapeDtypeStruct((HALO * SUB, c), F32),
                   jax.ShapeDtypeStruct((SUB, c), F32), jax.ShapeDtypeStruct((SUB, c), F32),
                   jax.ShapeDtypeStruct((SUB, c), F32)],
        scratch_shapes=[pltpu.VMEM((SUB, tm + HALO, c), F32), pltpu.VMEM((SUB, tm + HALO, c), F32)],
        compiler_params=_params(1),
    )(proj, proj, proj, proj, proj, proj, u2, u2, dycat, dycat, dw_w, ln_w, ln_b)


Q_COL = 3 * CONV_W // HEAD_DIM
K_COL = Q_COL + N_HEADS
V_COL = K_COL + N_HEADS
GA_COL = V_COL + N_HEADS


SBA_TQ = 256
SBA_WK = 4 * QB


def _sb_window(qs, kw, ws, limit, t0, carry):
    tq, wk = qs.shape[0], kw.shape[0]
    z = _dot(qs, kw, NT)
    sg = ws + lax.broadcasted_iota(jnp.int32, (tq, wk), 1)
    tg = t0 + lax.broadcasted_iota(jnp.int32, (tq, wk), 0)
    mask = sg < jnp.minimum(tg, limit)
    sp = jnp.log(1.0 + jnp.exp(-jnp.abs(z)))
    ls = jnp.minimum(z, 0.0) - sp
    lk = jnp.where(mask, ls - z, 0.0)
    jj = lax.broadcasted_iota(jnp.int32, (QB, QB), 0)
    ss = lax.broadcasted_iota(jnp.int32, (QB, QB), 1)
    ustrict = jnp.where(jj > ss, 1.0, 0.0).astype(BF16)
    laters = [None] * (wk // QB)
    for ch in reversed(range(wk // QB)):
        lkc = lk[:, ch * QB:(ch + 1) * QB]
        laters[ch] = carry + _dotx2(lkc, ustrict)
        carry = carry + jnp.sum(lkc, axis=1, keepdims=True)
    w = jnp.where(mask, jnp.exp(ls + jnp.concatenate(laters, axis=1)), 0.0)
    return mask, ls, w, carry


def _sba_fwd(proj, nb, seq, *, tq=SBA_TQ, wk=SBA_WK):
    t = proj.shape[0]
    wk = min(wk, seq)
    nq = seq // tq
    scale = HEAD_DIM ** -0.5

    def body(q_ref, k_ref, v_ref, g_ref, o_ref, y_ref):
        i = pl.program_id(2)
        t0 = i * tq
        qs = (_f32(q_ref[...]) * scale).astype(BF16)

        def window(ws, limit, carry, acc):
            ws = pl.multiple_of(ws, QB)
            _, _, w, carry = _sb_window(qs, k_ref[pl.ds(ws, wk), :], ws, limit, t0, carry)
            return carry, acc + _dot(w, v_ref[pl.ds(ws, wk), :])

        ws0 = jnp.maximum(t0 + tq - wk, 0)
        carry, acc = window(ws0, seq, jnp.zeros((tq, 1), F32), jnp.zeros((tq, HEAD_DIM), F32))

        def cond(st):
            return jnp.logical_and(st[0] > 0, jnp.max(st[1]) > NEG_CUT)

        def step(st):
            c2, a2 = window(jnp.maximum(st[0] - wk, 0), st[0], st[1], st[2])
            return jnp.maximum(st[0] - wk, 0), c2, a2

        _, _, acc = lax.while_loop(cond, step, (ws0, carry, acc))
        o_ref[...] = acc
        gv = _f32(g_ref[...])
        y_ref[...] = (acc * gv * _sigmoid(gv)).astype(BF16)

    def tile(c0):
        return pl.BlockSpec((tq, HEAD_DIM), lambda b, h, i: (b * nq + i, c0 + h))

    def whole(c0):
        return pl.BlockSpec((seq, HEAD_DIM), lambda b, h, i: (b, c0 + h))

    return _pallas(
        body, name="sba_fwd", grid=(nb, N_HEADS, nq),
        in_specs=[tile(Q_COL), whole(K_COL), whole(V_COL), tile(GA_COL)],
        out_specs=[tile(0), tile(0)],
        out_shape=[jax.ShapeDtypeStruct((t, ATT_W), F32), jax.ShapeDtypeStruct((t, ATT_W), BF16)],
        compiler_params=_params(3),
    )(proj, proj, proj, proj)


def _sba_bwd(proj, o, dycat, nb, seq, *, tq=SBA_TQ, wk=SBA_WK):
    t = proj.shape[0]
    wk = min(wk, seq)
    nq = seq // tq
    nwin = -(-seq // wk) + 1
    nch = wk // QB
    scale = HEAD_DIM ** -0.5

    def body(q_ref, k_ref, v_ref, g_ref, o_ref, dy_ref, dq_ref, dk_ref, dv_ref, dg_ref, e_ref, sp_ref):
        i = pl.program_id(2)
        t0 = i * tq

        @pl.when(i == 0)
        def _():
            dk_ref[...] = jnp.zeros_like(dk_ref)
            dv_ref[...] = jnp.zeros_like(dv_ref)

        qs = (_f32(q_ref[...]) * scale).astype(BF16)
        gv = _f32(g_ref[...])
        sg = _sigmoid(gv)
        dy = _f32(dy_ref[...])
        do = (dy * gv * sg).astype(BF16)
        dg_ref[...] = (dy * o_ref[...] * _dsilu(gv, sg)).astype(BF16)

        def start_of(n):
            return pl.multiple_of(jnp.maximum(t0 + tq - (n + 1) * wk, 0), QB)

        def limit_of(n):
            return jnp.where(n == 0, seq, jnp.maximum(t0 + tq - n * wk, 0))

        def near(n, carry):
            ws = start_of(n)
            _, ls, w, carry = _sb_window(qs, k_ref[pl.ds(ws, wk), :], ws, limit_of(n), t0, carry)
            e_ref[n] = w * _dot(do, v_ref[pl.ds(ws, wk), :], NT)
            sp_ref[n] = jnp.exp(ls)
            dv_ref[pl.ds(ws, wk), :] += _dot(w, do, TN)
            return carry

        carry = near(0, jnp.zeros((tq, 1), F32))

        def cond(st):
            return jnp.logical_and(start_of(st[0] - 1) > 0, jnp.max(st[1]) > NEG_CUT)

        def step(st):
            return st[0] + 1, near(st[0], st[1])

        nvis, _ = lax.while_loop(cond, step, (1, carry))

        jj = lax.broadcasted_iota(jnp.int32, (QB, QB), 0)
        ss = lax.broadcasted_iota(jnp.int32, (QB, QB), 1)
        lstrict = jnp.where(jj < ss, 1.0, 0.0).astype(BF16)

        def far(r, st):
            pre, dq = st
            n = nvis - 1 - r
            ws = start_of(n)
            e = e_ref[n]
            spn = sp_ref[n]
            gs = []
            for ch in range(nch):
                ec = e[:, ch * QB:(ch + 1) * QB]
                gs.append(pre + _dotx2(ec, lstrict))
                pre = pre + jnp.sum(ec, axis=1, keepdims=True)
            sgl = ws + lax.broadcasted_iota(jnp.int32, (tq, wk), 1)
            tgl = t0 + lax.broadcasted_iota(jnp.int32, (tq, wk), 0)
            mask = sgl < jnp.minimum(tgl, limit_of(n))
            dz = jnp.where(mask, e * (1.0 - spn) - jnp.concatenate(gs, axis=1) * spn, 0.0).astype(BF16)
            dk_ref[pl.ds(ws, wk), :] += _dot(dz, qs, TN)
            return pre, dq + _dot(dz, k_ref[pl.ds(ws, wk), :])

        _, dq = lax.fori_loop(0, nvis, far, (jnp.zeros((tq, 1), F32), jnp.zeros((tq, HEAD_DIM), F32)))
        dq_ref[...] = (dq * scale).astype(BF16)

    def tile(c0):
        return pl.BlockSpec((tq, HEAD_DIM), lambda b, h, i: (b * nq + i, c0 + h))

    def whole(c0):
        return pl.BlockSpec((seq, HEAD_DIM), lambda b, h, i: (b, c0 + h))

    return _pallas(
        body, name="sba_bwd", grid=(nb, N_HEADS, nq),
        in_specs=[tile(Q_COL), whole(K_COL), whole(V_COL), tile(GA_COL), tile(0),
                  tile(CONV_W // HEAD_DIM)],
        out_specs=[tile(0), whole(0), whole(0), tile(0)],
        out_shape=[jax.ShapeDtypeStruct((t, ATT_W), BF16), jax.ShapeDtypeStruct((t, ATT_W), F32),
                   jax.ShapeDtypeStruct((t, ATT_W), F32), jax.ShapeDtypeStruct((t, ATT_W), BF16)],
        scratch_shapes=[pltpu.VMEM((nwin, tq, wk), F32), pltpu.VMEM((nwin, tq, wk), F32)],
        compiler_params=_params(3),
    )(proj, proj, proj, proj, o, dycat)


CT = 512
PH = 8
XRC = 32
X_SHIFTS = tuple(s for s in range(PH - SSM_K + 1, PH))
D_SHIFTS = tuple(range(1, SSM_K))
Z_BLK = 0
XBC_BLK = D_INNER // CT
DT_BLK = (D_INNER + XBC) // LANE


def _softplus(x):
    return jnp.maximum(x, 0.0) + jnp.log(1.0 + jnp.exp(-jnp.abs(x)))


def _dt_fwd(proj, dt_bias, *, tm=512):
    t = proj.shape[0]

    def body(p_ref, b_ref, o_ref):
        o_ref[...] = _softplus(p_ref[...] + b_ref[...])

    return _pallas(
        body, name="dt_fwd", grid=(t // tm,),
        in_specs=[pl.BlockSpec((tm, LANE), lambda i: (i, DT_BLK)), pl.BlockSpec((1, LANE), lambda i: (0, 0))],
        out_specs=pl.BlockSpec((tm, LANE), lambda i: (i, 0)),
        out_shape=jax.ShapeDtypeStruct((t, LANE), F32), compiler_params=_params(1),
    )(proj, dt_bias)


def _dt_bwd(proj, dt_bias, ddt, *, tm=512):
    t = proj.shape[0]

    def body(p_ref, b_ref, d_ref, o_ref, db_ref):
        i = pl.program_id(0)
        lanes = lax.broadcasted_iota(jnp.int32, (tm, LANE), 1)
        dr = jnp.where(lanes < SSM_H, d_ref[...] * _sigmoid(p_ref[...] + b_ref[...]), 0.0)
        o_ref[...] = dr

        @pl.when(i == 0)
        def _():
            db_ref[...] = jnp.zeros_like(db_ref)

        db_ref[...] += jnp.sum(dr, axis=0, keepdims=True)

    vec = pl.BlockSpec((1, LANE), lambda i: (0, 0))
    row = pl.BlockSpec((tm, LANE), lambda i: (i, 0))
    return _pallas(
        body, name="dt_bwd", grid=(t // tm,),
        in_specs=[pl.BlockSpec((tm, LANE), lambda i: (i, DT_BLK)), vec, row],
        out_specs=[row, vec],
        out_shape=[jax.ShapeDtypeStruct((t, LANE), F32), jax.ShapeDtypeStruct((1, LANE), F32)],
        compiler_params=_params(1),
    )(proj, dt_bias, ddt)


def _xconv_fwd(proj, conv_w, conv_b, seq, *, tm=512):
    t = proj.shape[0]
    tps = seq // tm
    hb = tm // PH

    def body(x_ref, h_ref, w_ref, b_ref, o_ref, sh_ref):
        i = pl.program_id(1)
        sh_ref[0, 0:PH, :] = h_ref[...] * jnp.where(i % tps == 0, 0.0, 1.0)
        sh_ref[0, PH:PH + tm, :] = x_ref[...]
        _make_shifts(sh_ref, tm, X_SHIFTS)

        def chunk(ci, carry):
            r0 = pl.multiple_of(ci * XRC, XRC)
            acc = jnp.zeros((XRC, CT), F32) + b_ref[...]
            for k in range(SSM_K):
                acc = acc + w_ref[k:k + 1, :] * _shifted(sh_ref, r0, PH - SSM_K + 1 + k, XRC)
            o_ref[pl.ds(r0, XRC), :] = acc * _sigmoid(acc)
            return carry

        lax.fori_loop(0, tm // XRC, chunk, 0)

    return _pallas(
        body, name="xconv_fwd", grid=(XBC // CT, t // tm),
        in_specs=[pl.BlockSpec((tm, CT), lambda j, i: (i, XBC_BLK + j)),
                  pl.BlockSpec((PH, CT), lambda j, i: (jnp.maximum(i * hb - 1, 0), XBC_BLK + j)),
                  pl.BlockSpec((PH, CT), lambda j, i: (0, j)),
                  pl.BlockSpec((1, CT), lambda j, i: (0, j))],
        out_specs=pl.BlockSpec((tm, CT), lambda j, i: (i, j)),
        out_shape=jax.ShapeDtypeStruct((t, XBC), F32),
        scratch_shapes=[pltpu.VMEM((SUB, tm + PH, CT), F32)], compiler_params=_params(2),
    )(proj, proj, conv_w, conv_b)


def _xconv_bwd(proj, dxc, conv_w, conv_b, seq, *, tm=512):
    t = proj.shape[0]
    tps = seq // tm
    hb = tm // PH
    nhb = t // PH
    te = tm + PH

    def fold(v):
        out = v[0:SUB]
        for q in range(1, v.shape[0] // SUB):
            out = out + v[q * SUB:(q + 1) * SUB]
        return out

    def body(x_ref, p_ref, n_ref, d_ref, nd_ref, w_ref, b_ref, dx_ref, dw_ref, db_ref, sx_ref, sd_ref):
        i = pl.program_id(1)
        first = i % tps == 0
        last = i % tps == tps - 1

        @pl.when(i == 0)
        def _():
            dw_ref[...] = jnp.zeros_like(dw_ref)
            db_ref[...] = jnp.zeros_like(db_ref)

        sx_ref[0, 0:PH, :] = p_ref[...] * jnp.where(first, 0.0, 1.0)
        sx_ref[0, PH:PH + tm, :] = x_ref[...]
        sx_ref[0, PH + tm:PH + te, :] = n_ref[...]
        _make_shifts(sx_ref, te, X_SHIFTS)

        def dv_of(r0, rows, dy):
            acc = jnp.zeros((rows, CT), F32) + b_ref[...]
            for k in range(SSM_K):
                acc = acc + w_ref[k:k + 1, :] * _shifted(sx_ref, r0, PH - SSM_K + 1 + k, rows)
            return dy * _dsilu(acc, _sigmoid(acc))

        def dv_chunk(ci, carry):
            r0 = pl.multiple_of(ci * XRC, XRC)
            dv = dv_of(r0, XRC, d_ref[pl.ds(r0, XRC), :])
            sd_ref[0, pl.ds(r0, XRC), :] = dv
            db_ref[...] += fold(dv)
            return carry

        lax.fori_loop(0, tm // XRC, dv_chunk, 0)
        sd_ref[0, tm:te, :] = dv_of(tm, PH, nd_ref[...]) * jnp.where(last, 0.0, 1.0)
        _make_shifts(sd_ref, tm, D_SHIFTS)

        def tap_chunk(ci, carry):
            r0 = pl.multiple_of(ci * XRC, XRC)
            dx = jnp.zeros((XRC, CT), F32)
            for k in range(SSM_K):
                dx = dx + w_ref[k:k + 1, :] * _shifted(sd_ref, r0, SSM_K - 1 - k, XRC)
            dx_ref[pl.ds(r0, XRC), :] = dx.astype(BF16)
            dv = sd_ref[0, pl.ds(r0, XRC), :]
            for k in range(SSM_K):
                dw_ref[k * SUB:(k + 1) * SUB, :] += fold(dv * _shifted(sx_ref, r0, PH - SSM_K + 1 + k, XRC))
            return carry

        lax.fori_loop(0, tm // XRC, tap_chunk, 0)

    return _pallas(
        body, name="xconv_bwd", grid=(XBC // CT, t // tm),
        in_specs=[pl.BlockSpec((tm, CT), lambda j, i: (i, XBC_BLK + j)),
                  pl.BlockSpec((PH, CT), lambda j, i: (jnp.maximum(i * hb - 1, 0), XBC_BLK + j)),
                  pl.BlockSpec((PH, CT), lambda j, i: (jnp.minimum((i + 1) * hb, nhb - 1), XBC_BLK + j)),
                  pl.BlockSpec((tm, CT), lambda j, i: (i, j)),
                  pl.BlockSpec((PH, CT), lambda j, i: (jnp.minimum((i + 1) * hb, nhb - 1), j)),
                  pl.BlockSpec((PH, CT), lambda j, i: (0, j)),
                  pl.BlockSpec((1, CT), lambda j, i: (0, j))],
        out_specs=[pl.BlockSpec((tm, CT), lambda j, i: (i, j)),
                   pl.BlockSpec((PH * SUB, CT), lambda j, i: (0, j)),
                   pl.BlockSpec((SUB, CT), lambda j, i: (0, j))],
        out_shape=[jax.ShapeDtypeStruct((t, XBC), BF16), jax.ShapeDtypeStruct((PH * SUB, XBC), F32),
                   jax.ShapeDtypeStruct((SUB, XBC), F32)],
        scratch_shapes=[pltpu.VMEM((SUB, tm + 2 * PH, CT), F32), pltpu.VMEM((SUB, te, CT), F32)],
        compiler_params=_params(2),
    )(proj, proj, proj, dxc, dxc, conv_w, conv_b)


def _ssd_common(xbc, dt, alog, ex):
    L = CHUNK
    a = -jnp.exp(alog)
    la = dt * a
    li = lax.broadcasted_iota(jnp.int32, (L, L), 0)
    si = lax.broadcasted_iota(jnp.int32, (L, L), 1)
    lower = si <= li
    tri = jnp.where(lower, 1.0, 0.0).astype(BF16)
    cs = _xdot(tri, la)
    cst = _dotx(la, tri, (((0,), (1,)), ((), ())))
    csl = cs[L - 1:L, :]
    ecs_x = _dotx2(jnp.exp(cs)[:, 0:SSM_H], ex)
    tail_x = _dotx2(jnp.exp(csl - cs)[:, 0:SSM_H], ex)
    dt_x = _dotx2(dt[:, 0:SSM_H], ex)
    return a, la, lower, tri, cs, cst, ecs_x, tail_x, dt_x


def _ssd_fwd(xbc_c, dt, a_log, ex, nb, seq):
    t = xbc_c.shape[0]
    L = CHUNK
    nc = seq // L
    GW = SSM_R * SSM_P

    def body(x_ref, dt_ref, al_ref, ex_ref, y_ref, st_ref, state):
        c = pl.program_id(1)

        @pl.when(c == 0)
        def _():
            state[...] = jnp.zeros_like(state)

        st_ref[0] = state[...]
        xbc = x_ref[...]
        _, _, lower, _, cs, cst, ecs_x, tail_x, dt_x = _ssd_common(xbc, dt_ref[...], al_ref[...], ex_ref[...])
        xd = xbc[:, 0:D_INNER] * dt_x
        xdb = xd.astype(BF16)
        xt = (xd * tail_x).astype(BF16)
        el_x = ecs_x[L - 1:L, :]
        for g in range(SSM_G):
            bg = xbc[:, D_INNER + g * SSM_N:D_INNER + (g + 1) * SSM_N].astype(BF16)
            cg = xbc[:, D_INNER + (SSM_G + g) * SSM_N:D_INNER + (SSM_G + g + 1) * SSM_N].astype(BF16)
            cb = _dot(cg, bg, NT)
            sg = state[:, g * GW:(g + 1) * GW]
            ys = _dot(cg, sg) * ecs_x[:, g * GW:(g + 1) * GW]
            for r in range(SSM_R):
                h = g * SSM_R + r
                seg = cs[:, h:h + 1] - cst[h:h + 1, :]
                dec = jnp.exp(jnp.where(lower, seg, -1e30))
                yh = _dot(cb * dec, xdb[:, h * SSM_P:(h + 1) * SSM_P])
                y_ref[:, h * SSM_P:(h + 1) * SSM_P] = yh + ys[:, r * SSM_P:(r + 1) * SSM_P]
            state[:, g * GW:(g + 1) * GW] = sg * el_x[:, g * GW:(g + 1) * GW] + _dot(bg, xt[:, g * GW:(g + 1) * GW], TN)

    return _pallas(
        body, name="ssd_fwd", grid=(nb, nc),
        in_specs=[pl.BlockSpec((L, XBC), lambda b, c: (b * nc + c, 0)),
                  pl.BlockSpec((L, LANE), lambda b, c: (b * nc + c, 0)),
                  pl.BlockSpec((1, LANE), lambda b, c: (0, 0)),
                  pl.BlockSpec((SSM_H, D_INNER), lambda b, c: (0, 0))],
        out_specs=[pl.BlockSpec((L, D_INNER), lambda b, c: (b * nc + c, 0)),
                   pl.BlockSpec((1, SSM_N, D_INNER), lambda b, c: (b * nc + c, 0, 0))],
        out_shape=[jax.ShapeDtypeStruct((t, D_INNER), F32),
                   jax.ShapeDtypeStruct((nb * nc, SSM_N, D_INNER), F32)],
        scratch_shapes=[pltpu.VMEM((SSM_N, D_INNER), F32)], compiler_params=_params(2),
    )(xbc_c, dt, a_log, ex)


def _ssd_bwd(xbc_c, dt, a_log, ex, ext, states, dy, d_x, nb, seq):
    t = xbc_c.shape[0]
    L = CHUNK
    nc = seq // L
    GW = SSM_R * SSM_P

    def body(x_ref, dt_ref, al_ref, ex_ref, ext_ref, st_ref, dy_ref, sk_ref, dx_ref, ddt_ref, da_ref,
             dstate, dxd, yd, lastv):
        b = pl.program_id(0)
        c = pl.program_id(1)

        @pl.when(c == 0)
        def _():
            dstate[...] = jnp.zeros_like(dstate)

        @pl.when(jnp.logical_and(b == 0, c == 0))
        def _():
            da_ref[...] = jnp.zeros_like(da_ref)

        xbc = x_ref[...]
        dtv = dt_ref[...]
        ex_t = ext_ref[...]
        a, la, lower, tri, cs, cst, ecs_x, tail_x, dt_x = _ssd_common(xbc, dtv, al_ref[...], ex_ref[...])
        xs = xbc[:, 0:D_INNER]
        xd = xs * dt_x
        xdb = xd.astype(BF16)
        dyv = dy_ref[...]
        dyb = dyv.astype(BF16)
        dys = dyv * ecs_x
        xt = xd * tail_x
        el_x = ecs_x[L - 1:L, :]
        lane = lax.broadcasted_iota(jnp.int32, (L, LANE), 1)
        sub = lax.broadcasted_iota(jnp.int32, (LANE, L), 0)
        row_part = jnp.zeros((L, LANE), F32)
        col_part = jnp.zeros((LANE, L), F32)
        for g in range(SSM_G):
            gs = slice(g * GW, (g + 1) * GW)
            bcol = slice(D_INNER + g * SSM_N, D_INNER + (g + 1) * SSM_N)
            ccol = slice(D_INNER + (SSM_G + g) * SSM_N, D_INNER + (SSM_G + g + 1) * SSM_N)
            bg = xbc[:, bcol].astype(BF16)
            cg = xbc[:, ccol].astype(BF16)
            cb = _dot(cg, bg, NT)
            sg = st_ref[0, :, gs]
            dsg = dstate[:, gs]
            dc = _dot(dys[:, gs], sg, NT)
            db = _dot(xt[:, gs], dsg, NT)
            dx_state = tail_x[:, gs] * _dot(bg, dsg)
            yd[:, gs] = dys[:, gs] * _dot(cg, sg) - xd[:, gs] * dx_state
            s_out = sg * el_x[:, gs] + _dot(bg, xt[:, gs], TN)
            lastv[:, gs] = jnp.broadcast_to(jnp.sum(dsg * s_out, axis=0, keepdims=True), (8, GW))
            dcb = jnp.zeros((L, L), F32)
            for r in range(SSM_R):
                h = g * SSM_R + r
                hs = slice(h * SSM_P, (h + 1) * SSM_P)
                seg = cs[:, h:h + 1] - cst[h:h + 1, :]
                dec = jnp.exp(jnp.where(lower, seg, -1e30))
                m = cb * dec
                dm = _dot(dyb[:, hs], xdb[:, hs], NT)
                dcb = dcb + dm * dec
                e = dm * m
                row_part = row_part + jnp.where(lane == h, jnp.sum(e, axis=1, keepdims=True), 0.0)
                col_part = col_part + jnp.where(sub == h, jnp.sum(e, axis=0, keepdims=True), 0.0)
                dxd[:, hs] = _dot(m, dyb[:, hs], TN) + dx_state[:, r * SSM_P:(r + 1) * SSM_P]
            dx_ref[:, bcol] = db + _dot(dcb, cg, TN)
            dx_ref[:, ccol] = dc + _dot(dcb, bg)
            dstate[:, gs] = dsg * el_x[:, gs] + _dot(cg, dys[:, gs], TN)
        dxv = dxd[...]
        dx_ref[:, 0:D_INNER] = dxv * dt_x + dyv * sk_ref[...]
        ddt_x = _dotx2(dxv * xs, ex_t)
        yst = _dotx2(yd[...], ex_t)
        lst = _dotx2(lastv[...], ex_t)[0:1, :]
        rows = lax.broadcasted_iota(jnp.int32, (L, LANE), 0)
        dcs = row_part - col_part.T + yst + jnp.where(rows == L - 1, lst, 0.0)
        li = lax.broadcasted_iota(jnp.int32, (L, L), 0)
        si = lax.broadcasted_iota(jnp.int32, (L, L), 1)
        upper = jnp.where(si >= li, 1.0, 0.0).astype(BF16)
        dla = _xdot(upper, dcs)
        ddt_ref[...] = dla * a + ddt_x
        da_ref[...] += jnp.sum(dla * dtv, axis=0, keepdims=True)

    def row(w):
        return pl.BlockSpec((L, w), lambda b, c: (b * nc + nc - 1 - c, 0))

    return _pallas(
        body, name="ssd_bwd", grid=(nb, nc),
        in_specs=[row(XBC), row(LANE), pl.BlockSpec((1, LANE), lambda b, c: (0, 0)),
                  pl.BlockSpec((SSM_H, D_INNER), lambda b, c: (0, 0)),
                  pl.BlockSpec((D_INNER, LANE), lambda b, c: (0, 0)),
                  pl.BlockSpec((1, SSM_N, D_INNER), lambda b, c: (b * nc + nc - 1 - c, 0, 0)),
                  row(D_INNER), pl.BlockSpec((1, D_INNER), lambda b, c: (0, 0))],
        out_specs=[row(XBC), row(LANE), pl.BlockSpec((1, LANE), lambda b, c: (0, 0))],
        out_shape=[jax.ShapeDtypeStruct((t, XBC), F32), jax.ShapeDtypeStruct((t, LANE), F32),
                   jax.ShapeDtypeStruct((1, LANE), F32)],
        scratch_shapes=[pltpu.VMEM((SSM_N, D_INNER), F32), pltpu.VMEM((L, D_INNER), F32),
                        pltpu.VMEM((L, D_INNER), F32), pltpu.VMEM((8, D_INNER), F32)],
        compiler_params=_params(2),
    )(xbc_c, dt, a_log, ex, ext, states, dy, d_x)


def _group_rms(y2):
    gw = D_INNER // SSM_G
    parts = []
    for g in range(SSM_G):
        v = y2[:, g * gw:(g + 1) * gw]
        r = lax.rsqrt(jnp.mean(v * v, axis=1, keepdims=True) + EPS)
        parts.append(jnp.broadcast_to(r, v.shape))
    return jnp.concatenate(parts, axis=1)


def _gate_fwd(y, xbc_c, proj, d_x, gn_w, *, tm=256):
    t = y.shape[0]

    def body(y_ref, x_ref, z_ref, d_ref, w_ref, o_ref):
        y1 = y_ref[...] + d_ref[...] * x_ref[...]
        zv = z_ref[...]
        y2 = y1 * zv * _sigmoid(zv)
        o_ref[...] = (y2 * _group_rms(y2) * w_ref[...]).astype(BF16)

    row = pl.BlockSpec((tm, D_INNER), lambda i: (i, 0))
    vec = pl.BlockSpec((1, D_INNER), lambda i: (0, 0))
    return _pallas(
        body, name="gate_fwd", grid=(t // tm,), in_specs=[row, row, row, vec, vec], out_specs=row,
        out_shape=jax.ShapeDtypeStruct((t, D_INNER), BF16), compiler_params=_params(1),
    )(y, xbc_c, proj, d_x, gn_w)


def _gate_bwd(dyg, y, xbc_c, proj, d_x, gn_w, *, tm=256):
    t = y.shape[0]
    gw = D_INNER // SSM_G

    def body(dg_ref, y_ref, x_ref, z_ref, d_ref, w_ref, dy_ref, dz_ref, dw_ref, dd_ref):
        i = pl.program_id(0)
        xv = x_ref[...]
        dxv = d_ref[...]
        y1 = y_ref[...] + dxv * xv
        zv = z_ref[...]
        sz = _sigmoid(zv)
        y2 = y1 * zv * sz
        rr = _group_rms(y2)
        xh = y2 * rr
        dg = _f32(dg_ref[...])
        gq = dg * w_ref[...]
        prod = gq * xh
        means = []
        for g in range(SSM_G):
            mg = jnp.mean(prod[:, g * gw:(g + 1) * gw], axis=1, keepdims=True)
            means.append(jnp.broadcast_to(mg, (tm, gw)))
        dy2 = rr * (gq - xh * jnp.concatenate(means, axis=1))
        dy1 = dy2 * zv * sz
        dy_ref[...] = dy1
        dz_ref[...] = (dy2 * y1 * _dsilu(zv, sz)).astype(BF16)

        @pl.when(i == 0)
        def _():
            dw_ref[...] = jnp.zeros_like(dw_ref)
            dd_ref[...] = jnp.zeros_like(dd_ref)

        dw_ref[...] += jnp.sum(dg * xh, axis=0, keepdims=True)
        dd_ref[...] += jnp.sum(dy1 * xv, axis=0, keepdims=True)

    row = pl.BlockSpec((tm, D_INNER), lambda i: (i, 0))
    vec = pl.BlockSpec((1, D_INNER), lambda i: (0, 0))
    return _pallas(
        body, name="gate_bwd", grid=(t // tm,), in_specs=[row, row, row, row, vec, vec],
        out_specs=[row, row, vec, vec],
        out_shape=[jax.ShapeDtypeStruct((t, D_INNER), F32),
                   jax.ShapeDtypeStruct((t, D_INNER), BF16), jax.ShapeDtypeStruct((1, D_INNER), F32),
                   jax.ShapeDtypeStruct((1, D_INNER), F32)],
        compiler_params=_params(1),
    )(dyg, y, xbc_c, proj, d_x, gn_w)


ANY = pl.BlockSpec(memory_space=pl.ANY)


def _remote(src, dst, sems, k, to):
    send_sems, recv_sems = sems
    return pltpu.make_async_remote_copy(src_ref=src, dst_ref=dst, send_sem=send_sems.at[k], recv_sem=recv_sems.at[k],
                                        device_id=to, device_id_type=MESH)


def _gather_shards(w_in0, w_in1, w_out0, w_out1, small):
    nchip = 4
    dm, n0 = w_in0.shape
    n1 = w_in1.shape[1]
    ro = w_out0.shape[0]
    hr, ho = dm // 2, ro // 2

    def body(a0, a1, b0, b1, sm, o0, o1, p0, p1, osm, ici_s, ici_r, d2d_s, d2d_r):
        x, y, c = lax.axis_index("x"), lax.axis_index("y"), lax.axis_index("c")
        me = 2 * x + y
        sib = (x, y, 1 - c)
        peers = [(1 - x, y), (x, 1 - y), (1 - x, 1 - y)]

        def region(chip, half):
            col = pl.multiple_of(chip * n0, LANE)
            return [o0.at[pl.ds(half * hr, hr), pl.ds(col, n0)], o1.at[chip, pl.ds(half * hr, hr), :],
                    p0.at[pl.ds(chip * ro + half * ho, ho), :], p1.at[pl.ds(chip * ro + half * ho, ho), :]]

        halves = [a0.at[pl.ds(c * hr, hr), :], a1.at[pl.ds(c * hr, hr), :],
                  b0.at[pl.ds(c * ho, ho), :], b1.at[pl.ds(c * ho, ho), :]]
        sends = []
        for k, (px, py) in enumerate(peers):
            to = (px, py, c)
            for j, (s, d) in enumerate(zip(halves, region(me, c))):
                sends.append(_remote(s, d, (ici_s, ici_r), 5 * k + j, to))
            sends.append(_remote(sm, osm.at[me], (ici_s, ici_r), 5 * k + 4, to))
        for cp in sends:
            cp.start()
        for k, (px, py) in enumerate(peers):
            q = 2 * px + py
            for j, d in enumerate(region(q, c)):
                _remote(d, d, (ici_s, ici_r), 5 * k + j, (px, py, c)).wait_recv()
                fwd = _remote(d, d, (d2d_s, d2d_r), 4 * k + j, sib)
                fwd.start()
                sends.append(fwd)
            _remote(sm, osm.at[q], (ici_s, ici_r), 5 * k + 4, (px, py, c)).wait_recv()
        for k, (px, py) in enumerate(peers):
            for j, d in enumerate(region(2 * px + py, 1 - c)):
                _remote(d, d, (d2d_s, d2d_r), 4 * k + j, sib).wait_recv()
        for cp in sends:
            cp.wait_send()

    return _pallas(
        body, name="gather_shards", in_specs=[ANY] * 5, out_specs=[ANY] * 5,
        out_shape=[jax.ShapeDtypeStruct((dm, nchip * n0), w_in0.dtype),
                   jax.ShapeDtypeStruct((nchip, dm, n1), w_in1.dtype),
                   jax.ShapeDtypeStruct((nchip * ro, w_out0.shape[1]), w_out0.dtype),
                   jax.ShapeDtypeStruct((nchip * ro, w_out1.shape[1]), w_out1.dtype),
                   jax.ShapeDtypeStruct((nchip,) + small.shape, small.dtype)],
        scratch_shapes=[pltpu.SemaphoreType.DMA((15,)), pltpu.SemaphoreType.DMA((15,)),
                        pltpu.SemaphoreType.DMA((12,)), pltpu.SemaphoreType.DMA((12,))],
    )(w_in0, w_in1, w_out0, w_out1, small)


def _pair_exchange(g_in0, g_in1, g_out0, g_out1, gsmall, grep):
    dm = g_in0.shape[0]
    hr = dm // 2
    ho = g_out0.shape[1] // 2

    def body(a0, a1, b0, b1, sm, rp, q0, q1, r0, r1, osm, orp, pair_s, pair_r, send_sems, recv_sems, local_sems):
        x, y, c = lax.axis_index("x"), lax.axis_index("y"), lax.axis_index("c")
        me = 4 * x + 2 * y + c
        chip = 2 * x + y
        sib = (x, y, 1 - c)
        rows = pl.ds(pl.multiple_of((1 - c) * hr, 8), hr)
        orows = pl.ds(pl.multiple_of((1 - c) * ho, 8), ho)
        pair = [_remote(a0.at[rows, :], q0, (pair_s, pair_r), 0, sib),
                _remote(a1.at[:, rows, :], q1, (pair_s, pair_r), 1, sib),
                _remote(b0.at[:, orows, :], r0, (pair_s, pair_r), 2, sib),
                _remote(b1.at[:, orows, :], r1, (pair_s, pair_r), 3, sib)]
        for cp in pair:
            cp.start()
        own = [pltpu.make_async_copy(sm.at[chip], osm.at[me], local_sems.at[0]),
               pltpu.make_async_copy(rp, orp.at[me], local_sems.at[1])]
        for cp in own:
            cp.start()
        peers = []
        for k in range(7):
            fx, fy, fc = ((k + 1) >> 2) & 1, ((k + 1) >> 1) & 1, (k + 1) & 1
            peers.append((1 - x if fx else x, 1 - y if fy else y, 1 - c if fc else c))
        sends = []
        for k, (px, py, pc) in enumerate(peers):
            sends.append(_remote(sm.at[2 * px + py], osm.at[me], (send_sems, recv_sems), 2 * k, (px, py, pc)))
            sends.append(_remote(rp, orp.at[me], (send_sems, recv_sems), 2 * k + 1, (px, py, pc)))
        for cp in sends:
            cp.start()
        for k, (px, py, pc) in enumerate(peers):
            slot = 4 * px + 2 * py + pc
            _remote(sm.at[chip], osm.at[slot], (send_sems, recv_sems), 2 * k, (px, py, pc)).wait_recv()
            _remote(rp, orp.at[slot], (send_sems, recv_sems), 2 * k + 1, (px, py, pc)).wait_recv()
        for cp in pair:
            cp.wait_recv()
        for cp in pair + sends:
            cp.wait_send()
        for cp in own:
            cp.wait()

    return _pallas(
        body, name="pair_exchange", in_specs=[ANY] * 6, out_specs=[ANY] * 6,
        out_shape=[jax.ShapeDtypeStruct((hr, g_in0.shape[1]), F32),
                   jax.ShapeDtypeStruct((g_in1.shape[0], hr, g_in1.shape[2]), F32),
                   jax.ShapeDtypeStruct((g_out0.shape[0], ho, g_out0.shape[2]), F32),
                   jax.ShapeDtypeStruct((g_out1.shape[0], ho, g_out1.shape[2]), F32),
                   jax.ShapeDtypeStruct((8,) + gsmall.shape[1:], F32),
                   jax.ShapeDtypeStruct((8,) + grep.shape, F32)],
        scratch_shapes=[pltpu.SemaphoreType.DMA((4,)), pltpu.SemaphoreType.DMA((4,)),
                        pltpu.SemaphoreType.DMA((14,)), pltpu.SemaphoreType.DMA((14,)),
                        pltpu.SemaphoreType.DMA((2,))],
    )(g_in0, g_in1, g_out0, g_out1, gsmall, grep)


def _core_index():
    return lax.axis_index("c").astype(jnp.int32).reshape(1)


def _half_add(full, other, *, axis, block, name):
    nd = full.ndim
    nblk = other.shape[axis] // block[axis]
    grid = tuple(other.shape[d] // block[d] for d in range(nd))

    def body(c_ref, f_ref, o_ref, out_ref):
        out_ref[...] = (f_ref[...] + o_ref[...]).astype(BF16)

    def full_map(*idx):
        ids, c_ref = list(idx[:nd]), idx[nd]
        ids[axis] = ids[axis] + c_ref[0] * nblk
        return tuple(ids)

    def plain_map(*idx):
        return tuple(idx[:nd])

    return _pallas(
        body, name=name,
        grid_spec=pltpu.PrefetchScalarGridSpec(
            num_scalar_prefetch=1, grid=grid,
            in_specs=[pl.BlockSpec(block, full_map), pl.BlockSpec(block, plain_map)],
            out_specs=pl.BlockSpec(block, plain_map)),
        out_shape=jax.ShapeDtypeStruct(other.shape, BF16), compiler_params=_params(nd),
    )(_core_index(), full, other)


def _chip_exchange(s_in0, s_in1, s_out0, s_out1):
    npeer = 3
    n0 = s_in0.shape[1] // 4

    def body(a0, a1, b0, b1, l0, l1, m0, m1, send_sems, recv_sems):
        x, y, c = lax.axis_index("x"), lax.axis_index("y"), lax.axis_index("c")
        me = 2 * x + y
        peers = [(1 - x, y), (x, 1 - y), (1 - x, 1 - y)]

        def pieces(chip):
            return [a0.at[:, pl.ds(pl.multiple_of(chip * n0, LANE), n0)], a1.at[chip], b0.at[chip], b1.at[chip]]

        def slots(k):
            return [l0.at[k], l1.at[k], m0.at[k], m1.at[k]]

        sends = []
        for k, (px, py) in enumerate(peers):
            for j, (s, d) in enumerate(zip(pieces(2 * px + py), slots(k))):
                sends.append(_remote(s, d, (send_sems, recv_sems), 4 * k + j, (px, py, c)))
        for cp in sends:
            cp.start()
        for k, (px, py) in enumerate(peers):
            for j, (s, d) in enumerate(zip(pieces(me), slots(k))):
                _remote(s, d, (send_sems, recv_sems), 4 * k + j, (px, py, c)).wait_recv()
        for cp in sends:
            cp.wait_send()

    return _pallas(
        body, name="chip_exchange", in_specs=[ANY] * 4, out_specs=[ANY] * 4,
        out_shape=[jax.ShapeDtypeStruct((npeer, s_in0.shape[0], n0), BF16),
                   jax.ShapeDtypeStruct((npeer,) + s_in1.shape[1:], BF16),
                   jax.ShapeDtypeStruct((npeer,) + s_out0.shape[1:], BF16),
                   jax.ShapeDtypeStruct((npeer,) + s_out1.shape[1:], BF16)],
        scratch_shapes=[pltpu.SemaphoreType.DMA((12,)), pltpu.SemaphoreType.DMA((12,))],
    )(s_in0, s_in1, s_out0, s_out1)


def _chip_index():
    return (2 * lax.axis_index("x") + lax.axis_index("y")).astype(jnp.int32).reshape(1)


def _chip_sum(own, slots, *, own_block, own_map, block, name):
    npeer = slots.shape[0]
    shape = slots.shape[1:]
    grid = (shape[0] // block[0], shape[1] // block[1])

    def body(p_ref, own_ref, s_ref, o_ref):
        acc = own_ref[...].reshape(block).astype(F32)
        for q in range(npeer):
            acc = acc + s_ref[q].astype(F32)
        o_ref[...] = acc

    return _pallas(
        body, name=name,
        grid_spec=pltpu.PrefetchScalarGridSpec(
            num_scalar_prefetch=1, grid=grid,
            in_specs=[pl.BlockSpec(own_block, own_map),
                      pl.BlockSpec((npeer,) + block, lambda i, j, p: (0, i, j))],
            out_specs=pl.BlockSpec(block, lambda i, j, p: (i, j))),
        out_shape=jax.ShapeDtypeStruct(shape, F32), compiler_params=_params(2),
    )(_chip_index(), own, slots)


def _pair_share(r_in0, r_in1, r_out0, r_out1):
    def body(a0, a1, b0, b1, g0, g1, h0, h1, send_sems, recv_sems):
        x, y, c = lax.axis_index("x"), lax.axis_index("y"), lax.axis_index("c")
        sib = (x, y, 1 - c)
        sends = [_remote(s, d, (send_sems, recv_sems), j, sib)
                 for j, (s, d) in enumerate(zip([a0, a1, b0, b1], [g0, g1, h0, h1]))]
        for cp in sends:
            cp.start()
        for cp in sends:
            cp.wait()

    return _pallas(
        body, name="pair_share", in_specs=[ANY] * 4, out_specs=[ANY] * 4,
        out_shape=[jax.ShapeDtypeStruct(r.shape, F32) for r in (r_in0, r_in1, r_out0, r_out1)],
        scratch_shapes=[pltpu.SemaphoreType.DMA((4,)), pltpu.SemaphoreType.DMA((4,))],
    )(r_in0, r_in1, r_out0, r_out1)


def _adam_math(g, w, m, v):
    c1 = 1.0 - ADAM_B1 ** ADAM_STEP
    c2 = 1.0 - ADAM_B2 ** ADAM_STEP
    m2 = ADAM_B1 * m + (1.0 - ADAM_B1) * g
    v2 = ADAM_B2 * v + (1.0 - ADAM_B2) * (g * g)
    delta = -ADAM_LR * ((m2 / c1) / (jnp.sqrt(v2 / c2) + ADAM_EPS) + ADAM_WD * w)
    return delta, m2, v2


def _adamw_nat(g_mine, g_sib, w, m, v, *, name, tr):
    rows, cw = w.shape
    nt = g_mine.shape[0] // tr

    def body(c_ref, gm_ref, gs_ref, w_ref, m_ref, v_ref, go_ref, d_ref, nm_ref, nv_ref):
        mine = pl.program_id(0) // nt == c_ref[0]
        gv = jnp.where(mine, gm_ref[...], gs_ref[...])[:, 0:cw]
        delta, m2, v2 = _adam_math(gv, w_ref[...], m_ref[...], v_ref[...])
        go_ref[...] = gv
        d_ref[...] = delta
        nm_ref[...] = m2
        nv_ref[...] = v2

    def mine_map(i, c_ref):
        return (jnp.where(i // nt == c_ref[0], i % nt, 0), 0)

    def sib_map(i, c_ref):
        return (jnp.where(i // nt == c_ref[0], 0, i % nt), 0)

    row = pl.BlockSpec((tr, cw), lambda i, c_ref: (i, 0))
    gspec = (tr, g_mine.shape[1])
    out = jax.ShapeDtypeStruct((rows, cw), F32)
    return _pallas(
        body, name=name,
        grid_spec=pltpu.PrefetchScalarGridSpec(
            num_scalar_prefetch=1, grid=(rows // tr,),
            in_specs=[pl.BlockSpec(gspec, mine_map), pl.BlockSpec(gspec, sib_map), row, row, row],
            out_specs=[row, row, row, row]),
        out_shape=[out, out, out, out], compiler_params=_params(1),
    )(_core_index(), g_mine, g_sib, w, m, v)


def _adamw(slots, w, m, v, *, name, tr):
    nd, rows, _ = slots.shape
    c1 = 1.0 - ADAM_B1 ** ADAM_STEP
    c2 = 1.0 - ADAM_B2 ** ADAM_STEP

    def body(s_ref, w_ref, m_ref, v_ref, g_ref, d_ref, nm_ref, nv_ref):
        g = s_ref[0]
        for d in range(1, nd):
            g = g + s_ref[d]
        m2 = ADAM_B1 * m_ref[...] + (1.0 - ADAM_B1) * g
        v2 = ADAM_B2 * v_ref[...] + (1.0 - ADAM_B2) * (g * g)
        g_ref[...] = g
        nm_ref[...] = m2
        nv_ref[...] = v2
        d_ref[...] = -ADAM_LR * ((m2 / c1) / (jnp.sqrt(v2 / c2) + ADAM_EPS) + ADAM_WD * w_ref[...])

    row = pl.BlockSpec((tr, LANE), lambda i: (i, 0))
    out = jax.ShapeDtypeStruct((rows, LANE), F32)
    return _pallas(
        body, name=name, grid=(rows // tr,),
        in_specs=[pl.BlockSpec((nd, tr, LANE), lambda i: (0, i, 0)), row, row, row],
        out_specs=[row, row, row, row], out_shape=[out, out, out, out], compiler_params=_params(1),
    )(slots, w, m, v)


def _rows(a):
    return a.reshape(-1, LANE)


def _pad_rows(a, mult):
    pad = (-a.shape[0]) % mult
    return jnp.pad(a, ((0, pad), (0, 0))) if pad else a


def _pack(parts, mult):
    return _pad_rows(jnp.concatenate([_rows(p) for p in parts], axis=0), mult)


def _unpack(slab, shapes):
    out, r0 = [], 0
    for shp in shapes:
        n = 1
        for s in shp:
            n *= s
        r = n // LANE
        out.append(slab[r0:r0 + r].reshape(shp))
        r0 += r
    return out


def _pack_rep(vecs, scal):
    srow = jnp.concatenate([s.reshape(-1) for s in scal] + [jnp.zeros((LANE - 3 * SSM_H,), F32)]).reshape(1, LANE)
    return _pad_rows(jnp.concatenate([_rows(vv) for vv in vecs] + [srow], axis=0), 8)


def _unpack_rep(slab, vec_shapes, scal_shape):
    vecs, r0 = [], 0
    for shp in vec_shapes:
        vecs.append(slab[r0:r0 + 8].reshape(shp))
        r0 += 8
    srow = slab[r0]
    scal = [srow[i * SSM_H:(i + 1) * SSM_H].reshape(scal_shape) for i in range(3)]
    return vecs, scal


def kernel(x, ev_norm_w, ev_w_in, ev_dw_w, ev_dw_b, ev_ln_w, ev_ln_b, ev_w_out, od_norm_w, od_w_in, od_conv_w, od_conv_b, od_dt_bias, od_a_log, od_d, od_gnorm_w, od_w_out, final_norm_w, loss_target, m_ev_norm_w, m_ev_w_in, m_ev_dw_w, m_ev_dw_b, m_ev_ln_w, m_ev_ln_b, m_ev_w_out, m_od_norm_w, m_od_w_in, m_od_conv_w, m_od_conv_b, m_od_dt_bias, m_od_a_log, m_od_d, m_od_gnorm_w, m_od_w_out, m_final_norm_w, v_ev_norm_w, v_ev_w_in, v_ev_dw_w, v_ev_dw_b, v_ev_ln_w, v_ev_ln_b, v_ev_w_out, v_od_norm_w, v_od_w_in, v_od_conv_w, v_od_conv_b, v_od_dt_bias, v_od_a_log, v_od_d, v_od_gnorm_w, v_od_w_out, v_final_norm_w):
    nb, seq, d = x.shape
    t = nb * seq
    nchip = 4
    xf = x.reshape(t, d)
    tgt = loss_target.reshape(t, d)

    big_w = [ev_w_in[0], od_w_in[0], ev_w_out[0], od_w_out[0]]
    small_w = [ev_dw_w[0], od_norm_w[0], od_conv_w[0], od_conv_b[0], od_gnorm_w[0]]
    small_shapes = [a.shape for a in small_w]
    big_b = [a.astype(BF16) for a in big_w]
    small_slab = _pack(small_w, 8)
    w_in0, w_in1g, w_out0, w_out1, gath_small = _gather_shards(*big_b, small_slab)
    chip = 2 * lax.axis_index("x") + lax.axis_index("y")
    w_in0 = lax.dynamic_update_slice(w_in0, big_b[0], (0, chip * big_b[0].shape[1]))
    w_in1g = lax.dynamic_update_slice(w_in1g, big_b[1][None], (chip, 0, 0))
    w_out0 = lax.dynamic_update_slice(w_out0, big_b[2], (chip * big_b[2].shape[0], 0))
    w_out1 = lax.dynamic_update_slice(w_out1, big_b[3], (chip * big_b[3].shape[0], 0))
    gath_small = lax.dynamic_update_slice(gath_small, small_slab[None], (chip, 0, 0))
    per_chip = [_unpack(gath_small[p], small_shapes) for p in range(nchip)]

    def cat(idx, axis):
        return jnp.concatenate([per_chip[p][idx] for p in range(nchip)], axis=axis)

    w_in1 = jnp.pad(jnp.concatenate([w_in1g[p] for p in range(nchip)], axis=1),
                    ((0, 0), (0, IN_ODD_PAD - IN_ODD)))
    dw_w = jnp.pad(cat(0, 1), ((0, HALO - CONF_K), (0, 0)))
    n1_w = cat(1, 0).reshape(1, d)
    conv_w = jnp.pad(cat(2, 1), ((0, PH - SSM_K), (0, 0)))
    conv_b = cat(3, 0).reshape(1, XBC)
    gn_w = cat(4, 0).reshape(1, D_INNER)

    def lanes(a):
        return jnp.pad(a.reshape(1, -1), ((0, 0), (0, LANE - a.size)))

    dt_bias, a_log = lanes(od_dt_bias), lanes(od_a_log)
    d_x = jnp.repeat(od_d.reshape(-1), SSM_P).reshape(1, D_INNER)
    hid = lax.broadcasted_iota(jnp.int32, (SSM_H, D_INNER), 1) // SSM_P
    ex = (hid == lax.broadcasted_iota(jnp.int32, (SSM_H, D_INNER), 0)).astype(BF16)
    ex_t = jnp.pad(ex.T, ((0, 0), (0, LANE - SSM_H)))
    fn_w = final_norm_w.reshape(1, d)

    n0 = _rms_fwd(xf, ev_norm_w, name="rms_fwd0")
    proj0 = _matmul(n0, w_in0, mode="nn", out_dtype=BF16, bm=512, bn=1024, bk=d, name="in_proj0", n_major=True)
    y_conv, u2 = _conf_fwd(proj0, dw_w, ev_dw_b, ev_ln_w, ev_ln_b, seq)
    o_att, y_att = _sba_fwd(proj0, nb, seq)
    ycat0 = jnp.concatenate([y_conv, y_att], axis=1)
    h1 = _matmul(ycat0, w_out0, mode="nn", out_dtype=F32, bm=512, bn=d, bk=D_INNER, name="out_proj0", residual=xf)
    n1 = _rms_fwd(h1, n1_w, name="rms_fwd1")
    proj1 = _matmul(n1, w_in1, mode="nn", out_dtype=F32, bm=512, bn=768, bk=d, name="in_proj1", n_major=True)
    xbc_c = _xconv_fwd(proj1, conv_w, conv_b, seq)
    dt = _dt_fwd(proj1, dt_bias)
    y_ssd, states = _ssd_fwd(xbc_c, dt, a_log, ex, nb, seq)
    yg = _gate_fwd(y_ssd, xbc_c, proj1, d_x, gn_w)
    h2 = _matmul(yg, w_out1, mode="nn", out_dtype=F32, bm=512, bn=d, bk=D_INNER, name="out_proj1", residual=h1)
    dh2, g_fn, loss_part = _final_loss(h2, fn_w, tgt)

    dyg = _matmul(dh2, w_out1, mode="nt", out_dtype=BF16, bm=512, bn=1024, bk=d, name="d_out_proj1")
    g_w_out1 = _matmul(yg, dh2, mode="tn", out_dtype=F32, bm=1024, bn=d, bk=1024, name="dw_out_proj1")
    dy_ssd, dz, g_gn, g_dx = _gate_bwd(dyg, y_ssd, xbc_c, proj1, d_x, gn_w)
    dxbc_c, ddt, g_a = _ssd_bwd(xbc_c, dt, a_log, ex, ex_t, states, dy_ssd, d_x, nb, seq)
    dxbc, g_conv_w, g_conv_b = _xconv_bwd(proj1, dxbc_c, conv_w, conv_b, seq)
    ddt_raw, g_dt_bias = _dt_bwd(proj1, dt_bias, ddt)
    dproj1 = jnp.concatenate([dz, dxbc, ddt_raw.astype(BF16),
                              jnp.zeros((t, IN_ODD_PAD - IN_ODD - (LANE - SSM_H)), BF16)], axis=1)
    dn1 = _matmul(dproj1, w_in1, mode="nt", out_dtype=BF16, bm=1024, bn=d, bk=1792, name="d_in_proj1")
    g_w_in1 = _matmul(n1, dproj1, mode="tn", out_dtype=F32, bm=d, bn=1792, bk=1024, name="dw_in_proj1")
    dh1, g_n1 = _rms_bwd(dn1, h1, n1_w, dh2, name="rms_bwd1")

    dycat0 = _matmul(dh1, w_out0, mode="nt", out_dtype=BF16, bm=512, bn=1024, bk=d, name="d_out_proj0")
    g_w_out0 = _matmul(ycat0, dh1, mode="tn", out_dtype=F32, bm=1024, bn=d, bk=1024, name="dw_out_proj0")
    dq, dk, dv, dga = _sba_bwd(proj0, o_att, dycat0, nb, seq)
    dpc, g_dw_w, g_dw_b, g_ln_w, g_ln_b = _conf_bwd(proj0, u2, dycat0, dw_w, ev_ln_w, ev_ln_b, seq)
    dproj0 = jnp.concatenate([dpc, dq, dk.astype(BF16), dv.astype(BF16), dga], axis=1)
    dn0 = _matmul(dproj0, w_in0, mode="nt", out_dtype=BF16, bm=1024, bn=d, bk=1792, name="d_in_proj0")
    g_w_in0 = _matmul(n0, dproj0, mode="tn", out_dtype=F32, bm=d, bn=1792, bk=1024, name="dw_in_proj0")
    grad_x, g_n0 = _rms_bwd(dn0, xf, ev_norm_w, dh1, name="rms_bwd0")

    g_dw_w = g_dw_w.reshape(HALO, SUB, CONV_W).sum(axis=1)[0:CONF_K]
    g_dw_b, g_ln_w, g_ln_b = (a.sum(axis=0, keepdims=True) for a in (g_dw_b, g_ln_w, g_ln_b))
    g_conv_w = g_conv_w.reshape(PH, SUB, XBC).sum(axis=1)[0:SSM_K]
    g_conv_b = g_conv_b.sum(axis=0, keepdims=True)
    a_neg = -jnp.exp(od_a_log.reshape(-1))
    g_a_log = g_a[0, 0:SSM_H] * a_neg
    g_d = g_dx.reshape(SSM_H, SSM_P).sum(axis=1)
    n1 = IN_ODD // nchip
    n1p = -(-n1 // LANE) * LANE
    g_w_in1c = jnp.stack([jnp.pad(g_w_in1[:, p * n1:(p + 1) * n1], ((0, 0), (0, n1p - n1))) for p in range(nchip)])

    def chip_slab_small(p):
        c0, c1, c2, c3 = CONV_W // nchip, d // nchip, XBC // nchip, D_INNER // nchip
        return _pack([g_dw_w[:, p * c0:(p + 1) * c0], g_n1[0, p * c1:(p + 1) * c1],
                      g_conv_w[:, p * c2:(p + 1) * c2], g_conv_b[0, p * c2:(p + 1) * c2],
                      g_gn[0, p * c3:(p + 1) * c3]], 8)

    gsmall = jnp.stack([chip_slab_small(p) for p in range(nchip)])
    rep_vec_shapes = [ev_norm_w.shape, ev_dw_b.shape, ev_ln_w.shape, ev_ln_b.shape, final_norm_w.shape]
    grep = _pack_rep([g_n0, g_dw_b, g_ln_w, g_ln_b, g_fn], [g_dt_bias[0, 0:SSM_H], g_a_log, g_d])

    ro = D_INNER // nchip
    g_w_out0c = g_w_out0.reshape(nchip, ro, d)
    g_w_out1c = g_w_out1.reshape(nchip, ro, d)
    q_in0, q_in1, q_out0, q_out1, ssmall, srep = _pair_exchange(g_w_in0, g_w_in1c, g_w_out0c, g_w_out1c, gsmall, grep)
    s_in0 = _half_add(g_w_in0, q_in0, axis=0, block=(128, IN_EVEN), name="half_add_in0")
    s_in1 = _half_add(g_w_in1c, q_in1, axis=1, block=(1, 256, n1p), name="half_add_in1")
    s_out0 = _half_add(g_w_out0c, q_out0, axis=1, block=(1, ro // 2, d), name="half_add_out0")
    s_out1 = _half_add(g_w_out1c, q_out1, axis=1, block=(1, ro // 2, d), name="half_add_out1")
    l_in0, l_in1, l_out0, l_out1 = _chip_exchange(s_in0, s_in1, s_out0, s_out1)
    r_in0 = _chip_sum(s_in0, l_in0, own_block=(128, IN_EVEN // nchip), own_map=lambda i, j, p: (i, p[0]),
                      block=(128, IN_EVEN // nchip), name="chip_sum_in0")
    r_in1 = _chip_sum(s_in1, l_in1, own_block=(1, 256, n1p), own_map=lambda i, j, p: (p[0], i, 0),
                      block=(256, n1p), name="chip_sum_in1")
    r_out0 = _chip_sum(s_out0, l_out0, own_block=(1, ro // 2, d), own_map=lambda i, j, p: (p[0], 0, 0),
                       block=(ro // 2, d), name="chip_sum_out0")
    r_out1 = _chip_sum(s_out1, l_out1, own_block=(1, ro // 2, d), own_map=lambda i, j, p: (p[0], 0, 0),
                       block=(ro // 2, d), name="chip_sum_out1")
    big_r = [r_in0, r_in1, r_out0, r_out1]
    big_q = _pair_share(*big_r)

    big_m = [m_ev_w_in[0], m_od_w_in[0], m_ev_w_out[0], m_od_w_out[0]]
    big_v = [v_ev_w_in[0], v_od_w_in[0], v_ev_w_out[0], v_od_w_out[0]]
    big_names = ["adamw_in0", "adamw_in1", "adamw_out0", "adamw_out1"]
    out_bigs = [_adamw_nat(gm, gs, w, m, v, name=nm, tr=128)
                for gm, gs, w, m, v, nm in zip(big_r, big_q, big_w, big_m, big_v, big_names)]

    def upd(slots, ws, ms, vs, packer, name, tr):
        return _adamw(slots, packer(ws), packer(ms), packer(vs), name=name, tr=tr)

    small_m = [m_ev_dw_w[0], m_od_norm_w[0], m_od_conv_w[0], m_od_conv_b[0], m_od_gnorm_w[0]]
    small_v = [v_ev_dw_w[0], v_od_norm_w[0], v_od_conv_w[0], v_od_conv_b[0], v_od_gnorm_w[0]]
    out_small = upd(ssmall, small_w, small_m, small_v, lambda a: _pack(a, 8), "adamw_small", ssmall.shape[1])

    def rep_pack(a):
        return _pack_rep(a[0:5], a[5:8])

    rep_w = [ev_norm_w, ev_dw_b, ev_ln_w, ev_ln_b, final_norm_w, od_dt_bias, od_a_log, od_d]
    rep_m = [m_ev_norm_w, m_ev_dw_b, m_ev_ln_w, m_ev_ln_b, m_final_norm_w, m_od_dt_bias, m_od_a_log, m_od_d]
    rep_v = [v_ev_norm_w, v_ev_dw_b, v_ev_ln_w, v_ev_ln_b, v_final_norm_w, v_od_dt_bias, v_od_a_log, v_od_d]
    out_rep = upd(srep, rep_w, rep_m, rep_v, rep_pack, "adamw_rep", srep.shape[1])

    results = []
    for kind in range(4):
        bw = [o[kind].reshape((1,) + o[kind].shape) for o in out_bigs]
        sw = _unpack(out_small[kind], small_shapes)
        vecs, scal = _unpack_rep(out_rep[kind], rep_vec_shapes, od_dt_bias.shape)
        results.append([
            vecs[0], bw[0], sw[0].reshape(ev_dw_w.shape), vecs[1], vecs[2], vecs[3], bw[2],
            sw[1].reshape(od_norm_w.shape), bw[1], sw[2].reshape(od_conv_w.shape), sw[3].reshape(od_conv_b.shape),
            scal[0], scal[1], scal[2], sw[4].reshape(od_gnorm_w.shape), bw[3], vecs[4]])
    loss = lax.psum(loss_part[0, 0], ("x", "y", "c"))
    return (loss, grad_x.reshape(x.shape), *results[0], *results[1], *results[2], *results[3])
```

```python
import jax
import jax.numpy as jnp
from jax import lax
from jax.experimental import pallas as pl
from jax.experimental.pallas import tpu as pltpu

F32 = jnp.float32
BF16 = jnp.bfloat16

D_MODEL = 1024
CONV_W = 1024
ATT_W = 1024
HEAD_DIM = 128
N_HEADS = 8
CONF_K = 31
IN_EVEN = 7168
D_INNER = 2048
SSM_P = 64
SSM_H = 32
SSM_G = 4
SSM_R = SSM_H // SSM_G
SSM_N = 128
SSM_K = 4
CHUNK = 128
XBC = D_INNER + 2 * SSM_G * SSM_N
IN_ODD = D_INNER + XBC + SSM_H
IN_ODD_PAD = 5376
EPS = 1e-6
QB = 128
NEG_CUT = -100.0

ADAM_LR = 0.001
ADAM_B1 = 0.9
ADAM_B2 = 0.999
ADAM_EPS = 1e-08
ADAM_WD = 0.01
ADAM_STEP = 10

LANE = 128
VMEM_LIMIT = 56 * 1024 * 1024
MESH = pl.DeviceIdType.MESH

NN = (((1,), (0,)), ((), ()))
NT = (((1,), (1,)), ((), ()))
TN = (((0,), (0,)), ((), ()))


def _pallas(body, **kw):
    return pl.pallas_call(body, **kw)


def _params(n_axes):
    return pltpu.CompilerParams(dimension_semantics=("arbitrary",) * n_axes, vmem_limit_bytes=VMEM_LIMIT)


def _dot(a, b, dims=NN):
    return lax.dot_general(a.astype(BF16), b.astype(BF16), dims, preferred_element_type=F32)


def _parts(x):
    h = x.astype(BF16)
    r = x - h.astype(F32)
    m = r.astype(BF16)
    l = (r - m.astype(F32)).astype(BF16)
    return (h, m, l)


def _dotx(x, e01, dims=NN):
    acc = None
    for p in _parts(x):
        t = lax.dot_general(p, e01, dims, preferred_element_type=F32)
        acc = t if acc is None else acc + t
    return acc


def _dotx2(x, e01, dims=NN):
    h = x.astype(BF16)
    l = (x - h.astype(F32)).astype(BF16)
    return (lax.dot_general(h, e01, dims, preferred_element_type=F32)
            + lax.dot_general(l, e01, dims, preferred_element_type=F32))


def _xdot(e01, x, dims=NN):
    acc = None
    for p in _parts(x):
        t = lax.dot_general(e01, p, dims, preferred_element_type=F32)
        acc = t if acc is None else acc + t
    return acc


def _f32(x):
    return x.astype(F32)


def _sigmoid(x):
    return 1.0 / (1.0 + jnp.exp(-x))


def _dsilu(x, s):
    return s * (1.0 + x * (1.0 - s))


def _matmul(a, b, *, mode, out_dtype, bm, bn, bk, name, residual=None, n_major=False):
    if mode == "nn":
        (m, k), n = a.shape, b.shape[1]
        a_blk, a_map = (bm, bk), lambda i, j, kk: (i, kk)
        b_blk, b_map = (bk, bn), lambda i, j, kk: (kk, j)
        dims = NN
    elif mode == "nt":
        (m, k), n = a.shape, b.shape[0]
        a_blk, a_map = (bm, bk), lambda i, j, kk: (i, kk)
        b_blk, b_map = (bn, bk), lambda i, j, kk: (j, kk)
        dims = NT
    else:
        (k, m), n = a.shape, b.shape[1]
        a_blk, a_map = (bk, bm), lambda i, j, kk: (kk, i)
        b_blk, b_map = (bk, bn), lambda i, j, kk: (kk, j)
        dims = TN
    bm, bn, bk = min(bm, m), min(bn, n), min(bk, k)
    if mode != "nn":
        a_blk = (bm, bk) if mode == "nt" else (bk, bm)
        b_blk = (bn, bk) if mode == "nt" else (bk, bn)
    else:
        a_blk, b_blk = (bm, bk), (bk, bn)
    assert m % bm == 0 and n % bn == 0 and k % bk == 0, (name, m, n, k)
    nk = k // bk
    has_res = residual is not None

    def order(f):
        return (lambda j, i, kk: f(i, j, kk)) if n_major else f

    def body(*refs):
        a_ref, b_ref = refs[0], refs[1]
        r_ref = refs[2] if has_res else None
        o_ref = refs[2 + has_res]

        def finish(r):
            if has_res:
                r = r + r_ref[...]
            o_ref[...] = r.astype(out_dtype)

        if nk == 1:
            finish(_dot(a_ref[...], b_ref[...], dims))
            return
        acc_ref = refs[3 + has_res]
        kk = pl.program_id(2)

        @pl.when(kk == 0)
        def _():
            acc_ref[...] = jnp.zeros_like(acc_ref)

        acc_ref[...] += _dot(a_ref[...], b_ref[...], dims)

        @pl.when(kk == nk - 1)
        def _():
            finish(acc_ref[...])

    in_specs = [pl.BlockSpec(a_blk, order(a_map)), pl.BlockSpec(b_blk, order(b_map))]
    args = [a, b]
    out_map = order(lambda i, j, kk: (i, j))
    if has_res:
        in_specs.append(pl.BlockSpec((bm, bn), out_map))
        args.append(residual)
    grid = (n // bn, m // bm, nk) if n_major else (m // bm, n // bn, nk)
    return _pallas(
        body, name=name, grid=grid, in_specs=in_specs,
        out_specs=pl.BlockSpec((bm, bn), out_map),
        out_shape=jax.ShapeDtypeStruct((m, n), out_dtype),
        scratch_shapes=[pltpu.VMEM((bm, bn), F32)] if nk > 1 else [], compiler_params=_params(3),
    )(*args)


def _rms_fwd(x, w, *, name, tm=512):
    t, d = x.shape

    def body(x_ref, w_ref, o_ref):
        xv = x_ref[...]
        r = lax.rsqrt(jnp.mean(xv * xv, axis=1, keepdims=True) + EPS)
        o_ref[...] = (xv * r * w_ref[...]).astype(BF16)

    return _pallas(
        body, name=name, grid=(t // tm,),
        in_specs=[pl.BlockSpec((tm, d), lambda i: (i, 0)), pl.BlockSpec((1, d), lambda i: (0, 0))],
        out_specs=pl.BlockSpec((tm, d), lambda i: (i, 0)),
        out_shape=jax.ShapeDtypeStruct((t, d), BF16), compiler_params=_params(1),
    )(x, w)


def _rms_bwd(dn, x, w, dres, *, name, tm=512):
    t, d = x.shape

    def body(dn_ref, x_ref, w_ref, dr_ref, dx_ref, dw_ref):
        i = pl.program_id(0)
        xv = x_ref[...]
        r = lax.rsqrt(jnp.mean(xv * xv, axis=1, keepdims=True) + EPS)
        xh = xv * r
        dy = dn_ref[...].astype(F32)
        g = dy * w_ref[...]
        dx_ref[...] = dr_ref[...] + r * (g - xh * jnp.mean(g * xh, axis=1, keepdims=True))

        @pl.when(i == 0)
        def _():
            dw_ref[...] = jnp.zeros_like(dw_ref)

        dw_ref[...] += jnp.sum(dy * xh, axis=0, keepdims=True)

    row = pl.BlockSpec((tm, d), lambda i: (i, 0))
    vec = pl.BlockSpec((1, d), lambda i: (0, 0))
    return _pallas(
        body, name=name, grid=(t // tm,), in_specs=[row, row, vec, row], out_specs=[row, vec],
        out_shape=[jax.ShapeDtypeStruct((t, d), F32), jax.ShapeDtypeStruct((1, d), F32)],
        compiler_params=_params(1),
    )(dn, x, w, dres)


def _final_loss(h, w, target, *, tm=512):
    t, d = h.shape

    def body(h_ref, w_ref, t_ref, dh_ref, dw_ref, loss_ref):
        i = pl.program_id(0)
        xv = h_ref[...]
        r = lax.rsqrt(jnp.mean(xv * xv, axis=1, keepdims=True) + EPS)
        xh = xv * r
        wv = w_ref[...]
        err = xh * wv - t_ref[...]
        dy = err * (1.0 / d)
        g = dy * wv
        dh_ref[...] = r * (g - xh * jnp.mean(g * xh, axis=1, keepdims=True))

        @pl.when(i == 0)
        def _():
            dw_ref[...] = jnp.zeros_like(dw_ref)
            loss_ref[...] = jnp.zeros_like(loss_ref)

        dw_ref[...] += jnp.sum(dy * xh, axis=0, keepdims=True)
        part = jnp.sum(jnp.sum(err * err, axis=1, keepdims=True), axis=0, keepdims=True)
        loss_ref[...] += part * (0.5 / d)

    row = pl.BlockSpec((tm, d), lambda i: (i, 0))
    vec = pl.BlockSpec((1, d), lambda i: (0, 0))
    return _pallas(
        body, name="final_loss", grid=(t // tm,), in_specs=[row, vec, row],
        out_specs=[row, vec, pl.BlockSpec((1, LANE), lambda i: (0, 0))],
        out_shape=[jax.ShapeDtypeStruct((t, d), F32), jax.ShapeDtypeStruct((1, d), F32),
                   jax.ShapeDtypeStruct((1, LANE), F32)],
        compiler_params=_params(1),
    )(h, w, target)


HALO = 32


SUB = 8
RC = 16


def _make_shifts(sh_ref, rows, shifts=tuple(range(1, SUB))):
    for s in shifts:
        sh_ref[s, 0:rows, :] = sh_ref[0, s:s + rows, :]


def _shifted(sh_ref, r0, j, rows):
    return sh_ref[j % SUB, pl.ds(r0 + (j - j % SUB), rows), :]


def _taps(w8_ref, sh_ref, r0, first, step, init):
    accs = [init] * (RC // SUB)
    for k in range(CONF_K):
        wk = w8_ref[k * SUB:(k + 1) * SUB, :]
        x = _shifted(sh_ref, r0, first + step * k, RC)
        accs = [a + wk * x[q * SUB:(q + 1) * SUB] for q, a in enumerate(accs)]
    return jnp.concatenate(accs, axis=0)


def _conf_fwd(proj, dw_w, dw_b, ln_w, ln_b, seq, *, tm=256):
    t = proj.shape[0]
    c = CONV_W
    tps = seq // tm
    hb = tm // HALO

    def body(a_ref, b_ref, g_ref, ha_ref, hb_ref, w_ref, wb_ref, lw_ref, lb_ref, y_ref, u2_ref, sh_ref):
        i = pl.program_id(0)
        keep = jnp.where(i % tps == 0, 0.0, 1.0)
        sh_ref[0, 0:HALO, :] = _f32(ha_ref[...]) * _sigmoid(_f32(hb_ref[...])) * keep
        sh_ref[0, HALO:HALO + tm, :] = _f32(a_ref[...]) * _sigmoid(_f32(b_ref[...]))
        _make_shifts(sh_ref, tm + HALO - SUB)

        def chunk(ci, carry):
            r0 = pl.multiple_of(ci * RC, RC)
            acc = _taps(w_ref, sh_ref, r0, HALO - CONF_K + 1, 1, jnp.broadcast_to(wb_ref[...], (SUB, c)))
            u2_ref[pl.ds(r0, RC), :] = acc
            mu = jnp.mean(acc, axis=1, keepdims=True)
            xc = acc - mu
            rs = lax.rsqrt(jnp.mean(xc * xc, axis=1, keepdims=True) + EPS)
            u3 = xc * rs * lw_ref[...] + lb_ref[...]
            gv = _f32(g_ref[pl.ds(r0, RC), :])
            y_ref[pl.ds(r0, RC), :] = (u3 * _sigmoid(u3) * gv * _sigmoid(gv)).astype(BF16)
            return carry

        lax.fori_loop(0, tm // RC, chunk, 0, unroll=2)

    def col(j):
        return pl.BlockSpec((tm, c), lambda i: (i, j))

    def prev(j):
        return pl.BlockSpec((HALO, c), lambda i: (jnp.maximum(i * hb - 1, 0), j))

    vec = pl.BlockSpec((1, c), lambda i: (0, 0))
    return _pallas(
        body, name="conf_fwd", grid=(t // tm,),
        in_specs=[col(0), col(1), col(2), prev(0), prev(1),
                  pl.BlockSpec((HALO * SUB, c), lambda i: (0, 0)), vec, vec, vec],
        out_specs=[pl.BlockSpec((tm, c), lambda i: (i, 0)), pl.BlockSpec((tm, c), lambda i: (i, 0))],
        out_shape=[jax.ShapeDtypeStruct((t, c), BF16), jax.ShapeDtypeStruct((t, c), F32)],
        scratch_shapes=[pltpu.VMEM((SUB, tm + HALO, c), F32)], compiler_params=_params(1),
    )(proj, proj, proj, proj, proj, dw_w, dw_b, ln_w, ln_b)


def _conf_bwd(proj, u2, dycat, dw_w, ln_w, ln_b, seq, *, tm=256):
    t = proj.shape[0]
    c = CONV_W
    tps = seq // tm
    hb = tm // HALO
    nhb = t // HALO

    def fold(v):
        out = v[0:SUB]
        for q in range(1, RC // SUB):
            out = out + v[q * SUB:(q + 1) * SUB]
        return out

    def body(a_ref, b_ref, g_ref, pa_ref, pb_ref, ng_ref, u2_ref, nu2_ref, dy_ref, ndy_ref,
             w_ref, lw_ref, lb_ref,
             dp_ref, dww_ref, dwb_ref, dlw_ref, dlb_ref, su_ref, sd_ref):
        i = pl.program_id(0)
        first = i % tps == 0
        last = i % tps == tps - 1

        @pl.when(i == 0)
        def _():
            dww_ref[...] = jnp.zeros_like(dww_ref)
            dwb_ref[...] = jnp.zeros_like(dwb_ref)
            dlw_ref[...] = jnp.zeros_like(dlw_ref)
            dlb_ref[...] = jnp.zeros_like(dlb_ref)

        su_ref[0, 0:HALO, :] = _f32(pa_ref[...]) * _sigmoid(_f32(pb_ref[...])) * jnp.where(first, 0.0, 1.0)
        su_ref[0, HALO:HALO + tm, :] = _f32(a_ref[...]) * _sigmoid(_f32(b_ref[...]))
        _make_shifts(su_ref, tm + HALO - SUB)

        def ln_back(u2c, gv, dy):
            mu = jnp.mean(u2c, axis=1, keepdims=True)
            xc = u2c - mu
            rs = lax.rsqrt(jnp.mean(xc * xc, axis=1, keepdims=True) + EPS)
            xh = xc * rs
            lw = lw_ref[...]
            u3 = xh * lw + lb_ref[...]
            s3 = _sigmoid(u3)
            sg = _sigmoid(gv)
            dgc = dy * (u3 * s3) * _dsilu(gv, sg)
            du3 = dy * gv * sg * _dsilu(u3, s3)
            dxh = du3 * lw
            du2 = rs * (dxh - jnp.mean(dxh, axis=1, keepdims=True)
                        - xh * jnp.mean(dxh * xh, axis=1, keepdims=True))
            return du2, dgc, du3, xh

        def tile_chunk(ci, carry):
            r0 = pl.multiple_of(ci * RC, RC)
            rows = pl.ds(r0, RC)
            du2, dgc, du3, xh = ln_back(u2_ref[rows, :], _f32(g_ref[rows, :]), _f32(dy_ref[rows, :]))
            sd_ref[0, rows, :] = du2
            dp_ref[rows, 2 * c:3 * c] = dgc.astype(BF16)
            dwb_ref[...] += fold(du2)
            dlw_ref[...] += fold(du3 * xh)
            dlb_ref[...] += fold(du3)
            return carry

        lax.fori_loop(0, tm // RC, tile_chunk, 0, unroll=2)
        live = jnp.where(last, 0.0, 1.0)
        for ci in range(HALO // RC):
            rows = slice(ci * RC, (ci + 1) * RC)
            du2, _, _, _ = ln_back(nu2_ref[rows, :], _f32(ng_ref[rows, :]), _f32(ndy_ref[rows, :]))
            sd_ref[0, tm + ci * RC:tm + (ci + 1) * RC, :] = du2 * live
        _make_shifts(sd_ref, tm + HALO - SUB)

        def tap_chunk(ci, carry):
            r0 = pl.multiple_of(ci * RC, RC)
            rows = pl.ds(r0, RC)
            du1 = _taps(w_ref, sd_ref, r0, CONF_K - 1, -1, jnp.zeros((SUB, c), F32))
            sb = _sigmoid(_f32(b_ref[rows, :]))
            dp_ref[rows, 0:c] = (du1 * sb).astype(BF16)
            dp_ref[rows, c:2 * c] = (du1 * _f32(a_ref[rows, :]) * sb * (1.0 - sb)).astype(BF16)
            du2 = sd_ref[0, rows, :]
            for k in range(CONF_K):
                dww_ref[k * SUB:(k + 1) * SUB, :] += fold(du2 * _shifted(su_ref, r0, HALO - CONF_K + 1 + k, RC))
            return carry

        lax.fori_loop(0, tm // RC, tap_chunk, 0)

    def col(j):
        return pl.BlockSpec((tm, c), lambda i: (i, j))

    def prev(j):
        return pl.BlockSpec((HALO, c), lambda i: (jnp.maximum(i * hb - 1, 0), j))

    def nxt(j):
        return pl.BlockSpec((HALO, c), lambda i: (jnp.minimum((i + 1) * hb, nhb - 1), j))

    vec = pl.BlockSpec((1, c), lambda i: (0, 0))
    acc = pl.BlockSpec((SUB, c), lambda i: (0, 0))
    return _pallas(
        body, name="conf_bwd", grid=(t // tm,),
        in_specs=[col(0), col(1), col(2), prev(0), prev(1), nxt(2), col(0), nxt(0), col(0), nxt(0),
                  pl.BlockSpec((HALO * SUB, c), lambda i: (0, 0)), vec, vec],
        out_specs=[pl.BlockSpec((tm, 3 * c), lambda i: (i, 0)),
                   pl.BlockSpec((HALO * SUB, c), lambda i: (0, 0)), acc, acc, acc],
        out_shape=[jax.ShapeDtypeStruct((t, 3 * c), BF16), jax.ShapeDtypeStruct((HALO * SUB, c), F32),
                   jax.ShapeDtypeStruct((SUB, c), F32), jax.ShapeDtypeStruct((SUB, c), F32),
                   jax.ShapeDtypeStruct((SUB, c), F32)],
        scratch_shapes=[pltpu.VMEM((SUB, tm + HALO, c), F32), pltpu.VMEM((SUB, tm + HALO, c), F32)],
        compiler_params=_params(1),
    )(proj, proj, proj, proj, proj, proj, u2, u2, dycat, dycat, dw_w, ln_w, ln_b)


Q_COL = 3 * CONV_W // HEAD_DIM
K_COL = Q_COL + N_HEADS
V_COL = K_COL + N_HEADS
GA_COL = V_COL + N_HEADS


SBA_TQ = 256
SBA_WK = 4 * QB


def _sb_window(qs, kw, ws, limit, t0, carry):
    tq, wk = qs.shape[0], kw.shape[0]
    z = _dot(qs, kw, NT)
    sg = ws + lax.broadcasted_iota(jnp.int32, (tq, wk), 1)
    tg = t0 + lax.broadcasted_iota(jnp.int32, (tq, wk), 0)
    mask = sg < jnp.minimum(tg, limit)
    sp = jnp.log(1.0 + jnp.exp(-jnp.abs(z)))
    ls = jnp.minimum(z, 0.0) - sp
    lk = jnp.where(mask, ls - z, 0.0)
    jj = lax.broadcasted_iota(jnp.int32, (QB, QB), 0)
    ss = lax.broadcasted_iota(jnp.int32, (QB, QB), 1)
    ustrict = jnp.where(jj > ss, 1.0, 0.0).astype(BF16)
    laters = [None] * (wk // QB)
    for ch in reversed(range(wk // QB)):
        lkc = lk[:, ch * QB:(ch + 1) * QB]
        laters[ch] = carry + _dotx2(lkc, ustrict)
        carry = carry + jnp.sum(lkc, axis=1, keepdims=True)
    w = jnp.where(mask, jnp.exp(ls + jnp.concatenate(laters, axis=1)), 0.0)
    return mask, ls, w, carry


def _sba_fwd(proj, nb, seq, *, tq=SBA_TQ, wk=SBA_WK):
    t = proj.shape[0]
    wk = min(wk, seq)
    nq = seq // tq
    scale = HEAD_DIM ** -0.5

    def body(q_ref, k_ref, v_ref, g_ref, o_ref, y_ref):
        i = pl.program_id(2)
        t0 = i * tq
        qs = (_f32(q_ref[...]) * scale).astype(BF16)

        def window(ws, limit, carry, acc):
            ws = pl.multiple_of(ws, QB)
            _, _, w, carry = _sb_window(qs, k_ref[pl.ds(ws, wk), :], ws, limit, t0, carry)
            return carry, acc + _dot(w, v_ref[pl.ds(ws, wk), :])

        ws0 = jnp.maximum(t0 + tq - wk, 0)
        carry, acc = window(ws0, seq, jnp.zeros((tq, 1), F32), jnp.zeros((tq, HEAD_DIM), F32))

        def cond(st):
            return jnp.logical_and(st[0] > 0, jnp.max(st[1]) > NEG_CUT)

        def step(st):
            c2, a2 = window(jnp.maximum(st[0] - wk, 0), st[0], st[1], st[2])
            return jnp.maximum(st[0] - wk, 0), c2, a2

        _, _, acc = lax.while_loop(cond, step, (ws0, carry, acc))
        o_ref[...] = acc
        gv = _f32(g_ref[...])
        y_ref[...] = (acc * gv * _sigmoid(gv)).astype(BF16)

    def tile(c0):
        return pl.BlockSpec((tq, HEAD_DIM), lambda b, h, i: (b * nq + i, c0 + h))

    def whole(c0):
        return pl.BlockSpec((seq, HEAD_DIM), lambda b, h, i: (b, c0 + h))

    return _pallas(
        body, name="sba_fwd", grid=(nb, N_HEADS, nq),
        in_specs=[tile(Q_COL), whole(K_COL), whole(V_COL), tile(GA_COL)],
        out_specs=[tile(0), tile(0)],
        out_shape=[jax.ShapeDtypeStruct((t, ATT_W), F32), jax.ShapeDtypeStruct((t, ATT_W), BF16)],
        compiler_params=_params(3),
    )(proj, proj, proj, proj)


def _sba_bwd(proj, o, dycat, nb, seq, *, tq=SBA_TQ, wk=SBA_WK):
    t = proj.shape[0]
    wk = min(wk, seq)
    nq = seq // tq
    nwin = -(-seq // wk) + 1
    nch = wk // QB
    scale = HEAD_DIM ** -0.5

    def body(q_ref, k_ref, v_ref, g_ref, o_ref, dy_ref, dq_ref, dk_ref, dv_ref, dg_ref, e_ref, sp_ref):
        i = pl.program_id(2)
        t0 = i * tq

        @pl.when(i == 0)
        def _():
            dk_ref[...] = jnp.zeros_like(dk_ref)
            dv_ref[...] = jnp.zeros_like(dv_ref)

        qs = (_f32(q_ref[...]) * scale).astype(BF16)
        gv = _f32(g_ref[...])
        sg = _sigmoid(gv)
        dy = _f32(dy_ref[...])
        do = (dy * gv * sg).astype(BF16)
        dg_ref[...] = (dy * o_ref[...] * _dsilu(gv, sg)).astype(BF16)

        def start_of(n):
            return pl.multiple_of(jnp.maximum(t0 + tq - (n + 1) * wk, 0), QB)

        def limit_of(n):
            return jnp.where(n == 0, seq, jnp.maximum(t0 + tq - n * wk, 0))

        def near(n, carry):
            ws = start_of(n)
            _, ls, w, carry = _sb_window(qs, k_ref[pl.ds(ws, wk), :], ws, limit_of(n), t0, carry)
            e_ref[n] = w * _dot(do, v_ref[pl.ds(ws, wk), :], NT)
            sp_ref[n] = jnp.exp(ls)
            dv_ref[pl.ds(ws, wk), :] += _dot(w, do, TN)
            return carry

        carry = near(0, jnp.zeros((tq, 1), F32))

        def cond(st):
            return jnp.logical_and(start_of(st[0] - 1) > 0, jnp.max(st[1]) > NEG_CUT)

        def step(st):
            return st[0] + 1, near(st[0], st[1])

        nvis, _ = lax.while_loop(cond, step, (1, carry))

        jj = lax.broadcasted_iota(jnp.int32, (QB, QB), 0)
        ss = lax.broadcasted_iota(jnp.int32, (QB, QB), 1)
        lstrict = jnp.where(jj < ss, 1.0, 0.0).astype(BF16)

        def far(r, st):
            pre, dq = st
            n = nvis - 1 - r
            ws = start_of(n)
            e = e_ref[n]
            spn = sp_ref[n]
            gs = []
            for ch in range(nch):
                ec = e[:, ch * QB:(ch + 1) * QB]
                gs.append(pre + _dotx2(ec, lstrict))
                pre = pre + jnp.sum(ec, axis=1, keepdims=True)
            sgl = ws + lax.broadcasted_iota(jnp.int32, (tq, wk), 1)
            tgl = t0 + lax.broadcasted_iota(jnp.int32, (tq, wk), 0)
            mask = sgl < jnp.minimum(tgl, limit_of(n))
            dz = jnp.where(mask, e * (1.0 - spn) - jnp.concatenate(gs, axis=1) * spn, 0.0).astype(BF16)
            dk_ref[pl.ds(ws, wk), :] += _dot(dz, qs, TN)
            return pre, dq + _dot(dz, k_ref[pl.ds(ws, wk), :])

        _, dq = lax.fori_loop(0, nvis, far, (jnp.zeros((tq, 1), F32), jnp.zeros((tq, HEAD_DIM), F32)))
        dq_ref[...] = (dq * scale).astype(BF16)

    def tile(c0):
        return pl.BlockSpec((tq, HEAD_DIM), lambda b, h, i: (b * nq + i, c0 + h))

    def whole(c0):
        return pl.BlockSpec((seq, HEAD_DIM), lambda b, h, i: (b, c0 + h))

    return _pallas(
        body, name="sba_bwd", grid=(nb, N_HEADS, nq),
        in_specs=[tile(Q_COL), whole(K_COL), whole(V_COL), tile(GA_COL), tile(0),
                  tile(CONV_W // HEAD_DIM)],
        out_specs=[tile(0), whole(0), whole(0), tile(0)],
        out_shape=[jax.ShapeDtypeStruct((t, ATT_W), BF16), jax.ShapeDtypeStruct((t, ATT_W), F32),
                   jax.ShapeDtypeStruct((t, ATT_W), F32), jax.ShapeDtypeStruct((t, ATT_W), BF16)],
        scratch_shapes=[pltpu.VMEM((nwin, tq, wk), F32), pltpu.VMEM((nwin, tq, wk), F32)],
        compiler_params=_params(3),
    )(proj, proj, proj, proj, o, dycat)


CT = 512
PH = 8
XRC = 32
X_SHIFTS = tuple(s for s in range(PH - SSM_K + 1, PH))
D_SHIFTS = tuple(range(1, SSM_K))
Z_BLK = 0
XBC_BLK = D_INNER // CT
DT_BLK = (D_INNER + XBC) // LANE


def _softplus(x):
    return jnp.maximum(x, 0.0) + jnp.log(1.0 + jnp.exp(-jnp.abs(x)))


def _dt_fwd(proj, dt_bias, *, tm=512):
    t = proj.shape[0]

    def body(p_ref, b_ref, o_ref):
        o_ref[...] = _softplus(p_ref[...] + b_ref[...])

    return _pallas(
        body, name="dt_fwd", grid=(t // tm,),
        in_specs=[pl.BlockSpec((tm, LANE), lambda i: (i, DT_BLK)), pl.BlockSpec((1, LANE), lambda i: (0, 0))],
        out_specs=pl.BlockSpec((tm, LANE), lambda i: (i, 0)),
        out_shape=jax.ShapeDtypeStruct((t, LANE), F32), compiler_params=_params(1),
    )(proj, dt_bias)


def _dt_bwd(proj, dt_bias, ddt, *, tm=512):
    t = proj.shape[0]

    def body(p_ref, b_ref, d_ref, o_ref, db_ref):
        i = pl.program_id(0)
        lanes = lax.broadcasted_iota(jnp.int32, (tm, LANE), 1)
        dr = jnp.where(lanes < SSM_H, d_ref[...] * _sigmoid(p_ref[...] + b_ref[...]), 0.0)
        o_ref[...] = dr

        @pl.when(i == 0)
        def _():
            db_ref[...] = jnp.zeros_like(db_ref)

        db_ref[...] += jnp.sum(dr, axis=0, keepdims=True)

    vec = pl.BlockSpec((1, LANE), lambda i: (0, 0))
    row = pl.BlockSpec((tm, LANE), lambda i: (i, 0))
    return _pallas(
        body, name="dt_bwd", grid=(t // tm,),
        in_specs=[pl.BlockSpec((tm, LANE), lambda i: (i, DT_BLK)), vec, row],
        out_specs=[row, vec],
        out_shape=[jax.ShapeDtypeStruct((t, LANE), F32), jax.ShapeDtypeStruct((1, LANE), F32)],
        compiler_params=_params(1),
    )(proj, dt_bias, ddt)


def _xconv_fwd(proj, conv_w, conv_b, seq, *, tm=512):
    t = proj.shape[0]
    tps = seq // tm
    hb = tm // PH

    def body(x_ref, h_ref, w_ref, b_ref, o_ref, sh_ref):
        i = pl.program_id(1)
        sh_ref[0, 0:PH, :] = h_ref[...] * jnp.where(i % tps == 0, 0.0, 1.0)
        sh_ref[0, PH:PH + tm, :] = x_ref[...]
        _make_shifts(sh_ref, tm, X_SHIFTS)

        def chunk(ci, carry):
            r0 = pl.multiple_of(ci * XRC, XRC)
            acc = jnp.zeros((XRC, CT), F32) + b_ref[...]
            for k in range(SSM_K):
                acc = acc + w_ref[k:k + 1, :] * _shifted(sh_ref, r0, PH - SSM_K + 1 + k, XRC)
            o_ref[pl.ds(r0, XRC), :] = acc * _sigmoid(acc)
            return carry

        lax.fori_loop(0, tm // XRC, chunk, 0)

    return _pallas(
        body, name="xconv_fwd", grid=(XBC // CT, t // tm),
        in_specs=[pl.BlockSpec((tm, CT), lambda j, i: (i, XBC_BLK + j)),
                  pl.BlockSpec((PH, CT), lambda j, i: (jnp.maximum(i * hb - 1, 0), XBC_BLK + j)),
                  pl.BlockSpec((PH, CT), lambda j, i: (0, j)),
                  pl.BlockSpec((1, CT), lambda j, i: (0, j))],
        out_specs=pl.BlockSpec((tm, CT), lambda j, i: (i, j)),
        out_shape=jax.ShapeDtypeStruct((t, XBC), F32),
        scratch_shapes=[pltpu.VMEM((SUB, tm + PH, CT), F32)], compiler_params=_params(2),
    )(proj, proj, conv_w, conv_b)


def _xconv_bwd(proj, dxc, conv_w, conv_b, seq, *, tm=512):
    t = proj.shape[0]
    tps = seq // tm
    hb = tm // PH
    nhb = t // PH
    te = tm + PH

    def fold(v):
        out = v[0:SUB]
        for q in range(1, v.shape[0] // SUB):
            out = out + v[q * SUB:(q + 1) * SUB]
        return out

    def body(x_ref, p_ref, n_ref, d_ref, nd_ref, w_ref, b_ref, dx_ref, dw_ref, db_ref, sx_ref, sd_ref):
        i = pl.program_id(1)
        first = i % tps == 0
        last = i % tps == tps - 1

        @pl.when(i == 0)
        def _():
            dw_ref[...] = jnp.zeros_like(dw_ref)
            db_ref[...] = jnp.zeros_like(db_ref)

        sx_ref[0, 0:PH, :] = p_ref[...] * jnp.where(first, 0.0, 1.0)
        sx_ref[0, PH:PH + tm, :] = x_ref[...]
        sx_ref[0, PH + tm:PH + te, :] = n_ref[...]
        _make_shifts(sx_ref, te, X_SHIFTS)

        def dv_of(r0, rows, dy):
            acc = jnp.zeros((rows, CT), F32) + b_ref[...]
            for k in range(SSM_K):
                acc = acc + w_ref[k:k + 1, :] * _shifted(sx_ref, r0, PH - SSM_K + 1 + k, rows)
            return dy * _dsilu(acc, _sigmoid(acc))

        def dv_chunk(ci, carry):
            r0 = pl.multiple_of(ci * XRC, XRC)
            dv = dv_of(r0, XRC, d_ref[pl.ds(r0, XRC), :])
            sd_ref[0, pl.ds(r0, XRC), :] = dv
            db_ref[...] += fold(dv)
            return carry

        lax.fori_loop(0, tm // XRC, dv_chunk, 0)
        sd_ref[0, tm:te, :] = dv_of(tm, PH, nd_ref[...]) * jnp.where(last, 0.0, 1.0)
        _make_shifts(sd_ref, tm, D_SHIFTS)

        def tap_chunk(ci, carry):
            r0 = pl.multiple_of(ci * XRC, XRC)
            dx = jnp.zeros((XRC, CT), F32)
            for k in range(SSM_K):
                dx = dx + w_ref[k:k + 1, :] * _shifted(sd_ref, r0, SSM_K - 1 - k, XRC)
            dx_ref[pl.ds(r0, XRC), :] = dx.astype(BF16)
            dv = sd_ref[0, pl.ds(r0, XRC), :]
            for k in range(SSM_K):
                dw_ref[k * SUB:(k + 1) * SUB, :] += fold(dv * _shifted(sx_ref, r0, PH - SSM_K + 1 + k, XRC))
            return carry

        lax.fori_loop(0, tm // XRC, tap_chunk, 0)

    return _pallas(
        body, name="xconv_bwd", grid=(XBC // CT, t // tm),
        in_specs=[pl.BlockSpec((tm, CT), lambda j, i: (i, XBC_BLK + j)),
                  pl.BlockSpec((PH, CT), lambda j, i: (jnp.maximum(i * hb - 1, 0), XBC_BLK + j)),
                  pl.BlockSpec((PH, CT), lambda j, i: (jnp.minimum((i + 1) * hb, nhb - 1), XBC_BLK + j)),
                  pl.BlockSpec((tm, CT), lambda j, i: (i, j)),
                  pl.BlockSpec((PH, CT), lambda j, i: (jnp.minimum((i + 1) * hb, nhb - 1), j)),
                  pl.BlockSpec((PH, CT), lambda j, i: (0, j)),
                  pl.BlockSpec((1, CT), lambda j, i: (0, j))],
        out_specs=[pl.BlockSpec((tm, CT), lambda j, i: (i, j)),
                   pl.BlockSpec((PH * SUB, CT), lambda j, i: (0, j)),
                   pl.BlockSpec((SUB, CT), lambda j, i: (0, j))],
        out_shape=[jax.ShapeDtypeStruct((t, XBC), BF16), jax.ShapeDtypeStruct((PH * SUB, XBC), F32),
                   jax.ShapeDtypeStruct((SUB, XBC), F32)],
        scratch_shapes=[pltpu.VMEM((SUB, tm + 2 * PH, CT), F32), pltpu.VMEM((SUB, te, CT), F32)],
        compiler_params=_params(2),
    )(proj, proj, proj, dxc, dxc, conv_w, conv_b)


def _ssd_common(xbc, dt, alog, ex):
    L = CHUNK
    a = -jnp.exp(alog)
    la = dt * a
    li = lax.broadcasted_iota(jnp.int32, (L, L), 0)
    si = lax.broadcasted_iota(jnp.int32, (L, L), 1)
    lower = si <= li
    tri = jnp.where(lower, 1.0, 0.0).astype(BF16)
    cs = _xdot(tri, la)
    cst = _dotx(la, tri, (((0,), (1,)), ((), ())))
    csl = cs[L - 1:L, :]
    ecs_x = _dotx2(jnp.exp(cs)[:, 0:SSM_H], ex)
    tail_x = _dotx2(jnp.exp(csl - cs)[:, 0:SSM_H], ex)
    dt_x = _dotx2(dt[:, 0:SSM_H], ex)
    return a, la, lower, tri, cs, cst, ecs_x, tail_x, dt_x


def _ssd_fwd(xbc_c, dt, a_log, ex, nb, seq):
    t = xbc_c.shape[0]
    L = CHUNK
    nc = seq // L
    GW = SSM_R * SSM_P

    def body(x_ref, dt_ref, al_ref, ex_ref, y_ref, st_ref, state):
        c = pl.program_id(1)

        @pl.when(c == 0)
        def _():
            state[...] = jnp.zeros_like(state)

        st_ref[0] = state[...]
        xbc = x_ref[...]
        _, _, lower, _, cs, cst, ecs_x, tail_x, dt_x = _ssd_common(xbc, dt_ref[...], al_ref[...], ex_ref[...])
        xd = xbc[:, 0:D_INNER] * dt_x
        xdb = xd.astype(BF16)
        xt = (xd * tail_x).astype(BF16)
        el_x = ecs_x[L - 1:L, :]
        for g in range(SSM_G):
            bg = xbc[:, D_INNER + g * SSM_N:D_INNER + (g + 1) * SSM_N].astype(BF16)
            cg = xbc[:, D_INNER + (SSM_G + g) * SSM_N:D_INNER + (SSM_G + g + 1) * SSM_N].astype(BF16)
            cb = _dot(cg, bg, NT)
            sg = state[:, g * GW:(g + 1) * GW]
            ys = _dot(cg, sg) * ecs_x[:, g * GW:(g + 1) * GW]
            for r in range(SSM_R):
                h = g * SSM_R + r
                seg = cs[:, h:h + 1] - cst[h:h + 1, :]
                dec = jnp.exp(jnp.where(lower, seg, -1e30))
                yh = _dot(cb * dec, xdb[:, h * SSM_P:(h + 1) * SSM_P])
                y_ref[:, h * SSM_P:(h + 1) * SSM_P] = yh + ys[:, r * SSM_P:(r + 1) * SSM_P]
            state[:, g * GW:(g + 1) * GW] = sg * el_x[:, g * GW:(g + 1) * GW] + _dot(bg, xt[:, g * GW:(g + 1) * GW], TN)

    return _pallas(
        body, name="ssd_fwd", grid=(nb, nc),
        in_specs=[pl.BlockSpec((L, XBC), lambda b, c: (b * nc + c, 0)),
                  pl.BlockSpec((L, LANE), lambda b, c: (b * nc + c, 0)),
                  pl.BlockSpec((1, LANE), lambda b, c: (0, 0)),
                  pl.BlockSpec((SSM_H, D_INNER), lambda b, c: (0, 0))],
        out_specs=[pl.BlockSpec((L, D_INNER), lambda b, c: (b * nc + c, 0)),
                   pl.BlockSpec((1, SSM_N, D_INNER), lambda b, c: (b * nc + c, 0, 0))],
        out_shape=[jax.ShapeDtypeStruct((t, D_INNER), F32),
                   jax.ShapeDtypeStruct((nb * nc, SSM_N, D_INNER), F32)],
        scratch_shapes=[pltpu.VMEM((SSM_N, D_INNER), F32)], compiler_params=_params(2),
    )(xbc_c, dt, a_log, ex)


def _ssd_bwd(xbc_c, dt, a_log, ex, ext, states, dy, d_x, nb, seq):
    t = xbc_c.shape[0]
    L = CHUNK
    nc = seq // L
    GW = SSM_R * SSM_P

    def body(x_ref, dt_ref, al_ref, ex_ref, ext_ref, st_ref, dy_ref, sk_ref, dx_ref, ddt_ref, da_ref,
             dstate, dxd, yd, lastv):
        b = pl.program_id(0)
        c = pl.program_id(1)

        @pl.when(c == 0)
        def _():
            dstate[...] = jnp.zeros_like(dstate)

        @pl.when(jnp.logical_and(b == 0, c == 0))
        def _():
            da_ref[...] = jnp.zeros_like(da_ref)

        xbc = x_ref[...]
        dtv = dt_ref[...]
        ex_t = ext_ref[...]
        a, la, lower, tri, cs, cst, ecs_x, tail_x, dt_x = _ssd_common(xbc, dtv, al_ref[...], ex_ref[...])
        xs = xbc[:, 0:D_INNER]
        xd = xs * dt_x
        xdb = xd.astype(BF16)
        dyv = dy_ref[...]
        dyb = dyv.astype(BF16)
        dys = dyv * ecs_x
        xt = xd * tail_x
        el_x = ecs_x[L - 1:L, :]
        lane = lax.broadcasted_iota(jnp.int32, (L, LANE), 1)
        sub = lax.broadcasted_iota(jnp.int32, (LANE, L), 0)
        row_part = jnp.zeros((L, LANE), F32)
        col_part = jnp.zeros((LANE, L), F32)
        for g in range(SSM_G):
            gs = slice(g * GW, (g + 1) * GW)
            bcol = slice(D_INNER + g * SSM_N, D_INNER + (g + 1) * SSM_N)
            ccol = slice(D_INNER + (SSM_G + g) * SSM_N, D_INNER + (SSM_G + g + 1) * SSM_N)
            bg = xbc[:, bcol].astype(BF16)
            cg = xbc[:, ccol].astype(BF16)
            cb = _dot(cg, bg, NT)
            sg = st_ref[0, :, gs]
            dsg = dstate[:, gs]
            dc = _dot(dys[:, gs], sg, NT)
            db = _dot(xt[:, gs], dsg, NT)
            dx_state = tail_x[:, gs] * _dot(bg, dsg)
            yd[:, gs] = dys[:, gs] * _dot(cg, sg) - xd[:, gs] * dx_state
            s_out = sg * el_x[:, gs] + _dot(bg, xt[:, gs], TN)
            lastv[:, gs] = jnp.broadcast_to(jnp.sum(dsg * s_out, axis=0, keepdims=True), (8, GW))
            dcb = jnp.zeros((L, L), F32)
            for r in range(SSM_R):
                h = g * SSM_R + r
                hs = slice(h * SSM_P, (h + 1) * SSM_P)
                seg = cs[:, h:h + 1] - cst[h:h + 1, :]
                dec = jnp.exp(jnp.where(lower, seg, -1e30))
                m = cb * dec
                dm = _dot(dyb[:, hs], xdb[:, hs], NT)
                dcb = dcb + dm * dec
                e = dm * m
                row_part = row_part + jnp.where(lane == h, jnp.sum(e, axis=1, keepdims=True), 0.0)
                col_part = col_part + jnp.where(sub == h, jnp.sum(e, axis=0, keepdims=True), 0.0)
                dxd[:, hs] = _dot(m, dyb[:, hs], TN) + dx_state[:, r * SSM_P:(r + 1) * SSM_P]
            dx_ref[:, bcol] = db + _dot(dcb, cg, TN)
            dx_ref[:, ccol] = dc + _dot(dcb, bg)
            dstate[:, gs] = dsg * el_x[:, gs] + _dot(cg, dys[:, gs], TN)
        dxv = dxd[...]
        dx_ref[:, 0:D_INNER] = dxv * dt_x + dyv * sk_ref[...]
        ddt_x = _dotx(dxv * xs, ex_t)
        yst = _dotx(yd[...], ex_t)
        lst = _dotx(lastv[...], ex_t)[0:1, :]
        rows = lax.broadcasted_iota(jnp.int32, (L, LANE), 0)
        dcs = row_part - col_part.T + yst + jnp.where(rows == L - 1, lst, 0.0)
        li = lax.broadcasted_iota(jnp.int32, (L, L), 0)
        si = lax.broadcasted_iota(jnp.int32, (L, L), 1)
        upper = jnp.where(si >= li, 1.0, 0.0).astype(BF16)
        dla = _xdot(upper, dcs)
        ddt_ref[...] = dla * a + ddt_x
        da_ref[...] += jnp.sum(dla * dtv, axis=0, keepdims=True)

    def row(w):
        return pl.BlockSpec((L, w), lambda b, c: (b * nc + nc - 1 - c, 0))

    return _pallas(
        body, name="ssd_bwd", grid=(nb, nc),
        in_specs=[row(XBC), row(LANE), pl.BlockSpec((1, LANE), lambda b, c: (0, 0)),
                  pl.BlockSpec((SSM_H, D_INNER), lambda b, c: (0, 0)),
                  pl.BlockSpec((D_INNER, LANE), lambda b, c: (0, 0)),
                  pl.BlockSpec((1, SSM_N, D_INNER), lambda b, c: (b * nc + nc - 1 - c, 0, 0)),
                  row(D_INNER), pl.BlockSpec((1, D_INNER), lambda b, c: (0, 0))],
        out_specs=[row(XBC), row(LANE), pl.BlockSpec((1, LANE), lambda b, c: (0, 0))],
        out_shape=[jax.ShapeDtypeStruct((t, XBC), F32), jax.ShapeDtypeStruct((t, LANE), F32),
                   jax.ShapeDtypeStruct((1, LANE), F32)],
        scratch_shapes=[pltpu.VMEM((SSM_N, D_INNER), F32), pltpu.VMEM((L, D_INNER), F32),
                        pltpu.VMEM((L, D_INNER), F32), pltpu.VMEM((8, D_INNER), F32)],
        compiler_params=_params(2),
    )(xbc_c, dt, a_log, ex, ext, states, dy, d_x)


def _group_rms(y2):
    gw = D_INNER // SSM_G
    parts = []
    for g in range(SSM_G):
        v = y2[:, g * gw:(g + 1) * gw]
        r = lax.rsqrt(jnp.mean(v * v, axis=1, keepdims=True) + EPS)
        parts.append(jnp.broadcast_to(r, v.shape))
    return jnp.concatenate(parts, axis=1)


def _gate_fwd(y, xbc_c, proj, d_x, gn_w, *, tm=256):
    t = y.shape[0]

    def body(y_ref, x_ref, z_ref, d_ref, w_ref, o_ref):
        y1 = y_ref[...] + d_ref[...] * x_ref[...]
        zv = z_ref[...]
        y2 = y1 * zv * _sigmoid(zv)
        o_ref[...] = (y2 * _group_rms(y2) * w_ref[...]).astype(BF16)

    row = pl.BlockSpec((tm, D_INNER), lambda i: (i, 0))
    vec = pl.BlockSpec((1, D_INNER), lambda i: (0, 0))
    return _pallas(
        body, name="gate_fwd", grid=(t // tm,), in_specs=[row, row, row, vec, vec], out_specs=row,
        out_shape=jax.ShapeDtypeStruct((t, D_INNER), BF16), compiler_params=_params(1),
    )(y, xbc_c, proj, d_x, gn_w)


def _gate_bwd(dyg, y, xbc_c, proj, d_x, gn_w, *, tm=256):
    t = y.shape[0]
    gw = D_INNER // SSM_G

    def body(dg_ref, y_ref, x_ref, z_ref, d_ref, w_ref, dy_ref, dz_ref, dw_ref, dd_ref):
        i = pl.program_id(0)
        xv = x_ref[...]
        dxv = d_ref[...]
        y1 = y_ref[...] + dxv * xv
        zv = z_ref[...]
        sz = _sigmoid(zv)
        y2 = y1 * zv * sz
        rr = _group_rms(y2)
        xh = y2 * rr
        dg = _f32(dg_ref[...])
        gq = dg * w_ref[...]
        prod = gq * xh
        means = []
        for g in range(SSM_G):
            mg = jnp.mean(prod[:, g * gw:(g + 1) * gw], axis=1, keepdims=True)
            means.append(jnp.broadcast_to(mg, (tm, gw)))
        dy2 = rr * (gq - xh * jnp.concatenate(means, axis=1))
        dy1 = dy2 * zv * sz
        dy_ref[...] = dy1
        dz_ref[...] = (dy2 * y1 * _dsilu(zv, sz)).astype(BF16)

        @pl.when(i == 0)
        def _():
            dw_ref[...] = jnp.zeros_like(dw_ref)
            dd_ref[...] = jnp.zeros_like(dd_ref)

        dw_ref[...] += jnp.sum(dg * xh, axis=0, keepdims=True)
        dd_ref[...] += jnp.sum(dy1 * xv, axis=0, keepdims=True)

    row = pl.BlockSpec((tm, D_INNER), lambda i: (i, 0))
    vec = pl.BlockSpec((1, D_INNER), lambda i: (0, 0))
    return _pallas(
        body, name="gate_bwd", grid=(t // tm,), in_specs=[row, row, row, row, vec, vec],
        out_specs=[row, row, vec, vec],
        out_shape=[jax.ShapeDtypeStruct((t, D_INNER), F32),
                   jax.ShapeDtypeStruct((t, D_INNER), BF16), jax.ShapeDtypeStruct((1, D_INNER), F32),
                   jax.ShapeDtypeStruct((1, D_INNER), F32)],
        compiler_params=_params(1),
    )(dyg, y, xbc_c, proj, d_x, gn_w)


ANY = pl.BlockSpec(memory_space=pl.ANY)


def _remote(src, dst, sems, k, to):
    send_sems, recv_sems = sems
    return pltpu.make_async_remote_copy(src_ref=src, dst_ref=dst, send_sem=send_sems.at[k], recv_sem=recv_sems.at[k],
                                        device_id=to, device_id_type=MESH)


def _gather_shards(w_in0, w_in1, w_out0, w_out1, small):
    nchip = 4
    dm, n0 = w_in0.shape
    n1 = w_in1.shape[1]
    ro = w_out0.shape[0]
    hr, ho = dm // 2, ro // 2

    def body(a0, a1, b0, b1, sm, o0, o1, p0, p1, osm, ici_s, ici_r, d2d_s, d2d_r):
        x, y, c = lax.axis_index("x"), lax.axis_index("y"), lax.axis_index("c")
        me = 2 * x + y
        sib = (x, y, 1 - c)
        peers = [(1 - x, y), (x, 1 - y), (1 - x, 1 - y)]

        def region(chip, half):
            col = pl.multiple_of(chip * n0, LANE)
            return [o0.at[pl.ds(half * hr, hr), pl.ds(col, n0)], o1.at[chip, pl.ds(half * hr, hr), :],
                    p0.at[pl.ds(chip * ro + half * ho, ho), :], p1.at[pl.ds(chip * ro + half * ho, ho), :]]

        halves = [a0.at[pl.ds(c * hr, hr), :], a1.at[pl.ds(c * hr, hr), :],
                  b0.at[pl.ds(c * ho, ho), :], b1.at[pl.ds(c * ho, ho), :]]
        sends = []
        for k, (px, py) in enumerate(peers):
            to = (px, py, c)
            for j, (s, d) in enumerate(zip(halves, region(me, c))):
                sends.append(_remote(s, d, (ici_s, ici_r), 5 * k + j, to))
            sends.append(_remote(sm, osm.at[me], (ici_s, ici_r), 5 * k + 4, to))
        for cp in sends:
            cp.start()
        for k, (px, py) in enumerate(peers):
            q = 2 * px + py
            for j, d in enumerate(region(q, c)):
                _remote(d, d, (ici_s, ici_r), 5 * k + j, (px, py, c)).wait_recv()
                fwd = _remote(d, d, (d2d_s, d2d_r), 4 * k + j, sib)
                fwd.start()
                sends.append(fwd)
            _remote(sm, osm.at[q], (ici_s, ici_r), 5 * k + 4, (px, py, c)).wait_recv()
        for k, (px, py) in enumerate(peers):
            for j, d in enumerate(region(2 * px + py, 1 - c)):
                _remote(d, d, (d2d_s, d2d_r), 4 * k + j, sib).wait_recv()
        for cp in sends:
            cp.wait_send()

    return _pallas(
        body, name="gather_shards", in_specs=[ANY] * 5, out_specs=[ANY] * 5,
        out_shape=[jax.ShapeDtypeStruct((dm, nchip * n0), w_in0.dtype),
                   jax.ShapeDtypeStruct((nchip, dm, n1), w_in1.dtype),
                   jax.ShapeDtypeStruct((nchip * ro, w_out0.shape[1]), w_out0.dtype),
                   jax.ShapeDtypeStruct((nchip * ro, w_out1.shape[1]), w_out1.dtype),
                   jax.ShapeDtypeStruct((nchip,) + small.shape, small.dtype)],
        scratch_shapes=[pltpu.SemaphoreType.DMA((15,)), pltpu.SemaphoreType.DMA((15,)),
                        pltpu.SemaphoreType.DMA((12,)), pltpu.SemaphoreType.DMA((12,))],
    )(w_in0, w_in1, w_out0, w_out1, small)


def _pair_exchange(g_in0, g_in1, g_out0, g_out1, gsmall, grep):
    dm = g_in0.shape[0]
    hr = dm // 2
    ho = g_out0.shape[1] // 2

    def body(a0, a1, b0, b1, sm, rp, q0, q1, r0, r1, osm, orp, pair_s, pair_r, send_sems, recv_sems, local_sems):
        x, y, c = lax.axis_index("x"), lax.axis_index("y"), lax.axis_index("c")
        me = 4 * x + 2 * y + c
        chip = 2 * x + y
        sib = (x, y, 1 - c)
        rows = pl.ds(pl.multiple_of((1 - c) * hr, 8), hr)
        orows = pl.ds(pl.multiple_of((1 - c) * ho, 8), ho)
        pair = [_remote(a0.at[rows, :], q0, (pair_s, pair_r), 0, sib),
                _remote(a1.at[rows, :], q1, (pair_s, pair_r), 1, sib),
                _remote(b0.at[:, orows, :], r0, (pair_s, pair_r), 2, sib),
                _remote(b1.at[:, orows, :], r1, (pair_s, pair_r), 3, sib)]
        for cp in pair:
            cp.start()
        own = [pltpu.make_async_copy(sm.at[chip], osm.at[me], local_sems.at[0]),
               pltpu.make_async_copy(rp, orp.at[me], local_sems.at[1])]
        for cp in own:
            cp.start()
        peers = []
        for k in range(7):
            fx, fy, fc = ((k + 1) >> 2) & 1, ((k + 1) >> 1) & 1, (k + 1) & 1
            peers.append((1 - x if fx else x, 1 - y if fy else y, 1 - c if fc else c))
        sends = []
        for k, (px, py, pc) in enumerate(peers):
            sends.append(_remote(sm.at[2 * px + py], osm.at[me], (send_sems, recv_sems), 2 * k, (px, py, pc)))
            sends.append(_remote(rp, orp.at[me], (send_sems, recv_sems), 2 * k + 1, (px, py, pc)))
        for cp in sends:
            cp.start()
        for k, (px, py, pc) in enumerate(peers):
            slot = 4 * px + 2 * py + pc
            _remote(sm.at[chip], osm.at[slot], (send_sems, recv_sems), 2 * k, (px, py, pc)).wait_recv()
            _remote(rp, orp.at[slot], (send_sems, recv_sems), 2 * k + 1, (px, py, pc)).wait_recv()
        for cp in pair:
            cp.wait_recv()
        for cp in pair + sends:
            cp.wait_send()
        for cp in own:
            cp.wait()

    return _pallas(
        body, name="pair_exchange", in_specs=[ANY] * 6, out_specs=[ANY] * 6,
        out_shape=[jax.ShapeDtypeStruct((hr, g_in0.shape[1]), F32),
                   jax.ShapeDtypeStruct((hr, g_in1.shape[1]), F32),
                   jax.ShapeDtypeStruct((g_out0.shape[0], ho, g_out0.shape[2]), F32),
                   jax.ShapeDtypeStruct((g_out1.shape[0], ho, g_out1.shape[2]), F32),
                   jax.ShapeDtypeStruct((8,) + gsmall.shape[1:], F32),
                   jax.ShapeDtypeStruct((8,) + grep.shape, F32)],
        scratch_shapes=[pltpu.SemaphoreType.DMA((4,)), pltpu.SemaphoreType.DMA((4,)),
                        pltpu.SemaphoreType.DMA((14,)), pltpu.SemaphoreType.DMA((14,)),
                        pltpu.SemaphoreType.DMA((2,))],
    )(g_in0, g_in1, g_out0, g_out1, gsmall, grep)


def _core_index():
    return lax.axis_index("c").astype(jnp.int32).reshape(1)


def _half_add(full, other, *, axis, block, name):
    nd = full.ndim
    nblk = other.shape[axis] // block[axis]
    grid = tuple(other.shape[d] // block[d] for d in range(nd))

    def body(c_ref, f_ref, o_ref, out_ref):
        out_ref[...] = (f_ref[...] + o_ref[...]).astype(BF16)

    def full_map(*idx):
        ids, c_ref = list(idx[:nd]), idx[nd]
        ids[axis] = ids[axis] + c_ref[0] * nblk
        return tuple(ids)

    def plain_map(*idx):
        return tuple(idx[:nd])

    return _pallas(
        body, name=name,
        grid_spec=pltpu.PrefetchScalarGridSpec(
            num_scalar_prefetch=1, grid=grid,
            in_specs=[pl.BlockSpec(block, full_map), pl.BlockSpec(block, plain_map)],
            out_specs=pl.BlockSpec(block, plain_map)),
        out_shape=jax.ShapeDtypeStruct(other.shape, BF16), compiler_params=_params(nd),
    )(_core_index(), full, other)


def _chip_exchange(s_in0, s_in1, s_out0, s_out1):
    npeer = 3
    n0 = s_in0.shape[1] // 4

    def body(a0, a1, b0, b1, l0, l1, m0, m1, send_sems, recv_sems):
        x, y, c = lax.axis_index("x"), lax.axis_index("y"), lax.axis_index("c")
        me = 2 * x + y
        peers = [(1 - x, y), (x, 1 - y), (1 - x, 1 - y)]

        def pieces(chip):
            return [a0.at[:, pl.ds(pl.multiple_of(chip * n0, LANE), n0)], a1.at[chip], b0.at[chip], b1.at[chip]]

        def slots(k):
            return [l0.at[k], l1.at[k], m0.at[k], m1.at[k]]

        sends = []
        for k, (px, py) in enumerate(peers):
            for j, (s, d) in enumerate(zip(pieces(2 * px + py), slots(k))):
                sends.append(_remote(s, d, (send_sems, recv_sems), 4 * k + j, (px, py, c)))
        for cp in sends:
            cp.start()
        for k, (px, py) in enumerate(peers):
            for j, (s, d) in enumerate(zip(pieces(me), slots(k))):
                _remote(s, d, (send_sems, recv_sems), 4 * k + j, (px, py, c)).wait_recv()
        for cp in sends:
            cp.wait_send()

    return _pallas(
        body, name="chip_exchange", in_specs=[ANY] * 4, out_specs=[ANY] * 4,
        out_shape=[jax.ShapeDtypeStruct((npeer, s_in0.shape[0], n0), BF16),
                   jax.ShapeDtypeStruct((npeer,) + s_in1.shape[1:], BF16),
                   jax.ShapeDtypeStruct((npeer,) + s_out0.shape[1:], BF16),
                   jax.ShapeDtypeStruct((npeer,) + s_out1.shape[1:], BF16)],
        scratch_shapes=[pltpu.SemaphoreType.DMA((12,)), pltpu.SemaphoreType.DMA((12,))],
    )(s_in0, s_in1, s_out0, s_out1)


def _chip_index():
    return (2 * lax.axis_index("x") + lax.axis_index("y")).astype(jnp.int32).reshape(1)


def _chip_sum(own, slots, *, own_block, own_map, block, name):
    npeer = slots.shape[0]
    shape = slots.shape[1:]
    grid = (shape[0] // block[0], shape[1] // block[1])

    def body(p_ref, own_ref, s_ref, o_ref):
        acc = own_ref[...].reshape(block).astype(F32)
        for q in range(npeer):
            acc = acc + s_ref[q].astype(F32)
        o_ref[...] = acc

    return _pallas(
        body, name=name,
        grid_spec=pltpu.PrefetchScalarGridSpec(
            num_scalar_prefetch=1, grid=grid,
            in_specs=[pl.BlockSpec(own_block, own_map),
                      pl.BlockSpec((npeer,) + block, lambda i, j, p: (0, i, j))],
            out_specs=pl.BlockSpec(block, lambda i, j, p: (i, j))),
        out_shape=jax.ShapeDtypeStruct(shape, F32), compiler_params=_params(2),
    )(_chip_index(), own, slots)


def _pair_share(r_in0, r_in1, r_out0, r_out1):
    def body(a0, a1, b0, b1, g0, g1, h0, h1, send_sems, recv_sems):
        x, y, c = lax.axis_index("x"), lax.axis_index("y"), lax.axis_index("c")
        sib = (x, y, 1 - c)
        sends = [_remote(s, d, (send_sems, recv_sems), j, sib)
                 for j, (s, d) in enumerate(zip([a0, a1, b0, b1], [g0, g1, h0, h1]))]
        for cp in sends:
            cp.start()
        for cp in sends:
            cp.wait()

    return _pallas(
        body, name="pair_share", in_specs=[ANY] * 4, out_specs=[ANY] * 4,
        out_shape=[jax.ShapeDtypeStruct(r.shape, F32) for r in (r_in0, r_in1, r_out0, r_out1)],
        scratch_shapes=[pltpu.SemaphoreType.DMA((4,)), pltpu.SemaphoreType.DMA((4,))],
    )(r_in0, r_in1, r_out0, r_out1)


def _adam_math(g, w, m, v):
    c1 = 1.0 - ADAM_B1 ** ADAM_STEP
    c2 = 1.0 - ADAM_B2 ** ADAM_STEP
    m2 = ADAM_B1 * m + (1.0 - ADAM_B1) * g
    v2 = ADAM_B2 * v + (1.0 - ADAM_B2) * (g * g)
    delta = -ADAM_LR * ((m2 / c1) / (jnp.sqrt(v2 / c2) + ADAM_EPS) + ADAM_WD * w)
    return delta, m2, v2


def _adamw_nat(g_mine, g_sib, w, m, v, *, name, tr):
    rows, cw = w.shape
    nt = g_mine.shape[0] // tr

    def body(c_ref, gm_ref, gs_ref, w_ref, m_ref, v_ref, go_ref, d_ref, nm_ref, nv_ref):
        mine = pl.program_id(0) // nt == c_ref[0]
        gv = jnp.where(mine, gm_ref[...], gs_ref[...])[:, 0:cw]
        delta, m2, v2 = _adam_math(gv, w_ref[...], m_ref[...], v_ref[...])
        go_ref[...] = gv
        d_ref[...] = delta
        nm_ref[...] = m2
        nv_ref[...] = v2

    def mine_map(i, c_ref):
        return (jnp.where(i // nt == c_ref[0], i % nt, 0), 0)

    def sib_map(i, c_ref):
        return (jnp.where(i // nt == c_ref[0], 0, i % nt), 0)

    row = pl.BlockSpec((tr, cw), lambda i, c_ref: (i, 0))
    gspec = (tr, g_mine.shape[1])
    out = jax.ShapeDtypeStruct((rows, cw), F32)
    return _pallas(
        body, name=name,
        grid_spec=pltpu.PrefetchScalarGridSpec(
            num_scalar_prefetch=1, grid=(rows // tr,),
            in_specs=[pl.BlockSpec(gspec, mine_map), pl.BlockSpec(gspec, sib_map), row, row, row],
            out_specs=[row, row, row, row]),
        out_shape=[out, out, out, out], compiler_params=_params(1),
    )(_core_index(), g_mine, g_sib, w, m, v)


def _adamw(slots, w, m, v, *, name, tr):
    nd, rows, _ = slots.shape
    c1 = 1.0 - ADAM_B1 ** ADAM_STEP
    c2 = 1.0 - ADAM_B2 ** ADAM_STEP

    def body(s_ref, w_ref, m_ref, v_ref, g_ref, d_ref, nm_ref, nv_ref):
        g = s_ref[0]
        for d in range(1, nd):
            g = g + s_ref[d]
        m2 = ADAM_B1 * m_ref[...] + (1.0 - ADAM_B1) * g
        v2 = ADAM_B2 * v_ref[...] + (1.0 - ADAM_B2) * (g * g)
        g_ref[...] = g
        nm_ref[...] = m2
        nv_ref[...] = v2
        d_ref[...] = -ADAM_LR * ((m2 / c1) / (jnp.sqrt(v2 / c2) + ADAM_EPS) + ADAM_WD * w_ref[...])

    row = pl.BlockSpec((tr, LANE), lambda i: (i, 0))
    out = jax.ShapeDtypeStruct((rows, LANE), F32)
    return _pallas(
        body, name=name, grid=(rows // tr,),
        in_specs=[pl.BlockSpec((nd, tr, LANE), lambda i: (0, i, 0)), row, row, row],
        out_specs=[row, row, row, row], out_shape=[out, out, out, out], compiler_params=_params(1),
    )(slots, w, m, v)


def _rows(a):
    return a.reshape(-1, LANE)


def _pad_rows(a, mult):
    pad = (-a.shape[0]) % mult
    return jnp.pad(a, ((0, pad), (0, 0))) if pad else a


def _pack(parts, mult):
    return _pad_rows(jnp.concatenate([_rows(p) for p in parts], axis=0), mult)


def _unpack(slab, shapes):
    out, r0 = [], 0
    for shp in shapes:
        n = 1
        for s in shp:
            n *= s
        r = n // LANE
        out.append(slab[r0:r0 + r].reshape(shp))
        r0 += r
    return out


def _pack_rep(vecs, scal):
    srow = jnp.concatenate([s.reshape(-1) for s in scal] + [jnp.zeros((LANE - 3 * SSM_H,), F32)]).reshape(1, LANE)
    return _pad_rows(jnp.concatenate([_rows(vv) for vv in vecs] + [srow], axis=0), 8)


def _unpack_rep(slab, vec_shapes, scal_shape):
    vecs, r0 = [], 0
    for shp in vec_shapes:
        vecs.append(slab[r0:r0 + 8].reshape(shp))
        r0 += 8
    srow = slab[r0]
    scal = [srow[i * SSM_H:(i + 1) * SSM_H].reshape(scal_shape) for i in range(3)]
    return vecs, scal


def kernel(x, ev_norm_w, ev_w_in, ev_dw_w, ev_dw_b, ev_ln_w, ev_ln_b, ev_w_out, od_norm_w, od_w_in, od_conv_w, od_conv_b, od_dt_bias, od_a_log, od_d, od_gnorm_w, od_w_out, final_norm_w, loss_target, m_ev_norm_w, m_ev_w_in, m_ev_dw_w, m_ev_dw_b, m_ev_ln_w, m_ev_ln_b, m_ev_w_out, m_od_norm_w, m_od_w_in, m_od_conv_w, m_od_conv_b, m_od_dt_bias, m_od_a_log, m_od_d, m_od_gnorm_w, m_od_w_out, m_final_norm_w, v_ev_norm_w, v_ev_w_in, v_ev_dw_w, v_ev_dw_b, v_ev_ln_w, v_ev_ln_b, v_ev_w_out, v_od_norm_w, v_od_w_in, v_od_conv_w, v_od_conv_b, v_od_dt_bias, v_od_a_log, v_od_d, v_od_gnorm_w, v_od_w_out, v_final_norm_w):
    nb, seq, d = x.shape
    t = nb * seq
    nchip = 4
    xf = x.reshape(t, d)
    tgt = loss_target.reshape(t, d)

    big_w = [ev_w_in[0], od_w_in[0], ev_w_out[0], od_w_out[0]]
    small_w = [ev_dw_w[0], od_norm_w[0], od_conv_w[0], od_conv_b[0], od_gnorm_w[0]]
    small_shapes = [a.shape for a in small_w]
    big_b = [a.astype(BF16) for a in big_w]
    small_slab = _pack(small_w, 8)
    w_in0, w_in1g, w_out0, w_out1, gath_small = _gather_shards(*big_b, small_slab)
    chip = 2 * lax.axis_index("x") + lax.axis_index("y")
    w_in0 = lax.dynamic_update_slice(w_in0, big_b[0], (0, chip * big_b[0].shape[1]))
    w_in1g = lax.dynamic_update_slice(w_in1g, big_b[1][None], (chip, 0, 0))
    w_out0 = lax.dynamic_update_slice(w_out0, big_b[2], (chip * big_b[2].shape[0], 0))
    w_out1 = lax.dynamic_update_slice(w_out1, big_b[3], (chip * big_b[3].shape[0], 0))
    gath_small = lax.dynamic_update_slice(gath_small, small_slab[None], (chip, 0, 0))
    per_chip = [_unpack(gath_small[p], small_shapes) for p in range(nchip)]

    def cat(idx, axis):
        return jnp.concatenate([per_chip[p][idx] for p in range(nchip)], axis=axis)

    w_in1 = jnp.pad(jnp.concatenate([w_in1g[p] for p in range(nchip)], axis=1),
                    ((0, 0), (0, IN_ODD_PAD - IN_ODD)))
    dw_w = jnp.pad(cat(0, 1), ((0, HALO - CONF_K), (0, 0)))
    dw_w8 = jnp.repeat(dw_w, SUB, axis=0)
    n1_w = cat(1, 0).reshape(1, d)
    conv_w = jnp.pad(cat(2, 1), ((0, PH - SSM_K), (0, 0)))
    conv_b = cat(3, 0).reshape(1, XBC)
    gn_w = cat(4, 0).reshape(1, D_INNER)

    def lanes(a):
        return jnp.pad(a.reshape(1, -1), ((0, 0), (0, LANE - a.size)))

    dt_bias, a_log = lanes(od_dt_bias), lanes(od_a_log)
    d_x = jnp.repeat(od_d.reshape(-1), SSM_P).reshape(1, D_INNER)
    hid = lax.broadcasted_iota(jnp.int32, (SSM_H, D_INNER), 1) // SSM_P
    ex = (hid == lax.broadcasted_iota(jnp.int32, (SSM_H, D_INNER), 0)).astype(BF16)
    ex_t = jnp.pad(ex.T, ((0, 0), (0, LANE - SSM_H)))
    fn_w = final_norm_w.reshape(1, d)

    n0 = _rms_fwd(xf, ev_norm_w, name="rms_fwd0")
    proj0 = _matmul(n0, w_in0, mode="nn", out_dtype=BF16, bm=512, bn=1024, bk=d, name="in_proj0", n_major=True)
    y_conv, u2 = _conf_fwd(proj0, dw_w8, ev_dw_b, ev_ln_w, ev_ln_b, seq)
    o_att, y_att = _sba_fwd(proj0, nb, seq)
    ycat0 = jnp.concatenate([y_conv, y_att], axis=1)
    h1 = _matmul(ycat0, w_out0, mode="nn", out_dtype=F32, bm=512, bn=d, bk=D_INNER, name="out_proj0", residual=xf)
    n1 = _rms_fwd(h1, n1_w, name="rms_fwd1")
    proj1 = _matmul(n1, w_in1, mode="nn", out_dtype=F32, bm=512, bn=768, bk=d, name="in_proj1", n_major=True)
    xbc_c = _xconv_fwd(proj1, conv_w, conv_b, seq)
    dt = _dt_fwd(proj1, dt_bias)
    y_ssd, states = _ssd_fwd(xbc_c, dt, a_log, ex, nb, seq)
    yg = _gate_fwd(y_ssd, xbc_c, proj1, d_x, gn_w)
    h2 = _matmul(yg, w_out1, mode="nn", out_dtype=F32, bm=512, bn=d, bk=D_INNER, name="out_proj1", residual=h1)
    dh2, g_fn, loss_part = _final_loss(h2, fn_w, tgt)

    dyg = _matmul(dh2, w_out1, mode="nt", out_dtype=F32, bm=512, bn=1024, bk=d, name="d_out_proj1")
    g_w_out1 = _matmul(yg, dh2, mode="tn", out_dtype=F32, bm=1024, bn=d, bk=1024, name="dw_out_proj1")
    dy_ssd, dz, g_gn, g_dx = _gate_bwd(dyg, y_ssd, xbc_c, proj1, d_x, gn_w)
    dxbc_c, ddt, g_a = _ssd_bwd(xbc_c, dt, a_log, ex, ex_t, states, dy_ssd, d_x, nb, seq)
    dxbc, g_conv_w, g_conv_b = _xconv_bwd(proj1, dxbc_c, conv_w, conv_b, seq)
    ddt_raw, g_dt_bias = _dt_bwd(proj1, dt_bias, ddt)
    dproj1 = jnp.concatenate([dz, dxbc, ddt_raw.astype(BF16),
                              jnp.zeros((t, IN_ODD_PAD - IN_ODD - (LANE - SSM_H)), BF16)], axis=1)
    dn1 = _matmul(dproj1, w_in1, mode="nt", out_dtype=BF16, bm=1024, bn=d, bk=1792, name="d_in_proj1")
    g_w_in1 = _matmul(n1, dproj1, mode="tn", out_dtype=F32, bm=d, bn=1792, bk=1024, name="dw_in_proj1")
    dh1, g_n1 = _rms_bwd(dn1, h1, n1_w, dh2, name="rms_bwd1")

    dycat0 = _matmul(dh1, w_out0, mode="nt", out_dtype=BF16, bm=512, bn=1024, bk=d, name="d_out_proj0")
    g_w_out0 = _matmul(ycat0, dh1, mode="tn", out_dtype=F32, bm=1024, bn=d, bk=1024, name="dw_out_proj0")
    dq, dk, dv, dga = _sba_bwd(proj0, o_att, dycat0, nb, seq)
    dpc, g_dw_w, g_dw_b, g_ln_w, g_ln_b = _conf_bwd(proj0, u2, dycat0, dw_w8, ev_ln_w, ev_ln_b, seq)
    dproj0 = jnp.concatenate([dpc, dq, dk.astype(BF16), dv.astype(BF16), dga], axis=1)
    dn0 = _matmul(dproj0, w_in0, mode="nt", out_dtype=BF16, bm=1024, bn=d, bk=1792, name="d_in_proj0")
    g_w_in0 = _matmul(n0, dproj0, mode="tn", out_dtype=F32, bm=d, bn=1792, bk=1024, name="dw_in_proj0")
    grad_x, g_n0 = _rms_bwd(dn0, xf, ev_norm_w, dh1, name="rms_bwd0")

    g_dw_w = g_dw_w.reshape(HALO, SUB, CONV_W).sum(axis=1)[0:CONF_K]
    g_dw_b, g_ln_w, g_ln_b = (a.sum(axis=0, keepdims=True) for a in (g_dw_b, g_ln_w, g_ln_b))
    g_conv_w = g_conv_w.reshape(PH, SUB, XBC).sum(axis=1)[0:SSM_K]
    g_conv_b = g_conv_b.sum(axis=0, keepdims=True)
    a_neg = -jnp.exp(od_a_log.reshape(-1))
    g_a_log = g_a[0, 0:SSM_H] * a_neg
    g_d = g_dx.reshape(SSM_H, SSM_P).sum(axis=1)
    n1 = IN_ODD // nchip
    n1p = -(-n1 // LANE) * LANE

    def chip_slab_small(p):
        c0, c1, c2, c3 = CONV_W // nchip, d // nchip, XBC // nchip, D_INNER // nchip
        return _pack([g_dw_w[:, p * c0:(p + 1) * c0], g_n1[0, p * c1:(p + 1) * c1],
                      g_conv_w[:, p * c2:(p + 1) * c2], g_conv_b[0, p * c2:(p + 1) * c2],
                      g_gn[0, p * c3:(p + 1) * c3]], 8)

    gsmall = jnp.stack([chip_slab_small(p) for p in range(nchip)])
    rep_vec_shapes = [ev_norm_w.shape, ev_dw_b.shape, ev_ln_w.shape, ev_ln_b.shape, final_norm_w.shape]
    grep = _pack_rep([g_n0, g_dw_b, g_ln_w, g_ln_b, g_fn], [g_dt_bias[0, 0:SSM_H], g_a_log, g_d])

    ro = D_INNER // nchip
    g_w_out0c = g_w_out0.reshape(nchip, ro, d)
    g_w_out1c = g_w_out1.reshape(nchip, ro, d)
    q_in0, q_in1, q_out0, q_out1, ssmall, srep = _pair_exchange(g_w_in0, g_w_in1, g_w_out0c, g_w_out1c, gsmall, grep)
    s_in0 = _half_add(g_w_in0, q_in0, axis=0, block=(128, IN_EVEN), name="half_add_in0")
    s_in1n = _half_add(g_w_in1, q_in1, axis=0, block=(128, IN_ODD_PAD), name="half_add_in1")
    s_in1 = jnp.stack([jnp.pad(s_in1n[:, p * n1:(p + 1) * n1], ((0, 0), (0, n1p - n1))) for p in range(nchip)])
    s_out0 = _half_add(g_w_out0c, q_out0, axis=1, block=(1, ro // 2, d), name="half_add_out0")
    s_out1 = _half_add(g_w_out1c, q_out1, axis=1, block=(1, ro // 2, d), name="half_add_out1")
    l_in0, l_in1, l_out0, l_out1 = _chip_exchange(s_in0, s_in1, s_out0, s_out1)
    r_in0 = _chip_sum(s_in0, l_in0, own_block=(128, IN_EVEN // nchip), own_map=lambda i, j, p: (i, p[0]),
                      block=(128, IN_EVEN // nchip), name="chip_sum_in0")
    r_in1 = _chip_sum(s_in1, l_in1, own_block=(1, 256, n1p), own_map=lambda i, j, p: (p[0], i, 0),
                      block=(256, n1p), name="chip_sum_in1")
    r_out0 = _chip_sum(s_out0, l_out0, own_block=(1, ro // 2, d), own_map=lambda i, j, p: (p[0], 0, 0),
                       block=(ro // 2, d), name="chip_sum_out0")
    r_out1 = _chip_sum(s_out1, l_out1, own_block=(1, ro // 2, d), own_map=lambda i, j, p: (p[0], 0, 0),
                       block=(ro // 2, d), name="chip_sum_out1")
    big_r = [r_in0, r_in1, r_out0, r_out1]
    big_q = _pair_share(*big_r)

    big_m = [m_ev_w_in[0], m_od_w_in[0], m_ev_w_out[0], m_od_w_out[0]]
    big_v = [v_ev_w_in[0], v_od_w_in[0], v_ev_w_out[0], v_od_w_out[0]]
    big_names = ["adamw_in0", "adamw_in1", "adamw_out0", "adamw_out1"]
    out_bigs = [_adamw_nat(gm, gs, w, m, v, name=nm, tr=128)
                for gm, gs, w, m, v, nm in zip(big_r, big_q, big_w, big_m, big_v, big_names)]

    def upd(slots, ws, ms, vs, packer, name, tr):
        return _adamw(slots, packer(ws), packer(ms), packer(vs), name=name, tr=tr)

    small_m = [m_ev_dw_w[0], m_od_norm_w[0], m_od_conv_w[0], m_od_conv_b[0], m_od_gnorm_w[0]]
    small_v = [v_ev_dw_w[0], v_od_norm_w[0], v_od_conv_w[0], v_od_conv_b[0], v_od_gnorm_w[0]]
    out_small = upd(ssmall, small_w, small_m, small_v, lambda a: _pack(a, 8), "adamw_small", ssmall.shape[1])

    def rep_pack(a):
        return _pack_rep(a[0:5], a[5:8])

    rep_w = [ev_norm_w, ev_dw_b, ev_ln_w, ev_ln_b, final_norm_w, od_dt_bias, od_a_log, od_d]
    rep_m = [m_ev_norm_w, m_ev_dw_b, m_ev_ln_w, m_ev_ln_b, m_final_norm_w, m_od_dt_bias, m_od_a_log, m_od_d]
    rep_v = [v_ev_norm_w, v_ev_dw_b, v_ev_ln_w, v_ev_ln_b, v_final_norm_w, v_od_dt_bias, v_od_a_log, v_od_d]
    out_rep = upd(srep, rep_w, rep_m, rep_v, rep_pack, "adamw_rep", srep.shape[1])

    results = []
    for kind in range(4):
        bw = [o[kind].reshape((1,) + o[kind].shape) for o in out_bigs]
        sw = _unpack(out_small[kind], small_shapes)
        vecs, scal = _unpack_rep(out_rep[kind], rep_vec_shapes, od_dt_bias.shape)
        results.append([
            vecs[0], bw[0], sw[0].reshape(ev_dw_w.shape), vecs[1], vecs[2], vecs[3], bw[2],
            sw[1].reshape(od_norm_w.shape), bw[1], sw[2].reshape(od_conv_w.shape), sw[3].reshape(od_conv_b.shape),
            scal[0], scal[1], scal[2], sw[4].reshape(od_gnorm_w.shape), bw[3], vecs[4]])
    loss = lax.psum(loss_part[0, 0], ("x", "y", "c"))
    return (loss, grad_x.reshape(x.shape), *results[0], *results[1], *results[2], *results[3])
```

```python
import jax
import jax.numpy as jnp
from jax import lax
from jax.experimental import pallas as pl
from jax.experimental.pallas import tpu as pltpu

F32 = jnp.float32
BF16 = jnp.bfloat16

D_MODEL = 1024
CONV_W = 1024
ATT_W = 1024
HEAD_DIM = 128
N_HEADS = 8
CONF_K = 31
IN_EVEN = 7168
D_INNER = 2048
SSM_P = 64
SSM_H = 32
SSM_G = 4
SSM_R = SSM_H // SSM_G
SSM_N = 128
SSM_K = 4
CHUNK = 128
XBC = D_INNER + 2 * SSM_G * SSM_N
IN_ODD = D_INNER + XBC + SSM_H
IN_ODD_PAD = 5376
EPS = 1e-6
QB = 128
NEG_CUT = -100.0

ADAM_LR = 0.001
ADAM_B1 = 0.9
ADAM_B2 = 0.999
ADAM_EPS = 1e-08
ADAM_WD = 0.01
ADAM_STEP = 10

LANE = 128
VMEM_LIMIT = 56 * 1024 * 1024
MESH = pl.DeviceIdType.MESH

NN = (((1,), (0,)), ((), ()))
NT = (((1,), (1,)), ((), ()))
TN = (((0,), (0,)), ((), ()))


def _pallas(body, **kw):
    return pl.pallas_call(body, **kw)


def _params(n_axes):
    return pltpu.CompilerParams(dimension_semantics=("arbitrary",) * n_axes, vmem_limit_bytes=VMEM_LIMIT)


def _dot(a, b, dims=NN):
    return lax.dot_general(a.astype(BF16), b.astype(BF16), dims, preferred_element_type=F32)


def _parts(x):
    h = x.astype(BF16)
    r = x - h.astype(F32)
    m = r.astype(BF16)
    l = (r - m.astype(F32)).astype(BF16)
    return (h, m, l)


def _dotx(x, e01, dims=NN):
    acc = None
    for p in _parts(x):
        t = lax.dot_general(p, e01, dims, preferred_element_type=F32)
        acc = t if acc is None else acc + t
    return acc


def _dotx2(x, e01, dims=NN):
    h = x.astype(BF16)
    l = (x - h.astype(F32)).astype(BF16)
    return (lax.dot_general(h, e01, dims, preferred_element_type=F32)
            + lax.dot_general(l, e01, dims, preferred_element_type=F32))


def _xdot(e01, x, dims=NN):
    acc = None
    for p in _parts(x):
        t = lax.dot_general(e01, p, dims, preferred_element_type=F32)
        acc = t if acc is None else acc + t
    return acc


def _f32(x):
    return x.astype(F32)


def _sigmoid(x):
    return 1.0 / (1.0 + jnp.exp(-x))


def _dsilu(x, s):
    return s * (1.0 + x * (1.0 - s))


def _matmul(a, b, *, mode, out_dtype, bm, bn, bk, name, residual=None, n_major=False):
    if mode == "nn":
        (m, k), n = a.shape, b.shape[1]
        a_blk, a_map = (bm, bk), lambda i, j, kk: (i, kk)
        b_blk, b_map = (bk, bn), lambda i, j, kk: (kk, j)
        dims = NN
    elif mode == "nt":
        (m, k), n = a.shape, b.shape[0]
        a_blk, a_map = (bm, bk), lambda i, j, kk: (i, kk)
        b_blk, b_map = (bn, bk), lambda i, j, kk: (j, kk)
        dims = NT
    else:
        (k, m), n = a.shape, b.shape[1]
        a_blk, a_map = (bk, bm), lambda i, j, kk: (kk, i)
        b_blk, b_map = (bk, bn), lambda i, j, kk: (kk, j)
        dims = TN
    bm, bn, bk = min(bm, m), min(bn, n), min(bk, k)
    if mode != "nn":
        a_blk = (bm, bk) if mode == "nt" else (bk, bm)
        b_blk = (bn, bk) if mode == "nt" else (bk, bn)
    else:
        a_blk, b_blk = (bm, bk), (bk, bn)
    assert m % bm == 0 and n % bn == 0 and k % bk == 0, (name, m, n, k)
    nk = k // bk
    has_res = residual is not None

    def order(f):
        return (lambda j, i, kk: f(i, j, kk)) if n_major else f

    def body(*refs):
        a_ref, b_ref = refs[0], refs[1]
        r_ref = refs[2] if has_res else None
        o_ref = refs[2 + has_res]

        def finish(r):
            if has_res:
                r = r + r_ref[...]
            o_ref[...] = r.astype(out_dtype)

        if nk == 1:
            finish(_dot(a_ref[...], b_ref[...], dims))
            return
        acc_ref = refs[3 + has_res]
        kk = pl.program_id(2)

        @pl.when(kk == 0)
        def _():
            acc_ref[...] = jnp.zeros_like(acc_ref)

        acc_ref[...] += _dot(a_ref[...], b_ref[...], dims)

        @pl.when(kk == nk - 1)
        def _():
            finish(acc_ref[...])

    in_specs = [pl.BlockSpec(a_blk, order(a_map)), pl.BlockSpec(b_blk, order(b_map))]
    args = [a, b]
    out_map = order(lambda i, j, kk: (i, j))
    if has_res:
        in_specs.append(pl.BlockSpec((bm, bn), out_map))
        args.append(residual)
    grid = (n // bn, m // bm, nk) if n_major else (m // bm, n // bn, nk)
    return _pallas(
        body, name=name, grid=grid, in_specs=in_specs,
        out_specs=pl.BlockSpec((bm, bn), out_map),
        out_shape=jax.ShapeDtypeStruct((m, n), out_dtype),
        scratch_shapes=[pltpu.VMEM((bm, bn), F32)] if nk > 1 else [], compiler_params=_params(3),
    )(*args)


def _rms_fwd(x, w, *, name, tm=512):
    t, d = x.shape

    def body(x_ref, w_ref, o_ref):
        xv = x_ref[...]
        r = lax.rsqrt(jnp.mean(xv * xv, axis=1, keepdims=True) + EPS)
        o_ref[...] = (xv * r * w_ref[...]).astype(BF16)

    return _pallas(
        body, name=name, grid=(t // tm,),
        in_specs=[pl.BlockSpec((tm, d), lambda i: (i, 0)), pl.BlockSpec((1, d), lambda i: (0, 0))],
        out_specs=pl.BlockSpec((tm, d), lambda i: (i, 0)),
        out_shape=jax.ShapeDtypeStruct((t, d), BF16), compiler_params=_params(1),
    )(x, w)


def _rms_bwd(dn, x, w, dres, *, name, tm=512):
    t, d = x.shape

    def body(dn_ref, x_ref, w_ref, dr_ref, dx_ref, dw_ref):
        i = pl.program_id(0)
        xv = x_ref[...]
        r = lax.rsqrt(jnp.mean(xv * xv, axis=1, keepdims=True) + EPS)
        xh = xv * r
        dy = dn_ref[...].astype(F32)
        g = dy * w_ref[...]
        dx_ref[...] = dr_ref[...] + r * (g - xh * jnp.mean(g * xh, axis=1, keepdims=True))

        @pl.when(i == 0)
        def _():
            dw_ref[...] = jnp.zeros_like(dw_ref)

        dw_ref[...] += jnp.sum(dy * xh, axis=0, keepdims=True)

    row = pl.BlockSpec((tm, d), lambda i: (i, 0))
    vec = pl.BlockSpec((1, d), lambda i: (0, 0))
    return _pallas(
        body, name=name, grid=(t // tm,), in_specs=[row, row, vec, row], out_specs=[row, vec],
        out_shape=[jax.ShapeDtypeStruct((t, d), F32), jax.ShapeDtypeStruct((1, d), F32)],
        compiler_params=_params(1),
    )(dn, x, w, dres)


def _final_loss(h, w, target, *, tm=512):
    t, d = h.shape

    def body(h_ref, w_ref, t_ref, dh_ref, dw_ref, loss_ref):
        i = pl.program_id(0)
        xv = h_ref[...]
        r = lax.rsqrt(jnp.mean(xv * xv, axis=1, keepdims=True) + EPS)
        xh = xv * r
        wv = w_ref[...]
        err = xh * wv - t_ref[...]
        dy = err * (1.0 / d)
        g = dy * wv
        dh_ref[...] = r * (g - xh * jnp.mean(g * xh, axis=1, keepdims=True))

        @pl.when(i == 0)
        def _():
            dw_ref[...] = jnp.zeros_like(dw_ref)
            loss_ref[...] = jnp.zeros_like(loss_ref)

        dw_ref[...] += jnp.sum(dy * xh, axis=0, keepdims=True)
        part = jnp.sum(jnp.sum(err * err, axis=1, keepdims=True), axis=0, keepdims=True)
        loss_ref[...] += part * (0.5 / d)

    row = pl.BlockSpec((tm, d), lambda i: (i, 0))
    vec = pl.BlockSpec((1, d), lambda i: (0, 0))
    return _pallas(
        body, name="final_loss", grid=(t // tm,), in_specs=[row, vec, row],
        out_specs=[row, vec, pl.BlockSpec((1, LANE), lambda i: (0, 0))],
        out_shape=[jax.ShapeDtypeStruct((t, d), F32), jax.ShapeDtypeStruct((1, d), F32),
                   jax.ShapeDtypeStruct((1, LANE), F32)],
        compiler_params=_params(1),
    )(h, w, target)


HALO = 32


SUB = 8
RC = 16


def _make_shifts(sh_ref, rows, shifts=tuple(range(1, SUB))):
    for s in shifts:
        sh_ref[s, 0:rows, :] = sh_ref[0, s:s + rows, :]


def _shifted(sh_ref, r0, j, rows):
    return sh_ref[j % SUB, pl.ds(r0 + (j - j % SUB), rows), :]


def _taps(w8_ref, sh_ref, r0, first, step, init):
    accs = [init] * (RC // SUB)
    for k in range(CONF_K):
        wk = w8_ref[k * SUB:(k + 1) * SUB, :]
        x = _shifted(sh_ref, r0, first + step * k, RC)
        accs = [a + wk * x[q * SUB:(q + 1) * SUB] for q, a in enumerate(accs)]
    return jnp.concatenate(accs, axis=0)


def _conf_fwd(proj, dw_w, dw_b, ln_w, ln_b, seq, *, tm=256):
    t = proj.shape[0]
    c = CONV_W
    tps = seq // tm
    hb = tm // HALO

    def body(a_ref, b_ref, g_ref, ha_ref, hb_ref, w_ref, wb_ref, lw_ref, lb_ref, y_ref, u2_ref, sh_ref):
        i = pl.program_id(0)
        keep = jnp.where(i % tps == 0, 0.0, 1.0)
        sh_ref[0, 0:HALO, :] = _f32(ha_ref[...]) * _sigmoid(_f32(hb_ref[...])) * keep
        sh_ref[0, HALO:HALO + tm, :] = _f32(a_ref[...]) * _sigmoid(_f32(b_ref[...]))
        _make_shifts(sh_ref, tm + HALO - SUB)

        def chunk(ci, carry):
            r0 = pl.multiple_of(ci * RC, RC)
            acc = _taps(w_ref, sh_ref, r0, HALO - CONF_K + 1, 1, jnp.broadcast_to(wb_ref[...], (SUB, c)))
            u2_ref[pl.ds(r0, RC), :] = acc
            mu = jnp.mean(acc, axis=1, keepdims=True)
            xc = acc - mu
            rs = lax.rsqrt(jnp.mean(xc * xc, axis=1, keepdims=True) + EPS)
            u3 = xc * rs * lw_ref[...] + lb_ref[...]
            gv = _f32(g_ref[pl.ds(r0, RC), :])
            y_ref[pl.ds(r0, RC), :] = (u3 * _sigmoid(u3) * gv * _sigmoid(gv)).astype(BF16)
            return carry

        lax.fori_loop(0, tm // RC, chunk, 0, unroll=2)

    def col(j):
        return pl.BlockSpec((tm, c), lambda i: (i, j))

    def prev(j):
        return pl.BlockSpec((HALO, c), lambda i: (jnp.maximum(i * hb - 1, 0), j))

    vec = pl.BlockSpec((1, c), lambda i: (0, 0))
    return _pallas(
        body, name="conf_fwd", grid=(t // tm,),
        in_specs=[col(0), col(1), col(2), prev(0), prev(1),
                  pl.BlockSpec((HALO * SUB, c), lambda i: (0, 0)), vec, vec, vec],
        out_specs=[pl.BlockSpec((tm, c), lambda i: (i, 0)), pl.BlockSpec((tm, c), lambda i: (i, 0))],
        out_shape=[jax.ShapeDtypeStruct((t, c), BF16), jax.ShapeDtypeStruct((t, c), F32)],
        scratch_shapes=[pltpu.VMEM((SUB, tm + HALO, c), F32)], compiler_params=_params(1),
    )(proj, proj, proj, proj, proj, dw_w, dw_b, ln_w, ln_b)


def _conf_bwd(proj, u2, dycat, dw_w, ln_w, ln_b, seq, *, tm=256):
    t = proj.shape[0]
    c = CONV_W
    tps = seq // tm
    hb = tm // HALO
    nhb = t // HALO

    def fold(v):
        out = v[0:SUB]
        for q in range(1, RC // SUB):
            out = out + v[q * SUB:(q + 1) * SUB]
        return out

    def body(a_ref, b_ref, g_ref, pa_ref, pb_ref, ng_ref, u2_ref, nu2_ref, dy_ref, ndy_ref,
             w_ref, lw_ref, lb_ref,
             dp_ref, dww_ref, dwb_ref, dlw_ref, dlb_ref, su_ref, sd_ref):
        i = pl.program_id(0)
        first = i % tps == 0
        last = i % tps == tps - 1

        @pl.when(i == 0)
        def _():
            dww_ref[...] = jnp.zeros_like(dww_ref)
            dwb_ref[...] = jnp.zeros_like(dwb_ref)
            dlw_ref[...] = jnp.zeros_like(dlw_ref)
            dlb_ref[...] = jnp.zeros_like(dlb_ref)

        su_ref[0, 0:HALO, :] = _f32(pa_ref[...]) * _sigmoid(_f32(pb_ref[...])) * jnp.where(first, 0.0, 1.0)
        su_ref[0, HALO:HALO + tm, :] = _f32(a_ref[...]) * _sigmoid(_f32(b_ref[...]))
        _make_shifts(su_ref, tm + HALO - SUB)

        def ln_back(u2c, gv, dy):
            mu = jnp.mean(u2c, axis=1, keepdims=True)
            xc = u2c - mu
            rs = lax.rsqrt(jnp.mean(xc * xc, axis=1, keepdims=True) + EPS)
            xh = xc * rs
            lw = lw_ref[...]
            u3 = xh * lw + lb_ref[...]
            s3 = _sigmoid(u3)
            sg = _sigmoid(gv)
            dgc = dy * (u3 * s3) * _dsilu(gv, sg)
            du3 = dy * gv * sg * _dsilu(u3, s3)
            dxh = du3 * lw
            du2 = rs * (dxh - jnp.mean(dxh, axis=1, keepdims=True)
                        - xh * jnp.mean(dxh * xh, axis=1, keepdims=True))
            return du2, dgc, du3, xh

        def tile_chunk(ci, carry):
            r0 = pl.multiple_of(ci * RC, RC)
            rows = pl.ds(r0, RC)
            du2, dgc, du3, xh = ln_back(u2_ref[rows, :], _f32(g_ref[rows, :]), _f32(dy_ref[rows, :]))
            sd_ref[0, rows, :] = du2
            dp_ref[rows, 2 * c:3 * c] = dgc.astype(BF16)
            dwb_ref[...] += fold(du2)
            dlw_ref[...] += fold(du3 * xh)
            dlb_ref[...] += fold(du3)
            return carry

        lax.fori_loop(0, tm // RC, tile_chunk, 0, unroll=2)
        live = jnp.where(last, 0.0, 1.0)
        for ci in range(HALO // RC):
            rows = slice(ci * RC, (ci + 1) * RC)
            du2, _, _, _ = ln_back(nu2_ref[rows, :], _f32(ng_ref[rows, :]), _f32(ndy_ref[rows, :]))
            sd_ref[0, tm + ci * RC:tm + (ci + 1) * RC, :] = du2 * live
        _make_shifts(sd_ref, tm + HALO - SUB)

        def tap_chunk(ci, carry):
            r0 = pl.multiple_of(ci * RC, RC)
            rows = pl.ds(r0, RC)
            du1 = _taps(w_ref, sd_ref, r0, CONF_K - 1, -1, jnp.zeros((SUB, c), F32))
            sb = _sigmoid(_f32(b_ref[rows, :]))
            dp_ref[rows, 0:c] = (du1 * sb).astype(BF16)
            dp_ref[rows, c:2 * c] = (du1 * _f32(a_ref[rows, :]) * sb * (1.0 - sb)).astype(BF16)
            du2 = sd_ref[0, rows, :]
            for k in range(CONF_K):
                dww_ref[k * SUB:(k + 1) * SUB, :] += fold(du2 * _shifted(su_ref, r0, HALO - CONF_K + 1 + k, RC))
            return carry

        lax.fori_loop(0, tm // RC, tap_chunk, 0)

    def col(j):
        return pl.BlockSpec((tm, c), lambda i: (i, j))

    def prev(j):
        return pl.BlockSpec((HALO, c), lambda i: (jnp.maximum(i * hb - 1, 0), j))

    def nxt(j):
        return pl.BlockSpec((HALO, c), lambda i: (jnp.minimum((i + 1) * hb, nhb - 1), j))

    vec = pl.BlockSpec((1, c), lambda i: (0, 0))
    acc = pl.BlockSpec((SUB, c), lambda i: (0, 0))
    return _pallas(
        body, name="conf_bwd", grid=(t // tm,),
        in_specs=[col(0), col(1), col(2), prev(0), prev(1), nxt(2), col(0), nxt(0), col(0), nxt(0),
                  pl.BlockSpec((HALO * SUB, c), lambda i: (0, 0)), vec, vec],
        out_specs=[pl.BlockSpec((tm, 3 * c), lambda i: (i, 0)),
                   pl.BlockSpec((HALO * SUB, c), lambda i: (0, 0)), acc, acc, acc],
        out_shape=[jax.ShapeDtypeStruct((t, 3 * c), BF16), jax.ShapeDtypeStruct((HALO * SUB, c), F32),
                   jax.ShapeDtypeStruct((SUB, c), F32), jax.ShapeDtypeStruct((SUB, c), F32),
                   jax.ShapeDtypeStruct((SUB, c), F32)],
        scratch_shapes=[pltpu.VMEM((SUB, tm + HALO, c), F32), pltpu.VMEM((SUB, tm + HALO, c), F32)],
        compiler_params=_params(1),
    )(proj, proj, proj, proj, proj, proj, u2, u2, dycat, dycat, dw_w, ln_w, ln_b)


Q_COL = 3 * CONV_W // HEAD_DIM
K_COL = Q_COL + N_HEADS
V_COL = K_COL + N_HEADS
GA_COL = V_COL + N_HEADS


SBA_TQ = 256
SBA_WK = 4 * QB


def _sb_window(qs, kw, ws, limit, t0, carry):
    tq, wk = qs.shape[0], kw.shape[0]
    z = _dot(qs, kw, NT)
    sg = ws + lax.broadcasted_iota(jnp.int32, (tq, wk), 1)
    tg = t0 + lax.broadcasted_iota(jnp.int32, (tq, wk), 0)
    mask = sg < jnp.minimum(tg, limit)
    sp = jnp.log(1.0 + jnp.exp(-jnp.abs(z)))
    ls = jnp.minimum(z, 0.0) - sp
    lk = jnp.where(mask, ls - z, 0.0)
    jj = lax.broadcasted_iota(jnp.int32, (QB, QB), 0)
    ss = lax.broadcasted_iota(jnp.int32, (QB, QB), 1)
    ustrict = jnp.where(jj > ss, 1.0, 0.0).astype(BF16)
    laters = [None] * (wk // QB)
    for ch in reversed(range(wk // QB)):
        lkc = lk[:, ch * QB:(ch + 1) * QB]
        laters[ch] = carry + _dotx2(lkc, ustrict)
        carry = carry + jnp.sum(lkc, axis=1, keepdims=True)
    w = jnp.where(mask, jnp.exp(ls + jnp.concatenate(laters, axis=1)), 0.0)
    return mask, ls, w, carry


def _sba_fwd(proj, nb, seq, *, tq=SBA_TQ, wk=SBA_WK):
    t = proj.shape[0]
    wk = min(wk, seq)
    nq = seq // tq
    scale = HEAD_DIM ** -0.5

    def body(q_ref, k_ref, v_ref, g_ref, o_ref, y_ref):
        i = pl.program_id(2)
        t0 = i * tq
        qs = (_f32(q_ref[...]) * scale).astype(BF16)

        def window(ws, limit, carry, acc):
            ws = pl.multiple_of(ws, QB)
            _, _, w, carry = _sb_window(qs, k_ref[pl.ds(ws, wk), :], ws, limit, t0, carry)
            return carry, acc + _dot(w, v_ref[pl.ds(ws, wk), :])

        ws0 = jnp.maximum(t0 + tq - wk, 0)
        carry, acc = window(ws0, seq, jnp.zeros((tq, 1), F32), jnp.zeros((tq, HEAD_DIM), F32))

        def cond(st):
            return jnp.logical_and(st[0] > 0, jnp.max(st[1]) > NEG_CUT)

        def step(st):
            c2, a2 = window(jnp.maximum(st[0] - wk, 0), st[0], st[1], st[2])
            return jnp.maximum(st[0] - wk, 0), c2, a2

        _, _, acc = lax.while_loop(cond, step, (ws0, carry, acc))
        o_ref[...] = acc
        gv = _f32(g_ref[...])
        y_ref[...] = (acc * gv * _sigmoid(gv)).astype(BF16)

    def tile(c0):
        return pl.BlockSpec((tq, HEAD_DIM), lambda b, h, i: (b * nq + i, c0 + h))

    def whole(c0):
        return pl.BlockSpec((seq, HEAD_DIM), lambda b, h, i: (b, c0 + h))

    return _pallas(
        body, name="sba_fwd", grid=(nb, N_HEADS, nq),
        in_specs=[tile(Q_COL), whole(K_COL), whole(V_COL), tile(GA_COL)],
        out_specs=[tile(0), tile(0)],
        out_shape=[jax.ShapeDtypeStruct((t, ATT_W), F32), jax.ShapeDtypeStruct((t, ATT_W), BF16)],
        compiler_params=_params(3),
    )(proj, proj, proj, proj)


def _sba_bwd(proj, o, dycat, nb, seq, *, tq=SBA_TQ, wk=SBA_WK):
    t = proj.shape[0]
    wk = min(wk, seq)
    nq = seq // tq
    nwin = -(-seq // wk) + 1
    nch = wk // QB
    scale = HEAD_DIM ** -0.5

    def body(q_ref, k_ref, v_ref, g_ref, o_ref, dy_ref, dq_ref, dk_ref, dv_ref, dg_ref, e_ref, sp_ref):
        i = pl.program_id(2)
        t0 = i * tq

        @pl.when(i == 0)
        def _():
            dk_ref[...] = jnp.zeros_like(dk_ref)
            dv_ref[...] = jnp.zeros_like(dv_ref)

        qs = (_f32(q_ref[...]) * scale).astype(BF16)
        gv = _f32(g_ref[...])
        sg = _sigmoid(gv)
        dy = _f32(dy_ref[...])
        do = (dy * gv * sg).astype(BF16)
        dg_ref[...] = (dy * o_ref[...] * _dsilu(gv, sg)).astype(BF16)

        def start_of(n):
            return pl.multiple_of(jnp.maximum(t0 + tq - (n + 1) * wk, 0), QB)

        def limit_of(n):
            return jnp.where(n == 0, seq, jnp.maximum(t0 + tq - n * wk, 0))

        def near(n, carry):
            ws = start_of(n)
            _, ls, w, carry = _sb_window(qs, k_ref[pl.ds(ws, wk), :], ws, limit_of(n), t0, carry)
            e_ref[n] = w * _dot(do, v_ref[pl.ds(ws, wk), :], NT)
            sp_ref[n] = jnp.exp(ls)
            dv_ref[pl.ds(ws, wk), :] += _dot(w, do, TN)
            return carry

        carry = near(0, jnp.zeros((tq, 1), F32))

        def cond(st):
            return jnp.logical_and(start_of(st[0] - 1) > 0, jnp.max(st[1]) > NEG_CUT)

        def step(st):
            return st[0] + 1, near(st[0], st[1])

        nvis, _ = lax.while_loop(cond, step, (1, carry))

        jj = lax.broadcasted_iota(jnp.int32, (QB, QB), 0)
        ss = lax.broadcasted_iota(jnp.int32, (QB, QB), 1)
        lstrict = jnp.where(jj < ss, 1.0, 0.0).astype(BF16)

        def far(r, st):
            pre, dq = st
            n = nvis - 1 - r
            ws = start_of(n)
            e = e_ref[n]
            spn = sp_ref[n]
            gs = []
            for ch in range(nch):
                ec = e[:, ch * QB:(ch + 1) * QB]
                gs.append(pre + _dotx2(ec, lstrict))
                pre = pre + jnp.sum(ec, axis=1, keepdims=True)
            sgl = ws + lax.broadcasted_iota(jnp.int32, (tq, wk), 1)
            tgl = t0 + lax.broadcasted_iota(jnp.int32, (tq, wk), 0)
            mask = sgl < jnp.minimum(tgl, limit_of(n))
            dz = jnp.where(mask, e * (1.0 - spn) - jnp.concatenate(gs, axis=1) * spn, 0.0).astype(BF16)
            dk_ref[pl.ds(ws, wk), :] += _dot(dz, qs, TN)
            return pre, dq + _dot(dz, k_ref[pl.ds(ws, wk), :])

        _, dq = lax.fori_loop(0, nvis, far, (jnp.zeros((tq, 1), F32), jnp.zeros((tq, HEAD_DIM), F32)))
        dq_ref[...] = (dq * scale).astype(BF16)

    def tile(c0):
        return pl.BlockSpec((tq, HEAD_DIM), lambda b, h, i: (b * nq + i, c0 + h))

    def whole(c0):
        return pl.BlockSpec((seq, HEAD_DIM), lambda b, h, i: (b, c0 + h))

    return _pallas(
        body, name="sba_bwd", grid=(nb, N_HEADS, nq),
        in_specs=[tile(Q_COL), whole(K_COL), whole(V_COL), tile(GA_COL), tile(0),
                  tile(CONV_W // HEAD_DIM)],
        out_specs=[tile(0), whole(0), whole(0), tile(0)],
        out_shape=[jax.ShapeDtypeStruct((t, ATT_W), BF16), jax.ShapeDtypeStruct((t, ATT_W), F32),
                   jax.ShapeDtypeStruct((t, ATT_W), F32), jax.ShapeDtypeStruct((t, ATT_W), BF16)],
        scratch_shapes=[pltpu.VMEM((nwin, tq, wk), F32), pltpu.VMEM((nwin, tq, wk), F32)],
        compiler_params=_params(3),
    )(proj, proj, proj, proj, o, dycat)


CT = 512
PH = 8
XRC = 32
X_SHIFTS = tuple(s for s in range(PH - SSM_K + 1, PH))
D_SHIFTS = tuple(range(1, SSM_K))
Z_BLK = 0
XBC_BLK = D_INNER // CT
DT_BLK = (D_INNER + XBC) // LANE


def _softplus(x):
    return jnp.maximum(x, 0.0) + jnp.log(1.0 + jnp.exp(-jnp.abs(x)))


def _dt_fwd(proj, dt_bias, *, tm=512):
    t = proj.shape[0]

    def body(p_ref, b_ref, o_ref):
        o_ref[...] = _softplus(p_ref[...] + b_ref[...])

    return _pallas(
        body, name="dt_fwd", grid=(t // tm,),
        in_specs=[pl.BlockSpec((tm, LANE), lambda i: (i, DT_BLK)), pl.BlockSpec((1, LANE), lambda i: (0, 0))],
        out_specs=pl.BlockSpec((tm, LANE), lambda i: (i, 0)),
        out_shape=jax.ShapeDtypeStruct((t, LANE), F32), compiler_params=_params(1),
    )(proj, dt_bias)


def _dt_bwd(proj, dt_bias, ddt, *, tm=512):
    t = proj.shape[0]

    def body(p_ref, b_ref, d_ref, o_ref, db_ref):
        i = pl.program_id(0)
        lanes = lax.broadcasted_iota(jnp.int32, (tm, LANE), 1)
        dr = jnp.where(lanes < SSM_H, d_ref[...] * _sigmoid(p_ref[...] + b_ref[...]), 0.0)
        o_ref[...] = dr

        @pl.when(i == 0)
        def _():
            db_ref[...] = jnp.zeros_like(db_ref)

        db_ref[...] += jnp.sum(dr, axis=0, keepdims=True)

    vec = pl.BlockSpec((1, LANE), lambda i: (0, 0))
    row = pl.BlockSpec((tm, LANE), lambda i: (i, 0))
    return _pallas(
        body, name="dt_bwd", grid=(t // tm,),
        in_specs=[pl.BlockSpec((tm, LANE), lambda i: (i, DT_BLK)), vec, row],
        out_specs=[row, vec],
        out_shape=[jax.ShapeDtypeStruct((t, LANE), F32), jax.ShapeDtypeStruct((1, LANE), F32)],
        compiler_params=_params(1),
    )(proj, dt_bias, ddt)


def _xconv_fwd(proj, conv_w, conv_b, seq, *, tm=512):
    t = proj.shape[0]
    tps = seq // tm
    hb = tm // PH

    def body(x_ref, h_ref, w_ref, b_ref, o_ref, sh_ref):
        i = pl.program_id(1)
        sh_ref[0, 0:PH, :] = h_ref[...] * jnp.where(i % tps == 0, 0.0, 1.0)
        sh_ref[0, PH:PH + tm, :] = x_ref[...]
        _make_shifts(sh_ref, tm, X_SHIFTS)

        def chunk(ci, carry):
            r0 = pl.multiple_of(ci * XRC, XRC)
            acc = jnp.zeros((XRC, CT), F32) + b_ref[...]
            for k in range(SSM_K):
                acc = acc + w_ref[k:k + 1, :] * _shifted(sh_ref, r0, PH - SSM_K + 1 + k, XRC)
            o_ref[pl.ds(r0, XRC), :] = acc * _sigmoid(acc)
            return carry

        lax.fori_loop(0, tm // XRC, chunk, 0)

    return _pallas(
        body, name="xconv_fwd", grid=(XBC // CT, t // tm),
        in_specs=[pl.BlockSpec((tm, CT), lambda j, i: (i, XBC_BLK + j)),
                  pl.BlockSpec((PH, CT), lambda j, i: (jnp.maximum(i * hb - 1, 0), XBC_BLK + j)),
                  pl.BlockSpec((PH, CT), lambda j, i: (0, j)),
                  pl.BlockSpec((1, CT), lambda j, i: (0, j))],
        out_specs=pl.BlockSpec((tm, CT), lambda j, i: (i, j)),
        out_shape=jax.ShapeDtypeStruct((t, XBC), F32),
        scratch_shapes=[pltpu.VMEM((SUB, tm + PH, CT), F32)], compiler_params=_params(2),
    )(proj, proj, conv_w, conv_b)


def _xconv_bwd(proj, dxc, conv_w, conv_b, seq, *, tm=512):
    t = proj.shape[0]
    tps = seq // tm
    hb = tm // PH
    nhb = t // PH
    te = tm + PH

    def fold(v):
        out = v[0:SUB]
        for q in range(1, v.shape[0] // SUB):
            out = out + v[q * SUB:(q + 1) * SUB]
        return out

    def body(x_ref, p_ref, n_ref, d_ref, nd_ref, w_ref, b_ref, dx_ref, dw_ref, db_ref, sx_ref, sd_ref):
        i = pl.program_id(1)
        first = i % tps == 0
        last = i % tps == tps - 1

        @pl.when(i == 0)
        def _():
            dw_ref[...] = jnp.zeros_like(dw_ref)
            db_ref[...] = jnp.zeros_like(db_ref)

        sx_ref[0, 0:PH, :] = p_ref[...] * jnp.where(first, 0.0, 1.0)
        sx_ref[0, PH:PH + tm, :] = x_ref[...]
        sx_ref[0, PH + tm:PH + te, :] = n_ref[...]
        _make_shifts(sx_ref, te, X_SHIFTS)

        def dv_of(r0, rows, dy):
            acc = jnp.zeros((rows, CT), F32) + b_ref[...]
            for k in range(SSM_K):
                acc = acc + w_ref[k:k + 1, :] * _shifted(sx_ref, r0, PH - SSM_K + 1 + k, rows)
            return dy * _dsilu(acc, _sigmoid(acc))

        def dv_chunk(ci, carry):
            r0 = pl.multiple_of(ci * XRC, XRC)
            dv = dv_of(r0, XRC, d_ref[pl.ds(r0, XRC), :])
            sd_ref[0, pl.ds(r0, XRC), :] = dv
            db_ref[...] += fold(dv)
            return carry

        lax.fori_loop(0, tm // XRC, dv_chunk, 0)
        sd_ref[0, tm:te, :] = dv_of(tm, PH, nd_ref[...]) * jnp.where(last, 0.0, 1.0)
        _make_shifts(sd_ref, tm, D_SHIFTS)

        def tap_chunk(ci, carry):
            r0 = pl.multiple_of(ci * XRC, XRC)
            dx = jnp.zeros((XRC, CT), F32)
            for k in range(SSM_K):
                dx = dx + w_ref[k:k + 1, :] * _shifted(sd_ref, r0, SSM_K - 1 - k, XRC)
            dx_ref[pl.ds(r0, XRC), :] = dx.astype(BF16)
            dv = sd_ref[0, pl.ds(r0, XRC), :]
            for k in range(SSM_K):
                dw_ref[k * SUB:(k + 1) * SUB, :] += fold(dv * _shifted(sx_ref, r0, PH - SSM_K + 1 + k, XRC))
            return carry

        lax.fori_loop(0, tm // XRC, tap_chunk, 0)

    return _pallas(
        body, name="xconv_bwd", grid=(XBC // CT, t // tm),
        in_specs=[pl.BlockSpec((tm, CT), lambda j, i: (i, XBC_BLK + j)),
                  pl.BlockSpec((PH, CT), lambda j, i: (jnp.maximum(i * hb - 1, 0), XBC_BLK + j)),
                  pl.BlockSpec((PH, CT), lambda j, i: (jnp.minimum((i + 1) * hb, nhb - 1), XBC_BLK + j)),
                  pl.BlockSpec((tm, CT), lambda j, i: (i, j)),
                  pl.BlockSpec((PH, CT), lambda j, i: (jnp.minimum((i + 1) * hb, nhb - 1), j)),
                  pl.BlockSpec((PH, CT), lambda j, i: (0, j)),
                  pl.BlockSpec((1, CT), lambda j, i: (0, j))],
        out_specs=[pl.BlockSpec((tm, CT), lambda j, i: (i, j)),
                   pl.BlockSpec((PH * SUB, CT), lambda j, i: (0, j)),
                   pl.BlockSpec((SUB, CT), lambda j, i: (0, j))],
        out_shape=[jax.ShapeDtypeStruct((t, XBC), BF16), jax.ShapeDtypeStruct((PH * SUB, XBC), F32),
                   jax.ShapeDtypeStruct((SUB, XBC), F32)],
        scratch_shapes=[pltpu.VMEM((SUB, tm + 2 * PH, CT), F32), pltpu.VMEM((SUB, te, CT), F32)],
        compiler_params=_params(2),
    )(proj, proj, proj, dxc, dxc, conv_w, conv_b)


def _ssd_common(xbc, dt, alog, ex):
    L = CHUNK
    a = -jnp.exp(alog)
    la = dt * a
    li = lax.broadcasted_iota(jnp.int32, (L, L), 0)
    si = lax.broadcasted_iota(jnp.int32, (L, L), 1)
    lower = si <= li
    tri = jnp.where(lower, 1.0, 0.0).astype(BF16)
    cs = _xdot(tri, la)
    cst = _dotx(la, tri, (((0,), (1,)), ((), ())))
    csl = cs[L - 1:L, :]
    ecs_x = _dotx2(jnp.exp(cs)[:, 0:SSM_H], ex)
    tail_x = _dotx2(jnp.exp(csl - cs)[:, 0:SSM_H], ex)
    dt_x = _dotx2(dt[:, 0:SSM_H], ex)
    return a, la, lower, tri, cs, cst, ecs_x, tail_x, dt_x


def _ssd_fwd(xbc_c, dt, a_log, ex, nb, seq):
    t = xbc_c.shape[0]
    L = CHUNK
    nc = seq // L
    GW = SSM_R * SSM_P

    def body(x_ref, dt_ref, al_ref, ex_ref, y_ref, st_ref, state):
        c = pl.program_id(1)

        @pl.when(c == 0)
        def _():
            state[...] = jnp.zeros_like(state)

        st_ref[0] = state[...]
        xbc = x_ref[...]
        _, _, lower, _, cs, cst, ecs_x, tail_x, dt_x = _ssd_common(xbc, dt_ref[...], al_ref[...], ex_ref[...])
        xd = xbc[:, 0:D_INNER] * dt_x
        xdb = xd.astype(BF16)
        xt = (xd * tail_x).astype(BF16)
        el_x = ecs_x[L - 1:L, :]
        for g in range(SSM_G):
            bg = xbc[:, D_INNER + g * SSM_N:D_INNER + (g + 1) * SSM_N].astype(BF16)
            cg = xbc[:, D_INNER + (SSM_G + g) * SSM_N:D_INNER + (SSM_G + g + 1) * SSM_N].astype(BF16)
            cb = _dot(cg, bg, NT)
            sg = state[:, g * GW:(g + 1) * GW]
            ys = _dot(cg, sg) * ecs_x[:, g * GW:(g + 1) * GW]
            for r in range(SSM_R):
                h = g * SSM_R + r
                seg = cs[:, h:h + 1] - cst[h:h + 1, :]
                dec = jnp.exp(jnp.where(lower, seg, -1e30))
                yh = _dot(cb * dec, xdb[:, h * SSM_P:(h + 1) * SSM_P])
                y_ref[:, h * SSM_P:(h + 1) * SSM_P] = yh + ys[:, r * SSM_P:(r + 1) * SSM_P]
            state[:, g * GW:(g + 1) * GW] = sg * el_x[:, g * GW:(g + 1) * GW] + _dot(bg, xt[:, g * GW:(g + 1) * GW], TN)

    return _pallas(
        body, name="ssd_fwd", grid=(nb, nc),
        in_specs=[pl.BlockSpec((L, XBC), lambda b, c: (b * nc + c, 0)),
                  pl.BlockSpec((L, LANE), lambda b, c: (b * nc + c, 0)),
                  pl.BlockSpec((1, LANE), lambda b, c: (0, 0)),
                  pl.BlockSpec((SSM_H, D_INNER), lambda b, c: (0, 0))],
        out_specs=[pl.BlockSpec((L, D_INNER), lambda b, c: (b * nc + c, 0)),
                   pl.BlockSpec((1, SSM_N, D_INNER), lambda b, c: (b * nc + c, 0, 0))],
        out_shape=[jax.ShapeDtypeStruct((t, D_INNER), F32),
                   jax.ShapeDtypeStruct((nb * nc, SSM_N, D_INNER), F32)],
        scratch_shapes=[pltpu.VMEM((SSM_N, D_INNER), F32)], compiler_params=_params(2),
    )(xbc_c, dt, a_log, ex)


def _ssd_bwd(xbc_c, dt, a_log, ex, ext, states, dy, d_x, nb, seq):
    t = xbc_c.shape[0]
    L = CHUNK
    nc = seq // L
    GW = SSM_R * SSM_P

    def body(x_ref, dt_ref, al_ref, ex_ref, ext_ref, st_ref, dy_ref, sk_ref, dx_ref, ddt_ref, da_ref,
             dstate, dxd, yd, lastv):
        b = pl.program_id(0)
        c = pl.program_id(1)

        @pl.when(c == 0)
        def _():
            dstate[...] = jnp.zeros_like(dstate)

        @pl.when(jnp.logical_and(b == 0, c == 0))
        def _():
            da_ref[...] = jnp.zeros_like(da_ref)

        xbc = x_ref[...]
        dtv = dt_ref[...]
        ex_t = ext_ref[...]
        a, la, lower, tri, cs, cst, ecs_x, tail_x, dt_x = _ssd_common(xbc, dtv, al_ref[...], ex_ref[...])
        xs = xbc[:, 0:D_INNER]
        xd = xs * dt_x
        xdb = xd.astype(BF16)
        dyv = dy_ref[...]
        dyb = dyv.astype(BF16)
        dys = dyv * ecs_x
        xt = xd * tail_x
        el_x = ecs_x[L - 1:L, :]
        lane = lax.broadcasted_iota(jnp.int32, (L, LANE), 1)
        sub = lax.broadcasted_iota(jnp.int32, (LANE, L), 0)
        row_part = jnp.zeros((L, LANE), F32)
        col_part = jnp.zeros((LANE, L), F32)
        for g in range(SSM_G):
            gs = slice(g * GW, (g + 1) * GW)
            bcol = slice(D_INNER + g * SSM_N, D_INNER + (g + 1) * SSM_N)
            ccol = slice(D_INNER + (SSM_G + g) * SSM_N, D_INNER + (SSM_G + g + 1) * SSM_N)
            bg = xbc[:, bcol].astype(BF16)
            cg = xbc[:, ccol].astype(BF16)
            cb = _dot(cg, bg, NT)
            sg = st_ref[0, :, gs]
            dsg = dstate[:, gs]
            dc = _dot(dys[:, gs], sg, NT)
            db = _dot(xt[:, gs], dsg, NT)
            dx_state = tail_x[:, gs] * _dot(bg, dsg)
            tail_part = xd[:, gs] * dx_state
            yd[:, gs] = dys[:, gs] * _dot(cg, sg) - tail_part
            last = jnp.sum(tail_part, axis=0, keepdims=True) + el_x[:, gs] * jnp.sum(dsg * sg, axis=0, keepdims=True)
            lastv[:, gs] = jnp.broadcast_to(last, (8, GW))
            dcb = jnp.zeros((L, L), F32)
            for r in range(SSM_R):
                h = g * SSM_R + r
                hs = slice(h * SSM_P, (h + 1) * SSM_P)
                seg = cs[:, h:h + 1] - cst[h:h + 1, :]
                dec = jnp.exp(jnp.where(lower, seg, -1e30))
                m = cb * dec
                dm = _dot(dyb[:, hs], xdb[:, hs], NT)
                dcb = dcb + dm * dec
                e = dm * m
                row_part = row_part + jnp.where(lane == h, jnp.sum(e, axis=1, keepdims=True), 0.0)
                col_part = col_part + jnp.where(sub == h, jnp.sum(e, axis=0, keepdims=True), 0.0)
                dxd[:, hs] = _dot(m, dyb[:, hs], TN) + dx_state[:, r * SSM_P:(r + 1) * SSM_P]
            dx_ref[:, bcol] = db + _dot(dcb, cg, TN)
            dx_ref[:, ccol] = dc + _dot(dcb, bg)
            dstate[:, gs] = dsg * el_x[:, gs] + _dot(cg, dys[:, gs], TN)
        dxv = dxd[...]
        dx_ref[:, 0:D_INNER] = dxv * dt_x + dyv * sk_ref[...]
        ddt_x = _dotx(dxv * xs, ex_t)
        yst = _dotx(yd[...], ex_t)
        lst = _dotx(lastv[...], ex_t)[0:1, :]
        rows = lax.broadcasted_iota(jnp.int32, (L, LANE), 0)
        dcs = row_part - col_part.T + yst + jnp.where(rows == L - 1, lst, 0.0)
        li = lax.broadcasted_iota(jnp.int32, (L, L), 0)
        si = lax.broadcasted_iota(jnp.int32, (L, L), 1)
        upper = jnp.where(si >= li, 1.0, 0.0).astype(BF16)
        dla = _xdot(upper, dcs)
        ddt_ref[...] = dla * a + ddt_x
        da_ref[...] += jnp.sum(dla * dtv, axis=0, keepdims=True)

    def row(w):
        return pl.BlockSpec((L, w), lambda b, c: (b * nc + nc - 1 - c, 0))

    return _pallas(
        body, name="ssd_bwd", grid=(nb, nc),
        in_specs=[row(XBC), row(LANE), pl.BlockSpec((1, LANE), lambda b, c: (0, 0)),
                  pl.BlockSpec((SSM_H, D_INNER), lambda b, c: (0, 0)),
                  pl.BlockSpec((D_INNER, LANE), lambda b, c: (0, 0)),
                  pl.BlockSpec((1, SSM_N, D_INNER), lambda b, c: (b * nc + nc - 1 - c, 0, 0)),
                  row(D_INNER), pl.BlockSpec((1, D_INNER), lambda b, c: (0, 0))],
        out_specs=[row(XBC), row(LANE), pl.BlockSpec((1, LANE), lambda b, c: (0, 0))],
        out_shape=[jax.ShapeDtypeStruct((t, XBC), F32), jax.ShapeDtypeStruct((t, LANE), F32),
                   jax.ShapeDtypeStruct((1, LANE), F32)],
        scratch_shapes=[pltpu.VMEM((SSM_N, D_INNER), F32), pltpu.VMEM((L, D_INNER), F32),
                        pltpu.VMEM((L, D_INNER), F32), pltpu.VMEM((8, D_INNER), F32)],
        compiler_params=_params(2),
    )(xbc_c, dt, a_log, ex, ext, states, dy, d_x)


def _group_rms(y2):
    gw = D_INNER // SSM_G
    parts = []
    for g in range(SSM_G):
        v = y2[:, g * gw:(g + 1) * gw]
        r = lax.rsqrt(jnp.mean(v * v, axis=1, keepdims=True) + EPS)
        parts.append(jnp.broadcast_to(r, v.shape))
    return jnp.concatenate(parts, axis=1)


def _gate_fwd(y, xbc_c, proj, d_x, gn_w, *, tm=256):
    t = y.shape[0]

    def body(y_ref, x_ref, z_ref, d_ref, w_ref, o_ref):
        y1 = y_ref[...] + d_ref[...] * x_ref[...]
        zv = z_ref[...]
        y2 = y1 * zv * _sigmoid(zv)
        o_ref[...] = (y2 * _group_rms(y2) * w_ref[...]).astype(BF16)

    row = pl.BlockSpec((tm, D_INNER), lambda i: (i, 0))
    vec = pl.BlockSpec((1, D_INNER), lambda i: (0, 0))
    return _pallas(
        body, name="gate_fwd", grid=(t // tm,), in_specs=[row, row, row, vec, vec], out_specs=row,
        out_shape=jax.ShapeDtypeStruct((t, D_INNER), BF16), compiler_params=_params(1),
    )(y, xbc_c, proj, d_x, gn_w)


def _gate_bwd(dyg, y, xbc_c, proj, d_x, gn_w, *, tm=256):
    t = y.shape[0]
    gw = D_INNER // SSM_G

    def body(dg_ref, y_ref, x_ref, z_ref, d_ref, w_ref, dy_ref, dz_ref, dw_ref, dd_ref):
        i = pl.program_id(0)
        xv = x_ref[...]
        dxv = d_ref[...]
        y1 = y_ref[...] + dxv * xv
        zv = z_ref[...]
        sz = _sigmoid(zv)
        y2 = y1 * zv * sz
        rr = _group_rms(y2)
        xh = y2 * rr
        dg = _f32(dg_ref[...])
        gq = dg * w_ref[...]
        prod = gq * xh
        means = []
        for g in range(SSM_G):
            mg = jnp.mean(prod[:, g * gw:(g + 1) * gw], axis=1, keepdims=True)
            means.append(jnp.broadcast_to(mg, (tm, gw)))
        dy2 = rr * (gq - xh * jnp.concatenate(means, axis=1))
        dy1 = dy2 * zv * sz
        dy_ref[...] = dy1
        dz_ref[...] = (dy2 * y1 * _dsilu(zv, sz)).astype(BF16)

        @pl.when(i == 0)
        def _():
            dw_ref[...] = jnp.zeros_like(dw_ref)
            dd_ref[...] = jnp.zeros_like(dd_ref)

        dw_ref[...] += jnp.sum(dg * xh, axis=0, keepdims=True)
        dd_ref[...] += jnp.sum(dy1 * xv, axis=0, keepdims=True)

    row = pl.BlockSpec((tm, D_INNER), lambda i: (i, 0))
    vec = pl.BlockSpec((1, D_INNER), lambda i: (0, 0))
    return _pallas(
        body, name="gate_bwd", grid=(t // tm,), in_specs=[row, row, row, row, vec, vec],
        out_specs=[row, row, vec, vec],
        out_shape=[jax.ShapeDtypeStruct((t, D_INNER), F32),
                   jax.ShapeDtypeStruct((t, D_INNER), BF16), jax.ShapeDtypeStruct((1, D_INNER), F32),
                   jax.ShapeDtypeStruct((1, D_INNER), F32)],
        compiler_params=_params(1),
    )(dyg, y, xbc_c, proj, d_x, gn_w)


ANY = pl.BlockSpec(memory_space=pl.ANY)


def _remote(src, dst, sems, k, to):
    send_sems, recv_sems = sems
    return pltpu.make_async_remote_copy(src_ref=src, dst_ref=dst, send_sem=send_sems.at[k], recv_sem=recv_sems.at[k],
                                        device_id=to, device_id_type=MESH)


NCHIP = 4


def _gathered_shape(kind, shard):
    r, n = shard.shape
    shape = {"cols": (r, NCHIP * n), "slab": (NCHIP, r, n), "rows": (NCHIP * r, n)}[kind]
    return jax.ShapeDtypeStruct(shape, shard.dtype)


def _gather_plan(kinds, shards, outs, sems, small=None):
    ici_s, ici_r, d2d_s, d2d_r = sems
    nw = len(shards)
    per = nw + (small is not None)

    def place():
        x, y, c = lax.axis_index("x"), lax.axis_index("y"), lax.axis_index("c")
        return 2 * x + y, c, (x, y, 1 - c), [(1 - x, y), (x, 1 - y), (1 - x, 1 - y)]

    def region(j, chip, half):
        r, n = shards[j].shape
        h = r // 2
        if kinds[j] == "cols":
            return outs[j].at[pl.ds(half * h, h), pl.ds(pl.multiple_of(chip * n, LANE), n)]
        if kinds[j] == "slab":
            return outs[j].at[chip, pl.ds(half * h, h), :]
        return outs[j].at[pl.ds(chip * r + half * h, h), :]

    def my_sends(me, c, peers):
        cps = []
        for k, (px, py) in enumerate(peers):
            for j in range(nw):
                h = shards[j].shape[0] // 2
                cps.append(_remote(shards[j].at[pl.ds(c * h, h), :], region(j, me, c), (ici_s, ici_r), per * k + j, (px, py, c)))
            if small is not None:
                cps.append(_remote(small[0], small[1].at[me], (ici_s, ici_r), per * k + nw, (px, py, c)))
        return cps

    def start():
        me, c, _, peers = place()
        for cp in my_sends(me, c, peers):
            cp.start()

    def finish():
        me, c, sib, peers = place()
        fwds = []
        for k, (px, py) in enumerate(peers):
            q = 2 * px + py
            for j in range(nw):
                d = region(j, q, c)
                _remote(d, d, (ici_s, ici_r), per * k + j, (px, py, c)).wait_recv()
                fwds.append(_remote(d, d, (d2d_s, d2d_r), nw * k + j, sib))
                fwds[-1].start()
            if small is not None:
                _remote(small[0], small[1].at[q], (ici_s, ici_r), per * k + nw, (px, py, c)).wait_recv()
        for k, (px, py) in enumerate(peers):
            for j in range(nw):
                d = region(j, 2 * px + py, 1 - c)
                _remote(d, d, (d2d_s, d2d_r), nw * k + j, sib).wait_recv()
        for cp in my_sends(me, c, peers) + fwds:
            cp.wait_send()

    return start, finish


def _gather_sems(nw, with_small):
    n_ici = 3 * (nw + with_small)
    return [pltpu.SemaphoreType.DMA((n_ici,)), pltpu.SemaphoreType.DMA((n_ici,)),
            pltpu.SemaphoreType.DMA((3 * nw,)), pltpu.SemaphoreType.DMA((3 * nw,))]


def _gather_shards(kinds, shards, small):
    nw = len(shards)

    def body(*refs):
        ins, sm, outs, osm, sems = refs[:nw], refs[nw], refs[nw + 1:2 * nw + 1], refs[2 * nw + 1], refs[2 * nw + 2:]
        start, finish = _gather_plan(kinds, ins, outs, sems, small=(sm, osm))
        start()
        finish()

    return _pallas(
        body, name="gather_shards", in_specs=[ANY] * (nw + 1), out_specs=[ANY] * (nw + 1),
        out_shape=[_gathered_shape(kd, s) for kd, s in zip(kinds, shards)]
        + [jax.ShapeDtypeStruct((NCHIP,) + small.shape, small.dtype)],
        scratch_shapes=_gather_sems(nw, 1),
    )(*shards, small)


def _matmul_with_gather(a, b, kinds, shards, *, out_dtype, bm, bn, name):
    (m, k), n = a.shape, b.shape[1]
    nw = len(shards)
    nj, ni = n // bn, m // bm

    def body(*refs):
        a_ref, b_ref, ins, o_ref = refs[0], refs[1], refs[2:2 + nw], refs[2 + nw]
        outs, sems = refs[3 + nw:3 + 2 * nw], refs[3 + 2 * nw:]
        start, finish = _gather_plan(kinds, ins, outs, sems)
        j, i = pl.program_id(0), pl.program_id(1)

        @pl.when(jnp.logical_and(j == 0, i == 0))
        def _():
            start()

        o_ref[...] = _dot(a_ref[...], b_ref[...]).astype(out_dtype)

        @pl.when(jnp.logical_and(j == nj - 1, i == ni - 1))
        def _():
            finish()

    return _pallas(
        body, name=name, grid=(nj, ni),
        in_specs=[pl.BlockSpec((bm, k), lambda j, i: (i, 0)), pl.BlockSpec((k, bn), lambda j, i: (0, j))] + [ANY] * nw,
        out_specs=[pl.BlockSpec((bm, bn), lambda j, i: (i, j))] + [ANY] * nw,
        out_shape=[jax.ShapeDtypeStruct((m, n), out_dtype)] + [_gathered_shape(kd, s) for kd, s in zip(kinds, shards)],
        scratch_shapes=_gather_sems(nw, 0), compiler_params=_params(2),
    )(a, b, *shards)


def _pair_exchange(g_in0, g_in1, g_out0, g_out1, gsmall, grep):
    dm = g_in0.shape[0]
    hr = dm // 2
    ho = g_out0.shape[1] // 2

    def body(a0, a1, b0, b1, sm, rp, q0, q1, r0, r1, osm, orp, pair_s, pair_r, send_sems, recv_sems, local_sems):
        x, y, c = lax.axis_index("x"), lax.axis_index("y"), lax.axis_index("c")
        me = 4 * x + 2 * y + c
        chip = 2 * x + y
        sib = (x, y, 1 - c)
        rows = pl.ds(pl.multiple_of((1 - c) * hr, 8), hr)
        orows = pl.ds(pl.multiple_of((1 - c) * ho, 8), ho)
        pair = [_remote(a0.at[rows, :], q0, (pair_s, pair_r), 0, sib),
                _remote(a1.at[rows, :], q1, (pair_s, pair_r), 1, sib),
                _remote(b0.at[:, orows, :], r0, (pair_s, pair_r), 2, sib),
                _remote(b1.at[:, orows, :], r1, (pair_s, pair_r), 3, sib)]
        for cp in pair:
            cp.start()
        own = [pltpu.make_async_copy(sm.at[chip], osm.at[me], local_sems.at[0]),
               pltpu.make_async_copy(rp, orp.at[me], local_sems.at[1])]
        for cp in own:
            cp.start()
        peers = []
        for k in range(7):
            fx, fy, fc = ((k + 1) >> 2) & 1, ((k + 1) >> 1) & 1, (k + 1) & 1
            peers.append((1 - x if fx else x, 1 - y if fy else y, 1 - c if fc else c))
        sends = []
        for k, (px, py, pc) in enumerate(peers):
            sends.append(_remote(sm.at[2 * px + py], osm.at[me], (send_sems, recv_sems), 2 * k, (px, py, pc)))
            sends.append(_remote(rp, orp.at[me], (send_sems, recv_sems), 2 * k + 1, (px, py, pc)))
        for cp in sends:
            cp.start()
        for k, (px, py, pc) in enumerate(peers):
            slot = 4 * px + 2 * py + pc
            _remote(sm.at[chip], osm.at[slot], (send_sems, recv_sems), 2 * k, (px, py, pc)).wait_recv()
            _remote(rp, orp.at[slot], (send_sems, recv_sems), 2 * k + 1, (px, py, pc)).wait_recv()
        for cp in pair:
            cp.wait_recv()
        for cp in pair + sends:
            cp.wait_send()
        for cp in own:
            cp.wait()

    return _pallas(
        body, name="pair_exchange", in_specs=[ANY] * 6, out_specs=[ANY] * 6,
        out_shape=[jax.ShapeDtypeStruct((hr, g_in0.shape[1]), F32),
                   jax.ShapeDtypeStruct((hr, g_in1.shape[1]), F32),
                   jax.ShapeDtypeStruct((g_out0.shape[0], ho, g_out0.shape[2]), F32),
                   jax.ShapeDtypeStruct((g_out1.shape[0], ho, g_out1.shape[2]), F32),
                   jax.ShapeDtypeStruct((8,) + gsmall.shape[1:], F32),
                   jax.ShapeDtypeStruct((8,) + grep.shape, F32)],
        scratch_shapes=[pltpu.SemaphoreType.DMA((4,)), pltpu.SemaphoreType.DMA((4,)),
                        pltpu.SemaphoreType.DMA((14,)), pltpu.SemaphoreType.DMA((14,)),
                        pltpu.SemaphoreType.DMA((2,))],
    )(g_in0, g_in1, g_out0, g_out1, gsmall, grep)


def _core_index():
    return lax.axis_index("c").astype(jnp.int32).reshape(1)


def _half_add(full, other, *, axis, block, name):
    nd = full.ndim
    nblk = other.shape[axis] // block[axis]
    grid = tuple(other.shape[d] // block[d] for d in range(nd))

    def body(c_ref, f_ref, o_ref, out_ref):
        out_ref[...] = (f_ref[...] + o_ref[...]).astype(BF16)

    def full_map(*idx):
        ids, c_ref = list(idx[:nd]), idx[nd]
        ids[axis] = ids[axis] + c_ref[0] * nblk
        return tuple(ids)

    def plain_map(*idx):
        return tuple(idx[:nd])

    return _pallas(
        body, name=name,
        grid_spec=pltpu.PrefetchScalarGridSpec(
            num_scalar_prefetch=1, grid=grid,
            in_specs=[pl.BlockSpec(block, full_map), pl.BlockSpec(block, plain_map)],
            out_specs=pl.BlockSpec(block, plain_map)),
        out_shape=jax.ShapeDtypeStruct(other.shape, BF16), compiler_params=_params(nd),
    )(_core_index(), full, other)


def _chip_exchange(s_in0, s_in1, s_out0, s_out1):
    npeer = 3
    n0 = s_in0.shape[1] // 4

    def body(a0, a1, b0, b1, l0, l1, m0, m1, send_sems, recv_sems):
        x, y, c = lax.axis_index("x"), lax.axis_index("y"), lax.axis_index("c")
        me = 2 * x + y
        peers = [(1 - x, y), (x, 1 - y), (1 - x, 1 - y)]

        def pieces(chip):
            return [a0.at[:, pl.ds(pl.multiple_of(chip * n0, LANE), n0)], a1.at[chip], b0.at[chip], b1.at[chip]]

        def slots(k):
            return [l0.at[k], l1.at[k], m0.at[k], m1.at[k]]

        sends = []
        for k, (px, py) in enumerate(peers):
            for j, (s, d) in enumerate(zip(pieces(2 * px + py), slots(k))):
                sends.append(_remote(s, d, (send_sems, recv_sems), 4 * k + j, (px, py, c)))
        for cp in sends:
            cp.start()
        for k, (px, py) in enumerate(peers):
            for j, (s, d) in enumerate(zip(pieces(me), slots(k))):
                _remote(s, d, (send_sems, recv_sems), 4 * k + j, (px, py, c)).wait_recv()
        for cp in sends:
            cp.wait_send()

    return _pallas(
        body, name="chip_exchange", in_specs=[ANY] * 4, out_specs=[ANY] * 4,
        out_shape=[jax.ShapeDtypeStruct((npeer, s_in0.shape[0], n0), BF16),
                   jax.ShapeDtypeStruct((npeer,) + s_in1.shape[1:], BF16),
                   jax.ShapeDtypeStruct((npeer,) + s_out0.shape[1:], BF16),
                   jax.ShapeDtypeStruct((npeer,) + s_out1.shape[1:], BF16)],
        scratch_shapes=[pltpu.SemaphoreType.DMA((12,)), pltpu.SemaphoreType.DMA((12,))],
    )(s_in0, s_in1, s_out0, s_out1)


def _chip_index():
    return (2 * lax.axis_index("x") + lax.axis_index("y")).astype(jnp.int32).reshape(1)


def _chip_sum(own, slots, *, own_block, own_map, block, name):
    npeer = slots.shape[0]
    shape = slots.shape[1:]
    grid = (shape[0] // block[0], shape[1] // block[1])

    def body(p_ref, own_ref, s_ref, o_ref):
        acc = own_ref[...].reshape(block).astype(F32)
        for q in range(npeer):
            acc = acc + s_ref[q].astype(F32)
        o_ref[...] = acc

    return _pallas(
        body, name=name,
        grid_spec=pltpu.PrefetchScalarGridSpec(
            num_scalar_prefetch=1, grid=grid,
            in_specs=[pl.BlockSpec(own_block, own_map),
                      pl.BlockSpec((npeer,) + block, lambda i, j, p: (0, i, j))],
            out_specs=pl.BlockSpec(block, lambda i, j, p: (i, j))),
        out_shape=jax.ShapeDtypeStruct(shape, F32), compiler_params=_params(2),
    )(_chip_index(), own, slots)


def _pair_share(r_in0, r_in1, r_out0, r_out1):
    def body(a0, a1, b0, b1, g0, g1, h0, h1, send_sems, recv_sems):
        x, y, c = lax.axis_index("x"), lax.axis_index("y"), lax.axis_index("c")
        sib = (x, y, 1 - c)
        sends = [_remote(s, d, (send_sems, recv_sems), j, sib)
                 for j, (s, d) in enumerate(zip([a0, a1, b0, b1], [g0, g1, h0, h1]))]
        for cp in sends:
            cp.start()
        for cp in sends:
            cp.wait()

    return _pallas(
        body, name="pair_share", in_specs=[ANY] * 4, out_specs=[ANY] * 4,
        out_shape=[jax.ShapeDtypeStruct(r.shape, F32) for r in (r_in0, r_in1, r_out0, r_out1)],
        scratch_shapes=[pltpu.SemaphoreType.DMA((4,)), pltpu.SemaphoreType.DMA((4,))],
    )(r_in0, r_in1, r_out0, r_out1)


def _adam_math(g, w, m, v):
    c1 = 1.0 - ADAM_B1 ** ADAM_STEP
    c2 = 1.0 - ADAM_B2 ** ADAM_STEP
    m2 = ADAM_B1 * m + (1.0 - ADAM_B1) * g
    v2 = ADAM_B2 * v + (1.0 - ADAM_B2) * (g * g)
    delta = -ADAM_LR * ((m2 / c1) / (jnp.sqrt(v2 / c2) + ADAM_EPS) + ADAM_WD * w)
    return delta, m2, v2


def _adamw_nat(g_mine, g_sib, w, m, v, *, name, tr):
    rows, cw = w.shape
    nt = g_mine.shape[0] // tr

    def body(c_ref, gm_ref, gs_ref, w_ref, m_ref, v_ref, go_ref, d_ref, nm_ref, nv_ref):
        mine = pl.program_id(0) // nt == c_ref[0]
        gv = jnp.where(mine, gm_ref[...], gs_ref[...])[:, 0:cw]
        delta, m2, v2 = _adam_math(gv, w_ref[...], m_ref[...], v_ref[...])
        go_ref[...] = gv
        d_ref[...] = delta
        nm_ref[...] = m2
        nv_ref[...] = v2

    def mine_map(i, c_ref):
        return (jnp.where(i // nt == c_ref[0], i % nt, 0), 0)

    def sib_map(i, c_ref):
        return (jnp.where(i // nt == c_ref[0], 0, i % nt), 0)

    row = pl.BlockSpec((tr, cw), lambda i, c_ref: (i, 0))
    gspec = (tr, g_mine.shape[1])
    out = jax.ShapeDtypeStruct((rows, cw), F32)
    return _pallas(
        body, name=name,
        grid_spec=pltpu.PrefetchScalarGridSpec(
            num_scalar_prefetch=1, grid=(rows // tr,),
            in_specs=[pl.BlockSpec(gspec, mine_map), pl.BlockSpec(gspec, sib_map), row, row, row],
            out_specs=[row, row, row, row]),
        out_shape=[out, out, out, out], compiler_params=_params(1),
    )(_core_index(), g_mine, g_sib, w, m, v)


def _adamw(slots, w, m, v, *, name, tr):
    nd, rows, _ = slots.shape
    c1 = 1.0 - ADAM_B1 ** ADAM_STEP
    c2 = 1.0 - ADAM_B2 ** ADAM_STEP

    def body(s_ref, w_ref, m_ref, v_ref, g_ref, d_ref, nm_ref, nv_ref):
        g = s_ref[0]
        for d in range(1, nd):
            g = g + s_ref[d]
        m2 = ADAM_B1 * m_ref[...] + (1.0 - ADAM_B1) * g
        v2 = ADAM_B2 * v_ref[...] + (1.0 - ADAM_B2) * (g * g)
        g_ref[...] = g
        nm_ref[...] = m2
        nv_ref[...] = v2
        d_ref[...] = -ADAM_LR * ((m2 / c1) / (jnp.sqrt(v2 / c2) + ADAM_EPS) + ADAM_WD * w_ref[...])

    row = pl.BlockSpec((tr, LANE), lambda i: (i, 0))
    out = jax.ShapeDtypeStruct((rows, LANE), F32)
    return _pallas(
        body, name=name, grid=(rows // tr,),
        in_specs=[pl.BlockSpec((nd, tr, LANE), lambda i: (0, i, 0)), row, row, row],
        out_specs=[row, row, row, row], out_shape=[out, out, out, out], compiler_params=_params(1),
    )(slots, w, m, v)


def _rows(a):
    return a.reshape(-1, LANE)


def _pad_rows(a, mult):
    pad = (-a.shape[0]) % mult
    return jnp.pad(a, ((0, pad), (0, 0))) if pad else a


def _pack(parts, mult):
    return _pad_rows(jnp.concatenate([_rows(p) for p in parts], axis=0), mult)


def _unpack(slab, shapes):
    out, r0 = [], 0
    for shp in shapes:
        n = 1
        for s in shp:
            n *= s
        r = n // LANE
        out.append(slab[r0:r0 + r].reshape(shp))
        r0 += r
    return out


def _pack_rep(vecs, scal):
    srow = jnp.concatenate([s.reshape(-1) for s in scal] + [jnp.zeros((LANE - 3 * SSM_H,), F32)]).reshape(1, LANE)
    return _pad_rows(jnp.concatenate([_rows(vv) for vv in vecs] + [srow], axis=0), 8)


def _unpack_rep(slab, vec_shapes, scal_shape):
    vecs, r0 = [], 0
    for shp in vec_shapes:
        vecs.append(slab[r0:r0 + 8].reshape(shp))
        r0 += 8
    srow = slab[r0]
    scal = [srow[i * SSM_H:(i + 1) * SSM_H].reshape(scal_shape) for i in range(3)]
    return vecs, scal


def kernel(x, ev_norm_w, ev_w_in, ev_dw_w, ev_dw_b, ev_ln_w, ev_ln_b, ev_w_out, od_norm_w, od_w_in, od_conv_w, od_conv_b, od_dt_bias, od_a_log, od_d, od_gnorm_w, od_w_out, final_norm_w, loss_target, m_ev_norm_w, m_ev_w_in, m_ev_dw_w, m_ev_dw_b, m_ev_ln_w, m_ev_ln_b, m_ev_w_out, m_od_norm_w, m_od_w_in, m_od_conv_w, m_od_conv_b, m_od_dt_bias, m_od_a_log, m_od_d, m_od_gnorm_w, m_od_w_out, m_final_norm_w, v_ev_norm_w, v_ev_w_in, v_ev_dw_w, v_ev_dw_b, v_ev_ln_w, v_ev_ln_b, v_ev_w_out, v_od_norm_w, v_od_w_in, v_od_conv_w, v_od_conv_b, v_od_dt_bias, v_od_a_log, v_od_d, v_od_gnorm_w, v_od_w_out, v_final_norm_w):
    nb, seq, d = x.shape
    t = nb * seq
    nchip = 4
    xf = x.reshape(t, d)
    tgt = loss_target.reshape(t, d)

    big_w = [ev_w_in[0], od_w_in[0], ev_w_out[0], od_w_out[0]]
    small_w = [ev_dw_w[0], od_norm_w[0], od_conv_w[0], od_conv_b[0], od_gnorm_w[0]]
    small_shapes = [a.shape for a in small_w]
    big_b = [a.astype(BF16) for a in big_w]
    small_slab = _pack(small_w, 8)
    w_in0, w_out0, gath_small = _gather_shards(("cols", "rows"), [big_b[0], big_b[2]], small_slab)
    chip = 2 * lax.axis_index("x") + lax.axis_index("y")
    w_in0 = lax.dynamic_update_slice(w_in0, big_b[0], (0, chip * big_b[0].shape[1]))
    w_out0 = lax.dynamic_update_slice(w_out0, big_b[2], (chip * big_b[2].shape[0], 0))
    gath_small = lax.dynamic_update_slice(gath_small, small_slab[None], (chip, 0, 0))
    per_chip = [_unpack(gath_small[p], small_shapes) for p in range(nchip)]

    def cat(idx, axis):
        return jnp.concatenate([per_chip[p][idx] for p in range(nchip)], axis=axis)

    dw_w = jnp.pad(cat(0, 1), ((0, HALO - CONF_K), (0, 0)))
    dw_w8 = jnp.repeat(dw_w, SUB, axis=0)
    n1_w = cat(1, 0).reshape(1, d)
    conv_w = jnp.pad(cat(2, 1), ((0, PH - SSM_K), (0, 0)))
    conv_b = cat(3, 0).reshape(1, XBC)
    gn_w = cat(4, 0).reshape(1, D_INNER)

    def lanes(a):
        return jnp.pad(a.reshape(1, -1), ((0, 0), (0, LANE - a.size)))

    dt_bias, a_log = lanes(od_dt_bias), lanes(od_a_log)
    d_x = jnp.repeat(od_d.reshape(-1), SSM_P).reshape(1, D_INNER)
    hid = lax.broadcasted_iota(jnp.int32, (SSM_H, D_INNER), 1) // SSM_P
    ex = (hid == lax.broadcasted_iota(jnp.int32, (SSM_H, D_INNER), 0)).astype(BF16)
    ex_t = jnp.pad(ex.T, ((0, 0), (0, LANE - SSM_H)))
    fn_w = final_norm_w.reshape(1, d)

    n0 = _rms_fwd(xf, ev_norm_w, name="rms_fwd0")
    proj0, w_in1g, w_out1 = _matmul_with_gather(n0, w_in0, ("slab", "rows"), [big_b[1], big_b[3]],
                                                out_dtype=BF16, bm=512, bn=1024, name="in_proj0")
    w_in1g = lax.dynamic_update_slice(w_in1g, big_b[1][None], (chip, 0, 0))
    w_out1 = lax.dynamic_update_slice(w_out1, big_b[3], (chip * big_b[3].shape[0], 0))
    w_in1 = jnp.pad(jnp.concatenate([w_in1g[p] for p in range(nchip)], axis=1),
                    ((0, 0), (0, IN_ODD_PAD - IN_ODD)))
    y_conv, u2 = _conf_fwd(proj0, dw_w8, ev_dw_b, ev_ln_w, ev_ln_b, seq)
    o_att, y_att = _sba_fwd(proj0, nb, seq)
    ycat0 = jnp.concatenate([y_conv, y_att], axis=1)
    h1 = _matmul(ycat0, w_out0, mode="nn", out_dtype=F32, bm=512, bn=d, bk=D_INNER, name="out_proj0", residual=xf)
    n1 = _rms_fwd(h1, n1_w, name="rms_fwd1")
    proj1 = _matmul(n1, w_in1, mode="nn", out_dtype=F32, bm=512, bn=768, bk=d, name="in_proj1", n_major=True)
    xbc_c = _xconv_fwd(proj1, conv_w, conv_b, seq)
    dt = _dt_fwd(proj1, dt_bias)
    y_ssd, states = _ssd_fwd(xbc_c, dt, a_log, ex, nb, seq)
    yg = _gate_fwd(y_ssd, xbc_c, proj1, d_x, gn_w)
    h2 = _matmul(yg, w_out1, mode="nn", out_dtype=F32, bm=512, bn=d, bk=D_INNER, name="out_proj1", residual=h1)
    dh2, g_fn, loss_part = _final_loss(h2, fn_w, tgt)

    dyg = _matmul(dh2, w_out1, mode="nt", out_dtype=BF16, bm=512, bn=1024, bk=d, name="d_out_proj1")
    g_w_out1 = _matmul(yg, dh2, mode="tn", out_dtype=F32, bm=1024, bn=d, bk=1024, name="dw_out_proj1")
    dy_ssd, dz, g_gn, g_dx = _gate_bwd(dyg, y_ssd, xbc_c, proj1, d_x, gn_w)
    dxbc_c, ddt, g_a = _ssd_bwd(xbc_c, dt, a_log, ex, ex_t, states, dy_ssd, d_x, nb, seq)
    dxbc, g_conv_w, g_conv_b = _xconv_bwd(proj1, dxbc_c, conv_w, conv_b, seq)
    ddt_raw, g_dt_bias = _dt_bwd(proj1, dt_bias, ddt)
    dproj1 = jnp.concatenate([dz, dxbc, ddt_raw.astype(BF16),
                              jnp.zeros((t, IN_ODD_PAD - IN_ODD - (LANE - SSM_H)), BF16)], axis=1)
    dn1 = _matmul(dproj1, w_in1, mode="nt", out_dtype=BF16, bm=1024, bn=d, bk=1792, name="d_in_proj1")
    g_w_in1 = _matmul(n1, dproj1, mode="tn", out_dtype=F32, bm=d, bn=1792, bk=1024, name="dw_in_proj1")
    dh1, g_n1 = _rms_bwd(dn1, h1, n1_w, dh2, name="rms_bwd1")

    dycat0 = _matmul(dh1, w_out0, mode="nt", out_dtype=BF16, bm=512, bn=1024, bk=d, name="d_out_proj0")
    g_w_out0 = _matmul(ycat0, dh1, mode="tn", out_dtype=F32, bm=1024, bn=d, bk=1024, name="dw_out_proj0")
    dq, dk, dv, dga = _sba_bwd(proj0, o_att, dycat0, nb, seq)
    dpc, g_dw_w, g_dw_b, g_ln_w, g_ln_b = _conf_bwd(proj0, u2, dycat0, dw_w8, ev_ln_w, ev_ln_b, seq)
    dproj0 = jnp.concatenate([dpc, dq, dk.astype(BF16), dv.astype(BF16), dga], axis=1)
    dn0 = _matmul(dproj0, w_in0, mode="nt", out_dtype=BF16, bm=1024, bn=d, bk=1792, name="d_in_proj0")
    g_w_in0 = _matmul(n0, dproj0, mode="tn", out_dtype=F32, bm=d, bn=1792, bk=1024, name="dw_in_proj0")
    grad_x, g_n0 = _rms_bwd(dn0, xf, ev_norm_w, dh1, name="rms_bwd0")

    g_dw_w = g_dw_w.reshape(HALO, SUB, CONV_W).sum(axis=1)[0:CONF_K]
    g_dw_b, g_ln_w, g_ln_b = (a.sum(axis=0, keepdims=True) for a in (g_dw_b, g_ln_w, g_ln_b))
    g_conv_w = g_conv_w.reshape(PH, SUB, XBC).sum(axis=1)[0:SSM_K]
    g_conv_b = g_conv_b.sum(axis=0, keepdims=True)
    a_neg = -jnp.exp(od_a_log.reshape(-1))
    g_a_log = g_a[0, 0:SSM_H] * a_neg
    g_d = g_dx.reshape(SSM_H, SSM_P).sum(axis=1)
    n1 = IN_ODD // nchip
    n1p = -(-n1 // LANE) * LANE

    def chip_slab_small(p):
        c0, c1, c2, c3 = CONV_W // nchip, d // nchip, XBC // nchip, D_INNER // nchip
        return _pack([g_dw_w[:, p * c0:(p + 1) * c0], g_n1[0, p * c1:(p + 1) * c1],
                      g_conv_w[:, p * c2:(p + 1) * c2], g_conv_b[0, p * c2:(p + 1) * c2],
                      g_gn[0, p * c3:(p + 1) * c3]], 8)

    gsmall = jnp.stack([chip_slab_small(p) for p in range(nchip)])
    rep_vec_shapes = [ev_norm_w.shape, ev_dw_b.shape, ev_ln_w.shape, ev_ln_b.shape, final_norm_w.shape]
    grep = _pack_rep([g_n0, g_dw_b, g_ln_w, g_ln_b, g_fn], [g_dt_bias[0, 0:SSM_H], g_a_log, g_d])

    ro = D_INNER // nchip
    g_w_out0c = g_w_out0.reshape(nchip, ro, d)
    g_w_out1c = g_w_out1.reshape(nchip, ro, d)
    q_in0, q_in1, q_out0, q_out1, ssmall, srep = _pair_exchange(g_w_in0, g_w_in1, g_w_out0c, g_w_out1c, gsmall, grep)
    s_in0 = _half_add(g_w_in0, q_in0, axis=0, block=(128, IN_EVEN), name="half_add_in0")
    s_in1n = _half_add(g_w_in1, q_in1, axis=0, block=(128, IN_ODD_PAD), name="half_add_in1")
    s_in1 = jnp.stack([jnp.pad(s_in1n[:, p * n1:(p + 1) * n1], ((0, 0), (0, n1p - n1))) for p in range(nchip)])
    s_out0 = _half_add(g_w_out0c, q_out0, axis=1, block=(1, ro // 2, d), name="half_add_out0")
    s_out1 = _half_add(g_w_out1c, q_out1, axis=1, block=(1, ro // 2, d), name="half_add_out1")
    l_in0, l_in1, l_out0, l_out1 = _chip_exchange(s_in0, s_in1, s_out0, s_out1)
    r_in0 = _chip_sum(s_in0, l_in0, own_block=(128, IN_EVEN // nchip), own_map=lambda i, j, p: (i, p[0]),
                      block=(128, IN_EVEN // nchip), name="chip_sum_in0")
    r_in1 = _chip_sum(s_in1, l_in1, own_block=(1, 256, n1p), own_map=lambda i, j, p: (p[0], i, 0),
                      block=(256, n1p), name="chip_sum_in1")
    r_out0 = _chip_sum(s_out0, l_out0, own_block=(1, ro // 2, d), own_map=lambda i, j, p: (p[0], 0, 0),
                       block=(ro // 2, d), name="chip_sum_out0")
    r_out1 = _chip_sum(s_out1, l_out1, own_block=(1, ro // 2, d), own_map=lambda i, j, p: (p[0], 0, 0),
                       block=(ro // 2, d), name="chip_sum_out1")
    big_r = [r_in0, r_in1, r_out0, r_out1]
    big_q = _pair_share(*big_r)

    big_m = [m_ev_w_in[0], m_od_w_in[0], m_ev_w_out[0], m_od_w_out[0]]
    big_v = [v_ev_w_in[0], v_od_w_in[0], v_ev_w_out[0], v_od_w_out[0]]
    big_names = ["adamw_in0", "adamw_in1", "adamw_out0", "adamw_out1"]
    out_bigs = [_adamw_nat(gm, gs, w, m, v, name=nm, tr=128)
                for gm, gs, w, m, v, nm in zip(big_r, big_q, big_w, big_m, big_v, big_names)]

    def upd(slots, ws, ms, vs, packer, name, tr):
        return _adamw(slots, packer(ws), packer(ms), packer(vs), name=name, tr=tr)

    small_m = [m_ev_dw_w[0], m_od_norm_w[0], m_od_conv_w[0], m_od_conv_b[0], m_od_gnorm_w[0]]
    small_v = [v_ev_dw_w[0], v_od_norm_w[0], v_od_conv_w[0], v_od_conv_b[0], v_od_gnorm_w[0]]
    out_small = upd(ssmall, small_w, small_m, small_v, lambda a: _pack(a, 8), "adamw_small", ssmall.shape[1])

    def rep_pack(a):
        return _pack_rep(a[0:5], a[5:8])

    rep_w = [ev_norm_w, ev_dw_b, ev_ln_w, ev_ln_b, final_norm_w, od_dt_bias, od_a_log, od_d]
    rep_m = [m_ev_norm_w, m_ev_dw_b, m_ev_ln_w, m_ev_ln_b, m_final_norm_w, m_od_dt_bias, m_od_a_log, m_od_d]
    rep_v = [v_ev_norm_w, v_ev_dw_b, v_ev_ln_w, v_ev_ln_b, v_final_norm_w, v_od_dt_bias, v_od_a_log, v_od_d]
    out_rep = upd(srep, rep_w, rep_m, rep_v, rep_pack, "adamw_rep", srep.shape[1])

    results = []
    for kind in range(4):
        bw = [o[kind].reshape((1,) + o[kind].shape) for o in out_bigs]
        sw = _unpack(out_small[kind], small_shapes)
        vecs, scal = _unpack_rep(out_rep[kind], rep_vec_shapes, od_dt_bias.shape)
        results.append([
            vecs[0], bw[0], sw[0].reshape(ev_dw_w.shape), vecs[1], vecs[2], vecs[3], bw[2],
            sw[1].reshape(od_norm_w.shape), bw[1], sw[2].reshape(od_conv_w.shape), sw[3].reshape(od_conv_b.shape),
            scal[0], scal[1], scal[2], sw[4].reshape(od_gnorm_w.shape), bw[3], vecs[4]])
    loss = lax.psum(loss_part[0, 0], ("x", "y", "c"))
    return (loss, grad_x.reshape(x.shape), *results[0], *results[1], *results[2], *results[3])
```

```python
import jax
import jax.numpy as jnp
from jax import lax
from jax.experimental import pallas as pl
from jax.experimental.pallas import tpu as pltpu

F32 = jnp.float32
BF16 = jnp.bfloat16

D_MODEL = 1024
CONV_W = 1024
ATT_W = 1024
HEAD_DIM = 128
N_HEADS = 8
CONF_K = 31
IN_EVEN = 7168
D_INNER = 2048
SSM_P = 64
SSM_H = 32
SSM_G = 4
SSM_R = SSM_H // SSM_G
SSM_N = 128
SSM_K = 4
CHUNK = 128
XBC = D_INNER + 2 * SSM_G * SSM_N
IN_ODD = D_INNER + XBC + SSM_H
IN_ODD_PAD = 5376
EPS = 1e-6
QB = 128
NEG_CUT = -100.0

ADAM_LR = 0.001
ADAM_B1 = 0.9
ADAM_B2 = 0.999
ADAM_EPS = 1e-08
ADAM_WD = 0.01
ADAM_STEP = 10

LANE = 128
VMEM_LIMIT = 56 * 1024 * 1024
MESH = pl.DeviceIdType.MESH

NN = (((1,), (0,)), ((), ()))
NT = (((1,), (1,)), ((), ()))
TN = (((0,), (0,)), ((), ()))


def _pallas(body, **kw):
    return pl.pallas_call(body, **kw)


def _params(n_axes):
    return pltpu.CompilerParams(dimension_semantics=("arbitrary",) * n_axes, vmem_limit_bytes=VMEM_LIMIT)


def _dot(a, b, dims=NN):
    return lax.dot_general(a.astype(BF16), b.astype(BF16), dims, preferred_element_type=F32)


def _parts(x):
    h = x.astype(BF16)
    r = x - h.astype(F32)
    m = r.astype(BF16)
    l = (r - m.astype(F32)).astype(BF16)
    return (h, m, l)


def _dotx(x, e01, dims=NN):
    acc = None
    for p in _parts(x):
        t = lax.dot_general(p, e01, dims, preferred_element_type=F32)
        acc = t if acc is None else acc + t
    return acc


def _dotx2(x, e01, dims=NN):
    h = x.astype(BF16)
    l = (x - h.astype(F32)).astype(BF16)
    return (lax.dot_general(h, e01, dims, preferred_element_type=F32)
            + lax.dot_general(l, e01, dims, preferred_element_type=F32))


def _xdot(e01, x, dims=NN):
    acc = None
    for p in _parts(x):
        t = lax.dot_general(e01, p, dims, preferred_element_type=F32)
        acc = t if acc is None else acc + t
    return acc


def _f32(x):
    return x.astype(F32)


def _sigmoid(x):
    return 1.0 / (1.0 + jnp.exp(-x))


def _dsilu(x, s):
    return s * (1.0 + x * (1.0 - s))


def _matmul(a, b, *, mode, out_dtype, bm, bn, bk, name, residual=None, n_major=False):
    if mode == "nn":
        (m, k), n = a.shape, b.shape[1]
        a_blk, a_map = (bm, bk), lambda i, j, kk: (i, kk)
        b_blk, b_map = (bk, bn), lambda i, j, kk: (kk, j)
        dims = NN
    elif mode == "nt":
        (m, k), n = a.shape, b.shape[0]
        a_blk, a_map = (bm, bk), lambda i, j, kk: (i, kk)
        b_blk, b_map = (bn, bk), lambda i, j, kk: (j, kk)
        dims = NT
    else:
        (k, m), n = a.shape, b.shape[1]
        a_blk, a_map = (bk, bm), lambda i, j, kk: (kk, i)
        b_blk, b_map = (bk, bn), lambda i, j, kk: (kk, j)
        dims = TN
    bm, bn, bk = min(bm, m), min(bn, n), min(bk, k)
    if mode != "nn":
        a_blk = (bm, bk) if mode == "nt" else (bk, bm)
        b_blk = (bn, bk) if mode == "nt" else (bk, bn)
    else:
        a_blk, b_blk = (bm, bk), (bk, bn)
    assert m % bm == 0 and n % bn == 0 and k % bk == 0, (name, m, n, k)
    nk = k // bk
    has_res = residual is not None

    def order(f):
        return (lambda j, i, kk: f(i, j, kk)) if n_major else f

    def body(*refs):
        a_ref, b_ref = refs[0], refs[1]
        r_ref = refs[2] if has_res else None
        o_ref = refs[2 + has_res]

        def finish(r):
            if has_res:
                r = r + r_ref[...]
            o_ref[...] = r.astype(out_dtype)

        if nk == 1:
            finish(_dot(a_ref[...], b_ref[...], dims))
            return
        acc_ref = refs[3 + has_res]
        kk = pl.program_id(2)

        @pl.when(kk == 0)
        def _():
            acc_ref[...] = jnp.zeros_like(acc_ref)

        acc_ref[...] += _dot(a_ref[...], b_ref[...], dims)

        @pl.when(kk == nk - 1)
        def _():
            finish(acc_ref[...])

    in_specs = [pl.BlockSpec(a_blk, order(a_map)), pl.BlockSpec(b_blk, order(b_map))]
    args = [a, b]
    out_map = order(lambda i, j, kk: (i, j))
    if has_res:
        in_specs.append(pl.BlockSpec((bm, bn), out_map))
        args.append(residual)
    grid = (n // bn, m // bm, nk) if n_major else (m // bm, n // bn, nk)
    return _pallas(
        body, name=name, grid=grid, in_specs=in_specs,
        out_specs=pl.BlockSpec((bm, bn), out_map),
        out_shape=jax.ShapeDtypeStruct((m, n), out_dtype),
        scratch_shapes=[pltpu.VMEM((bm, bn), F32)] if nk > 1 else [], compiler_params=_params(3),
    )(*args)


def _rms_fwd(x, w, *, name, tm=512):
    t, d = x.shape

    def body(x_ref, w_ref, o_ref):
        xv = x_ref[...]
        r = lax.rsqrt(jnp.mean(xv * xv, axis=1, keepdims=True) + EPS)
        o_ref[...] = (xv * r * w_ref[...]).astype(BF16)

    return _pallas(
        body, name=name, grid=(t // tm,),
        in_specs=[pl.BlockSpec((tm, d), lambda i: (i, 0)), pl.BlockSpec((1, d), lambda i: (0, 0))],
        out_specs=pl.BlockSpec((tm, d), lambda i: (i, 0)),
        out_shape=jax.ShapeDtypeStruct((t, d), BF16), compiler_params=_params(1),
    )(x, w)


def _rms_bwd(dn, x, w, dres, *, name, tm=512):
    t, d = x.shape

    def body(dn_ref, x_ref, w_ref, dr_ref, dx_ref, dw_ref):
        i = pl.program_id(0)
        xv = x_ref[...]
        r = lax.rsqrt(jnp.mean(xv * xv, axis=1, keepdims=True) + EPS)
        xh = xv * r
        dy = dn_ref[...].astype(F32)
        g = dy * w_ref[...]
        dx_ref[...] = dr_ref[...] + r * (g - xh * jnp.mean(g * xh, axis=1, keepdims=True))

        @pl.when(i == 0)
        def _():
            dw_ref[...] = jnp.zeros_like(dw_ref)

        dw_ref[...] += jnp.sum(dy * xh, axis=0, keepdims=True)

    row = pl.BlockSpec((tm, d), lambda i: (i, 0))
    vec = pl.BlockSpec((1, d), lambda i: (0, 0))
    return _pallas(
        body, name=name, grid=(t // tm,), in_specs=[row, row, vec, row], out_specs=[row, vec],
        out_shape=[jax.ShapeDtypeStruct((t, d), F32), jax.ShapeDtypeStruct((1, d), F32)],
        compiler_params=_params(1),
    )(dn, x, w, dres)


def _final_loss(h, w, target, *, tm=512):
    t, d = h.shape

    def body(h_ref, w_ref, t_ref, dh_ref, dw_ref, loss_ref):
        i = pl.program_id(0)
        xv = h_ref[...]
        r = lax.rsqrt(jnp.mean(xv * xv, axis=1, keepdims=True) + EPS)
        xh = xv * r
        wv = w_ref[...]
        err = xh * wv - t_ref[...]
        dy = err * (1.0 / d)
        g = dy * wv
        dh_ref[...] = r * (g - xh * jnp.mean(g * xh, axis=1, keepdims=True))

        @pl.when(i == 0)
        def _():
            dw_ref[...] = jnp.zeros_like(dw_ref)
            loss_ref[...] = jnp.zeros_like(loss_ref)

        dw_ref[...] += jnp.sum(dy * xh, axis=0, keepdims=True)
        part = jnp.sum(jnp.sum(err * err, axis=1, keepdims=True), axis=0, keepdims=True)
        loss_ref[...] += part * (0.5 / d)

    row = pl.BlockSpec((tm, d), lambda i: (i, 0))
    vec = pl.BlockSpec((1, d), lambda i: (0, 0))
    return _pallas(
        body, name="final_loss", grid=(t // tm,), in_specs=[row, vec, row],
        out_specs=[row, vec, pl.BlockSpec((1, LANE), lambda i: (0, 0))],
        out_shape=[jax.ShapeDtypeStruct((t, d), F32), jax.ShapeDtypeStruct((1, d), F32),
                   jax.ShapeDtypeStruct((1, LANE), F32)],
        compiler_params=_params(1),
    )(h, w, target)


HALO = 32


SUB = 8
RC = 16


def _make_shifts(sh_ref, rows, shifts=tuple(range(1, SUB))):
    for s in shifts:
        sh_ref[s, 0:rows, :] = sh_ref[0, s:s + rows, :]


def _shifted(sh_ref, r0, j, rows):
    return sh_ref[j % SUB, pl.ds(r0 + (j - j % SUB), rows), :]


def _taps(w8_ref, sh_ref, r0, first, step, init):
    accs = [init] * (RC // SUB)
    for k in range(CONF_K):
        wk = w8_ref[k * SUB:(k + 1) * SUB, :]
        x = _shifted(sh_ref, r0, first + step * k, RC)
        accs = [a + wk * x[q * SUB:(q + 1) * SUB] for q, a in enumerate(accs)]
    return jnp.concatenate(accs, axis=0)


def _conf_fwd(proj, dw_w, dw_b, ln_w, ln_b, seq, *, tm=256):
    t = proj.shape[0]
    c = CONV_W
    tps = seq // tm
    hb = tm // HALO

    def body(a_ref, b_ref, g_ref, ha_ref, hb_ref, w_ref, wb_ref, lw_ref, lb_ref, y_ref, u2_ref, sh_ref):
        i = pl.program_id(0)
        keep = jnp.where(i % tps == 0, 0.0, 1.0)
        sh_ref[0, 0:HALO, :] = _f32(ha_ref[...]) * _sigmoid(_f32(hb_ref[...])) * keep
        sh_ref[0, HALO:HALO + tm, :] = _f32(a_ref[...]) * _sigmoid(_f32(b_ref[...]))
        _make_shifts(sh_ref, tm + HALO - SUB)

        def chunk(ci, carry):
            r0 = pl.multiple_of(ci * RC, RC)
            acc = _taps(w_ref, sh_ref, r0, HALO - CONF_K + 1, 1, jnp.broadcast_to(wb_ref[...], (SUB, c)))
            u2_ref[pl.ds(r0, RC), :] = acc
            mu = jnp.mean(acc, axis=1, keepdims=True)
            xc = acc - mu
            rs = lax.rsqrt(jnp.mean(xc * xc, axis=1, keepdims=True) + EPS)
            u3 = xc * rs * lw_ref[...] + lb_ref[...]
            gv = _f32(g_ref[pl.ds(r0, RC), :])
            y_ref[pl.ds(r0, RC), :] = (u3 * _sigmoid(u3) * gv * _sigmoid(gv)).astype(BF16)
            return carry

        lax.fori_loop(0, tm // RC, chunk, 0, unroll=2)

    def col(j):
        return pl.BlockSpec((tm, c), lambda i: (i, j))

    def prev(j):
        return pl.BlockSpec((HALO, c), lambda i: (jnp.maximum(i * hb - 1, 0), j))

    vec = pl.BlockSpec((1, c), lambda i: (0, 0))
    return _pallas(
        body, name="conf_fwd", grid=(t // tm,),
        in_specs=[col(0), col(1), col(2), prev(0), prev(1),
                  pl.BlockSpec((HALO * SUB, c), lambda i: (0, 0)), vec, vec, vec],
        out_specs=[pl.BlockSpec((tm, c), lambda i: (i, 0)), pl.BlockSpec((tm, c), lambda i: (i, 0))],
        out_shape=[jax.ShapeDtypeStruct((t, c), BF16), jax.ShapeDtypeStruct((t, c), F32)],
        scratch_shapes=[pltpu.VMEM((SUB, tm + HALO, c), F32)], compiler_params=_params(1),
    )(proj, proj, proj, proj, proj, dw_w, dw_b, ln_w, ln_b)


def _conf_bwd(proj, u2, dycat, dw_w, ln_w, ln_b, seq, comm_kinds, comm_srcs, *, tm=256):
    t = proj.shape[0]
    c = CONV_W
    tps = seq // tm
    hb = tm // HALO
    nhb = t // HALO
    nw = len(comm_srcs)
    nsteps = t // tm

    def fold(v):
        out = v[0:SUB]
        for q in range(1, RC // SUB):
            out = out + v[q * SUB:(q + 1) * SUB]
        return out

    def body(*refs):
        (a_ref, b_ref, g_ref, pa_ref, pb_ref, ng_ref, u2_ref, nu2_ref, dy_ref, ndy_ref,
         w_ref, lw_ref, lb_ref) = refs[:13]
        dp_ref, dww_ref, dwb_ref, dlw_ref, dlb_ref = refs[13 + nw:18 + nw]
        su_ref, sd_ref = refs[18 + 2 * nw:20 + 2 * nw]
        comm_start, comm_finish = _chip_plan(comm_kinds, refs[13:13 + nw], refs[18 + nw:18 + 2 * nw],
                                             refs[20 + 2 * nw:])
        i = pl.program_id(0)
        first = i % tps == 0
        last = i % tps == tps - 1

        @pl.when(i == 0)
        def _():
            comm_start()
            dww_ref[...] = jnp.zeros_like(dww_ref)
            dwb_ref[...] = jnp.zeros_like(dwb_ref)
            dlw_ref[...] = jnp.zeros_like(dlw_ref)
            dlb_ref[...] = jnp.zeros_like(dlb_ref)

        su_ref[0, 0:HALO, :] = _f32(pa_ref[...]) * _sigmoid(_f32(pb_ref[...])) * jnp.where(first, 0.0, 1.0)
        su_ref[0, HALO:HALO + tm, :] = _f32(a_ref[...]) * _sigmoid(_f32(b_ref[...]))
        _make_shifts(su_ref, tm + HALO - SUB)

        def ln_back(u2c, gv, dy):
            mu = jnp.mean(u2c, axis=1, keepdims=True)
            xc = u2c - mu
            rs = lax.rsqrt(jnp.mean(xc * xc, axis=1, keepdims=True) + EPS)
            xh = xc * rs
            lw = lw_ref[...]
            u3 = xh * lw + lb_ref[...]
            s3 = _sigmoid(u3)
            sg = _sigmoid(gv)
            dgc = dy * (u3 * s3) * _dsilu(gv, sg)
            du3 = dy * gv * sg * _dsilu(u3, s3)
            dxh = du3 * lw
            du2 = rs * (dxh - jnp.mean(dxh, axis=1, keepdims=True)
                        - xh * jnp.mean(dxh * xh, axis=1, keepdims=True))
            return du2, dgc, du3, xh

        def tile_chunk(ci, carry):
            r0 = pl.multiple_of(ci * RC, RC)
            rows = pl.ds(r0, RC)
            du2, dgc, du3, xh = ln_back(u2_ref[rows, :], _f32(g_ref[rows, :]), _f32(dy_ref[rows, :]))
            sd_ref[0, rows, :] = du2
            dp_ref[rows, 2 * c:3 * c] = dgc.astype(BF16)
            dwb_ref[...] += fold(du2)
            dlw_ref[...] += fold(du3 * xh)
            dlb_ref[...] += fold(du3)
            return carry

        lax.fori_loop(0, tm // RC, tile_chunk, 0, unroll=2)
        live = jnp.where(last, 0.0, 1.0)
        for ci in range(HALO // RC):
            rows = slice(ci * RC, (ci + 1) * RC)
            du2, _, _, _ = ln_back(nu2_ref[rows, :], _f32(ng_ref[rows, :]), _f32(ndy_ref[rows, :]))
            sd_ref[0, tm + ci * RC:tm + (ci + 1) * RC, :] = du2 * live
        _make_shifts(sd_ref, tm + HALO - SUB)

        def tap_chunk(ci, carry):
            r0 = pl.multiple_of(ci * RC, RC)
            rows = pl.ds(r0, RC)
            du1 = _taps(w_ref, sd_ref, r0, CONF_K - 1, -1, jnp.zeros((SUB, c), F32))
            sb = _sigmoid(_f32(b_ref[rows, :]))
            dp_ref[rows, 0:c] = (du1 * sb).astype(BF16)
            dp_ref[rows, c:2 * c] = (du1 * _f32(a_ref[rows, :]) * sb * (1.0 - sb)).astype(BF16)
            du2 = sd_ref[0, rows, :]
            for k in range(CONF_K):
                dww_ref[k * SUB:(k + 1) * SUB, :] += fold(du2 * _shifted(su_ref, r0, HALO - CONF_K + 1 + k, RC))
            return carry

        lax.fori_loop(0, tm // RC, tap_chunk, 0)

        @pl.when(i == nsteps - 1)
        def _():
            comm_finish()

    def col(j):
        return pl.BlockSpec((tm, c), lambda i: (i, j))

    def prev(j):
        return pl.BlockSpec((HALO, c), lambda i: (jnp.maximum(i * hb - 1, 0), j))

    def nxt(j):
        return pl.BlockSpec((HALO, c), lambda i: (jnp.minimum((i + 1) * hb, nhb - 1), j))

    vec = pl.BlockSpec((1, c), lambda i: (0, 0))
    acc = pl.BlockSpec((SUB, c), lambda i: (0, 0))
    any_spec = pl.BlockSpec(memory_space=pl.ANY)
    return _pallas(
        body, name="conf_bwd", grid=(nsteps,),
        in_specs=[col(0), col(1), col(2), prev(0), prev(1), nxt(2), col(0), nxt(0), col(0), nxt(0),
                  pl.BlockSpec((HALO * SUB, c), lambda i: (0, 0)), vec, vec] + [any_spec] * nw,
        out_specs=[pl.BlockSpec((tm, 3 * c), lambda i: (i, 0)),
                   pl.BlockSpec((HALO * SUB, c), lambda i: (0, 0)), acc, acc, acc] + [any_spec] * nw,
        out_shape=[jax.ShapeDtypeStruct((t, 3 * c), BF16), jax.ShapeDtypeStruct((HALO * SUB, c), F32),
                   jax.ShapeDtypeStruct((SUB, c), F32), jax.ShapeDtypeStruct((SUB, c), F32),
                   jax.ShapeDtypeStruct((SUB, c), F32)]
        + [_landing_shape(kd, s) for kd, s in zip(comm_kinds, comm_srcs)],
        scratch_shapes=[pltpu.VMEM((SUB, tm + HALO, c), F32), pltpu.VMEM((SUB, tm + HALO, c), F32)] + _chip_sems(nw),
        compiler_params=_params(1),
    )(proj, proj, proj, proj, proj, proj, u2, u2, dycat, dycat, dw_w, ln_w, ln_b, *comm_srcs)


Q_COL = 3 * CONV_W // HEAD_DIM
K_COL = Q_COL + N_HEADS
V_COL = K_COL + N_HEADS
GA_COL = V_COL + N_HEADS


SBA_TQ = 256
SBA_WK = 4 * QB


def _sb_window(qs, kw, ws, limit, t0, carry):
    tq, wk = qs.shape[0], kw.shape[0]
    z = _dot(qs, kw, NT)
    sg = ws + lax.broadcasted_iota(jnp.int32, (tq, wk), 1)
    tg = t0 + lax.broadcasted_iota(jnp.int32, (tq, wk), 0)
    mask = sg < jnp.minimum(tg, limit)
    sp = jnp.log(1.0 + jnp.exp(-jnp.abs(z)))
    ls = jnp.minimum(z, 0.0) - sp
    lk = jnp.where(mask, ls - z, 0.0)
    jj = lax.broadcasted_iota(jnp.int32, (QB, QB), 0)
    ss = lax.broadcasted_iota(jnp.int32, (QB, QB), 1)
    ustrict = jnp.where(jj > ss, 1.0, 0.0).astype(BF16)
    laters = [None] * (wk // QB)
    for ch in reversed(range(wk // QB)):
        lkc = lk[:, ch * QB:(ch + 1) * QB]
        laters[ch] = carry + _dotx2(lkc, ustrict)
        carry = carry + jnp.sum(lkc, axis=1, keepdims=True)
    w = jnp.where(mask, jnp.exp(ls + jnp.concatenate(laters, axis=1)), 0.0)
    return mask, ls, w, carry


def _sba_fwd(proj, nb, seq, *, tq=SBA_TQ, wk=SBA_WK):
    t = proj.shape[0]
    wk = min(wk, seq)
    nq = seq // tq
    scale = HEAD_DIM ** -0.5

    def body(q_ref, k_ref, v_ref, g_ref, o_ref, y_ref):
        i = pl.program_id(2)
        t0 = i * tq
        qs = (_f32(q_ref[...]) * scale).astype(BF16)

        def window(ws, limit, carry, acc):
            ws = pl.multiple_of(ws, QB)
            _, _, w, carry = _sb_window(qs, k_ref[pl.ds(ws, wk), :], ws, limit, t0, carry)
            return carry, acc + _dot(w, v_ref[pl.ds(ws, wk), :])

        ws0 = jnp.maximum(t0 + tq - wk, 0)
        carry, acc = window(ws0, seq, jnp.zeros((tq, 1), F32), jnp.zeros((tq, HEAD_DIM), F32))

        def cond(st):
            return jnp.logical_and(st[0] > 0, jnp.max(st[1]) > NEG_CUT)

        def step(st):
            c2, a2 = window(jnp.maximum(st[0] - wk, 0), st[0], st[1], st[2])
            return jnp.maximum(st[0] - wk, 0), c2, a2

        _, _, acc = lax.while_loop(cond, step, (ws0, carry, acc))
        o_ref[...] = acc
        gv = _f32(g_ref[...])
        y_ref[...] = (acc * gv * _sigmoid(gv)).astype(BF16)

    def tile(c0):
        return pl.BlockSpec((tq, HEAD_DIM), lambda b, h, i: (b * nq + i, c0 + h))

    def whole(c0):
        return pl.BlockSpec((seq, HEAD_DIM), lambda b, h, i: (b, c0 + h))

    return _pallas(
        body, name="sba_fwd", grid=(nb, N_HEADS, nq),
        in_specs=[tile(Q_COL), whole(K_COL), whole(V_COL), tile(GA_COL)],
        out_specs=[tile(0), tile(0)],
        out_shape=[jax.ShapeDtypeStruct((t, ATT_W), F32), jax.ShapeDtypeStruct((t, ATT_W), BF16)],
        compiler_params=_params(3),
    )(proj, proj, proj, proj)


def _sba_bwd(proj, o, dycat, nb, seq, *, tq=SBA_TQ, wk=SBA_WK):
    t = proj.shape[0]
    wk = min(wk, seq)
    nq = seq // tq
    nwin = -(-seq // wk) + 1
    nch = wk // QB
    scale = HEAD_DIM ** -0.5

    def body(q_ref, k_ref, v_ref, g_ref, o_ref, dy_ref, dq_ref, dk_ref, dv_ref, dg_ref, e_ref, sp_ref):
        i = pl.program_id(2)
        t0 = i * tq

        @pl.when(i == 0)
        def _():
            dk_ref[...] = jnp.zeros_like(dk_ref)
            dv_ref[...] = jnp.zeros_like(dv_ref)

        qs = (_f32(q_ref[...]) * scale).astype(BF16)
        gv = _f32(g_ref[...])
        sg = _sigmoid(gv)
        dy = _f32(dy_ref[...])
        do = (dy * gv * sg).astype(BF16)
        dg_ref[...] = (dy * o_ref[...] * _dsilu(gv, sg)).astype(BF16)

        def start_of(n):
            return pl.multiple_of(jnp.maximum(t0 + tq - (n + 1) * wk, 0), QB)

        def limit_of(n):
            return jnp.where(n == 0, seq, jnp.maximum(t0 + tq - n * wk, 0))

        def near(n, carry):
            ws = start_of(n)
            _, ls, w, carry = _sb_window(qs, k_ref[pl.ds(ws, wk), :], ws, limit_of(n), t0, carry)
            e_ref[n] = w * _dot(do, v_ref[pl.ds(ws, wk), :], NT)
            sp_ref[n] = jnp.exp(ls)
            dv_ref[pl.ds(ws, wk), :] += _dot(w, do, TN)
            return carry

        carry = near(0, jnp.zeros((tq, 1), F32))

        def cond(st):
            return jnp.logical_and(start_of(st[0] - 1) > 0, jnp.max(st[1]) > NEG_CUT)

        def step(st):
            return st[0] + 1, near(st[0], st[1])

        nvis, _ = lax.while_loop(cond, step, (1, carry))

        jj = lax.broadcasted_iota(jnp.int32, (QB, QB), 0)
        ss = lax.broadcasted_iota(jnp.int32, (QB, QB), 1)
        lstrict = jnp.where(jj < ss, 1.0, 0.0).astype(BF16)

        def far(r, st):
            pre, dq = st
            n = nvis - 1 - r
            ws = start_of(n)
            e = e_ref[n]
            spn = sp_ref[n]
            gs = []
            for ch in range(nch):
                ec = e[:, ch * QB:(ch + 1) * QB]
                gs.append(pre + _dotx2(ec, lstrict))
                pre = pre + jnp.sum(ec, axis=1, keepdims=True)
            sgl = ws + lax.broadcasted_iota(jnp.int32, (tq, wk), 1)
            tgl = t0 + lax.broadcasted_iota(jnp.int32, (tq, wk), 0)
            mask = sgl < jnp.minimum(tgl, limit_of(n))
            dz = jnp.where(mask, e * (1.0 - spn) - jnp.concatenate(gs, axis=1) * spn, 0.0).astype(BF16)
            dk_ref[pl.ds(ws, wk), :] += _dot(dz, qs, TN)
            return pre, dq + _dot(dz, k_ref[pl.ds(ws, wk), :])

        _, dq = lax.fori_loop(0, nvis, far, (jnp.zeros((tq, 1), F32), jnp.zeros((tq, HEAD_DIM), F32)))
        dq_ref[...] = (dq * scale).astype(BF16)

    def tile(c0):
        return pl.BlockSpec((tq, HEAD_DIM), lambda b, h, i: (b * nq + i, c0 + h))

    def whole(c0):
        return pl.BlockSpec((seq, HEAD_DIM), lambda b, h, i: (b, c0 + h))

    return _pallas(
        body, name="sba_bwd", grid=(nb, N_HEADS, nq),
        in_specs=[tile(Q_COL), whole(K_COL), whole(V_COL), tile(GA_COL), tile(0),
                  tile(CONV_W // HEAD_DIM)],
        out_specs=[tile(0), whole(0), whole(0), tile(0)],
        out_shape=[jax.ShapeDtypeStruct((t, ATT_W), BF16), jax.ShapeDtypeStruct((t, ATT_W), F32),
                   jax.ShapeDtypeStruct((t, ATT_W), F32), jax.ShapeDtypeStruct((t, ATT_W), BF16)],
        scratch_shapes=[pltpu.VMEM((nwin, tq, wk), F32), pltpu.VMEM((nwin, tq, wk), F32)],
        compiler_params=_params(3),
    )(proj, proj, proj, proj, o, dycat)


CT = 512
PH = 8
XRC = 32
X_SHIFTS = tuple(s for s in range(PH - SSM_K + 1, PH))
D_SHIFTS = tuple(range(1, SSM_K))
Z_BLK = 0
XBC_BLK = D_INNER // CT
DT_BLK = (D_INNER + XBC) // LANE


def _softplus(x):
    return jnp.maximum(x, 0.0) + jnp.log(1.0 + jnp.exp(-jnp.abs(x)))


def _dt_fwd(proj, dt_bias, *, tm=512):
    t = proj.shape[0]

    def body(p_ref, b_ref, o_ref):
        o_ref[...] = _softplus(p_ref[...] + b_ref[...])

    return _pallas(
        body, name="dt_fwd", grid=(t // tm,),
        in_specs=[pl.BlockSpec((tm, LANE), lambda i: (i, DT_BLK)), pl.BlockSpec((1, LANE), lambda i: (0, 0))],
        out_specs=pl.BlockSpec((tm, LANE), lambda i: (i, 0)),
        out_shape=jax.ShapeDtypeStruct((t, LANE), F32), compiler_params=_params(1),
    )(proj, dt_bias)


def _dt_bwd(proj, dt_bias, ddt, *, tm=512):
    t = proj.shape[0]

    def body(p_ref, b_ref, d_ref, o_ref, db_ref):
        i = pl.program_id(0)
        lanes = lax.broadcasted_iota(jnp.int32, (tm, LANE), 1)
        dr = jnp.where(lanes < SSM_H, d_ref[...] * _sigmoid(p_ref[...] + b_ref[...]), 0.0)
        o_ref[...] = dr

        @pl.when(i == 0)
        def _():
            db_ref[...] = jnp.zeros_like(db_ref)

        db_ref[...] += jnp.sum(dr, axis=0, keepdims=True)

    vec = pl.BlockSpec((1, LANE), lambda i: (0, 0))
    row = pl.BlockSpec((tm, LANE), lambda i: (i, 0))
    return _pallas(
        body, name="dt_bwd", grid=(t // tm,),
        in_specs=[pl.BlockSpec((tm, LANE), lambda i: (i, DT_BLK)), vec, row],
        out_specs=[row, vec],
        out_shape=[jax.ShapeDtypeStruct((t, LANE), F32), jax.ShapeDtypeStruct((1, LANE), F32)],
        compiler_params=_params(1),
    )(proj, dt_bias, ddt)


def _xconv_fwd(proj, conv_w, conv_b, seq, *, tm=512):
    t = proj.shape[0]
    tps = seq // tm
    hb = tm // PH

    def body(x_ref, h_ref, w_ref, b_ref, o_ref, sh_ref):
        i = pl.program_id(1)
        sh_ref[0, 0:PH, :] = h_ref[...] * jnp.where(i % tps == 0, 0.0, 1.0)
        sh_ref[0, PH:PH + tm, :] = x_ref[...]
        _make_shifts(sh_ref, tm, X_SHIFTS)

        def chunk(ci, carry):
            r0 = pl.multiple_of(ci * XRC, XRC)
            acc = jnp.zeros((XRC, CT), F32) + b_ref[...]
            for k in range(SSM_K):
                acc = acc + w_ref[k:k + 1, :] * _shifted(sh_ref, r0, PH - SSM_K + 1 + k, XRC)
            o_ref[pl.ds(r0, XRC), :] = acc * _sigmoid(acc)
            return carry

        lax.fori_loop(0, tm // XRC, chunk, 0)

    return _pallas(
        body, name="xconv_fwd", grid=(XBC // CT, t // tm),
        in_specs=[pl.BlockSpec((tm, CT), lambda j, i: (i, XBC_BLK + j)),
                  pl.BlockSpec((PH, CT), lambda j, i: (jnp.maximum(i * hb - 1, 0), XBC_BLK + j)),
                  pl.BlockSpec((PH, CT), lambda j, i: (0, j)),
                  pl.BlockSpec((1, CT), lambda j, i: (0, j))],
        out_specs=pl.BlockSpec((tm, CT), lambda j, i: (i, j)),
        out_shape=jax.ShapeDtypeStruct((t, XBC), F32),
        scratch_shapes=[pltpu.VMEM((SUB, tm + PH, CT), F32)], compiler_params=_params(2),
    )(proj, proj, conv_w, conv_b)


def _xconv_bwd(proj, dxc, conv_w, conv_b, seq, *, tm=512):
    t = proj.shape[0]
    tps = seq // tm
    hb = tm // PH
    nhb = t // PH
    te = tm + PH

    def fold(v):
        out = v[0:SUB]
        for q in range(1, v.shape[0] // SUB):
            out = out + v[q * SUB:(q + 1) * SUB]
        return out

    def body(x_ref, p_ref, n_ref, d_ref, nd_ref, w_ref, b_ref, dx_ref, dw_ref, db_ref, sx_ref, sd_ref):
        i = pl.program_id(1)
        first = i % tps == 0
        last = i % tps == tps - 1

        @pl.when(i == 0)
        def _():
            dw_ref[...] = jnp.zeros_like(dw_ref)
            db_ref[...] = jnp.zeros_like(db_ref)

        sx_ref[0, 0:PH, :] = p_ref[...] * jnp.where(first, 0.0, 1.0)
        sx_ref[0, PH:PH + tm, :] = x_ref[...]
        sx_ref[0, PH + tm:PH + te, :] = n_ref[...]
        _make_shifts(sx_ref, te, X_SHIFTS)

        def dv_of(r0, rows, dy):
            acc = jnp.zeros((rows, CT), F32) + b_ref[...]
            for k in range(SSM_K):
                acc = acc + w_ref[k:k + 1, :] * _shifted(sx_ref, r0, PH - SSM_K + 1 + k, rows)
            return dy * _dsilu(acc, _sigmoid(acc))

        def dv_chunk(ci, carry):
            r0 = pl.multiple_of(ci * XRC, XRC)
            dv = dv_of(r0, XRC, d_ref[pl.ds(r0, XRC), :])
            sd_ref[0, pl.ds(r0, XRC), :] = dv
            db_ref[...] += fold(dv)
            return carry

        lax.fori_loop(0, tm // XRC, dv_chunk, 0)
        sd_ref[0, tm:te, :] = dv_of(tm, PH, nd_ref[...]) * jnp.where(last, 0.0, 1.0)
        _make_shifts(sd_ref, tm, D_SHIFTS)

        def tap_chunk(ci, carry):
            r0 = pl.multiple_of(ci * XRC, XRC)
            dx = jnp.zeros((XRC, CT), F32)
            for k in range(SSM_K):
                dx = dx + w_ref[k:k + 1, :] * _shifted(sd_ref, r0, SSM_K - 1 - k, XRC)
            dx_ref[pl.ds(r0, XRC), :] = dx.astype(BF16)
            dv = sd_ref[0, pl.ds(r0, XRC), :]
            for k in range(SSM_K):
                dw_ref[k * SUB:(k + 1) * SUB, :] += fold(dv * _shifted(sx_ref, r0, PH - SSM_K + 1 + k, XRC))
            return carry

        lax.fori_loop(0, tm // XRC, tap_chunk, 0)

    return _pallas(
        body, name="xconv_bwd", grid=(XBC // CT, t // tm),
        in_specs=[pl.BlockSpec((tm, CT), lambda j, i: (i, XBC_BLK + j)),
                  pl.BlockSpec((PH, CT), lambda j, i: (jnp.maximum(i * hb - 1, 0), XBC_BLK + j)),
                  pl.BlockSpec((PH, CT), lambda j, i: (jnp.minimum((i + 1) * hb, nhb - 1), XBC_BLK + j)),
                  pl.BlockSpec((tm, CT), lambda j, i: (i, j)),
                  pl.BlockSpec((PH, CT), lambda j, i: (jnp.minimum((i + 1) * hb, nhb - 1), j)),
                  pl.BlockSpec((PH, CT), lambda j, i: (0, j)),
                  pl.BlockSpec((1, CT), lambda j, i: (0, j))],
        out_specs=[pl.BlockSpec((tm, CT), lambda j, i: (i, j)),
                   pl.BlockSpec((PH * SUB, CT), lambda j, i: (0, j)),
                   pl.BlockSpec((SUB, CT), lambda j, i: (0, j))],
        out_shape=[jax.ShapeDtypeStruct((t, XBC), BF16), jax.ShapeDtypeStruct((PH * SUB, XBC), F32),
                   jax.ShapeDtypeStruct((SUB, XBC), F32)],
        scratch_shapes=[pltpu.VMEM((SUB, tm + 2 * PH, CT), F32), pltpu.VMEM((SUB, te, CT), F32)],
        compiler_params=_params(2),
    )(proj, proj, proj, dxc, dxc, conv_w, conv_b)


def _ssd_common(xbc, dt, alog, ex):
    L = CHUNK
    a = -jnp.exp(alog)
    la = dt * a
    li = lax.broadcasted_iota(jnp.int32, (L, L), 0)
    si = lax.broadcasted_iota(jnp.int32, (L, L), 1)
    lower = si <= li
    tri = jnp.where(lower, 1.0, 0.0).astype(BF16)
    cs = _xdot(tri, la)
    cst = _dotx(la, tri, (((0,), (1,)), ((), ())))
    csl = cs[L - 1:L, :]
    ecs_x = _dotx2(jnp.exp(cs)[:, 0:SSM_H], ex)
    tail_x = _dotx2(jnp.exp(csl - cs)[:, 0:SSM_H], ex)
    dt_x = _dotx2(dt[:, 0:SSM_H], ex)
    return a, la, lower, tri, cs, cst, ecs_x, tail_x, dt_x


def _ssd_fwd(xbc_c, dt, a_log, ex, nb, seq):
    t = xbc_c.shape[0]
    L = CHUNK
    nc = seq // L
    GW = SSM_R * SSM_P

    def body(x_ref, dt_ref, al_ref, ex_ref, y_ref, st_ref, state):
        c = pl.program_id(1)

        @pl.when(c == 0)
        def _():
            state[...] = jnp.zeros_like(state)

        st_ref[0] = state[...]
        xbc = x_ref[...]
        _, _, lower, _, cs, cst, ecs_x, tail_x, dt_x = _ssd_common(xbc, dt_ref[...], al_ref[...], ex_ref[...])
        xd = xbc[:, 0:D_INNER] * dt_x
        xdb = xd.astype(BF16)
        xt = (xd * tail_x).astype(BF16)
        el_x = ecs_x[L - 1:L, :]
        for g in range(SSM_G):
            bg = xbc[:, D_INNER + g * SSM_N:D_INNER + (g + 1) * SSM_N].astype(BF16)
            cg = xbc[:, D_INNER + (SSM_G + g) * SSM_N:D_INNER + (SSM_G + g + 1) * SSM_N].astype(BF16)
            cb = _dot(cg, bg, NT)
            sg = state[:, g * GW:(g + 1) * GW]
            ys = _dot(cg, sg) * ecs_x[:, g * GW:(g + 1) * GW]
            for r in range(SSM_R):
                h = g * SSM_R + r
                seg = cs[:, h:h + 1] - cst[h:h + 1, :]
                dec = jnp.exp(jnp.where(lower, seg, -1e30))
                yh = _dot(cb * dec, xdb[:, h * SSM_P:(h + 1) * SSM_P])
                y_ref[:, h * SSM_P:(h + 1) * SSM_P] = yh + ys[:, r * SSM_P:(r + 1) * SSM_P]
            state[:, g * GW:(g + 1) * GW] = sg * el_x[:, g * GW:(g + 1) * GW] + _dot(bg, xt[:, g * GW:(g + 1) * GW], TN)

    return _pallas(
        body, name="ssd_fwd", grid=(nb, nc),
        in_specs=[pl.BlockSpec((L, XBC), lambda b, c: (b * nc + c, 0)),
                  pl.BlockSpec((L, LANE), lambda b, c: (b * nc + c, 0)),
                  pl.BlockSpec((1, LANE), lambda b, c: (0, 0)),
                  pl.BlockSpec((SSM_H, D_INNER), lambda b, c: (0, 0))],
        out_specs=[pl.BlockSpec((L, D_INNER), lambda b, c: (b * nc + c, 0)),
                   pl.BlockSpec((1, SSM_N, D_INNER), lambda b, c: (b * nc + c, 0, 0))],
        out_shape=[jax.ShapeDtypeStruct((t, D_INNER), F32),
                   jax.ShapeDtypeStruct((nb * nc, SSM_N, D_INNER), F32)],
        scratch_shapes=[pltpu.VMEM((SSM_N, D_INNER), F32)], compiler_params=_params(2),
    )(xbc_c, dt, a_log, ex)


def _ssd_bwd(xbc_c, dt, a_log, ex, ext, states, dy, d_x, nb, seq):
    t = xbc_c.shape[0]
    L = CHUNK
    nc = seq // L
    GW = SSM_R * SSM_P

    def body(x_ref, dt_ref, al_ref, ex_ref, ext_ref, st_ref, dy_ref, sk_ref, dx_ref, ddt_ref, da_ref,
             dstate, dxd, yd, lastv):
        b = pl.program_id(0)
        c = pl.program_id(1)

        @pl.when(c == 0)
        def _():
            dstate[...] = jnp.zeros_like(dstate)

        @pl.when(jnp.logical_and(b == 0, c == 0))
        def _():
            da_ref[...] = jnp.zeros_like(da_ref)

        xbc = x_ref[...]
        dtv = dt_ref[...]
        ex_t = ext_ref[...]
        a, la, lower, tri, cs, cst, ecs_x, tail_x, dt_x = _ssd_common(xbc, dtv, al_ref[...], ex_ref[...])
        xs = xbc[:, 0:D_INNER]
        xd = xs * dt_x
        xdb = xd.astype(BF16)
        dyv = dy_ref[...]
        dyb = dyv.astype(BF16)
        dys = dyv * ecs_x
        xt = xd * tail_x
        el_x = ecs_x[L - 1:L, :]
        lane = lax.broadcasted_iota(jnp.int32, (L, LANE), 1)
        sub = lax.broadcasted_iota(jnp.int32, (LANE, L), 0)
        row_part = jnp.zeros((L, LANE), F32)
        col_part = jnp.zeros((LANE, L), F32)
        for g in range(SSM_G):
            gs = slice(g * GW, (g + 1) * GW)
            bcol = slice(D_INNER + g * SSM_N, D_INNER + (g + 1) * SSM_N)
            ccol = slice(D_INNER + (SSM_G + g) * SSM_N, D_INNER + (SSM_G + g + 1) * SSM_N)
            bg = xbc[:, bcol].astype(BF16)
            cg = xbc[:, ccol].astype(BF16)
            cb = _dot(cg, bg, NT)
            sg = st_ref[0, :, gs]
            dsg = dstate[:, gs]
            dc = _dot(dys[:, gs], sg, NT)
            db = _dot(xt[:, gs], dsg, NT)
            dx_state = tail_x[:, gs] * _dot(bg, dsg)
            tail_part = xd[:, gs] * dx_state
            yd[:, gs] = dys[:, gs] * _dot(cg, sg) - tail_part
            last = jnp.sum(tail_part, axis=0, keepdims=True) + el_x[:, gs] * jnp.sum(dsg * sg, axis=0, keepdims=True)
            lastv[:, gs] = jnp.broadcast_to(last, (8, GW))
            dcb = jnp.zeros((L, L), F32)
            for r in range(SSM_R):
                h = g * SSM_R + r
                hs = slice(h * SSM_P, (h + 1) * SSM_P)
                seg = cs[:, h:h + 1] - cst[h:h + 1, :]
                dec = jnp.exp(jnp.where(lower, seg, -1e30))
                m = cb * dec
                dm = _dot(dyb[:, hs], xdb[:, hs], NT)
                dcb = dcb + dm * dec
                e = dm * m
                row_part = row_part + jnp.where(lane == h, jnp.sum(e, axis=1, keepdims=True), 0.0)
                col_part = col_part + jnp.where(sub == h, jnp.sum(e, axis=0, keepdims=True), 0.0)
                dxd[:, hs] = _dot(m, dyb[:, hs], TN) + dx_state[:, r * SSM_P:(r + 1) * SSM_P]
            dx_ref[:, bcol] = db + _dot(dcb, cg, TN)
            dx_ref[:, ccol] = dc + _dot(dcb, bg)
            dstate[:, gs] = dsg * el_x[:, gs] + _dot(cg, dys[:, gs], TN)
        dxv = dxd[...]
        dx_ref[:, 0:D_INNER] = dxv * dt_x + dyv * sk_ref[...]
        ddt_x = _dotx(dxv * xs, ex_t)
        yst = _dotx(yd[...], ex_t)
        lst = _dotx(lastv[...], ex_t)[0:1, :]
        rows = lax.broadcasted_iota(jnp.int32, (L, LANE), 0)
        dcs = row_part - col_part.T + yst + jnp.where(rows == L - 1, lst, 0.0)
        li = lax.broadcasted_iota(jnp.int32, (L, L), 0)
        si = lax.broadcasted_iota(jnp.int32, (L, L), 1)
        upper = jnp.where(si >= li, 1.0, 0.0).astype(BF16)
        dla = _xdot(upper, dcs)
        ddt_ref[...] = dla * a + ddt_x
        da_ref[...] += jnp.sum(dla * dtv, axis=0, keepdims=True)

    def row(w):
        return pl.BlockSpec((L, w), lambda b, c: (b * nc + nc - 1 - c, 0))

    return _pallas(
        body, name="ssd_bwd", grid=(nb, nc),
        in_specs=[row(XBC), row(LANE), pl.BlockSpec((1, LANE), lambda b, c: (0, 0)),
                  pl.BlockSpec((SSM_H, D_INNER), lambda b, c: (0, 0)),
                  pl.BlockSpec((D_INNER, LANE), lambda b, c: (0, 0)),
                  pl.BlockSpec((1, SSM_N, D_INNER), lambda b, c: (b * nc + nc - 1 - c, 0, 0)),
                  row(D_INNER), pl.BlockSpec((1, D_INNER), lambda b, c: (0, 0))],
        out_specs=[row(XBC), row(LANE), pl.BlockSpec((1, LANE), lambda b, c: (0, 0))],
        out_shape=[jax.ShapeDtypeStruct((t, XBC), F32), jax.ShapeDtypeStruct((t, LANE), F32),
                   jax.ShapeDtypeStruct((1, LANE), F32)],
        scratch_shapes=[pltpu.VMEM((SSM_N, D_INNER), F32), pltpu.VMEM((L, D_INNER), F32),
                        pltpu.VMEM((L, D_INNER), F32), pltpu.VMEM((8, D_INNER), F32)],
        compiler_params=_params(2),
    )(xbc_c, dt, a_log, ex, ext, states, dy, d_x)


def _group_rms(y2):
    gw = D_INNER // SSM_G
    parts = []
    for g in range(SSM_G):
        v = y2[:, g * gw:(g + 1) * gw]
        r = lax.rsqrt(jnp.mean(v * v, axis=1, keepdims=True) + EPS)
        parts.append(jnp.broadcast_to(r, v.shape))
    return jnp.concatenate(parts, axis=1)


def _gate_fwd(y, xbc_c, proj, d_x, gn_w, *, tm=256):
    t = y.shape[0]

    def body(y_ref, x_ref, z_ref, d_ref, w_ref, o_ref):
        y1 = y_ref[...] + d_ref[...] * x_ref[...]
        zv = z_ref[...]
        y2 = y1 * zv * _sigmoid(zv)
        o_ref[...] = (y2 * _group_rms(y2) * w_ref[...]).astype(BF16)

    row = pl.BlockSpec((tm, D_INNER), lambda i: (i, 0))
    vec = pl.BlockSpec((1, D_INNER), lambda i: (0, 0))
    return _pallas(
        body, name="gate_fwd", grid=(t // tm,), in_specs=[row, row, row, vec, vec], out_specs=row,
        out_shape=jax.ShapeDtypeStruct((t, D_INNER), BF16), compiler_params=_params(1),
    )(y, xbc_c, proj, d_x, gn_w)


def _gate_bwd(dyg, y, xbc_c, proj, d_x, gn_w, *, tm=256):
    t = y.shape[0]
    gw = D_INNER // SSM_G

    def body(dg_ref, y_ref, x_ref, z_ref, d_ref, w_ref, dy_ref, dz_ref, dw_ref, dd_ref):
        i = pl.program_id(0)
        xv = x_ref[...]
        dxv = d_ref[...]
        y1 = y_ref[...] + dxv * xv
        zv = z_ref[...]
        sz = _sigmoid(zv)
        y2 = y1 * zv * sz
        rr = _group_rms(y2)
        xh = y2 * rr
        dg = _f32(dg_ref[...])
        gq = dg * w_ref[...]
        prod = gq * xh
        means = []
        for g in range(SSM_G):
            mg = jnp.mean(prod[:, g * gw:(g + 1) * gw], axis=1, keepdims=True)
            means.append(jnp.broadcast_to(mg, (tm, gw)))
        dy2 = rr * (gq - xh * jnp.concatenate(means, axis=1))
        dy1 = dy2 * zv * sz
        dy_ref[...] = dy1
        dz_ref[...] = (dy2 * y1 * _dsilu(zv, sz)).astype(BF16)

        @pl.when(i == 0)
        def _():
            dw_ref[...] = jnp.zeros_like(dw_ref)
            dd_ref[...] = jnp.zeros_like(dd_ref)

        dw_ref[...] += jnp.sum(dg * xh, axis=0, keepdims=True)
        dd_ref[...] += jnp.sum(dy1 * xv, axis=0, keepdims=True)

    row = pl.BlockSpec((tm, D_INNER), lambda i: (i, 0))
    vec = pl.BlockSpec((1, D_INNER), lambda i: (0, 0))
    return _pallas(
        body, name="gate_bwd", grid=(t // tm,), in_specs=[row, row, row, row, vec, vec],
        out_specs=[row, row, vec, vec],
        out_shape=[jax.ShapeDtypeStruct((t, D_INNER), F32),
                   jax.ShapeDtypeStruct((t, D_INNER), BF16), jax.ShapeDtypeStruct((1, D_INNER), F32),
                   jax.ShapeDtypeStruct((1, D_INNER), F32)],
        compiler_params=_params(1),
    )(dyg, y, xbc_c, proj, d_x, gn_w)


ANY = pl.BlockSpec(memory_space=pl.ANY)


def _remote(src, dst, sems, k, to):
    send_sems, recv_sems = sems
    return pltpu.make_async_remote_copy(src_ref=src, dst_ref=dst, send_sem=send_sems.at[k], recv_sem=recv_sems.at[k],
                                        device_id=to, device_id_type=MESH)


NCHIP = 4


def _gathered_shape(kind, shard):
    r, n = shard.shape
    shape = {"cols": (r, NCHIP * n), "slab": (NCHIP, r, n), "rows": (NCHIP * r, n)}[kind]
    return jax.ShapeDtypeStruct(shape, shard.dtype)


def _gather_plan(kinds, shards, outs, sems, small=None):
    ici_s, ici_r, d2d_s, d2d_r = sems
    nw = len(shards)
    per = nw + (small is not None)

    def place():
        x, y, c = lax.axis_index("x"), lax.axis_index("y"), lax.axis_index("c")
        return 2 * x + y, c, (x, y, 1 - c), [(1 - x, y), (x, 1 - y), (1 - x, 1 - y)]

    def region(j, chip, half):
        r, n = shards[j].shape
        h = r // 2
        if kinds[j] == "cols":
            return outs[j].at[pl.ds(half * h, h), pl.ds(pl.multiple_of(chip * n, LANE), n)]
        if kinds[j] == "slab":
            return outs[j].at[chip, pl.ds(half * h, h), :]
        return outs[j].at[pl.ds(chip * r + half * h, h), :]

    def my_sends(me, c, peers):
        cps = []
        for k, (px, py) in enumerate(peers):
            for j in range(nw):
                h = shards[j].shape[0] // 2
                cps.append(_remote(shards[j].at[pl.ds(c * h, h), :], region(j, me, c), (ici_s, ici_r), per * k + j, (px, py, c)))
            if small is not None:
                cps.append(_remote(small[0], small[1].at[me], (ici_s, ici_r), per * k + nw, (px, py, c)))
        return cps

    def start():
        me, c, _, peers = place()
        for cp in my_sends(me, c, peers):
            cp.start()

    def finish():
        me, c, sib, peers = place()
        fwds = []
        for k, (px, py) in enumerate(peers):
            q = 2 * px + py
            for j in range(nw):
                d = region(j, q, c)
                _remote(d, d, (ici_s, ici_r), per * k + j, (px, py, c)).wait_recv()
                fwds.append(_remote(d, d, (d2d_s, d2d_r), nw * k + j, sib))
                fwds[-1].start()
            if small is not None:
                _remote(small[0], small[1].at[q], (ici_s, ici_r), per * k + nw, (px, py, c)).wait_recv()
        for k, (px, py) in enumerate(peers):
            for j in range(nw):
                d = region(j, 2 * px + py, 1 - c)
                _remote(d, d, (d2d_s, d2d_r), nw * k + j, sib).wait_recv()
        for cp in my_sends(me, c, peers) + fwds:
            cp.wait_send()

    return start, finish


def _gather_sems(nw, with_small):
    n_ici = 3 * (nw + with_small)
    return [pltpu.SemaphoreType.DMA((n_ici,)), pltpu.SemaphoreType.DMA((n_ici,)),
            pltpu.SemaphoreType.DMA((3 * nw,)), pltpu.SemaphoreType.DMA((3 * nw,))]


def _gather_shards(kinds, shards, small):
    nw = len(shards)

    def body(*refs):
        ins, sm, outs, osm, sems = refs[:nw], refs[nw], refs[nw + 1:2 * nw + 1], refs[2 * nw + 1], refs[2 * nw + 2:]
        start, finish = _gather_plan(kinds, ins, outs, sems, small=(sm, osm))
        start()
        finish()

    return _pallas(
        body, name="gather_shards", in_specs=[ANY] * (nw + 1), out_specs=[ANY] * (nw + 1),
        out_shape=[_gathered_shape(kd, s) for kd, s in zip(kinds, shards)]
        + [jax.ShapeDtypeStruct((NCHIP,) + small.shape, small.dtype)],
        scratch_shapes=_gather_sems(nw, 1),
    )(*shards, small)


def _matmul_with_gather(a, b, kinds, shards, *, out_dtype, bm, bn, name):
    (m, k), n = a.shape, b.shape[1]
    nw = len(shards)
    nj, ni = n // bn, m // bm

    def body(*refs):
        a_ref, b_ref, ins, o_ref = refs[0], refs[1], refs[2:2 + nw], refs[2 + nw]
        outs, sems = refs[3 + nw:3 + 2 * nw], refs[3 + 2 * nw:]
        start, finish = _gather_plan(kinds, ins, outs, sems)
        j, i = pl.program_id(0), pl.program_id(1)

        @pl.when(jnp.logical_and(j == 0, i == 0))
        def _():
            start()

        o_ref[...] = _dot(a_ref[...], b_ref[...]).astype(out_dtype)

        @pl.when(jnp.logical_and(j == nj - 1, i == ni - 1))
        def _():
            finish()

    return _pallas(
        body, name=name, grid=(nj, ni),
        in_specs=[pl.BlockSpec((bm, k), lambda j, i: (i, 0)), pl.BlockSpec((k, bn), lambda j, i: (0, j))] + [ANY] * nw,
        out_specs=[pl.BlockSpec((bm, bn), lambda j, i: (i, j))] + [ANY] * nw,
        out_shape=[jax.ShapeDtypeStruct((m, n), out_dtype)] + [_gathered_shape(kd, s) for kd, s in zip(kinds, shards)],
        scratch_shapes=_gather_sems(nw, 0), compiler_params=_params(2),
    )(a, b, *shards)


def _other_half(a, c):
    axis = a.ndim - 2
    h = a.shape[axis] // 2
    rows = pl.ds(pl.multiple_of((1 - c) * h, 8), h)
    return a.at[rows, :] if a.ndim == 2 else a.at[:, rows, :]


def _half_shape(a):
    axis = a.ndim - 2
    return jax.ShapeDtypeStruct(a.shape[:axis] + (a.shape[axis] // 2,) + a.shape[axis + 1:], a.dtype)


def _pair_swap(bigs, *, name):
    nb = len(bigs)

    def body(*refs):
        ins, outs, send_sems, recv_sems = refs[:nb], refs[nb:2 * nb], refs[2 * nb], refs[2 * nb + 1]
        x, y, c = lax.axis_index("x"), lax.axis_index("y"), lax.axis_index("c")
        pair = [_remote(_other_half(a, c), q, (send_sems, recv_sems), j, (x, y, 1 - c))
                for j, (a, q) in enumerate(zip(ins, outs))]
        for cp in pair:
            cp.start()
        for cp in pair:
            cp.wait()

    return _pallas(
        body, name=name, in_specs=[ANY] * nb, out_specs=[ANY] * nb, out_shape=[_half_shape(a) for a in bigs],
        scratch_shapes=[pltpu.SemaphoreType.DMA((nb,)), pltpu.SemaphoreType.DMA((nb,))],
    )(*bigs)


def _pair_exchange(bigs, gsmall, grep):
    nb = len(bigs)

    def body(*refs):
        ins, sm, rp = refs[:nb], refs[nb], refs[nb + 1]
        outs, osm, orp = refs[nb + 2:2 * nb + 2], refs[2 * nb + 2], refs[2 * nb + 3]
        pair_s, pair_r, send_sems, recv_sems, local_sems = refs[2 * nb + 4:]
        x, y, c = lax.axis_index("x"), lax.axis_index("y"), lax.axis_index("c")
        me = 4 * x + 2 * y + c
        chip = 2 * x + y
        sib = (x, y, 1 - c)
        pair = [_remote(_other_half(a, c), q, (pair_s, pair_r), j, sib) for j, (a, q) in enumerate(zip(ins, outs))]
        for cp in pair:
            cp.start()
        own = [pltpu.make_async_copy(sm.at[chip], osm.at[me], local_sems.at[0]),
               pltpu.make_async_copy(rp, orp.at[me], local_sems.at[1])]
        for cp in own:
            cp.start()
        peers = []
        for k in range(7):
            fx, fy, fc = ((k + 1) >> 2) & 1, ((k + 1) >> 1) & 1, (k + 1) & 1
            peers.append((1 - x if fx else x, 1 - y if fy else y, 1 - c if fc else c))
        sends = []
        for k, (px, py, pc) in enumerate(peers):
            sends.append(_remote(sm.at[2 * px + py], osm.at[me], (send_sems, recv_sems), 2 * k, (px, py, pc)))
            sends.append(_remote(rp, orp.at[me], (send_sems, recv_sems), 2 * k + 1, (px, py, pc)))
        for cp in sends:
            cp.start()
        for k, (px, py, pc) in enumerate(peers):
            slot = 4 * px + 2 * py + pc
            _remote(sm.at[chip], osm.at[slot], (send_sems, recv_sems), 2 * k, (px, py, pc)).wait_recv()
            _remote(rp, orp.at[slot], (send_sems, recv_sems), 2 * k + 1, (px, py, pc)).wait_recv()
        for cp in pair:
            cp.wait_recv()
        for cp in pair + sends:
            cp.wait_send()
        for cp in own:
            cp.wait()

    return _pallas(
        body, name="pair_exchange", in_specs=[ANY] * (nb + 2), out_specs=[ANY] * (nb + 2),
        out_shape=[_half_shape(a) for a in bigs]
        + [jax.ShapeDtypeStruct((8,) + gsmall.shape[1:], F32), jax.ShapeDtypeStruct((8,) + grep.shape, F32)],
        scratch_shapes=[pltpu.SemaphoreType.DMA((nb,)), pltpu.SemaphoreType.DMA((nb,)),
                        pltpu.SemaphoreType.DMA((14,)), pltpu.SemaphoreType.DMA((14,)),
                        pltpu.SemaphoreType.DMA((2,))],
    )(*bigs, gsmall, grep)


def _core_index():
    return lax.axis_index("c").astype(jnp.int32).reshape(1)


def _half_add(full, other, *, axis, block, name):
    nd = full.ndim
    nblk = other.shape[axis] // block[axis]
    grid = tuple(other.shape[d] // block[d] for d in range(nd))

    def body(c_ref, f_ref, o_ref, out_ref):
        out_ref[...] = (f_ref[...] + o_ref[...]).astype(BF16)

    def full_map(*idx):
        ids, c_ref = list(idx[:nd]), idx[nd]
        ids[axis] = ids[axis] + c_ref[0] * nblk
        return tuple(ids)

    def plain_map(*idx):
        return tuple(idx[:nd])

    return _pallas(
        body, name=name,
        grid_spec=pltpu.PrefetchScalarGridSpec(
            num_scalar_prefetch=1, grid=grid,
            in_specs=[pl.BlockSpec(block, full_map), pl.BlockSpec(block, plain_map)],
            out_specs=pl.BlockSpec(block, plain_map)),
        out_shape=jax.ShapeDtypeStruct(other.shape, BF16), compiler_params=_params(nd),
    )(_core_index(), full, other)


NPEER = 3


def _landing_shape(kind, src):
    if kind == "cols":
        return jax.ShapeDtypeStruct((NPEER, src.shape[0], src.shape[1] // NCHIP), src.dtype)
    return jax.ShapeDtypeStruct((NPEER,) + src.shape[1:], src.dtype)


def _chip_plan(kinds, srcs, lands, sems):
    nw = len(srcs)

    def place():
        x, y, c = lax.axis_index("x"), lax.axis_index("y"), lax.axis_index("c")
        return 2 * x + y, c, [(1 - x, y), (x, 1 - y), (1 - x, 1 - y)]

    def piece(j, chip):
        if kinds[j] == "cols":
            n = srcs[j].shape[1] // NCHIP
            return srcs[j].at[:, pl.ds(pl.multiple_of(chip * n, LANE), n)]
        return srcs[j].at[chip]

    def my_sends(c, peers):
        return [_remote(piece(j, 2 * px + py), lands[j].at[k], sems, nw * k + j, (px, py, c))
                for k, (px, py) in enumerate(peers) for j in range(nw)]

    def start():
        _, c, peers = place()
        for cp in my_sends(c, peers):
            cp.start()

    def finish():
        me, c, peers = place()
        for k, (px, py) in enumerate(peers):
            for j in range(nw):
                _remote(piece(j, me), lands[j].at[k], sems, nw * k + j, (px, py, c)).wait_recv()
        for cp in my_sends(c, peers):
            cp.wait_send()

    return start, finish


def _chip_sems(nw):
    return [pltpu.SemaphoreType.DMA((NPEER * nw,)), pltpu.SemaphoreType.DMA((NPEER * nw,))]


def _chip_exchange(kinds, srcs):
    nw = len(srcs)

    def body(*refs):
        start, finish = _chip_plan(kinds, refs[:nw], refs[nw:2 * nw], refs[2 * nw:])
        start()
        finish()

    return _pallas(
        body, name="chip_exchange", in_specs=[ANY] * nw, out_specs=[ANY] * nw,
        out_shape=[_landing_shape(kd, s) for kd, s in zip(kinds, srcs)], scratch_shapes=_chip_sems(nw),
    )(*srcs)


def _chip_index():
    return (2 * lax.axis_index("x") + lax.axis_index("y")).astype(jnp.int32).reshape(1)


def _chip_sum(own, slots, *, own_block, own_map, block, name):
    npeer = slots.shape[0]
    shape = slots.shape[1:]
    grid = (shape[0] // block[0], shape[1] // block[1])

    def body(p_ref, own_ref, s_ref, o_ref):
        acc = own_ref[...].reshape(block).astype(F32)
        for q in range(npeer):
            acc = acc + s_ref[q].astype(F32)
        o_ref[...] = acc

    return _pallas(
        body, name=name,
        grid_spec=pltpu.PrefetchScalarGridSpec(
            num_scalar_prefetch=1, grid=grid,
            in_specs=[pl.BlockSpec(own_block, own_map),
                      pl.BlockSpec((npeer,) + block, lambda i, j, p: (0, i, j))],
            out_specs=pl.BlockSpec(block, lambda i, j, p: (i, j))),
        out_shape=jax.ShapeDtypeStruct(shape, F32), compiler_params=_params(2),
    )(_chip_index(), own, slots)


def _pair_share(r_in0, r_in1, r_out0, r_out1):
    def body(a0, a1, b0, b1, g0, g1, h0, h1, send_sems, recv_sems):
        x, y, c = lax.axis_index("x"), lax.axis_index("y"), lax.axis_index("c")
        sib = (x, y, 1 - c)
        sends = [_remote(s, d, (send_sems, recv_sems), j, sib)
                 for j, (s, d) in enumerate(zip([a0, a1, b0, b1], [g0, g1, h0, h1]))]
        for cp in sends:
            cp.start()
        for cp in sends:
            cp.wait()

    return _pallas(
        body, name="pair_share", in_specs=[ANY] * 4, out_specs=[ANY] * 4,
        out_shape=[jax.ShapeDtypeStruct(r.shape, F32) for r in (r_in0, r_in1, r_out0, r_out1)],
        scratch_shapes=[pltpu.SemaphoreType.DMA((4,)), pltpu.SemaphoreType.DMA((4,))],
    )(r_in0, r_in1, r_out0, r_out1)


def _adam_math(g, w, m, v):
    c1 = 1.0 - ADAM_B1 ** ADAM_STEP
    c2 = 1.0 - ADAM_B2 ** ADAM_STEP
    m2 = ADAM_B1 * m + (1.0 - ADAM_B1) * g
    v2 = ADAM_B2 * v + (1.0 - ADAM_B2) * (g * g)
    delta = -ADAM_LR * ((m2 / c1) / (jnp.sqrt(v2 / c2) + ADAM_EPS) + ADAM_WD * w)
    return delta, m2, v2


def _adamw_nat(g_mine, g_sib, w, m, v, *, name, tr):
    rows, cw = w.shape
    nt = g_mine.shape[0] // tr

    def body(c_ref, gm_ref, gs_ref, w_ref, m_ref, v_ref, go_ref, d_ref, nm_ref, nv_ref):
        mine = pl.program_id(0) // nt == c_ref[0]
        gv = jnp.where(mine, gm_ref[...], gs_ref[...])[:, 0:cw]
        delta, m2, v2 = _adam_math(gv, w_ref[...], m_ref[...], v_ref[...])
        go_ref[...] = gv
        d_ref[...] = delta
        nm_ref[...] = m2
        nv_ref[...] = v2

    def mine_map(i, c_ref):
        return (jnp.where(i // nt == c_ref[0], i % nt, 0), 0)

    def sib_map(i, c_ref):
        return (jnp.where(i // nt == c_ref[0], 0, i % nt), 0)

    row = pl.BlockSpec((tr, cw), lambda i, c_ref: (i, 0))
    gspec = (tr, g_mine.shape[1])
    out = jax.ShapeDtypeStruct((rows, cw), F32)
    return _pallas(
        body, name=name,
        grid_spec=pltpu.PrefetchScalarGridSpec(
            num_scalar_prefetch=1, grid=(rows // tr,),
            in_specs=[pl.BlockSpec(gspec, mine_map), pl.BlockSpec(gspec, sib_map), row, row, row],
            out_specs=[row, row, row, row]),
        out_shape=[out, out, out, out], compiler_params=_params(1),
    )(_core_index(), g_mine, g_sib, w, m, v)


def _adamw(slots, w, m, v, *, name, tr):
    nd, rows, _ = slots.shape
    c1 = 1.0 - ADAM_B1 ** ADAM_STEP
    c2 = 1.0 - ADAM_B2 ** ADAM_STEP

    def body(s_ref, w_ref, m_ref, v_ref, g_ref, d_ref, nm_ref, nv_ref):
        g = s_ref[0]
        for d in range(1, nd):
            g = g + s_ref[d]
        m2 = ADAM_B1 * m_ref[...] + (1.0 - ADAM_B1) * g
        v2 = ADAM_B2 * v_ref[...] + (1.0 - ADAM_B2) * (g * g)
        g_ref[...] = g
        nm_ref[...] = m2
        nv_ref[...] = v2
        d_ref[...] = -ADAM_LR * ((m2 / c1) / (jnp.sqrt(v2 / c2) + ADAM_EPS) + ADAM_WD * w_ref[...])

    row = pl.BlockSpec((tr, LANE), lambda i: (i, 0))
    out = jax.ShapeDtypeStruct((rows, LANE), F32)
    return _pallas(
        body, name=name, grid=(rows // tr,),
        in_specs=[pl.BlockSpec((nd, tr, LANE), lambda i: (0, i, 0)), row, row, row],
        out_specs=[row, row, row, row], out_shape=[out, out, out, out], compiler_params=_params(1),
    )(slots, w, m, v)


def _rows(a):
    return a.reshape(-1, LANE)


def _pad_rows(a, mult):
    pad = (-a.shape[0]) % mult
    return jnp.pad(a, ((0, pad), (0, 0))) if pad else a


def _pack(parts, mult):
    return _pad_rows(jnp.concatenate([_rows(p) for p in parts], axis=0), mult)


def _unpack(slab, shapes):
    out, r0 = [], 0
    for shp in shapes:
        n = 1
        for s in shp:
            n *= s
        r = n // LANE
        out.append(slab[r0:r0 + r].reshape(shp))
        r0 += r
    return out


def _pack_rep(vecs, scal):
    srow = jnp.concatenate([s.reshape(-1) for s in scal] + [jnp.zeros((LANE - 3 * SSM_H,), F32)]).reshape(1, LANE)
    return _pad_rows(jnp.concatenate([_rows(vv) for vv in vecs] + [srow], axis=0), 8)


def _unpack_rep(slab, vec_shapes, scal_shape):
    vecs, r0 = [], 0
    for shp in vec_shapes:
        vecs.append(slab[r0:r0 + 8].reshape(shp))
        r0 += 8
    srow = slab[r0]
    scal = [srow[i * SSM_H:(i + 1) * SSM_H].reshape(scal_shape) for i in range(3)]
    return vecs, scal


def kernel(x, ev_norm_w, ev_w_in, ev_dw_w, ev_dw_b, ev_ln_w, ev_ln_b, ev_w_out, od_norm_w, od_w_in, od_conv_w, od_conv_b, od_dt_bias, od_a_log, od_d, od_gnorm_w, od_w_out, final_norm_w, loss_target, m_ev_norm_w, m_ev_w_in, m_ev_dw_w, m_ev_dw_b, m_ev_ln_w, m_ev_ln_b, m_ev_w_out, m_od_norm_w, m_od_w_in, m_od_conv_w, m_od_conv_b, m_od_dt_bias, m_od_a_log, m_od_d, m_od_gnorm_w, m_od_w_out, m_final_norm_w, v_ev_norm_w, v_ev_w_in, v_ev_dw_w, v_ev_dw_b, v_ev_ln_w, v_ev_ln_b, v_ev_w_out, v_od_norm_w, v_od_w_in, v_od_conv_w, v_od_conv_b, v_od_dt_bias, v_od_a_log, v_od_d, v_od_gnorm_w, v_od_w_out, v_final_norm_w):
    nb, seq, d = x.shape
    t = nb * seq
    nchip = 4
    xf = x.reshape(t, d)
    tgt = loss_target.reshape(t, d)

    big_w = [ev_w_in[0], od_w_in[0], ev_w_out[0], od_w_out[0]]
    small_w = [ev_dw_w[0], od_norm_w[0], od_conv_w[0], od_conv_b[0], od_gnorm_w[0]]
    small_shapes = [a.shape for a in small_w]
    big_b = [a.astype(BF16) for a in big_w]
    small_slab = _pack(small_w, 8)
    w_in0, w_out0, gath_small = _gather_shards(("cols", "rows"), [big_b[0], big_b[2]], small_slab)
    chip = 2 * lax.axis_index("x") + lax.axis_index("y")
    w_in0 = lax.dynamic_update_slice(w_in0, big_b[0], (0, chip * big_b[0].shape[1]))
    w_out0 = lax.dynamic_update_slice(w_out0, big_b[2], (chip * big_b[2].shape[0], 0))
    gath_small = lax.dynamic_update_slice(gath_small, small_slab[None], (chip, 0, 0))
    per_chip = [_unpack(gath_small[p], small_shapes) for p in range(nchip)]

    def cat(idx, axis):
        return jnp.concatenate([per_chip[p][idx] for p in range(nchip)], axis=axis)

    dw_w = jnp.pad(cat(0, 1), ((0, HALO - CONF_K), (0, 0)))
    dw_w8 = jnp.repeat(dw_w, SUB, axis=0)
    n1_w = cat(1, 0).reshape(1, d)
    conv_w = jnp.pad(cat(2, 1), ((0, PH - SSM_K), (0, 0)))
    conv_b = cat(3, 0).reshape(1, XBC)
    gn_w = cat(4, 0).reshape(1, D_INNER)

    def lanes(a):
        return jnp.pad(a.reshape(1, -1), ((0, 0), (0, LANE - a.size)))

    dt_bias, a_log = lanes(od_dt_bias), lanes(od_a_log)
    d_x = jnp.repeat(od_d.reshape(-1), SSM_P).reshape(1, D_INNER)
    hid = lax.broadcasted_iota(jnp.int32, (SSM_H, D_INNER), 1) // SSM_P
    ex = (hid == lax.broadcasted_iota(jnp.int32, (SSM_H, D_INNER), 0)).astype(BF16)
    ex_t = jnp.pad(ex.T, ((0, 0), (0, LANE - SSM_H)))
    fn_w = final_norm_w.reshape(1, d)

    n0 = _rms_fwd(xf, ev_norm_w, name="rms_fwd0")
    proj0, w_in1g, w_out1 = _matmul_with_gather(n0, w_in0, ("slab", "rows"), [big_b[1], big_b[3]],
                                                out_dtype=BF16, bm=512, bn=1024, name="in_proj0")
    w_in1g = lax.dynamic_update_slice(w_in1g, big_b[1][None], (chip, 0, 0))
    w_out1 = lax.dynamic_update_slice(w_out1, big_b[3], (chip * big_b[3].shape[0], 0))
    w_in1 = jnp.pad(jnp.concatenate([w_in1g[p] for p in range(nchip)], axis=1),
                    ((0, 0), (0, IN_ODD_PAD - IN_ODD)))
    y_conv, u2 = _conf_fwd(proj0, dw_w8, ev_dw_b, ev_ln_w, ev_ln_b, seq)
    o_att, y_att = _sba_fwd(proj0, nb, seq)
    ycat0 = jnp.concatenate([y_conv, y_att], axis=1)
    h1 = _matmul(ycat0, w_out0, mode="nn", out_dtype=F32, bm=512, bn=d, bk=D_INNER, name="out_proj0", residual=xf)
    n1 = _rms_fwd(h1, n1_w, name="rms_fwd1")
    proj1 = _matmul(n1, w_in1, mode="nn", out_dtype=F32, bm=512, bn=768, bk=d, name="in_proj1", n_major=True)
    xbc_c = _xconv_fwd(proj1, conv_w, conv_b, seq)
    dt = _dt_fwd(proj1, dt_bias)
    y_ssd, states = _ssd_fwd(xbc_c, dt, a_log, ex, nb, seq)
    yg = _gate_fwd(y_ssd, xbc_c, proj1, d_x, gn_w)
    h2 = _matmul(yg, w_out1, mode="nn", out_dtype=F32, bm=512, bn=d, bk=D_INNER, name="out_proj1", residual=h1)
    dh2, g_fn, loss_part = _final_loss(h2, fn_w, tgt)

    dyg = _matmul(dh2, w_out1, mode="nt", out_dtype=BF16, bm=512, bn=1024, bk=d, name="d_out_proj1")
    g_w_out1 = _matmul(yg, dh2, mode="tn", out_dtype=F32, bm=1024, bn=d, bk=1024, name="dw_out_proj1")
    dy_ssd, dz, g_gn, g_dx = _gate_bwd(dyg, y_ssd, xbc_c, proj1, d_x, gn_w)
    dxbc_c, ddt, g_a = _ssd_bwd(xbc_c, dt, a_log, ex, ex_t, states, dy_ssd, d_x, nb, seq)
    dxbc, g_conv_w, g_conv_b = _xconv_bwd(proj1, dxbc_c, conv_w, conv_b, seq)
    ddt_raw, g_dt_bias = _dt_bwd(proj1, dt_bias, ddt)
    dproj1 = jnp.concatenate([dz, dxbc, ddt_raw.astype(BF16),
                              jnp.zeros((t, IN_ODD_PAD - IN_ODD - (LANE - SSM_H)), BF16)], axis=1)
    dn1 = _matmul(dproj1, w_in1, mode="nt", out_dtype=BF16, bm=1024, bn=d, bk=1792, name="d_in_proj1")
    g_w_in1 = _matmul(n1, dproj1, mode="tn", out_dtype=F32, bm=d, bn=1792, bk=1024, name="dw_in_proj1")
    dh1, g_n1 = _rms_bwd(dn1, h1, n1_w, dh2, name="rms_bwd1")

    dycat0 = _matmul(dh1, w_out0, mode="nt", out_dtype=BF16, bm=512, bn=1024, bk=d, name="d_out_proj0")
    g_w_out0 = _matmul(ycat0, dh1, mode="tn", out_dtype=F32, bm=1024, bn=d, bk=1024, name="dw_out_proj0")
    dq, dk, dv, dga = _sba_bwd(proj0, o_att, dycat0, nb, seq)
    ro = D_INNER // nchip
    n1 = IN_ODD // nchip
    n1p = -(-n1 // LANE) * LANE
    g_w_out1c = g_w_out1.reshape(nchip, ro, d)
    q_in1, q_out1 = _pair_swap([g_w_in1, g_w_out1c], name="pair_swap_l1")
    s_in1n = _half_add(g_w_in1, q_in1, axis=0, block=(128, IN_ODD_PAD), name="half_add_in1")
    s_in1 = jnp.stack([jnp.pad(s_in1n[:, p * n1:(p + 1) * n1], ((0, 0), (0, n1p - n1))) for p in range(nchip)])
    s_out1 = _half_add(g_w_out1c, q_out1, axis=1, block=(1, ro // 2, d), name="half_add_out1")
    dpc, g_dw_w, g_dw_b, g_ln_w, g_ln_b, l_in1, l_out1 = _conf_bwd(
        proj0, u2, dycat0, dw_w8, ev_ln_w, ev_ln_b, seq, ("slab", "slab"), [s_in1, s_out1])
    dproj0 = jnp.concatenate([dpc, dq, dk.astype(BF16), dv.astype(BF16), dga], axis=1)
    dn0 = _matmul(dproj0, w_in0, mode="nt", out_dtype=BF16, bm=1024, bn=d, bk=1792, name="d_in_proj0")
    g_w_in0 = _matmul(n0, dproj0, mode="tn", out_dtype=F32, bm=d, bn=1792, bk=1024, name="dw_in_proj0")
    grad_x, g_n0 = _rms_bwd(dn0, xf, ev_norm_w, dh1, name="rms_bwd0")

    g_dw_w = g_dw_w.reshape(HALO, SUB, CONV_W).sum(axis=1)[0:CONF_K]
    g_dw_b, g_ln_w, g_ln_b = (a.sum(axis=0, keepdims=True) for a in (g_dw_b, g_ln_w, g_ln_b))
    g_conv_w = g_conv_w.reshape(PH, SUB, XBC).sum(axis=1)[0:SSM_K]
    g_conv_b = g_conv_b.sum(axis=0, keepdims=True)
    a_neg = -jnp.exp(od_a_log.reshape(-1))
    g_a_log = g_a[0, 0:SSM_H] * a_neg
    g_d = g_dx.reshape(SSM_H, SSM_P).sum(axis=1)

    def chip_slab_small(p):
        c0, c1, c2, c3 = CONV_W // nchip, d // nchip, XBC // nchip, D_INNER // nchip
        return _pack([g_dw_w[:, p * c0:(p + 1) * c0], g_n1[0, p * c1:(p + 1) * c1],
                      g_conv_w[:, p * c2:(p + 1) * c2], g_conv_b[0, p * c2:(p + 1) * c2],
                      g_gn[0, p * c3:(p + 1) * c3]], 8)

    gsmall = jnp.stack([chip_slab_small(p) for p in range(nchip)])
    rep_vec_shapes = [ev_norm_w.shape, ev_dw_b.shape, ev_ln_w.shape, ev_ln_b.shape, final_norm_w.shape]
    grep = _pack_rep([g_n0, g_dw_b, g_ln_w, g_ln_b, g_fn], [g_dt_bias[0, 0:SSM_H], g_a_log, g_d])

    g_w_out0c = g_w_out0.reshape(nchip, ro, d)
    q_in0, q_out0, ssmall, srep = _pair_exchange([g_w_in0, g_w_out0c], gsmall, grep)
    s_in0 = _half_add(g_w_in0, q_in0, axis=0, block=(128, IN_EVEN), name="half_add_in0")
    s_out0 = _half_add(g_w_out0c, q_out0, axis=1, block=(1, ro // 2, d), name="half_add_out0")
    l_in0, l_out0 = _chip_exchange(("cols", "slab"), [s_in0, s_out0])
    r_in0 = _chip_sum(s_in0, l_in0, own_block=(128, IN_EVEN // nchip), own_map=lambda i, j, p: (i, p[0]),
                      block=(128, IN_EVEN // nchip), name="chip_sum_in0")
    r_in1 = _chip_sum(s_in1, l_in1, own_block=(1, 256, n1p), own_map=lambda i, j, p: (p[0], i, 0),
                      block=(256, n1p), name="chip_sum_in1")
    r_out0 = _chip_sum(s_out0, l_out0, own_block=(1, ro // 2, d), own_map=lambda i, j, p: (p[0], 0, 0),
                       block=(ro // 2, d), name="chip_sum_out0")
    r_out1 = _chip_sum(s_out1, l_out1, own_block=(1, ro // 2, d), own_map=lambda i, j, p: (p[0], 0, 0),
                       block=(ro // 2, d), name="chip_sum_out1")
    big_r = [r_in0, r_in1, r_out0, r_out1]
    big_q = _pair_share(*big_r)

    big_m = [m_ev_w_in[0], m_od_w_in[0], m_ev_w_out[0], m_od_w_out[0]]
    big_v = [v_ev_w_in[0], v_od_w_in[0], v_ev_w_out[0], v_od_w_out[0]]
    big_names = ["adamw_in0", "adamw_in1", "adamw_out0", "adamw_out1"]
    out_bigs = [_adamw_nat(gm, gs, w, m, v, name=nm, tr=128)
                for gm, gs, w, m, v, nm in zip(big_r, big_q, big_w, big_m, big_v, big_names)]

    def upd(slots, ws, ms, vs, packer, name, tr):
        return _adamw(slots, packer(ws), packer(ms), packer(vs), name=name, tr=tr)

    small_m = [m_ev_dw_w[0], m_od_norm_w[0], m_od_conv_w[0], m_od_conv_b[0], m_od_gnorm_w[0]]
    small_v = [v_ev_dw_w[0], v_od_norm_w[0], v_od_conv_w[0], v_od_conv_b[0], v_od_gnorm_w[0]]
    out_small = upd(ssmall, small_w, small_m, small_v, lambda a: _pack(a, 8), "adamw_small", ssmall.shape[1])

    def rep_pack(a):
        return _pack_rep(a[0:5], a[5:8])

    rep_w = [ev_norm_w, ev_dw_b, ev_ln_w, ev_ln_b, final_norm_w, od_dt_bias, od_a_log, od_d]
    rep_m = [m_ev_norm_w, m_ev_dw_b, m_ev_ln_w, m_ev_ln_b, m_final_norm_w, m_od_dt_bias, m_od_a_log, m_od_d]
    rep_v = [v_ev_norm_w, v_ev_dw_b, v_ev_ln_w, v_ev_ln_b, v_final_norm_w, v_od_dt_bias, v_od_a_log, v_od_d]
    out_rep = upd(srep, rep_w, rep_m, rep_v, rep_pack, "adamw_rep", srep.shape[1])

    results = []
    for kind in range(4):
        bw = [o[kind].reshape((1,) + o[kind].shape) for o in out_bigs]
        sw = _unpack(out_small[kind], small_shapes)
        vecs, scal = _unpack_rep(out_rep[kind], rep_vec_shapes, od_dt_bias.shape)
        results.append([
            vecs[0], bw[0], sw[0].reshape(ev_dw_w.shape), vecs[1], vecs[2], vecs[3], bw[2],
            sw[1].reshape(od_norm_w.shape), bw[1], sw[2].reshape(od_conv_w.shape), sw[3].reshape(od_conv_b.shape),
            scal[0], scal[1], scal[2], sw[4].reshape(od_gnorm_w.shape), bw[3], vecs[4]])
    loss = lax.psum(loss_part[0, 0], ("x", "y", "c"))
    return (loss, grad_x.reshape(x.shape), *results[0], *results[1], *results[2], *results[3])
```

```python
import jax
import jax.numpy as jnp
from jax import lax
from jax.experimental import pallas as pl
from jax.experimental.pallas import tpu as pltpu

F32 = jnp.float32
BF16 = jnp.bfloat16

D_MODEL = 1024
CONV_W = 1024
ATT_W = 1024
HEAD_DIM = 128
N_HEADS = 8
CONF_K = 31
IN_EVEN = 7168
D_INNER = 2048
SSM_P = 64
SSM_H = 32
SSM_G = 4
SSM_R = SSM_H // SSM_G
SSM_N = 128
SSM_K = 4
CHUNK = 128
XBC = D_INNER + 2 * SSM_G * SSM_N
IN_ODD = D_INNER + XBC + SSM_H
IN_ODD_PAD = 5376
EPS = 1e-6
QB = 128
NEG_CUT = -100.0

ADAM_LR = 0.001
ADAM_B1 = 0.9
ADAM_B2 = 0.999
ADAM_EPS = 1e-08
ADAM_WD = 0.01
ADAM_STEP = 10

LANE = 128
VMEM_LIMIT = 56 * 1024 * 1024
MESH = pl.DeviceIdType.MESH

NN = (((1,), (0,)), ((), ()))
NT = (((1,), (1,)), ((), ()))
TN = (((0,), (0,)), ((), ()))


def _pallas(body, **kw):
    return pl.pallas_call(body, **kw)


def _params(n_axes):
    return pltpu.CompilerParams(dimension_semantics=("arbitrary",) * n_axes, vmem_limit_bytes=VMEM_LIMIT)


def _dot(a, b, dims=NN):
    return lax.dot_general(a.astype(BF16), b.astype(BF16), dims, preferred_element_type=F32)


def _parts(x):
    h = x.astype(BF16)
    r = x - h.astype(F32)
    m = r.astype(BF16)
    l = (r - m.astype(F32)).astype(BF16)
    return (h, m, l)


def _dotx(x, e01, dims=NN):
    acc = None
    for p in _parts(x):
        t = lax.dot_general(p, e01, dims, preferred_element_type=F32)
        acc = t if acc is None else acc + t
    return acc


def _dotx2(x, e01, dims=NN):
    h = x.astype(BF16)
    l = (x - h.astype(F32)).astype(BF16)
    return (lax.dot_general(h, e01, dims, preferred_element_type=F32)
            + lax.dot_general(l, e01, dims, preferred_element_type=F32))


def _xdot(e01, x, dims=NN):
    acc = None
    for p in _parts(x):
        t = lax.dot_general(e01, p, dims, preferred_element_type=F32)
        acc = t if acc is None else acc + t
    return acc


def _f32(x):
    return x.astype(F32)


def _sigmoid(x):
    return 1.0 / (1.0 + jnp.exp(-x))


def _dsilu(x, s):
    return s * (1.0 + x * (1.0 - s))


def _matmul(a, b, *, mode, out_dtype, bm, bn, bk, name, residual=None, n_major=False, comm_kinds=(), comm_srcs=()):
    if mode == "nn":
        (m, k), n = a.shape, b.shape[1]
        a_blk, a_map = (bm, bk), lambda i, j, kk: (i, kk)
        b_blk, b_map = (bk, bn), lambda i, j, kk: (kk, j)
        dims = NN
    elif mode == "nt":
        (m, k), n = a.shape, b.shape[0]
        a_blk, a_map = (bm, bk), lambda i, j, kk: (i, kk)
        b_blk, b_map = (bn, bk), lambda i, j, kk: (j, kk)
        dims = NT
    else:
        (k, m), n = a.shape, b.shape[1]
        a_blk, a_map = (bk, bm), lambda i, j, kk: (kk, i)
        b_blk, b_map = (bk, bn), lambda i, j, kk: (kk, j)
        dims = TN
    bm, bn, bk = min(bm, m), min(bn, n), min(bk, k)
    if mode != "nn":
        a_blk = (bm, bk) if mode == "nt" else (bk, bm)
        b_blk = (bn, bk) if mode == "nt" else (bk, bn)
    else:
        a_blk, b_blk = (bm, bk), (bk, bn)
    assert m % bm == 0 and n % bn == 0 and k % bk == 0, (name, m, n, k)
    nk = k // bk
    has_res = residual is not None

    def order(f):
        return (lambda j, i, kk: f(i, j, kk)) if n_major else f

    nw = len(comm_srcs)
    grid = (n // bn, m // bm, nk) if n_major else (m // bm, n // bn, nk)

    def body(*refs):
        a_ref, b_ref = refs[0], refs[1]
        r_ref = refs[2] if has_res else None
        n_in = 2 + has_res + nw
        o_ref = refs[n_in]
        n_out = n_in + 1 + nw

        def finish(r):
            if has_res:
                r = r + r_ref[...]
            o_ref[...] = r.astype(out_dtype)

        def compute():
            if nk == 1:
                finish(_dot(a_ref[...], b_ref[...], dims))
                return
            acc_ref = refs[n_out]
            kk = pl.program_id(2)

            @pl.when(kk == 0)
            def _():
                acc_ref[...] = jnp.zeros_like(acc_ref)

            acc_ref[...] += _dot(a_ref[...], b_ref[...], dims)

            @pl.when(kk == nk - 1)
            def _():
                finish(acc_ref[...])

        if not nw:
            compute()
            return
        start, done = _chip_plan(comm_kinds, refs[2 + has_res:n_in], refs[n_in + 1:n_out], refs[n_out + (nk > 1):])
        ids = [pl.program_id(ax) for ax in range(3)]

        @pl.when(jnp.logical_and(jnp.logical_and(ids[0] == 0, ids[1] == 0), ids[2] == 0))
        def _():
            start()

        compute()

        @pl.when(jnp.logical_and(jnp.logical_and(ids[0] == grid[0] - 1, ids[1] == grid[1] - 1), ids[2] == grid[2] - 1))
        def _():
            done()

    in_specs = [pl.BlockSpec(a_blk, order(a_map)), pl.BlockSpec(b_blk, order(b_map))]
    args = [a, b]
    out_map = order(lambda i, j, kk: (i, j))
    if has_res:
        in_specs.append(pl.BlockSpec((bm, bn), out_map))
        args.append(residual)
    any_spec = pl.BlockSpec(memory_space=pl.ANY)
    out_specs = [pl.BlockSpec((bm, bn), out_map)] + [any_spec] * nw
    out_shape = [jax.ShapeDtypeStruct((m, n), out_dtype)] + [_landing_shape(kd, s) for kd, s in zip(comm_kinds, comm_srcs)]
    res = _pallas(
        body, name=name, grid=grid, in_specs=in_specs + [any_spec] * nw, out_specs=out_specs, out_shape=out_shape,
        scratch_shapes=([pltpu.VMEM((bm, bn), F32)] if nk > 1 else []) + (_chip_sems(nw) if nw else []),
        compiler_params=_params(3),
    )(*args, *comm_srcs)
    return res if nw else res[0]


def _rms_fwd(x, w, *, name, tm=512):
    t, d = x.shape

    def body(x_ref, w_ref, o_ref):
        xv = x_ref[...]
        r = lax.rsqrt(jnp.mean(xv * xv, axis=1, keepdims=True) + EPS)
        o_ref[...] = (xv * r * w_ref[...]).astype(BF16)

    return _pallas(
        body, name=name, grid=(t // tm,),
        in_specs=[pl.BlockSpec((tm, d), lambda i: (i, 0)), pl.BlockSpec((1, d), lambda i: (0, 0))],
        out_specs=pl.BlockSpec((tm, d), lambda i: (i, 0)),
        out_shape=jax.ShapeDtypeStruct((t, d), BF16), compiler_params=_params(1),
    )(x, w)


def _rms_bwd(dn, x, w, dres, *, name, tm=512):
    t, d = x.shape

    def body(dn_ref, x_ref, w_ref, dr_ref, dx_ref, dw_ref):
        i = pl.program_id(0)
        xv = x_ref[...]
        r = lax.rsqrt(jnp.mean(xv * xv, axis=1, keepdims=True) + EPS)
        xh = xv * r
        dy = dn_ref[...].astype(F32)
        g = dy * w_ref[...]
        dx_ref[...] = dr_ref[...] + r * (g - xh * jnp.mean(g * xh, axis=1, keepdims=True))

        @pl.when(i == 0)
        def _():
            dw_ref[...] = jnp.zeros_like(dw_ref)

        dw_ref[...] += jnp.sum(dy * xh, axis=0, keepdims=True)

    row = pl.BlockSpec((tm, d), lambda i: (i, 0))
    vec = pl.BlockSpec((1, d), lambda i: (0, 0))
    return _pallas(
        body, name=name, grid=(t // tm,), in_specs=[row, row, vec, row], out_specs=[row, vec],
        out_shape=[jax.ShapeDtypeStruct((t, d), F32), jax.ShapeDtypeStruct((1, d), F32)],
        compiler_params=_params(1),
    )(dn, x, w, dres)


def _final_loss(h, w, target, *, tm=512):
    t, d = h.shape

    def body(h_ref, w_ref, t_ref, dh_ref, dw_ref, loss_ref):
        i = pl.program_id(0)
        xv = h_ref[...]
        r = lax.rsqrt(jnp.mean(xv * xv, axis=1, keepdims=True) + EPS)
        xh = xv * r
        wv = w_ref[...]
        err = xh * wv - t_ref[...]
        dy = err * (1.0 / d)
        g = dy * wv
        dh_ref[...] = r * (g - xh * jnp.mean(g * xh, axis=1, keepdims=True))

        @pl.when(i == 0)
        def _():
            dw_ref[...] = jnp.zeros_like(dw_ref)
            loss_ref[...] = jnp.zeros_like(loss_ref)

        dw_ref[...] += jnp.sum(dy * xh, axis=0, keepdims=True)
        part = jnp.sum(jnp.sum(err * err, axis=1, keepdims=True), axis=0, keepdims=True)
        loss_ref[...] += part * (0.5 / d)

    row = pl.BlockSpec((tm, d), lambda i: (i, 0))
    vec = pl.BlockSpec((1, d), lambda i: (0, 0))
    return _pallas(
        body, name="final_loss", grid=(t // tm,), in_specs=[row, vec, row],
        out_specs=[row, vec, pl.BlockSpec((1, LANE), lambda i: (0, 0))],
        out_shape=[jax.ShapeDtypeStruct((t, d), F32), jax.ShapeDtypeStruct((1, d), F32),
                   jax.ShapeDtypeStruct((1, LANE), F32)],
        compiler_params=_params(1),
    )(h, w, target)


HALO = 32


SUB = 8
RC = 16


def _make_shifts(sh_ref, rows, shifts=tuple(range(1, SUB))):
    for s in shifts:
        sh_ref[s, 0:rows, :] = sh_ref[0, s:s + rows, :]


def _shifted(sh_ref, r0, j, rows):
    return sh_ref[j % SUB, pl.ds(r0 + (j - j % SUB), rows), :]


def _taps(w8_ref, sh_ref, r0, first, step, init):
    accs = [init] * (RC // SUB)
    for k in range(CONF_K):
        wk = w8_ref[k * SUB:(k + 1) * SUB, :]
        x = _shifted(sh_ref, r0, first + step * k, RC)
        accs = [a + wk * x[q * SUB:(q + 1) * SUB] for q, a in enumerate(accs)]
    return jnp.concatenate(accs, axis=0)


def _conf_fwd(proj, dw_w, dw_b, ln_w, ln_b, seq, *, tm=256):
    t = proj.shape[0]
    c = CONV_W
    tps = seq // tm
    hb = tm // HALO

    def body(a_ref, b_ref, g_ref, ha_ref, hb_ref, w_ref, wb_ref, lw_ref, lb_ref, y_ref, u2_ref, sh_ref):
        i = pl.program_id(0)
        keep = jnp.where(i % tps == 0, 0.0, 1.0)
        sh_ref[0, 0:HALO, :] = _f32(ha_ref[...]) * _sigmoid(_f32(hb_ref[...])) * keep
        sh_ref[0, HALO:HALO + tm, :] = _f32(a_ref[...]) * _sigmoid(_f32(b_ref[...]))
        _make_shifts(sh_ref, tm + HALO - SUB)

        def chunk(ci, carry):
            r0 = pl.multiple_of(ci * RC, RC)
            acc = _taps(w_ref, sh_ref, r0, HALO - CONF_K + 1, 1, jnp.broadcast_to(wb_ref[...], (SUB, c)))
            u2_ref[pl.ds(r0, RC), :] = acc
            mu = jnp.mean(acc, axis=1, keepdims=True)
            xc = acc - mu
            rs = lax.rsqrt(jnp.mean(xc * xc, axis=1, keepdims=True) + EPS)
            u3 = xc * rs * lw_ref[...] + lb_ref[...]
            gv = _f32(g_ref[pl.ds(r0, RC), :])
            y_ref[pl.ds(r0, RC), :] = (u3 * _sigmoid(u3) * gv * _sigmoid(gv)).astype(BF16)
            return carry

        lax.fori_loop(0, tm // RC, chunk, 0, unroll=2)

    def col(j):
        return pl.BlockSpec((tm, c), lambda i: (i, j))

    def prev(j):
        return pl.BlockSpec((HALO, c), lambda i: (jnp.maximum(i * hb - 1, 0), j))

    vec = pl.BlockSpec((1, c), lambda i: (0, 0))
    return _pallas(
        body, name="conf_fwd", grid=(t // tm,),
        in_specs=[col(0), col(1), col(2), prev(0), prev(1),
                  pl.BlockSpec((HALO * SUB, c), lambda i: (0, 0)), vec, vec, vec],
        out_specs=[pl.BlockSpec((tm, c), lambda i: (i, 0)), pl.BlockSpec((tm, c), lambda i: (i, 0))],
        out_shape=[jax.ShapeDtypeStruct((t, c), BF16), jax.ShapeDtypeStruct((t, c), F32)],
        scratch_shapes=[pltpu.VMEM((SUB, tm + HALO, c), F32)], compiler_params=_params(1),
    )(proj, proj, proj, proj, proj, dw_w, dw_b, ln_w, ln_b)


def _conf_bwd(proj, u2, dycat, dw_w, ln_w, ln_b, seq, comm_kinds, comm_srcs, *, tm=256):
    t = proj.shape[0]
    c = CONV_W
    tps = seq // tm
    hb = tm // HALO
    nhb = t // HALO
    nw = len(comm_srcs)
    nsteps = t // tm

    def fold(v):
        out = v[0:SUB]
        for q in range(1, RC // SUB):
            out = out + v[q * SUB:(q + 1) * SUB]
        return out

    def body(*refs):
        (a_ref, b_ref, g_ref, pa_ref, pb_ref, ng_ref, u2_ref, nu2_ref, dy_ref, ndy_ref,
         w_ref, lw_ref, lb_ref) = refs[:13]
        dp_ref, dww_ref, dwb_ref, dlw_ref, dlb_ref = refs[13 + nw:18 + nw]
        su_ref, sd_ref = refs[18 + 2 * nw:20 + 2 * nw]
        comm_start, comm_finish = _chip_plan(comm_kinds, refs[13:13 + nw], refs[18 + nw:18 + 2 * nw],
                                             refs[20 + 2 * nw:])
        i = pl.program_id(0)
        first = i % tps == 0
        last = i % tps == tps - 1

        @pl.when(i == 0)
        def _():
            comm_start()
            dww_ref[...] = jnp.zeros_like(dww_ref)
            dwb_ref[...] = jnp.zeros_like(dwb_ref)
            dlw_ref[...] = jnp.zeros_like(dlw_ref)
            dlb_ref[...] = jnp.zeros_like(dlb_ref)

        su_ref[0, 0:HALO, :] = _f32(pa_ref[...]) * _sigmoid(_f32(pb_ref[...])) * jnp.where(first, 0.0, 1.0)
        su_ref[0, HALO:HALO + tm, :] = _f32(a_ref[...]) * _sigmoid(_f32(b_ref[...]))
        _make_shifts(su_ref, tm + HALO - SUB)

        def ln_back(u2c, gv, dy):
            mu = jnp.mean(u2c, axis=1, keepdims=True)
            xc = u2c - mu
            rs = lax.rsqrt(jnp.mean(xc * xc, axis=1, keepdims=True) + EPS)
            xh = xc * rs
            lw = lw_ref[...]
            u3 = xh * lw + lb_ref[...]
            s3 = _sigmoid(u3)
            sg = _sigmoid(gv)
            dgc = dy * (u3 * s3) * _dsilu(gv, sg)
            du3 = dy * gv * sg * _dsilu(u3, s3)
            dxh = du3 * lw
            du2 = rs * (dxh - jnp.mean(dxh, axis=1, keepdims=True)
                        - xh * jnp.mean(dxh * xh, axis=1, keepdims=True))
            return du2, dgc, du3, xh

        def tile_chunk(ci, carry):
            r0 = pl.multiple_of(ci * RC, RC)
            rows = pl.ds(r0, RC)
            du2, dgc, du3, xh = ln_back(u2_ref[rows, :], _f32(g_ref[rows, :]), _f32(dy_ref[rows, :]))
            sd_ref[0, rows, :] = du2
            dp_ref[rows, 2 * c:3 * c] = dgc.astype(BF16)
            dwb_ref[...] += fold(du2)
            dlw_ref[...] += fold(du3 * xh)
            dlb_ref[...] += fold(du3)
            return carry

        lax.fori_loop(0, tm // RC, tile_chunk, 0, unroll=2)
        live = jnp.where(last, 0.0, 1.0)
        for ci in range(HALO // RC):
            rows = slice(ci * RC, (ci + 1) * RC)
            du2, _, _, _ = ln_back(nu2_ref[rows, :], _f32(ng_ref[rows, :]), _f32(ndy_ref[rows, :]))
            sd_ref[0, tm + ci * RC:tm + (ci + 1) * RC, :] = du2 * live
        _make_shifts(sd_ref, tm + HALO - SUB)

        def tap_chunk(ci, carry):
            r0 = pl.multiple_of(ci * RC, RC)
            rows = pl.ds(r0, RC)
            du1 = _taps(w_ref, sd_ref, r0, CONF_K - 1, -1, jnp.zeros((SUB, c), F32))
            sb = _sigmoid(_f32(b_ref[rows, :]))
            dp_ref[rows, 0:c] = (du1 * sb).astype(BF16)
            dp_ref[rows, c:2 * c] = (du1 * _f32(a_ref[rows, :]) * sb * (1.0 - sb)).astype(BF16)
            du2 = sd_ref[0, rows, :]
            for k in range(CONF_K):
                dww_ref[k * SUB:(k + 1) * SUB, :] += fold(du2 * _shifted(su_ref, r0, HALO - CONF_K + 1 + k, RC))
            return carry

        lax.fori_loop(0, tm // RC, tap_chunk, 0)

        @pl.when(i == nsteps - 1)
        def _():
            comm_finish()

    def col(j):
        return pl.BlockSpec((tm, c), lambda i: (i, j))

    def prev(j):
        return pl.BlockSpec((HALO, c), lambda i: (jnp.maximum(i * hb - 1, 0), j))

    def nxt(j):
        return pl.BlockSpec((HALO, c), lambda i: (jnp.minimum((i + 1) * hb, nhb - 1), j))

    vec = pl.BlockSpec((1, c), lambda i: (0, 0))
    acc = pl.BlockSpec((SUB, c), lambda i: (0, 0))
    any_spec = pl.BlockSpec(memory_space=pl.ANY)
    return _pallas(
        body, name="conf_bwd", grid=(nsteps,),
        in_specs=[col(0), col(1), col(2), prev(0), prev(1), nxt(2), col(0), nxt(0), col(0), nxt(0),
                  pl.BlockSpec((HALO * SUB, c), lambda i: (0, 0)), vec, vec] + [any_spec] * nw,
        out_specs=[pl.BlockSpec((tm, 3 * c), lambda i: (i, 0)),
                   pl.BlockSpec((HALO * SUB, c), lambda i: (0, 0)), acc, acc, acc] + [any_spec] * nw,
        out_shape=[jax.ShapeDtypeStruct((t, 3 * c), BF16), jax.ShapeDtypeStruct((HALO * SUB, c), F32),
                   jax.ShapeDtypeStruct((SUB, c), F32), jax.ShapeDtypeStruct((SUB, c), F32),
                   jax.ShapeDtypeStruct((SUB, c), F32)]
        + [_landing_shape(kd, s) for kd, s in zip(comm_kinds, comm_srcs)],
        scratch_shapes=[pltpu.VMEM((SUB, tm + HALO, c), F32), pltpu.VMEM((SUB, tm + HALO, c), F32)] + _chip_sems(nw),
        compiler_params=_params(1),
    )(proj, proj, proj, proj, proj, proj, u2, u2, dycat, dycat, dw_w, ln_w, ln_b, *comm_srcs)


Q_COL = 3 * CONV_W // HEAD_DIM
K_COL = Q_COL + N_HEADS
V_COL = K_COL + N_HEADS
GA_COL = V_COL + N_HEADS


SBA_TQ = 256
SBA_WK = 4 * QB


def _sb_window(qs, kw, ws, limit, t0, carry):
    tq, wk = qs.shape[0], kw.shape[0]
    z = _dot(qs, kw, NT)
    sg = ws + lax.broadcasted_iota(jnp.int32, (tq, wk), 1)
    tg = t0 + lax.broadcasted_iota(jnp.int32, (tq, wk), 0)
    mask = sg < jnp.minimum(tg, limit)
    sp = jnp.log(1.0 + jnp.exp(-jnp.abs(z)))
    ls = jnp.minimum(z, 0.0) - sp
    lk = jnp.where(mask, ls - z, 0.0)
    jj = lax.broadcasted_iota(jnp.int32, (QB, QB), 0)
    ss = lax.broadcasted_iota(jnp.int32, (QB, QB), 1)
    ustrict = jnp.where(jj > ss, 1.0, 0.0).astype(BF16)
    laters = [None] * (wk // QB)
    for ch in reversed(range(wk // QB)):
        lkc = lk[:, ch * QB:(ch + 1) * QB]
        laters[ch] = carry + _dotx2(lkc, ustrict)
        carry = carry + jnp.sum(lkc, axis=1, keepdims=True)
    w = jnp.where(mask, jnp.exp(ls + jnp.concatenate(laters, axis=1)), 0.0)
    return mask, ls, w, carry


def _sba_fwd(proj, nb, seq, *, tq=SBA_TQ, wk=SBA_WK):
    t = proj.shape[0]
    wk = min(wk, seq)
    nq = seq // tq
    scale = HEAD_DIM ** -0.5

    def body(q_ref, k_ref, v_ref, g_ref, o_ref, y_ref):
        i = pl.program_id(2)
        t0 = i * tq
        qs = (_f32(q_ref[...]) * scale).astype(BF16)

        def window(ws, limit, carry, acc):
            ws = pl.multiple_of(ws, QB)
            _, _, w, carry = _sb_window(qs, k_ref[pl.ds(ws, wk), :], ws, limit, t0, carry)
            return carry, acc + _dot(w, v_ref[pl.ds(ws, wk), :])

        ws0 = jnp.maximum(t0 + tq - wk, 0)
        carry, acc = window(ws0, seq, jnp.zeros((tq, 1), F32), jnp.zeros((tq, HEAD_DIM), F32))

        def cond(st):
            return jnp.logical_and(st[0] > 0, jnp.max(st[1]) > NEG_CUT)

        def step(st):
            c2, a2 = window(jnp.maximum(st[0] - wk, 0), st[0], st[1], st[2])
            return jnp.maximum(st[0] - wk, 0), c2, a2

        _, _, acc = lax.while_loop(cond, step, (ws0, carry, acc))
        o_ref[...] = acc
        gv = _f32(g_ref[...])
        y_ref[...] = (acc * gv * _sigmoid(gv)).astype(BF16)

    def tile(c0):
        return pl.BlockSpec((tq, HEAD_DIM), lambda b, h, i: (b * nq + i, c0 + h))

    def whole(c0):
        return pl.BlockSpec((seq, HEAD_DIM), lambda b, h, i: (b, c0 + h))

    return _pallas(
        body, name="sba_fwd", grid=(nb, N_HEADS, nq),
        in_specs=[tile(Q_COL), whole(K_COL), whole(V_COL), tile(GA_COL)],
        out_specs=[tile(0), tile(0)],
        out_shape=[jax.ShapeDtypeStruct((t, ATT_W), F32), jax.ShapeDtypeStruct((t, ATT_W), BF16)],
        compiler_params=_params(3),
    )(proj, proj, proj, proj)


def _sba_bwd(proj, o, dycat, nb, seq, *, tq=SBA_TQ, wk=SBA_WK):
    t = proj.shape[0]
    wk = min(wk, seq)
    nq = seq // tq
    nwin = -(-seq // wk) + 1
    nch = wk // QB
    scale = HEAD_DIM ** -0.5

    def body(q_ref, k_ref, v_ref, g_ref, o_ref, dy_ref, dq_ref, dk_ref, dv_ref, dg_ref, e_ref, sp_ref):
        i = pl.program_id(2)
        t0 = i * tq

        @pl.when(i == 0)
        def _():
            dk_ref[...] = jnp.zeros_like(dk_ref)
            dv_ref[...] = jnp.zeros_like(dv_ref)

        qs = (_f32(q_ref[...]) * scale).astype(BF16)
        gv = _f32(g_ref[...])
        sg = _sigmoid(gv)
        dy = _f32(dy_ref[...])
        do = (dy * gv * sg).astype(BF16)
        dg_ref[...] = (dy * o_ref[...] * _dsilu(gv, sg)).astype(BF16)

        def start_of(n):
            return pl.multiple_of(jnp.maximum(t0 + tq - (n + 1) * wk, 0), QB)

        def limit_of(n):
            return jnp.where(n == 0, seq, jnp.maximum(t0 + tq - n * wk, 0))

        def near(n, carry):
            ws = start_of(n)
            _, ls, w, carry = _sb_window(qs, k_ref[pl.ds(ws, wk), :], ws, limit_of(n), t0, carry)
            e_ref[n] = w * _dot(do, v_ref[pl.ds(ws, wk), :], NT)
            sp_ref[n] = jnp.exp(ls)
            dv_ref[pl.ds(ws, wk), :] += _dot(w, do, TN)
            return carry

        carry = near(0, jnp.zeros((tq, 1), F32))

        def cond(st):
            return jnp.logical_and(start_of(st[0] - 1) > 0, jnp.max(st[1]) > NEG_CUT)

        def step(st):
            return st[0] + 1, near(st[0], st[1])

        nvis, _ = lax.while_loop(cond, step, (1, carry))

        jj = lax.broadcasted_iota(jnp.int32, (QB, QB), 0)
        ss = lax.broadcasted_iota(jnp.int32, (QB, QB), 1)
        lstrict = jnp.where(jj < ss, 1.0, 0.0).astype(BF16)

        def far(r, st):
            pre, dq = st
            n = nvis - 1 - r
            ws = start_of(n)
            e = e_ref[n]
            spn = sp_ref[n]
            gs = []
            for ch in range(nch):
                ec = e[:, ch * QB:(ch + 1) * QB]
                gs.append(pre + _dotx2(ec, lstrict))
                pre = pre + jnp.sum(ec, axis=1, keepdims=True)
            sgl = ws + lax.broadcasted_iota(jnp.int32, (tq, wk), 1)
            tgl = t0 + lax.broadcasted_iota(jnp.int32, (tq, wk), 0)
            mask = sgl < jnp.minimum(tgl, limit_of(n))
            dz = jnp.where(mask, e * (1.0 - spn) - jnp.concatenate(gs, axis=1) * spn, 0.0).astype(BF16)
            dk_ref[pl.ds(ws, wk), :] += _dot(dz, qs, TN)
            return pre, dq + _dot(dz, k_ref[pl.ds(ws, wk), :])

        _, dq = lax.fori_loop(0, nvis, far, (jnp.zeros((tq, 1), F32), jnp.zeros((tq, HEAD_DIM), F32)))
        dq_ref[...] = (dq * scale).astype(BF16)

    def tile(c0):
        return pl.BlockSpec((tq, HEAD_DIM), lambda b, h, i: (b * nq + i, c0 + h))

    def whole(c0):
        return pl.BlockSpec((seq, HEAD_DIM), lambda b, h, i: (b, c0 + h))

    return _pallas(
        body, name="sba_bwd", grid=(nb, N_HEADS, nq),
        in_specs=[tile(Q_COL), whole(K_COL), whole(V_COL), tile(GA_COL), tile(0),
                  tile(CONV_W // HEAD_DIM)],
        out_specs=[tile(0), whole(0), whole(0), tile(0)],
        out_shape=[jax.ShapeDtypeStruct((t, ATT_W), BF16), jax.ShapeDtypeStruct((t, ATT_W), F32),
                   jax.ShapeDtypeStruct((t, ATT_W), F32), jax.ShapeDtypeStruct((t, ATT_W), BF16)],
        scratch_shapes=[pltpu.VMEM((nwin, tq, wk), F32), pltpu.VMEM((nwin, tq, wk), F32)],
        compiler_params=_params(3),
    )(proj, proj, proj, proj, o, dycat)


CT = 512
PH = 8
XRC = 32
X_SHIFTS = tuple(s for s in range(PH - SSM_K + 1, PH))
D_SHIFTS = tuple(range(1, SSM_K))
Z_BLK = 0
XBC_BLK = D_INNER // CT
DT_BLK = (D_INNER + XBC) // LANE


def _softplus(x):
    return jnp.maximum(x, 0.0) + jnp.log(1.0 + jnp.exp(-jnp.abs(x)))


def _dt_fwd(proj, dt_bias, *, tm=512):
    t = proj.shape[0]

    def body(p_ref, b_ref, o_ref):
        o_ref[...] = _softplus(p_ref[...] + b_ref[...])

    return _pallas(
        body, name="dt_fwd", grid=(t // tm,),
        in_specs=[pl.BlockSpec((tm, LANE), lambda i: (i, DT_BLK)), pl.BlockSpec((1, LANE), lambda i: (0, 0))],
        out_specs=pl.BlockSpec((tm, LANE), lambda i: (i, 0)),
        out_shape=jax.ShapeDtypeStruct((t, LANE), F32), compiler_params=_params(1),
    )(proj, dt_bias)


def _dt_bwd(proj, dt_bias, ddt, *, tm=512):
    t = proj.shape[0]

    def body(p_ref, b_ref, d_ref, o_ref, db_ref):
        i = pl.program_id(0)
        lanes = lax.broadcasted_iota(jnp.int32, (tm, LANE), 1)
        dr = jnp.where(lanes < SSM_H, d_ref[...] * _sigmoid(p_ref[...] + b_ref[...]), 0.0)
        o_ref[...] = dr

        @pl.when(i == 0)
        def _():
            db_ref[...] = jnp.zeros_like(db_ref)

        db_ref[...] += jnp.sum(dr, axis=0, keepdims=True)

    vec = pl.BlockSpec((1, LANE), lambda i: (0, 0))
    row = pl.BlockSpec((tm, LANE), lambda i: (i, 0))
    return _pallas(
        body, name="dt_bwd", grid=(t // tm,),
        in_specs=[pl.BlockSpec((tm, LANE), lambda i: (i, DT_BLK)), vec, row],
        out_specs=[row, vec],
        out_shape=[jax.ShapeDtypeStruct((t, LANE), F32), jax.ShapeDtypeStruct((1, LANE), F32)],
        compiler_params=_params(1),
    )(proj, dt_bias, ddt)


def _xconv_fwd(proj, conv_w, conv_b, seq, *, tm=512):
    t = proj.shape[0]
    tps = seq // tm
    hb = tm // PH

    def body(x_ref, h_ref, w_ref, b_ref, o_ref, sh_ref):
        i = pl.program_id(1)
        sh_ref[0, 0:PH, :] = h_ref[...] * jnp.where(i % tps == 0, 0.0, 1.0)
        sh_ref[0, PH:PH + tm, :] = x_ref[...]
        _make_shifts(sh_ref, tm, X_SHIFTS)

        def chunk(ci, carry):
            r0 = pl.multiple_of(ci * XRC, XRC)
            acc = jnp.zeros((XRC, CT), F32) + b_ref[...]
            for k in range(SSM_K):
                acc = acc + w_ref[k:k + 1, :] * _shifted(sh_ref, r0, PH - SSM_K + 1 + k, XRC)
            o_ref[pl.ds(r0, XRC), :] = acc * _sigmoid(acc)
            return carry

        lax.fori_loop(0, tm // XRC, chunk, 0)

    return _pallas(
        body, name="xconv_fwd", grid=(XBC // CT, t // tm),
        in_specs=[pl.BlockSpec((tm, CT), lambda j, i: (i, XBC_BLK + j)),
                  pl.BlockSpec((PH, CT), lambda j, i: (jnp.maximum(i * hb - 1, 0), XBC_BLK + j)),
                  pl.BlockSpec((PH, CT), lambda j, i: (0, j)),
                  pl.BlockSpec((1, CT), lambda j, i: (0, j))],
        out_specs=pl.BlockSpec((tm, CT), lambda j, i: (i, j)),
        out_shape=jax.ShapeDtypeStruct((t, XBC), F32),
        scratch_shapes=[pltpu.VMEM((SUB, tm + PH, CT), F32)], compiler_params=_params(2),
    )(proj, proj, conv_w, conv_b)


def _xconv_bwd(proj, dxc, conv_w, conv_b, seq, *, tm=512):
    t = proj.shape[0]
    tps = seq // tm
    hb = tm // PH
    nhb = t // PH
    te = tm + PH

    def fold(v):
        out = v[0:SUB]
        for q in range(1, v.shape[0] // SUB):
            out = out + v[q * SUB:(q + 1) * SUB]
        return out

    def body(x_ref, p_ref, n_ref, d_ref, nd_ref, w_ref, b_ref, dx_ref, dw_ref, db_ref, sx_ref, sd_ref):
        i = pl.program_id(1)
        first = i % tps == 0
        last = i % tps == tps - 1

        @pl.when(i == 0)
        def _():
            dw_ref[...] = jnp.zeros_like(dw_ref)
            db_ref[...] = jnp.zeros_like(db_ref)

        sx_ref[0, 0:PH, :] = p_ref[...] * jnp.where(first, 0.0, 1.0)
        sx_ref[0, PH:PH + tm, :] = x_ref[...]
        sx_ref[0, PH + tm:PH + te, :] = n_ref[...]
        _make_shifts(sx_ref, te, X_SHIFTS)

        def dv_of(r0, rows, dy):
            acc = jnp.zeros((rows, CT), F32) + b_ref[...]
            for k in range(SSM_K):
                acc = acc + w_ref[k:k + 1, :] * _shifted(sx_ref, r0, PH - SSM_K + 1 + k, rows)
            return dy * _dsilu(acc, _sigmoid(acc))

        def dv_chunk(ci, carry):
            r0 = pl.multiple_of(ci * XRC, XRC)
            dv = dv_of(r0, XRC, d_ref[pl.ds(r0, XRC), :])
            sd_ref[0, pl.ds(r0, XRC), :] = dv
            db_ref[...] += fold(dv)
            return carry

        lax.fori_loop(0, tm // XRC, dv_chunk, 0)
        sd_ref[0, tm:te, :] = dv_of(tm, PH, nd_ref[...]) * jnp.where(last, 0.0, 1.0)
        _make_shifts(sd_ref, tm, D_SHIFTS)

        def tap_chunk(ci, carry):
            r0 = pl.multiple_of(ci * XRC, XRC)
            dx = jnp.zeros((XRC, CT), F32)
            for k in range(SSM_K):
                dx = dx + w_ref[k:k + 1, :] * _shifted(sd_ref, r0, SSM_K - 1 - k, XRC)
            dx_ref[pl.ds(r0, XRC), :] = dx.astype(BF16)
            dv = sd_ref[0, pl.ds(r0, XRC), :]
            for k in range(SSM_K):
                dw_ref[k * SUB:(k + 1) * SUB, :] += fold(dv * _shifted(sx_ref, r0, PH - SSM_K + 1 + k, XRC))
            return carry

        lax.fori_loop(0, tm // XRC, tap_chunk, 0)

    return _pallas(
        body, name="xconv_bwd", grid=(XBC // CT, t // tm),
        in_specs=[pl.BlockSpec((tm, CT), lambda j, i: (i, XBC_BLK + j)),
                  pl.BlockSpec((PH, CT), lambda j, i: (jnp.maximum(i * hb - 1, 0), XBC_BLK + j)),
                  pl.BlockSpec((PH, CT), lambda j, i: (jnp.minimum((i + 1) * hb, nhb - 1), XBC_BLK + j)),
                  pl.BlockSpec((tm, CT), lambda j, i: (i, j)),
                  pl.BlockSpec((PH, CT), lambda j, i: (jnp.minimum((i + 1) * hb, nhb - 1), j)),
                  pl.BlockSpec((PH, CT), lambda j, i: (0, j)),
                  pl.BlockSpec((1, CT), lambda j, i: (0, j))],
        out_specs=[pl.BlockSpec((tm, CT), lambda j, i: (i, j)),
                   pl.BlockSpec((PH * SUB, CT), lambda j, i: (0, j)),
                   pl.BlockSpec((SUB, CT), lambda j, i: (0, j))],
        out_shape=[jax.ShapeDtypeStruct((t, XBC), BF16), jax.ShapeDtypeStruct((PH * SUB, XBC), F32),
                   jax.ShapeDtypeStruct((SUB, XBC), F32)],
        scratch_shapes=[pltpu.VMEM((SUB, tm + 2 * PH, CT), F32), pltpu.VMEM((SUB, te, CT), F32)],
        compiler_params=_params(2),
    )(proj, proj, proj, dxc, dxc, conv_w, conv_b)


def _ssd_common(xbc, dt, alog, ex):
    L = CHUNK
    a = -jnp.exp(alog)
    la = dt * a
    li = lax.broadcasted_iota(jnp.int32, (L, L), 0)
    si = lax.broadcasted_iota(jnp.int32, (L, L), 1)
    lower = si <= li
    tri = jnp.where(lower, 1.0, 0.0).astype(BF16)
    cs = _xdot(tri, la)
    cst = _dotx(la, tri, (((0,), (1,)), ((), ())))
    csl = cs[L - 1:L, :]
    ecs_x = _dotx2(jnp.exp(cs)[:, 0:SSM_H], ex)
    tail_x = _dotx2(jnp.exp(csl - cs)[:, 0:SSM_H], ex)
    dt_x = _dotx2(dt[:, 0:SSM_H], ex)
    return a, la, lower, tri, cs, cst, ecs_x, tail_x, dt_x


def _ssd_fwd(xbc_c, dt, a_log, ex, nb, seq):
    t = xbc_c.shape[0]
    L = CHUNK
    nc = seq // L
    GW = SSM_R * SSM_P

    def body(x_ref, dt_ref, al_ref, ex_ref, y_ref, st_ref, state):
        c = pl.program_id(1)

        @pl.when(c == 0)
        def _():
            state[...] = jnp.zeros_like(state)

        st_ref[0] = state[...]
        xbc = x_ref[...]
        _, _, lower, _, cs, cst, ecs_x, tail_x, dt_x = _ssd_common(xbc, dt_ref[...], al_ref[...], ex_ref[...])
        xd = xbc[:, 0:D_INNER] * dt_x
        xdb = xd.astype(BF16)
        xt = (xd * tail_x).astype(BF16)
        el_x = ecs_x[L - 1:L, :]
        for g in range(SSM_G):
            bg = xbc[:, D_INNER + g * SSM_N:D_INNER + (g + 1) * SSM_N].astype(BF16)
            cg = xbc[:, D_INNER + (SSM_G + g) * SSM_N:D_INNER + (SSM_G + g + 1) * SSM_N].astype(BF16)
            cb = _dot(cg, bg, NT)
            sg = state[:, g * GW:(g + 1) * GW]
            ys = _dot(cg, sg) * ecs_x[:, g * GW:(g + 1) * GW]
            for r in range(SSM_R):
                h = g * SSM_R + r
                seg = cs[:, h:h + 1] - cst[h:h + 1, :]
                dec = jnp.exp(jnp.where(lower, seg, -1e30))
                yh = _dot(cb * dec, xdb[:, h * SSM_P:(h + 1) * SSM_P])
                y_ref[:, h * SSM_P:(h + 1) * SSM_P] = yh + ys[:, r * SSM_P:(r + 1) * SSM_P]
            state[:, g * GW:(g + 1) * GW] = sg * el_x[:, g * GW:(g + 1) * GW] + _dot(bg, xt[:, g * GW:(g + 1) * GW], TN)

    return _pallas(
        body, name="ssd_fwd", grid=(nb, nc),
        in_specs=[pl.BlockSpec((L, XBC), lambda b, c: (b * nc + c, 0)),
                  pl.BlockSpec((L, LANE), lambda b, c: (b * nc + c, 0)),
                  pl.BlockSpec((1, LANE), lambda b, c: (0, 0)),
                  pl.BlockSpec((SSM_H, D_INNER), lambda b, c: (0, 0))],
        out_specs=[pl.BlockSpec((L, D_INNER), lambda b, c: (b * nc + c, 0)),
                   pl.BlockSpec((1, SSM_N, D_INNER), lambda b, c: (b * nc + c, 0, 0))],
        out_shape=[jax.ShapeDtypeStruct((t, D_INNER), F32),
                   jax.ShapeDtypeStruct((nb * nc, SSM_N, D_INNER), F32)],
        scratch_shapes=[pltpu.VMEM((SSM_N, D_INNER), F32)], compiler_params=_params(2),
    )(xbc_c, dt, a_log, ex)


def _ssd_bwd(xbc_c, dt, a_log, ex, ext, states, dy, d_x, nb, seq):
    t = xbc_c.shape[0]
    L = CHUNK
    nc = seq // L
    GW = SSM_R * SSM_P

    def body(x_ref, dt_ref, al_ref, ex_ref, ext_ref, st_ref, dy_ref, sk_ref, dx_ref, ddt_ref, da_ref,
             dstate, dxd, yd, lastv):
        b = pl.program_id(0)
        c = pl.program_id(1)

        @pl.when(c == 0)
        def _():
            dstate[...] = jnp.zeros_like(dstate)

        @pl.when(jnp.logical_and(b == 0, c == 0))
        def _():
            da_ref[...] = jnp.zeros_like(da_ref)

        xbc = x_ref[...]
        dtv = dt_ref[...]
        ex_t = ext_ref[...]
        a, la, lower, tri, cs, cst, ecs_x, tail_x, dt_x = _ssd_common(xbc, dtv, al_ref[...], ex_ref[...])
        xs = xbc[:, 0:D_INNER]
        xd = xs * dt_x
        xdb = xd.astype(BF16)
        dyv = dy_ref[...]
        dyb = dyv.astype(BF16)
        dys = dyv * ecs_x
        xt = xd * tail_x
        el_x = ecs_x[L - 1:L, :]
        lane = lax.broadcasted_iota(jnp.int32, (L, LANE), 1)
        sub = lax.broadcasted_iota(jnp.int32, (LANE, L), 0)
        row_part = jnp.zeros((L, LANE), F32)
        col_part = jnp.zeros((LANE, L), F32)
        for g in range(SSM_G):
            gs = slice(g * GW, (g + 1) * GW)
            bcol = slice(D_INNER + g * SSM_N, D_INNER + (g + 1) * SSM_N)
            ccol = slice(D_INNER + (SSM_G + g) * SSM_N, D_INNER + (SSM_G + g + 1) * SSM_N)
            bg = xbc[:, bcol].astype(BF16)
            cg = xbc[:, ccol].astype(BF16)
            cb = _dot(cg, bg, NT)
            sg = st_ref[0, :, gs]
            dsg = dstate[:, gs]
            dc = _dot(dys[:, gs], sg, NT)
            db = _dot(xt[:, gs], dsg, NT)
            dx_state = tail_x[:, gs] * _dot(bg, dsg)
            tail_part = xd[:, gs] * dx_state
            yd[:, gs] = dys[:, gs] * _dot(cg, sg) - tail_part
            last = jnp.sum(tail_part, axis=0, keepdims=True) + el_x[:, gs] * jnp.sum(dsg * sg, axis=0, keepdims=True)
            lastv[:, gs] = jnp.broadcast_to(last, (8, GW))
            dcb = jnp.zeros((L, L), F32)
            for r in range(SSM_R):
                h = g * SSM_R + r
                hs = slice(h * SSM_P, (h + 1) * SSM_P)
                seg = cs[:, h:h + 1] - cst[h:h + 1, :]
                dec = jnp.exp(jnp.where(lower, seg, -1e30))
                m = cb * dec
                dm = _dot(dyb[:, hs], xdb[:, hs], NT)
                dcb = dcb + dm * dec
                e = dm * m
                row_part = row_part + jnp.where(lane == h, jnp.sum(e, axis=1, keepdims=True), 0.0)
                col_part = col_part + jnp.where(sub == h, jnp.sum(e, axis=0, keepdims=True), 0.0)
                dxd[:, hs] = _dot(m, dyb[:, hs], TN) + dx_state[:, r * SSM_P:(r + 1) * SSM_P]
            dx_ref[:, bcol] = db + _dot(dcb, cg, TN)
            dx_ref[:, ccol] = dc + _dot(dcb, bg)
            dstate[:, gs] = dsg * el_x[:, gs] + _dot(cg, dys[:, gs], TN)
        dxv = dxd[...]
        dx_ref[:, 0:D_INNER] = dxv * dt_x + dyv * sk_ref[...]
        ddt_x = _dotx(dxv * xs, ex_t)
        yst = _dotx(yd[...], ex_t)
        lst = _dotx(lastv[...], ex_t)[0:1, :]
        rows = lax.broadcasted_iota(jnp.int32, (L, LANE), 0)
        dcs = row_part - col_part.T + yst + jnp.where(rows == L - 1, lst, 0.0)
        li = lax.broadcasted_iota(jnp.int32, (L, L), 0)
        si = lax.broadcasted_iota(jnp.int32, (L, L), 1)
        upper = jnp.where(si >= li, 1.0, 0.0).astype(BF16)
        dla = _xdot(upper, dcs)
        ddt_ref[...] = dla * a + ddt_x
        da_ref[...] += jnp.sum(dla * dtv, axis=0, keepdims=True)

    def row(w):
        return pl.BlockSpec((L, w), lambda b, c: (b * nc + nc - 1 - c, 0))

    return _pallas(
        body, name="ssd_bwd", grid=(nb, nc),
        in_specs=[row(XBC), row(LANE), pl.BlockSpec((1, LANE), lambda b, c: (0, 0)),
                  pl.BlockSpec((SSM_H, D_INNER), lambda b, c: (0, 0)),
                  pl.BlockSpec((D_INNER, LANE), lambda b, c: (0, 0)),
                  pl.BlockSpec((1, SSM_N, D_INNER), lambda b, c: (b * nc + nc - 1 - c, 0, 0)),
                  row(D_INNER), pl.BlockSpec((1, D_INNER), lambda b, c: (0, 0))],
        out_specs=[row(XBC), row(LANE), pl.BlockSpec((1, LANE), lambda b, c: (0, 0))],
        out_shape=[jax.ShapeDtypeStruct((t, XBC), F32), jax.ShapeDtypeStruct((t, LANE), F32),
                   jax.ShapeDtypeStruct((1, LANE), F32)],
        scratch_shapes=[pltpu.VMEM((SSM_N, D_INNER), F32), pltpu.VMEM((L, D_INNER), F32),
                        pltpu.VMEM((L, D_INNER), F32), pltpu.VMEM((8, D_INNER), F32)],
        compiler_params=_params(2),
    )(xbc_c, dt, a_log, ex, ext, states, dy, d_x)


def _group_rms(y2):
    gw = D_INNER // SSM_G
    parts = []
    for g in range(SSM_G):
        v = y2[:, g * gw:(g + 1) * gw]
        r = lax.rsqrt(jnp.mean(v * v, axis=1, keepdims=True) + EPS)
        parts.append(jnp.broadcast_to(r, v.shape))
    return jnp.concatenate(parts, axis=1)


def _gate_fwd(y, xbc_c, proj, d_x, gn_w, *, tm=256):
    t = y.shape[0]

    def body(y_ref, x_ref, z_ref, d_ref, w_ref, o_ref):
        y1 = y_ref[...] + d_ref[...] * x_ref[...]
        zv = z_ref[...]
        y2 = y1 * zv * _sigmoid(zv)
        o_ref[...] = (y2 * _group_rms(y2) * w_ref[...]).astype(BF16)

    row = pl.BlockSpec((tm, D_INNER), lambda i: (i, 0))
    vec = pl.BlockSpec((1, D_INNER), lambda i: (0, 0))
    return _pallas(
        body, name="gate_fwd", grid=(t // tm,), in_specs=[row, row, row, vec, vec], out_specs=row,
        out_shape=jax.ShapeDtypeStruct((t, D_INNER), BF16), compiler_params=_params(1),
    )(y, xbc_c, proj, d_x, gn_w)


def _gate_bwd(dyg, y, xbc_c, proj, d_x, gn_w, *, tm=256):
    t = y.shape[0]
    gw = D_INNER // SSM_G

    def body(dg_ref, y_ref, x_ref, z_ref, d_ref, w_ref, dy_ref, dz_ref, dw_ref, dd_ref):
        i = pl.program_id(0)
        xv = x_ref[...]
        dxv = d_ref[...]
        y1 = y_ref[...] + dxv * xv
        zv = z_ref[...]
        sz = _sigmoid(zv)
        y2 = y1 * zv * sz
        rr = _group_rms(y2)
        xh = y2 * rr
        dg = _f32(dg_ref[...])
        gq = dg * w_ref[...]
        prod = gq * xh
        means = []
        for g in range(SSM_G):
            mg = jnp.mean(prod[:, g * gw:(g + 1) * gw], axis=1, keepdims=True)
            means.append(jnp.broadcast_to(mg, (tm, gw)))
        dy2 = rr * (gq - xh * jnp.concatenate(means, axis=1))
        dy1 = dy2 * zv * sz
        dy_ref[...] = dy1
        dz_ref[...] = (dy2 * y1 * _dsilu(zv, sz)).astype(BF16)

        @pl.when(i == 0)
        def _():
            dw_ref[...] = jnp.zeros_like(dw_ref)
            dd_ref[...] = jnp.zeros_like(dd_ref)

        dw_ref[...] += jnp.sum(dg * xh, axis=0, keepdims=True)
        dd_ref[...] += jnp.sum(dy1 * xv, axis=0, keepdims=True)

    row = pl.BlockSpec((tm, D_INNER), lambda i: (i, 0))
    vec = pl.BlockSpec((1, D_INNER), lambda i: (0, 0))
    return _pallas(
        body, name="gate_bwd", grid=(t // tm,), in_specs=[row, row, row, row, vec, vec],
        out_specs=[row, row, vec, vec],
        out_shape=[jax.ShapeDtypeStruct((t, D_INNER), F32),
                   jax.ShapeDtypeStruct((t, D_INNER), BF16), jax.ShapeDtypeStruct((1, D_INNER), F32),
                   jax.ShapeDtypeStruct((1, D_INNER), F32)],
        compiler_params=_params(1),
    )(dyg, y, xbc_c, proj, d_x, gn_w)


ANY = pl.BlockSpec(memory_space=pl.ANY)


def _remote(src, dst, sems, k, to):
    send_sems, recv_sems = sems
    return pltpu.make_async_remote_copy(src_ref=src, dst_ref=dst, send_sem=send_sems.at[k], recv_sem=recv_sems.at[k],
                                        device_id=to, device_id_type=MESH)


NCHIP = 4


def _gathered_shape(kind, shard):
    r, n = shard.shape
    shape = {"cols": (r, NCHIP * n), "slab": (NCHIP, r, n), "rows": (NCHIP * r, n)}[kind]
    return jax.ShapeDtypeStruct(shape, shard.dtype)


def _gather_plan(kinds, shards, outs, sems, small=None):
    ici_s, ici_r, d2d_s, d2d_r = sems
    nw = len(shards)
    per = nw + (small is not None)

    def place():
        x, y, c = lax.axis_index("x"), lax.axis_index("y"), lax.axis_index("c")
        return 2 * x + y, c, (x, y, 1 - c), [(1 - x, y), (x, 1 - y), (1 - x, 1 - y)]

    def region(j, chip, half):
        r, n = shards[j].shape
        h = r // 2
        if kinds[j] == "cols":
            return outs[j].at[pl.ds(half * h, h), pl.ds(pl.multiple_of(chip * n, LANE), n)]
        if kinds[j] == "slab":
            return outs[j].at[chip, pl.ds(half * h, h), :]
        return outs[j].at[pl.ds(chip * r + half * h, h), :]

    def my_sends(me, c, peers):
        cps = []
        for k, (px, py) in enumerate(peers):
            for j in range(nw):
                h = shards[j].shape[0] // 2
                cps.append(_remote(shards[j].at[pl.ds(c * h, h), :], region(j, me, c), (ici_s, ici_r), per * k + j, (px, py, c)))
            if small is not None:
                cps.append(_remote(small[0], small[1].at[me], (ici_s, ici_r), per * k + nw, (px, py, c)))
        return cps

    def start():
        me, c, _, peers = place()
        for cp in my_sends(me, c, peers):
            cp.start()

    def finish():
        me, c, sib, peers = place()
        fwds = []
        for k, (px, py) in enumerate(peers):
            q = 2 * px + py
            for j in range(nw):
                d = region(j, q, c)
                _remote(d, d, (ici_s, ici_r), per * k + j, (px, py, c)).wait_recv()
                fwds.append(_remote(d, d, (d2d_s, d2d_r), nw * k + j, sib))
                fwds[-1].start()
            if small is not None:
                _remote(small[0], small[1].at[q], (ici_s, ici_r), per * k + nw, (px, py, c)).wait_recv()
        for k, (px, py) in enumerate(peers):
            for j in range(nw):
                d = region(j, 2 * px + py, 1 - c)
                _remote(d, d, (d2d_s, d2d_r), nw * k + j, sib).wait_recv()
        for cp in my_sends(me, c, peers) + fwds:
            cp.wait_send()

    return start, finish


def _gather_sems(nw, with_small):
    n_ici = 3 * (nw + with_small)
    return [pltpu.SemaphoreType.DMA((n_ici,)), pltpu.SemaphoreType.DMA((n_ici,)),
            pltpu.SemaphoreType.DMA((3 * nw,)), pltpu.SemaphoreType.DMA((3 * nw,))]


def _gather_shards(kinds, shards, small):
    nw = len(shards)

    def body(*refs):
        ins, sm, outs, osm, sems = refs[:nw], refs[nw], refs[nw + 1:2 * nw + 1], refs[2 * nw + 1], refs[2 * nw + 2:]
        start, finish = _gather_plan(kinds, ins, outs, sems, small=(sm, osm))
        start()
        finish()

    return _pallas(
        body, name="gather_shards", in_specs=[ANY] * (nw + 1), out_specs=[ANY] * (nw + 1),
        out_shape=[_gathered_shape(kd, s) for kd, s in zip(kinds, shards)]
        + [jax.ShapeDtypeStruct((NCHIP,) + small.shape, small.dtype)],
        scratch_shapes=_gather_sems(nw, 1),
    )(*shards, small)


def _matmul_with_gather(a, b, kinds, shards, *, out_dtype, bm, bn, name):
    (m, k), n = a.shape, b.shape[1]
    nw = len(shards)
    nj, ni = n // bn, m // bm

    def body(*refs):
        a_ref, b_ref, ins, o_ref = refs[0], refs[1], refs[2:2 + nw], refs[2 + nw]
        outs, sems = refs[3 + nw:3 + 2 * nw], refs[3 + 2 * nw:]
        start, finish = _gather_plan(kinds, ins, outs, sems)
        j, i = pl.program_id(0), pl.program_id(1)

        @pl.when(jnp.logical_and(j == 0, i == 0))
        def _():
            start()

        o_ref[...] = _dot(a_ref[...], b_ref[...]).astype(out_dtype)

        @pl.when(jnp.logical_and(j == nj - 1, i == ni - 1))
        def _():
            finish()

    return _pallas(
        body, name=name, grid=(nj, ni),
        in_specs=[pl.BlockSpec((bm, k), lambda j, i: (i, 0)), pl.BlockSpec((k, bn), lambda j, i: (0, j))] + [ANY] * nw,
        out_specs=[pl.BlockSpec((bm, bn), lambda j, i: (i, j))] + [ANY] * nw,
        out_shape=[jax.ShapeDtypeStruct((m, n), out_dtype)] + [_gathered_shape(kd, s) for kd, s in zip(kinds, shards)],
        scratch_shapes=_gather_sems(nw, 0), compiler_params=_params(2),
    )(a, b, *shards)


def _other_half(a, c):
    axis = a.ndim - 2
    h = a.shape[axis] // 2
    rows = pl.ds(pl.multiple_of((1 - c) * h, 8), h)
    return a.at[rows, :] if a.ndim == 2 else a.at[:, rows, :]


def _half_shape(a):
    axis = a.ndim - 2
    return jax.ShapeDtypeStruct(a.shape[:axis] + (a.shape[axis] // 2,) + a.shape[axis + 1:], a.dtype)


def _pair_swap(bigs, *, name):
    nb = len(bigs)

    def body(*refs):
        ins, outs, send_sems, recv_sems = refs[:nb], refs[nb:2 * nb], refs[2 * nb], refs[2 * nb + 1]
        x, y, c = lax.axis_index("x"), lax.axis_index("y"), lax.axis_index("c")
        pair = [_remote(_other_half(a, c), q, (send_sems, recv_sems), j, (x, y, 1 - c))
                for j, (a, q) in enumerate(zip(ins, outs))]
        for cp in pair:
            cp.start()
        for cp in pair:
            cp.wait()

    return _pallas(
        body, name=name, in_specs=[ANY] * nb, out_specs=[ANY] * nb, out_shape=[_half_shape(a) for a in bigs],
        scratch_shapes=[pltpu.SemaphoreType.DMA((nb,)), pltpu.SemaphoreType.DMA((nb,))],
    )(*bigs)


def _pair_exchange(bigs, gsmall, grep):
    nb = len(bigs)

    def body(*refs):
        ins, sm, rp = refs[:nb], refs[nb], refs[nb + 1]
        outs, osm, orp = refs[nb + 2:2 * nb + 2], refs[2 * nb + 2], refs[2 * nb + 3]
        pair_s, pair_r, send_sems, recv_sems, local_sems = refs[2 * nb + 4:]
        x, y, c = lax.axis_index("x"), lax.axis_index("y"), lax.axis_index("c")
        me = 4 * x + 2 * y + c
        chip = 2 * x + y
        sib = (x, y, 1 - c)
        pair = [_remote(_other_half(a, c), q, (pair_s, pair_r), j, sib) for j, (a, q) in enumerate(zip(ins, outs))]
        for cp in pair:
            cp.start()
        own = [pltpu.make_async_copy(sm.at[chip], osm.at[me], local_sems.at[0]),
               pltpu.make_async_copy(rp, orp.at[me], local_sems.at[1])]
        for cp in own:
            cp.start()
        peers = []
        for k in range(7):
            fx, fy, fc = ((k + 1) >> 2) & 1, ((k + 1) >> 1) & 1, (k + 1) & 1
            peers.append((1 - x if fx else x, 1 - y if fy else y, 1 - c if fc else c))
        sends = []
        for k, (px, py, pc) in enumerate(peers):
            sends.append(_remote(sm.at[2 * px + py], osm.at[me], (send_sems, recv_sems), 2 * k, (px, py, pc)))
            sends.append(_remote(rp, orp.at[me], (send_sems, recv_sems), 2 * k + 1, (px, py, pc)))
        for cp in sends:
            cp.start()
        for k, (px, py, pc) in enumerate(peers):
            slot = 4 * px + 2 * py + pc
            _remote(sm.at[chip], osm.at[slot], (send_sems, recv_sems), 2 * k, (px, py, pc)).wait_recv()
            _remote(rp, orp.at[slot], (send_sems, recv_sems), 2 * k + 1, (px, py, pc)).wait_recv()
        for cp in pair:
            cp.wait_recv()
        for cp in pair + sends:
            cp.wait_send()
        for cp in own:
            cp.wait()

    return _pallas(
        body, name="pair_exchange", in_specs=[ANY] * (nb + 2), out_specs=[ANY] * (nb + 2),
        out_shape=[_half_shape(a) for a in bigs]
        + [jax.ShapeDtypeStruct((8,) + gsmall.shape[1:], F32), jax.ShapeDtypeStruct((8,) + grep.shape, F32)],
        scratch_shapes=[pltpu.SemaphoreType.DMA((max(nb, 1),)), pltpu.SemaphoreType.DMA((max(nb, 1),)),
                        pltpu.SemaphoreType.DMA((14,)), pltpu.SemaphoreType.DMA((14,)),
                        pltpu.SemaphoreType.DMA((2,))],
    )(*bigs, gsmall, grep)


def _core_index():
    return lax.axis_index("c").astype(jnp.int32).reshape(1)


def _half_add(full, other, *, axis, block, name):
    nd = full.ndim
    nblk = other.shape[axis] // block[axis]
    grid = tuple(other.shape[d] // block[d] for d in range(nd))

    def body(c_ref, f_ref, o_ref, out_ref):
        out_ref[...] = (f_ref[...] + o_ref[...]).astype(BF16)

    def full_map(*idx):
        ids, c_ref = list(idx[:nd]), idx[nd]
        ids[axis] = ids[axis] + c_ref[0] * nblk
        return tuple(ids)

    def plain_map(*idx):
        return tuple(idx[:nd])

    return _pallas(
        body, name=name,
        grid_spec=pltpu.PrefetchScalarGridSpec(
            num_scalar_prefetch=1, grid=grid,
            in_specs=[pl.BlockSpec(block, full_map), pl.BlockSpec(block, plain_map)],
            out_specs=pl.BlockSpec(block, plain_map)),
        out_shape=jax.ShapeDtypeStruct(other.shape, BF16), compiler_params=_params(nd),
    )(_core_index(), full, other)


NPEER = 3


def _landing_shape(kind, src):
    if kind == "cols":
        return jax.ShapeDtypeStruct((NPEER, src.shape[0], src.shape[1] // NCHIP), src.dtype)
    return jax.ShapeDtypeStruct((NPEER,) + src.shape[1:], src.dtype)


def _chip_plan(kinds, srcs, lands, sems):
    nw = len(srcs)

    def place():
        x, y, c = lax.axis_index("x"), lax.axis_index("y"), lax.axis_index("c")
        return 2 * x + y, c, [(1 - x, y), (x, 1 - y), (1 - x, 1 - y)]

    def piece(j, chip):
        if kinds[j] == "cols":
            n = srcs[j].shape[1] // NCHIP
            return srcs[j].at[:, pl.ds(pl.multiple_of(chip * n, LANE), n)]
        return srcs[j].at[chip]

    def my_sends(c, peers):
        return [_remote(piece(j, 2 * px + py), lands[j].at[k], sems, nw * k + j, (px, py, c))
                for k, (px, py) in enumerate(peers) for j in range(nw)]

    def start():
        _, c, peers = place()
        for cp in my_sends(c, peers):
            cp.start()

    def finish():
        me, c, peers = place()
        for k, (px, py) in enumerate(peers):
            for j in range(nw):
                _remote(piece(j, me), lands[j].at[k], sems, nw * k + j, (px, py, c)).wait_recv()
        for cp in my_sends(c, peers):
            cp.wait_send()

    return start, finish


def _chip_sems(nw):
    return [pltpu.SemaphoreType.DMA((NPEER * nw,)), pltpu.SemaphoreType.DMA((NPEER * nw,))]


def _chip_exchange(kinds, srcs):
    nw = len(srcs)

    def body(*refs):
        start, finish = _chip_plan(kinds, refs[:nw], refs[nw:2 * nw], refs[2 * nw:])
        start()
        finish()

    return _pallas(
        body, name="chip_exchange", in_specs=[ANY] * nw, out_specs=[ANY] * nw,
        out_shape=[_landing_shape(kd, s) for kd, s in zip(kinds, srcs)], scratch_shapes=_chip_sems(nw),
    )(*srcs)


def _chip_index():
    return (2 * lax.axis_index("x") + lax.axis_index("y")).astype(jnp.int32).reshape(1)


def _chip_sum(own, slots, *, own_block, own_map, block, name):
    npeer = slots.shape[0]
    shape = slots.shape[1:]
    grid = (shape[0] // block[0], shape[1] // block[1])

    def body(p_ref, own_ref, s_ref, o_ref):
        acc = own_ref[...].reshape(block).astype(F32)
        for q in range(npeer):
            acc = acc + s_ref[q].astype(F32)
        o_ref[...] = acc

    return _pallas(
        body, name=name,
        grid_spec=pltpu.PrefetchScalarGridSpec(
            num_scalar_prefetch=1, grid=grid,
            in_specs=[pl.BlockSpec(own_block, own_map),
                      pl.BlockSpec((npeer,) + block, lambda i, j, p: (0, i, j))],
            out_specs=pl.BlockSpec(block, lambda i, j, p: (i, j))),
        out_shape=jax.ShapeDtypeStruct(shape, F32), compiler_params=_params(2),
    )(_chip_index(), own, slots)


def _pair_share(r_in0, r_in1, r_out0, r_out1):
    def body(a0, a1, b0, b1, g0, g1, h0, h1, send_sems, recv_sems):
        x, y, c = lax.axis_index("x"), lax.axis_index("y"), lax.axis_index("c")
        sib = (x, y, 1 - c)
        sends = [_remote(s, d, (send_sems, recv_sems), j, sib)
                 for j, (s, d) in enumerate(zip([a0, a1, b0, b1], [g0, g1, h0, h1]))]
        for cp in sends:
            cp.start()
        for cp in sends:
            cp.wait()

    return _pallas(
        body, name="pair_share", in_specs=[ANY] * 4, out_specs=[ANY] * 4,
        out_shape=[jax.ShapeDtypeStruct(r.shape, F32) for r in (r_in0, r_in1, r_out0, r_out1)],
        scratch_shapes=[pltpu.SemaphoreType.DMA((4,)), pltpu.SemaphoreType.DMA((4,))],
    )(r_in0, r_in1, r_out0, r_out1)


def _adam_math(g, w, m, v):
    c1 = 1.0 - ADAM_B1 ** ADAM_STEP
    c2 = 1.0 - ADAM_B2 ** ADAM_STEP
    m2 = ADAM_B1 * m + (1.0 - ADAM_B1) * g
    v2 = ADAM_B2 * v + (1.0 - ADAM_B2) * (g * g)
    delta = -ADAM_LR * ((m2 / c1) / (jnp.sqrt(v2 / c2) + ADAM_EPS) + ADAM_WD * w)
    return delta, m2, v2


def _adamw_nat(g_mine, g_sib, w, m, v, *, name, tr):
    rows, cw = w.shape
    nt = g_mine.shape[0] // tr

    def body(c_ref, gm_ref, gs_ref, w_ref, m_ref, v_ref, go_ref, d_ref, nm_ref, nv_ref):
        mine = pl.program_id(0) // nt == c_ref[0]
        gv = jnp.where(mine, gm_ref[...], gs_ref[...])[:, 0:cw]
        delta, m2, v2 = _adam_math(gv, w_ref[...], m_ref[...], v_ref[...])
        go_ref[...] = gv
        d_ref[...] = delta
        nm_ref[...] = m2
        nv_ref[...] = v2

    def mine_map(i, c_ref):
        return (jnp.where(i // nt == c_ref[0], i % nt, 0), 0)

    def sib_map(i, c_ref):
        return (jnp.where(i // nt == c_ref[0], 0, i % nt), 0)

    row = pl.BlockSpec((tr, cw), lambda i, c_ref: (i, 0))
    gspec = (tr, g_mine.shape[1])
    out = jax.ShapeDtypeStruct((rows, cw), F32)
    return _pallas(
        body, name=name,
        grid_spec=pltpu.PrefetchScalarGridSpec(
            num_scalar_prefetch=1, grid=(rows // tr,),
            in_specs=[pl.BlockSpec(gspec, mine_map), pl.BlockSpec(gspec, sib_map), row, row, row],
            out_specs=[row, row, row, row]),
        out_shape=[out, out, out, out], compiler_params=_params(1),
    )(_core_index(), g_mine, g_sib, w, m, v)


def _adamw(slots, w, m, v, *, name, tr):
    nd, rows, _ = slots.shape
    c1 = 1.0 - ADAM_B1 ** ADAM_STEP
    c2 = 1.0 - ADAM_B2 ** ADAM_STEP

    def body(s_ref, w_ref, m_ref, v_ref, g_ref, d_ref, nm_ref, nv_ref):
        g = s_ref[0]
        for d in range(1, nd):
            g = g + s_ref[d]
        m2 = ADAM_B1 * m_ref[...] + (1.0 - ADAM_B1) * g
        v2 = ADAM_B2 * v_ref[...] + (1.0 - ADAM_B2) * (g * g)
        g_ref[...] = g
        nm_ref[...] = m2
        nv_ref[...] = v2
        d_ref[...] = -ADAM_LR * ((m2 / c1) / (jnp.sqrt(v2 / c2) + ADAM_EPS) + ADAM_WD * w_ref[...])

    row = pl.BlockSpec((tr, LANE), lambda i: (i, 0))
    out = jax.ShapeDtypeStruct((rows, LANE), F32)
    return _pallas(
        body, name=name, grid=(rows // tr,),
        in_specs=[pl.BlockSpec((nd, tr, LANE), lambda i: (0, i, 0)), row, row, row],
        out_specs=[row, row, row, row], out_shape=[out, out, out, out], compiler_params=_params(1),
    )(slots, w, m, v)


def _rows(a):
    return a.reshape(-1, LANE)


def _pad_rows(a, mult):
    pad = (-a.shape[0]) % mult
    return jnp.pad(a, ((0, pad), (0, 0))) if pad else a


def _pack(parts, mult):
    return _pad_rows(jnp.concatenate([_rows(p) for p in parts], axis=0), mult)


def _unpack(slab, shapes):
    out, r0 = [], 0
    for shp in shapes:
        n = 1
        for s in shp:
            n *= s
        r = n // LANE
        out.append(slab[r0:r0 + r].reshape(shp))
        r0 += r
    return out


def _pack_rep(vecs, scal):
    srow = jnp.concatenate([s.reshape(-1) for s in scal] + [jnp.zeros((LANE - 3 * SSM_H,), F32)]).reshape(1, LANE)
    return _pad_rows(jnp.concatenate([_rows(vv) for vv in vecs] + [srow], axis=0), 8)


def _unpack_rep(slab, vec_shapes, scal_shape):
    vecs, r0 = [], 0
    for shp in vec_shapes:
        vecs.append(slab[r0:r0 + 8].reshape(shp))
        r0 += 8
    srow = slab[r0]
    scal = [srow[i * SSM_H:(i + 1) * SSM_H].reshape(scal_shape) for i in range(3)]
    return vecs, scal


def kernel(x, ev_norm_w, ev_w_in, ev_dw_w, ev_dw_b, ev_ln_w, ev_ln_b, ev_w_out, od_norm_w, od_w_in, od_conv_w, od_conv_b, od_dt_bias, od_a_log, od_d, od_gnorm_w, od_w_out, final_norm_w, loss_target, m_ev_norm_w, m_ev_w_in, m_ev_dw_w, m_ev_dw_b, m_ev_ln_w, m_ev_ln_b, m_ev_w_out, m_od_norm_w, m_od_w_in, m_od_conv_w, m_od_conv_b, m_od_dt_bias, m_od_a_log, m_od_d, m_od_gnorm_w, m_od_w_out, m_final_norm_w, v_ev_norm_w, v_ev_w_in, v_ev_dw_w, v_ev_dw_b, v_ev_ln_w, v_ev_ln_b, v_ev_w_out, v_od_norm_w, v_od_w_in, v_od_conv_w, v_od_conv_b, v_od_dt_bias, v_od_a_log, v_od_d, v_od_gnorm_w, v_od_w_out, v_final_norm_w):
    nb, seq, d = x.shape
    t = nb * seq
    nchip = 4
    xf = x.reshape(t, d)
    tgt = loss_target.reshape(t, d)

    big_w = [ev_w_in[0], od_w_in[0], ev_w_out[0], od_w_out[0]]
    small_w = [ev_dw_w[0], od_norm_w[0], od_conv_w[0], od_conv_b[0], od_gnorm_w[0]]
    small_shapes = [a.shape for a in small_w]
    big_b = [a.astype(BF16) for a in big_w]
    small_slab = _pack(small_w, 8)
    w_in0, w_out0, gath_small = _gather_shards(("cols", "rows"), [big_b[0], big_b[2]], small_slab)
    chip = 2 * lax.axis_index("x") + lax.axis_index("y")
    w_in0 = lax.dynamic_update_slice(w_in0, big_b[0], (0, chip * big_b[0].shape[1]))
    w_out0 = lax.dynamic_update_slice(w_out0, big_b[2], (chip * big_b[2].shape[0], 0))
    gath_small = lax.dynamic_update_slice(gath_small, small_slab[None], (chip, 0, 0))
    per_chip = [_unpack(gath_small[p], small_shapes) for p in range(nchip)]

    def cat(idx, axis):
        return jnp.concatenate([per_chip[p][idx] for p in range(nchip)], axis=axis)

    dw_w = jnp.pad(cat(0, 1), ((0, HALO - CONF_K), (0, 0)))
    dw_w8 = jnp.repeat(dw_w, SUB, axis=0)
    n1_w = cat(1, 0).reshape(1, d)
    conv_w = jnp.pad(cat(2, 1), ((0, PH - SSM_K), (0, 0)))
    conv_b = cat(3, 0).reshape(1, XBC)
    gn_w = cat(4, 0).reshape(1, D_INNER)

    def lanes(a):
        return jnp.pad(a.reshape(1, -1), ((0, 0), (0, LANE - a.size)))

    dt_bias, a_log = lanes(od_dt_bias), lanes(od_a_log)
    d_x = jnp.repeat(od_d.reshape(-1), SSM_P).reshape(1, D_INNER)
    hid = lax.broadcasted_iota(jnp.int32, (SSM_H, D_INNER), 1) // SSM_P
    ex = (hid == lax.broadcasted_iota(jnp.int32, (SSM_H, D_INNER), 0)).astype(BF16)
    ex_t = jnp.pad(ex.T, ((0, 0), (0, LANE - SSM_H)))
    fn_w = final_norm_w.reshape(1, d)

    n0 = _rms_fwd(xf, ev_norm_w, name="rms_fwd0")
    proj0, w_in1g, w_out1 = _matmul_with_gather(n0, w_in0, ("slab", "rows"), [big_b[1], big_b[3]],
                                                out_dtype=BF16, bm=512, bn=1024, name="in_proj0")
    w_in1g = lax.dynamic_update_slice(w_in1g, big_b[1][None], (chip, 0, 0))
    w_out1 = lax.dynamic_update_slice(w_out1, big_b[3], (chip * big_b[3].shape[0], 0))
    w_in1 = jnp.pad(jnp.concatenate([w_in1g[p] for p in range(nchip)], axis=1),
                    ((0, 0), (0, IN_ODD_PAD - IN_ODD)))
    y_conv, u2 = _conf_fwd(proj0, dw_w8, ev_dw_b, ev_ln_w, ev_ln_b, seq)
    o_att, y_att = _sba_fwd(proj0, nb, seq)
    ycat0 = jnp.concatenate([y_conv, y_att], axis=1)
    h1 = _matmul(ycat0, w_out0, mode="nn", out_dtype=F32, bm=512, bn=d, bk=D_INNER, name="out_proj0", residual=xf)
    n1 = _rms_fwd(h1, n1_w, name="rms_fwd1")
    proj1 = _matmul(n1, w_in1, mode="nn", out_dtype=F32, bm=512, bn=768, bk=d, name="in_proj1", n_major=True)
    xbc_c = _xconv_fwd(proj1, conv_w, conv_b, seq)
    dt = _dt_fwd(proj1, dt_bias)
    y_ssd, states = _ssd_fwd(xbc_c, dt, a_log, ex, nb, seq)
    yg = _gate_fwd(y_ssd, xbc_c, proj1, d_x, gn_w)
    h2 = _matmul(yg, w_out1, mode="nn", out_dtype=F32, bm=512, bn=d, bk=D_INNER, name="out_proj1", residual=h1)
    dh2, g_fn, loss_part = _final_loss(h2, fn_w, tgt)

    dyg = _matmul(dh2, w_out1, mode="nt", out_dtype=BF16, bm=512, bn=1024, bk=d, name="d_out_proj1")
    g_w_out1 = _matmul(yg, dh2, mode="tn", out_dtype=F32, bm=1024, bn=d, bk=1024, name="dw_out_proj1")
    dy_ssd, dz, g_gn, g_dx = _gate_bwd(dyg, y_ssd, xbc_c, proj1, d_x, gn_w)
    dxbc_c, ddt, g_a = _ssd_bwd(xbc_c, dt, a_log, ex, ex_t, states, dy_ssd, d_x, nb, seq)
    dxbc, g_conv_w, g_conv_b = _xconv_bwd(proj1, dxbc_c, conv_w, conv_b, seq)
    ddt_raw, g_dt_bias = _dt_bwd(proj1, dt_bias, ddt)
    dproj1 = jnp.concatenate([dz, dxbc, ddt_raw.astype(BF16),
                              jnp.zeros((t, IN_ODD_PAD - IN_ODD - (LANE - SSM_H)), BF16)], axis=1)
    dn1 = _matmul(dproj1, w_in1, mode="nt", out_dtype=BF16, bm=1024, bn=d, bk=1792, name="d_in_proj1")
    g_w_in1 = _matmul(n1, dproj1, mode="tn", out_dtype=F32, bm=d, bn=1792, bk=1024, name="dw_in_proj1")
    dh1, g_n1 = _rms_bwd(dn1, h1, n1_w, dh2, name="rms_bwd1")

    dycat0 = _matmul(dh1, w_out0, mode="nt", out_dtype=BF16, bm=512, bn=1024, bk=d, name="d_out_proj0")
    g_w_out0 = _matmul(ycat0, dh1, mode="tn", out_dtype=F32, bm=1024, bn=d, bk=1024, name="dw_out_proj0")
    dq, dk, dv, dga = _sba_bwd(proj0, o_att, dycat0, nb, seq)
    ro = D_INNER // nchip
    n1 = IN_ODD // nchip
    n1p = -(-n1 // LANE) * LANE
    g_w_out1c = g_w_out1.reshape(nchip, ro, d)
    q_in1, q_out1 = _pair_swap([g_w_in1, g_w_out1c], name="pair_swap_l1")
    s_in1n = _half_add(g_w_in1, q_in1, axis=0, block=(128, IN_ODD_PAD), name="half_add_in1")
    s_in1 = jnp.stack([jnp.pad(s_in1n[:, p * n1:(p + 1) * n1], ((0, 0), (0, n1p - n1))) for p in range(nchip)])
    s_out1 = _half_add(g_w_out1c, q_out1, axis=1, block=(1, ro // 2, d), name="half_add_out1")
    dpc, g_dw_w, g_dw_b, g_ln_w, g_ln_b, l_in1, l_out1 = _conf_bwd(
        proj0, u2, dycat0, dw_w8, ev_ln_w, ev_ln_b, seq, ("slab", "slab"), [s_in1, s_out1])
    dproj0 = jnp.concatenate([dpc, dq, dk.astype(BF16), dv.astype(BF16), dga], axis=1)
    g_w_in0 = _matmul(n0, dproj0, mode="tn", out_dtype=F32, bm=d, bn=1792, bk=1024, name="dw_in_proj0")
    g_w_out0c = g_w_out0.reshape(nchip, ro, d)
    q_in0, q_out0 = _pair_swap([g_w_in0, g_w_out0c], name="pair_swap_l0")
    s_in0 = _half_add(g_w_in0, q_in0, axis=0, block=(128, IN_EVEN), name="half_add_in0")
    s_out0 = _half_add(g_w_out0c, q_out0, axis=1, block=(1, ro // 2, d), name="half_add_out0")
    dn0, l_in0, l_out0 = _matmul(dproj0, w_in0, mode="nt", out_dtype=BF16, bm=1024, bn=d, bk=1792, name="d_in_proj0",
                                 comm_kinds=("cols", "slab"), comm_srcs=[s_in0, s_out0])
    grad_x, g_n0 = _rms_bwd(dn0, xf, ev_norm_w, dh1, name="rms_bwd0")

    g_dw_w = g_dw_w.reshape(HALO, SUB, CONV_W).sum(axis=1)[0:CONF_K]
    g_dw_b, g_ln_w, g_ln_b = (a.sum(axis=0, keepdims=True) for a in (g_dw_b, g_ln_w, g_ln_b))
    g_conv_w = g_conv_w.reshape(PH, SUB, XBC).sum(axis=1)[0:SSM_K]
    g_conv_b = g_conv_b.sum(axis=0, keepdims=True)
    a_neg = -jnp.exp(od_a_log.reshape(-1))
    g_a_log = g_a[0, 0:SSM_H] * a_neg
    g_d = g_dx.reshape(SSM_H, SSM_P).sum(axis=1)

    def chip_slab_small(p):
        c0, c1, c2, c3 = CONV_W // nchip, d // nchip, XBC // nchip, D_INNER // nchip
        return _pack([g_dw_w[:, p * c0:(p + 1) * c0], g_n1[0, p * c1:(p + 1) * c1],
                      g_conv_w[:, p * c2:(p + 1) * c2], g_conv_b[0, p * c2:(p + 1) * c2],
                      g_gn[0, p * c3:(p + 1) * c3]], 8)

    gsmall = jnp.stack([chip_slab_small(p) for p in range(nchip)])
    rep_vec_shapes = [ev_norm_w.shape, ev_dw_b.shape, ev_ln_w.shape, ev_ln_b.shape, final_norm_w.shape]
    grep = _pack_rep([g_n0, g_dw_b, g_ln_w, g_ln_b, g_fn], [g_dt_bias[0, 0:SSM_H], g_a_log, g_d])

    ssmall, srep = _pair_exchange([], gsmall, grep)
    r_in0 = _chip_sum(s_in0, l_in0, own_block=(128, IN_EVEN // nchip), own_map=lambda i, j, p: (i, p[0]),
                      block=(128, IN_EVEN // nchip), name="chip_sum_in0")
    r_in1 = _chip_sum(s_in1, l_in1, own_block=(1, 256, n1p), own_map=lambda i, j, p: (p[0], i, 0),
                      block=(256, n1p), name="chip_sum_in1")
    r_out0 = _chip_sum(s_out0, l_out0, own_block=(1, ro // 2, d), own_map=lambda i, j, p: (p[0], 0, 0),
                       block=(ro // 2, d), name="chip_sum_out0")
    r_out1 = _chip_sum(s_out1, l_out1, own_block=(1, ro // 2, d), own_map=lambda i, j, p: (p[0], 0, 0),
                       block=(ro // 2, d), name="chip_sum_out1")
    big_r = [r_in0, r_in1, r_out0, r_out1]
    big_q = _pair_share(*big_r)

    big_m = [m_ev_w_in[0], m_od_w_in[0], m_ev_w_out[0], m_od_w_out[0]]
    big_v = [v_ev_w_in[0], v_od_w_in[0], v_ev_w_out[0], v_od_w_out[0]]
    big_names = ["adamw_in0", "adamw_in1", "adamw_out0", "adamw_out1"]
    out_bigs = [_adamw_nat(gm, gs, w, m, v, name=nm, tr=128)
                for gm, gs, w, m, v, nm in zip(big_r, big_q, big_w, big_m, big_v, big_names)]

    def upd(slots, ws, ms, vs, packer, name, tr):
        return _adamw(slots, packer(ws), packer(ms), packer(vs), name=name, tr=tr)

    small_m = [m_ev_dw_w[0], m_od_norm_w[0], m_od_conv_w[0], m_od_conv_b[0], m_od_gnorm_w[0]]
    small_v = [v_ev_dw_w[0], v_od_norm_w[0], v_od_conv_w[0], v_od_conv_b[0], v_od_gnorm_w[0]]
    out_small = upd(ssmall, small_w, small_m, small_v, lambda a: _pack(a, 8), "adamw_small", ssmall.shape[1])

    def rep_pack(a):
        return _pack_rep(a[0:5], a[5:8])

    rep_w = [ev_norm_w, ev_dw_b, ev_ln_w, ev_ln_b, final_norm_w, od_dt_bias, od_a_log, od_d]
    rep_m = [m_ev_norm_w, m_ev_dw_b, m_ev_ln_w, m_ev_ln_b, m_final_norm_w, m_od_dt_bias, m_od_a_log, m_od_d]
    rep_v = [v_ev_norm_w, v_ev_dw_b, v_ev_ln_w, v_ev_ln_b, v_final_norm_w, v_od_dt_bias, v_od_a_log, v_od_d]
    out_rep = upd(srep, rep_w, rep_m, rep_v, rep_pack, "adamw_rep", srep.shape[1])

    results = []
    for kind in range(4):
        bw = [o[kind].reshape((1,) + o[kind].shape) for o in out_bigs]
        sw = _unpack(out_small[kind], small_shapes)
        vecs, scal = _unpack_rep(out_rep[kind], rep_vec_shapes, od_dt_bias.shape)
        results.append([
            vecs[0], bw[0], sw[0].reshape(ev_dw_w.shape), vecs[1], vecs[2], vecs[3], bw[2],
            sw[1].reshape(od_norm_w.shape), bw[1], sw[2].reshape(od_conv_w.shape), sw[3].reshape(od_conv_b.shape),
            scal[0], scal[1], scal[2], sw[4].reshape(od_gnorm_w.shape), bw[3], vecs[4]])
    loss = lax.psum(loss_part[0, 0], ("x", "y", "c"))
    return (loss, grad_x.reshape(x.shape), *results[0], *results[1], *results[2], *results[3])
```

```python
import jax
import jax.numpy as jnp
from jax import lax
from jax.experimental import pallas as pl
from jax.experimental.pallas import tpu as pltpu

F32 = jnp.float32
BF16 = jnp.bfloat16

D_MODEL = 1024
CONV_W = 1024
ATT_W = 1024
HEAD_DIM = 128
N_HEADS = 8
CONF_K = 31
IN_EVEN = 7168
D_INNER = 2048
SSM_P = 64
SSM_H = 32
SSM_G = 4
SSM_R = SSM_H // SSM_G
SSM_N = 128
SSM_K = 4
CHUNK = 128
XBC = D_INNER + 2 * SSM_G * SSM_N
IN_ODD = D_INNER + XBC + SSM_H
IN_ODD_PAD = 5376
EPS = 1e-6
QB = 128
NEG_CUT = -100.0

ADAM_LR = 0.001
ADAM_B1 = 0.9
ADAM_B2 = 0.999
ADAM_EPS = 1e-08
ADAM_WD = 0.01
ADAM_STEP = 10

LANE = 128
VMEM_LIMIT = 56 * 1024 * 1024
MESH = pl.DeviceIdType.MESH

NN = (((1,), (0,)), ((), ()))
NT = (((1,), (1,)), ((), ()))
TN = (((0,), (0,)), ((), ()))


def _pallas(body, **kw):
    return pl.pallas_call(body, **kw)


def _params(n_axes):
    return pltpu.CompilerParams(dimension_semantics=("arbitrary",) * n_axes, vmem_limit_bytes=VMEM_LIMIT)


def _dot(a, b, dims=NN):
    return lax.dot_general(a.astype(BF16), b.astype(BF16), dims, preferred_element_type=F32)


def _parts(x):
    h = x.astype(BF16)
    r = x - h.astype(F32)
    m = r.astype(BF16)
    l = (r - m.astype(F32)).astype(BF16)
    return (h, m, l)


def _dotx(x, e01, dims=NN):
    acc = None
    for p in _parts(x):
        t = lax.dot_general(p, e01, dims, preferred_element_type=F32)
        acc = t if acc is None else acc + t
    return acc


def _dotx2(x, e01, dims=NN):
    h = x.astype(BF16)
    l = (x - h.astype(F32)).astype(BF16)
    return (lax.dot_general(h, e01, dims, preferred_element_type=F32)
            + lax.dot_general(l, e01, dims, preferred_element_type=F32))


def _xdot(e01, x, dims=NN):
    acc = None
    for p in _parts(x):
        t = lax.dot_general(e01, p, dims, preferred_element_type=F32)
        acc = t if acc is None else acc + t
    return acc


def _f32(x):
    return x.astype(F32)


def _sigmoid(x):
    return 1.0 / (1.0 + jnp.exp(-x))


def _dsilu(x, s):
    return s * (1.0 + x * (1.0 - s))


def _matmul(a, b, *, mode, out_dtype, bm, bn, bk, name, residual=None, n_major=False, comm_kinds=(), comm_srcs=()):
    if mode == "nn":
        (m, k), n = a.shape, b.shape[1]
        a_blk, a_map = (bm, bk), lambda i, j, kk: (i, kk)
        b_blk, b_map = (bk, bn), lambda i, j, kk: (kk, j)
        dims = NN
    elif mode == "nt":
        (m, k), n = a.shape, b.shape[0]
        a_blk, a_map = (bm, bk), lambda i, j, kk: (i, kk)
        b_blk, b_map = (bn, bk), lambda i, j, kk: (j, kk)
        dims = NT
    else:
        (k, m), n = a.shape, b.shape[1]
        a_blk, a_map = (bk, bm), lambda i, j, kk: (kk, i)
        b_blk, b_map = (bk, bn), lambda i, j, kk: (kk, j)
        dims = TN
    bm, bn, bk = min(bm, m), min(bn, n), min(bk, k)
    if mode != "nn":
        a_blk = (bm, bk) if mode == "nt" else (bk, bm)
        b_blk = (bn, bk) if mode == "nt" else (bk, bn)
    else:
        a_blk, b_blk = (bm, bk), (bk, bn)
    assert m % bm == 0 and n % bn == 0 and k % bk == 0, (name, m, n, k)
    nk = k // bk
    has_res = residual is not None

    def order(f):
        return (lambda j, i, kk: f(i, j, kk)) if n_major else f

    nw = len(comm_srcs)
    grid = (n // bn, m // bm, nk) if n_major else (m // bm, n // bn, nk)

    def body(*refs):
        a_ref, b_ref = refs[0], refs[1]
        r_ref = refs[2] if has_res else None
        n_in = 2 + has_res + nw
        o_ref = refs[n_in]
        n_out = n_in + 1 + nw

        def finish(r):
            if has_res:
                r = r + r_ref[...]
            o_ref[...] = r.astype(out_dtype)

        def compute():
            if nk == 1:
                finish(_dot(a_ref[...], b_ref[...], dims))
                return
            acc_ref = refs[n_out]
            kk = pl.program_id(2)

            @pl.when(kk == 0)
            def _():
                acc_ref[...] = jnp.zeros_like(acc_ref)

            acc_ref[...] += _dot(a_ref[...], b_ref[...], dims)

            @pl.when(kk == nk - 1)
            def _():
                finish(acc_ref[...])

        if not nw:
            compute()
            return
        start, done = _chip_plan(comm_kinds, refs[2 + has_res:n_in], refs[n_in + 1:n_out], refs[n_out + (nk > 1):])
        ids = [pl.program_id(ax) for ax in range(3)]

        @pl.when(jnp.logical_and(jnp.logical_and(ids[0] == 0, ids[1] == 0), ids[2] == 0))
        def _():
            start()

        compute()

        @pl.when(jnp.logical_and(jnp.logical_and(ids[0] == grid[0] - 1, ids[1] == grid[1] - 1), ids[2] == grid[2] - 1))
        def _():
            done()

    in_specs = [pl.BlockSpec(a_blk, order(a_map)), pl.BlockSpec(b_blk, order(b_map))]
    args = [a, b]
    out_map = order(lambda i, j, kk: (i, j))
    if has_res:
        in_specs.append(pl.BlockSpec((bm, bn), out_map))
        args.append(residual)
    any_spec = pl.BlockSpec(memory_space=pl.ANY)
    out_specs = [pl.BlockSpec((bm, bn), out_map)] + [any_spec] * nw
    out_shape = [jax.ShapeDtypeStruct((m, n), out_dtype)] + [_landing_shape(kd, s) for kd, s in zip(comm_kinds, comm_srcs)]
    res = _pallas(
        body, name=name, grid=grid, in_specs=in_specs + [any_spec] * nw, out_specs=out_specs, out_shape=out_shape,
        scratch_shapes=([pltpu.VMEM((bm, bn), F32)] if nk > 1 else []) + (_chip_sems(nw) if nw else []),
        compiler_params=_params(3),
    )(*args, *comm_srcs)
    return res if nw else res[0]


def _rms_fwd(x, w, *, name, tm=512):
    t, d = x.shape

    def body(x_ref, w_ref, o_ref):
        xv = x_ref[...]
        r = lax.rsqrt(jnp.mean(xv * xv, axis=1, keepdims=True) + EPS)
        o_ref[...] = (xv * r * w_ref[...]).astype(BF16)

    return _pallas(
        body, name=name, grid=(t // tm,),
        in_specs=[pl.BlockSpec((tm, d), lambda i: (i, 0)), pl.BlockSpec((1, d), lambda i: (0, 0))],
        out_specs=pl.BlockSpec((tm, d), lambda i: (i, 0)),
        out_shape=jax.ShapeDtypeStruct((t, d), BF16), compiler_params=_params(1),
    )(x, w)


def _rms_bwd(dn, x, w, dres, *, name, tm=512):
    t, d = x.shape

    def body(dn_ref, x_ref, w_ref, dr_ref, dx_ref, dw_ref):
        i = pl.program_id(0)
        xv = x_ref[...]
        r = lax.rsqrt(jnp.mean(xv * xv, axis=1, keepdims=True) + EPS)
        xh = xv * r
        dy = dn_ref[...].astype(F32)
        g = dy * w_ref[...]
        dx_ref[...] = dr_ref[...] + r * (g - xh * jnp.mean(g * xh, axis=1, keepdims=True))

        @pl.when(i == 0)
        def _():
            dw_ref[...] = jnp.zeros_like(dw_ref)

        dw_ref[...] += jnp.sum(dy * xh, axis=0, keepdims=True)

    row = pl.BlockSpec((tm, d), lambda i: (i, 0))
    vec = pl.BlockSpec((1, d), lambda i: (0, 0))
    return _pallas(
        body, name=name, grid=(t // tm,), in_specs=[row, row, vec, row], out_specs=[row, vec],
        out_shape=[jax.ShapeDtypeStruct((t, d), F32), jax.ShapeDtypeStruct((1, d), F32)],
        compiler_params=_params(1),
    )(dn, x, w, dres)


def _final_loss(h, w, target, *, tm=512):
    t, d = h.shape

    def body(h_ref, w_ref, t_ref, dh_ref, dw_ref, loss_ref):
        i = pl.program_id(0)
        xv = h_ref[...]
        r = lax.rsqrt(jnp.mean(xv * xv, axis=1, keepdims=True) + EPS)
        xh = xv * r
        wv = w_ref[...]
        err = xh * wv - t_ref[...]
        dy = err * (1.0 / d)
        g = dy * wv
        dh_ref[...] = r * (g - xh * jnp.mean(g * xh, axis=1, keepdims=True))

        @pl.when(i == 0)
        def _():
            dw_ref[...] = jnp.zeros_like(dw_ref)
            loss_ref[...] = jnp.zeros_like(loss_ref)

        dw_ref[...] += jnp.sum(dy * xh, axis=0, keepdims=True)
        part = jnp.sum(jnp.sum(err * err, axis=1, keepdims=True), axis=0, keepdims=True)
        loss_ref[...] += part * (0.5 / d)

    row = pl.BlockSpec((tm, d), lambda i: (i, 0))
    vec = pl.BlockSpec((1, d), lambda i: (0, 0))
    return _pallas(
        body, name="final_loss", grid=(t // tm,), in_specs=[row, vec, row],
        out_specs=[row, vec, pl.BlockSpec((1, LANE), lambda i: (0, 0))],
        out_shape=[jax.ShapeDtypeStruct((t, d), F32), jax.ShapeDtypeStruct((1, d), F32),
                   jax.ShapeDtypeStruct((1, LANE), F32)],
        compiler_params=_params(1),
    )(h, w, target)


HALO = 32


SUB = 8
RC = 16


def _make_shifts(sh_ref, rows, shifts=tuple(range(1, SUB))):
    for s in shifts:
        sh_ref[s, 0:rows, :] = sh_ref[0, s:s + rows, :]


def _shifted(sh_ref, r0, j, rows):
    return sh_ref[j % SUB, pl.ds(r0 + (j - j % SUB), rows), :]


def _taps(w8_ref, sh_ref, r0, first, step, init):
    accs = [init] * (RC // SUB)
    for k in range(CONF_K):
        wk = w8_ref[k * SUB:(k + 1) * SUB, :]
        x = _shifted(sh_ref, r0, first + step * k, RC)
        accs = [a + wk * x[q * SUB:(q + 1) * SUB] for q, a in enumerate(accs)]
    return jnp.concatenate(accs, axis=0)


def _conf_fwd(proj, dw_w, dw_b, ln_w, ln_b, seq, *, tm=256):
    t = proj.shape[0]
    c = CONV_W
    tps = seq // tm
    hb = tm // HALO

    def body(a_ref, b_ref, g_ref, ha_ref, hb_ref, w_ref, wb_ref, lw_ref, lb_ref, y_ref, u2_ref, sh_ref):
        i = pl.program_id(0)
        keep = jnp.where(i % tps == 0, 0.0, 1.0)
        sh_ref[0, 0:HALO, :] = _f32(ha_ref[...]) * _sigmoid(_f32(hb_ref[...])) * keep
        sh_ref[0, HALO:HALO + tm, :] = _f32(a_ref[...]) * _sigmoid(_f32(b_ref[...]))
        _make_shifts(sh_ref, tm + HALO - SUB)

        def chunk(ci, carry):
            r0 = pl.multiple_of(ci * RC, RC)
            acc = _taps(w_ref, sh_ref, r0, HALO - CONF_K + 1, 1, jnp.broadcast_to(wb_ref[...], (SUB, c)))
            u2_ref[pl.ds(r0, RC), :] = acc
            mu = jnp.mean(acc, axis=1, keepdims=True)
            xc = acc - mu
            rs = lax.rsqrt(jnp.mean(xc * xc, axis=1, keepdims=True) + EPS)
            u3 = xc * rs * lw_ref[...] + lb_ref[...]
            gv = _f32(g_ref[pl.ds(r0, RC), :])
            y_ref[pl.ds(r0, RC), :] = (u3 * _sigmoid(u3) * gv * _sigmoid(gv)).astype(BF16)
            return carry

        lax.fori_loop(0, tm // RC, chunk, 0, unroll=2)

    def col(j):
        return pl.BlockSpec((tm, c), lambda i: (i, j))

    def prev(j):
        return pl.BlockSpec((HALO, c), lambda i: (jnp.maximum(i * hb - 1, 0), j))

    vec = pl.BlockSpec((1, c), lambda i: (0, 0))
    return _pallas(
        body, name="conf_fwd", grid=(t // tm,),
        in_specs=[col(0), col(1), col(2), prev(0), prev(1),
                  pl.BlockSpec((HALO * SUB, c), lambda i: (0, 0)), vec, vec, vec],
        out_specs=[pl.BlockSpec((tm, c), lambda i: (i, 0)), pl.BlockSpec((tm, c), lambda i: (i, 0))],
        out_shape=[jax.ShapeDtypeStruct((t, c), BF16), jax.ShapeDtypeStruct((t, c), F32)],
        scratch_shapes=[pltpu.VMEM((SUB, tm + HALO, c), F32)], compiler_params=_params(1),
    )(proj, proj, proj, proj, proj, dw_w, dw_b, ln_w, ln_b)


def _conf_bwd(proj, u2, dycat, dw_w, ln_w, ln_b, seq, comm_kinds, comm_srcs, *, tm=256):
    t = proj.shape[0]
    c = CONV_W
    tps = seq // tm
    hb = tm // HALO
    nhb = t // HALO
    nw = len(comm_srcs)
    nsteps = t // tm

    def fold(v):
        out = v[0:SUB]
        for q in range(1, RC // SUB):
            out = out + v[q * SUB:(q + 1) * SUB]
        return out

    def body(*refs):
        (a_ref, b_ref, g_ref, pa_ref, pb_ref, ng_ref, u2_ref, nu2_ref, dy_ref, ndy_ref,
         w_ref, lw_ref, lb_ref) = refs[:13]
        dp_ref, dww_ref, dwb_ref, dlw_ref, dlb_ref = refs[13 + nw:18 + nw]
        su_ref, sd_ref = refs[18 + 2 * nw:20 + 2 * nw]
        comm_start, comm_finish = _chip_plan(comm_kinds, refs[13:13 + nw], refs[18 + nw:18 + 2 * nw],
                                             refs[20 + 2 * nw:])
        i = pl.program_id(0)
        first = i % tps == 0
        last = i % tps == tps - 1

        @pl.when(i == 0)
        def _():
            comm_start()
            dww_ref[...] = jnp.zeros_like(dww_ref)
            dwb_ref[...] = jnp.zeros_like(dwb_ref)
            dlw_ref[...] = jnp.zeros_like(dlw_ref)
            dlb_ref[...] = jnp.zeros_like(dlb_ref)

        su_ref[0, 0:HALO, :] = _f32(pa_ref[...]) * _sigmoid(_f32(pb_ref[...])) * jnp.where(first, 0.0, 1.0)
        su_ref[0, HALO:HALO + tm, :] = _f32(a_ref[...]) * _sigmoid(_f32(b_ref[...]))
        _make_shifts(su_ref, tm + HALO - SUB)

        def ln_back(u2c, gv, dy):
            mu = jnp.mean(u2c, axis=1, keepdims=True)
            xc = u2c - mu
            rs = lax.rsqrt(jnp.mean(xc * xc, axis=1, keepdims=True) + EPS)
            xh = xc * rs
            lw = lw_ref[...]
            u3 = xh * lw + lb_ref[...]
            s3 = _sigmoid(u3)
            sg = _sigmoid(gv)
            dgc = dy * (u3 * s3) * _dsilu(gv, sg)
            du3 = dy * gv * sg * _dsilu(u3, s3)
            dxh = du3 * lw
            du2 = rs * (dxh - jnp.mean(dxh, axis=1, keepdims=True)
                        - xh * jnp.mean(dxh * xh, axis=1, keepdims=True))
            return du2, dgc, du3, xh

        def tile_chunk(ci, carry):
            r0 = pl.multiple_of(ci * RC, RC)
            rows = pl.ds(r0, RC)
            du2, dgc, du3, xh = ln_back(u2_ref[rows, :], _f32(g_ref[rows, :]), _f32(dy_ref[rows, :]))
            sd_ref[0, rows, :] = du2
            dp_ref[rows, 2 * c:3 * c] = dgc.astype(BF16)
            dwb_ref[...] += fold(du2)
            dlw_ref[...] += fold(du3 * xh)
            dlb_ref[...] += fold(du3)
            return carry

        lax.fori_loop(0, tm // RC, tile_chunk, 0, unroll=2)
        live = jnp.where(last, 0.0, 1.0)
        for ci in range(HALO // RC):
            rows = slice(ci * RC, (ci + 1) * RC)
            du2, _, _, _ = ln_back(nu2_ref[rows, :], _f32(ng_ref[rows, :]), _f32(ndy_ref[rows, :]))
            sd_ref[0, tm + ci * RC:tm + (ci + 1) * RC, :] = du2 * live
        _make_shifts(sd_ref, tm + HALO - SUB)

        def tap_chunk(ci, carry):
            r0 = pl.multiple_of(ci * RC, RC)
            rows = pl.ds(r0, RC)
            du1 = _taps(w_ref, sd_ref, r0, CONF_K - 1, -1, jnp.zeros((SUB, c), F32))
            sb = _sigmoid(_f32(b_ref[rows, :]))
            dp_ref[rows, 0:c] = (du1 * sb).astype(BF16)
            dp_ref[rows, c:2 * c] = (du1 * _f32(a_ref[rows, :]) * sb * (1.0 - sb)).astype(BF16)
            du2 = sd_ref[0, rows, :]
            for k in range(CONF_K):
                dww_ref[k * SUB:(k + 1) * SUB, :] += fold(du2 * _shifted(su_ref, r0, HALO - CONF_K + 1 + k, RC))
            return carry

        lax.fori_loop(0, tm // RC, tap_chunk, 0)

        @pl.when(i == nsteps - 1)
        def _():
            comm_finish()

    def col(j):
        return pl.BlockSpec((tm, c), lambda i: (i, j))

    def prev(j):
        return pl.BlockSpec((HALO, c), lambda i: (jnp.maximum(i * hb - 1, 0), j))

    def nxt(j):
        return pl.BlockSpec((HALO, c), lambda i: (jnp.minimum((i + 1) * hb, nhb - 1), j))

    vec = pl.BlockSpec((1, c), lambda i: (0, 0))
    acc = pl.BlockSpec((SUB, c), lambda i: (0, 0))
    any_spec = pl.BlockSpec(memory_space=pl.ANY)
    return _pallas(
        body, name="conf_bwd", grid=(nsteps,),
        in_specs=[col(0), col(1), col(2), prev(0), prev(1), nxt(2), col(0), nxt(0), col(0), nxt(0),
                  pl.BlockSpec((HALO * SUB, c), lambda i: (0, 0)), vec, vec] + [any_spec] * nw,
        out_specs=[pl.BlockSpec((tm, 3 * c), lambda i: (i, 0)),
                   pl.BlockSpec((HALO * SUB, c), lambda i: (0, 0)), acc, acc, acc] + [any_spec] * nw,
        out_shape=[jax.ShapeDtypeStruct((t, 3 * c), BF16), jax.ShapeDtypeStruct((HALO * SUB, c), F32),
                   jax.ShapeDtypeStruct((SUB, c), F32), jax.ShapeDtypeStruct((SUB, c), F32),
                   jax.ShapeDtypeStruct((SUB, c), F32)]
        + [_landing_shape(kd, s) for kd, s in zip(comm_kinds, comm_srcs)],
        scratch_shapes=[pltpu.VMEM((SUB, tm + HALO, c), F32), pltpu.VMEM((SUB, tm + HALO, c), F32)] + _chip_sems(nw),
        compiler_params=_params(1),
    )(proj, proj, proj, proj, proj, proj, u2, u2, dycat, dycat, dw_w, ln_w, ln_b, *comm_srcs)


Q_COL = 3 * CONV_W // HEAD_DIM
K_COL = Q_COL + N_HEADS
V_COL = K_COL + N_HEADS
GA_COL = V_COL + N_HEADS


SBA_TQ = 256
SBA_WK = 4 * QB


def _sb_window(qs, kw, ws, limit, t0, carry):
    tq, wk = qs.shape[0], kw.shape[0]
    z = _dot(qs, kw, NT)
    sg = ws + lax.broadcasted_iota(jnp.int32, (tq, wk), 1)
    tg = t0 + lax.broadcasted_iota(jnp.int32, (tq, wk), 0)
    mask = sg < jnp.minimum(tg, limit)
    sp = jnp.log(1.0 + jnp.exp(-jnp.abs(z)))
    ls = jnp.minimum(z, 0.0) - sp
    lk = jnp.where(mask, ls - z, 0.0)
    jj = lax.broadcasted_iota(jnp.int32, (QB, QB), 0)
    ss = lax.broadcasted_iota(jnp.int32, (QB, QB), 1)
    ustrict = jnp.where(jj > ss, 1.0, 0.0).astype(BF16)
    laters = [None] * (wk // QB)
    for ch in reversed(range(wk // QB)):
        lkc = lk[:, ch * QB:(ch + 1) * QB]
        laters[ch] = carry + _dotx2(lkc, ustrict)
        carry = carry + jnp.sum(lkc, axis=1, keepdims=True)
    w = jnp.where(mask, jnp.exp(ls + jnp.concatenate(laters, axis=1)), 0.0)
    return mask, ls, w, carry


def _sba_fwd(proj, nb, seq, *, tq=SBA_TQ, wk=SBA_WK):
    t = proj.shape[0]
    wk = min(wk, seq)
    nq = seq // tq
    scale = HEAD_DIM ** -0.5

    def body(q_ref, k_ref, v_ref, g_ref, o_ref, y_ref):
        i = pl.program_id(2)
        t0 = i * tq
        qs = (_f32(q_ref[...]) * scale).astype(BF16)

        def window(ws, limit, carry, acc):
            ws = pl.multiple_of(ws, QB)
            _, _, w, carry = _sb_window(qs, k_ref[pl.ds(ws, wk), :], ws, limit, t0, carry)
            return carry, acc + _dot(w, v_ref[pl.ds(ws, wk), :])

        ws0 = jnp.maximum(t0 + tq - wk, 0)
        carry, acc = window(ws0, seq, jnp.zeros((tq, 1), F32), jnp.zeros((tq, HEAD_DIM), F32))

        def cond(st):
            return jnp.logical_and(st[0] > 0, jnp.max(st[1]) > NEG_CUT)

        def step(st):
            c2, a2 = window(jnp.maximum(st[0] - wk, 0), st[0], st[1], st[2])
            return jnp.maximum(st[0] - wk, 0), c2, a2

        _, _, acc = lax.while_loop(cond, step, (ws0, carry, acc))
        o_ref[...] = acc
        gv = _f32(g_ref[...])
        y_ref[...] = (acc * gv * _sigmoid(gv)).astype(BF16)

    def tile(c0):
        return pl.BlockSpec((tq, HEAD_DIM), lambda b, h, i: (b * nq + i, c0 + h))

    def whole(c0):
        return pl.BlockSpec((seq, HEAD_DIM), lambda b, h, i: (b, c0 + h))

    return _pallas(
        body, name="sba_fwd", grid=(nb, N_HEADS, nq),
        in_specs=[tile(Q_COL), whole(K_COL), whole(V_COL), tile(GA_COL)],
        out_specs=[tile(0), tile(0)],
        out_shape=[jax.ShapeDtypeStruct((t, ATT_W), F32), jax.ShapeDtypeStruct((t, ATT_W), BF16)],
        compiler_params=_params(3),
    )(proj, proj, proj, proj)


def _sba_bwd(proj, o, dycat, nb, seq, *, tq=SBA_TQ, wk=SBA_WK):
    t = proj.shape[0]
    wk = min(wk, seq)
    nq = seq // tq
    nwin = -(-seq // wk) + 1
    nch = wk // QB
    scale = HEAD_DIM ** -0.5

    def body(q_ref, k_ref, v_ref, g_ref, o_ref, dy_ref, dq_ref, dk_ref, dv_ref, dg_ref, e_ref, sp_ref):
        i = pl.program_id(2)
        t0 = i * tq

        @pl.when(i == 0)
        def _():
            dk_ref[...] = jnp.zeros_like(dk_ref)
            dv_ref[...] = jnp.zeros_like(dv_ref)

        qs = (_f32(q_ref[...]) * scale).astype(BF16)
        gv = _f32(g_ref[...])
        sg = _sigmoid(gv)
        dy = _f32(dy_ref[...])
        do = (dy * gv * sg).astype(BF16)
        dg_ref[...] = (dy * o_ref[...] * _dsilu(gv, sg)).astype(BF16)

        def start_of(n):
            return pl.multiple_of(jnp.maximum(t0 + tq - (n + 1) * wk, 0), QB)

        def limit_of(n):
            return jnp.where(n == 0, seq, jnp.maximum(t0 + tq - n * wk, 0))

        def near(n, carry):
            ws = start_of(n)
            _, ls, w, carry = _sb_window(qs, k_ref[pl.ds(ws, wk), :], ws, limit_of(n), t0, carry)
            e_ref[n] = w * _dot(do, v_ref[pl.ds(ws, wk), :], NT)
            sp_ref[n] = jnp.exp(ls)
            dv_ref[pl.ds(ws, wk), :] += _dot(w, do, TN)
            return carry

        carry = near(0, jnp.zeros((tq, 1), F32))

        def cond(st):
            return jnp.logical_and(start_of(st[0] - 1) > 0, jnp.max(st[1]) > NEG_CUT)

        def step(st):
            return st[0] + 1, near(st[0], st[1])

        nvis, _ = lax.while_loop(cond, step, (1, carry))

        jj = lax.broadcasted_iota(jnp.int32, (QB, QB), 0)
        ss = lax.broadcasted_iota(jnp.int32, (QB, QB), 1)
        lstrict = jnp.where(jj < ss, 1.0, 0.0).astype(BF16)

        def far(r, st):
            pre, dq = st
            n = nvis - 1 - r
            ws = start_of(n)
            e = e_ref[n]
            spn = sp_ref[n]
            gs = []
            for ch in range(nch):
                ec = e[:, ch * QB:(ch + 1) * QB]
                gs.append(pre + _dotx2(ec, lstrict))
                pre = pre + jnp.sum(ec, axis=1, keepdims=True)
            sgl = ws + lax.broadcasted_iota(jnp.int32, (tq, wk), 1)
            tgl = t0 + lax.broadcasted_iota(jnp.int32, (tq, wk), 0)
            mask = sgl < jnp.minimum(tgl, limit_of(n))
            dz = jnp.where(mask, e * (1.0 - spn) - jnp.concatenate(gs, axis=1) * spn, 0.0).astype(BF16)
            dk_ref[pl.ds(ws, wk), :] += _dot(dz, qs, TN)
            return pre, dq + _dot(dz, k_ref[pl.ds(ws, wk), :])

        _, dq = lax.fori_loop(0, nvis, far, (jnp.zeros((tq, 1), F32), jnp.zeros((tq, HEAD_DIM), F32)))
        dq_ref[...] = (dq * scale).astype(BF16)

    def tile(c0):
        return pl.BlockSpec((tq, HEAD_DIM), lambda b, h, i: (b * nq + i, c0 + h))

    def whole(c0):
        return pl.BlockSpec((seq, HEAD_DIM), lambda b, h, i: (b, c0 + h))

    return _pallas(
        body, name="sba_bwd", grid=(nb, N_HEADS, nq),
        in_specs=[tile(Q_COL), whole(K_COL), whole(V_COL), tile(GA_COL), tile(0),
                  tile(CONV_W // HEAD_DIM)],
        out_specs=[tile(0), whole(0), whole(0), tile(0)],
        out_shape=[jax.ShapeDtypeStruct((t, ATT_W), BF16), jax.ShapeDtypeStruct((t, ATT_W), F32),
                   jax.ShapeDtypeStruct((t, ATT_W), F32), jax.ShapeDtypeStruct((t, ATT_W), BF16)],
        scratch_shapes=[pltpu.VMEM((nwin, tq, wk), F32), pltpu.VMEM((nwin, tq, wk), F32)],
        compiler_params=_params(3),
    )(proj, proj, proj, proj, o, dycat)


CT = 512
PH = 8
XRC = 32
X_SHIFTS = tuple(s for s in range(PH - SSM_K + 1, PH))
D_SHIFTS = tuple(range(1, SSM_K))
Z_BLK = 0
XBC_BLK = D_INNER // CT
DT_BLK = (D_INNER + XBC) // LANE


def _softplus(x):
    return jnp.maximum(x, 0.0) + jnp.log(1.0 + jnp.exp(-jnp.abs(x)))


def _dt_fwd(proj, dt_bias, *, tm=512):
    t = proj.shape[0]

    def body(p_ref, b_ref, o_ref):
        o_ref[...] = _softplus(p_ref[...] + b_ref[...])

    return _pallas(
        body, name="dt_fwd", grid=(t // tm,),
        in_specs=[pl.BlockSpec((tm, LANE), lambda i: (i, DT_BLK)), pl.BlockSpec((1, LANE), lambda i: (0, 0))],
        out_specs=pl.BlockSpec((tm, LANE), lambda i: (i, 0)),
        out_shape=jax.ShapeDtypeStruct((t, LANE), F32), compiler_params=_params(1),
    )(proj, dt_bias)


def _dt_bwd(proj, dt_bias, ddt, dproj, *, tm=512):
    t = proj.shape[0]
    wide = IN_ODD_PAD - D_INNER - XBC

    def body(p_ref, b_ref, d_ref, dp_any, o_ref, db_ref):
        i = pl.program_id(0)
        lanes = lax.broadcasted_iota(jnp.int32, (tm, LANE), 1)
        dr = jnp.where(lanes < SSM_H, d_ref[...] * _sigmoid(p_ref[...] + b_ref[...]), 0.0)
        o_ref[:, 0:LANE] = dr.astype(BF16)
        o_ref[:, LANE:wide] = jnp.zeros((tm, wide - LANE), BF16)

        @pl.when(i == 0)
        def _():
            db_ref[...] = jnp.zeros_like(db_ref)

        db_ref[...] += jnp.sum(dr, axis=0, keepdims=True)

    vec = pl.BlockSpec((1, LANE), lambda i: (0, 0))
    row = pl.BlockSpec((tm, LANE), lambda i: (i, 0))
    return _pallas(
        body, name="dt_bwd", grid=(t // tm,),
        in_specs=[pl.BlockSpec((tm, LANE), lambda i: (i, DT_BLK)), vec, row, pl.BlockSpec(memory_space=pl.ANY)],
        out_specs=[pl.BlockSpec((tm, wide), lambda i: (i, (D_INNER + XBC) // wide)), vec],
        out_shape=[jax.ShapeDtypeStruct(dproj.shape, dproj.dtype), jax.ShapeDtypeStruct((1, LANE), F32)],
        input_output_aliases={3: 0}, compiler_params=_params(1),
    )(proj, dt_bias, ddt, dproj)


def _xconv_fwd(proj, conv_w, conv_b, seq, *, tm=512):
    t = proj.shape[0]
    tps = seq // tm
    hb = tm // PH

    def body(x_ref, h_ref, w_ref, b_ref, o_ref, sh_ref):
        i = pl.program_id(1)
        sh_ref[0, 0:PH, :] = h_ref[...] * jnp.where(i % tps == 0, 0.0, 1.0)
        sh_ref[0, PH:PH + tm, :] = x_ref[...]
        _make_shifts(sh_ref, tm, X_SHIFTS)

        def chunk(ci, carry):
            r0 = pl.multiple_of(ci * XRC, XRC)
            acc = jnp.zeros((XRC, CT), F32) + b_ref[...]
            for k in range(SSM_K):
                acc = acc + w_ref[k:k + 1, :] * _shifted(sh_ref, r0, PH - SSM_K + 1 + k, XRC)
            o_ref[pl.ds(r0, XRC), :] = acc * _sigmoid(acc)
            return carry

        lax.fori_loop(0, tm // XRC, chunk, 0)

    return _pallas(
        body, name="xconv_fwd", grid=(XBC // CT, t // tm),
        in_specs=[pl.BlockSpec((tm, CT), lambda j, i: (i, XBC_BLK + j)),
                  pl.BlockSpec((PH, CT), lambda j, i: (jnp.maximum(i * hb - 1, 0), XBC_BLK + j)),
                  pl.BlockSpec((PH, CT), lambda j, i: (0, j)),
                  pl.BlockSpec((1, CT), lambda j, i: (0, j))],
        out_specs=pl.BlockSpec((tm, CT), lambda j, i: (i, j)),
        out_shape=jax.ShapeDtypeStruct((t, XBC), F32),
        scratch_shapes=[pltpu.VMEM((SUB, tm + PH, CT), F32)], compiler_params=_params(2),
    )(proj, proj, conv_w, conv_b)


def _xconv_bwd(proj, dxc, conv_w, conv_b, dproj, seq, *, tm=512):
    t = proj.shape[0]
    tps = seq // tm
    hb = tm // PH
    nhb = t // PH
    te = tm + PH

    def fold(v):
        out = v[0:SUB]
        for q in range(1, v.shape[0] // SUB):
            out = out + v[q * SUB:(q + 1) * SUB]
        return out

    def body(x_ref, p_ref, n_ref, d_ref, nd_ref, w_ref, b_ref, dp_any, dx_ref, dw_ref, db_ref, sx_ref, sd_ref):
        i = pl.program_id(1)
        first = i % tps == 0
        last = i % tps == tps - 1

        @pl.when(i == 0)
        def _():
            dw_ref[...] = jnp.zeros_like(dw_ref)
            db_ref[...] = jnp.zeros_like(db_ref)

        sx_ref[0, 0:PH, :] = p_ref[...] * jnp.where(first, 0.0, 1.0)
        sx_ref[0, PH:PH + tm, :] = x_ref[...]
        sx_ref[0, PH + tm:PH + te, :] = n_ref[...]
        _make_shifts(sx_ref, te, X_SHIFTS)

        def dv_of(r0, rows, dy):
            acc = jnp.zeros((rows, CT), F32) + b_ref[...]
            for k in range(SSM_K):
                acc = acc + w_ref[k:k + 1, :] * _shifted(sx_ref, r0, PH - SSM_K + 1 + k, rows)
            return dy * _dsilu(acc, _sigmoid(acc))

        def dv_chunk(ci, carry):
            r0 = pl.multiple_of(ci * XRC, XRC)
            dv = dv_of(r0, XRC, d_ref[pl.ds(r0, XRC), :])
            sd_ref[0, pl.ds(r0, XRC), :] = dv
            db_ref[...] += fold(dv)
            return carry

        lax.fori_loop(0, tm // XRC, dv_chunk, 0)
        sd_ref[0, tm:te, :] = dv_of(tm, PH, nd_ref[...]) * jnp.where(last, 0.0, 1.0)
        _make_shifts(sd_ref, tm, D_SHIFTS)

        def tap_chunk(ci, carry):
            r0 = pl.multiple_of(ci * XRC, XRC)
            dx = jnp.zeros((XRC, CT), F32)
            for k in range(SSM_K):
                dx = dx + w_ref[k:k + 1, :] * _shifted(sd_ref, r0, SSM_K - 1 - k, XRC)
            dx_ref[pl.ds(r0, XRC), :] = dx.astype(BF16)
            dv = sd_ref[0, pl.ds(r0, XRC), :]
            for k in range(SSM_K):
                dw_ref[k * SUB:(k + 1) * SUB, :] += fold(dv * _shifted(sx_ref, r0, PH - SSM_K + 1 + k, XRC))
            return carry

        lax.fori_loop(0, tm // XRC, tap_chunk, 0)

    return _pallas(
        body, name="xconv_bwd", grid=(XBC // CT, t // tm),
        in_specs=[pl.BlockSpec((tm, CT), lambda j, i: (i, XBC_BLK + j)),
                  pl.BlockSpec((PH, CT), lambda j, i: (jnp.maximum(i * hb - 1, 0), XBC_BLK + j)),
                  pl.BlockSpec((PH, CT), lambda j, i: (jnp.minimum((i + 1) * hb, nhb - 1), XBC_BLK + j)),
                  pl.BlockSpec((tm, CT), lambda j, i: (i, j)),
                  pl.BlockSpec((PH, CT), lambda j, i: (jnp.minimum((i + 1) * hb, nhb - 1), j)),
                  pl.BlockSpec((PH, CT), lambda j, i: (0, j)),
                  pl.BlockSpec((1, CT), lambda j, i: (0, j)),
                  pl.BlockSpec(memory_space=pl.ANY)],
        out_specs=[pl.BlockSpec((tm, CT), lambda j, i: (i, XBC_BLK + j)),
                   pl.BlockSpec((PH * SUB, CT), lambda j, i: (0, j)),
                   pl.BlockSpec((SUB, CT), lambda j, i: (0, j))],
        out_shape=[jax.ShapeDtypeStruct(dproj.shape, dproj.dtype), jax.ShapeDtypeStruct((PH * SUB, XBC), F32),
                   jax.ShapeDtypeStruct((SUB, XBC), F32)],
        scratch_shapes=[pltpu.VMEM((SUB, tm + 2 * PH, CT), F32), pltpu.VMEM((SUB, te, CT), F32)],
        input_output_aliases={7: 0}, compiler_params=_params(2),
    )(proj, proj, proj, dxc, dxc, conv_w, conv_b, dproj)


def _ssd_common(xbc, dt, alog, ex):
    L = CHUNK
    a = -jnp.exp(alog)
    la = dt * a
    li = lax.broadcasted_iota(jnp.int32, (L, L), 0)
    si = lax.broadcasted_iota(jnp.int32, (L, L), 1)
    lower = si <= li
    tri = jnp.where(lower, 1.0, 0.0).astype(BF16)
    cs = _xdot(tri, la)
    cst = _dotx(la, tri, (((0,), (1,)), ((), ())))
    csl = cs[L - 1:L, :]
    ecs_x = _dotx2(jnp.exp(cs)[:, 0:SSM_H], ex)
    tail_x = _dotx2(jnp.exp(csl - cs)[:, 0:SSM_H], ex)
    dt_x = _dotx2(dt[:, 0:SSM_H], ex)
    return a, la, lower, tri, cs, cst, ecs_x, tail_x, dt_x


def _ssd_fwd(xbc_c, dt, a_log, ex, nb, seq):
    t = xbc_c.shape[0]
    L = CHUNK
    nc = seq // L
    GW = SSM_R * SSM_P

    def body(x_ref, dt_ref, al_ref, ex_ref, y_ref, st_ref, state):
        c = pl.program_id(1)

        @pl.when(c == 0)
        def _():
            state[...] = jnp.zeros_like(state)

        st_ref[0] = state[...]
        xbc = x_ref[...]
        _, _, lower, _, cs, cst, ecs_x, tail_x, dt_x = _ssd_common(xbc, dt_ref[...], al_ref[...], ex_ref[...])
        xd = xbc[:, 0:D_INNER] * dt_x
        xdb = xd.astype(BF16)
        xt = (xd * tail_x).astype(BF16)
        el_x = ecs_x[L - 1:L, :]
        for g in range(SSM_G):
            bg = xbc[:, D_INNER + g * SSM_N:D_INNER + (g + 1) * SSM_N].astype(BF16)
            cg = xbc[:, D_INNER + (SSM_G + g) * SSM_N:D_INNER + (SSM_G + g + 1) * SSM_N].astype(BF16)
            cb = _dot(cg, bg, NT)
            sg = state[:, g * GW:(g + 1) * GW]
            ys = _dot(cg, sg) * ecs_x[:, g * GW:(g + 1) * GW]
            for r in range(SSM_R):
                h = g * SSM_R + r
                seg = cs[:, h:h + 1] - cst[h:h + 1, :]
                dec = jnp.exp(jnp.where(lower, seg, -1e30))
                yh = _dot(cb * dec, xdb[:, h * SSM_P:(h + 1) * SSM_P])
                y_ref[:, h * SSM_P:(h + 1) * SSM_P] = yh + ys[:, r * SSM_P:(r + 1) * SSM_P]
            state[:, g * GW:(g + 1) * GW] = sg * el_x[:, g * GW:(g + 1) * GW] + _dot(bg, xt[:, g * GW:(g + 1) * GW], TN)

    return _pallas(
        body, name="ssd_fwd", grid=(nb, nc),
        in_specs=[pl.BlockSpec((L, XBC), lambda b, c: (b * nc + c, 0)),
                  pl.BlockSpec((L, LANE), lambda b, c: (b * nc + c, 0)),
                  pl.BlockSpec((1, LANE), lambda b, c: (0, 0)),
                  pl.BlockSpec((SSM_H, D_INNER), lambda b, c: (0, 0))],
        out_specs=[pl.BlockSpec((L, D_INNER), lambda b, c: (b * nc + c, 0)),
                   pl.BlockSpec((1, SSM_N, D_INNER), lambda b, c: (b * nc + c, 0, 0))],
        out_shape=[jax.ShapeDtypeStruct((t, D_INNER), F32),
                   jax.ShapeDtypeStruct((nb * nc, SSM_N, D_INNER), F32)],
        scratch_shapes=[pltpu.VMEM((SSM_N, D_INNER), F32)], compiler_params=_params(2),
    )(xbc_c, dt, a_log, ex)


def _ssd_bwd(xbc_c, dt, a_log, ex, ext, states, dy, d_x, nb, seq):
    t = xbc_c.shape[0]
    L = CHUNK
    nc = seq // L
    GW = SSM_R * SSM_P

    def body(x_ref, dt_ref, al_ref, ex_ref, ext_ref, st_ref, dy_ref, sk_ref, dx_ref, ddt_ref, da_ref,
             dstate, dxd, yd, lastv):
        b = pl.program_id(0)
        c = pl.program_id(1)

        @pl.when(c == 0)
        def _():
            dstate[...] = jnp.zeros_like(dstate)

        @pl.when(jnp.logical_and(b == 0, c == 0))
        def _():
            da_ref[...] = jnp.zeros_like(da_ref)

        xbc = x_ref[...]
        dtv = dt_ref[...]
        ex_t = ext_ref[...]
        a, la, lower, tri, cs, cst, ecs_x, tail_x, dt_x = _ssd_common(xbc, dtv, al_ref[...], ex_ref[...])
        xs = xbc[:, 0:D_INNER]
        xd = xs * dt_x
        xdb = xd.astype(BF16)
        dyv = dy_ref[...]
        dyb = dyv.astype(BF16)
        dys = dyv * ecs_x
        xt = xd * tail_x
        el_x = ecs_x[L - 1:L, :]
        lane = lax.broadcasted_iota(jnp.int32, (L, LANE), 1)
        sub = lax.broadcasted_iota(jnp.int32, (LANE, L), 0)
        row_part = jnp.zeros((L, LANE), F32)
        col_part = jnp.zeros((LANE, L), F32)
        for g in range(SSM_G):
            gs = slice(g * GW, (g + 1) * GW)
            bcol = slice(D_INNER + g * SSM_N, D_INNER + (g + 1) * SSM_N)
            ccol = slice(D_INNER + (SSM_G + g) * SSM_N, D_INNER + (SSM_G + g + 1) * SSM_N)
            bg = xbc[:, bcol].astype(BF16)
            cg = xbc[:, ccol].astype(BF16)
            cb = _dot(cg, bg, NT)
            sg = st_ref[0, :, gs]
            dsg = dstate[:, gs]
            dc = _dot(dys[:, gs], sg, NT)
            db = _dot(xt[:, gs], dsg, NT)
            dx_state = tail_x[:, gs] * _dot(bg, dsg)
            tail_part = xd[:, gs] * dx_state
            yd[:, gs] = dys[:, gs] * _dot(cg, sg) - tail_part
            last = jnp.sum(tail_part, axis=0, keepdims=True) + el_x[:, gs] * jnp.sum(dsg * sg, axis=0, keepdims=True)
            lastv[:, gs] = jnp.broadcast_to(last, (8, GW))
            dcb = jnp.zeros((L, L), F32)
            for r in range(SSM_R):
                h = g * SSM_R + r
                hs = slice(h * SSM_P, (h + 1) * SSM_P)
                seg = cs[:, h:h + 1] - cst[h:h + 1, :]
                dec = jnp.exp(jnp.where(lower, seg, -1e30))
                m = cb * dec
                dm = _dot(dyb[:, hs], xdb[:, hs], NT)
                dcb = dcb + dm * dec
                e = dm * m
                row_part = row_part + jnp.where(lane == h, jnp.sum(e, axis=1, keepdims=True), 0.0)
                col_part = col_part + jnp.where(sub == h, jnp.sum(e, axis=0, keepdims=True), 0.0)
                dxd[:, hs] = _dot(m, dyb[:, hs], TN) + dx_state[:, r * SSM_P:(r + 1) * SSM_P]
            dx_ref[:, bcol] = db + _dot(dcb, cg, TN)
            dx_ref[:, ccol] = dc + _dot(dcb, bg)
            dstate[:, gs] = dsg * el_x[:, gs] + _dot(cg, dys[:, gs], TN)
        dxv = dxd[...]
        dx_ref[:, 0:D_INNER] = dxv * dt_x + dyv * sk_ref[...]
        ddt_x = _dotx(dxv * xs, ex_t)
        yst = _dotx(yd[...], ex_t)
        lst = _dotx(lastv[...], ex_t)[0:1, :]
        rows = lax.broadcasted_iota(jnp.int32, (L, LANE), 0)
        dcs = row_part - col_part.T + yst + jnp.where(rows == L - 1, lst, 0.0)
        li = lax.broadcasted_iota(jnp.int32, (L, L), 0)
        si = lax.broadcasted_iota(jnp.int32, (L, L), 1)
        upper = jnp.where(si >= li, 1.0, 0.0).astype(BF16)
        dla = _xdot(upper, dcs)
        ddt_ref[...] = dla * a + ddt_x
        da_ref[...] += jnp.sum(dla * dtv, axis=0, keepdims=True)

    def row(w):
        return pl.BlockSpec((L, w), lambda b, c: (b * nc + nc - 1 - c, 0))

    return _pallas(
        body, name="ssd_bwd", grid=(nb, nc),
        in_specs=[row(XBC), row(LANE), pl.BlockSpec((1, LANE), lambda b, c: (0, 0)),
                  pl.BlockSpec((SSM_H, D_INNER), lambda b, c: (0, 0)),
                  pl.BlockSpec((D_INNER, LANE), lambda b, c: (0, 0)),
                  pl.BlockSpec((1, SSM_N, D_INNER), lambda b, c: (b * nc + nc - 1 - c, 0, 0)),
                  row(D_INNER), pl.BlockSpec((1, D_INNER), lambda b, c: (0, 0))],
        out_specs=[row(XBC), row(LANE), pl.BlockSpec((1, LANE), lambda b, c: (0, 0))],
        out_shape=[jax.ShapeDtypeStruct((t, XBC), F32), jax.ShapeDtypeStruct((t, LANE), F32),
                   jax.ShapeDtypeStruct((1, LANE), F32)],
        scratch_shapes=[pltpu.VMEM((SSM_N, D_INNER), F32), pltpu.VMEM((L, D_INNER), F32),
                        pltpu.VMEM((L, D_INNER), F32), pltpu.VMEM((8, D_INNER), F32)],
        compiler_params=_params(2),
    )(xbc_c, dt, a_log, ex, ext, states, dy, d_x)


def _group_rms(y2):
    gw = D_INNER // SSM_G
    parts = []
    for g in range(SSM_G):
        v = y2[:, g * gw:(g + 1) * gw]
        r = lax.rsqrt(jnp.mean(v * v, axis=1, keepdims=True) + EPS)
        parts.append(jnp.broadcast_to(r, v.shape))
    return jnp.concatenate(parts, axis=1)


def _gate_fwd(y, xbc_c, proj, d_x, gn_w, *, tm=256):
    t = y.shape[0]

    def body(y_ref, x_ref, z_ref, d_ref, w_ref, o_ref):
        y1 = y_ref[...] + d_ref[...] * x_ref[...]
        zv = z_ref[...]
        y2 = y1 * zv * _sigmoid(zv)
        o_ref[...] = (y2 * _group_rms(y2) * w_ref[...]).astype(BF16)

    row = pl.BlockSpec((tm, D_INNER), lambda i: (i, 0))
    vec = pl.BlockSpec((1, D_INNER), lambda i: (0, 0))
    return _pallas(
        body, name="gate_fwd", grid=(t // tm,), in_specs=[row, row, row, vec, vec], out_specs=row,
        out_shape=jax.ShapeDtypeStruct((t, D_INNER), BF16), compiler_params=_params(1),
    )(y, xbc_c, proj, d_x, gn_w)


def _gate_bwd(dyg, y, xbc_c, proj, d_x, gn_w, *, tm=256):
    t = y.shape[0]
    gw = D_INNER // SSM_G

    def body(dg_ref, y_ref, x_ref, z_ref, d_ref, w_ref, dy_ref, dz_ref, dw_ref, dd_ref):
        i = pl.program_id(0)
        xv = x_ref[...]
        dxv = d_ref[...]
        y1 = y_ref[...] + dxv * xv
        zv = z_ref[...]
        sz = _sigmoid(zv)
        y2 = y1 * zv * sz
        rr = _group_rms(y2)
        xh = y2 * rr
        dg = _f32(dg_ref[...])
        gq = dg * w_ref[...]
        prod = gq * xh
        means = []
        for g in range(SSM_G):
            mg = jnp.mean(prod[:, g * gw:(g + 1) * gw], axis=1, keepdims=True)
            means.append(jnp.broadcast_to(mg, (tm, gw)))
        dy2 = rr * (gq - xh * jnp.concatenate(means, axis=1))
        dy1 = dy2 * zv * sz
        dy_ref[...] = dy1
        dz_ref[...] = (dy2 * y1 * _dsilu(zv, sz)).astype(BF16)

        @pl.when(i == 0)
        def _():
            dw_ref[...] = jnp.zeros_like(dw_ref)
            dd_ref[...] = jnp.zeros_like(dd_ref)

        dw_ref[...] += jnp.sum(dg * xh, axis=0, keepdims=True)
        dd_ref[...] += jnp.sum(dy1 * xv, axis=0, keepdims=True)

    row = pl.BlockSpec((tm, D_INNER), lambda i: (i, 0))
    vec = pl.BlockSpec((1, D_INNER), lambda i: (0, 0))
    return _pallas(
        body, name="gate_bwd", grid=(t // tm,), in_specs=[row, row, row, row, vec, vec],
        out_specs=[row, row, vec, vec],
        out_shape=[jax.ShapeDtypeStruct((t, D_INNER), F32),
                   jax.ShapeDtypeStruct((t, IN_ODD_PAD), BF16), jax.ShapeDtypeStruct((1, D_INNER), F32),
                   jax.ShapeDtypeStruct((1, D_INNER), F32)],
        compiler_params=_params(1),
    )(dyg, y, xbc_c, proj, d_x, gn_w)


ANY = pl.BlockSpec(memory_space=pl.ANY)


def _remote(src, dst, sems, k, to):
    send_sems, recv_sems = sems
    return pltpu.make_async_remote_copy(src_ref=src, dst_ref=dst, send_sem=send_sems.at[k], recv_sem=recv_sems.at[k],
                                        device_id=to, device_id_type=MESH)


NCHIP = 4


def _gathered_shape(kind, shard):
    r, n = shard.shape
    shape = {"cols": (r, NCHIP * n), "slab": (NCHIP, r, n), "rows": (NCHIP * r, n)}[kind]
    return jax.ShapeDtypeStruct(shape, shard.dtype)


def _gather_plan(kinds, shards, outs, sems, small=None):
    ici_s, ici_r, d2d_s, d2d_r = sems
    nw = len(shards)
    per = nw + (small is not None)

    def place():
        x, y, c = lax.axis_index("x"), lax.axis_index("y"), lax.axis_index("c")
        return 2 * x + y, c, (x, y, 1 - c), [(1 - x, y), (x, 1 - y), (1 - x, 1 - y)]

    def region(j, chip, half):
        r, n = shards[j].shape
        h = r // 2
        if kinds[j] == "cols":
            return outs[j].at[pl.ds(half * h, h), pl.ds(pl.multiple_of(chip * n, LANE), n)]
        if kinds[j] == "slab":
            return outs[j].at[chip, pl.ds(half * h, h), :]
        return outs[j].at[pl.ds(chip * r + half * h, h), :]

    def my_sends(me, c, peers):
        cps = []
        for k, (px, py) in enumerate(peers):
            for j in range(nw):
                h = shards[j].shape[0] // 2
                cps.append(_remote(shards[j].at[pl.ds(c * h, h), :], region(j, me, c), (ici_s, ici_r), per * k + j, (px, py, c)))
            if small is not None:
                cps.append(_remote(small[0], small[1].at[me], (ici_s, ici_r), per * k + nw, (px, py, c)))
        return cps

    def start():
        me, c, _, peers = place()
        for cp in my_sends(me, c, peers):
            cp.start()

    def finish():
        me, c, sib, peers = place()
        fwds = []
        for k, (px, py) in enumerate(peers):
            q = 2 * px + py
            for j in range(nw):
                d = region(j, q, c)
                _remote(d, d, (ici_s, ici_r), per * k + j, (px, py, c)).wait_recv()
                fwds.append(_remote(d, d, (d2d_s, d2d_r), nw * k + j, sib))
                fwds[-1].start()
            if small is not None:
                _remote(small[0], small[1].at[q], (ici_s, ici_r), per * k + nw, (px, py, c)).wait_recv()
        for k, (px, py) in enumerate(peers):
            for j in range(nw):
                d = region(j, 2 * px + py, 1 - c)
                _remote(d, d, (d2d_s, d2d_r), nw * k + j, sib).wait_recv()
        for cp in my_sends(me, c, peers) + fwds:
            cp.wait_send()

    return start, finish


def _gather_sems(nw, with_small):
    n_ici = 3 * (nw + with_small)
    return [pltpu.SemaphoreType.DMA((n_ici,)), pltpu.SemaphoreType.DMA((n_ici,)),
            pltpu.SemaphoreType.DMA((3 * nw,)), pltpu.SemaphoreType.DMA((3 * nw,))]


def _gather_shards(kinds, shards, small):
    nw = len(shards)

    def body(*refs):
        ins, sm, outs, osm, sems = refs[:nw], refs[nw], refs[nw + 1:2 * nw + 1], refs[2 * nw + 1], refs[2 * nw + 2:]
        start, finish = _gather_plan(kinds, ins, outs, sems, small=(sm, osm))
        start()
        finish()

    return _pallas(
        body, name="gather_shards", in_specs=[ANY] * (nw + 1), out_specs=[ANY] * (nw + 1),
        out_shape=[_gathered_shape(kd, s) for kd, s in zip(kinds, shards)]
        + [jax.ShapeDtypeStruct((NCHIP,) + small.shape, small.dtype)],
        scratch_shapes=_gather_sems(nw, 1),
    )(*shards, small)


def _place_cols(full, shard, *, name, tr=256):
    r, n = shard.shape

    def body(p_ref, full_any, s_ref, o_ref):
        o_ref[...] = s_ref[...]

    return _pallas(
        body, name=name,
        grid_spec=pltpu.PrefetchScalarGridSpec(
            num_scalar_prefetch=1, grid=(r // tr,),
            in_specs=[pl.BlockSpec(memory_space=pl.ANY), pl.BlockSpec((tr, n), lambda i, p: (i, 0))],
            out_specs=pl.BlockSpec((tr, n), lambda i, p: (i, p[0]))),
        out_shape=jax.ShapeDtypeStruct(full.shape, full.dtype), input_output_aliases={1: 0},
        compiler_params=_params(1),
    )(_chip_index(), full, shard)


def _matmul_with_gather(a, b, kinds, shards, *, out_dtype, bm, bn, name):
    (m, k), n = a.shape, b.shape[1]
    nw = len(shards)
    nj, ni = n // bn, m // bm

    def body(*refs):
        a_ref, b_ref, ins, o_ref = refs[0], refs[1], refs[2:2 + nw], refs[2 + nw]
        outs, sems = refs[3 + nw:3 + 2 * nw], refs[3 + 2 * nw:]
        start, finish = _gather_plan(kinds, ins, outs, sems)
        j, i = pl.program_id(0), pl.program_id(1)

        @pl.when(jnp.logical_and(j == 0, i == 0))
        def _():
            start()

        o_ref[...] = _dot(a_ref[...], b_ref[...]).astype(out_dtype)

        @pl.when(jnp.logical_and(j == nj - 1, i == ni - 1))
        def _():
            finish()

    return _pallas(
        body, name=name, grid=(nj, ni),
        in_specs=[pl.BlockSpec((bm, k), lambda j, i: (i, 0)), pl.BlockSpec((k, bn), lambda j, i: (0, j))] + [ANY] * nw,
        out_specs=[pl.BlockSpec((bm, bn), lambda j, i: (i, j))] + [ANY] * nw,
        out_shape=[jax.ShapeDtypeStruct((m, n), out_dtype)] + [_gathered_shape(kd, s) for kd, s in zip(kinds, shards)],
        scratch_shapes=_gather_sems(nw, 0), compiler_params=_params(2),
    )(a, b, *shards)


def _other_half(a, c):
    axis = a.ndim - 2
    h = a.shape[axis] // 2
    rows = pl.ds(pl.multiple_of((1 - c) * h, 8), h)
    return a.at[rows, :] if a.ndim == 2 else a.at[:, rows, :]


def _half_shape(a):
    axis = a.ndim - 2
    return jax.ShapeDtypeStruct(a.shape[:axis] + (a.shape[axis] // 2,) + a.shape[axis + 1:], a.dtype)


def _pair_swap(bigs, *, name):
    nb = len(bigs)

    def body(*refs):
        ins, outs, send_sems, recv_sems = refs[:nb], refs[nb:2 * nb], refs[2 * nb], refs[2 * nb + 1]
        x, y, c = lax.axis_index("x"), lax.axis_index("y"), lax.axis_index("c")
        pair = [_remote(_other_half(a, c), q, (send_sems, recv_sems), j, (x, y, 1 - c))
                for j, (a, q) in enumerate(zip(ins, outs))]
        for cp in pair:
            cp.start()
        for cp in pair:
            cp.wait()

    return _pallas(
        body, name=name, in_specs=[ANY] * nb, out_specs=[ANY] * nb, out_shape=[_half_shape(a) for a in bigs],
        scratch_shapes=[pltpu.SemaphoreType.DMA((nb,)), pltpu.SemaphoreType.DMA((nb,))],
    )(*bigs)


def _pair_exchange(bigs, gsmall, grep):
    nb = len(bigs)

    def body(*refs):
        ins, sm, rp = refs[:nb], refs[nb], refs[nb + 1]
        outs, osm, orp = refs[nb + 2:2 * nb + 2], refs[2 * nb + 2], refs[2 * nb + 3]
        pair_s, pair_r, send_sems, recv_sems, local_sems = refs[2 * nb + 4:]
        x, y, c = lax.axis_index("x"), lax.axis_index("y"), lax.axis_index("c")
        me = 4 * x + 2 * y + c
        chip = 2 * x + y
        sib = (x, y, 1 - c)
        pair = [_remote(_other_half(a, c), q, (pair_s, pair_r), j, sib) for j, (a, q) in enumerate(zip(ins, outs))]
        for cp in pair:
            cp.start()
        own = [pltpu.make_async_copy(sm.at[chip], osm.at[me], local_sems.at[0]),
               pltpu.make_async_copy(rp, orp.at[me], local_sems.at[1])]
        for cp in own:
            cp.start()
        peers = []
        for k in range(7):
            fx, fy, fc = ((k + 1) >> 2) & 1, ((k + 1) >> 1) & 1, (k + 1) & 1
            peers.append((1 - x if fx else x, 1 - y if fy else y, 1 - c if fc else c))
        sends = []
        for k, (px, py, pc) in enumerate(peers):
            sends.append(_remote(sm.at[2 * px + py], osm.at[me], (send_sems, recv_sems), 2 * k, (px, py, pc)))
            sends.append(_remote(rp, orp.at[me], (send_sems, recv_sems), 2 * k + 1, (px, py, pc)))
        for cp in sends:
            cp.start()
        for k, (px, py, pc) in enumerate(peers):
            slot = 4 * px + 2 * py + pc
            _remote(sm.at[chip], osm.at[slot], (send_sems, recv_sems), 2 * k, (px, py, pc)).wait_recv()
            _remote(rp, orp.at[slot], (send_sems, recv_sems), 2 * k + 1, (px, py, pc)).wait_recv()
        for cp in pair:
            cp.wait_recv()
        for cp in pair + sends:
            cp.wait_send()
        for cp in own:
            cp.wait()

    return _pallas(
        body, name="pair_exchange", in_specs=[ANY] * (nb + 2), out_specs=[ANY] * (nb + 2),
        out_shape=[_half_shape(a) for a in bigs]
        + [jax.ShapeDtypeStruct((8,) + gsmall.shape[1:], F32), jax.ShapeDtypeStruct((8,) + grep.shape, F32)],
        scratch_shapes=[pltpu.SemaphoreType.DMA((max(nb, 1),)), pltpu.SemaphoreType.DMA((max(nb, 1),)),
                        pltpu.SemaphoreType.DMA((14,)), pltpu.SemaphoreType.DMA((14,)),
                        pltpu.SemaphoreType.DMA((2,))],
    )(*bigs, gsmall, grep)


def _core_index():
    return lax.axis_index("c").astype(jnp.int32).reshape(1)


def _half_add(full, other, *, axis, block, name):
    nd = full.ndim
    nblk = other.shape[axis] // block[axis]
    grid = tuple(other.shape[d] // block[d] for d in range(nd))

    def body(c_ref, f_ref, o_ref, out_ref):
        out_ref[...] = (f_ref[...] + o_ref[...]).astype(BF16)

    def full_map(*idx):
        ids, c_ref = list(idx[:nd]), idx[nd]
        ids[axis] = ids[axis] + c_ref[0] * nblk
        return tuple(ids)

    def plain_map(*idx):
        return tuple(idx[:nd])

    return _pallas(
        body, name=name,
        grid_spec=pltpu.PrefetchScalarGridSpec(
            num_scalar_prefetch=1, grid=grid,
            in_specs=[pl.BlockSpec(block, full_map), pl.BlockSpec(block, plain_map)],
            out_specs=pl.BlockSpec(block, plain_map)),
        out_shape=jax.ShapeDtypeStruct(other.shape, BF16), compiler_params=_params(nd),
    )(_core_index(), full, other)


NPEER = 3


def _landing_shape(kind, src):
    if kind == "cols":
        return jax.ShapeDtypeStruct((NPEER, src.shape[0], src.shape[1] // NCHIP), src.dtype)
    return jax.ShapeDtypeStruct((NPEER,) + src.shape[1:], src.dtype)


def _chip_plan(kinds, srcs, lands, sems):
    nw = len(srcs)

    def place():
        x, y, c = lax.axis_index("x"), lax.axis_index("y"), lax.axis_index("c")
        return 2 * x + y, c, [(1 - x, y), (x, 1 - y), (1 - x, 1 - y)]

    def piece(j, chip):
        if kinds[j] == "cols":
            n = srcs[j].shape[1] // NCHIP
            return srcs[j].at[:, pl.ds(pl.multiple_of(chip * n, LANE), n)]
        return srcs[j].at[chip]

    def my_sends(c, peers):
        return [_remote(piece(j, 2 * px + py), lands[j].at[k], sems, nw * k + j, (px, py, c))
                for k, (px, py) in enumerate(peers) for j in range(nw)]

    def start():
        _, c, peers = place()
        for cp in my_sends(c, peers):
            cp.start()

    def finish():
        me, c, peers = place()
        for k, (px, py) in enumerate(peers):
            for j in range(nw):
                _remote(piece(j, me), lands[j].at[k], sems, nw * k + j, (px, py, c)).wait_recv()
        for cp in my_sends(c, peers):
            cp.wait_send()

    return start, finish


def _chip_sems(nw):
    return [pltpu.SemaphoreType.DMA((NPEER * nw,)), pltpu.SemaphoreType.DMA((NPEER * nw,))]


def _chip_exchange(kinds, srcs):
    nw = len(srcs)

    def body(*refs):
        start, finish = _chip_plan(kinds, refs[:nw], refs[nw:2 * nw], refs[2 * nw:])
        start()
        finish()

    return _pallas(
        body, name="chip_exchange", in_specs=[ANY] * nw, out_specs=[ANY] * nw,
        out_shape=[_landing_shape(kd, s) for kd, s in zip(kinds, srcs)], scratch_shapes=_chip_sems(nw),
    )(*srcs)


def _chip_index():
    return (2 * lax.axis_index("x") + lax.axis_index("y")).astype(jnp.int32).reshape(1)


def _chip_sum(own, slots, *, own_block, own_map, block, name):
    npeer = slots.shape[0]
    shape = slots.shape[1:]
    grid = (shape[0] // block[0], shape[1] // block[1])

    def body(p_ref, own_ref, s_ref, o_ref):
        acc = own_ref[...].reshape(block).astype(F32)
        for q in range(npeer):
            acc = acc + s_ref[q].astype(F32)
        o_ref[...] = acc

    return _pallas(
        body, name=name,
        grid_spec=pltpu.PrefetchScalarGridSpec(
            num_scalar_prefetch=1, grid=grid,
            in_specs=[pl.BlockSpec(own_block, own_map),
                      pl.BlockSpec((npeer,) + block, lambda i, j, p: (0, i, j))],
            out_specs=pl.BlockSpec(block, lambda i, j, p: (i, j))),
        out_shape=jax.ShapeDtypeStruct(shape, F32), compiler_params=_params(2),
    )(_chip_index(), own, slots)


def _pair_share(r_in0, r_in1, r_out0, r_out1):
    def body(a0, a1, b0, b1, g0, g1, h0, h1, send_sems, recv_sems):
        x, y, c = lax.axis_index("x"), lax.axis_index("y"), lax.axis_index("c")
        sib = (x, y, 1 - c)
        sends = [_remote(s, d, (send_sems, recv_sems), j, sib)
                 for j, (s, d) in enumerate(zip([a0, a1, b0, b1], [g0, g1, h0, h1]))]
        for cp in sends:
            cp.start()
        for cp in sends:
            cp.wait()

    return _pallas(
        body, name="pair_share", in_specs=[ANY] * 4, out_specs=[ANY] * 4,
        out_shape=[jax.ShapeDtypeStruct(r.shape, F32) for r in (r_in0, r_in1, r_out0, r_out1)],
        scratch_shapes=[pltpu.SemaphoreType.DMA((4,)), pltpu.SemaphoreType.DMA((4,))],
    )(r_in0, r_in1, r_out0, r_out1)


def _adam_math(g, w, m, v):
    c1 = 1.0 - ADAM_B1 ** ADAM_STEP
    c2 = 1.0 - ADAM_B2 ** ADAM_STEP
    m2 = ADAM_B1 * m + (1.0 - ADAM_B1) * g
    v2 = ADAM_B2 * v + (1.0 - ADAM_B2) * (g * g)
    delta = -ADAM_LR * ((m2 / c1) / (jnp.sqrt(v2 / c2) + ADAM_EPS) + ADAM_WD * w)
    return delta, m2, v2


def _adamw_nat(g_mine, g_sib, w, m, v, *, name, tr):
    rows, cw = w.shape
    nt = g_mine.shape[0] // tr

    def body(c_ref, gm_ref, gs_ref, w_ref, m_ref, v_ref, go_ref, d_ref, nm_ref, nv_ref):
        mine = pl.program_id(0) // nt == c_ref[0]
        gv = jnp.where(mine, gm_ref[...], gs_ref[...])[:, 0:cw]
        delta, m2, v2 = _adam_math(gv, w_ref[...], m_ref[...], v_ref[...])
        go_ref[...] = gv
        d_ref[...] = delta
        nm_ref[...] = m2
        nv_ref[...] = v2

    def mine_map(i, c_ref):
        return (jnp.where(i // nt == c_ref[0], i % nt, 0), 0)

    def sib_map(i, c_ref):
        return (jnp.where(i // nt == c_ref[0], 0, i % nt), 0)

    row = pl.BlockSpec((tr, cw), lambda i, c_ref: (i, 0))
    gspec = (tr, g_mine.shape[1])
    out = jax.ShapeDtypeStruct((rows, cw), F32)
    return _pallas(
        body, name=name,
        grid_spec=pltpu.PrefetchScalarGridSpec(
            num_scalar_prefetch=1, grid=(rows // tr,),
            in_specs=[pl.BlockSpec(gspec, mine_map), pl.BlockSpec(gspec, sib_map), row, row, row],
            out_specs=[row, row, row, row]),
        out_shape=[out, out, out, out], compiler_params=_params(1),
    )(_core_index(), g_mine, g_sib, w, m, v)


def _adamw(slots, w, m, v, *, name, tr):
    nd, rows, _ = slots.shape
    c1 = 1.0 - ADAM_B1 ** ADAM_STEP
    c2 = 1.0 - ADAM_B2 ** ADAM_STEP

    def body(s_ref, w_ref, m_ref, v_ref, g_ref, d_ref, nm_ref, nv_ref):
        g = s_ref[0]
        for d in range(1, nd):
            g = g + s_ref[d]
        m2 = ADAM_B1 * m_ref[...] + (1.0 - ADAM_B1) * g
        v2 = ADAM_B2 * v_ref[...] + (1.0 - ADAM_B2) * (g * g)
        g_ref[...] = g
        nm_ref[...] = m2
        nv_ref[...] = v2
        d_ref[...] = -ADAM_LR * ((m2 / c1) / (jnp.sqrt(v2 / c2) + ADAM_EPS) + ADAM_WD * w_ref[...])

    row = pl.BlockSpec((tr, LANE), lambda i: (i, 0))
    out = jax.ShapeDtypeStruct((rows, LANE), F32)
    return _pallas(
        body, name=name, grid=(rows // tr,),
        in_specs=[pl.BlockSpec((nd, tr, LANE), lambda i: (0, i, 0)), row, row, row],
        out_specs=[row, row, row, row], out_shape=[out, out, out, out], compiler_params=_params(1),
    )(slots, w, m, v)


def _rows(a):
    return a.reshape(-1, LANE)


def _pad_rows(a, mult):
    pad = (-a.shape[0]) % mult
    return jnp.pad(a, ((0, pad), (0, 0))) if pad else a


def _pack(parts, mult):
    return _pad_rows(jnp.concatenate([_rows(p) for p in parts], axis=0), mult)


def _unpack(slab, shapes):
    out, r0 = [], 0
    for shp in shapes:
        n = 1
        for s in shp:
            n *= s
        r = n // LANE
        out.append(slab[r0:r0 + r].reshape(shp))
        r0 += r
    return out


def _pack_rep(vecs, scal):
    srow = jnp.concatenate([s.reshape(-1) for s in scal] + [jnp.zeros((LANE - 3 * SSM_H,), F32)]).reshape(1, LANE)
    return _pad_rows(jnp.concatenate([_rows(vv) for vv in vecs] + [srow], axis=0), 8)


def _unpack_rep(slab, vec_shapes, scal_shape):
    vecs, r0 = [], 0
    for shp in vec_shapes:
        vecs.append(slab[r0:r0 + 8].reshape(shp))
        r0 += 8
    srow = slab[r0]
    scal = [srow[i * SSM_H:(i + 1) * SSM_H].reshape(scal_shape) for i in range(3)]
    return vecs, scal


def kernel(x, ev_norm_w, ev_w_in, ev_dw_w, ev_dw_b, ev_ln_w, ev_ln_b, ev_w_out, od_norm_w, od_w_in, od_conv_w, od_conv_b, od_dt_bias, od_a_log, od_d, od_gnorm_w, od_w_out, final_norm_w, loss_target, m_ev_norm_w, m_ev_w_in, m_ev_dw_w, m_ev_dw_b, m_ev_ln_w, m_ev_ln_b, m_ev_w_out, m_od_norm_w, m_od_w_in, m_od_conv_w, m_od_conv_b, m_od_dt_bias, m_od_a_log, m_od_d, m_od_gnorm_w, m_od_w_out, m_final_norm_w, v_ev_norm_w, v_ev_w_in, v_ev_dw_w, v_ev_dw_b, v_ev_ln_w, v_ev_ln_b, v_ev_w_out, v_od_norm_w, v_od_w_in, v_od_conv_w, v_od_conv_b, v_od_dt_bias, v_od_a_log, v_od_d, v_od_gnorm_w, v_od_w_out, v_final_norm_w):
    nb, seq, d = x.shape
    t = nb * seq
    nchip = 4
    xf = x.reshape(t, d)
    tgt = loss_target.reshape(t, d)

    big_w = [ev_w_in[0], od_w_in[0], ev_w_out[0], od_w_out[0]]
    small_w = [ev_dw_w[0], od_norm_w[0], od_conv_w[0], od_conv_b[0], od_gnorm_w[0]]
    small_shapes = [a.shape for a in small_w]
    big_b = [a.astype(BF16) for a in big_w]
    small_slab = _pack(small_w, 8)
    w_in0, w_out0, gath_small = _gather_shards(("cols", "rows"), [big_b[0], big_b[2]], small_slab)
    chip = 2 * lax.axis_index("x") + lax.axis_index("y")
    w_in0 = _place_cols(w_in0, big_b[0], name="place_w_in0")
    w_out0 = lax.dynamic_update_slice(w_out0, big_b[2], (chip * big_b[2].shape[0], 0))
    gath_small = lax.dynamic_update_slice(gath_small, small_slab[None], (chip, 0, 0))
    per_chip = [_unpack(gath_small[p], small_shapes) for p in range(nchip)]

    def cat(idx, axis):
        return jnp.concatenate([per_chip[p][idx] for p in range(nchip)], axis=axis)

    dw_w = jnp.pad(cat(0, 1), ((0, HALO - CONF_K), (0, 0)))
    dw_w8 = jnp.repeat(dw_w, SUB, axis=0)
    n1_w = cat(1, 0).reshape(1, d)
    conv_w = jnp.pad(cat(2, 1), ((0, PH - SSM_K), (0, 0)))
    conv_b = cat(3, 0).reshape(1, XBC)
    gn_w = cat(4, 0).reshape(1, D_INNER)

    def lanes(a):
        return jnp.pad(a.reshape(1, -1), ((0, 0), (0, LANE - a.size)))

    dt_bias, a_log = lanes(od_dt_bias), lanes(od_a_log)
    d_x = jnp.repeat(od_d.reshape(-1), SSM_P).reshape(1, D_INNER)
    hid = lax.broadcasted_iota(jnp.int32, (SSM_H, D_INNER), 1) // SSM_P
    ex = (hid == lax.broadcasted_iota(jnp.int32, (SSM_H, D_INNER), 0)).astype(BF16)
    ex_t = jnp.pad(ex.T, ((0, 0), (0, LANE - SSM_H)))
    fn_w = final_norm_w.reshape(1, d)

    n0 = _rms_fwd(xf, ev_norm_w, name="rms_fwd0")
    proj0, w_in1g, w_out1 = _matmul_with_gather(n0, w_in0, ("slab", "rows"), [big_b[1], big_b[3]],
                                                out_dtype=BF16, bm=512, bn=1024, name="in_proj0")
    w_in1g = lax.dynamic_update_slice(w_in1g, big_b[1][None], (chip, 0, 0))
    w_out1 = lax.dynamic_update_slice(w_out1, big_b[3], (chip * big_b[3].shape[0], 0))
    w_in1 = jnp.pad(jnp.concatenate([w_in1g[p] for p in range(nchip)], axis=1),
                    ((0, 0), (0, IN_ODD_PAD - IN_ODD)))
    y_conv, u2 = _conf_fwd(proj0, dw_w8, ev_dw_b, ev_ln_w, ev_ln_b, seq)
    o_att, y_att = _sba_fwd(proj0, nb, seq)
    ycat0 = jnp.concatenate([y_conv, y_att], axis=1)
    h1 = _matmul(ycat0, w_out0, mode="nn", out_dtype=F32, bm=512, bn=d, bk=D_INNER, name="out_proj0", residual=xf)
    n1 = _rms_fwd(h1, n1_w, name="rms_fwd1")
    proj1 = _matmul(n1, w_in1, mode="nn", out_dtype=F32, bm=512, bn=768, bk=d, name="in_proj1", n_major=True)
    xbc_c = _xconv_fwd(proj1, conv_w, conv_b, seq)
    dt = _dt_fwd(proj1, dt_bias)
    y_ssd, states = _ssd_fwd(xbc_c, dt, a_log, ex, nb, seq)
    yg = _gate_fwd(y_ssd, xbc_c, proj1, d_x, gn_w)
    h2 = _matmul(yg, w_out1, mode="nn", out_dtype=F32, bm=512, bn=d, bk=D_INNER, name="out_proj1", residual=h1)
    dh2, g_fn, loss_part = _final_loss(h2, fn_w, tgt)

    dyg = _matmul(dh2, w_out1, mode="nt", out_dtype=BF16, bm=512, bn=1024, bk=d, name="d_out_proj1")
    g_w_out1 = _matmul(yg, dh2, mode="tn", out_dtype=F32, bm=1024, bn=d, bk=1024, name="dw_out_proj1")
    dy_ssd, dz, g_gn, g_dx = _gate_bwd(dyg, y_ssd, xbc_c, proj1, d_x, gn_w)
    dxbc_c, ddt, g_a = _ssd_bwd(xbc_c, dt, a_log, ex, ex_t, states, dy_ssd, d_x, nb, seq)
    dproj1, g_conv_w, g_conv_b = _xconv_bwd(proj1, dxbc_c, conv_w, conv_b, dz, seq)
    dproj1, g_dt_bias = _dt_bwd(proj1, dt_bias, ddt, dproj1)
    dn1 = _matmul(dproj1, w_in1, mode="nt", out_dtype=BF16, bm=1024, bn=d, bk=1792, name="d_in_proj1")
    g_w_in1 = _matmul(n1, dproj1, mode="tn", out_dtype=F32, bm=d, bn=1792, bk=1024, name="dw_in_proj1")
    dh1, g_n1 = _rms_bwd(dn1, h1, n1_w, dh2, name="rms_bwd1")

    dycat0 = _matmul(dh1, w_out0, mode="nt", out_dtype=BF16, bm=512, bn=1024, bk=d, name="d_out_proj0")
    g_w_out0 = _matmul(ycat0, dh1, mode="tn", out_dtype=F32, bm=1024, bn=d, bk=1024, name="dw_out_proj0")
    dq, dk, dv, dga = _sba_bwd(proj0, o_att, dycat0, nb, seq)
    ro = D_INNER // nchip
    n1 = IN_ODD // nchip
    n1p = -(-n1 // LANE) * LANE
    g_w_out1c = g_w_out1.reshape(nchip, ro, d)
    q_in1, q_out1 = _pair_swap([g_w_in1, g_w_out1c], name="pair_swap_l1")
    s_in1n = _half_add(g_w_in1, q_in1, axis=0, block=(128, IN_ODD_PAD), name="half_add_in1")
    s_in1 = jnp.stack([jnp.pad(s_in1n[:, p * n1:(p + 1) * n1], ((0, 0), (0, n1p - n1))) for p in range(nchip)])
    s_out1 = _half_add(g_w_out1c, q_out1, axis=1, block=(1, ro // 2, d), name="half_add_out1")
    dpc, g_dw_w, g_dw_b, g_ln_w, g_ln_b, l_in1, l_out1 = _conf_bwd(
        proj0, u2, dycat0, dw_w8, ev_ln_w, ev_ln_b, seq, ("slab", "slab"), [s_in1, s_out1])
    dproj0 = jnp.concatenate([dpc, dq, dk.astype(BF16), dv.astype(BF16), dga], axis=1)
    g_w_in0 = _matmul(n0, dproj0, mode="tn", out_dtype=F32, bm=d, bn=1792, bk=1024, name="dw_in_proj0")
    g_w_out0c = g_w_out0.reshape(nchip, ro, d)
    q_in0, q_out0 = _pair_swap([g_w_in0, g_w_out0c], name="pair_swap_l0")
    s_in0 = _half_add(g_w_in0, q_in0, axis=0, block=(128, IN_EVEN), name="half_add_in0")
    s_out0 = _half_add(g_w_out0c, q_out0, axis=1, block=(1, ro // 2, d), name="half_add_out0")
    dn0, l_in0, l_out0 = _matmul(dproj0, w_in0, mode="nt", out_dtype=BF16, bm=1024, bn=d, bk=1792, name="d_in_proj0",
                                 comm_kinds=("cols", "slab"), comm_srcs=[s_in0, s_out0])
    grad_x, g_n0 = _rms_bwd(dn0, xf, ev_norm_w, dh1, name="rms_bwd0")

    g_dw_w = g_dw_w.reshape(HALO, SUB, CONV_W).sum(axis=1)[0:CONF_K]
    g_dw_b, g_ln_w, g_ln_b = (a.sum(axis=0, keepdims=True) for a in (g_dw_b, g_ln_w, g_ln_b))
    g_conv_w = g_conv_w.reshape(PH, SUB, XBC).sum(axis=1)[0:SSM_K]
    g_conv_b = g_conv_b.sum(axis=0, keepdims=True)
    a_neg = -jnp.exp(od_a_log.reshape(-1))
    g_a_log = g_a[0, 0:SSM_H] * a_neg
    g_d = g_dx.reshape(SSM_H, SSM_P).sum(axis=1)

    def chip_slab_small(p):
        c0, c1, c2, c3 = CONV_W // nchip, d // nchip, XBC // nchip, D_INNER // nchip
        return _pack([g_dw_w[:, p * c0:(p + 1) * c0], g_n1[0, p * c1:(p + 1) * c1],
                      g_conv_w[:, p * c2:(p + 1) * c2], g_conv_b[0, p * c2:(p + 1) * c2],
                      g_gn[0, p * c3:(p + 1) * c3]], 8)

    gsmall = jnp.stack([chip_slab_small(p) for p in range(nchip)])
    rep_vec_shapes = [ev_norm_w.shape, ev_dw_b.shape, ev_ln_w.shape, ev_ln_b.shape, final_norm_w.shape]
    grep = _pack_rep([g_n0, g_dw_b, g_ln_w, g_ln_b, g_fn], [g_dt_bias[0, 0:SSM_H], g_a_log, g_d])

    ssmall, srep = _pair_exchange([], gsmall, grep)
    r_in0 = _chip_sum(s_in0, l_in0, own_block=(128, IN_EVEN // nchip), own_map=lambda i, j, p: (i, p[0]),
                      block=(128, IN_EVEN // nchip), name="chip_sum_in0")
    r_in1 = _chip_sum(s_in1, l_in1, own_block=(1, 256, n1p), own_map=lambda i, j, p: (p[0], i, 0),
                      block=(256, n1p), name="chip_sum_in1")
    r_out0 = _chip_sum(s_out0, l_out0, own_block=(1, ro // 2, d), own_map=lambda i, j, p: (p[0], 0, 0),
                       block=(ro // 2, d), name="chip_sum_out0")
    r_out1 = _chip_sum(s_out1, l_out1, own_block=(1, ro // 2, d), own_map=lambda i, j, p: (p[0], 0, 0),
                       block=(ro // 2, d), name="chip_sum_out1")
    big_r = [r_in0, r_in1, r_out0, r_out1]
    big_q = _pair_share(*big_r)

    big_m = [m_ev_w_in[0], m_od_w_in[0], m_ev_w_out[0], m_od_w_out[0]]
    big_v = [v_ev_w_in[0], v_od_w_in[0], v_ev_w_out[0], v_od_w_out[0]]
    big_names = ["adamw_in0", "adamw_in1", "adamw_out0", "adamw_out1"]
    out_bigs = [_adamw_nat(gm, gs, w, m, v, name=nm, tr=128)
                for gm, gs, w, m, v, nm in zip(big_r, big_q, big_w, big_m, big_v, big_names)]

    def upd(slots, ws, ms, vs, packer, name, tr):
        return _adamw(slots, packer(ws), packer(ms), packer(vs), name=name, tr=tr)

    small_m = [m_ev_dw_w[0], m_od_norm_w[0], m_od_conv_w[0], m_od_conv_b[0], m_od_gnorm_w[0]]
    small_v = [v_ev_dw_w[0], v_od_norm_w[0], v_od_conv_w[0], v_od_conv_b[0], v_od_gnorm_w[0]]
    out_small = upd(ssmall, small_w, small_m, small_v, lambda a: _pack(a, 8), "adamw_small", ssmall.shape[1])

    def rep_pack(a):
        return _pack_rep(a[0:5], a[5:8])

    rep_w = [ev_norm_w, ev_dw_b, ev_ln_w, ev_ln_b, final_norm_w, od_dt_bias, od_a_log, od_d]
    rep_m = [m_ev_norm_w, m_ev_dw_b, m_ev_ln_w, m_ev_ln_b, m_final_norm_w, m_od_dt_bias, m_od_a_log, m_od_d]
    rep_v = [v_ev_norm_w, v_ev_dw_b, v_ev_ln_w, v_ev_ln_b, v_final_norm_w, v_od_dt_bias, v_od_a_log, v_od_d]
    out_rep = upd(srep, rep_w, rep_m, rep_v, rep_pack, "adamw_rep", srep.shape[1])

    results = []
    for kind in range(4):
        bw = [o[kind].reshape((1,) + o[kind].shape) for o in out_bigs]
        sw = _unpack(out_small[kind], small_shapes)
        vecs, scal = _unpack_rep(out_rep[kind], rep_vec_shapes, od_dt_bias.shape)
        results.append([
            vecs[0], bw[0], sw[0].reshape(ev_dw_w.shape), vecs[1], vecs[2], vecs[3], bw[2],
            sw[1].reshape(od_norm_w.shape), bw[1], sw[2].reshape(od_conv_w.shape), sw[3].reshape(od_conv_b.shape),
            scal[0], scal[1], scal[2], sw[4].reshape(od_gnorm_w.shape), bw[3], vecs[4]])
    loss = lax.psum(loss_part[0, 0], ("x", "y", "c"))
    return (loss, grad_x.reshape(x.shape), *results[0], *results[1], *results[2], *results[3])
```

```python
import jax
import jax.numpy as jnp
from jax import lax
from jax.experimental import pallas as pl
from jax.experimental.pallas import tpu as pltpu

F32 = jnp.float32
BF16 = jnp.bfloat16

D_MODEL = 1024
CONV_W = 1024
ATT_W = 1024
HEAD_DIM = 128
N_HEADS = 8
CONF_K = 31
IN_EVEN = 7168
D_INNER = 2048
SSM_P = 64
SSM_H = 32
SSM_G = 4
SSM_R = SSM_H // SSM_G
SSM_N = 128
SSM_K = 4
CHUNK = 128
XBC = D_INNER + 2 * SSM_G * SSM_N
IN_ODD = D_INNER + XBC + SSM_H
IN_ODD_PAD = 5376
EPS = 1e-6
QB = 128
NEG_CUT = -100.0

ADAM_LR = 0.001
ADAM_B1 = 0.9
ADAM_B2 = 0.999
ADAM_EPS = 1e-08
ADAM_WD = 0.01
ADAM_STEP = 10

LANE = 128
VMEM_LIMIT = 56 * 1024 * 1024
MESH = pl.DeviceIdType.MESH

NN = (((1,), (0,)), ((), ()))
NT = (((1,), (1,)), ((), ()))
TN = (((0,), (0,)), ((), ()))


def _pallas(body, **kw):
    return pl.pallas_call(body, **kw)


def _params(n_axes):
    return pltpu.CompilerParams(dimension_semantics=("arbitrary",) * n_axes, vmem_limit_bytes=VMEM_LIMIT)


def _dot(a, b, dims=NN):
    return lax.dot_general(a.astype(BF16), b.astype(BF16), dims, preferred_element_type=F32)


def _parts(x):
    h = x.astype(BF16)
    r = x - h.astype(F32)
    m = r.astype(BF16)
    l = (r - m.astype(F32)).astype(BF16)
    return (h, m, l)


def _dotx(x, e01, dims=NN):
    acc = None
    for p in _parts(x):
        t = lax.dot_general(p, e01, dims, preferred_element_type=F32)
        acc = t if acc is None else acc + t
    return acc


def _dotx2(x, e01, dims=NN):
    h = x.astype(BF16)
    l = (x - h.astype(F32)).astype(BF16)
    return (lax.dot_general(h, e01, dims, preferred_element_type=F32)
            + lax.dot_general(l, e01, dims, preferred_element_type=F32))


def _xdot(e01, x, dims=NN):
    acc = None
    for p in _parts(x):
        t = lax.dot_general(e01, p, dims, preferred_element_type=F32)
        acc = t if acc is None else acc + t
    return acc


def _f32(x):
    return x.astype(F32)


def _sigmoid(x):
    return 1.0 / (1.0 + jnp.exp(-x))


def _dsilu(x, s):
    return s * (1.0 + x * (1.0 - s))


def _matmul(a, b, *, mode, out_dtype, bm, bn, bk, name, residual=None, n_major=False, comm_kinds=(), comm_srcs=()):
    if mode == "nn":
        (m, k), n = a.shape, b.shape[1]
        a_blk, a_map = (bm, bk), lambda i, j, kk: (i, kk)
        b_blk, b_map = (bk, bn), lambda i, j, kk: (kk, j)
        dims = NN
    elif mode == "nt":
        (m, k), n = a.shape, b.shape[0]
        a_blk, a_map = (bm, bk), lambda i, j, kk: (i, kk)
        b_blk, b_map = (bn, bk), lambda i, j, kk: (j, kk)
        dims = NT
    else:
        (k, m), n = a.shape, b.shape[1]
        a_blk, a_map = (bk, bm), lambda i, j, kk: (kk, i)
        b_blk, b_map = (bk, bn), lambda i, j, kk: (kk, j)
        dims = TN
    bm, bn, bk = min(bm, m), min(bn, n), min(bk, k)
    if mode != "nn":
        a_blk = (bm, bk) if mode == "nt" else (bk, bm)
        b_blk = (bn, bk) if mode == "nt" else (bk, bn)
    else:
        a_blk, b_blk = (bm, bk), (bk, bn)
    assert m % bm == 0 and n % bn == 0 and k % bk == 0, (name, m, n, k)
    nk = k // bk
    has_res = residual is not None

    def order(f):
        return (lambda j, i, kk: f(i, j, kk)) if n_major else f

    nw = len(comm_srcs)
    grid = (n // bn, m // bm, nk) if n_major else (m // bm, n // bn, nk)

    def body(*refs):
        a_ref, b_ref = refs[0], refs[1]
        r_ref = refs[2] if has_res else None
        n_in = 2 + has_res + nw
        o_ref = refs[n_in]
        n_out = n_in + 1 + nw

        def finish(r):
            if has_res:
                r = r + r_ref[...]
            o_ref[...] = r.astype(out_dtype)

        def compute():
            if nk == 1:
                finish(_dot(a_ref[...], b_ref[...], dims))
                return
            acc_ref = refs[n_out]
            kk = pl.program_id(2)

            @pl.when(kk == 0)
            def _():
                acc_ref[...] = jnp.zeros_like(acc_ref)

            acc_ref[...] += _dot(a_ref[...], b_ref[...], dims)

            @pl.when(kk == nk - 1)
            def _():
                finish(acc_ref[...])

        if not nw:
            compute()
            return
        start, done = _chip_plan(comm_kinds, refs[2 + has_res:n_in], refs[n_in + 1:n_out], refs[n_out + (nk > 1):])
        ids = [pl.program_id(ax) for ax in range(3)]

        @pl.when(jnp.logical_and(jnp.logical_and(ids[0] == 0, ids[1] == 0), ids[2] == 0))
        def _():
            start()

        compute()

        @pl.when(jnp.logical_and(jnp.logical_and(ids[0] == grid[0] - 1, ids[1] == grid[1] - 1), ids[2] == grid[2] - 1))
        def _():
            done()

    in_specs = [pl.BlockSpec(a_blk, order(a_map)), pl.BlockSpec(b_blk, order(b_map))]
    args = [a, b]
    out_map = order(lambda i, j, kk: (i, j))
    if has_res:
        in_specs.append(pl.BlockSpec((bm, bn), out_map))
        args.append(residual)
    any_spec = pl.BlockSpec(memory_space=pl.ANY)
    out_specs = [pl.BlockSpec((bm, bn), out_map)] + [any_spec] * nw
    out_shape = [jax.ShapeDtypeStruct((m, n), out_dtype)] + [_landing_shape(kd, s) for kd, s in zip(comm_kinds, comm_srcs)]
    res = _pallas(
        body, name=name, grid=grid, in_specs=in_specs + [any_spec] * nw, out_specs=out_specs, out_shape=out_shape,
        scratch_shapes=([pltpu.VMEM((bm, bn), F32)] if nk > 1 else []) + (_chip_sems(nw) if nw else []),
        compiler_params=_params(3),
    )(*args, *comm_srcs)
    return res if nw else res[0]


def _rms_fwd(x, w, *, name, tm=512):
    t, d = x.shape

    def body(x_ref, w_ref, o_ref):
        xv = x_ref[...]
        r = lax.rsqrt(jnp.mean(xv * xv, axis=1, keepdims=True) + EPS)
        o_ref[...] = (xv * r * w_ref[...]).astype(BF16)

    return _pallas(
        body, name=name, grid=(t // tm,),
        in_specs=[pl.BlockSpec((tm, d), lambda i: (i, 0)), pl.BlockSpec((1, d), lambda i: (0, 0))],
        out_specs=pl.BlockSpec((tm, d), lambda i: (i, 0)),
        out_shape=jax.ShapeDtypeStruct((t, d), BF16), compiler_params=_params(1),
    )(x, w)


def _rms_bwd(dn, x, w, dres, *, name, tm=512):
    t, d = x.shape

    def body(dn_ref, x_ref, w_ref, dr_ref, dx_ref, dw_ref):
        i = pl.program_id(0)
        xv = x_ref[...]
        r = lax.rsqrt(jnp.mean(xv * xv, axis=1, keepdims=True) + EPS)
        xh = xv * r
        dy = dn_ref[...].astype(F32)
        g = dy * w_ref[...]
        dx_ref[...] = dr_ref[...] + r * (g - xh * jnp.mean(g * xh, axis=1, keepdims=True))

        @pl.when(i == 0)
        def _():
            dw_ref[...] = jnp.zeros_like(dw_ref)

        dw_ref[...] += jnp.sum(dy * xh, axis=0, keepdims=True)

    row = pl.BlockSpec((tm, d), lambda i: (i, 0))
    vec = pl.BlockSpec((1, d), lambda i: (0, 0))
    return _pallas(
        body, name=name, grid=(t // tm,), in_specs=[row, row, vec, row], out_specs=[row, vec],
        out_shape=[jax.ShapeDtypeStruct((t, d), F32), jax.ShapeDtypeStruct((1, d), F32)],
        compiler_params=_params(1),
    )(dn, x, w, dres)


def _final_loss(h, w, target, *, tm=512):
    t, d = h.shape

    def body(h_ref, w_ref, t_ref, dh_ref, dw_ref, loss_ref):
        i = pl.program_id(0)
        xv = h_ref[...]
        r = lax.rsqrt(jnp.mean(xv * xv, axis=1, keepdims=True) + EPS)
        xh = xv * r
        wv = w_ref[...]
        err = xh * wv - t_ref[...]
        dy = err * (1.0 / d)
        g = dy * wv
        dh_ref[...] = r * (g - xh * jnp.mean(g * xh, axis=1, keepdims=True))

        @pl.when(i == 0)
        def _():
            dw_ref[...] = jnp.zeros_like(dw_ref)
            loss_ref[...] = jnp.zeros_like(loss_ref)

        dw_ref[...] += jnp.sum(dy * xh, axis=0, keepdims=True)
        part = jnp.sum(jnp.sum(err * err, axis=1, keepdims=True), axis=0, keepdims=True)
        loss_ref[...] += part * (0.5 / d)

    row = pl.BlockSpec((tm, d), lambda i: (i, 0))
    vec = pl.BlockSpec((1, d), lambda i: (0, 0))
    return _pallas(
        body, name="final_loss", grid=(t // tm,), in_specs=[row, vec, row],
        out_specs=[row, vec, pl.BlockSpec((1, LANE), lambda i: (0, 0))],
        out_shape=[jax.ShapeDtypeStruct((t, d), F32), jax.ShapeDtypeStruct((1, d), F32),
                   jax.ShapeDtypeStruct((1, LANE), F32)],
        compiler_params=_params(1),
    )(h, w, target)


HALO = 32


SUB = 8
RC = 16


def _make_shifts(sh_ref, rows, shifts=tuple(range(1, SUB))):
    for s in shifts:
        sh_ref[s, 0:rows, :] = sh_ref[0, s:s + rows, :]


def _shifted(sh_ref, r0, j, rows):
    return sh_ref[j % SUB, pl.ds(r0 + (j - j % SUB), rows), :]


def _taps(w8_ref, sh_ref, r0, first, step, init):
    accs = [init] * (RC // SUB)
    for k in range(CONF_K):
        wk = w8_ref[k * SUB:(k + 1) * SUB, :]
        x = _shifted(sh_ref, r0, first + step * k, RC)
        accs = [a + wk * x[q * SUB:(q + 1) * SUB] for q, a in enumerate(accs)]
    return jnp.concatenate(accs, axis=0)


def _conf_fwd(proj, dw_w, dw_b, ln_w, ln_b, seq, *, tm=256):
    t = proj.shape[0]
    c = CONV_W
    tps = seq // tm
    hb = tm // HALO

    def body(a_ref, b_ref, g_ref, ha_ref, hb_ref, w_ref, wb_ref, lw_ref, lb_ref, y_ref, u2_ref, sh_ref):
        i = pl.program_id(0)
        keep = jnp.where(i % tps == 0, 0.0, 1.0)
        sh_ref[0, 0:HALO, :] = _f32(ha_ref[...]) * _sigmoid(_f32(hb_ref[...])) * keep
        sh_ref[0, HALO:HALO + tm, :] = _f32(a_ref[...]) * _sigmoid(_f32(b_ref[...]))
        _make_shifts(sh_ref, tm + HALO - SUB)

        def chunk(ci, carry):
            r0 = pl.multiple_of(ci * RC, RC)
            acc = _taps(w_ref, sh_ref, r0, HALO - CONF_K + 1, 1, jnp.broadcast_to(wb_ref[...], (SUB, c)))
            u2_ref[pl.ds(r0, RC), :] = acc
            mu = jnp.mean(acc, axis=1, keepdims=True)
            xc = acc - mu
            rs = lax.rsqrt(jnp.mean(xc * xc, axis=1, keepdims=True) + EPS)
            u3 = xc * rs * lw_ref[...] + lb_ref[...]
            gv = _f32(g_ref[pl.ds(r0, RC), :])
            y_ref[pl.ds(r0, RC), :] = (u3 * _sigmoid(u3) * gv * _sigmoid(gv)).astype(BF16)
            return carry

        lax.fori_loop(0, tm // RC, chunk, 0, unroll=2)

    def col(j):
        return pl.BlockSpec((tm, c), lambda i: (i, j))

    def prev(j):
        return pl.BlockSpec((HALO, c), lambda i: (jnp.maximum(i * hb - 1, 0), j))

    vec = pl.BlockSpec((1, c), lambda i: (0, 0))
    return _pallas(
        body, name="conf_fwd", grid=(t // tm,),
        in_specs=[col(0), col(1), col(2), prev(0), prev(1),
                  pl.BlockSpec((HALO * SUB, c), lambda i: (0, 0)), vec, vec, vec],
        out_specs=[pl.BlockSpec((tm, c), lambda i: (i, 0)), pl.BlockSpec((tm, c), lambda i: (i, 0))],
        out_shape=[jax.ShapeDtypeStruct((t, c), BF16), jax.ShapeDtypeStruct((t, c), F32)],
        scratch_shapes=[pltpu.VMEM((SUB, tm + HALO, c), F32)], compiler_params=_params(1),
    )(proj, proj, proj, proj, proj, dw_w, dw_b, ln_w, ln_b)


def _conf_bwd(proj, u2, dycat, dw_w, ln_w, ln_b, seq, comm_kinds, comm_srcs, *, tm=256):
    t = proj.shape[0]
    c = CONV_W
    tps = seq // tm
    hb = tm // HALO
    nhb = t // HALO
    nw = len(comm_srcs)
    nsteps = t // tm

    def fold(v):
        out = v[0:SUB]
        for q in range(1, RC // SUB):
            out = out + v[q * SUB:(q + 1) * SUB]
        return out

    def body(*refs):
        (a_ref, b_ref, g_ref, pa_ref, pb_ref, ng_ref, u2_ref, nu2_ref, dy_ref, ndy_ref,
         w_ref, lw_ref, lb_ref) = refs[:13]
        dp_ref, dww_ref, dwb_ref, dlw_ref, dlb_ref = refs[13 + nw:18 + nw]
        su_ref, sd_ref = refs[18 + 2 * nw:20 + 2 * nw]
        comm_start, comm_finish = _chip_plan(comm_kinds, refs[13:13 + nw], refs[18 + nw:18 + 2 * nw],
                                             refs[20 + 2 * nw:])
        i = pl.program_id(0)
        first = i % tps == 0
        last = i % tps == tps - 1

        @pl.when(i == 0)
        def _():
            comm_start()
            dww_ref[...] = jnp.zeros_like(dww_ref)
            dwb_ref[...] = jnp.zeros_like(dwb_ref)
            dlw_ref[...] = jnp.zeros_like(dlw_ref)
            dlb_ref[...] = jnp.zeros_like(dlb_ref)

        su_ref[0, 0:HALO, :] = _f32(pa_ref[...]) * _sigmoid(_f32(pb_ref[...])) * jnp.where(first, 0.0, 1.0)
        su_ref[0, HALO:HALO + tm, :] = _f32(a_ref[...]) * _sigmoid(_f32(b_ref[...]))
        _make_shifts(su_ref, tm + HALO - SUB)

        def ln_back(u2c, gv, dy):
            mu = jnp.mean(u2c, axis=1, keepdims=True)
            xc = u2c - mu
            rs = lax.rsqrt(jnp.mean(xc * xc, axis=1, keepdims=True) + EPS)
            xh = xc * rs
            lw = lw_ref[...]
            u3 = xh * lw + lb_ref[...]
            s3 = _sigmoid(u3)
            sg = _sigmoid(gv)
            dgc = dy * (u3 * s3) * _dsilu(gv, sg)
            du3 = dy * gv * sg * _dsilu(u3, s3)
            dxh = du3 * lw
            du2 = rs * (dxh - jnp.mean(dxh, axis=1, keepdims=True)
                        - xh * jnp.mean(dxh * xh, axis=1, keepdims=True))
            return du2, dgc, du3, xh

        def tile_chunk(ci, carry):
            r0 = pl.multiple_of(ci * RC, RC)
            rows = pl.ds(r0, RC)
            du2, dgc, du3, xh = ln_back(u2_ref[rows, :], _f32(g_ref[rows, :]), _f32(dy_ref[rows, :]))
            sd_ref[0, rows, :] = du2
            dp_ref[rows, 2 * c:3 * c] = dgc.astype(BF16)
            dwb_ref[...] += fold(du2)
            dlw_ref[...] += fold(du3 * xh)
            dlb_ref[...] += fold(du3)
            return carry

        lax.fori_loop(0, tm // RC, tile_chunk, 0, unroll=2)
        live = jnp.where(last, 0.0, 1.0)
        for ci in range(HALO // RC):
            rows = slice(ci * RC, (ci + 1) * RC)
            du2, _, _, _ = ln_back(nu2_ref[rows, :], _f32(ng_ref[rows, :]), _f32(ndy_ref[rows, :]))
            sd_ref[0, tm + ci * RC:tm + (ci + 1) * RC, :] = du2 * live
        _make_shifts(sd_ref, tm + HALO - SUB)

        def tap_chunk(ci, carry):
            r0 = pl.multiple_of(ci * RC, RC)
            rows = pl.ds(r0, RC)
            du1 = _taps(w_ref, sd_ref, r0, CONF_K - 1, -1, jnp.zeros((SUB, c), F32))
            sb = _sigmoid(_f32(b_ref[rows, :]))
            dp_ref[rows, 0:c] = (du1 * sb).astype(BF16)
            dp_ref[rows, c:2 * c] = (du1 * _f32(a_ref[rows, :]) * sb * (1.0 - sb)).astype(BF16)
            du2 = sd_ref[0, rows, :]
            for k in range(CONF_K):
                dww_ref[k * SUB:(k + 1) * SUB, :] += fold(du2 * _shifted(su_ref, r0, HALO - CONF_K + 1 + k, RC))
            return carry

        lax.fori_loop(0, tm // RC, tap_chunk, 0)

        @pl.when(i == nsteps - 1)
        def _():
            comm_finish()

    def col(j):
        return pl.BlockSpec((tm, c), lambda i: (i, j))

    def prev(j):
        return pl.BlockSpec((HALO, c), lambda i: (jnp.maximum(i * hb - 1, 0), j))

    def nxt(j):
        return pl.BlockSpec((HALO, c), lambda i: (jnp.minimum((i + 1) * hb, nhb - 1), j))

    vec = pl.BlockSpec((1, c), lambda i: (0, 0))
    acc = pl.BlockSpec((SUB, c), lambda i: (0, 0))
    any_spec = pl.BlockSpec(memory_space=pl.ANY)
    return _pallas(
        body, name="conf_bwd", grid=(nsteps,),
        in_specs=[col(0), col(1), col(2), prev(0), prev(1), nxt(2), col(0), nxt(0), col(0), nxt(0),
                  pl.BlockSpec((HALO * SUB, c), lambda i: (0, 0)), vec, vec] + [any_spec] * nw,
        out_specs=[pl.BlockSpec((tm, 3 * c), lambda i: (i, 0)),
                   pl.BlockSpec((HALO * SUB, c), lambda i: (0, 0)), acc, acc, acc] + [any_spec] * nw,
        out_shape=[jax.ShapeDtypeStruct((t, 3 * c), BF16), jax.ShapeDtypeStruct((HALO * SUB, c), F32),
                   jax.ShapeDtypeStruct((SUB, c), F32), jax.ShapeDtypeStruct((SUB, c), F32),
                   jax.ShapeDtypeStruct((SUB, c), F32)]
        + [_landing_shape(kd, s) for kd, s in zip(comm_kinds, comm_srcs)],
        scratch_shapes=[pltpu.VMEM((SUB, tm + HALO, c), F32), pltpu.VMEM((SUB, tm + HALO, c), F32)] + _chip_sems(nw),
        compiler_params=_params(1),
    )(proj, proj, proj, proj, proj, proj, u2, u2, dycat, dycat, dw_w, ln_w, ln_b, *comm_srcs)


Q_COL = 3 * CONV_W // HEAD_DIM
K_COL = Q_COL + N_HEADS
V_COL = K_COL + N_HEADS
GA_COL = V_COL + N_HEADS


SBA_TQ = 256
SBA_WK = 4 * QB


def _sb_window(qs, kw, ws, limit, t0, carry):
    tq, wk = qs.shape[0], kw.shape[0]
    z = _dot(qs, kw, NT)
    sg = ws + lax.broadcasted_iota(jnp.int32, (tq, wk), 1)
    tg = t0 + lax.broadcasted_iota(jnp.int32, (tq, wk), 0)
    mask = sg < jnp.minimum(tg, limit)
    sp = jnp.log(1.0 + jnp.exp(-jnp.abs(z)))
    ls = jnp.minimum(z, 0.0) - sp
    lk = jnp.where(mask, ls - z, 0.0)
    jj = lax.broadcasted_iota(jnp.int32, (QB, QB), 0)
    ss = lax.broadcasted_iota(jnp.int32, (QB, QB), 1)
    ustrict = jnp.where(jj > ss, 1.0, 0.0).astype(BF16)
    laters = [None] * (wk // QB)
    for ch in reversed(range(wk // QB)):
        lkc = lk[:, ch * QB:(ch + 1) * QB]
        laters[ch] = carry + _dotx2(lkc, ustrict)
        carry = carry + jnp.sum(lkc, axis=1, keepdims=True)
    w = jnp.where(mask, jnp.exp(ls + jnp.concatenate(laters, axis=1)), 0.0)
    return mask, ls, w, carry


def _sba_fwd(proj, nb, seq, *, tq=SBA_TQ, wk=SBA_WK):
    t = proj.shape[0]
    wk = min(wk, seq)
    nq = seq // tq
    scale = HEAD_DIM ** -0.5

    def body(q_ref, k_ref, v_ref, g_ref, o_ref, y_ref):
        i = pl.program_id(2)
        t0 = i * tq
        qs = (_f32(q_ref[...]) * scale).astype(BF16)

        def window(ws, limit, carry, acc):
            ws = pl.multiple_of(ws, QB)
            _, _, w, carry = _sb_window(qs, k_ref[pl.ds(ws, wk), :], ws, limit, t0, carry)
            return carry, acc + _dot(w, v_ref[pl.ds(ws, wk), :])

        ws0 = jnp.maximum(t0 + tq - wk, 0)
        carry, acc = window(ws0, seq, jnp.zeros((tq, 1), F32), jnp.zeros((tq, HEAD_DIM), F32))

        def cond(st):
            return jnp.logical_and(st[0] > 0, jnp.max(st[1]) > NEG_CUT)

        def step(st):
            c2, a2 = window(jnp.maximum(st[0] - wk, 0), st[0], st[1], st[2])
            return jnp.maximum(st[0] - wk, 0), c2, a2

        _, _, acc = lax.while_loop(cond, step, (ws0, carry, acc))
        o_ref[...] = acc
        gv = _f32(g_ref[...])
        y_ref[...] = (acc * gv * _sigmoid(gv)).astype(BF16)

    def tile(c0):
        return pl.BlockSpec((tq, HEAD_DIM), lambda b, h, i: (b * nq + i, c0 + h))

    def whole(c0):
        return pl.BlockSpec((seq, HEAD_DIM), lambda b, h, i: (b, c0 + h))

    return _pallas(
        body, name="sba_fwd", grid=(nb, N_HEADS, nq),
        in_specs=[tile(Q_COL), whole(K_COL), whole(V_COL), tile(GA_COL)],
        out_specs=[tile(0), tile(0)],
        out_shape=[jax.ShapeDtypeStruct((t, ATT_W), F32), jax.ShapeDtypeStruct((t, ATT_W), BF16)],
        compiler_params=_params(3),
    )(proj, proj, proj, proj)


def _sba_bwd(proj, o, dycat, nb, seq, *, tq=SBA_TQ, wk=SBA_WK):
    t = proj.shape[0]
    wk = min(wk, seq)
    nq = seq // tq
    nwin = -(-seq // wk) + 1
    nch = wk // QB
    scale = HEAD_DIM ** -0.5

    def body(q_ref, k_ref, v_ref, g_ref, o_ref, dy_ref, dq_ref, dko_ref, dvo_ref, dg_ref, e_ref, sp_ref,
             dk_ref, dv_ref):
        i = pl.program_id(2)
        t0 = i * tq

        @pl.when(i == 0)
        def _():
            dk_ref[...] = jnp.zeros_like(dk_ref)
            dv_ref[...] = jnp.zeros_like(dv_ref)

        qs = (_f32(q_ref[...]) * scale).astype(BF16)
        gv = _f32(g_ref[...])
        sg = _sigmoid(gv)
        dy = _f32(dy_ref[...])
        do = (dy * gv * sg).astype(BF16)
        dg_ref[...] = (dy * o_ref[...] * _dsilu(gv, sg)).astype(BF16)

        def start_of(n):
            return pl.multiple_of(jnp.maximum(t0 + tq - (n + 1) * wk, 0), QB)

        def limit_of(n):
            return jnp.where(n == 0, seq, jnp.maximum(t0 + tq - n * wk, 0))

        def near(n, carry):
            ws = start_of(n)
            _, ls, w, carry = _sb_window(qs, k_ref[pl.ds(ws, wk), :], ws, limit_of(n), t0, carry)
            e_ref[n] = w * _dot(do, v_ref[pl.ds(ws, wk), :], NT)
            sp_ref[n] = jnp.exp(ls)
            dv_ref[pl.ds(ws, wk), :] += _dot(w, do, TN)
            return carry

        carry = near(0, jnp.zeros((tq, 1), F32))

        def cond(st):
            return jnp.logical_and(start_of(st[0] - 1) > 0, jnp.max(st[1]) > NEG_CUT)

        def step(st):
            return st[0] + 1, near(st[0], st[1])

        nvis, _ = lax.while_loop(cond, step, (1, carry))

        jj = lax.broadcasted_iota(jnp.int32, (QB, QB), 0)
        ss = lax.broadcasted_iota(jnp.int32, (QB, QB), 1)
        lstrict = jnp.where(jj < ss, 1.0, 0.0).astype(BF16)

        def far(r, st):
            pre, dq = st
            n = nvis - 1 - r
            ws = start_of(n)
            e = e_ref[n]
            spn = sp_ref[n]
            gs = []
            for ch in range(nch):
                ec = e[:, ch * QB:(ch + 1) * QB]
                gs.append(pre + _dotx2(ec, lstrict))
                pre = pre + jnp.sum(ec, axis=1, keepdims=True)
            sgl = ws + lax.broadcasted_iota(jnp.int32, (tq, wk), 1)
            tgl = t0 + lax.broadcasted_iota(jnp.int32, (tq, wk), 0)
            mask = sgl < jnp.minimum(tgl, limit_of(n))
            dz = jnp.where(mask, e * (1.0 - spn) - jnp.concatenate(gs, axis=1) * spn, 0.0).astype(BF16)
            dk_ref[pl.ds(ws, wk), :] += _dot(dz, qs, TN)
            return pre, dq + _dot(dz, k_ref[pl.ds(ws, wk), :])

        _, dq = lax.fori_loop(0, nvis, far, (jnp.zeros((tq, 1), F32), jnp.zeros((tq, HEAD_DIM), F32)))
        dq_ref[...] = (dq * scale).astype(BF16)

        @pl.when(i == nq - 1)
        def _():
            dko_ref[...] = dk_ref[...].astype(BF16)
            dvo_ref[...] = dv_ref[...].astype(BF16)

    def tile(c0):
        return pl.BlockSpec((tq, HEAD_DIM), lambda b, h, i: (b * nq + i, c0 + h))

    def whole(c0):
        return pl.BlockSpec((seq, HEAD_DIM), lambda b, h, i: (b, c0 + h))

    return _pallas(
        body, name="sba_bwd", grid=(nb, N_HEADS, nq),
        in_specs=[tile(Q_COL), whole(K_COL), whole(V_COL), tile(GA_COL), tile(0),
                  tile(CONV_W // HEAD_DIM)],
        out_specs=[tile(0), whole(0), whole(0), tile(0)],
        out_shape=[jax.ShapeDtypeStruct((t, ATT_W), BF16)] * 4,
        scratch_shapes=[pltpu.VMEM((nwin, tq, wk), F32), pltpu.VMEM((nwin, tq, wk), F32),
                        pltpu.VMEM((seq, HEAD_DIM), F32), pltpu.VMEM((seq, HEAD_DIM), F32)],
        compiler_params=_params(3),
    )(proj, proj, proj, proj, o, dycat)


CT = 512
PH = 8
XRC = 32
X_SHIFTS = tuple(s for s in range(PH - SSM_K + 1, PH))
D_SHIFTS = tuple(range(1, SSM_K))
Z_BLK = 0
XBC_BLK = D_INNER // CT
DT_BLK = (D_INNER + XBC) // LANE


def _softplus(x):
    return jnp.maximum(x, 0.0) + jnp.log(1.0 + jnp.exp(-jnp.abs(x)))


def _dt_fwd(proj, dt_bias, *, tm=512):
    t = proj.shape[0]

    def body(p_ref, b_ref, o_ref):
        o_ref[...] = _softplus(p_ref[...] + b_ref[...])

    return _pallas(
        body, name="dt_fwd", grid=(t // tm,),
        in_specs=[pl.BlockSpec((tm, LANE), lambda i: (i, 0)), pl.BlockSpec((1, LANE), lambda i: (0, 0))],
        out_specs=pl.BlockSpec((tm, LANE), lambda i: (i, 0)),
        out_shape=jax.ShapeDtypeStruct((t, LANE), F32), compiler_params=_params(1),
    )(proj, dt_bias)


def _dt_bwd(proj, dt_bias, ddt, dproj, *, tm=512):
    t = proj.shape[0]
    wide = IN_ODD_PAD - D_INNER - XBC

    def body(p_ref, b_ref, d_ref, dp_any, o_ref, db_ref):
        i = pl.program_id(0)
        lanes = lax.broadcasted_iota(jnp.int32, (tm, LANE), 1)
        dr = jnp.where(lanes < SSM_H, d_ref[...] * _sigmoid(p_ref[...] + b_ref[...]), 0.0)
        o_ref[:, 0:LANE] = dr.astype(BF16)
        o_ref[:, LANE:wide] = jnp.zeros((tm, wide - LANE), BF16)

        @pl.when(i == 0)
        def _():
            db_ref[...] = jnp.zeros_like(db_ref)

        db_ref[...] += jnp.sum(dr, axis=0, keepdims=True)

    vec = pl.BlockSpec((1, LANE), lambda i: (0, 0))
    row = pl.BlockSpec((tm, LANE), lambda i: (i, 0))
    return _pallas(
        body, name="dt_bwd", grid=(t // tm,),
        in_specs=[pl.BlockSpec((tm, LANE), lambda i: (i, 0)), vec, row, pl.BlockSpec(memory_space=pl.ANY)],
        out_specs=[pl.BlockSpec((tm, wide), lambda i: (i, (D_INNER + XBC) // wide)), vec],
        out_shape=[jax.ShapeDtypeStruct(dproj.shape, dproj.dtype), jax.ShapeDtypeStruct((1, LANE), F32)],
        input_output_aliases={3: 0}, compiler_params=_params(1),
    )(proj, dt_bias, ddt, dproj)


def _xconv_fwd(proj, conv_w, conv_b, seq, *, tm=512):
    t = proj.shape[0]
    tps = seq // tm
    hb = tm // PH

    def body(x_ref, h_ref, w_ref, b_ref, o_ref, sh_ref):
        i = pl.program_id(1)
        sh_ref[0, 0:PH, :] = _f32(h_ref[...]) * jnp.where(i % tps == 0, 0.0, 1.0)
        sh_ref[0, PH:PH + tm, :] = _f32(x_ref[...])
        _make_shifts(sh_ref, tm, X_SHIFTS)

        def chunk(ci, carry):
            r0 = pl.multiple_of(ci * XRC, XRC)
            acc = jnp.zeros((XRC, CT), F32) + b_ref[...]
            for k in range(SSM_K):
                acc = acc + w_ref[k:k + 1, :] * _shifted(sh_ref, r0, PH - SSM_K + 1 + k, XRC)
            o_ref[pl.ds(r0, XRC), :] = acc * _sigmoid(acc)
            return carry

        lax.fori_loop(0, tm // XRC, chunk, 0)

    return _pallas(
        body, name="xconv_fwd", grid=(XBC // CT, t // tm),
        in_specs=[pl.BlockSpec((tm, CT), lambda j, i: (i, XBC_BLK + j)),
                  pl.BlockSpec((PH, CT), lambda j, i: (jnp.maximum(i * hb - 1, 0), XBC_BLK + j)),
                  pl.BlockSpec((PH, CT), lambda j, i: (0, j)),
                  pl.BlockSpec((1, CT), lambda j, i: (0, j))],
        out_specs=pl.BlockSpec((tm, CT), lambda j, i: (i, j)),
        out_shape=jax.ShapeDtypeStruct((t, XBC), F32),
        scratch_shapes=[pltpu.VMEM((SUB, tm + PH, CT), F32)], compiler_params=_params(2),
    )(proj, proj, conv_w, conv_b)


def _xconv_bwd(proj, dxc, conv_w, conv_b, dproj, seq, *, tm=512):
    t = proj.shape[0]
    tps = seq // tm
    hb = tm // PH
    nhb = t // PH
    te = tm + PH

    def fold(v):
        out = v[0:SUB]
        for q in range(1, v.shape[0] // SUB):
            out = out + v[q * SUB:(q + 1) * SUB]
        return out

    def body(x_ref, p_ref, n_ref, d_ref, nd_ref, w_ref, b_ref, dp_any, dx_ref, dw_ref, db_ref, sx_ref, sd_ref):
        i = pl.program_id(1)
        first = i % tps == 0
        last = i % tps == tps - 1

        @pl.when(i == 0)
        def _():
            dw_ref[...] = jnp.zeros_like(dw_ref)
            db_ref[...] = jnp.zeros_like(db_ref)

        sx_ref[0, 0:PH, :] = _f32(p_ref[...]) * jnp.where(first, 0.0, 1.0)
        sx_ref[0, PH:PH + tm, :] = _f32(x_ref[...])
        sx_ref[0, PH + tm:PH + te, :] = _f32(n_ref[...])
        _make_shifts(sx_ref, te, X_SHIFTS)

        def dv_of(r0, rows, dy):
            acc = jnp.zeros((rows, CT), F32) + b_ref[...]
            for k in range(SSM_K):
                acc = acc + w_ref[k:k + 1, :] * _shifted(sx_ref, r0, PH - SSM_K + 1 + k, rows)
            return dy * _dsilu(acc, _sigmoid(acc))

        def dv_chunk(ci, carry):
            r0 = pl.multiple_of(ci * XRC, XRC)
            dv = dv_of(r0, XRC, d_ref[pl.ds(r0, XRC), :])
            sd_ref[0, pl.ds(r0, XRC), :] = dv
            db_ref[...] += fold(dv)
            return carry

        lax.fori_loop(0, tm // XRC, dv_chunk, 0)
        sd_ref[0, tm:te, :] = dv_of(tm, PH, nd_ref[...]) * jnp.where(last, 0.0, 1.0)
        _make_shifts(sd_ref, tm, D_SHIFTS)

        def tap_chunk(ci, carry):
            r0 = pl.multiple_of(ci * XRC, XRC)
            dx = jnp.zeros((XRC, CT), F32)
            for k in range(SSM_K):
                dx = dx + w_ref[k:k + 1, :] * _shifted(sd_ref, r0, SSM_K - 1 - k, XRC)
            dx_ref[pl.ds(r0, XRC), :] = dx.astype(BF16)
            dv = sd_ref[0, pl.ds(r0, XRC), :]
            for k in range(SSM_K):
                dw_ref[k * SUB:(k + 1) * SUB, :] += fold(dv * _shifted(sx_ref, r0, PH - SSM_K + 1 + k, XRC))
            return carry

        lax.fori_loop(0, tm // XRC, tap_chunk, 0)

    return _pallas(
        body, name="xconv_bwd", grid=(XBC // CT, t // tm),
        in_specs=[pl.BlockSpec((tm, CT), lambda j, i: (i, XBC_BLK + j)),
                  pl.BlockSpec((PH, CT), lambda j, i: (jnp.maximum(i * hb - 1, 0), XBC_BLK + j)),
                  pl.BlockSpec((PH, CT), lambda j, i: (jnp.minimum((i + 1) * hb, nhb - 1), XBC_BLK + j)),
                  pl.BlockSpec((tm, CT), lambda j, i: (i, j)),
                  pl.BlockSpec((PH, CT), lambda j, i: (jnp.minimum((i + 1) * hb, nhb - 1), j)),
                  pl.BlockSpec((PH, CT), lambda j, i: (0, j)),
                  pl.BlockSpec((1, CT), lambda j, i: (0, j)),
                  pl.BlockSpec(memory_space=pl.ANY)],
        out_specs=[pl.BlockSpec((tm, CT), lambda j, i: (i, XBC_BLK + j)),
                   pl.BlockSpec((PH * SUB, CT), lambda j, i: (0, j)),
                   pl.BlockSpec((SUB, CT), lambda j, i: (0, j))],
        out_shape=[jax.ShapeDtypeStruct(dproj.shape, dproj.dtype), jax.ShapeDtypeStruct((PH * SUB, XBC), F32),
                   jax.ShapeDtypeStruct((SUB, XBC), F32)],
        scratch_shapes=[pltpu.VMEM((SUB, tm + 2 * PH, CT), F32), pltpu.VMEM((SUB, te, CT), F32)],
        input_output_aliases={7: 0}, compiler_params=_params(2),
    )(proj, proj, proj, dxc, dxc, conv_w, conv_b, dproj)


def _ssd_common(xbc, dt, alog, ex):
    L = CHUNK
    a = -jnp.exp(alog)
    la = dt * a
    li = lax.broadcasted_iota(jnp.int32, (L, L), 0)
    si = lax.broadcasted_iota(jnp.int32, (L, L), 1)
    lower = si <= li
    tri = jnp.where(lower, 1.0, 0.0).astype(BF16)
    cs = _xdot(tri, la)
    cst = _dotx(la, tri, (((0,), (1,)), ((), ())))
    csl = cs[L - 1:L, :]
    ecs_x = _dotx2(jnp.exp(cs)[:, 0:SSM_H], ex)
    tail_x = _dotx2(jnp.exp(csl - cs)[:, 0:SSM_H], ex)
    dt_x = _dotx2(dt[:, 0:SSM_H], ex)
    return a, la, lower, tri, cs, cst, ecs_x, tail_x, dt_x


def _ssd_fwd(xbc_c, dt, a_log, ex, nb, seq):
    t = xbc_c.shape[0]
    L = CHUNK
    nc = seq // L
    GW = SSM_R * SSM_P

    def body(x_ref, dt_ref, al_ref, ex_ref, y_ref, st_ref, state):
        c = pl.program_id(1)

        @pl.when(c == 0)
        def _():
            state[...] = jnp.zeros_like(state)

        st_ref[0] = state[...]
        xbc = x_ref[...]
        _, _, lower, _, cs, cst, ecs_x, tail_x, dt_x = _ssd_common(xbc, dt_ref[...], al_ref[...], ex_ref[...])
        xd = xbc[:, 0:D_INNER] * dt_x
        xdb = xd.astype(BF16)
        xt = (xd * tail_x).astype(BF16)
        el_x = ecs_x[L - 1:L, :]
        for g in range(SSM_G):
            bg = xbc[:, D_INNER + g * SSM_N:D_INNER + (g + 1) * SSM_N].astype(BF16)
            cg = xbc[:, D_INNER + (SSM_G + g) * SSM_N:D_INNER + (SSM_G + g + 1) * SSM_N].astype(BF16)
            cb = _dot(cg, bg, NT)
            sg = state[:, g * GW:(g + 1) * GW]
            ys = _dot(cg, sg) * ecs_x[:, g * GW:(g + 1) * GW]
            for r in range(SSM_R):
                h = g * SSM_R + r
                seg = cs[:, h:h + 1] - cst[h:h + 1, :]
                dec = jnp.exp(jnp.where(lower, seg, -1e30))
                yh = _dot(cb * dec, xdb[:, h * SSM_P:(h + 1) * SSM_P])
                y_ref[:, h * SSM_P:(h + 1) * SSM_P] = yh + ys[:, r * SSM_P:(r + 1) * SSM_P]
            state[:, g * GW:(g + 1) * GW] = sg * el_x[:, g * GW:(g + 1) * GW] + _dot(bg, xt[:, g * GW:(g + 1) * GW], TN)

    return _pallas(
        body, name="ssd_fwd", grid=(nb, nc),
        in_specs=[pl.BlockSpec((L, XBC), lambda b, c: (b * nc + c, 0)),
                  pl.BlockSpec((L, LANE), lambda b, c: (b * nc + c, 0)),
                  pl.BlockSpec((1, LANE), lambda b, c: (0, 0)),
                  pl.BlockSpec((SSM_H, D_INNER), lambda b, c: (0, 0))],
        out_specs=[pl.BlockSpec((L, D_INNER), lambda b, c: (b * nc + c, 0)),
                   pl.BlockSpec((1, SSM_N, D_INNER), lambda b, c: (b * nc + c, 0, 0))],
        out_shape=[jax.ShapeDtypeStruct((t, D_INNER), F32),
                   jax.ShapeDtypeStruct((nb * nc, SSM_N, D_INNER), F32)],
        scratch_shapes=[pltpu.VMEM((SSM_N, D_INNER), F32)], compiler_params=_params(2),
    )(xbc_c, dt, a_log, ex)


def _ssd_bwd(xbc_c, dt, a_log, ex, ext, states, dy, d_x, nb, seq):
    t = xbc_c.shape[0]
    L = CHUNK
    nc = seq // L
    GW = SSM_R * SSM_P

    def body(x_ref, dt_ref, al_ref, ex_ref, ext_ref, st_ref, dy_ref, sk_ref, dx_ref, ddt_ref, da_ref,
             dstate, dxd, yd, lastv):
        b = pl.program_id(0)
        c = pl.program_id(1)

        @pl.when(c == 0)
        def _():
            dstate[...] = jnp.zeros_like(dstate)

        @pl.when(jnp.logical_and(b == 0, c == 0))
        def _():
            da_ref[...] = jnp.zeros_like(da_ref)

        xbc = x_ref[...]
        dtv = dt_ref[...]
        ex_t = ext_ref[...]
        a, la, lower, tri, cs, cst, ecs_x, tail_x, dt_x = _ssd_common(xbc, dtv, al_ref[...], ex_ref[...])
        xs = xbc[:, 0:D_INNER]
        xd = xs * dt_x
        xdb = xd.astype(BF16)
        dyv = dy_ref[...]
        dyb = dyv.astype(BF16)
        dys = dyv * ecs_x
        xt = xd * tail_x
        el_x = ecs_x[L - 1:L, :]
        lane = lax.broadcasted_iota(jnp.int32, (L, LANE), 1)
        sub = lax.broadcasted_iota(jnp.int32, (LANE, L), 0)
        row_part = jnp.zeros((L, LANE), F32)
        col_part = jnp.zeros((LANE, L), F32)
        for g in range(SSM_G):
            gs = slice(g * GW, (g + 1) * GW)
            bcol = slice(D_INNER + g * SSM_N, D_INNER + (g + 1) * SSM_N)
            ccol = slice(D_INNER + (SSM_G + g) * SSM_N, D_INNER + (SSM_G + g + 1) * SSM_N)
            bg = xbc[:, bcol].astype(BF16)
            cg = xbc[:, ccol].astype(BF16)
            cb = _dot(cg, bg, NT)
            sg = st_ref[0, :, gs]
            dsg = dstate[:, gs]
            dc = _dot(dys[:, gs], sg, NT)
            db = _dot(xt[:, gs], dsg, NT)
            dx_state = tail_x[:, gs] * _dot(bg, dsg)
            tail_part = xd[:, gs] * dx_state
            yd[:, gs] = dys[:, gs] * _dot(cg, sg) - tail_part
            last = jnp.sum(tail_part, axis=0, keepdims=True) + el_x[:, gs] * jnp.sum(dsg * sg, axis=0, keepdims=True)
            lastv[:, gs] = jnp.broadcast_to(last, (8, GW))
            dcb = jnp.zeros((L, L), F32)
            for r in range(SSM_R):
                h = g * SSM_R + r
                hs = slice(h * SSM_P, (h + 1) * SSM_P)
                seg = cs[:, h:h + 1] - cst[h:h + 1, :]
                dec = jnp.exp(jnp.where(lower, seg, -1e30))
                m = cb * dec
                dm = _dot(dyb[:, hs], xdb[:, hs], NT)
                dcb = dcb + dm * dec
                e = dm * m
                row_part = row_part + jnp.where(lane == h, jnp.sum(e, axis=1, keepdims=True), 0.0)
                col_part = col_part + jnp.where(sub == h, jnp.sum(e, axis=0, keepdims=True), 0.0)
                dxd[:, hs] = _dot(m, dyb[:, hs], TN) + dx_state[:, r * SSM_P:(r + 1) * SSM_P]
            dx_ref[:, bcol] = db + _dot(dcb, cg, TN)
            dx_ref[:, ccol] = dc + _dot(dcb, bg)
            dstate[:, gs] = dsg * el_x[:, gs] + _dot(cg, dys[:, gs], TN)
        dxv = dxd[...]
        dx_ref[:, 0:D_INNER] = dxv * dt_x + dyv * sk_ref[...]
        ddt_x = _dotx(dxv * xs, ex_t)
        yst = _dotx(yd[...], ex_t)
        lst = _dotx(lastv[...], ex_t)[0:1, :]
        rows = lax.broadcasted_iota(jnp.int32, (L, LANE), 0)
        dcs = row_part - col_part.T + yst + jnp.where(rows == L - 1, lst, 0.0)
        li = lax.broadcasted_iota(jnp.int32, (L, L), 0)
        si = lax.broadcasted_iota(jnp.int32, (L, L), 1)
        upper = jnp.where(si >= li, 1.0, 0.0).astype(BF16)
        dla = _xdot(upper, dcs)
        ddt_ref[...] = dla * a + ddt_x
        da_ref[...] += jnp.sum(dla * dtv, axis=0, keepdims=True)

    def row(w):
        return pl.BlockSpec((L, w), lambda b, c: (b * nc + nc - 1 - c, 0))

    return _pallas(
        body, name="ssd_bwd", grid=(nb, nc),
        in_specs=[row(XBC), row(LANE), pl.BlockSpec((1, LANE), lambda b, c: (0, 0)),
                  pl.BlockSpec((SSM_H, D_INNER), lambda b, c: (0, 0)),
                  pl.BlockSpec((D_INNER, LANE), lambda b, c: (0, 0)),
                  pl.BlockSpec((1, SSM_N, D_INNER), lambda b, c: (b * nc + nc - 1 - c, 0, 0)),
                  row(D_INNER), pl.BlockSpec((1, D_INNER), lambda b, c: (0, 0))],
        out_specs=[row(XBC), row(LANE), pl.BlockSpec((1, LANE), lambda b, c: (0, 0))],
        out_shape=[jax.ShapeDtypeStruct((t, XBC), F32), jax.ShapeDtypeStruct((t, LANE), F32),
                   jax.ShapeDtypeStruct((1, LANE), F32)],
        scratch_shapes=[pltpu.VMEM((SSM_N, D_INNER), F32), pltpu.VMEM((L, D_INNER), F32),
                        pltpu.VMEM((L, D_INNER), F32), pltpu.VMEM((8, D_INNER), F32)],
        compiler_params=_params(2),
    )(xbc_c, dt, a_log, ex, ext, states, dy, d_x)


def _group_rms(y2):
    gw = D_INNER // SSM_G
    parts = []
    for g in range(SSM_G):
        v = y2[:, g * gw:(g + 1) * gw]
        r = lax.rsqrt(jnp.mean(v * v, axis=1, keepdims=True) + EPS)
        parts.append(jnp.broadcast_to(r, v.shape))
    return jnp.concatenate(parts, axis=1)


def _gate_fwd(y, xbc_c, proj, d_x, gn_w, *, tm=256):
    t = y.shape[0]

    def body(y_ref, x_ref, z_ref, d_ref, w_ref, o_ref):
        y1 = y_ref[...] + d_ref[...] * x_ref[...]
        zv = _f32(z_ref[...])
        y2 = y1 * zv * _sigmoid(zv)
        o_ref[...] = (y2 * _group_rms(y2) * w_ref[...]).astype(BF16)

    row = pl.BlockSpec((tm, D_INNER), lambda i: (i, 0))
    vec = pl.BlockSpec((1, D_INNER), lambda i: (0, 0))
    return _pallas(
        body, name="gate_fwd", grid=(t // tm,), in_specs=[row, row, row, vec, vec], out_specs=row,
        out_shape=jax.ShapeDtypeStruct((t, D_INNER), BF16), compiler_params=_params(1),
    )(y, xbc_c, proj, d_x, gn_w)


def _gate_bwd(dyg, y, xbc_c, proj, d_x, gn_w, *, tm=256):
    t = y.shape[0]
    gw = D_INNER // SSM_G

    def body(dg_ref, y_ref, x_ref, z_ref, d_ref, w_ref, dy_ref, dz_ref, dw_ref, dd_ref):
        i = pl.program_id(0)
        xv = x_ref[...]
        dxv = d_ref[...]
        y1 = y_ref[...] + dxv * xv
        zv = _f32(z_ref[...])
        sz = _sigmoid(zv)
        y2 = y1 * zv * sz
        rr = _group_rms(y2)
        xh = y2 * rr
        dg = _f32(dg_ref[...])
        gq = dg * w_ref[...]
        prod = gq * xh
        means = []
        for g in range(SSM_G):
            mg = jnp.mean(prod[:, g * gw:(g + 1) * gw], axis=1, keepdims=True)
            means.append(jnp.broadcast_to(mg, (tm, gw)))
        dy2 = rr * (gq - xh * jnp.concatenate(means, axis=1))
        dy1 = dy2 * zv * sz
        dy_ref[...] = dy1
        dz_ref[...] = (dy2 * y1 * _dsilu(zv, sz)).astype(BF16)

        @pl.when(i == 0)
        def _():
            dw_ref[...] = jnp.zeros_like(dw_ref)
            dd_ref[...] = jnp.zeros_like(dd_ref)

        dw_ref[...] += jnp.sum(dg * xh, axis=0, keepdims=True)
        dd_ref[...] += jnp.sum(dy1 * xv, axis=0, keepdims=True)

    row = pl.BlockSpec((tm, D_INNER), lambda i: (i, 0))
    vec = pl.BlockSpec((1, D_INNER), lambda i: (0, 0))
    return _pallas(
        body, name="gate_bwd", grid=(t // tm,), in_specs=[row, row, row, row, vec, vec],
        out_specs=[row, row, vec, vec],
        out_shape=[jax.ShapeDtypeStruct((t, D_INNER), F32),
                   jax.ShapeDtypeStruct((t, IN_ODD_PAD), BF16), jax.ShapeDtypeStruct((1, D_INNER), F32),
                   jax.ShapeDtypeStruct((1, D_INNER), F32)],
        compiler_params=_params(1),
    )(dyg, y, xbc_c, proj, d_x, gn_w)


ANY = pl.BlockSpec(memory_space=pl.ANY)


def _remote(src, dst, sems, k, to):
    send_sems, recv_sems = sems
    return pltpu.make_async_remote_copy(src_ref=src, dst_ref=dst, send_sem=send_sems.at[k], recv_sem=recv_sems.at[k],
                                        device_id=to, device_id_type=MESH)


NCHIP = 4


def _gathered_shape(kind, shard):
    r, n = shard.shape
    shape = {"cols": (r, NCHIP * n), "slab": (NCHIP, r, n), "rows": (NCHIP * r, n)}[kind]
    return jax.ShapeDtypeStruct(shape, shard.dtype)


def _gather_plan(kinds, shards, outs, sems, small=None):
    ici_s, ici_r, d2d_s, d2d_r = sems
    nw = len(shards)
    per = nw + (small is not None)

    def place():
        x, y, c = lax.axis_index("x"), lax.axis_index("y"), lax.axis_index("c")
        return 2 * x + y, c, (x, y, 1 - c), [(1 - x, y), (x, 1 - y), (1 - x, 1 - y)]

    def region(j, chip, half):
        r, n = shards[j].shape
        h = r // 2
        if kinds[j] == "cols":
            return outs[j].at[pl.ds(half * h, h), pl.ds(pl.multiple_of(chip * n, LANE), n)]
        if kinds[j] == "slab":
            return outs[j].at[chip, pl.ds(half * h, h), :]
        return outs[j].at[pl.ds(chip * r + half * h, h), :]

    def my_sends(me, c, peers):
        cps = []
        for k, (px, py) in enumerate(peers):
            for j in range(nw):
                h = shards[j].shape[0] // 2
                cps.append(_remote(shards[j].at[pl.ds(c * h, h), :], region(j, me, c), (ici_s, ici_r), per * k + j, (px, py, c)))
            if small is not None:
                cps.append(_remote(small[0], small[1].at[me], (ici_s, ici_r), per * k + nw, (px, py, c)))
        return cps

    def start():
        me, c, _, peers = place()
        for cp in my_sends(me, c, peers):
            cp.start()

    def finish():
        me, c, sib, peers = place()
        fwds = []
        for k, (px, py) in enumerate(peers):
            q = 2 * px + py
            for j in range(nw):
                d = region(j, q, c)
                _remote(d, d, (ici_s, ici_r), per * k + j, (px, py, c)).wait_recv()
                fwds.append(_remote(d, d, (d2d_s, d2d_r), nw * k + j, sib))
                fwds[-1].start()
            if small is not None:
                _remote(small[0], small[1].at[q], (ici_s, ici_r), per * k + nw, (px, py, c)).wait_recv()
        for k, (px, py) in enumerate(peers):
            for j in range(nw):
                d = region(j, 2 * px + py, 1 - c)
                _remote(d, d, (d2d_s, d2d_r), nw * k + j, sib).wait_recv()
        for cp in my_sends(me, c, peers) + fwds:
            cp.wait_send()

    return start, finish


def _gather_sems(nw, with_small):
    n_ici = 3 * (nw + with_small)
    return [pltpu.SemaphoreType.DMA((n_ici,)), pltpu.SemaphoreType.DMA((n_ici,)),
            pltpu.SemaphoreType.DMA((3 * nw,)), pltpu.SemaphoreType.DMA((3 * nw,))]


def _gather_shards(kinds, shards, small):
    nw = len(shards)

    def body(*refs):
        ins, sm, outs, osm, sems = refs[:nw], refs[nw], refs[nw + 1:2 * nw + 1], refs[2 * nw + 1], refs[2 * nw + 2:]
        start, finish = _gather_plan(kinds, ins, outs, sems, small=(sm, osm))
        start()
        finish()

    return _pallas(
        body, name="gather_shards", in_specs=[ANY] * (nw + 1), out_specs=[ANY] * (nw + 1),
        out_shape=[_gathered_shape(kd, s) for kd, s in zip(kinds, shards)]
        + [jax.ShapeDtypeStruct((NCHIP,) + small.shape, small.dtype)],
        scratch_shapes=_gather_sems(nw, 1),
    )(*shards, small)


def _place_cols(full, shard, *, name, tr=256):
    r, n = shard.shape

    def body(p_ref, full_any, s_ref, o_ref):
        o_ref[...] = s_ref[...]

    return _pallas(
        body, name=name,
        grid_spec=pltpu.PrefetchScalarGridSpec(
            num_scalar_prefetch=1, grid=(r // tr,),
            in_specs=[pl.BlockSpec(memory_space=pl.ANY), pl.BlockSpec((tr, n), lambda i, p: (i, 0))],
            out_specs=pl.BlockSpec((tr, n), lambda i, p: (i, p[0]))),
        out_shape=jax.ShapeDtypeStruct(full.shape, full.dtype), input_output_aliases={1: 0},
        compiler_params=_params(1),
    )(_chip_index(), full, shard)


def _matmul_with_gather(a, b, kinds, shards, *, out_dtype, bm, bn, name):
    (m, k), n = a.shape, b.shape[1]
    nw = len(shards)
    nj, ni = n // bn, m // bm

    def body(*refs):
        a_ref, b_ref, ins, o_ref = refs[0], refs[1], refs[2:2 + nw], refs[2 + nw]
        outs, sems = refs[3 + nw:3 + 2 * nw], refs[3 + 2 * nw:]
        start, finish = _gather_plan(kinds, ins, outs, sems)
        j, i = pl.program_id(0), pl.program_id(1)

        @pl.when(jnp.logical_and(j == 0, i == 0))
        def _():
            start()

        o_ref[...] = _dot(a_ref[...], b_ref[...]).astype(out_dtype)

        @pl.when(jnp.logical_and(j == nj - 1, i == ni - 1))
        def _():
            finish()

    return _pallas(
        body, name=name, grid=(nj, ni),
        in_specs=[pl.BlockSpec((bm, k), lambda j, i: (i, 0)), pl.BlockSpec((k, bn), lambda j, i: (0, j))] + [ANY] * nw,
        out_specs=[pl.BlockSpec((bm, bn), lambda j, i: (i, j))] + [ANY] * nw,
        out_shape=[jax.ShapeDtypeStruct((m, n), out_dtype)] + [_gathered_shape(kd, s) for kd, s in zip(kinds, shards)],
        scratch_shapes=_gather_sems(nw, 0), compiler_params=_params(2),
    )(a, b, *shards)


def _other_half(a, c):
    axis = a.ndim - 2
    h = a.shape[axis] // 2
    rows = pl.ds(pl.multiple_of((1 - c) * h, 8), h)
    return a.at[rows, :] if a.ndim == 2 else a.at[:, rows, :]


def _half_shape(a):
    axis = a.ndim - 2
    return jax.ShapeDtypeStruct(a.shape[:axis] + (a.shape[axis] // 2,) + a.shape[axis + 1:], a.dtype)


def _pair_swap(bigs, *, name):
    nb = len(bigs)

    def body(*refs):
        ins, outs, send_sems, recv_sems = refs[:nb], refs[nb:2 * nb], refs[2 * nb], refs[2 * nb + 1]
        x, y, c = lax.axis_index("x"), lax.axis_index("y"), lax.axis_index("c")
        pair = [_remote(_other_half(a, c), q, (send_sems, recv_sems), j, (x, y, 1 - c))
                for j, (a, q) in enumerate(zip(ins, outs))]
        for cp in pair:
            cp.start()
        for cp in pair:
            cp.wait()

    return _pallas(
        body, name=name, in_specs=[ANY] * nb, out_specs=[ANY] * nb, out_shape=[_half_shape(a) for a in bigs],
        scratch_shapes=[pltpu.SemaphoreType.DMA((nb,)), pltpu.SemaphoreType.DMA((nb,))],
    )(*bigs)


def _pair_exchange(bigs, gsmall, grep):
    nb = len(bigs)

    def body(*refs):
        ins, sm, rp = refs[:nb], refs[nb], refs[nb + 1]
        outs, osm, orp = refs[nb + 2:2 * nb + 2], refs[2 * nb + 2], refs[2 * nb + 3]
        pair_s, pair_r, send_sems, recv_sems, local_sems = refs[2 * nb + 4:]
        x, y, c = lax.axis_index("x"), lax.axis_index("y"), lax.axis_index("c")
        me = 4 * x + 2 * y + c
        chip = 2 * x + y
        sib = (x, y, 1 - c)
        pair = [_remote(_other_half(a, c), q, (pair_s, pair_r), j, sib) for j, (a, q) in enumerate(zip(ins, outs))]
        for cp in pair:
            cp.start()
        own = [pltpu.make_async_copy(sm.at[chip], osm.at[me], local_sems.at[0]),
               pltpu.make_async_copy(rp, orp.at[me], local_sems.at[1])]
        for cp in own:
            cp.start()
        peers = []
        for k in range(7):
            fx, fy, fc = ((k + 1) >> 2) & 1, ((k + 1) >> 1) & 1, (k + 1) & 1
            peers.append((1 - x if fx else x, 1 - y if fy else y, 1 - c if fc else c))
        sends = []
        for k, (px, py, pc) in enumerate(peers):
            sends.append(_remote(sm.at[2 * px + py], osm.at[me], (send_sems, recv_sems), 2 * k, (px, py, pc)))
            sends.append(_remote(rp, orp.at[me], (send_sems, recv_sems), 2 * k + 1, (px, py, pc)))
        for cp in sends:
            cp.start()
        for k, (px, py, pc) in enumerate(peers):
            slot = 4 * px + 2 * py + pc
            _remote(sm.at[chip], osm.at[slot], (send_sems, recv_sems), 2 * k, (px, py, pc)).wait_recv()
            _remote(rp, orp.at[slot], (send_sems, recv_sems), 2 * k + 1, (px, py, pc)).wait_recv()
        for cp in pair:
            cp.wait_recv()
        for cp in pair + sends:
            cp.wait_send()
        for cp in own:
            cp.wait()

    return _pallas(
        body, name="pair_exchange", in_specs=[ANY] * (nb + 2), out_specs=[ANY] * (nb + 2),
        out_shape=[_half_shape(a) for a in bigs]
        + [jax.ShapeDtypeStruct((8,) + gsmall.shape[1:], F32), jax.ShapeDtypeStruct((8,) + grep.shape, F32)],
        scratch_shapes=[pltpu.SemaphoreType.DMA((max(nb, 1),)), pltpu.SemaphoreType.DMA((max(nb, 1),)),
                        pltpu.SemaphoreType.DMA((14,)), pltpu.SemaphoreType.DMA((14,)),
                        pltpu.SemaphoreType.DMA((2,))],
    )(*bigs, gsmall, grep)


def _core_index():
    return lax.axis_index("c").astype(jnp.int32).reshape(1)


def _half_add(full, other, *, axis, block, name):
    nd = full.ndim
    nblk = other.shape[axis] // block[axis]
    grid = tuple(other.shape[d] // block[d] for d in range(nd))

    def body(c_ref, f_ref, o_ref, out_ref):
        out_ref[...] = (f_ref[...] + o_ref[...]).astype(BF16)

    def full_map(*idx):
        ids, c_ref = list(idx[:nd]), idx[nd]
        ids[axis] = ids[axis] + c_ref[0] * nblk
        return tuple(ids)

    def plain_map(*idx):
        return tuple(idx[:nd])

    return _pallas(
        body, name=name,
        grid_spec=pltpu.PrefetchScalarGridSpec(
            num_scalar_prefetch=1, grid=grid,
            in_specs=[pl.BlockSpec(block, full_map), pl.BlockSpec(block, plain_map)],
            out_specs=pl.BlockSpec(block, plain_map)),
        out_shape=jax.ShapeDtypeStruct(other.shape, BF16), compiler_params=_params(nd),
    )(_core_index(), full, other)


NPEER = 3


def _landing_shape(kind, src):
    if kind == "cols":
        return jax.ShapeDtypeStruct((NPEER, src.shape[0], src.shape[1] // NCHIP), src.dtype)
    return jax.ShapeDtypeStruct((NPEER,) + src.shape[1:], src.dtype)


def _chip_plan(kinds, srcs, lands, sems):
    nw = len(srcs)

    def place():
        x, y, c = lax.axis_index("x"), lax.axis_index("y"), lax.axis_index("c")
        return 2 * x + y, c, [(1 - x, y), (x, 1 - y), (1 - x, 1 - y)]

    def piece(j, chip):
        if kinds[j] == "cols":
            n = srcs[j].shape[1] // NCHIP
            return srcs[j].at[:, pl.ds(pl.multiple_of(chip * n, LANE), n)]
        return srcs[j].at[chip]

    def my_sends(c, peers):
        return [_remote(piece(j, 2 * px + py), lands[j].at[k], sems, nw * k + j, (px, py, c))
                for k, (px, py) in enumerate(peers) for j in range(nw)]

    def start():
        _, c, peers = place()
        for cp in my_sends(c, peers):
            cp.start()

    def finish():
        me, c, peers = place()
        for k, (px, py) in enumerate(peers):
            for j in range(nw):
                _remote(piece(j, me), lands[j].at[k], sems, nw * k + j, (px, py, c)).wait_recv()
        for cp in my_sends(c, peers):
            cp.wait_send()

    return start, finish


def _chip_sems(nw):
    return [pltpu.SemaphoreType.DMA((NPEER * nw,)), pltpu.SemaphoreType.DMA((NPEER * nw,))]


def _chip_exchange(kinds, srcs):
    nw = len(srcs)

    def body(*refs):
        start, finish = _chip_plan(kinds, refs[:nw], refs[nw:2 * nw], refs[2 * nw:])
        start()
        finish()

    return _pallas(
        body, name="chip_exchange", in_specs=[ANY] * nw, out_specs=[ANY] * nw,
        out_shape=[_landing_shape(kd, s) for kd, s in zip(kinds, srcs)], scratch_shapes=_chip_sems(nw),
    )(*srcs)


def _chip_index():
    return (2 * lax.axis_index("x") + lax.axis_index("y")).astype(jnp.int32).reshape(1)


def _chip_sum(own, slots, *, own_block, own_map, block, name):
    npeer = slots.shape[0]
    shape = slots.shape[1:]
    grid = (shape[0] // block[0], shape[1] // block[1])

    def body(p_ref, own_ref, s_ref, o_ref):
        acc = own_ref[...].reshape(block).astype(F32)
        for q in range(npeer):
            acc = acc + s_ref[q].astype(F32)
        o_ref[...] = acc

    return _pallas(
        body, name=name,
        grid_spec=pltpu.PrefetchScalarGridSpec(
            num_scalar_prefetch=1, grid=grid,
            in_specs=[pl.BlockSpec(own_block, own_map),
                      pl.BlockSpec((npeer,) + block, lambda i, j, p: (0, i, j))],
            out_specs=pl.BlockSpec(block, lambda i, j, p: (i, j))),
        out_shape=jax.ShapeDtypeStruct(shape, F32), compiler_params=_params(2),
    )(_chip_index(), own, slots)


def _pair_share(r_in0, r_in1, r_out0, r_out1):
    def body(a0, a1, b0, b1, g0, g1, h0, h1, send_sems, recv_sems):
        x, y, c = lax.axis_index("x"), lax.axis_index("y"), lax.axis_index("c")
        sib = (x, y, 1 - c)
        sends = [_remote(s, d, (send_sems, recv_sems), j, sib)
                 for j, (s, d) in enumerate(zip([a0, a1, b0, b1], [g0, g1, h0, h1]))]
        for cp in sends:
            cp.start()
        for cp in sends:
            cp.wait()

    return _pallas(
        body, name="pair_share", in_specs=[ANY] * 4, out_specs=[ANY] * 4,
        out_shape=[jax.ShapeDtypeStruct(r.shape, F32) for r in (r_in0, r_in1, r_out0, r_out1)],
        scratch_shapes=[pltpu.SemaphoreType.DMA((4,)), pltpu.SemaphoreType.DMA((4,))],
    )(r_in0, r_in1, r_out0, r_out1)


def _adam_math(g, w, m, v):
    c1 = 1.0 - ADAM_B1 ** ADAM_STEP
    c2 = 1.0 - ADAM_B2 ** ADAM_STEP
    m2 = ADAM_B1 * m + (1.0 - ADAM_B1) * g
    v2 = ADAM_B2 * v + (1.0 - ADAM_B2) * (g * g)
    delta = -ADAM_LR * ((m2 / c1) / (jnp.sqrt(v2 / c2) + ADAM_EPS) + ADAM_WD * w)
    return delta, m2, v2


def _adamw_nat(g_mine, g_sib, w, m, v, *, name, tr):
    rows, cw = w.shape
    nt = g_mine.shape[0] // tr

    def body(c_ref, gm_ref, gs_ref, w_ref, m_ref, v_ref, go_ref, d_ref, nm_ref, nv_ref):
        mine = pl.program_id(0) // nt == c_ref[0]
        gv = jnp.where(mine, gm_ref[...], gs_ref[...])[:, 0:cw]
        delta, m2, v2 = _adam_math(gv, w_ref[...], m_ref[...], v_ref[...])
        go_ref[...] = gv
        d_ref[...] = delta
        nm_ref[...] = m2
        nv_ref[...] = v2

    def mine_map(i, c_ref):
        return (jnp.where(i // nt == c_ref[0], i % nt, 0), 0)

    def sib_map(i, c_ref):
        return (jnp.where(i // nt == c_ref[0], 0, i % nt), 0)

    row = pl.BlockSpec((tr, cw), lambda i, c_ref: (i, 0))
    gspec = (tr, g_mine.shape[1])
    out = jax.ShapeDtypeStruct((rows, cw), F32)
    return _pallas(
        body, name=name,
        grid_spec=pltpu.PrefetchScalarGridSpec(
            num_scalar_prefetch=1, grid=(rows // tr,),
            in_specs=[pl.BlockSpec(gspec, mine_map), pl.BlockSpec(gspec, sib_map), row, row, row],
            out_specs=[row, row, row, row]),
        out_shape=[out, out, out, out], compiler_params=_params(1),
    )(_core_index(), g_mine, g_sib, w, m, v)


def _adamw(slots, w, m, v, *, name, tr):
    nd, rows, _ = slots.shape
    c1 = 1.0 - ADAM_B1 ** ADAM_STEP
    c2 = 1.0 - ADAM_B2 ** ADAM_STEP

    def body(s_ref, w_ref, m_ref, v_ref, g_ref, d_ref, nm_ref, nv_ref):
        g = s_ref[0]
        for d in range(1, nd):
            g = g + s_ref[d]
        m2 = ADAM_B1 * m_ref[...] + (1.0 - ADAM_B1) * g
        v2 = ADAM_B2 * v_ref[...] + (1.0 - ADAM_B2) * (g * g)
        g_ref[...] = g
        nm_ref[...] = m2
        nv_ref[...] = v2
        d_ref[...] = -ADAM_LR * ((m2 / c1) / (jnp.sqrt(v2 / c2) + ADAM_EPS) + ADAM_WD * w_ref[...])

    row = pl.BlockSpec((tr, LANE), lambda i: (i, 0))
    out = jax.ShapeDtypeStruct((rows, LANE), F32)
    return _pallas(
        body, name=name, grid=(rows // tr,),
        in_specs=[pl.BlockSpec((nd, tr, LANE), lambda i: (0, i, 0)), row, row, row],
        out_specs=[row, row, row, row], out_shape=[out, out, out, out], compiler_params=_params(1),
    )(slots, w, m, v)


def _rows(a):
    return a.reshape(-1, LANE)


def _pad_rows(a, mult):
    pad = (-a.shape[0]) % mult
    return jnp.pad(a, ((0, pad), (0, 0))) if pad else a


def _pack(parts, mult):
    return _pad_rows(jnp.concatenate([_rows(p) for p in parts], axis=0), mult)


def _unpack(slab, shapes):
    out, r0 = [], 0
    for shp in shapes:
        n = 1
        for s in shp:
            n *= s
        r = n // LANE
        out.append(slab[r0:r0 + r].reshape(shp))
        r0 += r
    return out


def _pack_rep(vecs, scal):
    srow = jnp.concatenate([s.reshape(-1) for s in scal] + [jnp.zeros((LANE - 3 * SSM_H,), F32)]).reshape(1, LANE)
    return _pad_rows(jnp.concatenate([_rows(vv) for vv in vecs] + [srow], axis=0), 8)


def _unpack_rep(slab, vec_shapes, scal_shape):
    vecs, r0 = [], 0
    for shp in vec_shapes:
        vecs.append(slab[r0:r0 + 8].reshape(shp))
        r0 += 8
    srow = slab[r0]
    scal = [srow[i * SSM_H:(i + 1) * SSM_H].reshape(scal_shape) for i in range(3)]
    return vecs, scal


def kernel(x, ev_norm_w, ev_w_in, ev_dw_w, ev_dw_b, ev_ln_w, ev_ln_b, ev_w_out, od_norm_w, od_w_in, od_conv_w, od_conv_b, od_dt_bias, od_a_log, od_d, od_gnorm_w, od_w_out, final_norm_w, loss_target, m_ev_norm_w, m_ev_w_in, m_ev_dw_w, m_ev_dw_b, m_ev_ln_w, m_ev_ln_b, m_ev_w_out, m_od_norm_w, m_od_w_in, m_od_conv_w, m_od_conv_b, m_od_dt_bias, m_od_a_log, m_od_d, m_od_gnorm_w, m_od_w_out, m_final_norm_w, v_ev_norm_w, v_ev_w_in, v_ev_dw_w, v_ev_dw_b, v_ev_ln_w, v_ev_ln_b, v_ev_w_out, v_od_norm_w, v_od_w_in, v_od_conv_w, v_od_conv_b, v_od_dt_bias, v_od_a_log, v_od_d, v_od_gnorm_w, v_od_w_out, v_final_norm_w):
    nb, seq, d = x.shape
    t = nb * seq
    nchip = 4
    xf = x.reshape(t, d)
    tgt = loss_target.reshape(t, d)

    big_w = [ev_w_in[0], od_w_in[0], ev_w_out[0], od_w_out[0]]
    small_w = [ev_dw_w[0], od_norm_w[0], od_conv_w[0], od_conv_b[0], od_gnorm_w[0]]
    small_shapes = [a.shape for a in small_w]
    big_b = [a.astype(BF16) for a in big_w]
    small_slab = _pack(small_w, 8)
    w_in0, w_out0, gath_small = _gather_shards(("cols", "rows"), [big_b[0], big_b[2]], small_slab)
    chip = 2 * lax.axis_index("x") + lax.axis_index("y")
    w_in0 = _place_cols(w_in0, big_b[0], name="place_w_in0")
    w_out0 = lax.dynamic_update_slice(w_out0, big_b[2], (chip * big_b[2].shape[0], 0))
    gath_small = lax.dynamic_update_slice(gath_small, small_slab[None], (chip, 0, 0))
    per_chip = [_unpack(gath_small[p], small_shapes) for p in range(nchip)]

    def cat(idx, axis):
        return jnp.concatenate([per_chip[p][idx] for p in range(nchip)], axis=axis)

    dw_w = jnp.pad(cat(0, 1), ((0, HALO - CONF_K), (0, 0)))
    dw_w8 = jnp.repeat(dw_w, SUB, axis=0)
    n1_w = cat(1, 0).reshape(1, d)
    conv_w = jnp.pad(cat(2, 1), ((0, PH - SSM_K), (0, 0)))
    conv_b = cat(3, 0).reshape(1, XBC)
    gn_w = cat(4, 0).reshape(1, D_INNER)

    def lanes(a):
        return jnp.pad(a.reshape(1, -1), ((0, 0), (0, LANE - a.size)))

    dt_bias, a_log = lanes(od_dt_bias), lanes(od_a_log)
    d_x = jnp.repeat(od_d.reshape(-1), SSM_P).reshape(1, D_INNER)
    hid = lax.broadcasted_iota(jnp.int32, (SSM_H, D_INNER), 1) // SSM_P
    ex = (hid == lax.broadcasted_iota(jnp.int32, (SSM_H, D_INNER), 0)).astype(BF16)
    ex_t = jnp.pad(ex.T, ((0, 0), (0, LANE - SSM_H)))
    fn_w = final_norm_w.reshape(1, d)

    n0 = _rms_fwd(xf, ev_norm_w, name="rms_fwd0")
    proj0, w_in1g, w_out1 = _matmul_with_gather(n0, w_in0, ("slab", "rows"), [big_b[1], big_b[3]],
                                                out_dtype=BF16, bm=512, bn=1024, name="in_proj0")
    w_in1g = lax.dynamic_update_slice(w_in1g, big_b[1][None], (chip, 0, 0))
    w_out1 = lax.dynamic_update_slice(w_out1, big_b[3], (chip * big_b[3].shape[0], 0))
    w_in1 = jnp.pad(jnp.concatenate([w_in1g[p] for p in range(nchip)], axis=1),
                    ((0, 0), (0, IN_ODD_PAD - IN_ODD)))
    y_conv, u2 = _conf_fwd(proj0, dw_w8, ev_dw_b, ev_ln_w, ev_ln_b, seq)
    o_att, y_att = _sba_fwd(proj0, nb, seq)
    ycat0 = jnp.concatenate([y_conv, y_att], axis=1)
    h1 = _matmul(ycat0, w_out0, mode="nn", out_dtype=F32, bm=512, bn=d, bk=D_INNER, name="out_proj0", residual=xf)
    n1 = _rms_fwd(h1, n1_w, name="rms_fwd1")
    proj1 = _matmul(n1, w_in1, mode="nn", out_dtype=BF16, bm=512, bn=768, bk=d, name="in_proj1", n_major=True)
    dt_raw = _matmul(n1, w_in1[:, D_INNER + XBC:IN_ODD_PAD], mode="nn", out_dtype=F32, bm=512,
                     bn=IN_ODD_PAD - D_INNER - XBC, bk=d, name="in_proj1_dt")
    xbc_c = _xconv_fwd(proj1, conv_w, conv_b, seq)
    dt = _dt_fwd(dt_raw, dt_bias)
    y_ssd, states = _ssd_fwd(xbc_c, dt, a_log, ex, nb, seq)
    yg = _gate_fwd(y_ssd, xbc_c, proj1, d_x, gn_w)
    h2 = _matmul(yg, w_out1, mode="nn", out_dtype=F32, bm=512, bn=d, bk=D_INNER, name="out_proj1", residual=h1)
    dh2, g_fn, loss_part = _final_loss(h2, fn_w, tgt)

    dyg = _matmul(dh2, w_out1, mode="nt", out_dtype=BF16, bm=512, bn=1024, bk=d, name="d_out_proj1")
    g_w_out1 = _matmul(yg, dh2, mode="tn", out_dtype=F32, bm=1024, bn=d, bk=1024, name="dw_out_proj1")
    dy_ssd, dz, g_gn, g_dx = _gate_bwd(dyg, y_ssd, xbc_c, proj1, d_x, gn_w)
    dxbc_c, ddt, g_a = _ssd_bwd(xbc_c, dt, a_log, ex, ex_t, states, dy_ssd, d_x, nb, seq)
    dproj1, g_conv_w, g_conv_b = _xconv_bwd(proj1, dxbc_c, conv_w, conv_b, dz, seq)
    dproj1, g_dt_bias = _dt_bwd(dt_raw, dt_bias, ddt, dproj1)
    dn1 = _matmul(dproj1, w_in1, mode="nt", out_dtype=BF16, bm=1024, bn=d, bk=1792, name="d_in_proj1")
    g_w_in1 = _matmul(n1, dproj1, mode="tn", out_dtype=F32, bm=d, bn=1792, bk=1024, name="dw_in_proj1")
    dh1, g_n1 = _rms_bwd(dn1, h1, n1_w, dh2, name="rms_bwd1")

    dycat0 = _matmul(dh1, w_out0, mode="nt", out_dtype=BF16, bm=512, bn=1024, bk=d, name="d_out_proj0")
    g_w_out0 = _matmul(ycat0, dh1, mode="tn", out_dtype=F32, bm=1024, bn=d, bk=1024, name="dw_out_proj0")
    dq, dk, dv, dga = _sba_bwd(proj0, o_att, dycat0, nb, seq)
    ro = D_INNER // nchip
    n1 = IN_ODD // nchip
    n1p = -(-n1 // LANE) * LANE
    g_w_out1c = g_w_out1.reshape(nchip, ro, d)
    q_in1, q_out1 = _pair_swap([g_w_in1, g_w_out1c], name="pair_swap_l1")
    s_in1n = _half_add(g_w_in1, q_in1, axis=0, block=(128, IN_ODD_PAD), name="half_add_in1")
    s_in1 = jnp.stack([jnp.pad(s_in1n[:, p * n1:(p + 1) * n1], ((0, 0), (0, n1p - n1))) for p in range(nchip)])
    s_out1 = _half_add(g_w_out1c, q_out1, axis=1, block=(1, ro // 2, d), name="half_add_out1")
    dpc, g_dw_w, g_dw_b, g_ln_w, g_ln_b, l_in1, l_out1 = _conf_bwd(
        proj0, u2, dycat0, dw_w8, ev_ln_w, ev_ln_b, seq, ("slab", "slab"), [s_in1, s_out1])
    dproj0 = jnp.concatenate([dpc, dq, dk, dv, dga], axis=1)
    g_w_in0 = _matmul(n0, dproj0, mode="tn", out_dtype=F32, bm=d, bn=1792, bk=1024, name="dw_in_proj0")
    g_w_out0c = g_w_out0.reshape(nchip, ro, d)
    q_in0, q_out0 = _pair_swap([g_w_in0, g_w_out0c], name="pair_swap_l0")
    s_in0 = _half_add(g_w_in0, q_in0, axis=0, block=(128, IN_EVEN), name="half_add_in0")
    s_out0 = _half_add(g_w_out0c, q_out0, axis=1, block=(1, ro // 2, d), name="half_add_out0")
    dn0, l_in0, l_out0 = _matmul(dproj0, w_in0, mode="nt", out_dtype=BF16, bm=1024, bn=d, bk=1792, name="d_in_proj0",
                                 comm_kinds=("cols", "slab"), comm_srcs=[s_in0, s_out0])
    grad_x, g_n0 = _rms_bwd(dn0, xf, ev_norm_w, dh1, name="rms_bwd0")

    g_dw_w = g_dw_w.reshape(HALO, SUB, CONV_W).sum(axis=1)[0:CONF_K]
    g_dw_b, g_ln_w, g_ln_b = (a.sum(axis=0, keepdims=True) for a in (g_dw_b, g_ln_w, g_ln_b))
    g_conv_w = g_conv_w.reshape(PH, SUB, XBC).sum(axis=1)[0:SSM_K]
    g_conv_b = g_conv_b.sum(axis=0, keepdims=True)
    a_neg = -jnp.exp(od_a_log.reshape(-1))
    g_a_log = g_a[0, 0:SSM_H] * a_neg
    g_d = g_dx.reshape(SSM_H, SSM_P).sum(axis=1)

    def chip_slab_small(p):
        c0, c1, c2, c3 = CONV_W // nchip, d // nchip, XBC // nchip, D_INNER // nchip
        return _pack([g_dw_w[:, p * c0:(p + 1) * c0], g_n1[0, p * c1:(p + 1) * c1],
                      g_conv_w[:, p * c2:(p + 1) * c2], g_conv_b[0, p * c2:(p + 1) * c2],
                      g_gn[0, p * c3:(p + 1) * c3]], 8)

    gsmall = jnp.stack([chip_slab_small(p) for p in range(nchip)])
    rep_vec_shapes = [ev_norm_w.shape, ev_dw_b.shape, ev_ln_w.shape, ev_ln_b.shape, final_norm_w.shape]
    grep = _pack_rep([g_n0, g_dw_b, g_ln_w, g_ln_b, g_fn], [g_dt_bias[0, 0:SSM_H], g_a_log, g_d])

    ssmall, srep = _pair_exchange([], gsmall, grep)
    r_in0 = _chip_sum(s_in0, l_in0, own_block=(128, IN_EVEN // nchip), own_map=lambda i, j, p: (i, p[0]),
                      block=(128, IN_EVEN // nchip), name="chip_sum_in0")
    r_in1 = _chip_sum(s_in1, l_in1, own_block=(1, 256, n1p), own_map=lambda i, j, p: (p[0], i, 0),
                      block=(256, n1p), name="chip_sum_in1")
    r_out0 = _chip_sum(s_out0, l_out0, own_block=(1, ro // 2, d), own_map=lambda i, j, p: (p[0], 0, 0),
                       block=(ro // 2, d), name="chip_sum_out0")
    r_out1 = _chip_sum(s_out1, l_out1, own_block=(1, ro // 2, d), own_map=lambda i, j, p: (p[0], 0, 0),
                       block=(ro // 2, d), name="chip_sum_out1")
    big_r = [r_in0, r_in1, r_out0, r_out1]
    big_q = _pair_share(*big_r)

    big_m = [m_ev_w_in[0], m_od_w_in[0], m_ev_w_out[0], m_od_w_out[0]]
    big_v = [v_ev_w_in[0], v_od_w_in[0], v_ev_w_out[0], v_od_w_out[0]]
    big_names = ["adamw_in0", "adamw_in1", "adamw_out0", "adamw_out1"]
    out_bigs = [_adamw_nat(gm, gs, w, m, v, name=nm, tr=128)
                for gm, gs, w, m, v, nm in zip(big_r, big_q, big_w, big_m, big_v, big_names)]

    def upd(slots, ws, ms, vs, packer, name, tr):
        return _adamw(slots, packer(ws), packer(ms), packer(vs), name=name, tr=tr)

    small_m = [m_ev_dw_w[0], m_od_norm_w[0], m_od_conv_w[0], m_od_conv_b[0], m_od_gnorm_w[0]]
    small_v = [v_ev_dw_w[0], v_od_norm_w[0], v_od_conv_w[0], v_od_conv_b[0], v_od_gnorm_w[0]]
    out_small = upd(ssmall, small_w, small_m, small_v, lambda a: _pack(a, 8), "adamw_small", ssmall.shape[1])

    def rep_pack(a):
        return _pack_rep(a[0:5], a[5:8])

    rep_w = [ev_norm_w, ev_dw_b, ev_ln_w, ev_ln_b, final_norm_w, od_dt_bias, od_a_log, od_d]
    rep_m = [m_ev_norm_w, m_ev_dw_b, m_ev_ln_w, m_ev_ln_b, m_final_norm_w, m_od_dt_bias, m_od_a_log, m_od_d]
    rep_v = [v_ev_norm_w, v_ev_dw_b, v_ev_ln_w, v_ev_ln_b, v_final_norm_w, v_od_dt_bias, v_od_a_log, v_od_d]
    out_rep = upd(srep, rep_w, rep_m, rep_v, rep_pack, "adamw_rep", srep.shape[1])

    results = []
    for kind in range(4):
        bw = [o[kind].reshape((1,) + o[kind].shape) for o in out_bigs]
        sw = _unpack(out_small[kind], small_shapes)
        vecs, scal = _unpack_rep(out_rep[kind], rep_vec_shapes, od_dt_bias.shape)
        results.append([
            vecs[0], bw[0], sw[0].reshape(ev_dw_w.shape), vecs[1], vecs[2], vecs[3], bw[2],
            sw[1].reshape(od_norm_w.shape), bw[1], sw[2].reshape(od_conv_w.shape), sw[3].reshape(od_conv_b.shape),
            scal[0], scal[1], scal[2], sw[4].reshape(od_gnorm_w.shape), bw[3], vecs[4]])
    loss = lax.psum(loss_part[0, 0], ("x", "y", "c"))
    return (loss, grad_x.reshape(x.shape), *results[0], *results[1], *results[2], *results[3])
```

```python
import jax
import jax.numpy as jnp
from jax import lax
from jax.experimental import pallas as pl
from jax.experimental.pallas import tpu as pltpu

F32 = jnp.float32
BF16 = jnp.bfloat16

D_MODEL = 1024
CONV_W = 1024
ATT_W = 1024
HEAD_DIM = 128
N_HEADS = 8
CONF_K = 31
IN_EVEN = 7168
D_INNER = 2048
SSM_P = 64
SSM_H = 32
SSM_G = 4
SSM_R = SSM_H // SSM_G
SSM_N = 128
SSM_K = 4
CHUNK = 128
XBC = D_INNER + 2 * SSM_G * SSM_N
IN_ODD = D_INNER + XBC + SSM_H
IN_ODD_PAD = 5376
EPS = 1e-6
QB = 128
NEG_CUT = -100.0

ADAM_LR = 0.001
ADAM_B1 = 0.9
ADAM_B2 = 0.999
ADAM_EPS = 1e-08
ADAM_WD = 0.01
ADAM_STEP = 10

LANE = 128
VMEM_LIMIT = 56 * 1024 * 1024
MESH = pl.DeviceIdType.MESH

NN = (((1,), (0,)), ((), ()))
NT = (((1,), (1,)), ((), ()))
TN = (((0,), (0,)), ((), ()))


def _pallas(body, **kw):
    return pl.pallas_call(body, **kw)


def _params(n_axes):
    return pltpu.CompilerParams(dimension_semantics=("arbitrary",) * n_axes, vmem_limit_bytes=VMEM_LIMIT)


def _dot(a, b, dims=NN):
    return lax.dot_general(a.astype(BF16), b.astype(BF16), dims, preferred_element_type=F32)


def _parts(x):
    h = x.astype(BF16)
    r = x - h.astype(F32)
    m = r.astype(BF16)
    l = (r - m.astype(F32)).astype(BF16)
    return (h, m, l)


def _dotx(x, e01, dims=NN):
    acc = None
    for p in _parts(x):
        t = lax.dot_general(p, e01, dims, preferred_element_type=F32)
        acc = t if acc is None else acc + t
    return acc


def _dotx2(x, e01, dims=NN):
    h = x.astype(BF16)
    l = (x - h.astype(F32)).astype(BF16)
    return (lax.dot_general(h, e01, dims, preferred_element_type=F32)
            + lax.dot_general(l, e01, dims, preferred_element_type=F32))


def _xdot(e01, x, dims=NN):
    acc = None
    for p in _parts(x):
        t = lax.dot_general(e01, p, dims, preferred_element_type=F32)
        acc = t if acc is None else acc + t
    return acc


def _f32(x):
    return x.astype(F32)


def _sigmoid(x):
    return 1.0 / (1.0 + jnp.exp(-x))


def _dsilu(x, s):
    return s * (1.0 + x * (1.0 - s))


def _matmul(a, b, *, mode, out_dtype, bm, bn, bk, name, residual=None, n_major=False, comm_kinds=(), comm_srcs=()):
    if mode == "nn":
        (m, k), n = a.shape, b.shape[1]
        a_blk, a_map = (bm, bk), lambda i, j, kk: (i, kk)
        b_blk, b_map = (bk, bn), lambda i, j, kk: (kk, j)
        dims = NN
    elif mode == "nt":
        (m, k), n = a.shape, b.shape[0]
        a_blk, a_map = (bm, bk), lambda i, j, kk: (i, kk)
        b_blk, b_map = (bn, bk), lambda i, j, kk: (j, kk)
        dims = NT
    else:
        (k, m), n = a.shape, b.shape[1]
        a_blk, a_map = (bk, bm), lambda i, j, kk: (kk, i)
        b_blk, b_map = (bk, bn), lambda i, j, kk: (kk, j)
        dims = TN
    bm, bn, bk = min(bm, m), min(bn, n), min(bk, k)
    if mode != "nn":
        a_blk = (bm, bk) if mode == "nt" else (bk, bm)
        b_blk = (bn, bk) if mode == "nt" else (bk, bn)
    else:
        a_blk, b_blk = (bm, bk), (bk, bn)
    assert m % bm == 0 and n % bn == 0 and k % bk == 0, (name, m, n, k)
    nk = k // bk
    has_res = residual is not None

    def order(f):
        return (lambda j, i, kk: f(i, j, kk)) if n_major else f

    nw = len(comm_srcs)
    grid = (n // bn, m // bm, nk) if n_major else (m // bm, n // bn, nk)

    def body(*refs):
        a_ref, b_ref = refs[0], refs[1]
        r_ref = refs[2] if has_res else None
        n_in = 2 + has_res + nw
        o_ref = refs[n_in]
        n_out = n_in + 1 + nw

        def finish(r):
            if has_res:
                r = r + r_ref[...]
            o_ref[...] = r.astype(out_dtype)

        def compute():
            if nk == 1:
                finish(_dot(a_ref[...], b_ref[...], dims))
                return
            acc_ref = refs[n_out]
            kk = pl.program_id(2)

            @pl.when(kk == 0)
            def _():
                acc_ref[...] = jnp.zeros_like(acc_ref)

            acc_ref[...] += _dot(a_ref[...], b_ref[...], dims)

            @pl.when(kk == nk - 1)
            def _():
                finish(acc_ref[...])

        if not nw:
            compute()
            return
        start, done = _chip_plan(comm_kinds, refs[2 + has_res:n_in], refs[n_in + 1:n_out], refs[n_out + (nk > 1):])
        ids = [pl.program_id(ax) for ax in range(3)]

        @pl.when(jnp.logical_and(jnp.logical_and(ids[0] == 0, ids[1] == 0), ids[2] == 0))
        def _():
            start()

        compute()

        @pl.when(jnp.logical_and(jnp.logical_and(ids[0] == grid[0] - 1, ids[1] == grid[1] - 1), ids[2] == grid[2] - 1))
        def _():
            done()

    in_specs = [pl.BlockSpec(a_blk, order(a_map)), pl.BlockSpec(b_blk, order(b_map))]
    args = [a, b]
    out_map = order(lambda i, j, kk: (i, j))
    if has_res:
        in_specs.append(pl.BlockSpec((bm, bn), out_map))
        args.append(residual)
    any_spec = pl.BlockSpec(memory_space=pl.ANY)
    out_specs = [pl.BlockSpec((bm, bn), out_map)] + [any_spec] * nw
    out_shape = [jax.ShapeDtypeStruct((m, n), out_dtype)] + [_landing_shape(kd, s) for kd, s in zip(comm_kinds, comm_srcs)]
    res = _pallas(
        body, name=name, grid=grid, in_specs=in_specs + [any_spec] * nw, out_specs=out_specs, out_shape=out_shape,
        scratch_shapes=([pltpu.VMEM((bm, bn), F32)] if nk > 1 else []) + (_chip_sems(nw) if nw else []),
        compiler_params=_params(3),
    )(*args, *comm_srcs)
    return res if nw else res[0]


def _rms_fwd(x, w, *, name, tm=512):
    t, d = x.shape

    def body(x_ref, w_ref, o_ref):
        xv = x_ref[...]
        r = lax.rsqrt(jnp.mean(xv * xv, axis=1, keepdims=True) + EPS)
        o_ref[...] = (xv * r * w_ref[...]).astype(BF16)

    return _pallas(
        body, name=name, grid=(t // tm,),
        in_specs=[pl.BlockSpec((tm, d), lambda i: (i, 0)), pl.BlockSpec((1, d), lambda i: (0, 0))],
        out_specs=pl.BlockSpec((tm, d), lambda i: (i, 0)),
        out_shape=jax.ShapeDtypeStruct((t, d), BF16), compiler_params=_params(1),
    )(x, w)


def _rms_bwd(dn, x, w, dres, *, name, tm=512):
    t, d = x.shape

    def body(dn_ref, x_ref, w_ref, dr_ref, dx_ref, dw_ref):
        i = pl.program_id(0)
        xv = x_ref[...]
        r = lax.rsqrt(jnp.mean(xv * xv, axis=1, keepdims=True) + EPS)
        xh = xv * r
        dy = dn_ref[...].astype(F32)
        g = dy * w_ref[...]
        dx_ref[...] = dr_ref[...] + r * (g - xh * jnp.mean(g * xh, axis=1, keepdims=True))

        @pl.when(i == 0)
        def _():
            dw_ref[...] = jnp.zeros_like(dw_ref)

        dw_ref[...] += jnp.sum(dy * xh, axis=0, keepdims=True)

    row = pl.BlockSpec((tm, d), lambda i: (i, 0))
    vec = pl.BlockSpec((1, d), lambda i: (0, 0))
    return _pallas(
        body, name=name, grid=(t // tm,), in_specs=[row, row, vec, row], out_specs=[row, vec],
        out_shape=[jax.ShapeDtypeStruct((t, d), F32), jax.ShapeDtypeStruct((1, d), F32)],
        compiler_params=_params(1),
    )(dn, x, w, dres)


def _final_loss(h, w, target, *, tm=512):
    t, d = h.shape

    def body(h_ref, w_ref, t_ref, dh_ref, dw_ref, loss_ref):
        i = pl.program_id(0)
        xv = h_ref[...]
        r = lax.rsqrt(jnp.mean(xv * xv, axis=1, keepdims=True) + EPS)
        xh = xv * r
        wv = w_ref[...]
        err = xh * wv - t_ref[...]
        dy = err * (1.0 / d)
        g = dy * wv
        dh_ref[...] = r * (g - xh * jnp.mean(g * xh, axis=1, keepdims=True))

        @pl.when(i == 0)
        def _():
            dw_ref[...] = jnp.zeros_like(dw_ref)
            loss_ref[...] = jnp.zeros_like(loss_ref)

        dw_ref[...] += jnp.sum(dy * xh, axis=0, keepdims=True)
        part = jnp.sum(jnp.sum(err * err, axis=1, keepdims=True), axis=0, keepdims=True)
        loss_ref[...] += part * (0.5 / d)

    row = pl.BlockSpec((tm, d), lambda i: (i, 0))
    vec = pl.BlockSpec((1, d), lambda i: (0, 0))
    return _pallas(
        body, name="final_loss", grid=(t // tm,), in_specs=[row, vec, row],
        out_specs=[row, vec, pl.BlockSpec((1, LANE), lambda i: (0, 0))],
        out_shape=[jax.ShapeDtypeStruct((t, d), F32), jax.ShapeDtypeStruct((1, d), F32),
                   jax.ShapeDtypeStruct((1, LANE), F32)],
        compiler_params=_params(1),
    )(h, w, target)


HALO = 32


SUB = 8
RC = 16


def _make_shifts(sh_ref, rows, shifts=tuple(range(1, SUB))):
    for s in shifts:
        sh_ref[s, 0:rows, :] = sh_ref[0, s:s + rows, :]


def _shifted(sh_ref, r0, j, rows):
    return sh_ref[j % SUB, pl.ds(r0 + (j - j % SUB), rows), :]


def _taps(w8_ref, sh_ref, r0, first, step, init):
    accs = [init] * (RC // SUB)
    for k in range(CONF_K):
        wk = w8_ref[k * SUB:(k + 1) * SUB, :]
        x = _shifted(sh_ref, r0, first + step * k, RC)
        accs = [a + wk * x[q * SUB:(q + 1) * SUB] for q, a in enumerate(accs)]
    return jnp.concatenate(accs, axis=0)


def _conf_fwd(proj, dw_w, dw_b, ln_w, ln_b, seq, *, tm=256):
    t = proj.shape[0]
    c = CONV_W
    tps = seq // tm
    hb = tm // HALO

    def body(a_ref, b_ref, g_ref, ha_ref, hb_ref, w_ref, wb_ref, lw_ref, lb_ref, y_ref, u2_ref, sh_ref):
        i = pl.program_id(0)
        keep = jnp.where(i % tps == 0, 0.0, 1.0)
        sh_ref[0, 0:HALO, :] = _f32(ha_ref[...]) * _sigmoid(_f32(hb_ref[...])) * keep
        sh_ref[0, HALO:HALO + tm, :] = _f32(a_ref[...]) * _sigmoid(_f32(b_ref[...]))
        _make_shifts(sh_ref, tm + HALO - SUB)

        def chunk(ci, carry):
            r0 = pl.multiple_of(ci * RC, RC)
            acc = _taps(w_ref, sh_ref, r0, HALO - CONF_K + 1, 1, jnp.broadcast_to(wb_ref[...], (SUB, c)))
            u2_ref[pl.ds(r0, RC), :] = acc
            mu = jnp.mean(acc, axis=1, keepdims=True)
            xc = acc - mu
            rs = lax.rsqrt(jnp.mean(xc * xc, axis=1, keepdims=True) + EPS)
            u3 = xc * rs * lw_ref[...] + lb_ref[...]
            gv = _f32(g_ref[pl.ds(r0, RC), :])
            y_ref[pl.ds(r0, RC), :] = (u3 * _sigmoid(u3) * gv * _sigmoid(gv)).astype(BF16)
            return carry

        lax.fori_loop(0, tm // RC, chunk, 0, unroll=2)

    def col(j):
        return pl.BlockSpec((tm, c), lambda i: (i, j))

    def prev(j):
        return pl.BlockSpec((HALO, c), lambda i: (jnp.maximum(i * hb - 1, 0), j))

    vec = pl.BlockSpec((1, c), lambda i: (0, 0))
    return _pallas(
        body, name="conf_fwd", grid=(t // tm,),
        in_specs=[col(0), col(1), col(2), prev(0), prev(1),
                  pl.BlockSpec((HALO * SUB, c), lambda i: (0, 0)), vec, vec, vec],
        out_specs=[pl.BlockSpec((tm, c), lambda i: (i, 0)), pl.BlockSpec((tm, c), lambda i: (i, 0))],
        out_shape=[jax.ShapeDtypeStruct((t, c), BF16), jax.ShapeDtypeStruct((t, c), F32)],
        scratch_shapes=[pltpu.VMEM((SUB, tm + HALO, c), F32)], compiler_params=_params(1),
    )(proj, proj, proj, proj, proj, dw_w, dw_b, ln_w, ln_b)


def _conf_bwd(proj, u2, dycat, dw_w, ln_w, ln_b, seq, comm_kinds, comm_srcs, *, tm=256):
    t = proj.shape[0]
    c = CONV_W
    tps = seq // tm
    hb = tm // HALO
    nhb = t // HALO
    nw = len(comm_srcs)
    nsteps = t // tm

    def fold(v):
        out = v[0:SUB]
        for q in range(1, RC // SUB):
            out = out + v[q * SUB:(q + 1) * SUB]
        return out

    def body(*refs):
        (a_ref, b_ref, g_ref, pa_ref, pb_ref, ng_ref, u2_ref, nu2_ref, dy_ref, ndy_ref,
         w_ref, lw_ref, lb_ref) = refs[:13]
        dp_ref, dww_ref, dwb_ref, dlw_ref, dlb_ref = refs[13 + nw:18 + nw]
        su_ref, sd_ref = refs[18 + 2 * nw:20 + 2 * nw]
        comm_start, comm_finish = _chip_plan(comm_kinds, refs[13:13 + nw], refs[18 + nw:18 + 2 * nw],
                                             refs[20 + 2 * nw:])
        i = pl.program_id(0)
        first = i % tps == 0
        last = i % tps == tps - 1

        @pl.when(i == 0)
        def _():
            comm_start()
            dww_ref[...] = jnp.zeros_like(dww_ref)
            dwb_ref[...] = jnp.zeros_like(dwb_ref)
            dlw_ref[...] = jnp.zeros_like(dlw_ref)
            dlb_ref[...] = jnp.zeros_like(dlb_ref)

        su_ref[0, 0:HALO, :] = _f32(pa_ref[...]) * _sigmoid(_f32(pb_ref[...])) * jnp.where(first, 0.0, 1.0)
        su_ref[0, HALO:HALO + tm, :] = _f32(a_ref[...]) * _sigmoid(_f32(b_ref[...]))
        _make_shifts(su_ref, tm + HALO - SUB)

        def ln_back(u2c, gv, dy):
            mu = jnp.mean(u2c, axis=1, keepdims=True)
            xc = u2c - mu
            rs = lax.rsqrt(jnp.mean(xc * xc, axis=1, keepdims=True) + EPS)
            xh = xc * rs
            lw = lw_ref[...]
            u3 = xh * lw + lb_ref[...]
            s3 = _sigmoid(u3)
            sg = _sigmoid(gv)
            dgc = dy * (u3 * s3) * _dsilu(gv, sg)
            du3 = dy * gv * sg * _dsilu(u3, s3)
            dxh = du3 * lw
            du2 = rs * (dxh - jnp.mean(dxh, axis=1, keepdims=True)
                        - xh * jnp.mean(dxh * xh, axis=1, keepdims=True))
            return du2, dgc, du3, xh

        def tile_chunk(ci, carry):
            r0 = pl.multiple_of(ci * RC, RC)
            rows = pl.ds(r0, RC)
            du2, dgc, du3, xh = ln_back(u2_ref[rows, :], _f32(g_ref[rows, :]), _f32(dy_ref[rows, :]))
            sd_ref[0, rows, :] = du2
            dp_ref[rows, 2 * c:3 * c] = dgc.astype(BF16)
            dwb_ref[...] += fold(du2)
            dlw_ref[...] += fold(du3 * xh)
            dlb_ref[...] += fold(du3)
            return carry

        lax.fori_loop(0, tm // RC, tile_chunk, 0, unroll=2)
        live = jnp.where(last, 0.0, 1.0)
        for ci in range(HALO // RC):
            rows = slice(ci * RC, (ci + 1) * RC)
            du2, _, _, _ = ln_back(nu2_ref[rows, :], _f32(ng_ref[rows, :]), _f32(ndy_ref[rows, :]))
            sd_ref[0, tm + ci * RC:tm + (ci + 1) * RC, :] = du2 * live
        _make_shifts(sd_ref, tm + HALO - SUB)

        def tap_chunk(ci, carry):
            r0 = pl.multiple_of(ci * RC, RC)
            rows = pl.ds(r0, RC)
            du1 = _taps(w_ref, sd_ref, r0, CONF_K - 1, -1, jnp.zeros((SUB, c), F32))
            sb = _sigmoid(_f32(b_ref[rows, :]))
            dp_ref[rows, 0:c] = (du1 * sb).astype(BF16)
            dp_ref[rows, c:2 * c] = (du1 * _f32(a_ref[rows, :]) * sb * (1.0 - sb)).astype(BF16)
            du2 = sd_ref[0, rows, :]
            for k in range(CONF_K):
                dww_ref[k * SUB:(k + 1) * SUB, :] += fold(du2 * _shifted(su_ref, r0, HALO - CONF_K + 1 + k, RC))
            return carry

        lax.fori_loop(0, tm // RC, tap_chunk, 0)

        @pl.when(i == nsteps - 1)
        def _():
            comm_finish()

    def col(j):
        return pl.BlockSpec((tm, c), lambda i: (i, j))

    def prev(j):
        return pl.BlockSpec((HALO, c), lambda i: (jnp.maximum(i * hb - 1, 0), j))

    def nxt(j):
        return pl.BlockSpec((HALO, c), lambda i: (jnp.minimum((i + 1) * hb, nhb - 1), j))

    vec = pl.BlockSpec((1, c), lambda i: (0, 0))
    acc = pl.BlockSpec((SUB, c), lambda i: (0, 0))
    any_spec = pl.BlockSpec(memory_space=pl.ANY)
    return _pallas(
        body, name="conf_bwd", grid=(nsteps,),
        in_specs=[col(0), col(1), col(2), prev(0), prev(1), nxt(2), col(0), nxt(0), col(0), nxt(0),
                  pl.BlockSpec((HALO * SUB, c), lambda i: (0, 0)), vec, vec] + [any_spec] * nw,
        out_specs=[pl.BlockSpec((tm, 3 * c), lambda i: (i, 0)),
                   pl.BlockSpec((HALO * SUB, c), lambda i: (0, 0)), acc, acc, acc] + [any_spec] * nw,
        out_shape=[jax.ShapeDtypeStruct((t, 3 * c), BF16), jax.ShapeDtypeStruct((HALO * SUB, c), F32),
                   jax.ShapeDtypeStruct((SUB, c), F32), jax.ShapeDtypeStruct((SUB, c), F32),
                   jax.ShapeDtypeStruct((SUB, c), F32)]
        + [_landing_shape(kd, s) for kd, s in zip(comm_kinds, comm_srcs)],
        scratch_shapes=[pltpu.VMEM((SUB, tm + HALO, c), F32), pltpu.VMEM((SUB, tm + HALO, c), F32)] + _chip_sems(nw),
        compiler_params=_params(1),
    )(proj, proj, proj, proj, proj, proj, u2, u2, dycat, dycat, dw_w, ln_w, ln_b, *comm_srcs)


Q_COL = 3 * CONV_W // HEAD_DIM
K_COL = Q_COL + N_HEADS
V_COL = K_COL + N_HEADS
GA_COL = V_COL + N_HEADS


SBA_TQ = 256
SBA_WK = 4 * QB


def _sb_window(qs, kw, ws, limit, t0, carry):
    tq, wk = qs.shape[0], kw.shape[0]
    z = _dot(qs, kw, NT)
    sg = ws + lax.broadcasted_iota(jnp.int32, (tq, wk), 1)
    tg = t0 + lax.broadcasted_iota(jnp.int32, (tq, wk), 0)
    mask = sg < jnp.minimum(tg, limit)
    sp = jnp.log(1.0 + jnp.exp(-jnp.abs(z)))
    ls = jnp.minimum(z, 0.0) - sp
    lk = jnp.where(mask, ls - z, 0.0)
    jj = lax.broadcasted_iota(jnp.int32, (QB, QB), 0)
    ss = lax.broadcasted_iota(jnp.int32, (QB, QB), 1)
    ustrict = jnp.where(jj > ss, 1.0, 0.0).astype(BF16)
    laters = [None] * (wk // QB)
    for ch in reversed(range(wk // QB)):
        lkc = lk[:, ch * QB:(ch + 1) * QB]
        laters[ch] = carry + _dotx2(lkc, ustrict)
        carry = carry + jnp.sum(lkc, axis=1, keepdims=True)
    w = jnp.where(mask, jnp.exp(ls + jnp.concatenate(laters, axis=1)), 0.0)
    return mask, ls, w, carry


def _sba_fwd(proj, nb, seq, *, tq=SBA_TQ, wk=SBA_WK):
    t = proj.shape[0]
    wk = min(wk, seq)
    nq = seq // tq
    scale = HEAD_DIM ** -0.5

    def body(q_ref, k_ref, v_ref, g_ref, o_ref, y_ref):
        i = pl.program_id(2)
        t0 = i * tq
        qs = (_f32(q_ref[...]) * scale).astype(BF16)

        def window(ws, limit, carry, acc):
            ws = pl.multiple_of(ws, QB)
            _, _, w, carry = _sb_window(qs, k_ref[pl.ds(ws, wk), :], ws, limit, t0, carry)
            return carry, acc + _dot(w, v_ref[pl.ds(ws, wk), :])

        ws0 = jnp.maximum(t0 + tq - wk, 0)
        carry, acc = window(ws0, seq, jnp.zeros((tq, 1), F32), jnp.zeros((tq, HEAD_DIM), F32))

        def cond(st):
            return jnp.logical_and(st[0] > 0, jnp.max(st[1]) > NEG_CUT)

        def step(st):
            c2, a2 = window(jnp.maximum(st[0] - wk, 0), st[0], st[1], st[2])
            return jnp.maximum(st[0] - wk, 0), c2, a2

        _, _, acc = lax.while_loop(cond, step, (ws0, carry, acc))
        o_ref[...] = acc
        gv = _f32(g_ref[...])
        y_ref[...] = (acc * gv * _sigmoid(gv)).astype(BF16)

    def tile(c0):
        return pl.BlockSpec((tq, HEAD_DIM), lambda b, h, i: (b * nq + i, c0 + h))

    def whole(c0):
        return pl.BlockSpec((seq, HEAD_DIM), lambda b, h, i: (b, c0 + h))

    return _pallas(
        body, name="sba_fwd", grid=(nb, N_HEADS, nq),
        in_specs=[tile(Q_COL), whole(K_COL), whole(V_COL), tile(GA_COL)],
        out_specs=[tile(0), tile(0)],
        out_shape=[jax.ShapeDtypeStruct((t, ATT_W), F32), jax.ShapeDtypeStruct((t, ATT_W), BF16)],
        compiler_params=_params(3),
    )(proj, proj, proj, proj)


def _sba_bwd(proj, o, dycat, nb, seq, *, tq=SBA_TQ, wk=SBA_WK):
    t = proj.shape[0]
    wk = min(wk, seq)
    nq = seq // tq
    nwin = -(-seq // wk) + 1
    nch = wk // QB
    scale = HEAD_DIM ** -0.5

    def body(q_ref, k_ref, v_ref, g_ref, o_ref, dy_ref, dq_ref, dko_ref, dvo_ref, dg_ref, e_ref, sp_ref,
             dk_ref, dv_ref):
        i = pl.program_id(2)
        t0 = i * tq

        @pl.when(i == 0)
        def _():
            dk_ref[...] = jnp.zeros_like(dk_ref)
            dv_ref[...] = jnp.zeros_like(dv_ref)

        qs = (_f32(q_ref[...]) * scale).astype(BF16)
        gv = _f32(g_ref[...])
        sg = _sigmoid(gv)
        dy = _f32(dy_ref[...])
        do = (dy * gv * sg).astype(BF16)
        dg_ref[...] = (dy * o_ref[...] * _dsilu(gv, sg)).astype(BF16)

        def start_of(n):
            return pl.multiple_of(jnp.maximum(t0 + tq - (n + 1) * wk, 0), QB)

        def limit_of(n):
            return jnp.where(n == 0, seq, jnp.maximum(t0 + tq - n * wk, 0))

        def near(n, carry):
            ws = start_of(n)
            _, ls, w, carry = _sb_window(qs, k_ref[pl.ds(ws, wk), :], ws, limit_of(n), t0, carry)
            e_ref[n] = w * _dot(do, v_ref[pl.ds(ws, wk), :], NT)
            sp_ref[n] = jnp.exp(ls)
            dv_ref[pl.ds(ws, wk), :] += _dot(w, do, TN)
            return carry

        carry = near(0, jnp.zeros((tq, 1), F32))

        def cond(st):
            return jnp.logical_and(start_of(st[0] - 1) > 0, jnp.max(st[1]) > NEG_CUT)

        def step(st):
            return st[0] + 1, near(st[0], st[1])

        nvis, _ = lax.while_loop(cond, step, (1, carry))

        jj = lax.broadcasted_iota(jnp.int32, (QB, QB), 0)
        ss = lax.broadcasted_iota(jnp.int32, (QB, QB), 1)
        lstrict = jnp.where(jj < ss, 1.0, 0.0).astype(BF16)

        def far(r, st):
            pre, dq = st
            n = nvis - 1 - r
            ws = start_of(n)
            e = e_ref[n]
            spn = sp_ref[n]
            gs = []
            for ch in range(nch):
                ec = e[:, ch * QB:(ch + 1) * QB]
                gs.append(pre + _dotx2(ec, lstrict))
                pre = pre + jnp.sum(ec, axis=1, keepdims=True)
            sgl = ws + lax.broadcasted_iota(jnp.int32, (tq, wk), 1)
            tgl = t0 + lax.broadcasted_iota(jnp.int32, (tq, wk), 0)
            mask = sgl < jnp.minimum(tgl, limit_of(n))
            dz = jnp.where(mask, e * (1.0 - spn) - jnp.concatenate(gs, axis=1) * spn, 0.0).astype(BF16)
            dk_ref[pl.ds(ws, wk), :] += _dot(dz, qs, TN)
            return pre, dq + _dot(dz, k_ref[pl.ds(ws, wk), :])

        _, dq = lax.fori_loop(0, nvis, far, (jnp.zeros((tq, 1), F32), jnp.zeros((tq, HEAD_DIM), F32)))
        dq_ref[...] = (dq * scale).astype(BF16)

        @pl.when(i == nq - 1)
        def _():
            dko_ref[...] = dk_ref[...].astype(BF16)
            dvo_ref[...] = dv_ref[...].astype(BF16)

    def tile(c0):
        return pl.BlockSpec((tq, HEAD_DIM), lambda b, h, i: (b * nq + i, c0 + h))

    def whole(c0):
        return pl.BlockSpec((seq, HEAD_DIM), lambda b, h, i: (b, c0 + h))

    return _pallas(
        body, name="sba_bwd", grid=(nb, N_HEADS, nq),
        in_specs=[tile(Q_COL), whole(K_COL), whole(V_COL), tile(GA_COL), tile(0),
                  tile(CONV_W // HEAD_DIM)],
        out_specs=[tile(0), whole(0), whole(0), tile(0)],
        out_shape=[jax.ShapeDtypeStruct((t, ATT_W), BF16)] * 4,
        scratch_shapes=[pltpu.VMEM((nwin, tq, wk), F32), pltpu.VMEM((nwin, tq, wk), F32),
                        pltpu.VMEM((seq, HEAD_DIM), F32), pltpu.VMEM((seq, HEAD_DIM), F32)],
        compiler_params=_params(3),
    )(proj, proj, proj, proj, o, dycat)


CT = 512
PH = 8
XRC = 32
X_SHIFTS = tuple(s for s in range(PH - SSM_K + 1, PH))
D_SHIFTS = tuple(range(1, SSM_K))
XBC_BLK = D_INNER // CT


def _softplus(x):
    return jnp.maximum(x, 0.0) + jnp.log(1.0 + jnp.exp(-jnp.abs(x)))


def _dt_fwd(proj, dt_bias, *, tm=512):
    t = proj.shape[0]

    def body(p_ref, b_ref, o_ref):
        o_ref[...] = _softplus(p_ref[...] + b_ref[...])

    return _pallas(
        body, name="dt_fwd", grid=(t // tm,),
        in_specs=[pl.BlockSpec((tm, LANE), lambda i: (i, 0)), pl.BlockSpec((1, LANE), lambda i: (0, 0))],
        out_specs=pl.BlockSpec((tm, LANE), lambda i: (i, 0)),
        out_shape=jax.ShapeDtypeStruct((t, LANE), F32), compiler_params=_params(1),
    )(proj, dt_bias)


def _dt_bwd(proj, dt_bias, ddt, dproj, *, tm=512):
    t = proj.shape[0]
    wide = IN_ODD_PAD - D_INNER - XBC

    def body(p_ref, b_ref, d_ref, dp_any, o_ref, db_ref):
        i = pl.program_id(0)
        lanes = lax.broadcasted_iota(jnp.int32, (tm, LANE), 1)
        dr = jnp.where(lanes < SSM_H, d_ref[...] * _sigmoid(p_ref[...] + b_ref[...]), 0.0)
        o_ref[:, 0:LANE] = dr.astype(BF16)
        o_ref[:, LANE:wide] = jnp.zeros((tm, wide - LANE), BF16)

        @pl.when(i == 0)
        def _():
            db_ref[...] = jnp.zeros_like(db_ref)

        db_ref[...] += jnp.sum(dr, axis=0, keepdims=True)

    vec = pl.BlockSpec((1, LANE), lambda i: (0, 0))
    row = pl.BlockSpec((tm, LANE), lambda i: (i, 0))
    return _pallas(
        body, name="dt_bwd", grid=(t // tm,),
        in_specs=[pl.BlockSpec((tm, LANE), lambda i: (i, 0)), vec, row, pl.BlockSpec(memory_space=pl.ANY)],
        out_specs=[pl.BlockSpec((tm, wide), lambda i: (i, (D_INNER + XBC) // wide)), vec],
        out_shape=[jax.ShapeDtypeStruct(dproj.shape, dproj.dtype), jax.ShapeDtypeStruct((1, LANE), F32)],
        input_output_aliases={3: 0}, compiler_params=_params(1),
    )(proj, dt_bias, ddt, dproj)


def _xconv_fwd(proj, conv_w, conv_b, seq, *, tm=512):
    t = proj.shape[0]
    tps = seq // tm
    hb = tm // PH

    def body(x_ref, h_ref, w_ref, b_ref, o_ref, sh_ref):
        i = pl.program_id(1)
        sh_ref[0, 0:PH, :] = _f32(h_ref[...]) * jnp.where(i % tps == 0, 0.0, 1.0)
        sh_ref[0, PH:PH + tm, :] = _f32(x_ref[...])
        _make_shifts(sh_ref, tm, X_SHIFTS)

        def chunk(ci, carry):
            r0 = pl.multiple_of(ci * XRC, XRC)
            acc = jnp.zeros((XRC, CT), F32) + b_ref[...]
            for k in range(SSM_K):
                acc = acc + w_ref[k:k + 1, :] * _shifted(sh_ref, r0, PH - SSM_K + 1 + k, XRC)
            o_ref[pl.ds(r0, XRC), :] = acc * _sigmoid(acc)
            return carry

        lax.fori_loop(0, tm // XRC, chunk, 0)

    return _pallas(
        body, name="xconv_fwd", grid=(XBC // CT, t // tm),
        in_specs=[pl.BlockSpec((tm, CT), lambda j, i: (i, XBC_BLK + j)),
                  pl.BlockSpec((PH, CT), lambda j, i: (jnp.maximum(i * hb - 1, 0), XBC_BLK + j)),
                  pl.BlockSpec((PH, CT), lambda j, i: (0, j)),
                  pl.BlockSpec((1, CT), lambda j, i: (0, j))],
        out_specs=pl.BlockSpec((tm, CT), lambda j, i: (i, j)),
        out_shape=jax.ShapeDtypeStruct((t, XBC), F32),
        scratch_shapes=[pltpu.VMEM((SUB, tm + PH, CT), F32)], compiler_params=_params(2),
    )(proj, proj, conv_w, conv_b)


def _xconv_bwd(proj, dxc, conv_w, conv_b, dproj, seq, *, tm=512):
    t = proj.shape[0]
    tps = seq // tm
    hb = tm // PH
    nhb = t // PH
    te = tm + PH

    def fold(v):
        out = v[0:SUB]
        for q in range(1, v.shape[0] // SUB):
            out = out + v[q * SUB:(q + 1) * SUB]
        return out

    def body(x_ref, p_ref, n_ref, d_ref, nd_ref, w_ref, b_ref, dp_any, dx_ref, dw_ref, db_ref, sx_ref, sd_ref):
        i = pl.program_id(1)
        first = i % tps == 0
        last = i % tps == tps - 1

        @pl.when(i == 0)
        def _():
            dw_ref[...] = jnp.zeros_like(dw_ref)
            db_ref[...] = jnp.zeros_like(db_ref)

        sx_ref[0, 0:PH, :] = _f32(p_ref[...]) * jnp.where(first, 0.0, 1.0)
        sx_ref[0, PH:PH + tm, :] = _f32(x_ref[...])
        sx_ref[0, PH + tm:PH + te, :] = _f32(n_ref[...])
        _make_shifts(sx_ref, te, X_SHIFTS)

        def dv_of(r0, rows, dy):
            acc = jnp.zeros((rows, CT), F32) + b_ref[...]
            for k in range(SSM_K):
                acc = acc + w_ref[k:k + 1, :] * _shifted(sx_ref, r0, PH - SSM_K + 1 + k, rows)
            return dy * _dsilu(acc, _sigmoid(acc))

        def dv_chunk(ci, carry):
            r0 = pl.multiple_of(ci * XRC, XRC)
            dv = dv_of(r0, XRC, d_ref[pl.ds(r0, XRC), :])
            sd_ref[0, pl.ds(r0, XRC), :] = dv
            db_ref[...] += fold(dv)
            return carry

        lax.fori_loop(0, tm // XRC, dv_chunk, 0)
        sd_ref[0, tm:te, :] = dv_of(tm, PH, nd_ref[...]) * jnp.where(last, 0.0, 1.0)
        _make_shifts(sd_ref, tm, D_SHIFTS)

        def tap_chunk(ci, carry):
            r0 = pl.multiple_of(ci * XRC, XRC)
            dx = jnp.zeros((XRC, CT), F32)
            for k in range(SSM_K):
                dx = dx + w_ref[k:k + 1, :] * _shifted(sd_ref, r0, SSM_K - 1 - k, XRC)
            dx_ref[pl.ds(r0, XRC), :] = dx.astype(BF16)
            dv = sd_ref[0, pl.ds(r0, XRC), :]
            for k in range(SSM_K):
                dw_ref[k * SUB:(k + 1) * SUB, :] += fold(dv * _shifted(sx_ref, r0, PH - SSM_K + 1 + k, XRC))
            return carry

        lax.fori_loop(0, tm // XRC, tap_chunk, 0)

    return _pallas(
        body, name="xconv_bwd", grid=(XBC // CT, t // tm),
        in_specs=[pl.BlockSpec((tm, CT), lambda j, i: (i, XBC_BLK + j)),
                  pl.BlockSpec((PH, CT), lambda j, i: (jnp.maximum(i * hb - 1, 0), XBC_BLK + j)),
                  pl.BlockSpec((PH, CT), lambda j, i: (jnp.minimum((i + 1) * hb, nhb - 1), XBC_BLK + j)),
                  pl.BlockSpec((tm, CT), lambda j, i: (i, j)),
                  pl.BlockSpec((PH, CT), lambda j, i: (jnp.minimum((i + 1) * hb, nhb - 1), j)),
                  pl.BlockSpec((PH, CT), lambda j, i: (0, j)),
                  pl.BlockSpec((1, CT), lambda j, i: (0, j)),
                  pl.BlockSpec(memory_space=pl.ANY)],
        out_specs=[pl.BlockSpec((tm, CT), lambda j, i: (i, XBC_BLK + j)),
                   pl.BlockSpec((PH * SUB, CT), lambda j, i: (0, j)),
                   pl.BlockSpec((SUB, CT), lambda j, i: (0, j))],
        out_shape=[jax.ShapeDtypeStruct(dproj.shape, dproj.dtype), jax.ShapeDtypeStruct((PH * SUB, XBC), F32),
                   jax.ShapeDtypeStruct((SUB, XBC), F32)],
        scratch_shapes=[pltpu.VMEM((SUB, tm + 2 * PH, CT), F32), pltpu.VMEM((SUB, te, CT), F32)],
        input_output_aliases={7: 0}, compiler_params=_params(2),
    )(proj, proj, proj, dxc, dxc, conv_w, conv_b, dproj)


def _ssd_common(xbc, dt, alog, ex):
    L = CHUNK
    a = -jnp.exp(alog)
    la = dt * a
    li = lax.broadcasted_iota(jnp.int32, (L, L), 0)
    si = lax.broadcasted_iota(jnp.int32, (L, L), 1)
    lower = si <= li
    tri = jnp.where(lower, 1.0, 0.0).astype(BF16)
    cs = _xdot(tri, la)
    cst = _dotx(la, tri, (((0,), (1,)), ((), ())))
    csl = cs[L - 1:L, :]
    ecs_x = _dotx2(jnp.exp(cs)[:, 0:SSM_H], ex)
    tail_x = _dotx2(jnp.exp(csl - cs)[:, 0:SSM_H], ex)
    dt_x = _dotx2(dt[:, 0:SSM_H], ex)
    return a, la, lower, tri, cs, cst, ecs_x, tail_x, dt_x


def _ssd_fwd(xbc_c, dt, a_log, ex, nb, seq):
    t = xbc_c.shape[0]
    L = CHUNK
    nc = seq // L
    GW = SSM_R * SSM_P

    def body(x_ref, dt_ref, al_ref, ex_ref, y_ref, st_ref, state):
        c = pl.program_id(1)

        @pl.when(c == 0)
        def _():
            state[...] = jnp.zeros_like(state)

        st_ref[0] = state[...]
        xbc = x_ref[...]
        _, _, lower, _, cs, cst, ecs_x, tail_x, dt_x = _ssd_common(xbc, dt_ref[...], al_ref[...], ex_ref[...])
        xd = xbc[:, 0:D_INNER] * dt_x
        xdb = xd.astype(BF16)
        xt = (xd * tail_x).astype(BF16)
        el_x = ecs_x[L - 1:L, :]
        for g in range(SSM_G):
            bg = xbc[:, D_INNER + g * SSM_N:D_INNER + (g + 1) * SSM_N].astype(BF16)
            cg = xbc[:, D_INNER + (SSM_G + g) * SSM_N:D_INNER + (SSM_G + g + 1) * SSM_N].astype(BF16)
            cb = _dot(cg, bg, NT)
            sg = state[:, g * GW:(g + 1) * GW]
            ys = _dot(cg, sg) * ecs_x[:, g * GW:(g + 1) * GW]
            for r in range(SSM_R):
                h = g * SSM_R + r
                seg = cs[:, h:h + 1] - cst[h:h + 1, :]
                dec = jnp.exp(jnp.where(lower, seg, -1e30))
                yh = _dot(cb * dec, xdb[:, h * SSM_P:(h + 1) * SSM_P])
                y_ref[:, h * SSM_P:(h + 1) * SSM_P] = yh + ys[:, r * SSM_P:(r + 1) * SSM_P]
            state[:, g * GW:(g + 1) * GW] = sg * el_x[:, g * GW:(g + 1) * GW] + _dot(bg, xt[:, g * GW:(g + 1) * GW], TN)

    return _pallas(
        body, name="ssd_fwd", grid=(nb, nc),
        in_specs=[pl.BlockSpec((L, XBC), lambda b, c: (b * nc + c, 0)),
                  pl.BlockSpec((L, LANE), lambda b, c: (b * nc + c, 0)),
                  pl.BlockSpec((1, LANE), lambda b, c: (0, 0)),
                  pl.BlockSpec((SSM_H, D_INNER), lambda b, c: (0, 0))],
        out_specs=[pl.BlockSpec((L, D_INNER), lambda b, c: (b * nc + c, 0)),
                   pl.BlockSpec((1, SSM_N, D_INNER), lambda b, c: (b * nc + c, 0, 0))],
        out_shape=[jax.ShapeDtypeStruct((t, D_INNER), F32),
                   jax.ShapeDtypeStruct((nb * nc, SSM_N, D_INNER), F32)],
        scratch_shapes=[pltpu.VMEM((SSM_N, D_INNER), F32)], compiler_params=_params(2),
    )(xbc_c, dt, a_log, ex)


def _ssd_bwd(xbc_c, dt, a_log, ex, ext, states, dy, d_x, nb, seq):
    t = xbc_c.shape[0]
    L = CHUNK
    nc = seq // L
    GW = SSM_R * SSM_P

    def body(x_ref, dt_ref, al_ref, ex_ref, ext_ref, st_ref, dy_ref, sk_ref, dx_ref, ddt_ref, da_ref,
             dstate, dxd, yd, lastv):
        b = pl.program_id(0)
        c = pl.program_id(1)

        @pl.when(c == 0)
        def _():
            dstate[...] = jnp.zeros_like(dstate)

        @pl.when(jnp.logical_and(b == 0, c == 0))
        def _():
            da_ref[...] = jnp.zeros_like(da_ref)

        xbc = x_ref[...]
        dtv = dt_ref[...]
        ex_t = ext_ref[...]
        a, la, lower, tri, cs, cst, ecs_x, tail_x, dt_x = _ssd_common(xbc, dtv, al_ref[...], ex_ref[...])
        xs = xbc[:, 0:D_INNER]
        xd = xs * dt_x
        xdb = xd.astype(BF16)
        dyv = dy_ref[...]
        dyb = dyv.astype(BF16)
        dys = dyv * ecs_x
        xt = xd * tail_x
        el_x = ecs_x[L - 1:L, :]
        lane = lax.broadcasted_iota(jnp.int32, (L, LANE), 1)
        sub = lax.broadcasted_iota(jnp.int32, (LANE, L), 0)
        row_part = jnp.zeros((L, LANE), F32)
        col_part = jnp.zeros((LANE, L), F32)
        for g in range(SSM_G):
            gs = slice(g * GW, (g + 1) * GW)
            bcol = slice(D_INNER + g * SSM_N, D_INNER + (g + 1) * SSM_N)
            ccol = slice(D_INNER + (SSM_G + g) * SSM_N, D_INNER + (SSM_G + g + 1) * SSM_N)
            bg = xbc[:, bcol].astype(BF16)
            cg = xbc[:, ccol].astype(BF16)
            cb = _dot(cg, bg, NT)
            sg = st_ref[0, :, gs]
            dsg = dstate[:, gs]
            dc = _dot(dys[:, gs], sg, NT)
            db = _dot(xt[:, gs], dsg, NT)
            dx_state = tail_x[:, gs] * _dot(bg, dsg)
            tail_part = xd[:, gs] * dx_state
            yd[:, gs] = dys[:, gs] * _dot(cg, sg) - tail_part
            last = jnp.sum(tail_part, axis=0, keepdims=True) + el_x[:, gs] * jnp.sum(dsg * sg, axis=0, keepdims=True)
            lastv[:, gs] = jnp.broadcast_to(last, (8, GW))
            dcb = jnp.zeros((L, L), F32)
            for r in range(SSM_R):
                h = g * SSM_R + r
                hs = slice(h * SSM_P, (h + 1) * SSM_P)
                seg = cs[:, h:h + 1] - cst[h:h + 1, :]
                dec = jnp.exp(jnp.where(lower, seg, -1e30))
                m = cb * dec
                dm = _dot(dyb[:, hs], xdb[:, hs], NT)
                dcb = dcb + dm * dec
                e = dm * m
                row_part = row_part + jnp.where(lane == h, jnp.sum(e, axis=1, keepdims=True), 0.0)
                col_part = col_part + jnp.where(sub == h, jnp.sum(e, axis=0, keepdims=True), 0.0)
                dxd[:, hs] = _dot(m, dyb[:, hs], TN) + dx_state[:, r * SSM_P:(r + 1) * SSM_P]
            dx_ref[:, bcol] = db + _dot(dcb, cg, TN)
            dx_ref[:, ccol] = dc + _dot(dcb, bg)
            dstate[:, gs] = dsg * el_x[:, gs] + _dot(cg, dys[:, gs], TN)
        dxv = dxd[...]
        dx_ref[:, 0:D_INNER] = dxv * dt_x + dyv * sk_ref[...]
        ddt_x = _dotx(dxv * xs, ex_t)
        yst = _dotx(yd[...], ex_t)
        lst = _dotx(lastv[...], ex_t)[0:1, :]
        rows = lax.broadcasted_iota(jnp.int32, (L, LANE), 0)
        dcs = row_part - col_part.T + yst + jnp.where(rows == L - 1, lst, 0.0)
        li = lax.broadcasted_iota(jnp.int32, (L, L), 0)
        si = lax.broadcasted_iota(jnp.int32, (L, L), 1)
        upper = jnp.where(si >= li, 1.0, 0.0).astype(BF16)
        dla = _xdot(upper, dcs)
        ddt_ref[...] = dla * a + ddt_x
        da_ref[...] += jnp.sum(dla * dtv, axis=0, keepdims=True)

    def row(w):
        return pl.BlockSpec((L, w), lambda b, c: (b * nc + nc - 1 - c, 0))

    return _pallas(
        body, name="ssd_bwd", grid=(nb, nc),
        in_specs=[row(XBC), row(LANE), pl.BlockSpec((1, LANE), lambda b, c: (0, 0)),
                  pl.BlockSpec((SSM_H, D_INNER), lambda b, c: (0, 0)),
                  pl.BlockSpec((D_INNER, LANE), lambda b, c: (0, 0)),
                  pl.BlockSpec((1, SSM_N, D_INNER), lambda b, c: (b * nc + nc - 1 - c, 0, 0)),
                  row(D_INNER), pl.BlockSpec((1, D_INNER), lambda b, c: (0, 0))],
        out_specs=[row(XBC), row(LANE), pl.BlockSpec((1, LANE), lambda b, c: (0, 0))],
        out_shape=[jax.ShapeDtypeStruct((t, XBC), F32), jax.ShapeDtypeStruct((t, LANE), F32),
                   jax.ShapeDtypeStruct((1, LANE), F32)],
        scratch_shapes=[pltpu.VMEM((SSM_N, D_INNER), F32), pltpu.VMEM((L, D_INNER), F32),
                        pltpu.VMEM((L, D_INNER), F32), pltpu.VMEM((8, D_INNER), F32)],
        compiler_params=_params(2),
    )(xbc_c, dt, a_log, ex, ext, states, dy, d_x)


def _group_rms(y2):
    gw = D_INNER // SSM_G
    parts = []
    for g in range(SSM_G):
        v = y2[:, g * gw:(g + 1) * gw]
        r = lax.rsqrt(jnp.mean(v * v, axis=1, keepdims=True) + EPS)
        parts.append(jnp.broadcast_to(r, v.shape))
    return jnp.concatenate(parts, axis=1)


def _gate_fwd(y, xbc_c, proj, d_x, gn_w, *, tm=256):
    t = y.shape[0]

    def body(y_ref, x_ref, z_ref, d_ref, w_ref, o_ref):
        y1 = y_ref[...] + d_ref[...] * x_ref[...]
        zv = _f32(z_ref[...])
        y2 = y1 * zv * _sigmoid(zv)
        o_ref[...] = (y2 * _group_rms(y2) * w_ref[...]).astype(BF16)

    row = pl.BlockSpec((tm, D_INNER), lambda i: (i, 0))
    vec = pl.BlockSpec((1, D_INNER), lambda i: (0, 0))
    return _pallas(
        body, name="gate_fwd", grid=(t // tm,), in_specs=[row, row, row, vec, vec], out_specs=row,
        out_shape=jax.ShapeDtypeStruct((t, D_INNER), BF16), compiler_params=_params(1),
    )(y, xbc_c, proj, d_x, gn_w)


def _gate_bwd(dyg, y, xbc_c, proj, d_x, gn_w, *, tm=256):
    t = y.shape[0]
    gw = D_INNER // SSM_G

    def body(dg_ref, y_ref, x_ref, z_ref, d_ref, w_ref, dy_ref, dz_ref, dw_ref, dd_ref):
        i = pl.program_id(0)
        xv = x_ref[...]
        dxv = d_ref[...]
        y1 = y_ref[...] + dxv * xv
        zv = _f32(z_ref[...])
        sz = _sigmoid(zv)
        y2 = y1 * zv * sz
        rr = _group_rms(y2)
        xh = y2 * rr
        dg = _f32(dg_ref[...])
        gq = dg * w_ref[...]
        prod = gq * xh
        means = []
        for g in range(SSM_G):
            mg = jnp.mean(prod[:, g * gw:(g + 1) * gw], axis=1, keepdims=True)
            means.append(jnp.broadcast_to(mg, (tm, gw)))
        dy2 = rr * (gq - xh * jnp.concatenate(means, axis=1))
        dy1 = dy2 * zv * sz
        dy_ref[...] = dy1
        dz_ref[...] = (dy2 * y1 * _dsilu(zv, sz)).astype(BF16)

        @pl.when(i == 0)
        def _():
            dw_ref[...] = jnp.zeros_like(dw_ref)
            dd_ref[...] = jnp.zeros_like(dd_ref)

        dw_ref[...] += jnp.sum(dg * xh, axis=0, keepdims=True)
        dd_ref[...] += jnp.sum(dy1 * xv, axis=0, keepdims=True)

    row = pl.BlockSpec((tm, D_INNER), lambda i: (i, 0))
    vec = pl.BlockSpec((1, D_INNER), lambda i: (0, 0))
    return _pallas(
        body, name="gate_bwd", grid=(t // tm,), in_specs=[row, row, row, row, vec, vec],
        out_specs=[row, row, vec, vec],
        out_shape=[jax.ShapeDtypeStruct((t, D_INNER), F32),
                   jax.ShapeDtypeStruct((t, IN_ODD_PAD), BF16), jax.ShapeDtypeStruct((1, D_INNER), F32),
                   jax.ShapeDtypeStruct((1, D_INNER), F32)],
        compiler_params=_params(1),
    )(dyg, y, xbc_c, proj, d_x, gn_w)


ANY = pl.BlockSpec(memory_space=pl.ANY)


def _remote(src, dst, sems, k, to):
    send_sems, recv_sems = sems
    return pltpu.make_async_remote_copy(src_ref=src, dst_ref=dst, send_sem=send_sems.at[k], recv_sem=recv_sems.at[k],
                                        device_id=to, device_id_type=MESH)


NCHIP = 4


def _gathered_shape(kind, shard):
    r, n = shard.shape
    shape = {"cols": (r, NCHIP * n), "slab": (NCHIP, r, n), "rows": (NCHIP * r, n)}[kind]
    return jax.ShapeDtypeStruct(shape, shard.dtype)


def _gather_plan(kinds, shards, outs, sems, small=None):
    ici_s, ici_r, d2d_s, d2d_r = sems
    nw = len(shards)
    per = nw + (small is not None)

    def place():
        x, y, c = lax.axis_index("x"), lax.axis_index("y"), lax.axis_index("c")
        return 2 * x + y, c, (x, y, 1 - c), [(1 - x, y), (x, 1 - y), (1 - x, 1 - y)]

    def region(j, chip, half):
        r, n = shards[j].shape
        h = r // 2
        if kinds[j] == "cols":
            return outs[j].at[pl.ds(half * h, h), pl.ds(pl.multiple_of(chip * n, LANE), n)]
        if kinds[j] == "slab":
            return outs[j].at[chip, pl.ds(half * h, h), :]
        return outs[j].at[pl.ds(chip * r + half * h, h), :]

    def my_sends(me, c, peers):
        cps = []
        for k, (px, py) in enumerate(peers):
            for j in range(nw):
                h = shards[j].shape[0] // 2
                cps.append(_remote(shards[j].at[pl.ds(c * h, h), :], region(j, me, c), (ici_s, ici_r), per * k + j, (px, py, c)))
            if small is not None:
                cps.append(_remote(small[0], small[1].at[me], (ici_s, ici_r), per * k + nw, (px, py, c)))
        return cps

    def start():
        me, c, _, peers = place()
        for cp in my_sends(me, c, peers):
            cp.start()

    def finish():
        me, c, sib, peers = place()
        fwds = []
        for k, (px, py) in enumerate(peers):
            q = 2 * px + py
            for j in range(nw):
                d = region(j, q, c)
                _remote(d, d, (ici_s, ici_r), per * k + j, (px, py, c)).wait_recv()
                fwds.append(_remote(d, d, (d2d_s, d2d_r), nw * k + j, sib))
                fwds[-1].start()
            if small is not None:
                _remote(small[0], small[1].at[q], (ici_s, ici_r), per * k + nw, (px, py, c)).wait_recv()
        for k, (px, py) in enumerate(peers):
            for j in range(nw):
                d = region(j, 2 * px + py, 1 - c)
                _remote(d, d, (d2d_s, d2d_r), nw * k + j, sib).wait_recv()
        for cp in my_sends(me, c, peers) + fwds:
            cp.wait_send()

    return start, finish


def _gather_sems(nw, with_small):
    n_ici = 3 * (nw + with_small)
    return [pltpu.SemaphoreType.DMA((n_ici,)), pltpu.SemaphoreType.DMA((n_ici,)),
            pltpu.SemaphoreType.DMA((3 * nw,)), pltpu.SemaphoreType.DMA((3 * nw,))]


def _gather_shards(kinds, shards, small):
    nw = len(shards)

    def body(*refs):
        ins, sm, outs, osm, sems = refs[:nw], refs[nw], refs[nw + 1:2 * nw + 1], refs[2 * nw + 1], refs[2 * nw + 2:]
        start, finish = _gather_plan(kinds, ins, outs, sems, small=(sm, osm))
        start()
        finish()

    return _pallas(
        body, name="gather_shards", in_specs=[ANY] * (nw + 1), out_specs=[ANY] * (nw + 1),
        out_shape=[_gathered_shape(kd, s) for kd, s in zip(kinds, shards)]
        + [jax.ShapeDtypeStruct((NCHIP,) + small.shape, small.dtype)],
        scratch_shapes=_gather_sems(nw, 1),
    )(*shards, small)


def _place_cols(full, shard, *, name, tr=256):
    r, n = shard.shape

    def body(p_ref, full_any, s_ref, o_ref):
        o_ref[...] = s_ref[...]

    return _pallas(
        body, name=name,
        grid_spec=pltpu.PrefetchScalarGridSpec(
            num_scalar_prefetch=1, grid=(r // tr,),
            in_specs=[pl.BlockSpec(memory_space=pl.ANY), pl.BlockSpec((tr, n), lambda i, p: (i, 0))],
            out_specs=pl.BlockSpec((tr, n), lambda i, p: (i, p[0]))),
        out_shape=jax.ShapeDtypeStruct(full.shape, full.dtype), input_output_aliases={1: 0},
        compiler_params=_params(1),
    )(_chip_index(), full, shard)


def _matmul_with_gather(a, b, kinds, shards, *, out_dtype, bm, bn, name):
    (m, k), n = a.shape, b.shape[1]
    nw = len(shards)
    nj, ni = n // bn, m // bm

    def body(*refs):
        a_ref, b_ref, ins, o_ref = refs[0], refs[1], refs[2:2 + nw], refs[2 + nw]
        outs, sems = refs[3 + nw:3 + 2 * nw], refs[3 + 2 * nw:]
        start, finish = _gather_plan(kinds, ins, outs, sems)
        j, i = pl.program_id(0), pl.program_id(1)

        @pl.when(jnp.logical_and(j == 0, i == 0))
        def _():
            start()

        o_ref[...] = _dot(a_ref[...], b_ref[...]).astype(out_dtype)

        @pl.when(jnp.logical_and(j == nj - 1, i == ni - 1))
        def _():
            finish()

    return _pallas(
        body, name=name, grid=(nj, ni),
        in_specs=[pl.BlockSpec((bm, k), lambda j, i: (i, 0)), pl.BlockSpec((k, bn), lambda j, i: (0, j))] + [ANY] * nw,
        out_specs=[pl.BlockSpec((bm, bn), lambda j, i: (i, j))] + [ANY] * nw,
        out_shape=[jax.ShapeDtypeStruct((m, n), out_dtype)] + [_gathered_shape(kd, s) for kd, s in zip(kinds, shards)],
        scratch_shapes=_gather_sems(nw, 0), compiler_params=_params(2),
    )(a, b, *shards)


def _other_half(a, c):
    axis = a.ndim - 2
    h = a.shape[axis] // 2
    rows = pl.ds(pl.multiple_of((1 - c) * h, 8), h)
    return a.at[rows, :] if a.ndim == 2 else a.at[:, rows, :]


def _half_shape(a):
    axis = a.ndim - 2
    return jax.ShapeDtypeStruct(a.shape[:axis] + (a.shape[axis] // 2,) + a.shape[axis + 1:], a.dtype)


def _pair_swap(bigs, *, name):
    nb = len(bigs)

    def body(*refs):
        ins, outs, send_sems, recv_sems = refs[:nb], refs[nb:2 * nb], refs[2 * nb], refs[2 * nb + 1]
        x, y, c = lax.axis_index("x"), lax.axis_index("y"), lax.axis_index("c")
        pair = [_remote(_other_half(a, c), q, (send_sems, recv_sems), j, (x, y, 1 - c))
                for j, (a, q) in enumerate(zip(ins, outs))]
        for cp in pair:
            cp.start()
        for cp in pair:
            cp.wait()

    return _pallas(
        body, name=name, in_specs=[ANY] * nb, out_specs=[ANY] * nb, out_shape=[_half_shape(a) for a in bigs],
        scratch_shapes=[pltpu.SemaphoreType.DMA((nb,)), pltpu.SemaphoreType.DMA((nb,))],
    )(*bigs)


def _pair_exchange(bigs, gsmall, grep):
    nb = len(bigs)

    def body(*refs):
        ins, sm, rp = refs[:nb], refs[nb], refs[nb + 1]
        outs, osm, orp = refs[nb + 2:2 * nb + 2], refs[2 * nb + 2], refs[2 * nb + 3]
        pair_s, pair_r, send_sems, recv_sems, local_sems = refs[2 * nb + 4:]
        x, y, c = lax.axis_index("x"), lax.axis_index("y"), lax.axis_index("c")
        me = 4 * x + 2 * y + c
        chip = 2 * x + y
        sib = (x, y, 1 - c)
        pair = [_remote(_other_half(a, c), q, (pair_s, pair_r), j, sib) for j, (a, q) in enumerate(zip(ins, outs))]
        for cp in pair:
            cp.start()
        own = [pltpu.make_async_copy(sm.at[chip], osm.at[me], local_sems.at[0]),
               pltpu.make_async_copy(rp, orp.at[me], local_sems.at[1])]
        for cp in own:
            cp.start()
        peers = []
        for k in range(7):
            fx, fy, fc = ((k + 1) >> 2) & 1, ((k + 1) >> 1) & 1, (k + 1) & 1
            peers.append((1 - x if fx else x, 1 - y if fy else y, 1 - c if fc else c))
        sends = []
        for k, (px, py, pc) in enumerate(peers):
            sends.append(_remote(sm.at[2 * px + py], osm.at[me], (send_sems, recv_sems), 2 * k, (px, py, pc)))
            sends.append(_remote(rp, orp.at[me], (send_sems, recv_sems), 2 * k + 1, (px, py, pc)))
        for cp in sends:
            cp.start()
        for k, (px, py, pc) in enumerate(peers):
            slot = 4 * px + 2 * py + pc
            _remote(sm.at[chip], osm.at[slot], (send_sems, recv_sems), 2 * k, (px, py, pc)).wait_recv()
            _remote(rp, orp.at[slot], (send_sems, recv_sems), 2 * k + 1, (px, py, pc)).wait_recv()
        for cp in pair:
            cp.wait_recv()
        for cp in pair + sends:
            cp.wait_send()
        for cp in own:
            cp.wait()

    return _pallas(
        body, name="pair_exchange", in_specs=[ANY] * (nb + 2), out_specs=[ANY] * (nb + 2),
        out_shape=[_half_shape(a) for a in bigs]
        + [jax.ShapeDtypeStruct((8,) + gsmall.shape[1:], F32), jax.ShapeDtypeStruct((8,) + grep.shape, F32)],
        scratch_shapes=[pltpu.SemaphoreType.DMA((max(nb, 1),)), pltpu.SemaphoreType.DMA((max(nb, 1),)),
                        pltpu.SemaphoreType.DMA((14,)), pltpu.SemaphoreType.DMA((14,)),
                        pltpu.SemaphoreType.DMA((2,))],
    )(*bigs, gsmall, grep)


def _core_index():
    return lax.axis_index("c").astype(jnp.int32).reshape(1)


def _half_add(full, other, *, axis, block, name):
    nd = full.ndim
    nblk = other.shape[axis] // block[axis]
    grid = tuple(other.shape[d] // block[d] for d in range(nd))

    def body(c_ref, f_ref, o_ref, out_ref):
        out_ref[...] = (f_ref[...] + o_ref[...]).astype(BF16)

    def full_map(*idx):
        ids, c_ref = list(idx[:nd]), idx[nd]
        ids[axis] = ids[axis] + c_ref[0] * nblk
        return tuple(ids)

    def plain_map(*idx):
        return tuple(idx[:nd])

    return _pallas(
        body, name=name,
        grid_spec=pltpu.PrefetchScalarGridSpec(
            num_scalar_prefetch=1, grid=grid,
            in_specs=[pl.BlockSpec(block, full_map), pl.BlockSpec(block, plain_map)],
            out_specs=pl.BlockSpec(block, plain_map)),
        out_shape=jax.ShapeDtypeStruct(other.shape, BF16), compiler_params=_params(nd),
    )(_core_index(), full, other)


NPEER = 3


def _landing_shape(kind, src):
    if kind == "pair":
        return _half_shape(src)
    if kind == "cols":
        return jax.ShapeDtypeStruct((NPEER, src.shape[0], src.shape[1] // NCHIP), src.dtype)
    return jax.ShapeDtypeStruct((NPEER,) + src.shape[1:], src.dtype)


def _chip_plan(kinds, srcs, lands, sems):
    nw = len(srcs)
    chipwise = [j for j in range(nw) if kinds[j] != "pair"]
    pairwise = [j for j in range(nw) if kinds[j] == "pair"]

    def place():
        x, y, c = lax.axis_index("x"), lax.axis_index("y"), lax.axis_index("c")
        return 2 * x + y, c, [(1 - x, y), (x, 1 - y), (1 - x, 1 - y)]

    def pair_copies(c):
        sib = (lax.axis_index("x"), lax.axis_index("y"), 1 - c)
        return [_remote(_other_half(srcs[j], c), lands[j], sems, j, sib) for j in pairwise]

    def piece(j, chip):
        if kinds[j] == "cols":
            n = srcs[j].shape[1] // NCHIP
            return srcs[j].at[:, pl.ds(pl.multiple_of(chip * n, LANE), n)]
        return srcs[j].at[chip]

    def my_sends(c, peers):
        return pair_copies(c) + [_remote(piece(j, 2 * px + py), lands[j].at[k], sems, nw * k + j, (px, py, c))
                                 for k, (px, py) in enumerate(peers) for j in chipwise]

    def start():
        _, c, peers = place()
        for cp in my_sends(c, peers):
            cp.start()

    def finish():
        me, c, peers = place()
        for cp in pair_copies(c):
            cp.wait_recv()
        for k, (px, py) in enumerate(peers):
            for j in chipwise:
                _remote(piece(j, me), lands[j].at[k], sems, nw * k + j, (px, py, c)).wait_recv()
        for cp in my_sends(c, peers):
            cp.wait_send()

    return start, finish


def _chip_sems(nw):
    return [pltpu.SemaphoreType.DMA((NPEER * nw,)), pltpu.SemaphoreType.DMA((NPEER * nw,))]


def _chip_index():
    return (2 * lax.axis_index("x") + lax.axis_index("y")).astype(jnp.int32).reshape(1)


def _chip_sum(own, slots, *, own_block, own_map, block, name):
    npeer = slots.shape[0]
    shape = slots.shape[1:]
    grid = (shape[0] // block[0], shape[1] // block[1])

    def body(p_ref, own_ref, s_ref, o_ref):
        acc = own_ref[...].reshape(block).astype(F32)
        for q in range(npeer):
            acc = acc + s_ref[q].astype(F32)
        o_ref[...] = acc

    return _pallas(
        body, name=name,
        grid_spec=pltpu.PrefetchScalarGridSpec(
            num_scalar_prefetch=1, grid=grid,
            in_specs=[pl.BlockSpec(own_block, own_map),
                      pl.BlockSpec((npeer,) + block, lambda i, j, p: (0, i, j))],
            out_specs=pl.BlockSpec(block, lambda i, j, p: (i, j))),
        out_shape=jax.ShapeDtypeStruct(shape, F32), compiler_params=_params(2),
    )(_chip_index(), own, slots)


def _pair_share(r_in0, r_in1, r_out0, r_out1):
    def body(a0, a1, b0, b1, g0, g1, h0, h1, send_sems, recv_sems):
        x, y, c = lax.axis_index("x"), lax.axis_index("y"), lax.axis_index("c")
        sib = (x, y, 1 - c)
        sends = [_remote(s, d, (send_sems, recv_sems), j, sib)
                 for j, (s, d) in enumerate(zip([a0, a1, b0, b1], [g0, g1, h0, h1]))]
        for cp in sends:
            cp.start()
        for cp in sends:
            cp.wait()

    return _pallas(
        body, name="pair_share", in_specs=[ANY] * 4, out_specs=[ANY] * 4,
        out_shape=[jax.ShapeDtypeStruct(r.shape, F32) for r in (r_in0, r_in1, r_out0, r_out1)],
        scratch_shapes=[pltpu.SemaphoreType.DMA((4,)), pltpu.SemaphoreType.DMA((4,))],
    )(r_in0, r_in1, r_out0, r_out1)


def _adam_math(g, w, m, v):
    c1 = 1.0 - ADAM_B1 ** ADAM_STEP
    c2 = 1.0 - ADAM_B2 ** ADAM_STEP
    m2 = ADAM_B1 * m + (1.0 - ADAM_B1) * g
    v2 = ADAM_B2 * v + (1.0 - ADAM_B2) * (g * g)
    delta = -ADAM_LR * ((m2 / c1) / (jnp.sqrt(v2 / c2) + ADAM_EPS) + ADAM_WD * w)
    return delta, m2, v2


def _adamw_nat(g_mine, g_sib, w, m, v, *, name, tr):
    rows, cw = w.shape
    nt = g_mine.shape[0] // tr

    def body(c_ref, gm_ref, gs_ref, w_ref, m_ref, v_ref, go_ref, d_ref, nm_ref, nv_ref):
        mine = pl.program_id(0) // nt == c_ref[0]
        gv = jnp.where(mine, gm_ref[...], gs_ref[...])[:, 0:cw]
        delta, m2, v2 = _adam_math(gv, w_ref[...], m_ref[...], v_ref[...])
        go_ref[...] = gv
        d_ref[...] = delta
        nm_ref[...] = m2
        nv_ref[...] = v2

    def mine_map(i, c_ref):
        return (jnp.where(i // nt == c_ref[0], i % nt, 0), 0)

    def sib_map(i, c_ref):
        return (jnp.where(i // nt == c_ref[0], 0, i % nt), 0)

    row = pl.BlockSpec((tr, cw), lambda i, c_ref: (i, 0))
    gspec = (tr, g_mine.shape[1])
    out = jax.ShapeDtypeStruct((rows, cw), F32)
    return _pallas(
        body, name=name,
        grid_spec=pltpu.PrefetchScalarGridSpec(
            num_scalar_prefetch=1, grid=(rows // tr,),
            in_specs=[pl.BlockSpec(gspec, mine_map), pl.BlockSpec(gspec, sib_map), row, row, row],
            out_specs=[row, row, row, row]),
        out_shape=[out, out, out, out], compiler_params=_params(1),
    )(_core_index(), g_mine, g_sib, w, m, v)


def _adamw(slots, w, m, v, *, name, tr):
    nd, rows, _ = slots.shape
    c1 = 1.0 - ADAM_B1 ** ADAM_STEP
    c2 = 1.0 - ADAM_B2 ** ADAM_STEP

    def body(s_ref, w_ref, m_ref, v_ref, g_ref, d_ref, nm_ref, nv_ref):
        g = s_ref[0]
        for d in range(1, nd):
            g = g + s_ref[d]
        m2 = ADAM_B1 * m_ref[...] + (1.0 - ADAM_B1) * g
        v2 = ADAM_B2 * v_ref[...] + (1.0 - ADAM_B2) * (g * g)
        g_ref[...] = g
        nm_ref[...] = m2
        nv_ref[...] = v2
        d_ref[...] = -ADAM_LR * ((m2 / c1) / (jnp.sqrt(v2 / c2) + ADAM_EPS) + ADAM_WD * w_ref[...])

    row = pl.BlockSpec((tr, LANE), lambda i: (i, 0))
    out = jax.ShapeDtypeStruct((rows, LANE), F32)
    return _pallas(
        body, name=name, grid=(rows // tr,),
        in_specs=[pl.BlockSpec((nd, tr, LANE), lambda i: (0, i, 0)), row, row, row],
        out_specs=[row, row, row, row], out_shape=[out, out, out, out], compiler_params=_params(1),
    )(slots, w, m, v)


def _rows(a):
    return a.reshape(-1, LANE)


def _pad_rows(a, mult):
    pad = (-a.shape[0]) % mult
    return jnp.pad(a, ((0, pad), (0, 0))) if pad else a


def _pack(parts, mult):
    return _pad_rows(jnp.concatenate([_rows(p) for p in parts], axis=0), mult)


def _unpack(slab, shapes):
    out, r0 = [], 0
    for shp in shapes:
        n = 1
        for s in shp:
            n *= s
        r = n // LANE
        out.append(slab[r0:r0 + r].reshape(shp))
        r0 += r
    return out


def _pack_rep(vecs, scal):
    srow = jnp.concatenate([s.reshape(-1) for s in scal] + [jnp.zeros((LANE - 3 * SSM_H,), F32)]).reshape(1, LANE)
    return _pad_rows(jnp.concatenate([_rows(vv) for vv in vecs] + [srow], axis=0), 8)


def _unpack_rep(slab, vec_shapes, scal_shape):
    vecs, r0 = [], 0
    for shp in vec_shapes:
        vecs.append(slab[r0:r0 + 8].reshape(shp))
        r0 += 8
    srow = slab[r0]
    scal = [srow[i * SSM_H:(i + 1) * SSM_H].reshape(scal_shape) for i in range(3)]
    return vecs, scal


def kernel(x, ev_norm_w, ev_w_in, ev_dw_w, ev_dw_b, ev_ln_w, ev_ln_b, ev_w_out, od_norm_w, od_w_in, od_conv_w, od_conv_b, od_dt_bias, od_a_log, od_d, od_gnorm_w, od_w_out, final_norm_w, loss_target, m_ev_norm_w, m_ev_w_in, m_ev_dw_w, m_ev_dw_b, m_ev_ln_w, m_ev_ln_b, m_ev_w_out, m_od_norm_w, m_od_w_in, m_od_conv_w, m_od_conv_b, m_od_dt_bias, m_od_a_log, m_od_d, m_od_gnorm_w, m_od_w_out, m_final_norm_w, v_ev_norm_w, v_ev_w_in, v_ev_dw_w, v_ev_dw_b, v_ev_ln_w, v_ev_ln_b, v_ev_w_out, v_od_norm_w, v_od_w_in, v_od_conv_w, v_od_conv_b, v_od_dt_bias, v_od_a_log, v_od_d, v_od_gnorm_w, v_od_w_out, v_final_norm_w):
    nb, seq, d = x.shape
    t = nb * seq
    nchip = 4
    xf = x.reshape(t, d)
    tgt = loss_target.reshape(t, d)

    big_w = [ev_w_in[0], od_w_in[0], ev_w_out[0], od_w_out[0]]
    small_w = [ev_dw_w[0], od_norm_w[0], od_conv_w[0], od_conv_b[0], od_gnorm_w[0]]
    small_shapes = [a.shape for a in small_w]
    big_b = [a.astype(BF16) for a in big_w]
    small_slab = _pack(small_w, 8)
    w_in0, w_out0, gath_small = _gather_shards(("cols", "rows"), [big_b[0], big_b[2]], small_slab)
    chip = 2 * lax.axis_index("x") + lax.axis_index("y")
    w_in0 = _place_cols(w_in0, big_b[0], name="place_w_in0")
    w_out0 = lax.dynamic_update_slice(w_out0, big_b[2], (chip * big_b[2].shape[0], 0))
    gath_small = lax.dynamic_update_slice(gath_small, small_slab[None], (chip, 0, 0))
    per_chip = [_unpack(gath_small[p], small_shapes) for p in range(nchip)]

    def cat(idx, axis):
        return jnp.concatenate([per_chip[p][idx] for p in range(nchip)], axis=axis)

    dw_w = jnp.pad(cat(0, 1), ((0, HALO - CONF_K), (0, 0)))
    dw_w8 = jnp.repeat(dw_w, SUB, axis=0)
    n1_w = cat(1, 0).reshape(1, d)
    conv_w = jnp.pad(cat(2, 1), ((0, PH - SSM_K), (0, 0)))
    conv_b = cat(3, 0).reshape(1, XBC)
    gn_w = cat(4, 0).reshape(1, D_INNER)

    def lanes(a):
        return jnp.pad(a.reshape(1, -1), ((0, 0), (0, LANE - a.size)))

    dt_bias, a_log = lanes(od_dt_bias), lanes(od_a_log)
    d_x = jnp.repeat(od_d.reshape(-1), SSM_P).reshape(1, D_INNER)
    hid = lax.broadcasted_iota(jnp.int32, (SSM_H, D_INNER), 1) // SSM_P
    ex = (hid == lax.broadcasted_iota(jnp.int32, (SSM_H, D_INNER), 0)).astype(BF16)
    ex_t = jnp.pad(ex.T, ((0, 0), (0, LANE - SSM_H)))
    fn_w = final_norm_w.reshape(1, d)

    n0 = _rms_fwd(xf, ev_norm_w, name="rms_fwd0")
    proj0, w_in1g, w_out1 = _matmul_with_gather(n0, w_in0, ("slab", "rows"), [big_b[1], big_b[3]],
                                                out_dtype=BF16, bm=512, bn=1024, name="in_proj0")
    w_in1g = lax.dynamic_update_slice(w_in1g, big_b[1][None], (chip, 0, 0))
    w_out1 = lax.dynamic_update_slice(w_out1, big_b[3], (chip * big_b[3].shape[0], 0))
    w_in1 = jnp.pad(jnp.concatenate([w_in1g[p] for p in range(nchip)], axis=1),
                    ((0, 0), (0, IN_ODD_PAD - IN_ODD)))
    y_conv, u2 = _conf_fwd(proj0, dw_w8, ev_dw_b, ev_ln_w, ev_ln_b, seq)
    o_att, y_att = _sba_fwd(proj0, nb, seq)
    ycat0 = jnp.concatenate([y_conv, y_att], axis=1)
    h1 = _matmul(ycat0, w_out0, mode="nn", out_dtype=F32, bm=512, bn=d, bk=D_INNER, name="out_proj0", residual=xf)
    n1 = _rms_fwd(h1, n1_w, name="rms_fwd1")
    proj1 = _matmul(n1, w_in1, mode="nn", out_dtype=BF16, bm=512, bn=768, bk=d, name="in_proj1", n_major=True)
    dt_raw = _matmul(n1, w_in1[:, D_INNER + XBC:IN_ODD_PAD], mode="nn", out_dtype=F32, bm=512,
                     bn=IN_ODD_PAD - D_INNER - XBC, bk=d, name="in_proj1_dt")
    xbc_c = _xconv_fwd(proj1, conv_w, conv_b, seq)
    dt = _dt_fwd(dt_raw, dt_bias)
    y_ssd, states = _ssd_fwd(xbc_c, dt, a_log, ex, nb, seq)
    yg = _gate_fwd(y_ssd, xbc_c, proj1, d_x, gn_w)
    h2 = _matmul(yg, w_out1, mode="nn", out_dtype=F32, bm=512, bn=d, bk=D_INNER, name="out_proj1", residual=h1)
    dh2, g_fn, loss_part = _final_loss(h2, fn_w, tgt)

    dyg = _matmul(dh2, w_out1, mode="nt", out_dtype=BF16, bm=512, bn=1024, bk=d, name="d_out_proj1")
    g_w_out1 = _matmul(yg, dh2, mode="tn", out_dtype=F32, bm=1024, bn=d, bk=1024, name="dw_out_proj1")
    dy_ssd, dz, g_gn, g_dx = _gate_bwd(dyg, y_ssd, xbc_c, proj1, d_x, gn_w)
    dxbc_c, ddt, g_a = _ssd_bwd(xbc_c, dt, a_log, ex, ex_t, states, dy_ssd, d_x, nb, seq)
    dproj1, g_conv_w, g_conv_b = _xconv_bwd(proj1, dxbc_c, conv_w, conv_b, dz, seq)
    dproj1, g_dt_bias = _dt_bwd(dt_raw, dt_bias, ddt, dproj1)
    dn1 = _matmul(dproj1, w_in1, mode="nt", out_dtype=BF16, bm=1024, bn=d, bk=1792, name="d_in_proj1")
    g_w_in1 = _matmul(n1, dproj1, mode="tn", out_dtype=F32, bm=d, bn=1792, bk=1024, name="dw_in_proj1")
    dh1, g_n1 = _rms_bwd(dn1, h1, n1_w, dh2, name="rms_bwd1")

    ro = D_INNER // nchip
    n1 = IN_ODD // nchip
    n1p = -(-n1 // LANE) * LANE
    g_w_out1c = g_w_out1.reshape(nchip, ro, d)
    dycat0, q_in1, q_out1 = _matmul(dh1, w_out0, mode="nt", out_dtype=BF16, bm=512, bn=1024, bk=d, name="d_out_proj0",
                                    comm_kinds=("pair", "pair"), comm_srcs=[g_w_in1, g_w_out1c])
    g_w_out0 = _matmul(ycat0, dh1, mode="tn", out_dtype=F32, bm=1024, bn=d, bk=1024, name="dw_out_proj0")
    dq, dk, dv, dga = _sba_bwd(proj0, o_att, dycat0, nb, seq)
    s_in1n = _half_add(g_w_in1, q_in1, axis=0, block=(128, IN_ODD_PAD), name="half_add_in1")
    s_in1 = jnp.stack([jnp.pad(s_in1n[:, p * n1:(p + 1) * n1], ((0, 0), (0, n1p - n1))) for p in range(nchip)])
    s_out1 = _half_add(g_w_out1c, q_out1, axis=1, block=(1, ro // 2, d), name="half_add_out1")
    dpc, g_dw_w, g_dw_b, g_ln_w, g_ln_b, l_in1, l_out1 = _conf_bwd(
        proj0, u2, dycat0, dw_w8, ev_ln_w, ev_ln_b, seq, ("slab", "slab"), [s_in1, s_out1])
    dproj0 = jnp.concatenate([dpc, dq, dk, dv, dga], axis=1)
    g_w_in0 = _matmul(n0, dproj0, mode="tn", out_dtype=F32, bm=d, bn=1792, bk=1024, name="dw_in_proj0")
    g_w_out0c = g_w_out0.reshape(nchip, ro, d)
    q_in0, q_out0 = _pair_swap([g_w_in0, g_w_out0c], name="pair_swap_l0")
    s_in0 = _half_add(g_w_in0, q_in0, axis=0, block=(128, IN_EVEN), name="half_add_in0")
    s_out0 = _half_add(g_w_out0c, q_out0, axis=1, block=(1, ro // 2, d), name="half_add_out0")
    dn0, l_in0, l_out0 = _matmul(dproj0, w_in0, mode="nt", out_dtype=BF16, bm=1024, bn=d, bk=1792, name="d_in_proj0",
                                 comm_kinds=("cols", "slab"), comm_srcs=[s_in0, s_out0])
    grad_x, g_n0 = _rms_bwd(dn0, xf, ev_norm_w, dh1, name="rms_bwd0")

    g_dw_w = g_dw_w.reshape(HALO, SUB, CONV_W).sum(axis=1)[0:CONF_K]
    g_dw_b, g_ln_w, g_ln_b = (a.sum(axis=0, keepdims=True) for a in (g_dw_b, g_ln_w, g_ln_b))
    g_conv_w = g_conv_w.reshape(PH, SUB, XBC).sum(axis=1)[0:SSM_K]
    g_conv_b = g_conv_b.sum(axis=0, keepdims=True)
    a_neg = -jnp.exp(od_a_log.reshape(-1))
    g_a_log = g_a[0, 0:SSM_H] * a_neg
    g_d = g_dx.reshape(SSM_H, SSM_P).sum(axis=1)

    def chip_slab_small(p):
        c0, c1, c2, c3 = CONV_W // nchip, d // nchip, XBC // nchip, D_INNER // nchip
        return _pack([g_dw_w[:, p * c0:(p + 1) * c0], g_n1[0, p * c1:(p + 1) * c1],
                      g_conv_w[:, p * c2:(p + 1) * c2], g_conv_b[0, p * c2:(p + 1) * c2],
                      g_gn[0, p * c3:(p + 1) * c3]], 8)

    gsmall = jnp.stack([chip_slab_small(p) for p in range(nchip)])
    rep_vec_shapes = [ev_norm_w.shape, ev_dw_b.shape, ev_ln_w.shape, ev_ln_b.shape, final_norm_w.shape]
    grep = _pack_rep([g_n0, g_dw_b, g_ln_w, g_ln_b, g_fn], [g_dt_bias[0, 0:SSM_H], g_a_log, g_d])

    ssmall, srep = _pair_exchange([], gsmall, grep)
    r_in0 = _chip_sum(s_in0, l_in0, own_block=(128, IN_EVEN // nchip), own_map=lambda i, j, p: (i, p[0]),
                      block=(128, IN_EVEN // nchip), name="chip_sum_in0")
    r_in1 = _chip_sum(s_in1, l_in1, own_block=(1, 256, n1p), own_map=lambda i, j, p: (p[0], i, 0),
                      block=(256, n1p), name="chip_sum_in1")
    r_out0 = _chip_sum(s_out0, l_out0, own_block=(1, ro // 2, d), own_map=lambda i, j, p: (p[0], 0, 0),
                       block=(ro // 2, d), name="chip_sum_out0")
    r_out1 = _chip_sum(s_out1, l_out1, own_block=(1, ro // 2, d), own_map=lambda i, j, p: (p[0], 0, 0),
                       block=(ro // 2, d), name="chip_sum_out1")
    big_r = [r_in0, r_in1, r_out0, r_out1]
    big_q = _pair_share(*big_r)

    big_m = [m_ev_w_in[0], m_od_w_in[0], m_ev_w_out[0], m_od_w_out[0]]
    big_v = [v_ev_w_in[0], v_od_w_in[0], v_ev_w_out[0], v_od_w_out[0]]
    big_names = ["adamw_in0", "adamw_in1", "adamw_out0", "adamw_out1"]
    out_bigs = [_adamw_nat(gm, gs, w, m, v, name=nm, tr=128)
                for gm, gs, w, m, v, nm in zip(big_r, big_q, big_w, big_m, big_v, big_names)]

    def upd(slots, ws, ms, vs, packer, name, tr):
        return _adamw(slots, packer(ws), packer(ms), packer(vs), name=name, tr=tr)

    small_m = [m_ev_dw_w[0], m_od_norm_w[0], m_od_conv_w[0], m_od_conv_b[0], m_od_gnorm_w[0]]
    small_v = [v_ev_dw_w[0], v_od_norm_w[0], v_od_conv_w[0], v_od_conv_b[0], v_od_gnorm_w[0]]
    out_small = upd(ssmall, small_w, small_m, small_v, lambda a: _pack(a, 8), "adamw_small", ssmall.shape[1])

    def rep_pack(a):
        return _pack_rep(a[0:5], a[5:8])

    rep_w = [ev_norm_w, ev_dw_b, ev_ln_w, ev_ln_b, final_norm_w, od_dt_bias, od_a_log, od_d]
    rep_m = [m_ev_norm_w, m_ev_dw_b, m_ev_ln_w, m_ev_ln_b, m_final_norm_w, m_od_dt_bias, m_od_a_log, m_od_d]
    rep_v = [v_ev_norm_w, v_ev_dw_b, v_ev_ln_w, v_ev_ln_b, v_final_norm_w, v_od_dt_bias, v_od_a_log, v_od_d]
    out_rep = upd(srep, rep_w, rep_m, rep_v, rep_pack, "adamw_rep", srep.shape[1])

    results = []
    for kind in range(4):
        bw = [o[kind].reshape((1,) + o[kind].shape) for o in out_bigs]
        sw = _unpack(out_small[kind], small_shapes)
        vecs, scal = _unpack_rep(out_rep[kind], rep_vec_shapes, od_dt_bias.shape)
        results.append([
            vecs[0], bw[0], sw[0].reshape(ev_dw_w.shape), vecs[1], vecs[2], vecs[3], bw[2],
            sw[1].reshape(od_norm_w.shape), bw[1], sw[2].reshape(od_conv_w.shape), sw[3].reshape(od_conv_b.shape),
            scal[0], scal[1], scal[2], sw[4].reshape(od_gnorm_w.shape), bw[3], vecs[4]])
    loss = lax.psum(loss_part[0, 0], ("x", "y", "c"))
    return (loss, grad_x.reshape(x.shape), *results[0], *results[1], *results[2], *results[3])
```

```python
import jax
import jax.numpy as jnp
from jax import lax
from jax.experimental import pallas as pl
from jax.experimental.pallas import tpu as pltpu

F32 = jnp.float32
BF16 = jnp.bfloat16

D_MODEL = 1024
CONV_W = 1024
ATT_W = 1024
HEAD_DIM = 128
N_HEADS = 8
CONF_K = 31
IN_EVEN = 7168
D_INNER = 2048
SSM_P = 64
SSM_H = 32
SSM_G = 4
SSM_R = SSM_H // SSM_G
SSM_N = 128
SSM_K = 4
CHUNK = 128
XBC = D_INNER + 2 * SSM_G * SSM_N
IN_ODD = D_INNER + XBC + SSM_H
IN_ODD_PAD = 5376
EPS = 1e-6
QB = 128
NEG_CUT = -100.0

ADAM_LR = 0.001
ADAM_B1 = 0.9
ADAM_B2 = 0.999
ADAM_EPS = 1e-08
ADAM_WD = 0.01
ADAM_STEP = 10

LANE = 128
VMEM_LIMIT = 56 * 1024 * 1024
MESH = pl.DeviceIdType.MESH

NN = (((1,), (0,)), ((), ()))
NT = (((1,), (1,)), ((), ()))
TN = (((0,), (0,)), ((), ()))


def _pallas(body, **kw):
    return pl.pallas_call(body, **kw)


def _params(n_axes):
    return pltpu.CompilerParams(dimension_semantics=("arbitrary",) * n_axes, vmem_limit_bytes=VMEM_LIMIT)


def _dot(a, b, dims=NN):
    return lax.dot_general(a.astype(BF16), b.astype(BF16), dims, preferred_element_type=F32)


def _parts(x):
    h = x.astype(BF16)
    r = x - h.astype(F32)
    m = r.astype(BF16)
    l = (r - m.astype(F32)).astype(BF16)
    return (h, m, l)


def _dotx(x, e01, dims=NN):
    acc = None
    for p in _parts(x):
        t = lax.dot_general(p, e01, dims, preferred_element_type=F32)
        acc = t if acc is None else acc + t
    return acc


def _dotx2(x, e01, dims=NN):
    h = x.astype(BF16)
    l = (x - h.astype(F32)).astype(BF16)
    return (lax.dot_general(h, e01, dims, preferred_element_type=F32)
            + lax.dot_general(l, e01, dims, preferred_element_type=F32))


def _xdot(e01, x, dims=NN):
    acc = None
    for p in _parts(x):
        t = lax.dot_general(e01, p, dims, preferred_element_type=F32)
        acc = t if acc is None else acc + t
    return acc


def _f32(x):
    return x.astype(F32)


def _sigmoid(x):
    return 1.0 / (1.0 + jnp.exp(-x))


def _dsilu(x, s):
    return s * (1.0 + x * (1.0 - s))


def _matmul(a, b, *, mode, out_dtype, bm, bn, bk, name, residual=None, n_major=False, comm_kinds=(), comm_srcs=()):
    if mode == "nn":
        (m, k), n = a.shape, b.shape[1]
        a_blk, a_map = (bm, bk), lambda i, j, kk: (i, kk)
        b_blk, b_map = (bk, bn), lambda i, j, kk: (kk, j)
        dims = NN
    elif mode == "nt":
        (m, k), n = a.shape, b.shape[0]
        a_blk, a_map = (bm, bk), lambda i, j, kk: (i, kk)
        b_blk, b_map = (bn, bk), lambda i, j, kk: (j, kk)
        dims = NT
    else:
        (k, m), n = a.shape, b.shape[1]
        a_blk, a_map = (bk, bm), lambda i, j, kk: (kk, i)
        b_blk, b_map = (bk, bn), lambda i, j, kk: (kk, j)
        dims = TN
    bm, bn, bk = min(bm, m), min(bn, n), min(bk, k)
    if mode != "nn":
        a_blk = (bm, bk) if mode == "nt" else (bk, bm)
        b_blk = (bn, bk) if mode == "nt" else (bk, bn)
    else:
        a_blk, b_blk = (bm, bk), (bk, bn)
    assert m % bm == 0 and n % bn == 0 and k % bk == 0, (name, m, n, k)
    nk = k // bk
    has_res = residual is not None

    def order(f):
        return (lambda j, i, kk: f(i, j, kk)) if n_major else f

    nw = len(comm_srcs)
    grid = (n // bn, m // bm, nk) if n_major else (m // bm, n // bn, nk)

    def body(*refs):
        a_ref, b_ref = refs[0], refs[1]
        r_ref = refs[2] if has_res else None
        n_in = 2 + has_res + nw
        o_ref = refs[n_in]
        n_out = n_in + 1 + nw

        def finish(r):
            if has_res:
                r = r + r_ref[...]
            o_ref[...] = r.astype(out_dtype)

        def compute():
            if nk == 1:
                finish(_dot(a_ref[...], b_ref[...], dims))
                return
            acc_ref = refs[n_out]
            kk = pl.program_id(2)

            @pl.when(kk == 0)
            def _():
                acc_ref[...] = jnp.zeros_like(acc_ref)

            acc_ref[...] += _dot(a_ref[...], b_ref[...], dims)

            @pl.when(kk == nk - 1)
            def _():
                finish(acc_ref[...])

        if not nw:
            compute()
            return
        start, done = _chip_plan(comm_kinds, refs[2 + has_res:n_in], refs[n_in + 1:n_out], refs[n_out + (nk > 1):])
        ids = [pl.program_id(ax) for ax in range(3)]

        @pl.when(jnp.logical_and(jnp.logical_and(ids[0] == 0, ids[1] == 0), ids[2] == 0))
        def _():
            start()

        compute()

        @pl.when(jnp.logical_and(jnp.logical_and(ids[0] == grid[0] - 1, ids[1] == grid[1] - 1), ids[2] == grid[2] - 1))
        def _():
            done()

    in_specs = [pl.BlockSpec(a_blk, order(a_map)), pl.BlockSpec(b_blk, order(b_map))]
    args = [a, b]
    out_map = order(lambda i, j, kk: (i, j))
    if has_res:
        in_specs.append(pl.BlockSpec((bm, bn), out_map))
        args.append(residual)
    any_spec = pl.BlockSpec(memory_space=pl.ANY)
    out_specs = [pl.BlockSpec((bm, bn), out_map)] + [any_spec] * nw
    out_shape = [jax.ShapeDtypeStruct((m, n), out_dtype)] + [_landing_shape(kd, s) for kd, s in zip(comm_kinds, comm_srcs)]
    res = _pallas(
        body, name=name, grid=grid, in_specs=in_specs + [any_spec] * nw, out_specs=out_specs, out_shape=out_shape,
        scratch_shapes=([pltpu.VMEM((bm, bn), F32)] if nk > 1 else []) + (_chip_sems(nw) if nw else []),
        compiler_params=_params(3),
    )(*args, *comm_srcs)
    return res if nw else res[0]


def _rms_fwd(x, w, *, name, tm=512):
    t, d = x.shape

    def body(x_ref, w_ref, o_ref):
        xv = x_ref[...]
        r = lax.rsqrt(jnp.mean(xv * xv, axis=1, keepdims=True) + EPS)
        o_ref[...] = (xv * r * w_ref[...]).astype(BF16)

    return _pallas(
        body, name=name, grid=(t // tm,),
        in_specs=[pl.BlockSpec((tm, d), lambda i: (i, 0)), pl.BlockSpec((1, d), lambda i: (0, 0))],
        out_specs=pl.BlockSpec((tm, d), lambda i: (i, 0)),
        out_shape=jax.ShapeDtypeStruct((t, d), BF16), compiler_params=_params(1),
    )(x, w)


def _rms_bwd(dn, x, w, dres, *, name, tm=512):
    t, d = x.shape

    def body(dn_ref, x_ref, w_ref, dr_ref, dx_ref, dw_ref):
        i = pl.program_id(0)
        xv = x_ref[...]
        r = lax.rsqrt(jnp.mean(xv * xv, axis=1, keepdims=True) + EPS)
        xh = xv * r
        dy = dn_ref[...].astype(F32)
        g = dy * w_ref[...]
        dx_ref[...] = dr_ref[...] + r * (g - xh * jnp.mean(g * xh, axis=1, keepdims=True))

        @pl.when(i == 0)
        def _():
            dw_ref[...] = jnp.zeros_like(dw_ref)

        dw_ref[...] += jnp.sum(dy * xh, axis=0, keepdims=True)

    row = pl.BlockSpec((tm, d), lambda i: (i, 0))
    vec = pl.BlockSpec((1, d), lambda i: (0, 0))
    return _pallas(
        body, name=name, grid=(t // tm,), in_specs=[row, row, vec, row], out_specs=[row, vec],
        out_shape=[jax.ShapeDtypeStruct((t, d), F32), jax.ShapeDtypeStruct((1, d), F32)],
        compiler_params=_params(1),
    )(dn, x, w, dres)


def _final_loss(h, w, target, *, tm=512):
    t, d = h.shape

    def body(h_ref, w_ref, t_ref, dh_ref, dw_ref, loss_ref):
        i = pl.program_id(0)
        xv = h_ref[...]
        r = lax.rsqrt(jnp.mean(xv * xv, axis=1, keepdims=True) + EPS)
        xh = xv * r
        wv = w_ref[...]
        err = xh * wv - t_ref[...]
        dy = err * (1.0 / d)
        g = dy * wv
        dh_ref[...] = r * (g - xh * jnp.mean(g * xh, axis=1, keepdims=True))

        @pl.when(i == 0)
        def _():
            dw_ref[...] = jnp.zeros_like(dw_ref)
            loss_ref[...] = jnp.zeros_like(loss_ref)

        dw_ref[...] += jnp.sum(dy * xh, axis=0, keepdims=True)
        part = jnp.sum(jnp.sum(err * err, axis=1, keepdims=True), axis=0, keepdims=True)
        loss_ref[...] += part * (0.5 / d)

    row = pl.BlockSpec((tm, d), lambda i: (i, 0))
    vec = pl.BlockSpec((1, d), lambda i: (0, 0))
    return _pallas(
        body, name="final_loss", grid=(t // tm,), in_specs=[row, vec, row],
        out_specs=[row, vec, pl.BlockSpec((1, LANE), lambda i: (0, 0))],
        out_shape=[jax.ShapeDtypeStruct((t, d), F32), jax.ShapeDtypeStruct((1, d), F32),
                   jax.ShapeDtypeStruct((1, LANE), F32)],
        compiler_params=_params(1),
    )(h, w, target)


HALO = 32


SUB = 8
RC = 16


def _make_shifts(sh_ref, rows, shifts=tuple(range(1, SUB))):
    for s in shifts:
        sh_ref[s, 0:rows, :] = sh_ref[0, s:s + rows, :]


def _shifted(sh_ref, r0, j, rows):
    return sh_ref[j % SUB, pl.ds(r0 + (j - j % SUB), rows), :]


def _taps(w8_ref, sh_ref, r0, first, step, init):
    accs = [init] * (RC // SUB)
    for k in range(CONF_K):
        wk = w8_ref[k * SUB:(k + 1) * SUB, :]
        x = _shifted(sh_ref, r0, first + step * k, RC)
        accs = [a + wk * x[q * SUB:(q + 1) * SUB] for q, a in enumerate(accs)]
    return jnp.concatenate(accs, axis=0)


def _conf_fwd(proj, dw_w, dw_b, ln_w, ln_b, seq, *, tm=256):
    t = proj.shape[0]
    c = CONV_W
    tps = seq // tm
    hb = tm // HALO

    def body(a_ref, b_ref, g_ref, ha_ref, hb_ref, w_ref, wb_ref, lw_ref, lb_ref, y_ref, u2_ref, sh_ref):
        i = pl.program_id(0)
        keep = jnp.where(i % tps == 0, 0.0, 1.0)
        sh_ref[0, 0:HALO, :] = _f32(ha_ref[...]) * _sigmoid(_f32(hb_ref[...])) * keep
        sh_ref[0, HALO:HALO + tm, :] = _f32(a_ref[...]) * _sigmoid(_f32(b_ref[...]))
        _make_shifts(sh_ref, tm + HALO - SUB)

        def chunk(ci, carry):
            r0 = pl.multiple_of(ci * RC, RC)
            acc = _taps(w_ref, sh_ref, r0, HALO - CONF_K + 1, 1, jnp.broadcast_to(wb_ref[...], (SUB, c)))
            u2_ref[pl.ds(r0, RC), :] = acc
            mu = jnp.mean(acc, axis=1, keepdims=True)
            xc = acc - mu
            rs = lax.rsqrt(jnp.mean(xc * xc, axis=1, keepdims=True) + EPS)
            u3 = xc * rs * lw_ref[...] + lb_ref[...]
            gv = _f32(g_ref[pl.ds(r0, RC), :])
            y_ref[pl.ds(r0, RC), :] = (u3 * _sigmoid(u3) * gv * _sigmoid(gv)).astype(BF16)
            return carry

        lax.fori_loop(0, tm // RC, chunk, 0, unroll=2)

    def col(j):
        return pl.BlockSpec((tm, c), lambda i: (i, j))

    def prev(j):
        return pl.BlockSpec((HALO, c), lambda i: (jnp.maximum(i * hb - 1, 0), j))

    vec = pl.BlockSpec((1, c), lambda i: (0, 0))
    return _pallas(
        body, name="conf_fwd", grid=(t // tm,),
        in_specs=[col(0), col(1), col(2), prev(0), prev(1),
                  pl.BlockSpec((HALO * SUB, c), lambda i: (0, 0)), vec, vec, vec],
        out_specs=[pl.BlockSpec((tm, c), lambda i: (i, 0)), pl.BlockSpec((tm, c), lambda i: (i, 0))],
        out_shape=[jax.ShapeDtypeStruct((t, c), BF16), jax.ShapeDtypeStruct((t, c), F32)],
        scratch_shapes=[pltpu.VMEM((SUB, tm + HALO, c), F32)], compiler_params=_params(1),
    )(proj, proj, proj, proj, proj, dw_w, dw_b, ln_w, ln_b)


def _conf_bwd(proj, u2, dycat, dw_w, ln_w, ln_b, seq, comm_kinds, comm_srcs, *, tm=256):
    t = proj.shape[0]
    c = CONV_W
    tps = seq // tm
    hb = tm // HALO
    nhb = t // HALO
    nw = len(comm_srcs)
    nsteps = t // tm

    def fold(v):
        out = v[0:SUB]
        for q in range(1, RC // SUB):
            out = out + v[q * SUB:(q + 1) * SUB]
        return out

    def body(*refs):
        (a_ref, b_ref, g_ref, pa_ref, pb_ref, ng_ref, u2_ref, nu2_ref, dy_ref, ndy_ref,
         w_ref, lw_ref, lb_ref) = refs[:13]
        dp_ref, dww_ref, dwb_ref, dlw_ref, dlb_ref = refs[13 + nw:18 + nw]
        su_ref, sd_ref = refs[18 + 2 * nw:20 + 2 * nw]
        comm_start, comm_finish = _chip_plan(comm_kinds, refs[13:13 + nw], refs[18 + nw:18 + 2 * nw],
                                             refs[20 + 2 * nw:])
        i = pl.program_id(0)
        first = i % tps == 0
        last = i % tps == tps - 1

        @pl.when(i == 0)
        def _():
            comm_start()
            dww_ref[...] = jnp.zeros_like(dww_ref)
            dwb_ref[...] = jnp.zeros_like(dwb_ref)
            dlw_ref[...] = jnp.zeros_like(dlw_ref)
            dlb_ref[...] = jnp.zeros_like(dlb_ref)

        su_ref[0, 0:HALO, :] = _f32(pa_ref[...]) * _sigmoid(_f32(pb_ref[...])) * jnp.where(first, 0.0, 1.0)
        su_ref[0, HALO:HALO + tm, :] = _f32(a_ref[...]) * _sigmoid(_f32(b_ref[...]))
        _make_shifts(su_ref, tm + HALO - SUB)

        def ln_back(u2c, gv, dy):
            mu = jnp.mean(u2c, axis=1, keepdims=True)
            xc = u2c - mu
            rs = lax.rsqrt(jnp.mean(xc * xc, axis=1, keepdims=True) + EPS)
            xh = xc * rs
            lw = lw_ref[...]
            u3 = xh * lw + lb_ref[...]
            s3 = _sigmoid(u3)
            sg = _sigmoid(gv)
            dgc = dy * (u3 * s3) * _dsilu(gv, sg)
            du3 = dy * gv * sg * _dsilu(u3, s3)
            dxh = du3 * lw
            du2 = rs * (dxh - jnp.mean(dxh, axis=1, keepdims=True)
                        - xh * jnp.mean(dxh * xh, axis=1, keepdims=True))
            return du2, dgc, du3, xh

        def tile_chunk(ci, carry):
            r0 = pl.multiple_of(ci * RC, RC)
            rows = pl.ds(r0, RC)
            du2, dgc, du3, xh = ln_back(u2_ref[rows, :], _f32(g_ref[rows, :]), _f32(dy_ref[rows, :]))
            sd_ref[0, rows, :] = du2
            dp_ref[rows, 2 * c:3 * c] = dgc.astype(BF16)
            dwb_ref[...] += fold(du2)
            dlw_ref[...] += fold(du3 * xh)
            dlb_ref[...] += fold(du3)
            return carry

        lax.fori_loop(0, tm // RC, tile_chunk, 0, unroll=2)
        live = jnp.where(last, 0.0, 1.0)
        for ci in range(HALO // RC):
            rows = slice(ci * RC, (ci + 1) * RC)
            du2, _, _, _ = ln_back(nu2_ref[rows, :], _f32(ng_ref[rows, :]), _f32(ndy_ref[rows, :]))
            sd_ref[0, tm + ci * RC:tm + (ci + 1) * RC, :] = du2 * live
        _make_shifts(sd_ref, tm + HALO - SUB)

        def tap_chunk(ci, carry):
            r0 = pl.multiple_of(ci * RC, RC)
            rows = pl.ds(r0, RC)
            du1 = _taps(w_ref, sd_ref, r0, CONF_K - 1, -1, jnp.zeros((SUB, c), F32))
            sb = _sigmoid(_f32(b_ref[rows, :]))
            dp_ref[rows, 0:c] = (du1 * sb).astype(BF16)
            dp_ref[rows, c:2 * c] = (du1 * _f32(a_ref[rows, :]) * sb * (1.0 - sb)).astype(BF16)
            du2 = sd_ref[0, rows, :]
            for k in range(CONF_K):
                dww_ref[k * SUB:(k + 1) * SUB, :] += fold(du2 * _shifted(su_ref, r0, HALO - CONF_K + 1 + k, RC))
            return carry

        lax.fori_loop(0, tm // RC, tap_chunk, 0)

        @pl.when(i == nsteps - 1)
        def _():
            comm_finish()

    def col(j):
        return pl.BlockSpec((tm, c), lambda i: (i, j))

    def prev(j):
        return pl.BlockSpec((HALO, c), lambda i: (jnp.maximum(i * hb - 1, 0), j))

    def nxt(j):
        return pl.BlockSpec((HALO, c), lambda i: (jnp.minimum((i + 1) * hb, nhb - 1), j))

    vec = pl.BlockSpec((1, c), lambda i: (0, 0))
    acc = pl.BlockSpec((SUB, c), lambda i: (0, 0))
    any_spec = pl.BlockSpec(memory_space=pl.ANY)
    return _pallas(
        body, name="conf_bwd", grid=(nsteps,),
        in_specs=[col(0), col(1), col(2), prev(0), prev(1), nxt(2), col(0), nxt(0), col(0), nxt(0),
                  pl.BlockSpec((HALO * SUB, c), lambda i: (0, 0)), vec, vec] + [any_spec] * nw,
        out_specs=[pl.BlockSpec((tm, 3 * c), lambda i: (i, 0)),
                   pl.BlockSpec((HALO * SUB, c), lambda i: (0, 0)), acc, acc, acc] + [any_spec] * nw,
        out_shape=[jax.ShapeDtypeStruct((t, 3 * c), BF16), jax.ShapeDtypeStruct((HALO * SUB, c), F32),
                   jax.ShapeDtypeStruct((SUB, c), F32), jax.ShapeDtypeStruct((SUB, c), F32),
                   jax.ShapeDtypeStruct((SUB, c), F32)]
        + [_landing_shape(kd, s) for kd, s in zip(comm_kinds, comm_srcs)],
        scratch_shapes=[pltpu.VMEM((SUB, tm + HALO, c), F32), pltpu.VMEM((SUB, tm + HALO, c), F32)] + _chip_sems(nw),
        compiler_params=_params(1),
    )(proj, proj, proj, proj, proj, proj, u2, u2, dycat, dycat, dw_w, ln_w, ln_b, *comm_srcs)


Q_COL = 3 * CONV_W // HEAD_DIM
K_COL = Q_COL + N_HEADS
V_COL = K_COL + N_HEADS
GA_COL = V_COL + N_HEADS


SBA_TQ = 256
SBA_WK = 4 * QB


def _sb_window(qs, kw, ws, limit, t0, carry):
    tq, wk = qs.shape[0], kw.shape[0]
    z = _dot(qs, kw, NT)
    sg = ws + lax.broadcasted_iota(jnp.int32, (tq, wk), 1)
    tg = t0 + lax.broadcasted_iota(jnp.int32, (tq, wk), 0)
    mask = sg < jnp.minimum(tg, limit)
    sp = jnp.log(1.0 + jnp.exp(-jnp.abs(z)))
    ls = jnp.minimum(z, 0.0) - sp
    lk = jnp.where(mask, ls - z, 0.0)
    jj = lax.broadcasted_iota(jnp.int32, (QB, QB), 0)
    ss = lax.broadcasted_iota(jnp.int32, (QB, QB), 1)
    ustrict = jnp.where(jj > ss, 1.0, 0.0).astype(BF16)
    laters = [None] * (wk // QB)
    for ch in reversed(range(wk // QB)):
        lkc = lk[:, ch * QB:(ch + 1) * QB]
        laters[ch] = carry + _dotx2(lkc, ustrict)
        carry = carry + jnp.sum(lkc, axis=1, keepdims=True)
    w = jnp.where(mask, jnp.exp(ls + jnp.concatenate(laters, axis=1)), 0.0)
    return mask, ls, w, carry


def _sba_fwd(proj, nb, seq, *, tq=SBA_TQ, wk=SBA_WK):
    t = proj.shape[0]
    wk = min(wk, seq)
    nq = seq // tq
    scale = HEAD_DIM ** -0.5

    def body(q_ref, k_ref, v_ref, g_ref, o_ref, y_ref):
        i = pl.program_id(2)
        t0 = i * tq
        qs = (_f32(q_ref[...]) * scale).astype(BF16)

        def window(ws, limit, carry, acc):
            ws = pl.multiple_of(ws, QB)
            _, _, w, carry = _sb_window(qs, k_ref[pl.ds(ws, wk), :], ws, limit, t0, carry)
            return carry, acc + _dot(w, v_ref[pl.ds(ws, wk), :])

        ws0 = jnp.maximum(t0 + tq - wk, 0)
        carry, acc = window(ws0, seq, jnp.zeros((tq, 1), F32), jnp.zeros((tq, HEAD_DIM), F32))

        def cond(st):
            return jnp.logical_and(st[0] > 0, jnp.max(st[1]) > NEG_CUT)

        def step(st):
            c2, a2 = window(jnp.maximum(st[0] - wk, 0), st[0], st[1], st[2])
            return jnp.maximum(st[0] - wk, 0), c2, a2

        _, _, acc = lax.while_loop(cond, step, (ws0, carry, acc))
        o_ref[...] = acc
        gv = _f32(g_ref[...])
        y_ref[...] = (acc * gv * _sigmoid(gv)).astype(BF16)

    def tile(c0):
        return pl.BlockSpec((tq, HEAD_DIM), lambda b, h, i: (b * nq + i, c0 + h))

    def whole(c0):
        return pl.BlockSpec((seq, HEAD_DIM), lambda b, h, i: (b, c0 + h))

    return _pallas(
        body, name="sba_fwd", grid=(nb, N_HEADS, nq),
        in_specs=[tile(Q_COL), whole(K_COL), whole(V_COL), tile(GA_COL)],
        out_specs=[tile(0), tile(0)],
        out_shape=[jax.ShapeDtypeStruct((t, ATT_W), F32), jax.ShapeDtypeStruct((t, ATT_W), BF16)],
        compiler_params=_params(3),
    )(proj, proj, proj, proj)


def _sba_bwd(proj, o, dycat, nb, seq, *, tq=SBA_TQ, wk=SBA_WK):
    t = proj.shape[0]
    wk = min(wk, seq)
    nq = seq // tq
    nwin = -(-seq // wk) + 1
    nch = wk // QB
    scale = HEAD_DIM ** -0.5

    def body(q_ref, k_ref, v_ref, g_ref, o_ref, dy_ref, dq_ref, dko_ref, dvo_ref, dg_ref, e_ref, sp_ref,
             dk_ref, dv_ref):
        i = pl.program_id(2)
        t0 = i * tq

        @pl.when(i == 0)
        def _():
            dk_ref[...] = jnp.zeros_like(dk_ref)
            dv_ref[...] = jnp.zeros_like(dv_ref)

        qs = (_f32(q_ref[...]) * scale).astype(BF16)
        gv = _f32(g_ref[...])
        sg = _sigmoid(gv)
        dy = _f32(dy_ref[...])
        do = (dy * gv * sg).astype(BF16)
        dg_ref[...] = (dy * o_ref[...] * _dsilu(gv, sg)).astype(BF16)

        def start_of(n):
            return pl.multiple_of(jnp.maximum(t0 + tq - (n + 1) * wk, 0), QB)

        def limit_of(n):
            return jnp.where(n == 0, seq, jnp.maximum(t0 + tq - n * wk, 0))

        def near(n, carry):
            ws = start_of(n)
            _, ls, w, carry = _sb_window(qs, k_ref[pl.ds(ws, wk), :], ws, limit_of(n), t0, carry)
            e_ref[n] = w * _dot(do, v_ref[pl.ds(ws, wk), :], NT)
            sp_ref[n] = jnp.exp(ls)
            dv_ref[pl.ds(ws, wk), :] += _dot(w, do, TN)
            return carry

        carry = near(0, jnp.zeros((tq, 1), F32))

        def cond(st):
            return jnp.logical_and(start_of(st[0] - 1) > 0, jnp.max(st[1]) > NEG_CUT)

        def step(st):
            return st[0] + 1, near(st[0], st[1])

        nvis, _ = lax.while_loop(cond, step, (1, carry))

        jj = lax.broadcasted_iota(jnp.int32, (QB, QB), 0)
        ss = lax.broadcasted_iota(jnp.int32, (QB, QB), 1)
        lstrict = jnp.where(jj < ss, 1.0, 0.0).astype(BF16)

        def far(r, st):
            pre, dq = st
            n = nvis - 1 - r
            ws = start_of(n)
            e = e_ref[n]
            spn = sp_ref[n]
            gs = []
            for ch in range(nch):
                ec = e[:, ch * QB:(ch + 1) * QB]
                gs.append(pre + _dotx2(ec, lstrict))
                pre = pre + jnp.sum(ec, axis=1, keepdims=True)
            sgl = ws + lax.broadcasted_iota(jnp.int32, (tq, wk), 1)
            tgl = t0 + lax.broadcasted_iota(jnp.int32, (tq, wk), 0)
            mask = sgl < jnp.minimum(tgl, limit_of(n))
            dz = jnp.where(mask, e * (1.0 - spn) - jnp.concatenate(gs, axis=1) * spn, 0.0).astype(BF16)
            dk_ref[pl.ds(ws, wk), :] += _dot(dz, qs, TN)
            return pre, dq + _dot(dz, k_ref[pl.ds(ws, wk), :])

        _, dq = lax.fori_loop(0, nvis, far, (jnp.zeros((tq, 1), F32), jnp.zeros((tq, HEAD_DIM), F32)))
        dq_ref[...] = (dq * scale).astype(BF16)

        @pl.when(i == nq - 1)
        def _():
            dko_ref[...] = dk_ref[...].astype(BF16)
            dvo_ref[...] = dv_ref[...].astype(BF16)

    def tile(c0):
        return pl.BlockSpec((tq, HEAD_DIM), lambda b, h, i: (b * nq + i, c0 + h))

    def whole(c0):
        return pl.BlockSpec((seq, HEAD_DIM), lambda b, h, i: (b, c0 + h))

    return _pallas(
        body, name="sba_bwd", grid=(nb, N_HEADS, nq),
        in_specs=[tile(Q_COL), whole(K_COL), whole(V_COL), tile(GA_COL), tile(0),
                  tile(CONV_W // HEAD_DIM)],
        out_specs=[tile(0), whole(0), whole(0), tile(0)],
        out_shape=[jax.ShapeDtypeStruct((t, ATT_W), BF16)] * 4,
        scratch_shapes=[pltpu.VMEM((nwin, tq, wk), F32), pltpu.VMEM((nwin, tq, wk), F32),
                        pltpu.VMEM((seq, HEAD_DIM), F32), pltpu.VMEM((seq, HEAD_DIM), F32)],
        compiler_params=_params(3),
    )(proj, proj, proj, proj, o, dycat)


CT = 512
PH = 8
XRC = 32
X_SHIFTS = tuple(s for s in range(PH - SSM_K + 1, PH))
D_SHIFTS = tuple(range(1, SSM_K))
XBC_BLK = D_INNER // CT


def _softplus(x):
    return jnp.maximum(x, 0.0) + jnp.log(1.0 + jnp.exp(-jnp.abs(x)))


def _dt_fwd(proj, dt_bias, *, tm=512):
    t = proj.shape[0]

    def body(p_ref, b_ref, o_ref):
        o_ref[...] = _softplus(p_ref[...] + b_ref[...])

    return _pallas(
        body, name="dt_fwd", grid=(t // tm,),
        in_specs=[pl.BlockSpec((tm, LANE), lambda i: (i, 0)), pl.BlockSpec((1, LANE), lambda i: (0, 0))],
        out_specs=pl.BlockSpec((tm, LANE), lambda i: (i, 0)),
        out_shape=jax.ShapeDtypeStruct((t, LANE), F32), compiler_params=_params(1),
    )(proj, dt_bias)


def _dt_bwd(proj, dt_bias, ddt, dproj, *, tm=512):
    t = proj.shape[0]
    wide = IN_ODD_PAD - D_INNER - XBC

    def body(p_ref, b_ref, d_ref, dp_any, o_ref, db_ref):
        i = pl.program_id(0)
        lanes = lax.broadcasted_iota(jnp.int32, (tm, LANE), 1)
        dr = jnp.where(lanes < SSM_H, d_ref[...] * _sigmoid(p_ref[...] + b_ref[...]), 0.0)
        o_ref[:, 0:LANE] = dr.astype(BF16)
        o_ref[:, LANE:wide] = jnp.zeros((tm, wide - LANE), BF16)

        @pl.when(i == 0)
        def _():
            db_ref[...] = jnp.zeros_like(db_ref)

        db_ref[...] += jnp.sum(dr, axis=0, keepdims=True)

    vec = pl.BlockSpec((1, LANE), lambda i: (0, 0))
    row = pl.BlockSpec((tm, LANE), lambda i: (i, 0))
    return _pallas(
        body, name="dt_bwd", grid=(t // tm,),
        in_specs=[pl.BlockSpec((tm, LANE), lambda i: (i, 0)), vec, row, pl.BlockSpec(memory_space=pl.ANY)],
        out_specs=[pl.BlockSpec((tm, wide), lambda i: (i, (D_INNER + XBC) // wide)), vec],
        out_shape=[jax.ShapeDtypeStruct(dproj.shape, dproj.dtype), jax.ShapeDtypeStruct((1, LANE), F32)],
        input_output_aliases={3: 0}, compiler_params=_params(1),
    )(proj, dt_bias, ddt, dproj)


def _xconv_fwd(proj, conv_w, conv_b, seq, *, tm=512):
    t = proj.shape[0]
    tps = seq // tm
    hb = tm // PH

    def body(x_ref, h_ref, w_ref, b_ref, o_ref, sh_ref):
        i = pl.program_id(1)
        sh_ref[0, 0:PH, :] = _f32(h_ref[...]) * jnp.where(i % tps == 0, 0.0, 1.0)
        sh_ref[0, PH:PH + tm, :] = _f32(x_ref[...])
        _make_shifts(sh_ref, tm, X_SHIFTS)

        def chunk(ci, carry):
            r0 = pl.multiple_of(ci * XRC, XRC)
            acc = jnp.zeros((XRC, CT), F32) + b_ref[...]
            for k in range(SSM_K):
                acc = acc + w_ref[k:k + 1, :] * _shifted(sh_ref, r0, PH - SSM_K + 1 + k, XRC)
            o_ref[pl.ds(r0, XRC), :] = acc * _sigmoid(acc)
            return carry

        lax.fori_loop(0, tm // XRC, chunk, 0, unroll=4)

    return _pallas(
        body, name="xconv_fwd", grid=(XBC // CT, t // tm),
        in_specs=[pl.BlockSpec((tm, CT), lambda j, i: (i, XBC_BLK + j)),
                  pl.BlockSpec((PH, CT), lambda j, i: (jnp.maximum(i * hb - 1, 0), XBC_BLK + j)),
                  pl.BlockSpec((PH, CT), lambda j, i: (0, j)),
                  pl.BlockSpec((1, CT), lambda j, i: (0, j))],
        out_specs=pl.BlockSpec((tm, CT), lambda j, i: (i, j)),
        out_shape=jax.ShapeDtypeStruct((t, XBC), F32),
        scratch_shapes=[pltpu.VMEM((SUB, tm + PH, CT), F32)], compiler_params=_params(2),
    )(proj, proj, conv_w, conv_b)


def _xconv_bwd(proj, dxc, conv_w, conv_b, dproj, seq, *, tm=512):
    t = proj.shape[0]
    tps = seq // tm
    hb = tm // PH
    nhb = t // PH
    te = tm + PH

    def fold(v):
        out = v[0:SUB]
        for q in range(1, v.shape[0] // SUB):
            out = out + v[q * SUB:(q + 1) * SUB]
        return out

    def body(x_ref, p_ref, n_ref, d_ref, nd_ref, w_ref, b_ref, dp_any, dx_ref, dw_ref, db_ref, sx_ref, sd_ref):
        i = pl.program_id(1)
        first = i % tps == 0
        last = i % tps == tps - 1

        @pl.when(i == 0)
        def _():
            dw_ref[...] = jnp.zeros_like(dw_ref)
            db_ref[...] = jnp.zeros_like(db_ref)

        sx_ref[0, 0:PH, :] = _f32(p_ref[...]) * jnp.where(first, 0.0, 1.0)
        sx_ref[0, PH:PH + tm, :] = _f32(x_ref[...])
        sx_ref[0, PH + tm:PH + te, :] = _f32(n_ref[...])
        _make_shifts(sx_ref, te, X_SHIFTS)

        def dv_of(r0, rows, dy):
            acc = jnp.zeros((rows, CT), F32) + b_ref[...]
            for k in range(SSM_K):
                acc = acc + w_ref[k:k + 1, :] * _shifted(sx_ref, r0, PH - SSM_K + 1 + k, rows)
            return dy * _dsilu(acc, _sigmoid(acc))

        def dv_chunk(ci, carry):
            r0 = pl.multiple_of(ci * XRC, XRC)
            dv = dv_of(r0, XRC, d_ref[pl.ds(r0, XRC), :])
            sd_ref[0, pl.ds(r0, XRC), :] = dv
            db_ref[...] += fold(dv)
            return carry

        lax.fori_loop(0, tm // XRC, dv_chunk, 0, unroll=4)
        sd_ref[0, tm:te, :] = dv_of(tm, PH, nd_ref[...]) * jnp.where(last, 0.0, 1.0)
        _make_shifts(sd_ref, tm, D_SHIFTS)

        def tap_chunk(ci, carry):
            r0 = pl.multiple_of(ci * XRC, XRC)
            dx = jnp.zeros((XRC, CT), F32)
            for k in range(SSM_K):
                dx = dx + w_ref[k:k + 1, :] * _shifted(sd_ref, r0, SSM_K - 1 - k, XRC)
            dx_ref[pl.ds(r0, XRC), :] = dx.astype(BF16)
            dv = sd_ref[0, pl.ds(r0, XRC), :]
            for k in range(SSM_K):
                dw_ref[k * SUB:(k + 1) * SUB, :] += fold(dv * _shifted(sx_ref, r0, PH - SSM_K + 1 + k, XRC))
            return carry

        lax.fori_loop(0, tm // XRC, tap_chunk, 0, unroll=4)

    return _pallas(
        body, name="xconv_bwd", grid=(XBC // CT, t // tm),
        in_specs=[pl.BlockSpec((tm, CT), lambda j, i: (i, XBC_BLK + j)),
                  pl.BlockSpec((PH, CT), lambda j, i: (jnp.maximum(i * hb - 1, 0), XBC_BLK + j)),
                  pl.BlockSpec((PH, CT), lambda j, i: (jnp.minimum((i + 1) * hb, nhb - 1), XBC_BLK + j)),
                  pl.BlockSpec((tm, CT), lambda j, i: (i, j)),
                  pl.BlockSpec((PH, CT), lambda j, i: (jnp.minimum((i + 1) * hb, nhb - 1), j)),
                  pl.BlockSpec((PH, CT), lambda j, i: (0, j)),
                  pl.BlockSpec((1, CT), lambda j, i: (0, j)),
                  pl.BlockSpec(memory_space=pl.ANY)],
        out_specs=[pl.BlockSpec((tm, CT), lambda j, i: (i, XBC_BLK + j)),
                   pl.BlockSpec((PH * SUB, CT), lambda j, i: (0, j)),
                   pl.BlockSpec((SUB, CT), lambda j, i: (0, j))],
        out_shape=[jax.ShapeDtypeStruct(dproj.shape, dproj.dtype), jax.ShapeDtypeStruct((PH * SUB, XBC), F32),
                   jax.ShapeDtypeStruct((SUB, XBC), F32)],
        scratch_shapes=[pltpu.VMEM((SUB, tm + 2 * PH, CT), F32), pltpu.VMEM((SUB, te, CT), F32)],
        input_output_aliases={7: 0}, compiler_params=_params(2),
    )(proj, proj, proj, dxc, dxc, conv_w, conv_b, dproj)


def _ssd_common(xbc, dt, alog, ex):
    L = CHUNK
    a = -jnp.exp(alog)
    la = dt * a
    li = lax.broadcasted_iota(jnp.int32, (L, L), 0)
    si = lax.broadcasted_iota(jnp.int32, (L, L), 1)
    lower = si <= li
    tri = jnp.where(lower, 1.0, 0.0).astype(BF16)
    cs = _xdot(tri, la)
    cst = _dotx(la, tri, (((0,), (1,)), ((), ())))
    csl = cs[L - 1:L, :]
    ecs_x = _dotx2(jnp.exp(cs)[:, 0:SSM_H], ex)
    tail_x = _dotx2(jnp.exp(csl - cs)[:, 0:SSM_H], ex)
    dt_x = _dotx2(dt[:, 0:SSM_H], ex)
    return a, la, lower, tri, cs, cst, ecs_x, tail_x, dt_x


def _ssd_fwd(xbc_c, dt, a_log, ex, nb, seq):
    t = xbc_c.shape[0]
    L = CHUNK
    nc = seq // L
    GW = SSM_R * SSM_P

    def body(x_ref, dt_ref, al_ref, ex_ref, y_ref, st_ref, state):
        c = pl.program_id(1)

        @pl.when(c == 0)
        def _():
            state[...] = jnp.zeros_like(state)

        st_ref[0] = state[...]
        xbc = x_ref[...]
        _, _, lower, _, cs, cst, ecs_x, tail_x, dt_x = _ssd_common(xbc, dt_ref[...], al_ref[...], ex_ref[...])
        xd = xbc[:, 0:D_INNER] * dt_x
        xdb = xd.astype(BF16)
        xt = (xd * tail_x).astype(BF16)
        el_x = ecs_x[L - 1:L, :]
        for g in range(SSM_G):
            bg = xbc[:, D_INNER + g * SSM_N:D_INNER + (g + 1) * SSM_N].astype(BF16)
            cg = xbc[:, D_INNER + (SSM_G + g) * SSM_N:D_INNER + (SSM_G + g + 1) * SSM_N].astype(BF16)
            cb = _dot(cg, bg, NT)
            sg = state[:, g * GW:(g + 1) * GW]
            ys = _dot(cg, sg) * ecs_x[:, g * GW:(g + 1) * GW]
            for r in range(SSM_R):
                h = g * SSM_R + r
                seg = cs[:, h:h + 1] - cst[h:h + 1, :]
                dec = jnp.exp(jnp.where(lower, seg, -1e30))
                yh = _dot(cb * dec, xdb[:, h * SSM_P:(h + 1) * SSM_P])
                y_ref[:, h * SSM_P:(h + 1) * SSM_P] = yh + ys[:, r * SSM_P:(r + 1) * SSM_P]
            state[:, g * GW:(g + 1) * GW] = sg * el_x[:, g * GW:(g + 1) * GW] + _dot(bg, xt[:, g * GW:(g + 1) * GW], TN)

    return _pallas(
        body, name="ssd_fwd", grid=(nb, nc),
        in_specs=[pl.BlockSpec((L, XBC), lambda b, c: (b * nc + c, 0)),
                  pl.BlockSpec((L, LANE), lambda b, c: (b * nc + c, 0)),
                  pl.BlockSpec((1, LANE), lambda b, c: (0, 0)),
                  pl.BlockSpec((SSM_H, D_INNER), lambda b, c: (0, 0))],
        out_specs=[pl.BlockSpec((L, D_INNER), lambda b, c: (b * nc + c, 0)),
                   pl.BlockSpec((1, SSM_N, D_INNER), lambda b, c: (b * nc + c, 0, 0))],
        out_shape=[jax.ShapeDtypeStruct((t, D_INNER), F32),
                   jax.ShapeDtypeStruct((nb * nc, SSM_N, D_INNER), F32)],
        scratch_shapes=[pltpu.VMEM((SSM_N, D_INNER), F32)], compiler_params=_params(2),
    )(xbc_c, dt, a_log, ex)


def _ssd_bwd(xbc_c, dt, a_log, ex, ext, states, dy, d_x, nb, seq):
    t = xbc_c.shape[0]
    L = CHUNK
    nc = seq // L
    GW = SSM_R * SSM_P

    def body(x_ref, dt_ref, al_ref, ex_ref, ext_ref, st_ref, dy_ref, sk_ref, dx_ref, ddt_ref, da_ref,
             dstate, dxd, yd, lastv):
        b = pl.program_id(0)
        c = pl.program_id(1)

        @pl.when(c == 0)
        def _():
            dstate[...] = jnp.zeros_like(dstate)

        @pl.when(jnp.logical_and(b == 0, c == 0))
        def _():
            da_ref[...] = jnp.zeros_like(da_ref)

        xbc = x_ref[...]
        dtv = dt_ref[...]
        ex_t = ext_ref[...]
        a, la, lower, tri, cs, cst, ecs_x, tail_x, dt_x = _ssd_common(xbc, dtv, al_ref[...], ex_ref[...])
        xs = xbc[:, 0:D_INNER]
        xd = xs * dt_x
        xdb = xd.astype(BF16)
        dyv = dy_ref[...]
        dyb = dyv.astype(BF16)
        dys = dyv * ecs_x
        xt = xd * tail_x
        el_x = ecs_x[L - 1:L, :]
        lane = lax.broadcasted_iota(jnp.int32, (L, LANE), 1)
        sub = lax.broadcasted_iota(jnp.int32, (LANE, L), 0)
        row_part = jnp.zeros((L, LANE), F32)
        col_part = jnp.zeros((LANE, L), F32)
        for g in range(SSM_G):
            gs = slice(g * GW, (g + 1) * GW)
            bcol = slice(D_INNER + g * SSM_N, D_INNER + (g + 1) * SSM_N)
            ccol = slice(D_INNER + (SSM_G + g) * SSM_N, D_INNER + (SSM_G + g + 1) * SSM_N)
            bg = xbc[:, bcol].astype(BF16)
            cg = xbc[:, ccol].astype(BF16)
            cb = _dot(cg, bg, NT)
            sg = st_ref[0, :, gs]
            dsg = dstate[:, gs]
            dc = _dot(dys[:, gs], sg, NT)
            db = _dot(xt[:, gs], dsg, NT)
            dx_state = tail_x[:, gs] * _dot(bg, dsg)
            tail_part = xd[:, gs] * dx_state
            yd[:, gs] = dys[:, gs] * _dot(cg, sg) - tail_part
            last = jnp.sum(tail_part, axis=0, keepdims=True) + el_x[:, gs] * jnp.sum(dsg * sg, axis=0, keepdims=True)
            lastv[:, gs] = jnp.broadcast_to(last, (8, GW))
            dcb = jnp.zeros((L, L), F32)
            for r in range(SSM_R):
                h = g * SSM_R + r
                hs = slice(h * SSM_P, (h + 1) * SSM_P)
                seg = cs[:, h:h + 1] - cst[h:h + 1, :]
                dec = jnp.exp(jnp.where(lower, seg, -1e30))
                m = cb * dec
                dm = _dot(dyb[:, hs], xdb[:, hs], NT)
                dcb = dcb + dm * dec
                e = dm * m
                row_part = row_part + jnp.where(lane == h, jnp.sum(e, axis=1, keepdims=True), 0.0)
                col_part = col_part + jnp.where(sub == h, jnp.sum(e, axis=0, keepdims=True), 0.0)
                dxd[:, hs] = _dot(m, dyb[:, hs], TN) + dx_state[:, r * SSM_P:(r + 1) * SSM_P]
            dx_ref[:, bcol] = db + _dot(dcb, cg, TN)
            dx_ref[:, ccol] = dc + _dot(dcb, bg)
            dstate[:, gs] = dsg * el_x[:, gs] + _dot(cg, dys[:, gs], TN)
        dxv = dxd[...]
        dx_ref[:, 0:D_INNER] = dxv * dt_x + dyv * sk_ref[...]
        ddt_x = _dotx(dxv * xs, ex_t)
        yst = _dotx(yd[...], ex_t)
        lst = _dotx(lastv[...], ex_t)[0:1, :]
        rows = lax.broadcasted_iota(jnp.int32, (L, LANE), 0)
        dcs = row_part - col_part.T + yst + jnp.where(rows == L - 1, lst, 0.0)
        li = lax.broadcasted_iota(jnp.int32, (L, L), 0)
        si = lax.broadcasted_iota(jnp.int32, (L, L), 1)
        upper = jnp.where(si >= li, 1.0, 0.0).astype(BF16)
        dla = _xdot(upper, dcs)
        ddt_ref[...] = dla * a + ddt_x
        da_ref[...] += jnp.sum(dla * dtv, axis=0, keepdims=True)

    def row(w):
        return pl.BlockSpec((L, w), lambda b, c: (b * nc + nc - 1 - c, 0))

    return _pallas(
        body, name="ssd_bwd", grid=(nb, nc),
        in_specs=[row(XBC), row(LANE), pl.BlockSpec((1, LANE), lambda b, c: (0, 0)),
                  pl.BlockSpec((SSM_H, D_INNER), lambda b, c: (0, 0)),
                  pl.BlockSpec((D_INNER, LANE), lambda b, c: (0, 0)),
                  pl.BlockSpec((1, SSM_N, D_INNER), lambda b, c: (b * nc + nc - 1 - c, 0, 0)),
                  row(D_INNER), pl.BlockSpec((1, D_INNER), lambda b, c: (0, 0))],
        out_specs=[row(XBC), row(LANE), pl.BlockSpec((1, LANE), lambda b, c: (0, 0))],
        out_shape=[jax.ShapeDtypeStruct((t, XBC), F32), jax.ShapeDtypeStruct((t, LANE), F32),
                   jax.ShapeDtypeStruct((1, LANE), F32)],
        scratch_shapes=[pltpu.VMEM((SSM_N, D_INNER), F32), pltpu.VMEM((L, D_INNER), F32),
                        pltpu.VMEM((L, D_INNER), F32), pltpu.VMEM((8, D_INNER), F32)],
        compiler_params=_params(2),
    )(xbc_c, dt, a_log, ex, ext, states, dy, d_x)


def _group_rms(y2):
    gw = D_INNER // SSM_G
    parts = []
    for g in range(SSM_G):
        v = y2[:, g * gw:(g + 1) * gw]
        r = lax.rsqrt(jnp.mean(v * v, axis=1, keepdims=True) + EPS)
        parts.append(jnp.broadcast_to(r, v.shape))
    return jnp.concatenate(parts, axis=1)


def _gate_fwd(y, xbc_c, proj, d_x, gn_w, *, tm=256):
    t = y.shape[0]

    def body(y_ref, x_ref, z_ref, d_ref, w_ref, o_ref):
        y1 = y_ref[...] + d_ref[...] * x_ref[...]
        zv = _f32(z_ref[...])
        y2 = y1 * zv * _sigmoid(zv)
        o_ref[...] = (y2 * _group_rms(y2) * w_ref[...]).astype(BF16)

    row = pl.BlockSpec((tm, D_INNER), lambda i: (i, 0))
    vec = pl.BlockSpec((1, D_INNER), lambda i: (0, 0))
    return _pallas(
        body, name="gate_fwd", grid=(t // tm,), in_specs=[row, row, row, vec, vec], out_specs=row,
        out_shape=jax.ShapeDtypeStruct((t, D_INNER), BF16), compiler_params=_params(1),
    )(y, xbc_c, proj, d_x, gn_w)


def _gate_bwd(dyg, y, xbc_c, proj, d_x, gn_w, *, tm=256):
    t = y.shape[0]
    gw = D_INNER // SSM_G

    def body(dg_ref, y_ref, x_ref, z_ref, d_ref, w_ref, dy_ref, dz_ref, dw_ref, dd_ref):
        i = pl.program_id(0)
        xv = x_ref[...]
        dxv = d_ref[...]
        y1 = y_ref[...] + dxv * xv
        zv = _f32(z_ref[...])
        sz = _sigmoid(zv)
        y2 = y1 * zv * sz
        rr = _group_rms(y2)
        xh = y2 * rr
        dg = _f32(dg_ref[...])
        gq = dg * w_ref[...]
        prod = gq * xh
        means = []
        for g in range(SSM_G):
            mg = jnp.mean(prod[:, g * gw:(g + 1) * gw], axis=1, keepdims=True)
            means.append(jnp.broadcast_to(mg, (tm, gw)))
        dy2 = rr * (gq - xh * jnp.concatenate(means, axis=1))
        dy1 = dy2 * zv * sz
        dy_ref[...] = dy1
        dz_ref[...] = (dy2 * y1 * _dsilu(zv, sz)).astype(BF16)

        @pl.when(i == 0)
        def _():
            dw_ref[...] = jnp.zeros_like(dw_ref)
            dd_ref[...] = jnp.zeros_like(dd_ref)

        dw_ref[...] += jnp.sum(dg * xh, axis=0, keepdims=True)
        dd_ref[...] += jnp.sum(dy1 * xv, axis=0, keepdims=True)

    row = pl.BlockSpec((tm, D_INNER), lambda i: (i, 0))
    vec = pl.BlockSpec((1, D_INNER), lambda i: (0, 0))
    return _pallas(
        body, name="gate_bwd", grid=(t // tm,), in_specs=[row, row, row, row, vec, vec],
        out_specs=[row, row, vec, vec],
        out_shape=[jax.ShapeDtypeStruct((t, D_INNER), F32),
                   jax.ShapeDtypeStruct((t, IN_ODD_PAD), BF16), jax.ShapeDtypeStruct((1, D_INNER), F32),
                   jax.ShapeDtypeStruct((1, D_INNER), F32)],
        compiler_params=_params(1),
    )(dyg, y, xbc_c, proj, d_x, gn_w)


ANY = pl.BlockSpec(memory_space=pl.ANY)


def _remote(src, dst, sems, k, to):
    send_sems, recv_sems = sems
    return pltpu.make_async_remote_copy(src_ref=src, dst_ref=dst, send_sem=send_sems.at[k], recv_sem=recv_sems.at[k],
                                        device_id=to, device_id_type=MESH)


NCHIP = 4


def _gathered_shape(kind, shard):
    r, n = shard.shape
    shape = {"cols": (r, NCHIP * n), "slab": (NCHIP, r, n), "rows": (NCHIP * r, n)}[kind]
    return jax.ShapeDtypeStruct(shape, shard.dtype)


def _gather_plan(kinds, shards, outs, sems, small=None):
    ici_s, ici_r, d2d_s, d2d_r = sems
    nw = len(shards)
    per = nw + (small is not None)

    def place():
        x, y, c = lax.axis_index("x"), lax.axis_index("y"), lax.axis_index("c")
        return 2 * x + y, c, (x, y, 1 - c), [(1 - x, y), (x, 1 - y), (1 - x, 1 - y)]

    def region(j, chip, half):
        r, n = shards[j].shape
        h = r // 2
        if kinds[j] == "cols":
            return outs[j].at[pl.ds(half * h, h), pl.ds(pl.multiple_of(chip * n, LANE), n)]
        if kinds[j] == "slab":
            return outs[j].at[chip, pl.ds(half * h, h), :]
        return outs[j].at[pl.ds(chip * r + half * h, h), :]

    def my_sends(me, c, peers):
        cps = []
        for k, (px, py) in enumerate(peers):
            for j in range(nw):
                h = shards[j].shape[0] // 2
                cps.append(_remote(shards[j].at[pl.ds(c * h, h), :], region(j, me, c), (ici_s, ici_r), per * k + j, (px, py, c)))
            if small is not None:
                cps.append(_remote(small[0], small[1].at[me], (ici_s, ici_r), per * k + nw, (px, py, c)))
        return cps

    def start():
        me, c, _, peers = place()
        for cp in my_sends(me, c, peers):
            cp.start()

    def finish():
        me, c, sib, peers = place()
        fwds = []
        for k, (px, py) in enumerate(peers):
            q = 2 * px + py
            for j in range(nw):
                d = region(j, q, c)
                _remote(d, d, (ici_s, ici_r), per * k + j, (px, py, c)).wait_recv()
                fwds.append(_remote(d, d, (d2d_s, d2d_r), nw * k + j, sib))
                fwds[-1].start()
            if small is not None:
                _remote(small[0], small[1].at[q], (ici_s, ici_r), per * k + nw, (px, py, c)).wait_recv()
        for k, (px, py) in enumerate(peers):
            for j in range(nw):
                d = region(j, 2 * px + py, 1 - c)
                _remote(d, d, (d2d_s, d2d_r), nw * k + j, sib).wait_recv()
        for cp in my_sends(me, c, peers) + fwds:
            cp.wait_send()

    return start, finish


def _gather_sems(nw, with_small):
    n_ici = 3 * (nw + with_small)
    return [pltpu.SemaphoreType.DMA((n_ici,)), pltpu.SemaphoreType.DMA((n_ici,)),
            pltpu.SemaphoreType.DMA((3 * nw,)), pltpu.SemaphoreType.DMA((3 * nw,))]


def _gather_shards(kinds, shards, small):
    nw = len(shards)

    def body(*refs):
        ins, sm, outs, osm, sems = refs[:nw], refs[nw], refs[nw + 1:2 * nw + 1], refs[2 * nw + 1], refs[2 * nw + 2:]
        start, finish = _gather_plan(kinds, ins, outs, sems, small=(sm, osm))
        start()
        finish()

    return _pallas(
        body, name="gather_shards", in_specs=[ANY] * (nw + 1), out_specs=[ANY] * (nw + 1),
        out_shape=[_gathered_shape(kd, s) for kd, s in zip(kinds, shards)]
        + [jax.ShapeDtypeStruct((NCHIP,) + small.shape, small.dtype)],
        scratch_shapes=_gather_sems(nw, 1),
    )(*shards, small)


def _place_cols(full, shard, *, name, tr=256):
    r, n = shard.shape

    def body(p_ref, full_any, s_ref, o_ref):
        o_ref[...] = s_ref[...]

    return _pallas(
        body, name=name,
        grid_spec=pltpu.PrefetchScalarGridSpec(
            num_scalar_prefetch=1, grid=(r // tr,),
            in_specs=[pl.BlockSpec(memory_space=pl.ANY), pl.BlockSpec((tr, n), lambda i, p: (i, 0))],
            out_specs=pl.BlockSpec((tr, n), lambda i, p: (i, p[0]))),
        out_shape=jax.ShapeDtypeStruct(full.shape, full.dtype), input_output_aliases={1: 0},
        compiler_params=_params(1),
    )(_chip_index(), full, shard)


def _matmul_with_gather(a, b, kinds, shards, *, out_dtype, bm, bn, name):
    (m, k), n = a.shape, b.shape[1]
    nw = len(shards)
    nj, ni = n // bn, m // bm

    def body(*refs):
        a_ref, b_ref, ins, o_ref = refs[0], refs[1], refs[2:2 + nw], refs[2 + nw]
        outs, sems = refs[3 + nw:3 + 2 * nw], refs[3 + 2 * nw:]
        start, finish = _gather_plan(kinds, ins, outs, sems)
        j, i = pl.program_id(0), pl.program_id(1)

        @pl.when(jnp.logical_and(j == 0, i == 0))
        def _():
            start()

        o_ref[...] = _dot(a_ref[...], b_ref[...]).astype(out_dtype)

        @pl.when(jnp.logical_and(j == nj - 1, i == ni - 1))
        def _():
            finish()

    return _pallas(
        body, name=name, grid=(nj, ni),
        in_specs=[pl.BlockSpec((bm, k), lambda j, i: (i, 0)), pl.BlockSpec((k, bn), lambda j, i: (0, j))] + [ANY] * nw,
        out_specs=[pl.BlockSpec((bm, bn), lambda j, i: (i, j))] + [ANY] * nw,
        out_shape=[jax.ShapeDtypeStruct((m, n), out_dtype)] + [_gathered_shape(kd, s) for kd, s in zip(kinds, shards)],
        scratch_shapes=_gather_sems(nw, 0), compiler_params=_params(2),
    )(a, b, *shards)


def _other_half(a, c):
    axis = a.ndim - 2
    h = a.shape[axis] // 2
    rows = pl.ds(pl.multiple_of((1 - c) * h, 8), h)
    return a.at[rows, :] if a.ndim == 2 else a.at[:, rows, :]


def _half_shape(a):
    axis = a.ndim - 2
    return jax.ShapeDtypeStruct(a.shape[:axis] + (a.shape[axis] // 2,) + a.shape[axis + 1:], a.dtype)


def _pair_swap(bigs, *, name):
    nb = len(bigs)

    def body(*refs):
        ins, outs, send_sems, recv_sems = refs[:nb], refs[nb:2 * nb], refs[2 * nb], refs[2 * nb + 1]
        x, y, c = lax.axis_index("x"), lax.axis_index("y"), lax.axis_index("c")
        pair = [_remote(_other_half(a, c), q, (send_sems, recv_sems), j, (x, y, 1 - c))
                for j, (a, q) in enumerate(zip(ins, outs))]
        for cp in pair:
            cp.start()
        for cp in pair:
            cp.wait()

    return _pallas(
        body, name=name, in_specs=[ANY] * nb, out_specs=[ANY] * nb, out_shape=[_half_shape(a) for a in bigs],
        scratch_shapes=[pltpu.SemaphoreType.DMA((nb,)), pltpu.SemaphoreType.DMA((nb,))],
    )(*bigs)


def _pair_exchange(bigs, gsmall, grep):
    nb = len(bigs)

    def body(*refs):
        ins, sm, rp = refs[:nb], refs[nb], refs[nb + 1]
        outs, osm, orp = refs[nb + 2:2 * nb + 2], refs[2 * nb + 2], refs[2 * nb + 3]
        pair_s, pair_r, send_sems, recv_sems, local_sems = refs[2 * nb + 4:]
        x, y, c = lax.axis_index("x"), lax.axis_index("y"), lax.axis_index("c")
        me = 4 * x + 2 * y + c
        chip = 2 * x + y
        sib = (x, y, 1 - c)
        pair = [_remote(_other_half(a, c), q, (pair_s, pair_r), j, sib) for j, (a, q) in enumerate(zip(ins, outs))]
        for cp in pair:
            cp.start()
        own = [pltpu.make_async_copy(sm.at[chip], osm.at[me], local_sems.at[0]),
               pltpu.make_async_copy(rp, orp.at[me], local_sems.at[1])]
        for cp in own:
            cp.start()
        peers = []
        for k in range(7):
            fx, fy, fc = ((k + 1) >> 2) & 1, ((k + 1) >> 1) & 1, (k + 1) & 1
            peers.append((1 - x if fx else x, 1 - y if fy else y, 1 - c if fc else c))
        sends = []
        for k, (px, py, pc) in enumerate(peers):
            sends.append(_remote(sm.at[2 * px + py], osm.at[me], (send_sems, recv_sems), 2 * k, (px, py, pc)))
            sends.append(_remote(rp, orp.at[me], (send_sems, recv_sems), 2 * k + 1, (px, py, pc)))
        for cp in sends:
            cp.start()
        for k, (px, py, pc) in enumerate(peers):
            slot = 4 * px + 2 * py + pc
            _remote(sm.at[chip], osm.at[slot], (send_sems, recv_sems), 2 * k, (px, py, pc)).wait_recv()
            _remote(rp, orp.at[slot], (send_sems, recv_sems), 2 * k + 1, (px, py, pc)).wait_recv()
        for cp in pair:
            cp.wait_recv()
        for cp in pair + sends:
            cp.wait_send()
        for cp in own:
            cp.wait()

    return _pallas(
        body, name="pair_exchange", in_specs=[ANY] * (nb + 2), out_specs=[ANY] * (nb + 2),
        out_shape=[_half_shape(a) for a in bigs]
        + [jax.ShapeDtypeStruct((8,) + gsmall.shape[1:], F32), jax.ShapeDtypeStruct((8,) + grep.shape, F32)],
        scratch_shapes=[pltpu.SemaphoreType.DMA((max(nb, 1),)), pltpu.SemaphoreType.DMA((max(nb, 1),)),
                        pltpu.SemaphoreType.DMA((14,)), pltpu.SemaphoreType.DMA((14,)),
                        pltpu.SemaphoreType.DMA((2,))],
    )(*bigs, gsmall, grep)


def _core_index():
    return lax.axis_index("c").astype(jnp.int32).reshape(1)


def _half_add(full, other, *, axis, block, name):
    nd = full.ndim
    nblk = other.shape[axis] // block[axis]
    grid = tuple(other.shape[d] // block[d] for d in range(nd))

    def body(c_ref, f_ref, o_ref, out_ref):
        out_ref[...] = (f_ref[...] + o_ref[...]).astype(BF16)

    def full_map(*idx):
        ids, c_ref = list(idx[:nd]), idx[nd]
        ids[axis] = ids[axis] + c_ref[0] * nblk
        return tuple(ids)

    def plain_map(*idx):
        return tuple(idx[:nd])

    return _pallas(
        body, name=name,
        grid_spec=pltpu.PrefetchScalarGridSpec(
            num_scalar_prefetch=1, grid=grid,
            in_specs=[pl.BlockSpec(block, full_map), pl.BlockSpec(block, plain_map)],
            out_specs=pl.BlockSpec(block, plain_map)),
        out_shape=jax.ShapeDtypeStruct(other.shape, BF16), compiler_params=_params(nd),
    )(_core_index(), full, other)


NPEER = 3


def _landing_shape(kind, src):
    if kind == "pair":
        return _half_shape(src)
    if kind == "cols":
        return jax.ShapeDtypeStruct((NPEER, src.shape[0], src.shape[1] // NCHIP), src.dtype)
    return jax.ShapeDtypeStruct((NPEER,) + src.shape[1:], src.dtype)


def _chip_plan(kinds, srcs, lands, sems):
    nw = len(srcs)
    chipwise = [j for j in range(nw) if kinds[j] != "pair"]
    pairwise = [j for j in range(nw) if kinds[j] == "pair"]

    def place():
        x, y, c = lax.axis_index("x"), lax.axis_index("y"), lax.axis_index("c")
        return 2 * x + y, c, [(1 - x, y), (x, 1 - y), (1 - x, 1 - y)]

    def pair_copies(c):
        sib = (lax.axis_index("x"), lax.axis_index("y"), 1 - c)
        return [_remote(_other_half(srcs[j], c), lands[j], sems, j, sib) for j in pairwise]

    def piece(j, chip):
        if kinds[j] == "cols":
            n = srcs[j].shape[1] // NCHIP
            return srcs[j].at[:, pl.ds(pl.multiple_of(chip * n, LANE), n)]
        return srcs[j].at[chip]

    def my_sends(c, peers):
        return pair_copies(c) + [_remote(piece(j, 2 * px + py), lands[j].at[k], sems, nw * k + j, (px, py, c))
                                 for k, (px, py) in enumerate(peers) for j in chipwise]

    def start():
        _, c, peers = place()
        for cp in my_sends(c, peers):
            cp.start()

    def finish():
        me, c, peers = place()
        for cp in pair_copies(c):
            cp.wait_recv()
        for k, (px, py) in enumerate(peers):
            for j in chipwise:
                _remote(piece(j, me), lands[j].at[k], sems, nw * k + j, (px, py, c)).wait_recv()
        for cp in my_sends(c, peers):
            cp.wait_send()

    return start, finish


def _chip_sems(nw):
    return [pltpu.SemaphoreType.DMA((NPEER * nw,)), pltpu.SemaphoreType.DMA((NPEER * nw,))]


def _chip_index():
    return (2 * lax.axis_index("x") + lax.axis_index("y")).astype(jnp.int32).reshape(1)


def _chip_sum(own, slots, *, own_block, own_map, block, name):
    npeer = slots.shape[0]
    shape = slots.shape[1:]
    grid = (shape[0] // block[0], shape[1] // block[1])

    def body(p_ref, own_ref, s_ref, o_ref):
        acc = own_ref[...].reshape(block).astype(F32)
        for q in range(npeer):
            acc = acc + s_ref[q].astype(F32)
        o_ref[...] = acc

    return _pallas(
        body, name=name,
        grid_spec=pltpu.PrefetchScalarGridSpec(
            num_scalar_prefetch=1, grid=grid,
            in_specs=[pl.BlockSpec(own_block, own_map),
                      pl.BlockSpec((npeer,) + block, lambda i, j, p: (0, i, j))],
            out_specs=pl.BlockSpec(block, lambda i, j, p: (i, j))),
        out_shape=jax.ShapeDtypeStruct(shape, F32), compiler_params=_params(2),
    )(_chip_index(), own, slots)


def _pair_share(r_in0, r_in1, r_out0, r_out1):
    def body(a0, a1, b0, b1, g0, g1, h0, h1, send_sems, recv_sems):
        x, y, c = lax.axis_index("x"), lax.axis_index("y"), lax.axis_index("c")
        sib = (x, y, 1 - c)
        sends = [_remote(s, d, (send_sems, recv_sems), j, sib)
                 for j, (s, d) in enumerate(zip([a0, a1, b0, b1], [g0, g1, h0, h1]))]
        for cp in sends:
            cp.start()
        for cp in sends:
            cp.wait()

    return _pallas(
        body, name="pair_share", in_specs=[ANY] * 4, out_specs=[ANY] * 4,
        out_shape=[jax.ShapeDtypeStruct(r.shape, F32) for r in (r_in0, r_in1, r_out0, r_out1)],
        scratch_shapes=[pltpu.SemaphoreType.DMA((4,)), pltpu.SemaphoreType.DMA((4,))],
    )(r_in0, r_in1, r_out0, r_out1)


def _adam_math(g, w, m, v):
    c1 = 1.0 - ADAM_B1 ** ADAM_STEP
    c2 = 1.0 - ADAM_B2 ** ADAM_STEP
    m2 = ADAM_B1 * m + (1.0 - ADAM_B1) * g
    v2 = ADAM_B2 * v + (1.0 - ADAM_B2) * (g * g)
    delta = -ADAM_LR * ((m2 / c1) / (jnp.sqrt(v2 / c2) + ADAM_EPS) + ADAM_WD * w)
    return delta, m2, v2


def _adamw_nat(g_mine, g_sib, w, m, v, *, name, tr):
    rows, cw = w.shape
    nt = g_mine.shape[0] // tr

    def body(c_ref, gm_ref, gs_ref, w_ref, m_ref, v_ref, go_ref, d_ref, nm_ref, nv_ref):
        mine = pl.program_id(0) // nt == c_ref[0]
        gv = jnp.where(mine, gm_ref[...], gs_ref[...])[:, 0:cw]
        delta, m2, v2 = _adam_math(gv, w_ref[...], m_ref[...], v_ref[...])
        go_ref[...] = gv
        d_ref[...] = delta
        nm_ref[...] = m2
        nv_ref[...] = v2

    def mine_map(i, c_ref):
        return (jnp.where(i // nt == c_ref[0], i % nt, 0), 0)

    def sib_map(i, c_ref):
        return (jnp.where(i // nt == c_ref[0], 0, i % nt), 0)

    row = pl.BlockSpec((tr, cw), lambda i, c_ref: (i, 0))
    gspec = (tr, g_mine.shape[1])
    out = jax.ShapeDtypeStruct((rows, cw), F32)
    return _pallas(
        body, name=name,
        grid_spec=pltpu.PrefetchScalarGridSpec(
            num_scalar_prefetch=1, grid=(rows // tr,),
            in_specs=[pl.BlockSpec(gspec, mine_map), pl.BlockSpec(gspec, sib_map), row, row, row],
            out_specs=[row, row, row, row]),
        out_shape=[out, out, out, out], compiler_params=_params(1),
    )(_core_index(), g_mine, g_sib, w, m, v)


def _adamw(slots, w, m, v, *, name, tr):
    nd, rows, _ = slots.shape
    c1 = 1.0 - ADAM_B1 ** ADAM_STEP
    c2 = 1.0 - ADAM_B2 ** ADAM_STEP

    def body(s_ref, w_ref, m_ref, v_ref, g_ref, d_ref, nm_ref, nv_ref):
        g = s_ref[0]
        for d in range(1, nd):
            g = g + s_ref[d]
        m2 = ADAM_B1 * m_ref[...] + (1.0 - ADAM_B1) * g
        v2 = ADAM_B2 * v_ref[...] + (1.0 - ADAM_B2) * (g * g)
        g_ref[...] = g
        nm_ref[...] = m2
        nv_ref[...] = v2
        d_ref[...] = -ADAM_LR * ((m2 / c1) / (jnp.sqrt(v2 / c2) + ADAM_EPS) + ADAM_WD * w_ref[...])

    row = pl.BlockSpec((tr, LANE), lambda i: (i, 0))
    out = jax.ShapeDtypeStruct((rows, LANE), F32)
    return _pallas(
        body, name=name, grid=(rows // tr,),
        in_specs=[pl.BlockSpec((nd, tr, LANE), lambda i: (0, i, 0)), row, row, row],
        out_specs=[row, row, row, row], out_shape=[out, out, out, out], compiler_params=_params(1),
    )(slots, w, m, v)


def _rows(a):
    return a.reshape(-1, LANE)


def _pad_rows(a, mult):
    pad = (-a.shape[0]) % mult
    return jnp.pad(a, ((0, pad), (0, 0))) if pad else a


def _pack(parts, mult):
    return _pad_rows(jnp.concatenate([_rows(p) for p in parts], axis=0), mult)


def _unpack(slab, shapes):
    out, r0 = [], 0
    for shp in shapes:
        n = 1
        for s in shp:
            n *= s
        r = n // LANE
        out.append(slab[r0:r0 + r].reshape(shp))
        r0 += r
    return out


def _pack_rep(vecs, scal):
    srow = jnp.concatenate([s.reshape(-1) for s in scal] + [jnp.zeros((LANE - 3 * SSM_H,), F32)]).reshape(1, LANE)
    return _pad_rows(jnp.concatenate([_rows(vv) for vv in vecs] + [srow], axis=0), 8)


def _unpack_rep(slab, vec_shapes, scal_shape):
    vecs, r0 = [], 0
    for shp in vec_shapes:
        vecs.append(slab[r0:r0 + 8].reshape(shp))
        r0 += 8
    srow = slab[r0]
    scal = [srow[i * SSM_H:(i + 1) * SSM_H].reshape(scal_shape) for i in range(3)]
    return vecs, scal


def kernel(x, ev_norm_w, ev_w_in, ev_dw_w, ev_dw_b, ev_ln_w, ev_ln_b, ev_w_out, od_norm_w, od_w_in, od_conv_w, od_conv_b, od_dt_bias, od_a_log, od_d, od_gnorm_w, od_w_out, final_norm_w, loss_target, m_ev_norm_w, m_ev_w_in, m_ev_dw_w, m_ev_dw_b, m_ev_ln_w, m_ev_ln_b, m_ev_w_out, m_od_norm_w, m_od_w_in, m_od_conv_w, m_od_conv_b, m_od_dt_bias, m_od_a_log, m_od_d, m_od_gnorm_w, m_od_w_out, m_final_norm_w, v_ev_norm_w, v_ev_w_in, v_ev_dw_w, v_ev_dw_b, v_ev_ln_w, v_ev_ln_b, v_ev_w_out, v_od_norm_w, v_od_w_in, v_od_conv_w, v_od_conv_b, v_od_dt_bias, v_od_a_log, v_od_d, v_od_gnorm_w, v_od_w_out, v_final_norm_w):
    nb, seq, d = x.shape
    t = nb * seq
    nchip = 4
    xf = x.reshape(t, d)
    tgt = loss_target.reshape(t, d)

    big_w = [ev_w_in[0], od_w_in[0], ev_w_out[0], od_w_out[0]]
    small_w = [ev_dw_w[0], od_norm_w[0], od_conv_w[0], od_conv_b[0], od_gnorm_w[0]]
    small_shapes = [a.shape for a in small_w]
    big_b = [a.astype(BF16) for a in big_w]
    small_slab = _pack(small_w, 8)
    w_in0, w_out0, gath_small = _gather_shards(("cols", "rows"), [big_b[0], big_b[2]], small_slab)
    chip = 2 * lax.axis_index("x") + lax.axis_index("y")
    w_in0 = _place_cols(w_in0, big_b[0], name="place_w_in0")
    w_out0 = lax.dynamic_update_slice(w_out0, big_b[2], (chip * big_b[2].shape[0], 0))
    gath_small = lax.dynamic_update_slice(gath_small, small_slab[None], (chip, 0, 0))
    per_chip = [_unpack(gath_small[p], small_shapes) for p in range(nchip)]

    def cat(idx, axis):
        return jnp.concatenate([per_chip[p][idx] for p in range(nchip)], axis=axis)

    dw_w = jnp.pad(cat(0, 1), ((0, HALO - CONF_K), (0, 0)))
    dw_w8 = jnp.repeat(dw_w, SUB, axis=0)
    n1_w = cat(1, 0).reshape(1, d)
    conv_w = jnp.pad(cat(2, 1), ((0, PH - SSM_K), (0, 0)))
    conv_b = cat(3, 0).reshape(1, XBC)
    gn_w = cat(4, 0).reshape(1, D_INNER)

    def lanes(a):
        return jnp.pad(a.reshape(1, -1), ((0, 0), (0, LANE - a.size)))

    dt_bias, a_log = lanes(od_dt_bias), lanes(od_a_log)
    d_x = jnp.repeat(od_d.reshape(-1), SSM_P).reshape(1, D_INNER)
    hid = lax.broadcasted_iota(jnp.int32, (SSM_H, D_INNER), 1) // SSM_P
    ex = (hid == lax.broadcasted_iota(jnp.int32, (SSM_H, D_INNER), 0)).astype(BF16)
    ex_t = jnp.pad(ex.T, ((0, 0), (0, LANE - SSM_H)))
    fn_w = final_norm_w.reshape(1, d)

    n0 = _rms_fwd(xf, ev_norm_w, name="rms_fwd0")
    proj0, w_in1g, w_out1 = _matmul_with_gather(n0, w_in0, ("slab", "rows"), [big_b[1], big_b[3]],
                                                out_dtype=BF16, bm=512, bn=1024, name="in_proj0")
    w_in1g = lax.dynamic_update_slice(w_in1g, big_b[1][None], (chip, 0, 0))
    w_out1 = lax.dynamic_update_slice(w_out1, big_b[3], (chip * big_b[3].shape[0], 0))
    w_in1 = jnp.pad(jnp.concatenate([w_in1g[p] for p in range(nchip)], axis=1),
                    ((0, 0), (0, IN_ODD_PAD - IN_ODD)))
    y_conv, u2 = _conf_fwd(proj0, dw_w8, ev_dw_b, ev_ln_w, ev_ln_b, seq)
    o_att, y_att = _sba_fwd(proj0, nb, seq)
    ycat0 = jnp.concatenate([y_conv, y_att], axis=1)
    h1 = _matmul(ycat0, w_out0, mode="nn", out_dtype=F32, bm=512, bn=d, bk=D_INNER, name="out_proj0", residual=xf)
    n1 = _rms_fwd(h1, n1_w, name="rms_fwd1")
    proj1 = _matmul(n1, w_in1, mode="nn", out_dtype=BF16, bm=512, bn=768, bk=d, name="in_proj1", n_major=True)
    dt_raw = _matmul(n1, w_in1[:, D_INNER + XBC:IN_ODD_PAD], mode="nn", out_dtype=F32, bm=512,
                     bn=IN_ODD_PAD - D_INNER - XBC, bk=d, name="in_proj1_dt")
    xbc_c = _xconv_fwd(proj1, conv_w, conv_b, seq)
    dt = _dt_fwd(dt_raw, dt_bias)
    y_ssd, states = _ssd_fwd(xbc_c, dt, a_log, ex, nb, seq)
    yg = _gate_fwd(y_ssd, xbc_c, proj1, d_x, gn_w)
    h2 = _matmul(yg, w_out1, mode="nn", out_dtype=F32, bm=512, bn=d, bk=D_INNER, name="out_proj1", residual=h1)
    dh2, g_fn, loss_part = _final_loss(h2, fn_w, tgt)

    dyg = _matmul(dh2, w_out1, mode="nt", out_dtype=BF16, bm=512, bn=1024, bk=d, name="d_out_proj1")
    g_w_out1 = _matmul(yg, dh2, mode="tn", out_dtype=F32, bm=1024, bn=d, bk=1024, name="dw_out_proj1")
    dy_ssd, dz, g_gn, g_dx = _gate_bwd(dyg, y_ssd, xbc_c, proj1, d_x, gn_w)
    dxbc_c, ddt, g_a = _ssd_bwd(xbc_c, dt, a_log, ex, ex_t, states, dy_ssd, d_x, nb, seq)
    dproj1, g_conv_w, g_conv_b = _xconv_bwd(proj1, dxbc_c, conv_w, conv_b, dz, seq)
    dproj1, g_dt_bias = _dt_bwd(dt_raw, dt_bias, ddt, dproj1)
    dn1 = _matmul(dproj1, w_in1, mode="nt", out_dtype=BF16, bm=1024, bn=d, bk=1792, name="d_in_proj1")
    g_w_in1 = _matmul(n1, dproj1, mode="tn", out_dtype=F32, bm=d, bn=1792, bk=1024, name="dw_in_proj1")
    dh1, g_n1 = _rms_bwd(dn1, h1, n1_w, dh2, name="rms_bwd1")

    ro = D_INNER // nchip
    n1 = IN_ODD // nchip
    n1p = -(-n1 // LANE) * LANE
    g_w_out1c = g_w_out1.reshape(nchip, ro, d)
    dycat0, q_in1, q_out1 = _matmul(dh1, w_out0, mode="nt", out_dtype=BF16, bm=512, bn=1024, bk=d, name="d_out_proj0",
                                    comm_kinds=("pair", "pair"), comm_srcs=[g_w_in1, g_w_out1c])
    g_w_out0 = _matmul(ycat0, dh1, mode="tn", out_dtype=F32, bm=1024, bn=d, bk=1024, name="dw_out_proj0")
    dq, dk, dv, dga = _sba_bwd(proj0, o_att, dycat0, nb, seq)
    s_in1n = _half_add(g_w_in1, q_in1, axis=0, block=(128, IN_ODD_PAD), name="half_add_in1")
    s_in1 = jnp.stack([jnp.pad(s_in1n[:, p * n1:(p + 1) * n1], ((0, 0), (0, n1p - n1))) for p in range(nchip)])
    s_out1 = _half_add(g_w_out1c, q_out1, axis=1, block=(1, ro // 2, d), name="half_add_out1")
    dpc, g_dw_w, g_dw_b, g_ln_w, g_ln_b, l_in1, l_out1 = _conf_bwd(
        proj0, u2, dycat0, dw_w8, ev_ln_w, ev_ln_b, seq, ("slab", "slab"), [s_in1, s_out1])
    dproj0 = jnp.concatenate([dpc, dq, dk, dv, dga], axis=1)
    g_w_in0 = _matmul(n0, dproj0, mode="tn", out_dtype=F32, bm=d, bn=1792, bk=1024, name="dw_in_proj0")
    g_w_out0c = g_w_out0.reshape(nchip, ro, d)
    q_in0, q_out0 = _pair_swap([g_w_in0, g_w_out0c], name="pair_swap_l0")
    s_in0 = _half_add(g_w_in0, q_in0, axis=0, block=(128, IN_EVEN), name="half_add_in0")
    s_out0 = _half_add(g_w_out0c, q_out0, axis=1, block=(1, ro // 2, d), name="half_add_out0")
    dn0, l_in0, l_out0 = _matmul(dproj0, w_in0, mode="nt", out_dtype=BF16, bm=1024, bn=d, bk=1792, name="d_in_proj0",
                                 comm_kinds=("cols", "slab"), comm_srcs=[s_in0, s_out0])
    grad_x, g_n0 = _rms_bwd(dn0, xf, ev_norm_w, dh1, name="rms_bwd0")

    g_dw_w = g_dw_w.reshape(HALO, SUB, CONV_W).sum(axis=1)[0:CONF_K]
    g_dw_b, g_ln_w, g_ln_b = (a.sum(axis=0, keepdims=True) for a in (g_dw_b, g_ln_w, g_ln_b))
    g_conv_w = g_conv_w.reshape(PH, SUB, XBC).sum(axis=1)[0:SSM_K]
    g_conv_b = g_conv_b.sum(axis=0, keepdims=True)
    a_neg = -jnp.exp(od_a_log.reshape(-1))
    g_a_log = g_a[0, 0:SSM_H] * a_neg
    g_d = g_dx.reshape(SSM_H, SSM_P).sum(axis=1)

    def chip_slab_small(p):
        c0, c1, c2, c3 = CONV_W // nchip, d // nchip, XBC // nchip, D_INNER // nchip
        return _pack([g_dw_w[:, p * c0:(p + 1) * c0], g_n1[0, p * c1:(p + 1) * c1],
                      g_conv_w[:, p * c2:(p + 1) * c2], g_conv_b[0, p * c2:(p + 1) * c2],
                      g_gn[0, p * c3:(p + 1) * c3]], 8)

    gsmall = jnp.stack([chip_slab_small(p) for p in range(nchip)])
    rep_vec_shapes = [ev_norm_w.shape, ev_dw_b.shape, ev_ln_w.shape, ev_ln_b.shape, final_norm_w.shape]
    grep = _pack_rep([g_n0, g_dw_b, g_ln_w, g_ln_b, g_fn], [g_dt_bias[0, 0:SSM_H], g_a_log, g_d])

    ssmall, srep = _pair_exchange([], gsmall, grep)
    r_in0 = _chip_sum(s_in0, l_in0, own_block=(128, IN_EVEN // nchip), own_map=lambda i, j, p: (i, p[0]),
                      block=(128, IN_EVEN // nchip), name="chip_sum_in0")
    r_in1 = _chip_sum(s_in1, l_in1, own_block=(1, 256, n1p), own_map=lambda i, j, p: (p[0], i, 0),
                      block=(256, n1p), name="chip_sum_in1")
    r_out0 = _chip_sum(s_out0, l_out0, own_block=(1, ro // 2, d), own_map=lambda i, j, p: (p[0], 0, 0),
                       block=(ro // 2, d), name="chip_sum_out0")
    r_out1 = _chip_sum(s_out1, l_out1, own_block=(1, ro // 2, d), own_map=lambda i, j, p: (p[0], 0, 0),
                       block=(ro // 2, d), name="chip_sum_out1")
    big_r = [r_in0, r_in1, r_out0, r_out1]
    big_q = _pair_share(*big_r)

    big_m = [m_ev_w_in[0], m_od_w_in[0], m_ev_w_out[0], m_od_w_out[0]]
    big_v = [v_ev_w_in[0], v_od_w_in[0], v_ev_w_out[0], v_od_w_out[0]]
    big_names = ["adamw_in0", "adamw_in1", "adamw_out0", "adamw_out1"]
    out_bigs = [_adamw_nat(gm, gs, w, m, v, name=nm, tr=128)
                for gm, gs, w, m, v, nm in zip(big_r, big_q, big_w, big_m, big_v, big_names)]

    def upd(slots, ws, ms, vs, packer, name, tr):
        return _adamw(slots, packer(ws), packer(ms), packer(vs), name=name, tr=tr)

    small_m = [m_ev_dw_w[0], m_od_norm_w[0], m_od_conv_w[0], m_od_conv_b[0], m_od_gnorm_w[0]]
    small_v = [v_ev_dw_w[0], v_od_norm_w[0], v_od_conv_w[0], v_od_conv_b[0], v_od_gnorm_w[0]]
    out_small = upd(ssmall, small_w, small_m, small_v, lambda a: _pack(a, 8), "adamw_small", ssmall.shape[1])

    def rep_pack(a):
        return _pack_rep(a[0:5], a[5:8])

    rep_w = [ev_norm_w, ev_dw_b, ev_ln_w, ev_ln_b, final_norm_w, od_dt_bias, od_a_log, od_d]
    rep_m = [m_ev_norm_w, m_ev_dw_b, m_ev_ln_w, m_ev_ln_b, m_final_norm_w, m_od_dt_bias, m_od_a_log, m_od_d]
    rep_v = [v_ev_norm_w, v_ev_dw_b, v_ev_ln_w, v_ev_ln_b, v_final_norm_w, v_od_dt_bias, v_od_a_log, v_od_d]
    out_rep = upd(srep, rep_w, rep_m, rep_v, rep_pack, "adamw_rep", srep.shape[1])

    results = []
    for kind in range(4):
        bw = [o[kind].reshape((1,) + o[kind].shape) for o in out_bigs]
        sw = _unpack(out_small[kind], small_shapes)
        vecs, scal = _unpack_rep(out_rep[kind], rep_vec_shapes, od_dt_bias.shape)
        results.append([
            vecs[0], bw[0], sw[0].reshape(ev_dw_w.shape), vecs[1], vecs[2], vecs[3], bw[2],
            sw[1].reshape(od_norm_w.shape), bw[1], sw[2].reshape(od_conv_w.shape), sw[3].reshape(od_conv_b.shape),
            scal[0], scal[1], scal[2], sw[4].reshape(od_gnorm_w.shape), bw[3], vecs[4]])
    loss = lax.psum(loss_part[0, 0], ("x", "y", "c"))
    return (loss, grad_x.reshape(x.shape), *results[0], *results[1], *results[2], *results[3])
```

```python
import jax
import jax.numpy as jnp
from jax import lax
from jax.experimental import pallas as pl
from jax.experimental.pallas import tpu as pltpu

F32 = jnp.float32
BF16 = jnp.bfloat16

D_MODEL = 1024
CONV_W = 1024
ATT_W = 1024
HEAD_DIM = 128
N_HEADS = 8
CONF_K = 31
IN_EVEN = 7168
D_INNER = 2048
SSM_P = 64
SSM_H = 32
SSM_G = 4
SSM_R = SSM_H // SSM_G
SSM_N = 128
SSM_K = 4
CHUNK = 128
XBC = D_INNER + 2 * SSM_G * SSM_N
IN_ODD = D_INNER + XBC + SSM_H
IN_ODD_PAD = 5376
EPS = 1e-6
QB = 128
NEG_CUT = -100.0

ADAM_LR = 0.001
ADAM_B1 = 0.9
ADAM_B2 = 0.999
ADAM_EPS = 1e-08
ADAM_WD = 0.01
ADAM_STEP = 10

LANE = 128
VMEM_LIMIT = 56 * 1024 * 1024
MESH = pl.DeviceIdType.MESH

NN = (((1,), (0,)), ((), ()))
NT = (((1,), (1,)), ((), ()))
TN = (((0,), (0,)), ((), ()))


def _pallas(body, **kw):
    return pl.pallas_call(body, **kw)


def _params(n_axes):
    return pltpu.CompilerParams(dimension_semantics=("arbitrary",) * n_axes, vmem_limit_bytes=VMEM_LIMIT)


def _dot(a, b, dims=NN):
    return lax.dot_general(a.astype(BF16), b.astype(BF16), dims, preferred_element_type=F32)


def _parts(x):
    h = x.astype(BF16)
    r = x - h.astype(F32)
    m = r.astype(BF16)
    l = (r - m.astype(F32)).astype(BF16)
    return (h, m, l)


def _dotx(x, e01, dims=NN):
    acc = None
    for p in _parts(x):
        t = lax.dot_general(p, e01, dims, preferred_element_type=F32)
        acc = t if acc is None else acc + t
    return acc


def _dotx2(x, e01, dims=NN):
    h = x.astype(BF16)
    l = (x - h.astype(F32)).astype(BF16)
    return (lax.dot_general(h, e01, dims, preferred_element_type=F32)
            + lax.dot_general(l, e01, dims, preferred_element_type=F32))


def _xdot(e01, x, dims=NN):
    acc = None
    for p in _parts(x):
        t = lax.dot_general(e01, p, dims, preferred_element_type=F32)
        acc = t if acc is None else acc + t
    return acc


def _f32(x):
    return x.astype(F32)


def _sigmoid(x):
    return 1.0 / (1.0 + jnp.exp(-x))


def _dsilu(x, s):
    return s * (1.0 + x * (1.0 - s))


def _matmul(a, b, *, mode, out_dtype, bm, bn, bk, name, residual=None, n_major=False, comm_kinds=(), comm_srcs=()):
    if mode == "nn":
        (m, k), n = a.shape, b.shape[1]
        a_blk, a_map = (bm, bk), lambda i, j, kk: (i, kk)
        b_blk, b_map = (bk, bn), lambda i, j, kk: (kk, j)
        dims = NN
    elif mode == "nt":
        (m, k), n = a.shape, b.shape[0]
        a_blk, a_map = (bm, bk), lambda i, j, kk: (i, kk)
        b_blk, b_map = (bn, bk), lambda i, j, kk: (j, kk)
        dims = NT
    else:
        (k, m), n = a.shape, b.shape[1]
        a_blk, a_map = (bk, bm), lambda i, j, kk: (kk, i)
        b_blk, b_map = (bk, bn), lambda i, j, kk: (kk, j)
        dims = TN
    bm, bn, bk = min(bm, m), min(bn, n), min(bk, k)
    if mode != "nn":
        a_blk = (bm, bk) if mode == "nt" else (bk, bm)
        b_blk = (bn, bk) if mode == "nt" else (bk, bn)
    else:
        a_blk, b_blk = (bm, bk), (bk, bn)
    assert m % bm == 0 and n % bn == 0 and k % bk == 0, (name, m, n, k)
    nk = k // bk
    has_res = residual is not None

    def order(f):
        return (lambda j, i, kk: f(i, j, kk)) if n_major else f

    nw = len(comm_srcs)
    grid = (n // bn, m // bm, nk) if n_major else (m // bm, n // bn, nk)

    def body(*refs):
        a_ref, b_ref = refs[0], refs[1]
        r_ref = refs[2] if has_res else None
        n_in = 2 + has_res + nw
        o_ref = refs[n_in]
        n_out = n_in + 1 + nw

        def finish(r):
            if has_res:
                r = r + r_ref[...]
            o_ref[...] = r.astype(out_dtype)

        def compute():
            if nk == 1:
                finish(_dot(a_ref[...], b_ref[...], dims))
                return
            acc_ref = refs[n_out]
            kk = pl.program_id(2)

            @pl.when(kk == 0)
            def _():
                acc_ref[...] = jnp.zeros_like(acc_ref)

            acc_ref[...] += _dot(a_ref[...], b_ref[...], dims)

            @pl.when(kk == nk - 1)
            def _():
                finish(acc_ref[...])

        if not nw:
            compute()
            return
        start, done = _chip_plan(comm_kinds, refs[2 + has_res:n_in], refs[n_in + 1:n_out], refs[n_out + (nk > 1):])
        ids = [pl.program_id(ax) for ax in range(3)]

        @pl.when(jnp.logical_and(jnp.logical_and(ids[0] == 0, ids[1] == 0), ids[2] == 0))
        def _():
            start()

        compute()

        @pl.when(jnp.logical_and(jnp.logical_and(ids[0] == grid[0] - 1, ids[1] == grid[1] - 1), ids[2] == grid[2] - 1))
        def _():
            done()

    in_specs = [pl.BlockSpec(a_blk, order(a_map)), pl.BlockSpec(b_blk, order(b_map))]
    args = [a, b]
    out_map = order(lambda i, j, kk: (i, j))
    if has_res:
        in_specs.append(pl.BlockSpec((bm, bn), out_map))
        args.append(residual)
    any_spec = pl.BlockSpec(memory_space=pl.ANY)
    out_specs = [pl.BlockSpec((bm, bn), out_map)] + [any_spec] * nw
    out_shape = [jax.ShapeDtypeStruct((m, n), out_dtype)] + [_landing_shape(kd, s) for kd, s in zip(comm_kinds, comm_srcs)]
    res = _pallas(
        body, name=name, grid=grid, in_specs=in_specs + [any_spec] * nw, out_specs=out_specs, out_shape=out_shape,
        scratch_shapes=([pltpu.VMEM((bm, bn), F32)] if nk > 1 else []) + (_chip_sems(nw) if nw else []),
        compiler_params=_params(3),
    )(*args, *comm_srcs)
    return res if nw else res[0]


def _rms_fwd(x, w, *, name, tm=512):
    t, d = x.shape

    def body(x_ref, w_ref, o_ref):
        xv = x_ref[...]
        r = lax.rsqrt(jnp.mean(xv * xv, axis=1, keepdims=True) + EPS)
        o_ref[...] = (xv * r * w_ref[...]).astype(BF16)

    return _pallas(
        body, name=name, grid=(t // tm,),
        in_specs=[pl.BlockSpec((tm, d), lambda i: (i, 0)), pl.BlockSpec((1, d), lambda i: (0, 0))],
        out_specs=pl.BlockSpec((tm, d), lambda i: (i, 0)),
        out_shape=jax.ShapeDtypeStruct((t, d), BF16), compiler_params=_params(1),
    )(x, w)


def _rms_bwd(dn, x, w, dres, *, name, tm=512):
    t, d = x.shape

    def body(dn_ref, x_ref, w_ref, dr_ref, dx_ref, dw_ref):
        i = pl.program_id(0)
        xv = x_ref[...]
        r = lax.rsqrt(jnp.mean(xv * xv, axis=1, keepdims=True) + EPS)
        xh = xv * r
        dy = dn_ref[...].astype(F32)
        g = dy * w_ref[...]
        dx_ref[...] = dr_ref[...] + r * (g - xh * jnp.mean(g * xh, axis=1, keepdims=True))

        @pl.when(i == 0)
        def _():
            dw_ref[...] = jnp.zeros_like(dw_ref)

        dw_ref[...] += jnp.sum(dy * xh, axis=0, keepdims=True)

    row = pl.BlockSpec((tm, d), lambda i: (i, 0))
    vec = pl.BlockSpec((1, d), lambda i: (0, 0))
    return _pallas(
        body, name=name, grid=(t // tm,), in_specs=[row, row, vec, row], out_specs=[row, vec],
        out_shape=[jax.ShapeDtypeStruct((t, d), F32), jax.ShapeDtypeStruct((1, d), F32)],
        compiler_params=_params(1),
    )(dn, x, w, dres)


def _final_loss(h, w, target, *, tm=512):
    t, d = h.shape

    def body(h_ref, w_ref, t_ref, dh_ref, dw_ref, loss_ref):
        i = pl.program_id(0)
        xv = h_ref[...]
        r = lax.rsqrt(jnp.mean(xv * xv, axis=1, keepdims=True) + EPS)
        xh = xv * r
        wv = w_ref[...]
        err = xh * wv - t_ref[...]
        dy = err * (1.0 / d)
        g = dy * wv
        dh_ref[...] = r * (g - xh * jnp.mean(g * xh, axis=1, keepdims=True))

        @pl.when(i == 0)
        def _():
            dw_ref[...] = jnp.zeros_like(dw_ref)
            loss_ref[...] = jnp.zeros_like(loss_ref)

        dw_ref[...] += jnp.sum(dy * xh, axis=0, keepdims=True)
        part = jnp.sum(jnp.sum(err * err, axis=1, keepdims=True), axis=0, keepdims=True)
        loss_ref[...] += part * (0.5 / d)

    row = pl.BlockSpec((tm, d), lambda i: (i, 0))
    vec = pl.BlockSpec((1, d), lambda i: (0, 0))
    return _pallas(
        body, name="final_loss", grid=(t // tm,), in_specs=[row, vec, row],
        out_specs=[row, vec, pl.BlockSpec((1, LANE), lambda i: (0, 0))],
        out_shape=[jax.ShapeDtypeStruct((t, d), F32), jax.ShapeDtypeStruct((1, d), F32),
                   jax.ShapeDtypeStruct((1, LANE), F32)],
        compiler_params=_params(1),
    )(h, w, target)


HALO = 32


SUB = 8
RC = 16


def _make_shifts(sh_ref, rows, shifts=tuple(range(1, SUB))):
    for s in shifts:
        sh_ref[s, 0:rows, :] = sh_ref[0, s:s + rows, :]


def _shifted(sh_ref, r0, j, rows):
    return sh_ref[j % SUB, pl.ds(r0 + (j - j % SUB), rows), :]


def _taps(w8_ref, sh_ref, r0, first, step, init):
    accs = [init] * (RC // SUB)
    for k in range(CONF_K):
        wk = w8_ref[k * SUB:(k + 1) * SUB, :]
        x = _shifted(sh_ref, r0, first + step * k, RC)
        accs = [a + wk * x[q * SUB:(q + 1) * SUB] for q, a in enumerate(accs)]
    return jnp.concatenate(accs, axis=0)


def _conf_fwd(proj, dw_w, dw_b, ln_w, ln_b, seq, *, tm=256):
    t = proj.shape[0]
    c = CONV_W
    tps = seq // tm
    hb = tm // HALO

    def body(a_ref, b_ref, g_ref, ha_ref, hb_ref, w_ref, wb_ref, lw_ref, lb_ref, y_ref, u2_ref, sh_ref):
        i = pl.program_id(0)
        keep = jnp.where(i % tps == 0, 0.0, 1.0)
        sh_ref[0, 0:HALO, :] = _f32(ha_ref[...]) * _sigmoid(_f32(hb_ref[...])) * keep
        sh_ref[0, HALO:HALO + tm, :] = _f32(a_ref[...]) * _sigmoid(_f32(b_ref[...]))
        _make_shifts(sh_ref, tm + HALO - SUB)

        def chunk(ci, carry):
            r0 = pl.multiple_of(ci * RC, RC)
            acc = _taps(w_ref, sh_ref, r0, HALO - CONF_K + 1, 1, jnp.broadcast_to(wb_ref[...], (SUB, c)))
            u2_ref[pl.ds(r0, RC), :] = acc
            mu = jnp.mean(acc, axis=1, keepdims=True)
            xc = acc - mu
            rs = lax.rsqrt(jnp.mean(xc * xc, axis=1, keepdims=True) + EPS)
            u3 = xc * rs * lw_ref[...] + lb_ref[...]
            gv = _f32(g_ref[pl.ds(r0, RC), :])
            y_ref[pl.ds(r0, RC), :] = (u3 * _sigmoid(u3) * gv * _sigmoid(gv)).astype(BF16)
            return carry

        lax.fori_loop(0, tm // RC, chunk, 0, unroll=2)

    def col(j):
        return pl.BlockSpec((tm, c), lambda i: (i, j))

    def prev(j):
        return pl.BlockSpec((HALO, c), lambda i: (jnp.maximum(i * hb - 1, 0), j))

    vec = pl.BlockSpec((1, c), lambda i: (0, 0))
    return _pallas(
        body, name="conf_fwd", grid=(t // tm,),
        in_specs=[col(0), col(1), col(2), prev(0), prev(1),
                  pl.BlockSpec((HALO * SUB, c), lambda i: (0, 0)), vec, vec, vec],
        out_specs=[pl.BlockSpec((tm, c), lambda i: (i, 0)), pl.BlockSpec((tm, c), lambda i: (i, 0))],
        out_shape=[jax.ShapeDtypeStruct((t, c), BF16), jax.ShapeDtypeStruct((t, c), F32)],
        scratch_shapes=[pltpu.VMEM((SUB, tm + HALO, c), F32)], compiler_params=_params(1),
    )(proj, proj, proj, proj, proj, dw_w, dw_b, ln_w, ln_b)


def _conf_bwd(proj, u2, dycat, dw_w, ln_w, ln_b, seq, comm_kinds, comm_srcs, *, tm=256):
    t = proj.shape[0]
    c = CONV_W
    tps = seq // tm
    hb = tm // HALO
    nhb = t // HALO
    nw = len(comm_srcs)
    nsteps = t // tm

    def fold(v):
        out = v[0:SUB]
        for q in range(1, RC // SUB):
            out = out + v[q * SUB:(q + 1) * SUB]
        return out

    def body(*refs):
        (a_ref, b_ref, g_ref, pa_ref, pb_ref, ng_ref, u2_ref, nu2_ref, dy_ref, ndy_ref,
         w_ref, lw_ref, lb_ref) = refs[:13]
        dp_ref, dww_ref, dwb_ref, dlw_ref, dlb_ref = refs[13 + nw:18 + nw]
        su_ref, sd_ref = refs[18 + 2 * nw:20 + 2 * nw]
        comm_start, comm_finish = _chip_plan(comm_kinds, refs[13:13 + nw], refs[18 + nw:18 + 2 * nw],
                                             refs[20 + 2 * nw:])
        i = pl.program_id(0)
        first = i % tps == 0
        last = i % tps == tps - 1

        @pl.when(i == 0)
        def _():
            comm_start()
            dww_ref[...] = jnp.zeros_like(dww_ref)
            dwb_ref[...] = jnp.zeros_like(dwb_ref)
            dlw_ref[...] = jnp.zeros_like(dlw_ref)
            dlb_ref[...] = jnp.zeros_like(dlb_ref)

        su_ref[0, 0:HALO, :] = _f32(pa_ref[...]) * _sigmoid(_f32(pb_ref[...])) * jnp.where(first, 0.0, 1.0)
        su_ref[0, HALO:HALO + tm, :] = _f32(a_ref[...]) * _sigmoid(_f32(b_ref[...]))
        _make_shifts(su_ref, tm + HALO - SUB)

        def ln_back(u2c, gv, dy):
            mu = jnp.mean(u2c, axis=1, keepdims=True)
            xc = u2c - mu
            rs = lax.rsqrt(jnp.mean(xc * xc, axis=1, keepdims=True) + EPS)
            xh = xc * rs
            lw = lw_ref[...]
            u3 = xh * lw + lb_ref[...]
            s3 = _sigmoid(u3)
            sg = _sigmoid(gv)
            dgc = dy * (u3 * s3) * _dsilu(gv, sg)
            du3 = dy * gv * sg * _dsilu(u3, s3)
            dxh = du3 * lw
            du2 = rs * (dxh - jnp.mean(dxh, axis=1, keepdims=True)
                        - xh * jnp.mean(dxh * xh, axis=1, keepdims=True))
            return du2, dgc, du3, xh

        def tile_chunk(ci, carry):
            r0 = pl.multiple_of(ci * RC, RC)
            rows = pl.ds(r0, RC)
            du2, dgc, du3, xh = ln_back(u2_ref[rows, :], _f32(g_ref[rows, :]), _f32(dy_ref[rows, :]))
            sd_ref[0, rows, :] = du2
            dp_ref[rows, 2 * c:3 * c] = dgc.astype(BF16)
            dwb_ref[...] += fold(du2)
            dlw_ref[...] += fold(du3 * xh)
            dlb_ref[...] += fold(du3)
            return carry

        lax.fori_loop(0, tm // RC, tile_chunk, 0, unroll=2)
        live = jnp.where(last, 0.0, 1.0)
        for ci in range(HALO // RC):
            rows = slice(ci * RC, (ci + 1) * RC)
            du2, _, _, _ = ln_back(nu2_ref[rows, :], _f32(ng_ref[rows, :]), _f32(ndy_ref[rows, :]))
            sd_ref[0, tm + ci * RC:tm + (ci + 1) * RC, :] = du2 * live
        _make_shifts(sd_ref, tm + HALO - SUB)

        def tap_chunk(ci, carry):
            r0 = pl.multiple_of(ci * RC, RC)
            rows = pl.ds(r0, RC)
            du1 = _taps(w_ref, sd_ref, r0, CONF_K - 1, -1, jnp.zeros((SUB, c), F32))
            sb = _sigmoid(_f32(b_ref[rows, :]))
            dp_ref[rows, 0:c] = (du1 * sb).astype(BF16)
            dp_ref[rows, c:2 * c] = (du1 * _f32(a_ref[rows, :]) * sb * (1.0 - sb)).astype(BF16)
            du2 = sd_ref[0, rows, :]
            for k in range(CONF_K):
                dww_ref[k * SUB:(k + 1) * SUB, :] += fold(du2 * _shifted(su_ref, r0, HALO - CONF_K + 1 + k, RC))
            return carry

        lax.fori_loop(0, tm // RC, tap_chunk, 0, unroll=2)

        @pl.when(i == nsteps - 1)
        def _():
            comm_finish()

    def col(j):
        return pl.BlockSpec((tm, c), lambda i: (i, j))

    def prev(j):
        return pl.BlockSpec((HALO, c), lambda i: (jnp.maximum(i * hb - 1, 0), j))

    def nxt(j):
        return pl.BlockSpec((HALO, c), lambda i: (jnp.minimum((i + 1) * hb, nhb - 1), j))

    vec = pl.BlockSpec((1, c), lambda i: (0, 0))
    acc = pl.BlockSpec((SUB, c), lambda i: (0, 0))
    any_spec = pl.BlockSpec(memory_space=pl.ANY)
    return _pallas(
        body, name="conf_bwd", grid=(nsteps,),
        in_specs=[col(0), col(1), col(2), prev(0), prev(1), nxt(2), col(0), nxt(0), col(0), nxt(0),
                  pl.BlockSpec((HALO * SUB, c), lambda i: (0, 0)), vec, vec] + [any_spec] * nw,
        out_specs=[pl.BlockSpec((tm, 3 * c), lambda i: (i, 0)),
                   pl.BlockSpec((HALO * SUB, c), lambda i: (0, 0)), acc, acc, acc] + [any_spec] * nw,
        out_shape=[jax.ShapeDtypeStruct((t, 3 * c), BF16), jax.ShapeDtypeStruct((HALO * SUB, c), F32),
                   jax.ShapeDtypeStruct((SUB, c), F32), jax.ShapeDtypeStruct((SUB, c), F32),
                   jax.ShapeDtypeStruct((SUB, c), F32)]
        + [_landing_shape(kd, s) for kd, s in zip(comm_kinds, comm_srcs)],
        scratch_shapes=[pltpu.VMEM((SUB, tm + HALO, c), F32), pltpu.VMEM((SUB, tm + HALO, c), F32)] + _chip_sems(nw),
        compiler_params=_params(1),
    )(proj, proj, proj, proj, proj, proj, u2, u2, dycat, dycat, dw_w, ln_w, ln_b, *comm_srcs)


Q_COL = 3 * CONV_W // HEAD_DIM
K_COL = Q_COL + N_HEADS
V_COL = K_COL + N_HEADS
GA_COL = V_COL + N_HEADS


SBA_TQ = 256
SBA_WK = 4 * QB


def _sb_window(qs, kw, ws, limit, t0, carry):
    tq, wk = qs.shape[0], kw.shape[0]
    z = _dot(qs, kw, NT)
    sg = ws + lax.broadcasted_iota(jnp.int32, (tq, wk), 1)
    tg = t0 + lax.broadcasted_iota(jnp.int32, (tq, wk), 0)
    mask = sg < jnp.minimum(tg, limit)
    sp = jnp.log(1.0 + jnp.exp(-jnp.abs(z)))
    ls = jnp.minimum(z, 0.0) - sp
    lk = jnp.where(mask, ls - z, 0.0)
    jj = lax.broadcasted_iota(jnp.int32, (QB, QB), 0)
    ss = lax.broadcasted_iota(jnp.int32, (QB, QB), 1)
    ustrict = jnp.where(jj > ss, 1.0, 0.0).astype(BF16)
    laters = [None] * (wk // QB)
    for ch in reversed(range(wk // QB)):
        lkc = lk[:, ch * QB:(ch + 1) * QB]
        laters[ch] = carry + _dotx2(lkc, ustrict)
        carry = carry + jnp.sum(lkc, axis=1, keepdims=True)
    w = jnp.where(mask, jnp.exp(ls + jnp.concatenate(laters, axis=1)), 0.0)
    return mask, ls, w, carry


def _sba_fwd(proj, nb, seq, *, tq=SBA_TQ, wk=SBA_WK):
    t = proj.shape[0]
    wk = min(wk, seq)
    nq = seq // tq
    scale = HEAD_DIM ** -0.5

    def body(q_ref, k_ref, v_ref, g_ref, o_ref, y_ref):
        i = pl.program_id(2)
        t0 = i * tq
        qs = (_f32(q_ref[...]) * scale).astype(BF16)

        def window(ws, limit, carry, acc):
            ws = pl.multiple_of(ws, QB)
            _, _, w, carry = _sb_window(qs, k_ref[pl.ds(ws, wk), :], ws, limit, t0, carry)
            return carry, acc + _dot(w, v_ref[pl.ds(ws, wk), :])

        ws0 = jnp.maximum(t0 + tq - wk, 0)
        carry, acc = window(ws0, seq, jnp.zeros((tq, 1), F32), jnp.zeros((tq, HEAD_DIM), F32))

        def cond(st):
            return jnp.logical_and(st[0] > 0, jnp.max(st[1]) > NEG_CUT)

        def step(st):
            c2, a2 = window(jnp.maximum(st[0] - wk, 0), st[0], st[1], st[2])
            return jnp.maximum(st[0] - wk, 0), c2, a2

        _, _, acc = lax.while_loop(cond, step, (ws0, carry, acc))
        o_ref[...] = acc
        gv = _f32(g_ref[...])
        y_ref[...] = (acc * gv * _sigmoid(gv)).astype(BF16)

    def tile(c0):
        return pl.BlockSpec((tq, HEAD_DIM), lambda b, h, i: (b * nq + i, c0 + h))

    def whole(c0):
        return pl.BlockSpec((seq, HEAD_DIM), lambda b, h, i: (b, c0 + h))

    return _pallas(
        body, name="sba_fwd", grid=(nb, N_HEADS, nq),
        in_specs=[tile(Q_COL), whole(K_COL), whole(V_COL), tile(GA_COL)],
        out_specs=[tile(0), tile(0)],
        out_shape=[jax.ShapeDtypeStruct((t, ATT_W), F32), jax.ShapeDtypeStruct((t, ATT_W), BF16)],
        compiler_params=_params(3),
    )(proj, proj, proj, proj)


def _sba_bwd(proj, o, dycat, nb, seq, *, tq=SBA_TQ, wk=SBA_WK):
    t = proj.shape[0]
    wk = min(wk, seq)
    nq = seq // tq
    nwin = -(-seq // wk) + 1
    nch = wk // QB
    scale = HEAD_DIM ** -0.5

    def body(q_ref, k_ref, v_ref, g_ref, o_ref, dy_ref, dq_ref, dko_ref, dvo_ref, dg_ref, e_ref, sp_ref,
             dk_ref, dv_ref):
        i = pl.program_id(2)
        t0 = i * tq

        @pl.when(i == 0)
        def _():
            dk_ref[...] = jnp.zeros_like(dk_ref)
            dv_ref[...] = jnp.zeros_like(dv_ref)

        qs = (_f32(q_ref[...]) * scale).astype(BF16)
        gv = _f32(g_ref[...])
        sg = _sigmoid(gv)
        dy = _f32(dy_ref[...])
        do = (dy * gv * sg).astype(BF16)
        dg_ref[...] = (dy * o_ref[...] * _dsilu(gv, sg)).astype(BF16)

        def start_of(n):
            return pl.multiple_of(jnp.maximum(t0 + tq - (n + 1) * wk, 0), QB)

        def limit_of(n):
            return jnp.where(n == 0, seq, jnp.maximum(t0 + tq - n * wk, 0))

        def near(n, carry):
            ws = start_of(n)
            _, ls, w, carry = _sb_window(qs, k_ref[pl.ds(ws, wk), :], ws, limit_of(n), t0, carry)
            e_ref[n] = w * _dot(do, v_ref[pl.ds(ws, wk), :], NT)
            sp_ref[n] = jnp.exp(ls)
            dv_ref[pl.ds(ws, wk), :] += _dot(w, do, TN)
            return carry

        carry = near(0, jnp.zeros((tq, 1), F32))

        def cond(st):
            return jnp.logical_and(start_of(st[0] - 1) > 0, jnp.max(st[1]) > NEG_CUT)

        def step(st):
            return st[0] + 1, near(st[0], st[1])

        nvis, _ = lax.while_loop(cond, step, (1, carry))

        jj = lax.broadcasted_iota(jnp.int32, (QB, QB), 0)
        ss = lax.broadcasted_iota(jnp.int32, (QB, QB), 1)
        lstrict = jnp.where(jj < ss, 1.0, 0.0).astype(BF16)

        def far(r, st):
            pre, dq = st
            n = nvis - 1 - r
            ws = start_of(n)
            e = e_ref[n]
            spn = sp_ref[n]
            gs = []
            for ch in range(nch):
                ec = e[:, ch * QB:(ch + 1) * QB]
                gs.append(pre + _dotx2(ec, lstrict))
                pre = pre + jnp.sum(ec, axis=1, keepdims=True)
            sgl = ws + lax.broadcasted_iota(jnp.int32, (tq, wk), 1)
            tgl = t0 + lax.broadcasted_iota(jnp.int32, (tq, wk), 0)
            mask = sgl < jnp.minimum(tgl, limit_of(n))
            dz = jnp.where(mask, e * (1.0 - spn) - jnp.concatenate(gs, axis=1) * spn, 0.0).astype(BF16)
            dk_ref[pl.ds(ws, wk), :] += _dot(dz, qs, TN)
            return pre, dq + _dot(dz, k_ref[pl.ds(ws, wk), :])

        _, dq = lax.fori_loop(0, nvis, far, (jnp.zeros((tq, 1), F32), jnp.zeros((tq, HEAD_DIM), F32)))
        dq_ref[...] = (dq * scale).astype(BF16)

        @pl.when(i == nq - 1)
        def _():
            dko_ref[...] = dk_ref[...].astype(BF16)
            dvo_ref[...] = dv_ref[...].astype(BF16)

    def tile(c0):
        return pl.BlockSpec((tq, HEAD_DIM), lambda b, h, i: (b * nq + i, c0 + h))

    def whole(c0):
        return pl.BlockSpec((seq, HEAD_DIM), lambda b, h, i: (b, c0 + h))

    return _pallas(
        body, name="sba_bwd", grid=(nb, N_HEADS, nq),
        in_specs=[tile(Q_COL), whole(K_COL), whole(V_COL), tile(GA_COL), tile(0),
                  tile(CONV_W // HEAD_DIM)],
        out_specs=[tile(0), whole(0), whole(0), tile(0)],
        out_shape=[jax.ShapeDtypeStruct((t, ATT_W), BF16)] * 4,
        scratch_shapes=[pltpu.VMEM((nwin, tq, wk), F32), pltpu.VMEM((nwin, tq, wk), F32),
                        pltpu.VMEM((seq, HEAD_DIM), F32), pltpu.VMEM((seq, HEAD_DIM), F32)],
        compiler_params=_params(3),
    )(proj, proj, proj, proj, o, dycat)


CT = 512
PH = 8
XRC = 32
X_SHIFTS = tuple(s for s in range(PH - SSM_K + 1, PH))
D_SHIFTS = tuple(range(1, SSM_K))
XBC_BLK = D_INNER // CT


def _softplus(x):
    return jnp.maximum(x, 0.0) + jnp.log(1.0 + jnp.exp(-jnp.abs(x)))


def _dt_fwd(proj, dt_bias, *, tm=512):
    t = proj.shape[0]

    def body(p_ref, b_ref, o_ref):
        o_ref[...] = _softplus(p_ref[...] + b_ref[...])

    return _pallas(
        body, name="dt_fwd", grid=(t // tm,),
        in_specs=[pl.BlockSpec((tm, LANE), lambda i: (i, 0)), pl.BlockSpec((1, LANE), lambda i: (0, 0))],
        out_specs=pl.BlockSpec((tm, LANE), lambda i: (i, 0)),
        out_shape=jax.ShapeDtypeStruct((t, LANE), F32), compiler_params=_params(1),
    )(proj, dt_bias)


def _dt_bwd(proj, dt_bias, ddt, dproj, *, tm=512):
    t = proj.shape[0]
    wide = IN_ODD_PAD - D_INNER - XBC

    def body(p_ref, b_ref, d_ref, dp_any, o_ref, db_ref):
        i = pl.program_id(0)
        lanes = lax.broadcasted_iota(jnp.int32, (tm, LANE), 1)
        dr = jnp.where(lanes < SSM_H, d_ref[...] * _sigmoid(p_ref[...] + b_ref[...]), 0.0)
        o_ref[:, 0:LANE] = dr.astype(BF16)
        o_ref[:, LANE:wide] = jnp.zeros((tm, wide - LANE), BF16)

        @pl.when(i == 0)
        def _():
            db_ref[...] = jnp.zeros_like(db_ref)

        db_ref[...] += jnp.sum(dr, axis=0, keepdims=True)

    vec = pl.BlockSpec((1, LANE), lambda i: (0, 0))
    row = pl.BlockSpec((tm, LANE), lambda i: (i, 0))
    return _pallas(
        body, name="dt_bwd", grid=(t // tm,),
        in_specs=[pl.BlockSpec((tm, LANE), lambda i: (i, 0)), vec, row, pl.BlockSpec(memory_space=pl.ANY)],
        out_specs=[pl.BlockSpec((tm, wide), lambda i: (i, (D_INNER + XBC) // wide)), vec],
        out_shape=[jax.ShapeDtypeStruct(dproj.shape, dproj.dtype), jax.ShapeDtypeStruct((1, LANE), F32)],
        input_output_aliases={3: 0}, compiler_params=_params(1),
    )(proj, dt_bias, ddt, dproj)


def _xconv_fwd(proj, conv_w, conv_b, seq, *, tm=512):
    t = proj.shape[0]
    tps = seq // tm
    hb = tm // PH

    def body(x_ref, h_ref, w_ref, b_ref, o_ref, sh_ref):
        i = pl.program_id(1)
        sh_ref[0, 0:PH, :] = _f32(h_ref[...]) * jnp.where(i % tps == 0, 0.0, 1.0)
        sh_ref[0, PH:PH + tm, :] = _f32(x_ref[...])
        _make_shifts(sh_ref, tm, X_SHIFTS)

        def chunk(ci, carry):
            r0 = pl.multiple_of(ci * XRC, XRC)
            acc = jnp.zeros((XRC, CT), F32) + b_ref[...]
            for k in range(SSM_K):
                acc = acc + w_ref[k:k + 1, :] * _shifted(sh_ref, r0, PH - SSM_K + 1 + k, XRC)
            o_ref[pl.ds(r0, XRC), :] = acc * _sigmoid(acc)
            return carry

        lax.fori_loop(0, tm // XRC, chunk, 0, unroll=8)

    return _pallas(
        body, name="xconv_fwd", grid=(XBC // CT, t // tm),
        in_specs=[pl.BlockSpec((tm, CT), lambda j, i: (i, XBC_BLK + j)),
                  pl.BlockSpec((PH, CT), lambda j, i: (jnp.maximum(i * hb - 1, 0), XBC_BLK + j)),
                  pl.BlockSpec((PH, CT), lambda j, i: (0, j)),
                  pl.BlockSpec((1, CT), lambda j, i: (0, j))],
        out_specs=pl.BlockSpec((tm, CT), lambda j, i: (i, j)),
        out_shape=jax.ShapeDtypeStruct((t, XBC), F32),
        scratch_shapes=[pltpu.VMEM((SUB, tm + PH, CT), F32)], compiler_params=_params(2),
    )(proj, proj, conv_w, conv_b)


def _xconv_bwd(proj, dxc, conv_w, conv_b, dproj, seq, *, tm=512):
    t = proj.shape[0]
    tps = seq // tm
    hb = tm // PH
    nhb = t // PH
    te = tm + PH

    def fold(v):
        out = v[0:SUB]
        for q in range(1, v.shape[0] // SUB):
            out = out + v[q * SUB:(q + 1) * SUB]
        return out

    def body(x_ref, p_ref, n_ref, d_ref, nd_ref, w_ref, b_ref, dp_any, dx_ref, dw_ref, db_ref, sx_ref, sd_ref):
        i = pl.program_id(1)
        first = i % tps == 0
        last = i % tps == tps - 1

        @pl.when(i == 0)
        def _():
            dw_ref[...] = jnp.zeros_like(dw_ref)
            db_ref[...] = jnp.zeros_like(db_ref)

        sx_ref[0, 0:PH, :] = _f32(p_ref[...]) * jnp.where(first, 0.0, 1.0)
        sx_ref[0, PH:PH + tm, :] = _f32(x_ref[...])
        sx_ref[0, PH + tm:PH + te, :] = _f32(n_ref[...])
        _make_shifts(sx_ref, te, X_SHIFTS)

        def dv_of(r0, rows, dy):
            acc = jnp.zeros((rows, CT), F32) + b_ref[...]
            for k in range(SSM_K):
                acc = acc + w_ref[k:k + 1, :] * _shifted(sx_ref, r0, PH - SSM_K + 1 + k, rows)
            return dy * _dsilu(acc, _sigmoid(acc))

        def dv_chunk(ci, carry):
            r0 = pl.multiple_of(ci * XRC, XRC)
            dv = dv_of(r0, XRC, d_ref[pl.ds(r0, XRC), :])
            sd_ref[0, pl.ds(r0, XRC), :] = dv
            db_ref[...] += fold(dv)
            return carry

        lax.fori_loop(0, tm // XRC, dv_chunk, 0, unroll=8)
        sd_ref[0, tm:te, :] = dv_of(tm, PH, nd_ref[...]) * jnp.where(last, 0.0, 1.0)
        _make_shifts(sd_ref, tm, D_SHIFTS)

        def tap_chunk(ci, carry):
            r0 = pl.multiple_of(ci * XRC, XRC)
            dx = jnp.zeros((XRC, CT), F32)
            for k in range(SSM_K):
                dx = dx + w_ref[k:k + 1, :] * _shifted(sd_ref, r0, SSM_K - 1 - k, XRC)
            dx_ref[pl.ds(r0, XRC), :] = dx.astype(BF16)
            dv = sd_ref[0, pl.ds(r0, XRC), :]
            for k in range(SSM_K):
                dw_ref[k * SUB:(k + 1) * SUB, :] += fold(dv * _shifted(sx_ref, r0, PH - SSM_K + 1 + k, XRC))
            return carry

        lax.fori_loop(0, tm // XRC, tap_chunk, 0, unroll=8)

    return _pallas(
        body, name="xconv_bwd", grid=(XBC // CT, t // tm),
        in_specs=[pl.BlockSpec((tm, CT), lambda j, i: (i, XBC_BLK + j)),
                  pl.BlockSpec((PH, CT), lambda j, i: (jnp.maximum(i * hb - 1, 0), XBC_BLK + j)),
                  pl.BlockSpec((PH, CT), lambda j, i: (jnp.minimum((i + 1) * hb, nhb - 1), XBC_BLK + j)),
                  pl.BlockSpec((tm, CT), lambda j, i: (i, j)),
                  pl.BlockSpec((PH, CT), lambda j, i: (jnp.minimum((i + 1) * hb, nhb - 1), j)),
                  pl.BlockSpec((PH, CT), lambda j, i: (0, j)),
                  pl.BlockSpec((1, CT), lambda j, i: (0, j)),
                  pl.BlockSpec(memory_space=pl.ANY)],
        out_specs=[pl.BlockSpec((tm, CT), lambda j, i: (i, XBC_BLK + j)),
                   pl.BlockSpec((PH * SUB, CT), lambda j, i: (0, j)),
                   pl.BlockSpec((SUB, CT), lambda j, i: (0, j))],
        out_shape=[jax.ShapeDtypeStruct(dproj.shape, dproj.dtype), jax.ShapeDtypeStruct((PH * SUB, XBC), F32),
                   jax.ShapeDtypeStruct((SUB, XBC), F32)],
        scratch_shapes=[pltpu.VMEM((SUB, tm + 2 * PH, CT), F32), pltpu.VMEM((SUB, te, CT), F32)],
        input_output_aliases={7: 0}, compiler_params=_params(2),
    )(proj, proj, proj, dxc, dxc, conv_w, conv_b, dproj)


def _ssd_common(xbc, dt, alog, ex):
    L = CHUNK
    a = -jnp.exp(alog)
    la = dt * a
    li = lax.broadcasted_iota(jnp.int32, (L, L), 0)
    si = lax.broadcasted_iota(jnp.int32, (L, L), 1)
    lower = si <= li
    tri = jnp.where(lower, 1.0, 0.0).astype(BF16)
    cs = _xdot(tri, la)
    cst = _dotx(la, tri, (((0,), (1,)), ((), ())))
    csl = cs[L - 1:L, :]
    ecs_x = _dotx2(jnp.exp(cs)[:, 0:SSM_H], ex)
    tail_x = _dotx2(jnp.exp(csl - cs)[:, 0:SSM_H], ex)
    dt_x = _dotx2(dt[:, 0:SSM_H], ex)
    return a, la, lower, tri, cs, cst, ecs_x, tail_x, dt_x


def _ssd_fwd(xbc_c, dt, a_log, ex, nb, seq):
    t = xbc_c.shape[0]
    L = CHUNK
    nc = seq // L
    GW = SSM_R * SSM_P

    def body(x_ref, dt_ref, al_ref, ex_ref, y_ref, st_ref, state):
        c = pl.program_id(1)

        @pl.when(c == 0)
        def _():
            state[...] = jnp.zeros_like(state)

        st_ref[0] = state[...]
        xbc = x_ref[...]
        _, _, lower, _, cs, cst, ecs_x, tail_x, dt_x = _ssd_common(xbc, dt_ref[...], al_ref[...], ex_ref[...])
        xd = xbc[:, 0:D_INNER] * dt_x
        xdb = xd.astype(BF16)
        xt = (xd * tail_x).astype(BF16)
        el_x = ecs_x[L - 1:L, :]
        for g in range(SSM_G):
            bg = xbc[:, D_INNER + g * SSM_N:D_INNER + (g + 1) * SSM_N].astype(BF16)
            cg = xbc[:, D_INNER + (SSM_G + g) * SSM_N:D_INNER + (SSM_G + g + 1) * SSM_N].astype(BF16)
            cb = _dot(cg, bg, NT)
            sg = state[:, g * GW:(g + 1) * GW]
            ys = _dot(cg, sg) * ecs_x[:, g * GW:(g + 1) * GW]
            for r in range(SSM_R):
                h = g * SSM_R + r
                seg = cs[:, h:h + 1] - cst[h:h + 1, :]
                dec = jnp.exp(jnp.where(lower, seg, -1e30))
                yh = _dot(cb * dec, xdb[:, h * SSM_P:(h + 1) * SSM_P])
                y_ref[:, h * SSM_P:(h + 1) * SSM_P] = yh + ys[:, r * SSM_P:(r + 1) * SSM_P]
            state[:, g * GW:(g + 1) * GW] = sg * el_x[:, g * GW:(g + 1) * GW] + _dot(bg, xt[:, g * GW:(g + 1) * GW], TN)

    return _pallas(
        body, name="ssd_fwd", grid=(nb, nc),
        in_specs=[pl.BlockSpec((L, XBC), lambda b, c: (b * nc + c, 0)),
                  pl.BlockSpec((L, LANE), lambda b, c: (b * nc + c, 0)),
                  pl.BlockSpec((1, LANE), lambda b, c: (0, 0)),
                  pl.BlockSpec((SSM_H, D_INNER), lambda b, c: (0, 0))],
        out_specs=[pl.BlockSpec((L, D_INNER), lambda b, c: (b * nc + c, 0)),
                   pl.BlockSpec((1, SSM_N, D_INNER), lambda b, c: (b * nc + c, 0, 0))],
        out_shape=[jax.ShapeDtypeStruct((t, D_INNER), F32),
                   jax.ShapeDtypeStruct((nb * nc, SSM_N, D_INNER), F32)],
        scratch_shapes=[pltpu.VMEM((SSM_N, D_INNER), F32)], compiler_params=_params(2),
    )(xbc_c, dt, a_log, ex)


def _ssd_bwd(xbc_c, dt, a_log, ex, ext, states, dy, d_x, nb, seq):
    t = xbc_c.shape[0]
    L = CHUNK
    nc = seq // L
    GW = SSM_R * SSM_P

    def body(x_ref, dt_ref, al_ref, ex_ref, ext_ref, st_ref, dy_ref, sk_ref, dx_ref, ddt_ref, da_ref,
             dstate, dxd, yd, lastv):
        b = pl.program_id(0)
        c = pl.program_id(1)

        @pl.when(c == 0)
        def _():
            dstate[...] = jnp.zeros_like(dstate)

        @pl.when(jnp.logical_and(b == 0, c == 0))
        def _():
            da_ref[...] = jnp.zeros_like(da_ref)

        xbc = x_ref[...]
        dtv = dt_ref[...]
        ex_t = ext_ref[...]
        a, la, lower, tri, cs, cst, ecs_x, tail_x, dt_x = _ssd_common(xbc, dtv, al_ref[...], ex_ref[...])
        xs = xbc[:, 0:D_INNER]
        xd = xs * dt_x
        xdb = xd.astype(BF16)
        dyv = dy_ref[...]
        dyb = dyv.astype(BF16)
        dys = dyv * ecs_x
        xt = xd * tail_x
        el_x = ecs_x[L - 1:L, :]
        lane = lax.broadcasted_iota(jnp.int32, (L, LANE), 1)
        sub = lax.broadcasted_iota(jnp.int32, (LANE, L), 0)
        row_part = jnp.zeros((L, LANE), F32)
        col_part = jnp.zeros((LANE, L), F32)
        for g in range(SSM_G):
            gs = slice(g * GW, (g + 1) * GW)
            bcol = slice(D_INNER + g * SSM_N, D_INNER + (g + 1) * SSM_N)
            ccol = slice(D_INNER + (SSM_G + g) * SSM_N, D_INNER + (SSM_G + g + 1) * SSM_N)
            bg = xbc[:, bcol].astype(BF16)
            cg = xbc[:, ccol].astype(BF16)
            cb = _dot(cg, bg, NT)
            sg = st_ref[0, :, gs]
            dsg = dstate[:, gs]
            dc = _dot(dys[:, gs], sg, NT)
            db = _dot(xt[:, gs], dsg, NT)
            dx_state = tail_x[:, gs] * _dot(bg, dsg)
            tail_part = xd[:, gs] * dx_state
            yd[:, gs] = dys[:, gs] * _dot(cg, sg) - tail_part
            last = jnp.sum(tail_part, axis=0, keepdims=True) + el_x[:, gs] * jnp.sum(dsg * sg, axis=0, keepdims=True)
            lastv[:, gs] = jnp.broadcast_to(last, (8, GW))
            dcb = jnp.zeros((L, L), F32)
            for r in range(SSM_R):
                h = g * SSM_R + r
                hs = slice(h * SSM_P, (h + 1) * SSM_P)
                seg = cs[:, h:h + 1] - cst[h:h + 1, :]
                dec = jnp.exp(jnp.where(lower, seg, -1e30))
                m = cb * dec
                dm = _dot(dyb[:, hs], xdb[:, hs], NT)
                dcb = dcb + dm * dec
                e = dm * m
                row_part = row_part + jnp.where(lane == h, jnp.sum(e, axis=1, keepdims=True), 0.0)
                col_part = col_part + jnp.where(sub == h, jnp.sum(e, axis=0, keepdims=True), 0.0)
                dxd[:, hs] = _dot(m, dyb[:, hs], TN) + dx_state[:, r * SSM_P:(r + 1) * SSM_P]
            dx_ref[:, bcol] = db + _dot(dcb, cg, TN)
            dx_ref[:, ccol] = dc + _dot(dcb, bg)
            dstate[:, gs] = dsg * el_x[:, gs] + _dot(cg, dys[:, gs], TN)
        dxv = dxd[...]
        dx_ref[:, 0:D_INNER] = dxv * dt_x + dyv * sk_ref[...]
        ddt_x = _dotx(dxv * xs, ex_t)
        yst = _dotx(yd[...], ex_t)
        lst = _dotx(lastv[...], ex_t)[0:1, :]
        rows = lax.broadcasted_iota(jnp.int32, (L, LANE), 0)
        dcs = row_part - col_part.T + yst + jnp.where(rows == L - 1, lst, 0.0)
        li = lax.broadcasted_iota(jnp.int32, (L, L), 0)
        si = lax.broadcasted_iota(jnp.int32, (L, L), 1)
        upper = jnp.where(si >= li, 1.0, 0.0).astype(BF16)
        dla = _xdot(upper, dcs)
        ddt_ref[...] = dla * a + ddt_x
        da_ref[...] += jnp.sum(dla * dtv, axis=0, keepdims=True)

    def row(w):
        return pl.BlockSpec((L, w), lambda b, c: (b * nc + nc - 1 - c, 0))

    return _pallas(
        body, name="ssd_bwd", grid=(nb, nc),
        in_specs=[row(XBC), row(LANE), pl.BlockSpec((1, LANE), lambda b, c: (0, 0)),
                  pl.BlockSpec((SSM_H, D_INNER), lambda b, c: (0, 0)),
                  pl.BlockSpec((D_INNER, LANE), lambda b, c: (0, 0)),
                  pl.BlockSpec((1, SSM_N, D_INNER), lambda b, c: (b * nc + nc - 1 - c, 0, 0)),
                  row(D_INNER), pl.BlockSpec((1, D_INNER), lambda b, c: (0, 0))],
        out_specs=[row(XBC), row(LANE), pl.BlockSpec((1, LANE), lambda b, c: (0, 0))],
        out_shape=[jax.ShapeDtypeStruct((t, XBC), F32), jax.ShapeDtypeStruct((t, LANE), F32),
                   jax.ShapeDtypeStruct((1, LANE), F32)],
        scratch_shapes=[pltpu.VMEM((SSM_N, D_INNER), F32), pltpu.VMEM((L, D_INNER), F32),
                        pltpu.VMEM((L, D_INNER), F32), pltpu.VMEM((8, D_INNER), F32)],
        compiler_params=_params(2),
    )(xbc_c, dt, a_log, ex, ext, states, dy, d_x)


def _group_rms(y2):
    gw = D_INNER // SSM_G
    parts = []
    for g in range(SSM_G):
        v = y2[:, g * gw:(g + 1) * gw]
        r = lax.rsqrt(jnp.mean(v * v, axis=1, keepdims=True) + EPS)
        parts.append(jnp.broadcast_to(r, v.shape))
    return jnp.concatenate(parts, axis=1)


def _gate_fwd(y, xbc_c, proj, d_x, gn_w, *, tm=256):
    t = y.shape[0]

    def body(y_ref, x_ref, z_ref, d_ref, w_ref, o_ref):
        y1 = y_ref[...] + d_ref[...] * x_ref[...]
        zv = _f32(z_ref[...])
        y2 = y1 * zv * _sigmoid(zv)
        o_ref[...] = (y2 * _group_rms(y2) * w_ref[...]).astype(BF16)

    row = pl.BlockSpec((tm, D_INNER), lambda i: (i, 0))
    vec = pl.BlockSpec((1, D_INNER), lambda i: (0, 0))
    return _pallas(
        body, name="gate_fwd", grid=(t // tm,), in_specs=[row, row, row, vec, vec], out_specs=row,
        out_shape=jax.ShapeDtypeStruct((t, D_INNER), BF16), compiler_params=_params(1),
    )(y, xbc_c, proj, d_x, gn_w)


def _gate_bwd(dyg, y, xbc_c, proj, d_x, gn_w, *, tm=256):
    t = y.shape[0]
    gw = D_INNER // SSM_G

    def body(dg_ref, y_ref, x_ref, z_ref, d_ref, w_ref, dy_ref, dz_ref, dw_ref, dd_ref):
        i = pl.program_id(0)
        xv = x_ref[...]
        dxv = d_ref[...]
        y1 = y_ref[...] + dxv * xv
        zv = _f32(z_ref[...])
        sz = _sigmoid(zv)
        y2 = y1 * zv * sz
        rr = _group_rms(y2)
        xh = y2 * rr
        dg = _f32(dg_ref[...])
        gq = dg * w_ref[...]
        prod = gq * xh
        means = []
        for g in range(SSM_G):
            mg = jnp.mean(prod[:, g * gw:(g + 1) * gw], axis=1, keepdims=True)
            means.append(jnp.broadcast_to(mg, (tm, gw)))
        dy2 = rr * (gq - xh * jnp.concatenate(means, axis=1))
        dy1 = dy2 * zv * sz
        dy_ref[...] = dy1
        dz_ref[...] = (dy2 * y1 * _dsilu(zv, sz)).astype(BF16)

        @pl.when(i == 0)
        def _():
            dw_ref[...] = jnp.zeros_like(dw_ref)
            dd_ref[...] = jnp.zeros_like(dd_ref)

        dw_ref[...] += jnp.sum(dg * xh, axis=0, keepdims=True)
        dd_ref[...] += jnp.sum(dy1 * xv, axis=0, keepdims=True)

    row = pl.BlockSpec((tm, D_INNER), lambda i: (i, 0))
    vec = pl.BlockSpec((1, D_INNER), lambda i: (0, 0))
    return _pallas(
        body, name="gate_bwd", grid=(t // tm,), in_specs=[row, row, row, row, vec, vec],
        out_specs=[row, row, vec, vec],
        out_shape=[jax.ShapeDtypeStruct((t, D_INNER), F32),
                   jax.ShapeDtypeStruct((t, IN_ODD_PAD), BF16), jax.ShapeDtypeStruct((1, D_INNER), F32),
                   jax.ShapeDtypeStruct((1, D_INNER), F32)],
        compiler_params=_params(1),
    )(dyg, y, xbc_c, proj, d_x, gn_w)


ANY = pl.BlockSpec(memory_space=pl.ANY)


def _remote(src, dst, sems, k, to):
    send_sems, recv_sems = sems
    return pltpu.make_async_remote_copy(src_ref=src, dst_ref=dst, send_sem=send_sems.at[k], recv_sem=recv_sems.at[k],
                                        device_id=to, device_id_type=MESH)


NCHIP = 4


def _gathered_shape(kind, shard):
    r, n = shard.shape
    shape = {"cols": (r, NCHIP * n), "slab": (NCHIP, r, n), "rows": (NCHIP * r, n)}[kind]
    return jax.ShapeDtypeStruct(shape, shard.dtype)


def _gather_plan(kinds, shards, outs, sems, small=None):
    ici_s, ici_r, d2d_s, d2d_r = sems
    nw = len(shards)
    per = nw + (small is not None)

    def place():
        x, y, c = lax.axis_index("x"), lax.axis_index("y"), lax.axis_index("c")
        return 2 * x + y, c, (x, y, 1 - c), [(1 - x, y), (x, 1 - y), (1 - x, 1 - y)]

    def region(j, chip, half):
        r, n = shards[j].shape
        h = r // 2
        if kinds[j] == "cols":
            return outs[j].at[pl.ds(half * h, h), pl.ds(pl.multiple_of(chip * n, LANE), n)]
        if kinds[j] == "slab":
            return outs[j].at[chip, pl.ds(half * h, h), :]
        return outs[j].at[pl.ds(chip * r + half * h, h), :]

    def my_sends(me, c, peers):
        cps = []
        for k, (px, py) in enumerate(peers):
            for j in range(nw):
                h = shards[j].shape[0] // 2
                cps.append(_remote(shards[j].at[pl.ds(c * h, h), :], region(j, me, c), (ici_s, ici_r), per * k + j, (px, py, c)))
            if small is not None:
                cps.append(_remote(small[0], small[1].at[me], (ici_s, ici_r), per * k + nw, (px, py, c)))
        return cps

    def start():
        me, c, _, peers = place()
        for cp in my_sends(me, c, peers):
            cp.start()

    def finish():
        me, c, sib, peers = place()
        fwds = []
        for k, (px, py) in enumerate(peers):
            q = 2 * px + py
            for j in range(nw):
                d = region(j, q, c)
                _remote(d, d, (ici_s, ici_r), per * k + j, (px, py, c)).wait_recv()
                fwds.append(_remote(d, d, (d2d_s, d2d_r), nw * k + j, sib))
                fwds[-1].start()
            if small is not None:
                _remote(small[0], small[1].at[q], (ici_s, ici_r), per * k + nw, (px, py, c)).wait_recv()
        for k, (px, py) in enumerate(peers):
            for j in range(nw):
                d = region(j, 2 * px + py, 1 - c)
                _remote(d, d, (d2d_s, d2d_r), nw * k + j, sib).wait_recv()
        for cp in my_sends(me, c, peers) + fwds:
            cp.wait_send()

    return start, finish


def _gather_sems(nw, with_small):
    n_ici = 3 * (nw + with_small)
    return [pltpu.SemaphoreType.DMA((n_ici,)), pltpu.SemaphoreType.DMA((n_ici,)),
            pltpu.SemaphoreType.DMA((3 * nw,)), pltpu.SemaphoreType.DMA((3 * nw,))]


def _gather_shards(kinds, shards, small):
    nw = len(shards)

    def body(*refs):
        ins, sm, outs, osm, sems = refs[:nw], refs[nw], refs[nw + 1:2 * nw + 1], refs[2 * nw + 1], refs[2 * nw + 2:]
        start, finish = _gather_plan(kinds, ins, outs, sems, small=(sm, osm))
        start()
        finish()

    return _pallas(
        body, name="gather_shards", in_specs=[ANY] * (nw + 1), out_specs=[ANY] * (nw + 1),
        out_shape=[_gathered_shape(kd, s) for kd, s in zip(kinds, shards)]
        + [jax.ShapeDtypeStruct((NCHIP,) + small.shape, small.dtype)],
        scratch_shapes=_gather_sems(nw, 1),
    )(*shards, small)


def _place_cols(full, shard, *, name, tr=256):
    r, n = shard.shape

    def body(p_ref, full_any, s_ref, o_ref):
        o_ref[...] = s_ref[...]

    return _pallas(
        body, name=name,
        grid_spec=pltpu.PrefetchScalarGridSpec(
            num_scalar_prefetch=1, grid=(r // tr,),
            in_specs=[pl.BlockSpec(memory_space=pl.ANY), pl.BlockSpec((tr, n), lambda i, p: (i, 0))],
            out_specs=pl.BlockSpec((tr, n), lambda i, p: (i, p[0]))),
        out_shape=jax.ShapeDtypeStruct(full.shape, full.dtype), input_output_aliases={1: 0},
        compiler_params=_params(1),
    )(_chip_index(), full, shard)


def _matmul_with_gather(a, b, kinds, shards, *, out_dtype, bm, bn, name):
    (m, k), n = a.shape, b.shape[1]
    nw = len(shards)
    nj, ni = n // bn, m // bm

    def body(*refs):
        a_ref, b_ref, ins, o_ref = refs[0], refs[1], refs[2:2 + nw], refs[2 + nw]
        outs, sems = refs[3 + nw:3 + 2 * nw], refs[3 + 2 * nw:]
        start, finish = _gather_plan(kinds, ins, outs, sems)
        j, i = pl.program_id(0), pl.program_id(1)

        @pl.when(jnp.logical_and(j == 0, i == 0))
        def _():
            start()

        o_ref[...] = _dot(a_ref[...], b_ref[...]).astype(out_dtype)

        @pl.when(jnp.logical_and(j == nj - 1, i == ni - 1))
        def _():
            finish()

    return _pallas(
        body, name=name, grid=(nj, ni),
        in_specs=[pl.BlockSpec((bm, k), lambda j, i: (i, 0)), pl.BlockSpec((k, bn), lambda j, i: (0, j))] + [ANY] * nw,
        out_specs=[pl.BlockSpec((bm, bn), lambda j, i: (i, j))] + [ANY] * nw,
        out_shape=[jax.ShapeDtypeStruct((m, n), out_dtype)] + [_gathered_shape(kd, s) for kd, s in zip(kinds, shards)],
        scratch_shapes=_gather_sems(nw, 0), compiler_params=_params(2),
    )(a, b, *shards)


def _other_half(a, c):
    axis = a.ndim - 2
    h = a.shape[axis] // 2
    rows = pl.ds(pl.multiple_of((1 - c) * h, 8), h)
    return a.at[rows, :] if a.ndim == 2 else a.at[:, rows, :]


def _half_shape(a):
    axis = a.ndim - 2
    return jax.ShapeDtypeStruct(a.shape[:axis] + (a.shape[axis] // 2,) + a.shape[axis + 1:], a.dtype)


def _pair_swap(bigs, *, name):
    nb = len(bigs)

    def body(*refs):
        ins, outs, send_sems, recv_sems = refs[:nb], refs[nb:2 * nb], refs[2 * nb], refs[2 * nb + 1]
        x, y, c = lax.axis_index("x"), lax.axis_index("y"), lax.axis_index("c")
        pair = [_remote(_other_half(a, c), q, (send_sems, recv_sems), j, (x, y, 1 - c))
                for j, (a, q) in enumerate(zip(ins, outs))]
        for cp in pair:
            cp.start()
        for cp in pair:
            cp.wait()

    return _pallas(
        body, name=name, in_specs=[ANY] * nb, out_specs=[ANY] * nb, out_shape=[_half_shape(a) for a in bigs],
        scratch_shapes=[pltpu.SemaphoreType.DMA((nb,)), pltpu.SemaphoreType.DMA((nb,))],
    )(*bigs)


def _pair_exchange(bigs, gsmall, grep):
    nb = len(bigs)

    def body(*refs):
        ins, sm, rp = refs[:nb], refs[nb], refs[nb + 1]
        outs, osm, orp = refs[nb + 2:2 * nb + 2], refs[2 * nb + 2], refs[2 * nb + 3]
        pair_s, pair_r, send_sems, recv_sems, local_sems = refs[2 * nb + 4:]
        x, y, c = lax.axis_index("x"), lax.axis_index("y"), lax.axis_index("c")
        me = 4 * x + 2 * y + c
        chip = 2 * x + y
        sib = (x, y, 1 - c)
        pair = [_remote(_other_half(a, c), q, (pair_s, pair_r), j, sib) for j, (a, q) in enumerate(zip(ins, outs))]
        for cp in pair:
            cp.start()
        own = [pltpu.make_async_copy(sm.at[chip], osm.at[me], local_sems.at[0]),
               pltpu.make_async_copy(rp, orp.at[me], local_sems.at[1])]
        for cp in own:
            cp.start()
        peers = []
        for k in range(7):
            fx, fy, fc = ((k + 1) >> 2) & 1, ((k + 1) >> 1) & 1, (k + 1) & 1
            peers.append((1 - x if fx else x, 1 - y if fy else y, 1 - c if fc else c))
        sends = []
        for k, (px, py, pc) in enumerate(peers):
            sends.append(_remote(sm.at[2 * px + py], osm.at[me], (send_sems, recv_sems), 2 * k, (px, py, pc)))
            sends.append(_remote(rp, orp.at[me], (send_sems, recv_sems), 2 * k + 1, (px, py, pc)))
        for cp in sends:
            cp.start()
        for k, (px, py, pc) in enumerate(peers):
            slot = 4 * px + 2 * py + pc
            _remote(sm.at[chip], osm.at[slot], (send_sems, recv_sems), 2 * k, (px, py, pc)).wait_recv()
            _remote(rp, orp.at[slot], (send_sems, recv_sems), 2 * k + 1, (px, py, pc)).wait_recv()
        for cp in pair:
            cp.wait_recv()
        for cp in pair + sends:
            cp.wait_send()
        for cp in own:
            cp.wait()

    return _pallas(
        body, name="pair_exchange", in_specs=[ANY] * (nb + 2), out_specs=[ANY] * (nb + 2),
        out_shape=[_half_shape(a) for a in bigs]
        + [jax.ShapeDtypeStruct((8,) + gsmall.shape[1:], F32), jax.ShapeDtypeStruct((8,) + grep.shape, F32)],
        scratch_shapes=[pltpu.SemaphoreType.DMA((max(nb, 1),)), pltpu.SemaphoreType.DMA((max(nb, 1),)),
                        pltpu.SemaphoreType.DMA((14,)), pltpu.SemaphoreType.DMA((14,)),
                        pltpu.SemaphoreType.DMA((2,))],
    )(*bigs, gsmall, grep)


def _core_index():
    return lax.axis_index("c").astype(jnp.int32).reshape(1)


def _half_add(full, other, *, axis, block, name):
    nd = full.ndim
    nblk = other.shape[axis] // block[axis]
    grid = tuple(other.shape[d] // block[d] for d in range(nd))

    def body(c_ref, f_ref, o_ref, out_ref):
        out_ref[...] = (f_ref[...] + o_ref[...]).astype(BF16)

    def full_map(*idx):
        ids, c_ref = list(idx[:nd]), idx[nd]
        ids[axis] = ids[axis] + c_ref[0] * nblk
        return tuple(ids)

    def plain_map(*idx):
        return tuple(idx[:nd])

    return _pallas(
        body, name=name,
        grid_spec=pltpu.PrefetchScalarGridSpec(
            num_scalar_prefetch=1, grid=grid,
            in_specs=[pl.BlockSpec(block, full_map), pl.BlockSpec(block, plain_map)],
            out_specs=pl.BlockSpec(block, plain_map)),
        out_shape=jax.ShapeDtypeStruct(other.shape, BF16), compiler_params=_params(nd),
    )(_core_index(), full, other)


NPEER = 3


def _landing_shape(kind, src):
    if kind == "pair":
        return _half_shape(src)
    if kind == "cols":
        return jax.ShapeDtypeStruct((NPEER, src.shape[0], src.shape[1] // NCHIP), src.dtype)
    return jax.ShapeDtypeStruct((NPEER,) + src.shape[1:], src.dtype)


def _chip_plan(kinds, srcs, lands, sems):
    nw = len(srcs)
    chipwise = [j for j in range(nw) if kinds[j] != "pair"]
    pairwise = [j for j in range(nw) if kinds[j] == "pair"]

    def place():
        x, y, c = lax.axis_index("x"), lax.axis_index("y"), lax.axis_index("c")
        return 2 * x + y, c, [(1 - x, y), (x, 1 - y), (1 - x, 1 - y)]

    def pair_copies(c):
        sib = (lax.axis_index("x"), lax.axis_index("y"), 1 - c)
        return [_remote(_other_half(srcs[j], c), lands[j], sems, j, sib) for j in pairwise]

    def piece(j, chip):
        if kinds[j] == "cols":
            n = srcs[j].shape[1] // NCHIP
            return srcs[j].at[:, pl.ds(pl.multiple_of(chip * n, LANE), n)]
        return srcs[j].at[chip]

    def my_sends(c, peers):
        return pair_copies(c) + [_remote(piece(j, 2 * px + py), lands[j].at[k], sems, nw * k + j, (px, py, c))
                                 for k, (px, py) in enumerate(peers) for j in chipwise]

    def start():
        _, c, peers = place()
        for cp in my_sends(c, peers):
            cp.start()

    def finish():
        me, c, peers = place()
        for cp in pair_copies(c):
            cp.wait_recv()
        for k, (px, py) in enumerate(peers):
            for j in chipwise:
                _remote(piece(j, me), lands[j].at[k], sems, nw * k + j, (px, py, c)).wait_recv()
        for cp in my_sends(c, peers):
            cp.wait_send()

    return start, finish


def _chip_sems(nw):
    return [pltpu.SemaphoreType.DMA((NPEER * nw,)), pltpu.SemaphoreType.DMA((NPEER * nw,))]


def _chip_index():
    return (2 * lax.axis_index("x") + lax.axis_index("y")).astype(jnp.int32).reshape(1)


def _chip_sum(own, slots, *, own_block, own_map, block, name):
    npeer = slots.shape[0]
    shape = slots.shape[1:]
    grid = (shape[0] // block[0], shape[1] // block[1])

    def body(p_ref, own_ref, s_ref, o_ref):
        acc = own_ref[...].reshape(block).astype(F32)
        for q in range(npeer):
            acc = acc + s_ref[q].astype(F32)
        o_ref[...] = acc

    return _pallas(
        body, name=name,
        grid_spec=pltpu.PrefetchScalarGridSpec(
            num_scalar_prefetch=1, grid=grid,
            in_specs=[pl.BlockSpec(own_block, own_map),
                      pl.BlockSpec((npeer,) + block, lambda i, j, p: (0, i, j))],
            out_specs=pl.BlockSpec(block, lambda i, j, p: (i, j))),
        out_shape=jax.ShapeDtypeStruct(shape, F32), compiler_params=_params(2),
    )(_chip_index(), own, slots)


def _pair_share(r_in0, r_in1, r_out0, r_out1):
    def body(a0, a1, b0, b1, g0, g1, h0, h1, send_sems, recv_sems):
        x, y, c = lax.axis_index("x"), lax.axis_index("y"), lax.axis_index("c")
        sib = (x, y, 1 - c)
        sends = [_remote(s, d, (send_sems, recv_sems), j, sib)
                 for j, (s, d) in enumerate(zip([a0, a1, b0, b1], [g0, g1, h0, h1]))]
        for cp in sends:
            cp.start()
        for cp in sends:
            cp.wait()

    return _pallas(
        body, name="pair_share", in_specs=[ANY] * 4, out_specs=[ANY] * 4,
        out_shape=[jax.ShapeDtypeStruct(r.shape, F32) for r in (r_in0, r_in1, r_out0, r_out1)],
        scratch_shapes=[pltpu.SemaphoreType.DMA((4,)), pltpu.SemaphoreType.DMA((4,))],
    )(r_in0, r_in1, r_out0, r_out1)


def _adam_math(g, w, m, v):
    c1 = 1.0 - ADAM_B1 ** ADAM_STEP
    c2 = 1.0 - ADAM_B2 ** ADAM_STEP
    m2 = ADAM_B1 * m + (1.0 - ADAM_B1) * g
    v2 = ADAM_B2 * v + (1.0 - ADAM_B2) * (g * g)
    delta = -ADAM_LR * ((m2 / c1) / (jnp.sqrt(v2 / c2) + ADAM_EPS) + ADAM_WD * w)
    return delta, m2, v2


def _adamw_nat(g_mine, g_sib, w, m, v, *, name, tr):
    rows, cw = w.shape
    nt = g_mine.shape[0] // tr

    def body(c_ref, gm_ref, gs_ref, w_ref, m_ref, v_ref, go_ref, d_ref, nm_ref, nv_ref):
        mine = pl.program_id(0) // nt == c_ref[0]
        gv = jnp.where(mine, gm_ref[...], gs_ref[...])[:, 0:cw]
        delta, m2, v2 = _adam_math(gv, w_ref[...], m_ref[...], v_ref[...])
        go_ref[...] = gv
        d_ref[...] = delta
        nm_ref[...] = m2
        nv_ref[...] = v2

    def mine_map(i, c_ref):
        return (jnp.where(i // nt == c_ref[0], i % nt, 0), 0)

    def sib_map(i, c_ref):
        return (jnp.where(i // nt == c_ref[0], 0, i % nt), 0)

    row = pl.BlockSpec((tr, cw), lambda i, c_ref: (i, 0))
    gspec = (tr, g_mine.shape[1])
    out = jax.ShapeDtypeStruct((rows, cw), F32)
    return _pallas(
        body, name=name,
        grid_spec=pltpu.PrefetchScalarGridSpec(
            num_scalar_prefetch=1, grid=(rows // tr,),
            in_specs=[pl.BlockSpec(gspec, mine_map), pl.BlockSpec(gspec, sib_map), row, row, row],
            out_specs=[row, row, row, row]),
        out_shape=[out, out, out, out], compiler_params=_params(1),
    )(_core_index(), g_mine, g_sib, w, m, v)


def _adamw(slots, w, m, v, *, name, tr):
    nd, rows, _ = slots.shape
    c1 = 1.0 - ADAM_B1 ** ADAM_STEP
    c2 = 1.0 - ADAM_B2 ** ADAM_STEP

    def body(s_ref, w_ref, m_ref, v_ref, g_ref, d_ref, nm_ref, nv_ref):
        g = s_ref[0]
        for d in range(1, nd):
            g = g + s_ref[d]
        m2 = ADAM_B1 * m_ref[...] + (1.0 - ADAM_B1) * g
        v2 = ADAM_B2 * v_ref[...] + (1.0 - ADAM_B2) * (g * g)
        g_ref[...] = g
        nm_ref[...] = m2
        nv_ref[...] = v2
        d_ref[...] = -ADAM_LR * ((m2 / c1) / (jnp.sqrt(v2 / c2) + ADAM_EPS) + ADAM_WD * w_ref[...])

    row = pl.BlockSpec((tr, LANE), lambda i: (i, 0))
    out = jax.ShapeDtypeStruct((rows, LANE), F32)
    return _pallas(
        body, name=name, grid=(rows // tr,),
        in_specs=[pl.BlockSpec((nd, tr, LANE), lambda i: (0, i, 0)), row, row, row],
        out_specs=[row, row, row, row], out_shape=[out, out, out, out], compiler_params=_params(1),
    )(slots, w, m, v)


def _rows(a):
    return a.reshape(-1, LANE)


def _pad_rows(a, mult):
    pad = (-a.shape[0]) % mult
    return jnp.pad(a, ((0, pad), (0, 0))) if pad else a


def _pack(parts, mult):
    return _pad_rows(jnp.concatenate([_rows(p) for p in parts], axis=0), mult)


def _unpack(slab, shapes):
    out, r0 = [], 0
    for shp in shapes:
        n = 1
        for s in shp:
            n *= s
        r = n // LANE
        out.append(slab[r0:r0 + r].reshape(shp))
        r0 += r
    return out


def _pack_rep(vecs, scal):
    srow = jnp.concatenate([s.reshape(-1) for s in scal] + [jnp.zeros((LANE - 3 * SSM_H,), F32)]).reshape(1, LANE)
    return _pad_rows(jnp.concatenate([_rows(vv) for vv in vecs] + [srow], axis=0), 8)


def _unpack_rep(slab, vec_shapes, scal_shape):
    vecs, r0 = [], 0
    for shp in vec_shapes:
        vecs.append(slab[r0:r0 + 8].reshape(shp))
        r0 += 8
    srow = slab[r0]
    scal = [srow[i * SSM_H:(i + 1) * SSM_H].reshape(scal_shape) for i in range(3)]
    return vecs, scal


def kernel(x, ev_norm_w, ev_w_in, ev_dw_w, ev_dw_b, ev_ln_w, ev_ln_b, ev_w_out, od_norm_w, od_w_in, od_conv_w, od_conv_b, od_dt_bias, od_a_log, od_d, od_gnorm_w, od_w_out, final_norm_w, loss_target, m_ev_norm_w, m_ev_w_in, m_ev_dw_w, m_ev_dw_b, m_ev_ln_w, m_ev_ln_b, m_ev_w_out, m_od_norm_w, m_od_w_in, m_od_conv_w, m_od_conv_b, m_od_dt_bias, m_od_a_log, m_od_d, m_od_gnorm_w, m_od_w_out, m_final_norm_w, v_ev_norm_w, v_ev_w_in, v_ev_dw_w, v_ev_dw_b, v_ev_ln_w, v_ev_ln_b, v_ev_w_out, v_od_norm_w, v_od_w_in, v_od_conv_w, v_od_conv_b, v_od_dt_bias, v_od_a_log, v_od_d, v_od_gnorm_w, v_od_w_out, v_final_norm_w):
    nb, seq, d = x.shape
    t = nb * seq
    nchip = 4
    xf = x.reshape(t, d)
    tgt = loss_target.reshape(t, d)

    big_w = [ev_w_in[0], od_w_in[0], ev_w_out[0], od_w_out[0]]
    small_w = [ev_dw_w[0], od_norm_w[0], od_conv_w[0], od_conv_b[0], od_gnorm_w[0]]
    small_shapes = [a.shape for a in small_w]
    big_b = [a.astype(BF16) for a in big_w]
    small_slab = _pack(small_w, 8)
    w_in0, w_out0, gath_small = _gather_shards(("cols", "rows"), [big_b[0], big_b[2]], small_slab)
    chip = 2 * lax.axis_index("x") + lax.axis_index("y")
    w_in0 = _place_cols(w_in0, big_b[0], name="place_w_in0")
    w_out0 = lax.dynamic_update_slice(w_out0, big_b[2], (chip * big_b[2].shape[0], 0))
    gath_small = lax.dynamic_update_slice(gath_small, small_slab[None], (chip, 0, 0))
    per_chip = [_unpack(gath_small[p], small_shapes) for p in range(nchip)]

    def cat(idx, axis):
        return jnp.concatenate([per_chip[p][idx] for p in range(nchip)], axis=axis)

    dw_w = jnp.pad(cat(0, 1), ((0, HALO - CONF_K), (0, 0)))
    dw_w8 = jnp.repeat(dw_w, SUB, axis=0)
    n1_w = cat(1, 0).reshape(1, d)
    conv_w = jnp.pad(cat(2, 1), ((0, PH - SSM_K), (0, 0)))
    conv_b = cat(3, 0).reshape(1, XBC)
    gn_w = cat(4, 0).reshape(1, D_INNER)

    def lanes(a):
        return jnp.pad(a.reshape(1, -1), ((0, 0), (0, LANE - a.size)))

    dt_bias, a_log = lanes(od_dt_bias), lanes(od_a_log)
    d_x = jnp.repeat(od_d.reshape(-1), SSM_P).reshape(1, D_INNER)
    hid = lax.broadcasted_iota(jnp.int32, (SSM_H, D_INNER), 1) // SSM_P
    ex = (hid == lax.broadcasted_iota(jnp.int32, (SSM_H, D_INNER), 0)).astype(BF16)
    ex_t = jnp.pad(ex.T, ((0, 0), (0, LANE - SSM_H)))
    fn_w = final_norm_w.reshape(1, d)

    n0 = _rms_fwd(xf, ev_norm_w, name="rms_fwd0")
    proj0, w_in1g, w_out1 = _matmul_with_gather(n0, w_in0, ("slab", "rows"), [big_b[1], big_b[3]],
                                                out_dtype=BF16, bm=512, bn=1024, name="in_proj0")
    w_in1g = lax.dynamic_update_slice(w_in1g, big_b[1][None], (chip, 0, 0))
    w_out1 = lax.dynamic_update_slice(w_out1, big_b[3], (chip * big_b[3].shape[0], 0))
    w_in1 = jnp.pad(jnp.concatenate([w_in1g[p] for p in range(nchip)], axis=1),
                    ((0, 0), (0, IN_ODD_PAD - IN_ODD)))
    y_conv, u2 = _conf_fwd(proj0, dw_w8, ev_dw_b, ev_ln_w, ev_ln_b, seq)
    o_att, y_att = _sba_fwd(proj0, nb, seq)
    ycat0 = jnp.concatenate([y_conv, y_att], axis=1)
    h1 = _matmul(ycat0, w_out0, mode="nn", out_dtype=F32, bm=512, bn=d, bk=D_INNER, name="out_proj0", residual=xf)
    n1 = _rms_fwd(h1, n1_w, name="rms_fwd1")
    proj1 = _matmul(n1, w_in1, mode="nn", out_dtype=BF16, bm=512, bn=768, bk=d, name="in_proj1", n_major=True)
    dt_raw = _matmul(n1, w_in1[:, D_INNER + XBC:IN_ODD_PAD], mode="nn", out_dtype=F32, bm=512,
                     bn=IN_ODD_PAD - D_INNER - XBC, bk=d, name="in_proj1_dt")
    xbc_c = _xconv_fwd(proj1, conv_w, conv_b, seq)
    dt = _dt_fwd(dt_raw, dt_bias)
    y_ssd, states = _ssd_fwd(xbc_c, dt, a_log, ex, nb, seq)
    yg = _gate_fwd(y_ssd, xbc_c, proj1, d_x, gn_w)
    h2 = _matmul(yg, w_out1, mode="nn", out_dtype=F32, bm=512, bn=d, bk=D_INNER, name="out_proj1", residual=h1)
    dh2, g_fn, loss_part = _final_loss(h2, fn_w, tgt)

    dyg = _matmul(dh2, w_out1, mode="nt", out_dtype=BF16, bm=512, bn=1024, bk=d, name="d_out_proj1")
    g_w_out1 = _matmul(yg, dh2, mode="tn", out_dtype=F32, bm=1024, bn=d, bk=1024, name="dw_out_proj1")
    dy_ssd, dz, g_gn, g_dx = _gate_bwd(dyg, y_ssd, xbc_c, proj1, d_x, gn_w)
    dxbc_c, ddt, g_a = _ssd_bwd(xbc_c, dt, a_log, ex, ex_t, states, dy_ssd, d_x, nb, seq)
    dproj1, g_conv_w, g_conv_b = _xconv_bwd(proj1, dxbc_c, conv_w, conv_b, dz, seq)
    dproj1, g_dt_bias = _dt_bwd(dt_raw, dt_bias, ddt, dproj1)
    dn1 = _matmul(dproj1, w_in1, mode="nt", out_dtype=BF16, bm=1024, bn=d, bk=1792, name="d_in_proj1")
    g_w_in1 = _matmul(n1, dproj1, mode="tn", out_dtype=F32, bm=d, bn=1792, bk=1024, name="dw_in_proj1")
    dh1, g_n1 = _rms_bwd(dn1, h1, n1_w, dh2, name="rms_bwd1")

    ro = D_INNER // nchip
    n1 = IN_ODD // nchip
    n1p = -(-n1 // LANE) * LANE
    g_w_out1c = g_w_out1.reshape(nchip, ro, d)
    dycat0, q_in1, q_out1 = _matmul(dh1, w_out0, mode="nt", out_dtype=BF16, bm=512, bn=1024, bk=d, name="d_out_proj0",
                                    comm_kinds=("pair", "pair"), comm_srcs=[g_w_in1, g_w_out1c])
    g_w_out0 = _matmul(ycat0, dh1, mode="tn", out_dtype=F32, bm=1024, bn=d, bk=1024, name="dw_out_proj0")
    dq, dk, dv, dga = _sba_bwd(proj0, o_att, dycat0, nb, seq)
    s_in1n = _half_add(g_w_in1, q_in1, axis=0, block=(128, IN_ODD_PAD), name="half_add_in1")
    s_in1 = jnp.stack([jnp.pad(s_in1n[:, p * n1:(p + 1) * n1], ((0, 0), (0, n1p - n1))) for p in range(nchip)])
    s_out1 = _half_add(g_w_out1c, q_out1, axis=1, block=(1, ro // 2, d), name="half_add_out1")
    dpc, g_dw_w, g_dw_b, g_ln_w, g_ln_b, l_in1, l_out1 = _conf_bwd(
        proj0, u2, dycat0, dw_w8, ev_ln_w, ev_ln_b, seq, ("slab", "slab"), [s_in1, s_out1])
    dproj0 = jnp.concatenate([dpc, dq, dk, dv, dga], axis=1)
    g_w_in0 = _matmul(n0, dproj0, mode="tn", out_dtype=F32, bm=d, bn=1792, bk=1024, name="dw_in_proj0")
    g_w_out0c = g_w_out0.reshape(nchip, ro, d)
    q_in0, q_out0 = _pair_swap([g_w_in0, g_w_out0c], name="pair_swap_l0")
    s_in0 = _half_add(g_w_in0, q_in0, axis=0, block=(128, IN_EVEN), name="half_add_in0")
    s_out0 = _half_add(g_w_out0c, q_out0, axis=1, block=(1, ro // 2, d), name="half_add_out0")
    dn0, l_in0, l_out0 = _matmul(dproj0, w_in0, mode="nt", out_dtype=BF16, bm=1024, bn=d, bk=1792, name="d_in_proj0",
                                 comm_kinds=("cols", "slab"), comm_srcs=[s_in0, s_out0])
    grad_x, g_n0 = _rms_bwd(dn0, xf, ev_norm_w, dh1, name="rms_bwd0")

    g_dw_w = g_dw_w.reshape(HALO, SUB, CONV_W).sum(axis=1)[0:CONF_K]
    g_dw_b, g_ln_w, g_ln_b = (a.sum(axis=0, keepdims=True) for a in (g_dw_b, g_ln_w, g_ln_b))
    g_conv_w = g_conv_w.reshape(PH, SUB, XBC).sum(axis=1)[0:SSM_K]
    g_conv_b = g_conv_b.sum(axis=0, keepdims=True)
    a_neg = -jnp.exp(od_a_log.reshape(-1))
    g_a_log = g_a[0, 0:SSM_H] * a_neg
    g_d = g_dx.reshape(SSM_H, SSM_P).sum(axis=1)

    def chip_slab_small(p):
        c0, c1, c2, c3 = CONV_W // nchip, d // nchip, XBC // nchip, D_INNER // nchip
        return _pack([g_dw_w[:, p * c0:(p + 1) * c0], g_n1[0, p * c1:(p + 1) * c1],
                      g_conv_w[:, p * c2:(p + 1) * c2], g_conv_b[0, p * c2:(p + 1) * c2],
                      g_gn[0, p * c3:(p + 1) * c3]], 8)

    gsmall = jnp.stack([chip_slab_small(p) for p in range(nchip)])
    rep_vec_shapes = [ev_norm_w.shape, ev_dw_b.shape, ev_ln_w.shape, ev_ln_b.shape, final_norm_w.shape]
    grep = _pack_rep([g_n0, g_dw_b, g_ln_w, g_ln_b, g_fn], [g_dt_bias[0, 0:SSM_H], g_a_log, g_d])

    ssmall, srep = _pair_exchange([], gsmall, grep)
    r_in0 = _chip_sum(s_in0, l_in0, own_block=(128, IN_EVEN // nchip), own_map=lambda i, j, p: (i, p[0]),
                      block=(128, IN_EVEN // nchip), name="chip_sum_in0")
    r_in1 = _chip_sum(s_in1, l_in1, own_block=(1, 256, n1p), own_map=lambda i, j, p: (p[0], i, 0),
                      block=(256, n1p), name="chip_sum_in1")
    r_out0 = _chip_sum(s_out0, l_out0, own_block=(1, ro // 2, d), own_map=lambda i, j, p: (p[0], 0, 0),
                       block=(ro // 2, d), name="chip_sum_out0")
    r_out1 = _chip_sum(s_out1, l_out1, own_block=(1, ro // 2, d), own_map=lambda i, j, p: (p[0], 0, 0),
                       block=(ro // 2, d), name="chip_sum_out1")
    big_r = [r_in0, r_in1, r_out0, r_out1]
    big_q = _pair_share(*big_r)

    big_m = [m_ev_w_in[0], m_od_w_in[0], m_ev_w_out[0], m_od_w_out[0]]
    big_v = [v_ev_w_in[0], v_od_w_in[0], v_ev_w_out[0], v_od_w_out[0]]
    big_names = ["adamw_in0", "adamw_in1", "adamw_out0", "adamw_out1"]
    out_bigs = [_adamw_nat(gm, gs, w, m, v, name=nm, tr=128)
                for gm, gs, w, m, v, nm in zip(big_r, big_q, big_w, big_m, big_v, big_names)]

    def upd(slots, ws, ms, vs, packer, name, tr):
        return _adamw(slots, packer(ws), packer(ms), packer(vs), name=name, tr=tr)

    small_m = [m_ev_dw_w[0], m_od_norm_w[0], m_od_conv_w[0], m_od_conv_b[0], m_od_gnorm_w[0]]
    small_v = [v_ev_dw_w[0], v_od_norm_w[0], v_od_conv_w[0], v_od_conv_b[0], v_od_gnorm_w[0]]
    out_small = upd(ssmall, small_w, small_m, small_v, lambda a: _pack(a, 8), "adamw_small", ssmall.shape[1])

    def rep_pack(a):
        return _pack_rep(a[0:5], a[5:8])

    rep_w = [ev_norm_w, ev_dw_b, ev_ln_w, ev_ln_b, final_norm_w, od_dt_bias, od_a_log, od_d]
    rep_m = [m_ev_norm_w, m_ev_dw_b, m_ev_ln_w, m_ev_ln_b, m_final_norm_w, m_od_dt_bias, m_od_a_log, m_od_d]
    rep_v = [v_ev_norm_w, v_ev_dw_b, v_ev_ln_w, v_ev_ln_b, v_final_norm_w, v_od_dt_bias, v_od_a_log, v_od_d]
    out_rep = upd(srep, rep_w, rep_m, rep_v, rep_pack, "adamw_rep", srep.shape[1])

    results = []
    for kind in range(4):
        bw = [o[kind].reshape((1,) + o[kind].shape) for o in out_bigs]
        sw = _unpack(out_small[kind], small_shapes)
        vecs, scal = _unpack_rep(out_rep[kind], rep_vec_shapes, od_dt_bias.shape)
        results.append([
            vecs[0], bw[0], sw[0].reshape(ev_dw_w.shape), vecs[1], vecs[2], vecs[3], bw[2],
            sw[1].reshape(od_norm_w.shape), bw[1], sw[2].reshape(od_conv_w.shape), sw[3].reshape(od_conv_b.shape),
            scal[0], scal[1], scal[2], sw[4].reshape(od_gnorm_w.shape), bw[3], vecs[4]])
    loss = lax.psum(loss_part[0, 0], ("x", "y", "c"))
    return (loss, grad_x.reshape(x.shape), *results[0], *results[1], *results[2], *results[3])
```

```python
import jax
import jax.numpy as jnp
from jax import lax
from jax.experimental import pallas as pl
from jax.experimental.pallas import tpu as pltpu

F32 = jnp.float32
BF16 = jnp.bfloat16

D_MODEL = 1024
CONV_W = 1024
ATT_W = 1024
HEAD_DIM = 128
N_HEADS = 8
CONF_K = 31
IN_EVEN = 7168
D_INNER = 2048
SSM_P = 64
SSM_H = 32
SSM_G = 4
SSM_R = SSM_H // SSM_G
SSM_N = 128
SSM_K = 4
CHUNK = 128
XBC = D_INNER + 2 * SSM_G * SSM_N
IN_ODD = D_INNER + XBC + SSM_H
IN_ODD_PAD = 5376
EPS = 1e-6
QB = 128
NEG_CUT = -100.0

ADAM_LR = 0.001
ADAM_B1 = 0.9
ADAM_B2 = 0.999
ADAM_EPS = 1e-08
ADAM_WD = 0.01
ADAM_STEP = 10

LANE = 128
VMEM_LIMIT = 56 * 1024 * 1024
MESH = pl.DeviceIdType.MESH

NN = (((1,), (0,)), ((), ()))
NT = (((1,), (1,)), ((), ()))
TN = (((0,), (0,)), ((), ()))


def _pallas(body, **kw):
    return pl.pallas_call(body, **kw)


def _params(n_axes):
    return pltpu.CompilerParams(dimension_semantics=("arbitrary",) * n_axes, vmem_limit_bytes=VMEM_LIMIT)


def _dot(a, b, dims=NN):
    return lax.dot_general(a.astype(BF16), b.astype(BF16), dims, preferred_element_type=F32)


def _parts(x):
    h = x.astype(BF16)
    r = x - h.astype(F32)
    m = r.astype(BF16)
    l = (r - m.astype(F32)).astype(BF16)
    return (h, m, l)


def _dotx(x, e01, dims=NN):
    acc = None
    for p in _parts(x):
        t = lax.dot_general(p, e01, dims, preferred_element_type=F32)
        acc = t if acc is None else acc + t
    return acc


def _dotx2(x, e01, dims=NN):
    h = x.astype(BF16)
    l = (x - h.astype(F32)).astype(BF16)
    return (lax.dot_general(h, e01, dims, preferred_element_type=F32)
            + lax.dot_general(l, e01, dims, preferred_element_type=F32))


def _xdot(e01, x, dims=NN):
    acc = None
    for p in _parts(x):
        t = lax.dot_general(e01, p, dims, preferred_element_type=F32)
        acc = t if acc is None else acc + t
    return acc


def _f32(x):
    return x.astype(F32)


def _sigmoid(x):
    return 1.0 / (1.0 + jnp.exp(-x))


def _dsilu(x, s):
    return s * (1.0 + x * (1.0 - s))


def _matmul(a, b, *, mode, out_dtype, bm, bn, bk, name, residual=None, n_major=False, comm_kinds=(), comm_srcs=()):
    if mode == "nn":
        (m, k), n = a.shape, b.shape[1]
        a_blk, a_map = (bm, bk), lambda i, j, kk: (i, kk)
        b_blk, b_map = (bk, bn), lambda i, j, kk: (kk, j)
        dims = NN
    elif mode == "nt":
        (m, k), n = a.shape, b.shape[0]
        a_blk, a_map = (bm, bk), lambda i, j, kk: (i, kk)
        b_blk, b_map = (bn, bk), lambda i, j, kk: (j, kk)
        dims = NT
    else:
        (k, m), n = a.shape, b.shape[1]
        a_blk, a_map = (bk, bm), lambda i, j, kk: (kk, i)
        b_blk, b_map = (bk, bn), lambda i, j, kk: (kk, j)
        dims = TN
    bm, bn, bk = min(bm, m), min(bn, n), min(bk, k)
    if mode != "nn":
        a_blk = (bm, bk) if mode == "nt" else (bk, bm)
        b_blk = (bn, bk) if mode == "nt" else (bk, bn)
    else:
        a_blk, b_blk = (bm, bk), (bk, bn)
    assert m % bm == 0 and n % bn == 0 and k % bk == 0, (name, m, n, k)
    nk = k // bk
    has_res = residual is not None

    def order(f):
        return (lambda j, i, kk: f(i, j, kk)) if n_major else f

    nw = len(comm_srcs)
    grid = (n // bn, m // bm, nk) if n_major else (m // bm, n // bn, nk)

    def body(*refs):
        a_ref, b_ref = refs[0], refs[1]
        r_ref = refs[2] if has_res else None
        n_in = 2 + has_res + nw
        o_ref = refs[n_in]
        n_out = n_in + 1 + nw

        def finish(r):
            if has_res:
                r = r + r_ref[...]
            o_ref[...] = r.astype(out_dtype)

        def compute():
            if nk == 1:
                finish(_dot(a_ref[...], b_ref[...], dims))
                return
            acc_ref = refs[n_out]
            kk = pl.program_id(2)

            @pl.when(kk == 0)
            def _():
                acc_ref[...] = jnp.zeros_like(acc_ref)

            acc_ref[...] += _dot(a_ref[...], b_ref[...], dims)

            @pl.when(kk == nk - 1)
            def _():
                finish(acc_ref[...])

        if not nw:
            compute()
            return
        start, done = _chip_plan(comm_kinds, refs[2 + has_res:n_in], refs[n_in + 1:n_out], refs[n_out + (nk > 1):])
        ids = [pl.program_id(ax) for ax in range(3)]

        @pl.when(jnp.logical_and(jnp.logical_and(ids[0] == 0, ids[1] == 0), ids[2] == 0))
        def _():
            start()

        compute()

        @pl.when(jnp.logical_and(jnp.logical_and(ids[0] == grid[0] - 1, ids[1] == grid[1] - 1), ids[2] == grid[2] - 1))
        def _():
            done()

    in_specs = [pl.BlockSpec(a_blk, order(a_map)), pl.BlockSpec(b_blk, order(b_map))]
    args = [a, b]
    out_map = order(lambda i, j, kk: (i, j))
    if has_res:
        in_specs.append(pl.BlockSpec((bm, bn), out_map))
        args.append(residual)
    any_spec = pl.BlockSpec(memory_space=pl.ANY)
    out_specs = [pl.BlockSpec((bm, bn), out_map)] + [any_spec] * nw
    out_shape = [jax.ShapeDtypeStruct((m, n), out_dtype)] + [_landing_shape(kd, s) for kd, s in zip(comm_kinds, comm_srcs)]
    res = _pallas(
        body, name=name, grid=grid, in_specs=in_specs + [any_spec] * nw, out_specs=out_specs, out_shape=out_shape,
        scratch_shapes=([pltpu.VMEM((bm, bn), F32)] if nk > 1 else []) + (_chip_sems(nw) if nw else []),
        compiler_params=_params(3),
    )(*args, *comm_srcs)
    return res if nw else res[0]


def _rms_fwd(x, w, *, name, tm=512):
    t, d = x.shape

    def body(x_ref, w_ref, o_ref):
        xv = x_ref[...]
        r = lax.rsqrt(jnp.mean(xv * xv, axis=1, keepdims=True) + EPS)
        o_ref[...] = (xv * r * w_ref[...]).astype(BF16)

    return _pallas(
        body, name=name, grid=(t // tm,),
        in_specs=[pl.BlockSpec((tm, d), lambda i: (i, 0)), pl.BlockSpec((1, d), lambda i: (0, 0))],
        out_specs=pl.BlockSpec((tm, d), lambda i: (i, 0)),
        out_shape=jax.ShapeDtypeStruct((t, d), BF16), compiler_params=_params(1),
    )(x, w)


def _rms_bwd(dn, x, w, dres, *, name, tm=512):
    t, d = x.shape

    def body(dn_ref, x_ref, w_ref, dr_ref, dx_ref, dw_ref):
        i = pl.program_id(0)
        xv = x_ref[...]
        r = lax.rsqrt(jnp.mean(xv * xv, axis=1, keepdims=True) + EPS)
        xh = xv * r
        dy = dn_ref[...].astype(F32)
        g = dy * w_ref[...]
        dx_ref[...] = dr_ref[...] + r * (g - xh * jnp.mean(g * xh, axis=1, keepdims=True))

        @pl.when(i == 0)
        def _():
            dw_ref[...] = jnp.zeros_like(dw_ref)

        dw_ref[...] += jnp.sum(dy * xh, axis=0, keepdims=True)

    row = pl.BlockSpec((tm, d), lambda i: (i, 0))
    vec = pl.BlockSpec((1, d), lambda i: (0, 0))
    return _pallas(
        body, name=name, grid=(t // tm,), in_specs=[row, row, vec, row], out_specs=[row, vec],
        out_shape=[jax.ShapeDtypeStruct((t, d), F32), jax.ShapeDtypeStruct((1, d), F32)],
        compiler_params=_params(1),
    )(dn, x, w, dres)


def _final_loss(h, w, target, *, tm=512):
    t, d = h.shape

    def body(h_ref, w_ref, t_ref, dh_ref, dw_ref, loss_ref):
        i = pl.program_id(0)
        xv = h_ref[...]
        r = lax.rsqrt(jnp.mean(xv * xv, axis=1, keepdims=True) + EPS)
        xh = xv * r
        wv = w_ref[...]
        err = xh * wv - t_ref[...]
        dy = err * (1.0 / d)
        g = dy * wv
        dh_ref[...] = r * (g - xh * jnp.mean(g * xh, axis=1, keepdims=True))

        @pl.when(i == 0)
        def _():
            dw_ref[...] = jnp.zeros_like(dw_ref)
            loss_ref[...] = jnp.zeros_like(loss_ref)

        dw_ref[...] += jnp.sum(dy * xh, axis=0, keepdims=True)
        part = jnp.sum(jnp.sum(err * err, axis=1, keepdims=True), axis=0, keepdims=True)
        loss_ref[...] += part * (0.5 / d)

    row = pl.BlockSpec((tm, d), lambda i: (i, 0))
    vec = pl.BlockSpec((1, d), lambda i: (0, 0))
    return _pallas(
        body, name="final_loss", grid=(t // tm,), in_specs=[row, vec, row],
        out_specs=[row, vec, pl.BlockSpec((1, LANE), lambda i: (0, 0))],
        out_shape=[jax.ShapeDtypeStruct((t, d), F32), jax.ShapeDtypeStruct((1, d), F32),
                   jax.ShapeDtypeStruct((1, LANE), F32)],
        compiler_params=_params(1),
    )(h, w, target)


HALO = 32


SUB = 8
RC = 16


def _make_shifts(sh_ref, rows, shifts=tuple(range(1, SUB))):
    for s in shifts:
        sh_ref[s, 0:rows, :] = sh_ref[0, s:s + rows, :]


def _shifted(sh_ref, r0, j, rows):
    return sh_ref[j % SUB, pl.ds(r0 + (j - j % SUB), rows), :]


def _taps(w8_ref, sh_ref, r0, first, step, init):
    accs = [init] * (RC // SUB)
    for k in range(CONF_K):
        wk = w8_ref[k * SUB:(k + 1) * SUB, :]
        x = _shifted(sh_ref, r0, first + step * k, RC)
        accs = [a + wk * x[q * SUB:(q + 1) * SUB] for q, a in enumerate(accs)]
    return jnp.concatenate(accs, axis=0)


def _conf_fwd(proj, dw_w, dw_b, ln_w, ln_b, seq, *, tm=256):
    t = proj.shape[0]
    c = CONV_W
    tps = seq // tm
    hb = tm // HALO

    def body(a_ref, b_ref, g_ref, ha_ref, hb_ref, w_ref, wb_ref, lw_ref, lb_ref, y_ref, u2_ref, sh_ref):
        i = pl.program_id(0)
        keep = jnp.where(i % tps == 0, 0.0, 1.0)
        sh_ref[0, 0:HALO, :] = _f32(ha_ref[...]) * _sigmoid(_f32(hb_ref[...])) * keep
        sh_ref[0, HALO:HALO + tm, :] = _f32(a_ref[...]) * _sigmoid(_f32(b_ref[...]))
        _make_shifts(sh_ref, tm + HALO - SUB)

        def chunk(ci, carry):
            r0 = pl.multiple_of(ci * RC, RC)
            acc = _taps(w_ref, sh_ref, r0, HALO - CONF_K + 1, 1, jnp.broadcast_to(wb_ref[...], (SUB, c)))
            u2_ref[pl.ds(r0, RC), :] = acc
            mu = jnp.mean(acc, axis=1, keepdims=True)
            xc = acc - mu
            rs = lax.rsqrt(jnp.mean(xc * xc, axis=1, keepdims=True) + EPS)
            u3 = xc * rs * lw_ref[...] + lb_ref[...]
            gv = _f32(g_ref[pl.ds(r0, RC), :])
            y_ref[pl.ds(r0, RC), :] = (u3 * _sigmoid(u3) * gv * _sigmoid(gv)).astype(BF16)
            return carry

        lax.fori_loop(0, tm // RC, chunk, 0, unroll=4)

    def col(j):
        return pl.BlockSpec((tm, c), lambda i: (i, j))

    def prev(j):
        return pl.BlockSpec((HALO, c), lambda i: (jnp.maximum(i * hb - 1, 0), j))

    vec = pl.BlockSpec((1, c), lambda i: (0, 0))
    return _pallas(
        body, name="conf_fwd", grid=(t // tm,),
        in_specs=[col(0), col(1), col(2), prev(0), prev(1),
                  pl.BlockSpec((HALO * SUB, c), lambda i: (0, 0)), vec, vec, vec],
        out_specs=[pl.BlockSpec((tm, c), lambda i: (i, 0)), pl.BlockSpec((tm, c), lambda i: (i, 0))],
        out_shape=[jax.ShapeDtypeStruct((t, c), BF16), jax.ShapeDtypeStruct((t, c), F32)],
        scratch_shapes=[pltpu.VMEM((SUB, tm + HALO, c), F32)], compiler_params=_params(1),
    )(proj, proj, proj, proj, proj, dw_w, dw_b, ln_w, ln_b)


def _conf_bwd(proj, u2, dycat, dw_w, ln_w, ln_b, seq, comm_kinds, comm_srcs, *, tm=256):
    t = proj.shape[0]
    c = CONV_W
    tps = seq // tm
    hb = tm // HALO
    nhb = t // HALO
    nw = len(comm_srcs)
    nsteps = t // tm

    def fold(v):
        out = v[0:SUB]
        for q in range(1, RC // SUB):
            out = out + v[q * SUB:(q + 1) * SUB]
        return out

    def body(*refs):
        (a_ref, b_ref, g_ref, pa_ref, pb_ref, ng_ref, u2_ref, nu2_ref, dy_ref, ndy_ref,
         w_ref, lw_ref, lb_ref) = refs[:13]
        dp_ref, dww_ref, dwb_ref, dlw_ref, dlb_ref = refs[13 + nw:18 + nw]
        su_ref, sd_ref = refs[18 + 2 * nw:20 + 2 * nw]
        comm_start, comm_finish = _chip_plan(comm_kinds, refs[13:13 + nw], refs[18 + nw:18 + 2 * nw],
                                             refs[20 + 2 * nw:])
        i = pl.program_id(0)
        first = i % tps == 0
        last = i % tps == tps - 1

        @pl.when(i == 0)
        def _():
            comm_start()
            dww_ref[...] = jnp.zeros_like(dww_ref)
            dwb_ref[...] = jnp.zeros_like(dwb_ref)
            dlw_ref[...] = jnp.zeros_like(dlw_ref)
            dlb_ref[...] = jnp.zeros_like(dlb_ref)

        su_ref[0, 0:HALO, :] = _f32(pa_ref[...]) * _sigmoid(_f32(pb_ref[...])) * jnp.where(first, 0.0, 1.0)
        su_ref[0, HALO:HALO + tm, :] = _f32(a_ref[...]) * _sigmoid(_f32(b_ref[...]))
        _make_shifts(su_ref, tm + HALO - SUB)

        def ln_back(u2c, gv, dy):
            mu = jnp.mean(u2c, axis=1, keepdims=True)
            xc = u2c - mu
            rs = lax.rsqrt(jnp.mean(xc * xc, axis=1, keepdims=True) + EPS)
            xh = xc * rs
            lw = lw_ref[...]
            u3 = xh * lw + lb_ref[...]
            s3 = _sigmoid(u3)
            sg = _sigmoid(gv)
            dgc = dy * (u3 * s3) * _dsilu(gv, sg)
            du3 = dy * gv * sg * _dsilu(u3, s3)
            dxh = du3 * lw
            du2 = rs * (dxh - jnp.mean(dxh, axis=1, keepdims=True)
                        - xh * jnp.mean(dxh * xh, axis=1, keepdims=True))
            return du2, dgc, du3, xh

        def tile_chunk(ci, carry):
            r0 = pl.multiple_of(ci * RC, RC)
            rows = pl.ds(r0, RC)
            du2, dgc, du3, xh = ln_back(u2_ref[rows, :], _f32(g_ref[rows, :]), _f32(dy_ref[rows, :]))
            sd_ref[0, rows, :] = du2
            dp_ref[rows, 2 * c:3 * c] = dgc.astype(BF16)
            dwb_ref[...] += fold(du2)
            dlw_ref[...] += fold(du3 * xh)
            dlb_ref[...] += fold(du3)
            return carry

        lax.fori_loop(0, tm // RC, tile_chunk, 0, unroll=4)
        live = jnp.where(last, 0.0, 1.0)
        for ci in range(HALO // RC):
            rows = slice(ci * RC, (ci + 1) * RC)
            du2, _, _, _ = ln_back(nu2_ref[rows, :], _f32(ng_ref[rows, :]), _f32(ndy_ref[rows, :]))
            sd_ref[0, tm + ci * RC:tm + (ci + 1) * RC, :] = du2 * live
        _make_shifts(sd_ref, tm + HALO - SUB)

        def tap_chunk(ci, carry):
            r0 = pl.multiple_of(ci * RC, RC)
            rows = pl.ds(r0, RC)
            du1 = _taps(w_ref, sd_ref, r0, CONF_K - 1, -1, jnp.zeros((SUB, c), F32))
            sb = _sigmoid(_f32(b_ref[rows, :]))
            dp_ref[rows, 0:c] = (du1 * sb).astype(BF16)
            dp_ref[rows, c:2 * c] = (du1 * _f32(a_ref[rows, :]) * sb * (1.0 - sb)).astype(BF16)
            du2 = sd_ref[0, rows, :]
            for k in range(CONF_K):
                dww_ref[k * SUB:(k + 1) * SUB, :] += fold(du2 * _shifted(su_ref, r0, HALO - CONF_K + 1 + k, RC))
            return carry

        lax.fori_loop(0, tm // RC, tap_chunk, 0, unroll=2)

        @pl.when(i == nsteps - 1)
        def _():
            comm_finish()

    def col(j):
        return pl.BlockSpec((tm, c), lambda i: (i, j))

    def prev(j):
        return pl.BlockSpec((HALO, c), lambda i: (jnp.maximum(i * hb - 1, 0), j))

    def nxt(j):
        return pl.BlockSpec((HALO, c), lambda i: (jnp.minimum((i + 1) * hb, nhb - 1), j))

    vec = pl.BlockSpec((1, c), lambda i: (0, 0))
    acc = pl.BlockSpec((SUB, c), lambda i: (0, 0))
    any_spec = pl.BlockSpec(memory_space=pl.ANY)
    return _pallas(
        body, name="conf_bwd", grid=(nsteps,),
        in_specs=[col(0), col(1), col(2), prev(0), prev(1), nxt(2), col(0), nxt(0), col(0), nxt(0),
                  pl.BlockSpec((HALO * SUB, c), lambda i: (0, 0)), vec, vec] + [any_spec] * nw,
        out_specs=[pl.BlockSpec((tm, 3 * c), lambda i: (i, 0)),
                   pl.BlockSpec((HALO * SUB, c), lambda i: (0, 0)), acc, acc, acc] + [any_spec] * nw,
        out_shape=[jax.ShapeDtypeStruct((t, 3 * c), BF16), jax.ShapeDtypeStruct((HALO * SUB, c), F32),
                   jax.ShapeDtypeStruct((SUB, c), F32), jax.ShapeDtypeStruct((SUB, c), F32),
                   jax.ShapeDtypeStruct((SUB, c), F32)]
        + [_landing_shape(kd, s) for kd, s in zip(comm_kinds, comm_srcs)],
        scratch_shapes=[pltpu.VMEM((SUB, tm + HALO, c), F32), pltpu.VMEM((SUB, tm + HALO, c), F32)] + _chip_sems(nw),
        compiler_params=_params(1),
    )(proj, proj, proj, proj, proj, proj, u2, u2, dycat, dycat, dw_w, ln_w, ln_b, *comm_srcs)


Q_COL = 3 * CONV_W // HEAD_DIM
K_COL = Q_COL + N_HEADS
V_COL = K_COL + N_HEADS
GA_COL = V_COL + N_HEADS


SBA_TQ = 256
SBA_WK = 4 * QB


def _sb_window(qs, kw, ws, limit, t0, carry):
    tq, wk = qs.shape[0], kw.shape[0]
    z = _dot(qs, kw, NT)
    sg = ws + lax.broadcasted_iota(jnp.int32, (tq, wk), 1)
    tg = t0 + lax.broadcasted_iota(jnp.int32, (tq, wk), 0)
    mask = sg < jnp.minimum(tg, limit)
    sp = jnp.log(1.0 + jnp.exp(-jnp.abs(z)))
    ls = jnp.minimum(z, 0.0) - sp
    lk = jnp.where(mask, ls - z, 0.0)
    jj = lax.broadcasted_iota(jnp.int32, (QB, QB), 0)
    ss = lax.broadcasted_iota(jnp.int32, (QB, QB), 1)
    ustrict = jnp.where(jj > ss, 1.0, 0.0).astype(BF16)
    laters = [None] * (wk // QB)
    for ch in reversed(range(wk // QB)):
        lkc = lk[:, ch * QB:(ch + 1) * QB]
        laters[ch] = carry + _dotx2(lkc, ustrict)
        carry = carry + jnp.sum(lkc, axis=1, keepdims=True)
    w = jnp.where(mask, jnp.exp(ls + jnp.concatenate(laters, axis=1)), 0.0)
    return mask, ls, w, carry


def _sba_fwd(proj, nb, seq, *, tq=SBA_TQ, wk=SBA_WK):
    t = proj.shape[0]
    wk = min(wk, seq)
    nq = seq // tq
    scale = HEAD_DIM ** -0.5

    def body(q_ref, k_ref, v_ref, g_ref, o_ref, y_ref):
        i = pl.program_id(2)
        t0 = i * tq
        qs = (_f32(q_ref[...]) * scale).astype(BF16)

        def window(ws, limit, carry, acc):
            ws = pl.multiple_of(ws, QB)
            _, _, w, carry = _sb_window(qs, k_ref[pl.ds(ws, wk), :], ws, limit, t0, carry)
            return carry, acc + _dot(w, v_ref[pl.ds(ws, wk), :])

        ws0 = jnp.maximum(t0 + tq - wk, 0)
        carry, acc = window(ws0, seq, jnp.zeros((tq, 1), F32), jnp.zeros((tq, HEAD_DIM), F32))

        def cond(st):
            return jnp.logical_and(st[0] > 0, jnp.max(st[1]) > NEG_CUT)

        def step(st):
            c2, a2 = window(jnp.maximum(st[0] - wk, 0), st[0], st[1], st[2])
            return jnp.maximum(st[0] - wk, 0), c2, a2

        _, _, acc = lax.while_loop(cond, step, (ws0, carry, acc))
        o_ref[...] = acc
        gv = _f32(g_ref[...])
        y_ref[...] = (acc * gv * _sigmoid(gv)).astype(BF16)

    def tile(c0):
        return pl.BlockSpec((tq, HEAD_DIM), lambda b, h, i: (b * nq + i, c0 + h))

    def whole(c0):
        return pl.BlockSpec((seq, HEAD_DIM), lambda b, h, i: (b, c0 + h))

    return _pallas(
        body, name="sba_fwd", grid=(nb, N_HEADS, nq),
        in_specs=[tile(Q_COL), whole(K_COL), whole(V_COL), tile(GA_COL)],
        out_specs=[tile(0), tile(0)],
        out_shape=[jax.ShapeDtypeStruct((t, ATT_W), F32), jax.ShapeDtypeStruct((t, ATT_W), BF16)],
        compiler_params=_params(3),
    )(proj, proj, proj, proj)


def _sba_bwd(proj, o, dycat, nb, seq, *, tq=SBA_TQ, wk=SBA_WK):
    t = proj.shape[0]
    wk = min(wk, seq)
    nq = seq // tq
    nwin = -(-seq // wk) + 1
    nch = wk // QB
    scale = HEAD_DIM ** -0.5

    def body(q_ref, k_ref, v_ref, g_ref, o_ref, dy_ref, dq_ref, dko_ref, dvo_ref, dg_ref, e_ref, sp_ref,
             dk_ref, dv_ref):
        i = pl.program_id(2)
        t0 = i * tq

        @pl.when(i == 0)
        def _():
            dk_ref[...] = jnp.zeros_like(dk_ref)
            dv_ref[...] = jnp.zeros_like(dv_ref)

        qs = (_f32(q_ref[...]) * scale).astype(BF16)
        gv = _f32(g_ref[...])
        sg = _sigmoid(gv)
        dy = _f32(dy_ref[...])
        do = (dy * gv * sg).astype(BF16)
        dg_ref[...] = (dy * o_ref[...] * _dsilu(gv, sg)).astype(BF16)

        def start_of(n):
            return pl.multiple_of(jnp.maximum(t0 + tq - (n + 1) * wk, 0), QB)

        def limit_of(n):
            return jnp.where(n == 0, seq, jnp.maximum(t0 + tq - n * wk, 0))

        def near(n, carry):
            ws = start_of(n)
            _, ls, w, carry = _sb_window(qs, k_ref[pl.ds(ws, wk), :], ws, limit_of(n), t0, carry)
            e_ref[n] = w * _dot(do, v_ref[pl.ds(ws, wk), :], NT)
            sp_ref[n] = jnp.exp(ls)
            dv_ref[pl.ds(ws, wk), :] += _dot(w, do, TN)
            return carry

        carry = near(0, jnp.zeros((tq, 1), F32))

        def cond(st):
            return jnp.logical_and(start_of(st[0] - 1) > 0, jnp.max(st[1]) > NEG_CUT)

        def step(st):
            return st[0] + 1, near(st[0], st[1])

        nvis, _ = lax.while_loop(cond, step, (1, carry))

        jj = lax.broadcasted_iota(jnp.int32, (QB, QB), 0)
        ss = lax.broadcasted_iota(jnp.int32, (QB, QB), 1)
        lstrict = jnp.where(jj < ss, 1.0, 0.0).astype(BF16)

        def far(r, st):
            pre, dq = st
            n = nvis - 1 - r
            ws = start_of(n)
            e = e_ref[n]
            spn = sp_ref[n]
            gs = []
            for ch in range(nch):
                ec = e[:, ch * QB:(ch + 1) * QB]
                gs.append(pre + _dotx2(ec, lstrict))
                pre = pre + jnp.sum(ec, axis=1, keepdims=True)
            sgl = ws + lax.broadcasted_iota(jnp.int32, (tq, wk), 1)
            tgl = t0 + lax.broadcasted_iota(jnp.int32, (tq, wk), 0)
            mask = sgl < jnp.minimum(tgl, limit_of(n))
            dz = jnp.where(mask, e * (1.0 - spn) - jnp.concatenate(gs, axis=1) * spn, 0.0).astype(BF16)
            dk_ref[pl.ds(ws, wk), :] += _dot(dz, qs, TN)
            return pre, dq + _dot(dz, k_ref[pl.ds(ws, wk), :])

        _, dq = lax.fori_loop(0, nvis, far, (jnp.zeros((tq, 1), F32), jnp.zeros((tq, HEAD_DIM), F32)))
        dq_ref[...] = (dq * scale).astype(BF16)

        @pl.when(i == nq - 1)
        def _():
            dko_ref[...] = dk_ref[...].astype(BF16)
            dvo_ref[...] = dv_ref[...].astype(BF16)

    def tile(c0):
        return pl.BlockSpec((tq, HEAD_DIM), lambda b, h, i: (b * nq + i, c0 + h))

    def whole(c0):
        return pl.BlockSpec((seq, HEAD_DIM), lambda b, h, i: (b, c0 + h))

    return _pallas(
        body, name="sba_bwd", grid=(nb, N_HEADS, nq),
        in_specs=[tile(Q_COL), whole(K_COL), whole(V_COL), tile(GA_COL), tile(0),
                  tile(CONV_W // HEAD_DIM)],
        out_specs=[tile(0), whole(0), whole(0), tile(0)],
        out_shape=[jax.ShapeDtypeStruct((t, ATT_W), BF16)] * 4,
        scratch_shapes=[pltpu.VMEM((nwin, tq, wk), F32), pltpu.VMEM((nwin, tq, wk), F32),
                        pltpu.VMEM((seq, HEAD_DIM), F32), pltpu.VMEM((seq, HEAD_DIM), F32)],
        compiler_params=_params(3),
    )(proj, proj, proj, proj, o, dycat)


CT = 512
PH = 8
XRC = 32
X_SHIFTS = tuple(s for s in range(PH - SSM_K + 1, PH))
D_SHIFTS = tuple(range(1, SSM_K))
XBC_BLK = D_INNER // CT


def _softplus(x):
    return jnp.maximum(x, 0.0) + jnp.log(1.0 + jnp.exp(-jnp.abs(x)))


def _dt_fwd(proj, dt_bias, *, tm=512):
    t = proj.shape[0]

    def body(p_ref, b_ref, o_ref):
        o_ref[...] = _softplus(p_ref[...] + b_ref[...])

    return _pallas(
        body, name="dt_fwd", grid=(t // tm,),
        in_specs=[pl.BlockSpec((tm, LANE), lambda i: (i, 0)), pl.BlockSpec((1, LANE), lambda i: (0, 0))],
        out_specs=pl.BlockSpec((tm, LANE), lambda i: (i, 0)),
        out_shape=jax.ShapeDtypeStruct((t, LANE), F32), compiler_params=_params(1),
    )(proj, dt_bias)


def _dt_bwd(proj, dt_bias, ddt, dproj, *, tm=512):
    t = proj.shape[0]
    wide = IN_ODD_PAD - D_INNER - XBC

    def body(p_ref, b_ref, d_ref, dp_any, o_ref, db_ref):
        i = pl.program_id(0)
        lanes = lax.broadcasted_iota(jnp.int32, (tm, LANE), 1)
        dr = jnp.where(lanes < SSM_H, d_ref[...] * _sigmoid(p_ref[...] + b_ref[...]), 0.0)
        o_ref[:, 0:LANE] = dr.astype(BF16)
        o_ref[:, LANE:wide] = jnp.zeros((tm, wide - LANE), BF16)

        @pl.when(i == 0)
        def _():
            db_ref[...] = jnp.zeros_like(db_ref)

        db_ref[...] += jnp.sum(dr, axis=0, keepdims=True)

    vec = pl.BlockSpec((1, LANE), lambda i: (0, 0))
    row = pl.BlockSpec((tm, LANE), lambda i: (i, 0))
    return _pallas(
        body, name="dt_bwd", grid=(t // tm,),
        in_specs=[pl.BlockSpec((tm, LANE), lambda i: (i, 0)), vec, row, pl.BlockSpec(memory_space=pl.ANY)],
        out_specs=[pl.BlockSpec((tm, wide), lambda i: (i, (D_INNER + XBC) // wide)), vec],
        out_shape=[jax.ShapeDtypeStruct(dproj.shape, dproj.dtype), jax.ShapeDtypeStruct((1, LANE), F32)],
        input_output_aliases={3: 0}, compiler_params=_params(1),
    )(proj, dt_bias, ddt, dproj)


def _xconv_fwd(proj, conv_w, conv_b, seq, *, tm=512):
    t = proj.shape[0]
    tps = seq // tm
    hb = tm // PH

    def body(x_ref, h_ref, w_ref, b_ref, o_ref, sh_ref):
        i = pl.program_id(1)
        sh_ref[0, 0:PH, :] = _f32(h_ref[...]) * jnp.where(i % tps == 0, 0.0, 1.0)
        sh_ref[0, PH:PH + tm, :] = _f32(x_ref[...])
        _make_shifts(sh_ref, tm, X_SHIFTS)

        def chunk(ci, carry):
            r0 = pl.multiple_of(ci * XRC, XRC)
            acc = jnp.zeros((XRC, CT), F32) + b_ref[...]
            for k in range(SSM_K):
                acc = acc + w_ref[k:k + 1, :] * _shifted(sh_ref, r0, PH - SSM_K + 1 + k, XRC)
            o_ref[pl.ds(r0, XRC), :] = acc * _sigmoid(acc)
            return carry

        lax.fori_loop(0, tm // XRC, chunk, 0, unroll=8)

    return _pallas(
        body, name="xconv_fwd", grid=(XBC // CT, t // tm),
        in_specs=[pl.BlockSpec((tm, CT), lambda j, i: (i, XBC_BLK + j)),
                  pl.BlockSpec((PH, CT), lambda j, i: (jnp.maximum(i * hb - 1, 0), XBC_BLK + j)),
                  pl.BlockSpec((PH, CT), lambda j, i: (0, j)),
                  pl.BlockSpec((1, CT), lambda j, i: (0, j))],
        out_specs=pl.BlockSpec((tm, CT), lambda j, i: (i, j)),
        out_shape=jax.ShapeDtypeStruct((t, XBC), F32),
        scratch_shapes=[pltpu.VMEM((SUB, tm + PH, CT), F32)], compiler_params=_params(2),
    )(proj, proj, conv_w, conv_b)


def _xconv_bwd(proj, dxc, conv_w, conv_b, dproj, seq, *, tm=512):
    t = proj.shape[0]
    tps = seq // tm
    hb = tm // PH
    nhb = t // PH
    te = tm + PH

    def fold(v):
        out = v[0:SUB]
        for q in range(1, v.shape[0] // SUB):
            out = out + v[q * SUB:(q + 1) * SUB]
        return out

    def body(x_ref, p_ref, n_ref, d_ref, nd_ref, w_ref, b_ref, dp_any, dx_ref, dw_ref, db_ref, sx_ref, sd_ref):
        i = pl.program_id(1)
        first = i % tps == 0
        last = i % tps == tps - 1

        @pl.when(i == 0)
        def _():
            dw_ref[...] = jnp.zeros_like(dw_ref)
            db_ref[...] = jnp.zeros_like(db_ref)

        sx_ref[0, 0:PH, :] = _f32(p_ref[...]) * jnp.where(first, 0.0, 1.0)
        sx_ref[0, PH:PH + tm, :] = _f32(x_ref[...])
        sx_ref[0, PH + tm:PH + te, :] = _f32(n_ref[...])
        _make_shifts(sx_ref, te, X_SHIFTS)

        def dv_of(r0, rows, dy):
            acc = jnp.zeros((rows, CT), F32) + b_ref[...]
            for k in range(SSM_K):
                acc = acc + w_ref[k:k + 1, :] * _shifted(sx_ref, r0, PH - SSM_K + 1 + k, rows)
            return dy * _dsilu(acc, _sigmoid(acc))

        def dv_chunk(ci, carry):
            r0 = pl.multiple_of(ci * XRC, XRC)
            dv = dv_of(r0, XRC, d_ref[pl.ds(r0, XRC), :])
            sd_ref[0, pl.ds(r0, XRC), :] = dv
            db_ref[...] += fold(dv)
            return carry

        lax.fori_loop(0, tm // XRC, dv_chunk, 0, unroll=8)
        sd_ref[0, tm:te, :] = dv_of(tm, PH, nd_ref[...]) * jnp.where(last, 0.0, 1.0)
        _make_shifts(sd_ref, tm, D_SHIFTS)

        def tap_chunk(ci, carry):
            r0 = pl.multiple_of(ci * XRC, XRC)
            dx = jnp.zeros((XRC, CT), F32)
            for k in range(SSM_K):
                dx = dx + w_ref[k:k + 1, :] * _shifted(sd_ref, r0, SSM_K - 1 - k, XRC)
            dx_ref[pl.ds(r0, XRC), :] = dx.astype(BF16)
            dv = sd_ref[0, pl.ds(r0, XRC), :]
            for k in range(SSM_K):
                dw_ref[k * SUB:(k + 1) * SUB, :] += fold(dv * _shifted(sx_ref, r0, PH - SSM_K + 1 + k, XRC))
            return carry

        lax.fori_loop(0, tm // XRC, tap_chunk, 0, unroll=8)

    return _pallas(
        body, name="xconv_bwd", grid=(XBC // CT, t // tm),
        in_specs=[pl.BlockSpec((tm, CT), lambda j, i: (i, XBC_BLK + j)),
                  pl.BlockSpec((PH, CT), lambda j, i: (jnp.maximum(i * hb - 1, 0), XBC_BLK + j)),
                  pl.BlockSpec((PH, CT), lambda j, i: (jnp.minimum((i + 1) * hb, nhb - 1), XBC_BLK + j)),
                  pl.BlockSpec((tm, CT), lambda j, i: (i, j)),
                  pl.BlockSpec((PH, CT), lambda j, i: (jnp.minimum((i + 1) * hb, nhb - 1), j)),
                  pl.BlockSpec((PH, CT), lambda j, i: (0, j)),
                  pl.BlockSpec((1, CT), lambda j, i: (0, j)),
                  pl.BlockSpec(memory_space=pl.ANY)],
        out_specs=[pl.BlockSpec((tm, CT), lambda j, i: (i, XBC_BLK + j)),
                   pl.BlockSpec((PH * SUB, CT), lambda j, i: (0, j)),
                   pl.BlockSpec((SUB, CT), lambda j, i: (0, j))],
        out_shape=[jax.ShapeDtypeStruct(dproj.shape, dproj.dtype), jax.ShapeDtypeStruct((PH * SUB, XBC), F32),
                   jax.ShapeDtypeStruct((SUB, XBC), F32)],
        scratch_shapes=[pltpu.VMEM((SUB, tm + 2 * PH, CT), F32), pltpu.VMEM((SUB, te, CT), F32)],
        input_output_aliases={7: 0}, compiler_params=_params(2),
    )(proj, proj, proj, dxc, dxc, conv_w, conv_b, dproj)


def _ssd_common(xbc, dt, alog, ex):
    L = CHUNK
    a = -jnp.exp(alog)
    la = dt * a
    li = lax.broadcasted_iota(jnp.int32, (L, L), 0)
    si = lax.broadcasted_iota(jnp.int32, (L, L), 1)
    lower = si <= li
    tri = jnp.where(lower, 1.0, 0.0).astype(BF16)
    cs = _xdot(tri, la)
    cst = _dotx(la, tri, (((0,), (1,)), ((), ())))
    csl = cs[L - 1:L, :]
    ecs_x = _dotx2(jnp.exp(cs)[:, 0:SSM_H], ex)
    tail_x = _dotx2(jnp.exp(csl - cs)[:, 0:SSM_H], ex)
    dt_x = _dotx2(dt[:, 0:SSM_H], ex)
    return a, la, lower, tri, cs, cst, ecs_x, tail_x, dt_x


def _ssd_fwd(xbc_c, dt, a_log, ex, nb, seq):
    t = xbc_c.shape[0]
    L = CHUNK
    nc = seq // L
    GW = SSM_R * SSM_P

    def body(x_ref, dt_ref, al_ref, ex_ref, y_ref, st_ref, state):
        c = pl.program_id(1)

        @pl.when(c == 0)
        def _():
            state[...] = jnp.zeros_like(state)

        st_ref[0] = state[...]
        xbc = x_ref[...]
        _, _, lower, _, cs, cst, ecs_x, tail_x, dt_x = _ssd_common(xbc, dt_ref[...], al_ref[...], ex_ref[...])
        xd = xbc[:, 0:D_INNER] * dt_x
        xdb = xd.astype(BF16)
        xt = (xd * tail_x).astype(BF16)
        el_x = ecs_x[L - 1:L, :]
        for g in range(SSM_G):
            bg = xbc[:, D_INNER + g * SSM_N:D_INNER + (g + 1) * SSM_N].astype(BF16)
            cg = xbc[:, D_INNER + (SSM_G + g) * SSM_N:D_INNER + (SSM_G + g + 1) * SSM_N].astype(BF16)
            cb = _dot(cg, bg, NT)
            sg = state[:, g * GW:(g + 1) * GW]
            ys = _dot(cg, sg) * ecs_x[:, g * GW:(g + 1) * GW]
            for r in range(SSM_R):
                h = g * SSM_R + r
                seg = cs[:, h:h + 1] - cst[h:h + 1, :]
                dec = jnp.exp(jnp.where(lower, seg, -1e30))
                yh = _dot(cb * dec, xdb[:, h * SSM_P:(h + 1) * SSM_P])
                y_ref[:, h * SSM_P:(h + 1) * SSM_P] = yh + ys[:, r * SSM_P:(r + 1) * SSM_P]
            state[:, g * GW:(g + 1) * GW] = sg * el_x[:, g * GW:(g + 1) * GW] + _dot(bg, xt[:, g * GW:(g + 1) * GW], TN)

    return _pallas(
        body, name="ssd_fwd", grid=(nb, nc),
        in_specs=[pl.BlockSpec((L, XBC), lambda b, c: (b * nc + c, 0)),
                  pl.BlockSpec((L, LANE), lambda b, c: (b * nc + c, 0)),
                  pl.BlockSpec((1, LANE), lambda b, c: (0, 0)),
                  pl.BlockSpec((SSM_H, D_INNER), lambda b, c: (0, 0))],
        out_specs=[pl.BlockSpec((L, D_INNER), lambda b, c: (b * nc + c, 0)),
                   pl.BlockSpec((1, SSM_N, D_INNER), lambda b, c: (b * nc + c, 0, 0))],
        out_shape=[jax.ShapeDtypeStruct((t, D_INNER), F32),
                   jax.ShapeDtypeStruct((nb * nc, SSM_N, D_INNER), F32)],
        scratch_shapes=[pltpu.VMEM((SSM_N, D_INNER), F32)], compiler_params=_params(2),
    )(xbc_c, dt, a_log, ex)


def _ssd_bwd(xbc_c, dt, a_log, ex, ext, states, dy, d_x, nb, seq):
    t = xbc_c.shape[0]
    L = CHUNK
    nc = seq // L
    GW = SSM_R * SSM_P

    def body(x_ref, dt_ref, al_ref, ex_ref, ext_ref, st_ref, dy_ref, sk_ref, dx_ref, ddt_ref, da_ref,
             dstate, dxd, yd, lastv):
        b = pl.program_id(0)
        c = pl.program_id(1)

        @pl.when(c == 0)
        def _():
            dstate[...] = jnp.zeros_like(dstate)

        @pl.when(jnp.logical_and(b == 0, c == 0))
        def _():
            da_ref[...] = jnp.zeros_like(da_ref)

        xbc = x_ref[...]
        dtv = dt_ref[...]
        ex_t = ext_ref[...]
        a, la, lower, tri, cs, cst, ecs_x, tail_x, dt_x = _ssd_common(xbc, dtv, al_ref[...], ex_ref[...])
        xs = xbc[:, 0:D_INNER]
        xd = xs * dt_x
        xdb = xd.astype(BF16)
        dyv = dy_ref[...]
        dyb = dyv.astype(BF16)
        dys = dyv * ecs_x
        xt = xd * tail_x
        el_x = ecs_x[L - 1:L, :]
        lane = lax.broadcasted_iota(jnp.int32, (L, LANE), 1)
        sub = lax.broadcasted_iota(jnp.int32, (LANE, L), 0)
        row_part = jnp.zeros((L, LANE), F32)
        col_part = jnp.zeros((LANE, L), F32)
        for g in range(SSM_G):
            gs = slice(g * GW, (g + 1) * GW)
            bcol = slice(D_INNER + g * SSM_N, D_INNER + (g + 1) * SSM_N)
            ccol = slice(D_INNER + (SSM_G + g) * SSM_N, D_INNER + (SSM_G + g + 1) * SSM_N)
            bg = xbc[:, bcol].astype(BF16)
            cg = xbc[:, ccol].astype(BF16)
            cb = _dot(cg, bg, NT)
            sg = st_ref[0, :, gs]
            dsg = dstate[:, gs]
            dc = _dot(dys[:, gs], sg, NT)
            db = _dot(xt[:, gs], dsg, NT)
            dx_state = tail_x[:, gs] * _dot(bg, dsg)
            tail_part = xd[:, gs] * dx_state
            yd[:, gs] = dys[:, gs] * _dot(cg, sg) - tail_part
            last = jnp.sum(tail_part, axis=0, keepdims=True) + el_x[:, gs] * jnp.sum(dsg * sg, axis=0, keepdims=True)
            lastv[:, gs] = jnp.broadcast_to(last, (8, GW))
            dcb = jnp.zeros((L, L), F32)
            for r in range(SSM_R):
                h = g * SSM_R + r
                hs = slice(h * SSM_P, (h + 1) * SSM_P)
                seg = cs[:, h:h + 1] - cst[h:h + 1, :]
                dec = jnp.exp(jnp.where(lower, seg, -1e30))
                m = cb * dec
                dm = _dot(dyb[:, hs], xdb[:, hs], NT)
                dcb = dcb + dm * dec
                e = dm * m
                row_part = row_part + jnp.where(lane == h, jnp.sum(e, axis=1, keepdims=True), 0.0)
                col_part = col_part + jnp.where(sub == h, jnp.sum(e, axis=0, keepdims=True), 0.0)
                dxd[:, hs] = _dot(m, dyb[:, hs], TN) + dx_state[:, r * SSM_P:(r + 1) * SSM_P]
            dx_ref[:, bcol] = db + _dot(dcb, cg, TN)
            dx_ref[:, ccol] = dc + _dot(dcb, bg)
            dstate[:, gs] = dsg * el_x[:, gs] + _dot(cg, dys[:, gs], TN)
        dxv = dxd[...]
        dx_ref[:, 0:D_INNER] = dxv * dt_x + dyv * sk_ref[...]
        ddt_x = _dotx(dxv * xs, ex_t)
        yst = _dotx(yd[...], ex_t)
        lst = _dotx(lastv[...], ex_t)[0:1, :]
        rows = lax.broadcasted_iota(jnp.int32, (L, LANE), 0)
        dcs = row_part - col_part.T + yst + jnp.where(rows == L - 1, lst, 0.0)
        li = lax.broadcasted_iota(jnp.int32, (L, L), 0)
        si = lax.broadcasted_iota(jnp.int32, (L, L), 1)
        upper = jnp.where(si >= li, 1.0, 0.0).astype(BF16)
        dla = _xdot(upper, dcs)
        ddt_ref[...] = dla * a + ddt_x
        da_ref[...] += jnp.sum(dla * dtv, axis=0, keepdims=True)

    def row(w):
        return pl.BlockSpec((L, w), lambda b, c: (b * nc + nc - 1 - c, 0))

    return _pallas(
        body, name="ssd_bwd", grid=(nb, nc),
        in_specs=[row(XBC), row(LANE), pl.BlockSpec((1, LANE), lambda b, c: (0, 0)),
                  pl.BlockSpec((SSM_H, D_INNER), lambda b, c: (0, 0)),
                  pl.BlockSpec((D_INNER, LANE), lambda b, c: (0, 0)),
                  pl.BlockSpec((1, SSM_N, D_INNER), lambda b, c: (b * nc + nc - 1 - c, 0, 0)),
                  row(D_INNER), pl.BlockSpec((1, D_INNER), lambda b, c: (0, 0))],
        out_specs=[row(XBC), row(LANE), pl.BlockSpec((1, LANE), lambda b, c: (0, 0))],
        out_shape=[jax.ShapeDtypeStruct((t, XBC), F32), jax.ShapeDtypeStruct((t, LANE), F32),
                   jax.ShapeDtypeStruct((1, LANE), F32)],
        scratch_shapes=[pltpu.VMEM((SSM_N, D_INNER), F32), pltpu.VMEM((L, D_INNER), F32),
                        pltpu.VMEM((L, D_INNER), F32), pltpu.VMEM((8, D_INNER), F32)],
        compiler_params=_params(2),
    )(xbc_c, dt, a_log, ex, ext, states, dy, d_x)


def _group_rms(y2):
    gw = D_INNER // SSM_G
    parts = []
    for g in range(SSM_G):
        v = y2[:, g * gw:(g + 1) * gw]
        r = lax.rsqrt(jnp.mean(v * v, axis=1, keepdims=True) + EPS)
        parts.append(jnp.broadcast_to(r, v.shape))
    return jnp.concatenate(parts, axis=1)


def _gate_fwd(y, xbc_c, proj, d_x, gn_w, *, tm=256):
    t = y.shape[0]

    def body(y_ref, x_ref, z_ref, d_ref, w_ref, o_ref):
        y1 = y_ref[...] + d_ref[...] * x_ref[...]
        zv = _f32(z_ref[...])
        y2 = y1 * zv * _sigmoid(zv)
        o_ref[...] = (y2 * _group_rms(y2) * w_ref[...]).astype(BF16)

    row = pl.BlockSpec((tm, D_INNER), lambda i: (i, 0))
    vec = pl.BlockSpec((1, D_INNER), lambda i: (0, 0))
    return _pallas(
        body, name="gate_fwd", grid=(t // tm,), in_specs=[row, row, row, vec, vec], out_specs=row,
        out_shape=jax.ShapeDtypeStruct((t, D_INNER), BF16), compiler_params=_params(1),
    )(y, xbc_c, proj, d_x, gn_w)


def _gate_bwd(dyg, y, xbc_c, proj, d_x, gn_w, *, tm=256):
    t = y.shape[0]
    gw = D_INNER // SSM_G

    def body(dg_ref, y_ref, x_ref, z_ref, d_ref, w_ref, dy_ref, dz_ref, dw_ref, dd_ref):
        i = pl.program_id(0)
        xv = x_ref[...]
        dxv = d_ref[...]
        y1 = y_ref[...] + dxv * xv
        zv = _f32(z_ref[...])
        sz = _sigmoid(zv)
        y2 = y1 * zv * sz
        rr = _group_rms(y2)
        xh = y2 * rr
        dg = _f32(dg_ref[...])
        gq = dg * w_ref[...]
        prod = gq * xh
        means = []
        for g in range(SSM_G):
            mg = jnp.mean(prod[:, g * gw:(g + 1) * gw], axis=1, keepdims=True)
            means.append(jnp.broadcast_to(mg, (tm, gw)))
        dy2 = rr * (gq - xh * jnp.concatenate(means, axis=1))
        dy1 = dy2 * zv * sz
        dy_ref[...] = dy1
        dz_ref[...] = (dy2 * y1 * _dsilu(zv, sz)).astype(BF16)

        @pl.when(i == 0)
        def _():
            dw_ref[...] = jnp.zeros_like(dw_ref)
            dd_ref[...] = jnp.zeros_like(dd_ref)

        dw_ref[...] += jnp.sum(dg * xh, axis=0, keepdims=True)
        dd_ref[...] += jnp.sum(dy1 * xv, axis=0, keepdims=True)

    row = pl.BlockSpec((tm, D_INNER), lambda i: (i, 0))
    vec = pl.BlockSpec((1, D_INNER), lambda i: (0, 0))
    return _pallas(
        body, name="gate_bwd", grid=(t // tm,), in_specs=[row, row, row, row, vec, vec],
        out_specs=[row, row, vec, vec],
        out_shape=[jax.ShapeDtypeStruct((t, D_INNER), F32),
                   jax.ShapeDtypeStruct((t, IN_ODD_PAD), BF16), jax.ShapeDtypeStruct((1, D_INNER), F32),
                   jax.ShapeDtypeStruct((1, D_INNER), F32)],
        compiler_params=_params(1),
    )(dyg, y, xbc_c, proj, d_x, gn_w)


ANY = pl.BlockSpec(memory_space=pl.ANY)


def _remote(src, dst, sems, k, to):
    send_sems, recv_sems = sems
    return pltpu.make_async_remote_copy(src_ref=src, dst_ref=dst, send_sem=send_sems.at[k], recv_sem=recv_sems.at[k],
                                        device_id=to, device_id_type=MESH)


NCHIP = 4


def _gathered_shape(kind, shard):
    r, n = shard.shape
    shape = {"cols": (r, NCHIP * n), "slab": (NCHIP, r, n), "rows": (NCHIP * r, n)}[kind]
    return jax.ShapeDtypeStruct(shape, shard.dtype)


def _gather_plan(kinds, shards, outs, sems, small=None):
    ici_s, ici_r, d2d_s, d2d_r = sems
    nw = len(shards)
    per = nw + (small is not None)

    def place():
        x, y, c = lax.axis_index("x"), lax.axis_index("y"), lax.axis_index("c")
        return 2 * x + y, c, (x, y, 1 - c), [(1 - x, y), (x, 1 - y), (1 - x, 1 - y)]

    def region(j, chip, half):
        r, n = shards[j].shape
        h = r // 2
        if kinds[j] == "cols":
            return outs[j].at[pl.ds(half * h, h), pl.ds(pl.multiple_of(chip * n, LANE), n)]
        if kinds[j] == "slab":
            return outs[j].at[chip, pl.ds(half * h, h), :]
        return outs[j].at[pl.ds(chip * r + half * h, h), :]

    def my_sends(me, c, peers):
        cps = []
        for k, (px, py) in enumerate(peers):
            for j in range(nw):
                h = shards[j].shape[0] // 2
                cps.append(_remote(shards[j].at[pl.ds(c * h, h), :], region(j, me, c), (ici_s, ici_r), per * k + j, (px, py, c)))
            if small is not None:
                cps.append(_remote(small[0], small[1].at[me], (ici_s, ici_r), per * k + nw, (px, py, c)))
        return cps

    def start():
        me, c, _, peers = place()
        for cp in my_sends(me, c, peers):
            cp.start()

    def finish():
        me, c, sib, peers = place()
        fwds = []
        for k, (px, py) in enumerate(peers):
            q = 2 * px + py
            for j in range(nw):
                d = region(j, q, c)
                _remote(d, d, (ici_s, ici_r), per * k + j, (px, py, c)).wait_recv()
                fwds.append(_remote(d, d, (d2d_s, d2d_r), nw * k + j, sib))
                fwds[-1].start()
            if small is not None:
                _remote(small[0], small[1].at[q], (ici_s, ici_r), per * k + nw, (px, py, c)).wait_recv()
        for k, (px, py) in enumerate(peers):
            for j in range(nw):
                d = region(j, 2 * px + py, 1 - c)
                _remote(d, d, (d2d_s, d2d_r), nw * k + j, sib).wait_recv()
        for cp in my_sends(me, c, peers) + fwds:
            cp.wait_send()

    return start, finish


def _gather_sems(nw, with_small):
    n_ici = 3 * (nw + with_small)
    return [pltpu.SemaphoreType.DMA((n_ici,)), pltpu.SemaphoreType.DMA((n_ici,)),
            pltpu.SemaphoreType.DMA((3 * nw,)), pltpu.SemaphoreType.DMA((3 * nw,))]


def _gather_shards(kinds, shards, small):
    nw = len(shards)

    def body(*refs):
        ins, sm, outs, osm, sems = refs[:nw], refs[nw], refs[nw + 1:2 * nw + 1], refs[2 * nw + 1], refs[2 * nw + 2:]
        start, finish = _gather_plan(kinds, ins, outs, sems, small=(sm, osm))
        start()
        finish()

    return _pallas(
        body, name="gather_shards", in_specs=[ANY] * (nw + 1), out_specs=[ANY] * (nw + 1),
        out_shape=[_gathered_shape(kd, s) for kd, s in zip(kinds, shards)]
        + [jax.ShapeDtypeStruct((NCHIP,) + small.shape, small.dtype)],
        scratch_shapes=_gather_sems(nw, 1),
    )(*shards, small)


def _place_cols(full, shard, *, name, tr=256):
    r, n = shard.shape

    def body(p_ref, full_any, s_ref, o_ref):
        o_ref[...] = s_ref[...]

    return _pallas(
        body, name=name,
        grid_spec=pltpu.PrefetchScalarGridSpec(
            num_scalar_prefetch=1, grid=(r // tr,),
            in_specs=[pl.BlockSpec(memory_space=pl.ANY), pl.BlockSpec((tr, n), lambda i, p: (i, 0))],
            out_specs=pl.BlockSpec((tr, n), lambda i, p: (i, p[0]))),
        out_shape=jax.ShapeDtypeStruct(full.shape, full.dtype), input_output_aliases={1: 0},
        compiler_params=_params(1),
    )(_chip_index(), full, shard)


def _matmul_with_gather(a, b, kinds, shards, *, out_dtype, bm, bn, name):
    (m, k), n = a.shape, b.shape[1]
    nw = len(shards)
    nj, ni = n // bn, m // bm

    def body(*refs):
        a_ref, b_ref, ins, o_ref = refs[0], refs[1], refs[2:2 + nw], refs[2 + nw]
        outs, sems = refs[3 + nw:3 + 2 * nw], refs[3 + 2 * nw:]
        start, finish = _gather_plan(kinds, ins, outs, sems)
        j, i = pl.program_id(0), pl.program_id(1)

        @pl.when(jnp.logical_and(j == 0, i == 0))
        def _():
            start()

        o_ref[...] = _dot(a_ref[...], b_ref[...]).astype(out_dtype)

        @pl.when(jnp.logical_and(j == nj - 1, i == ni - 1))
        def _():
            finish()

    return _pallas(
        body, name=name, grid=(nj, ni),
        in_specs=[pl.BlockSpec((bm, k), lambda j, i: (i, 0)), pl.BlockSpec((k, bn), lambda j, i: (0, j))] + [ANY] * nw,
        out_specs=[pl.BlockSpec((bm, bn), lambda j, i: (i, j))] + [ANY] * nw,
        out_shape=[jax.ShapeDtypeStruct((m, n), out_dtype)] + [_gathered_shape(kd, s) for kd, s in zip(kinds, shards)],
        scratch_shapes=_gather_sems(nw, 0), compiler_params=_params(2),
    )(a, b, *shards)


def _other_half(a, c):
    axis = a.ndim - 2
    h = a.shape[axis] // 2
    rows = pl.ds(pl.multiple_of((1 - c) * h, 8), h)
    return a.at[rows, :] if a.ndim == 2 else a.at[:, rows, :]


def _half_shape(a):
    axis = a.ndim - 2
    return jax.ShapeDtypeStruct(a.shape[:axis] + (a.shape[axis] // 2,) + a.shape[axis + 1:], a.dtype)


def _pair_swap(bigs, *, name):
    nb = len(bigs)

    def body(*refs):
        ins, outs, send_sems, recv_sems = refs[:nb], refs[nb:2 * nb], refs[2 * nb], refs[2 * nb + 1]
        x, y, c = lax.axis_index("x"), lax.axis_index("y"), lax.axis_index("c")
        pair = [_remote(_other_half(a, c), q, (send_sems, recv_sems), j, (x, y, 1 - c))
                for j, (a, q) in enumerate(zip(ins, outs))]
        for cp in pair:
            cp.start()
        for cp in pair:
            cp.wait()

    return _pallas(
        body, name=name, in_specs=[ANY] * nb, out_specs=[ANY] * nb, out_shape=[_half_shape(a) for a in bigs],
        scratch_shapes=[pltpu.SemaphoreType.DMA((nb,)), pltpu.SemaphoreType.DMA((nb,))],
    )(*bigs)


def _pair_exchange(bigs, gsmall, grep):
    nb = len(bigs)

    def body(*refs):
        ins, sm, rp = refs[:nb], refs[nb], refs[nb + 1]
        outs, osm, orp = refs[nb + 2:2 * nb + 2], refs[2 * nb + 2], refs[2 * nb + 3]
        pair_s, pair_r, send_sems, recv_sems, local_sems = refs[2 * nb + 4:]
        x, y, c = lax.axis_index("x"), lax.axis_index("y"), lax.axis_index("c")
        me = 4 * x + 2 * y + c
        chip = 2 * x + y
        sib = (x, y, 1 - c)
        pair = [_remote(_other_half(a, c), q, (pair_s, pair_r), j, sib) for j, (a, q) in enumerate(zip(ins, outs))]
        for cp in pair:
            cp.start()
        own = [pltpu.make_async_copy(sm.at[chip], osm.at[me], local_sems.at[0]),
               pltpu.make_async_copy(rp, orp.at[me], local_sems.at[1])]
        for cp in own:
            cp.start()
        peers = []
        for k in range(7):
            fx, fy, fc = ((k + 1) >> 2) & 1, ((k + 1) >> 1) & 1, (k + 1) & 1
            peers.append((1 - x if fx else x, 1 - y if fy else y, 1 - c if fc else c))
        sends = []
        for k, (px, py, pc) in enumerate(peers):
            sends.append(_remote(sm.at[2 * px + py], osm.at[me], (send_sems, recv_sems), 2 * k, (px, py, pc)))
            sends.append(_remote(rp, orp.at[me], (send_sems, recv_sems), 2 * k + 1, (px, py, pc)))
        for cp in sends:
            cp.start()
        for k, (px, py, pc) in enumerate(peers):
            slot = 4 * px + 2 * py + pc
            _remote(sm.at[chip], osm.at[slot], (send_sems, recv_sems), 2 * k, (px, py, pc)).wait_recv()
            _remote(rp, orp.at[slot], (send_sems, recv_sems), 2 * k + 1, (px, py, pc)).wait_recv()
        for cp in pair:
            cp.wait_recv()
        for cp in pair + sends:
            cp.wait_send()
        for cp in own:
            cp.wait()

    return _pallas(
        body, name="pair_exchange", in_specs=[ANY] * (nb + 2), out_specs=[ANY] * (nb + 2),
        out_shape=[_half_shape(a) for a in bigs]
        + [jax.ShapeDtypeStruct((8,) + gsmall.shape[1:], F32), jax.ShapeDtypeStruct((8,) + grep.shape, F32)],
        scratch_shapes=[pltpu.SemaphoreType.DMA((max(nb, 1),)), pltpu.SemaphoreType.DMA((max(nb, 1),)),
                        pltpu.SemaphoreType.DMA((14,)), pltpu.SemaphoreType.DMA((14,)),
                        pltpu.SemaphoreType.DMA((2,))],
    )(*bigs, gsmall, grep)


def _core_index():
    return lax.axis_index("c").astype(jnp.int32).reshape(1)


def _half_add(full, other, *, axis, block, name):
    nd = full.ndim
    nblk = other.shape[axis] // block[axis]
    grid = tuple(other.shape[d] // block[d] for d in range(nd))

    def body(c_ref, f_ref, o_ref, out_ref):
        out_ref[...] = (f_ref[...] + o_ref[...]).astype(BF16)

    def full_map(*idx):
        ids, c_ref = list(idx[:nd]), idx[nd]
        ids[axis] = ids[axis] + c_ref[0] * nblk
        return tuple(ids)

    def plain_map(*idx):
        return tuple(idx[:nd])

    return _pallas(
        body, name=name,
        grid_spec=pltpu.PrefetchScalarGridSpec(
            num_scalar_prefetch=1, grid=grid,
            in_specs=[pl.BlockSpec(block, full_map), pl.BlockSpec(block, plain_map)],
            out_specs=pl.BlockSpec(block, plain_map)),
        out_shape=jax.ShapeDtypeStruct(other.shape, BF16), compiler_params=_params(nd),
    )(_core_index(), full, other)


NPEER = 3


def _landing_shape(kind, src):
    if kind == "pair":
        return _half_shape(src)
    if kind == "cols":
        return jax.ShapeDtypeStruct((NPEER, src.shape[0], src.shape[1] // NCHIP), src.dtype)
    return jax.ShapeDtypeStruct((NPEER,) + src.shape[1:], src.dtype)


def _chip_plan(kinds, srcs, lands, sems):
    nw = len(srcs)
    chipwise = [j for j in range(nw) if kinds[j] != "pair"]
    pairwise = [j for j in range(nw) if kinds[j] == "pair"]

    def place():
        x, y, c = lax.axis_index("x"), lax.axis_index("y"), lax.axis_index("c")
        return 2 * x + y, c, [(1 - x, y), (x, 1 - y), (1 - x, 1 - y)]

    def pair_copies(c):
        sib = (lax.axis_index("x"), lax.axis_index("y"), 1 - c)
        return [_remote(_other_half(srcs[j], c), lands[j], sems, j, sib) for j in pairwise]

    def piece(j, chip):
        if kinds[j] == "cols":
            n = srcs[j].shape[1] // NCHIP
            return srcs[j].at[:, pl.ds(pl.multiple_of(chip * n, LANE), n)]
        return srcs[j].at[chip]

    def my_sends(c, peers):
        return pair_copies(c) + [_remote(piece(j, 2 * px + py), lands[j].at[k], sems, nw * k + j, (px, py, c))
                                 for k, (px, py) in enumerate(peers) for j in chipwise]

    def start():
        _, c, peers = place()
        for cp in my_sends(c, peers):
            cp.start()

    def finish():
        me, c, peers = place()
        for cp in pair_copies(c):
            cp.wait_recv()
        for k, (px, py) in enumerate(peers):
            for j in chipwise:
                _remote(piece(j, me), lands[j].at[k], sems, nw * k + j, (px, py, c)).wait_recv()
        for cp in my_sends(c, peers):
            cp.wait_send()

    return start, finish


def _chip_sems(nw):
    return [pltpu.SemaphoreType.DMA((NPEER * nw,)), pltpu.SemaphoreType.DMA((NPEER * nw,))]


def _chip_index():
    return (2 * lax.axis_index("x") + lax.axis_index("y")).astype(jnp.int32).reshape(1)


def _chip_sum(own, slots, *, own_block, own_map, block, name):
    npeer = slots.shape[0]
    shape = slots.shape[1:]
    grid = (shape[0] // block[0], shape[1] // block[1])

    def body(p_ref, own_ref, s_ref, o_ref):
        acc = own_ref[...].reshape(block).astype(F32)
        for q in range(npeer):
            acc = acc + s_ref[q].astype(F32)
        o_ref[...] = acc

    return _pallas(
        body, name=name,
        grid_spec=pltpu.PrefetchScalarGridSpec(
            num_scalar_prefetch=1, grid=grid,
            in_specs=[pl.BlockSpec(own_block, own_map),
                      pl.BlockSpec((npeer,) + block, lambda i, j, p: (0, i, j))],
            out_specs=pl.BlockSpec(block, lambda i, j, p: (i, j))),
        out_shape=jax.ShapeDtypeStruct(shape, F32), compiler_params=_params(2),
    )(_chip_index(), own, slots)


def _pair_share(r_in0, r_in1, r_out0, r_out1):
    def body(a0, a1, b0, b1, g0, g1, h0, h1, send_sems, recv_sems):
        x, y, c = lax.axis_index("x"), lax.axis_index("y"), lax.axis_index("c")
        sib = (x, y, 1 - c)
        sends = [_remote(s, d, (send_sems, recv_sems), j, sib)
                 for j, (s, d) in enumerate(zip([a0, a1, b0, b1], [g0, g1, h0, h1]))]
        for cp in sends:
            cp.start()
        for cp in sends:
            cp.wait()

    return _pallas(
        body, name="pair_share", in_specs=[ANY] * 4, out_specs=[ANY] * 4,
        out_shape=[jax.ShapeDtypeStruct(r.shape, F32) for r in (r_in0, r_in1, r_out0, r_out1)],
        scratch_shapes=[pltpu.SemaphoreType.DMA((4,)), pltpu.SemaphoreType.DMA((4,))],
    )(r_in0, r_in1, r_out0, r_out1)


def _adam_math(g, w, m, v):
    c1 = 1.0 - ADAM_B1 ** ADAM_STEP
    c2 = 1.0 - ADAM_B2 ** ADAM_STEP
    m2 = ADAM_B1 * m + (1.0 - ADAM_B1) * g
    v2 = ADAM_B2 * v + (1.0 - ADAM_B2) * (g * g)
    delta = -ADAM_LR * ((m2 / c1) / (jnp.sqrt(v2 / c2) + ADAM_EPS) + ADAM_WD * w)
    return delta, m2, v2


def _adamw_nat(g_mine, g_sib, w, m, v, *, name, tr):
    rows, cw = w.shape
    nt = g_mine.shape[0] // tr

    def body(c_ref, gm_ref, gs_ref, w_ref, m_ref, v_ref, go_ref, d_ref, nm_ref, nv_ref):
        mine = pl.program_id(0) // nt == c_ref[0]
        gv = jnp.where(mine, gm_ref[...], gs_ref[...])[:, 0:cw]
        delta, m2, v2 = _adam_math(gv, w_ref[...], m_ref[...], v_ref[...])
        go_ref[...] = gv
        d_ref[...] = delta
        nm_ref[...] = m2
        nv_ref[...] = v2

    def mine_map(i, c_ref):
        return (jnp.where(i // nt == c_ref[0], i % nt, 0), 0)

    def sib_map(i, c_ref):
        return (jnp.where(i // nt == c_ref[0], 0, i % nt), 0)

    row = pl.BlockSpec((tr, cw), lambda i, c_ref: (i, 0))
    gspec = (tr, g_mine.shape[1])
    out = jax.ShapeDtypeStruct((rows, cw), F32)
    return _pallas(
        body, name=name,
        grid_spec=pltpu.PrefetchScalarGridSpec(
            num_scalar_prefetch=1, grid=(rows // tr,),
            in_specs=[pl.BlockSpec(gspec, mine_map), pl.BlockSpec(gspec, sib_map), row, row, row],
            out_specs=[row, row, row, row]),
        out_shape=[out, out, out, out], compiler_params=_params(1),
    )(_core_index(), g_mine, g_sib, w, m, v)


def _adamw(slots, w, m, v, *, name, tr):
    nd, rows, _ = slots.shape
    c1 = 1.0 - ADAM_B1 ** ADAM_STEP
    c2 = 1.0 - ADAM_B2 ** ADAM_STEP

    def body(s_ref, w_ref, m_ref, v_ref, g_ref, d_ref, nm_ref, nv_ref):
        g = s_ref[0]
        for d in range(1, nd):
            g = g + s_ref[d]
        m2 = ADAM_B1 * m_ref[...] + (1.0 - ADAM_B1) * g
        v2 = ADAM_B2 * v_ref[...] + (1.0 - ADAM_B2) * (g * g)
        g_ref[...] = g
        nm_ref[...] = m2
        nv_ref[...] = v2
        d_ref[...] = -ADAM_LR * ((m2 / c1) / (jnp.sqrt(v2 / c2) + ADAM_EPS) + ADAM_WD * w_ref[...])

    row = pl.BlockSpec((tr, LANE), lambda i: (i, 0))
    out = jax.ShapeDtypeStruct((rows, LANE), F32)
    return _pallas(
        body, name=name, grid=(rows // tr,),
        in_specs=[pl.BlockSpec((nd, tr, LANE), lambda i: (0, i, 0)), row, row, row],
        out_specs=[row, row, row, row], out_shape=[out, out, out, out], compiler_params=_params(1),
    )(slots, w, m, v)


def _rows(a):
    return a.reshape(-1, LANE)


def _pad_rows(a, mult):
    pad = (-a.shape[0]) % mult
    return jnp.pad(a, ((0, pad), (0, 0))) if pad else a


def _pack(parts, mult):
    return _pad_rows(jnp.concatenate([_rows(p) for p in parts], axis=0), mult)


def _unpack(slab, shapes):
    out, r0 = [], 0
    for shp in shapes:
        n = 1
        for s in shp:
            n *= s
        r = n // LANE
        out.append(slab[r0:r0 + r].reshape(shp))
        r0 += r
    return out


def _pack_rep(vecs, scal):
    srow = jnp.concatenate([s.reshape(-1) for s in scal] + [jnp.zeros((LANE - 3 * SSM_H,), F32)]).reshape(1, LANE)
    return _pad_rows(jnp.concatenate([_rows(vv) for vv in vecs] + [srow], axis=0), 8)


def _unpack_rep(slab, vec_shapes, scal_shape):
    vecs, r0 = [], 0
    for shp in vec_shapes:
        vecs.append(slab[r0:r0 + 8].reshape(shp))
        r0 += 8
    srow = slab[r0]
    scal = [srow[i * SSM_H:(i + 1) * SSM_H].reshape(scal_shape) for i in range(3)]
    return vecs, scal


def kernel(x, ev_norm_w, ev_w_in, ev_dw_w, ev_dw_b, ev_ln_w, ev_ln_b, ev_w_out, od_norm_w, od_w_in, od_conv_w, od_conv_b, od_dt_bias, od_a_log, od_d, od_gnorm_w, od_w_out, final_norm_w, loss_target, m_ev_norm_w, m_ev_w_in, m_ev_dw_w, m_ev_dw_b, m_ev_ln_w, m_ev_ln_b, m_ev_w_out, m_od_norm_w, m_od_w_in, m_od_conv_w, m_od_conv_b, m_od_dt_bias, m_od_a_log, m_od_d, m_od_gnorm_w, m_od_w_out, m_final_norm_w, v_ev_norm_w, v_ev_w_in, v_ev_dw_w, v_ev_dw_b, v_ev_ln_w, v_ev_ln_b, v_ev_w_out, v_od_norm_w, v_od_w_in, v_od_conv_w, v_od_conv_b, v_od_dt_bias, v_od_a_log, v_od_d, v_od_gnorm_w, v_od_w_out, v_final_norm_w):
    nb, seq, d = x.shape
    t = nb * seq
    nchip = 4
    xf = x.reshape(t, d)
    tgt = loss_target.reshape(t, d)

    big_w = [ev_w_in[0], od_w_in[0], ev_w_out[0], od_w_out[0]]
    small_w = [ev_dw_w[0], od_norm_w[0], od_conv_w[0], od_conv_b[0], od_gnorm_w[0]]
    small_shapes = [a.shape for a in small_w]
    big_b = [a.astype(BF16) for a in big_w]
    small_slab = _pack(small_w, 8)
    w_in0, w_out0, gath_small = _gather_shards(("cols", "rows"), [big_b[0], big_b[2]], small_slab)
    chip = 2 * lax.axis_index("x") + lax.axis_index("y")
    w_in0 = _place_cols(w_in0, big_b[0], name="place_w_in0")
    w_out0 = lax.dynamic_update_slice(w_out0, big_b[2], (chip * big_b[2].shape[0], 0))
    gath_small = lax.dynamic_update_slice(gath_small, small_slab[None], (chip, 0, 0))
    per_chip = [_unpack(gath_small[p], small_shapes) for p in range(nchip)]

    def cat(idx, axis):
        return jnp.concatenate([per_chip[p][idx] for p in range(nchip)], axis=axis)

    dw_w = jnp.pad(cat(0, 1), ((0, HALO - CONF_K), (0, 0)))
    dw_w8 = jnp.repeat(dw_w, SUB, axis=0)
    n1_w = cat(1, 0).reshape(1, d)
    conv_w = jnp.pad(cat(2, 1), ((0, PH - SSM_K), (0, 0)))
    conv_b = cat(3, 0).reshape(1, XBC)
    gn_w = cat(4, 0).reshape(1, D_INNER)

    def lanes(a):
        return jnp.pad(a.reshape(1, -1), ((0, 0), (0, LANE - a.size)))

    dt_bias, a_log = lanes(od_dt_bias), lanes(od_a_log)
    d_x = jnp.repeat(od_d.reshape(-1), SSM_P).reshape(1, D_INNER)
    hid = lax.broadcasted_iota(jnp.int32, (SSM_H, D_INNER), 1) // SSM_P
    ex = (hid == lax.broadcasted_iota(jnp.int32, (SSM_H, D_INNER), 0)).astype(BF16)
    ex_t = jnp.pad(ex.T, ((0, 0), (0, LANE - SSM_H)))
    fn_w = final_norm_w.reshape(1, d)

    n0 = _rms_fwd(xf, ev_norm_w, name="rms_fwd0")
    proj0, w_in1g, w_out1 = _matmul_with_gather(n0, w_in0, ("slab", "rows"), [big_b[1], big_b[3]],
                                                out_dtype=BF16, bm=512, bn=1024, name="in_proj0")
    w_in1g = lax.dynamic_update_slice(w_in1g, big_b[1][None], (chip, 0, 0))
    w_out1 = lax.dynamic_update_slice(w_out1, big_b[3], (chip * big_b[3].shape[0], 0))
    w_in1 = jnp.pad(jnp.concatenate([w_in1g[p] for p in range(nchip)], axis=1),
                    ((0, 0), (0, IN_ODD_PAD - IN_ODD)))
    y_conv, u2 = _conf_fwd(proj0, dw_w8, ev_dw_b, ev_ln_w, ev_ln_b, seq)
    o_att, y_att = _sba_fwd(proj0, nb, seq)
    ycat0 = jnp.concatenate([y_conv, y_att], axis=1)
    h1 = _matmul(ycat0, w_out0, mode="nn", out_dtype=F32, bm=512, bn=d, bk=D_INNER, name="out_proj0", residual=xf)
    n1 = _rms_fwd(h1, n1_w, name="rms_fwd1")
    proj1 = _matmul(n1, w_in1, mode="nn", out_dtype=BF16, bm=512, bn=768, bk=d, name="in_proj1", n_major=True)
    dt_raw = _matmul(n1, w_in1[:, D_INNER + XBC:IN_ODD_PAD], mode="nn", out_dtype=F32, bm=512,
                     bn=IN_ODD_PAD - D_INNER - XBC, bk=d, name="in_proj1_dt")
    xbc_c = _xconv_fwd(proj1, conv_w, conv_b, seq)
    dt = _dt_fwd(dt_raw, dt_bias)
    y_ssd, states = _ssd_fwd(xbc_c, dt, a_log, ex, nb, seq)
    yg = _gate_fwd(y_ssd, xbc_c, proj1, d_x, gn_w)
    h2 = _matmul(yg, w_out1, mode="nn", out_dtype=F32, bm=512, bn=d, bk=D_INNER, name="out_proj1", residual=h1)
    dh2, g_fn, loss_part = _final_loss(h2, fn_w, tgt)

    dyg = _matmul(dh2, w_out1, mode="nt", out_dtype=BF16, bm=512, bn=1024, bk=d, name="d_out_proj1")
    g_w_out1 = _matmul(yg, dh2, mode="tn", out_dtype=F32, bm=1024, bn=d, bk=1024, name="dw_out_proj1")
    dy_ssd, dz, g_gn, g_dx = _gate_bwd(dyg, y_ssd, xbc_c, proj1, d_x, gn_w)
    dxbc_c, ddt, g_a = _ssd_bwd(xbc_c, dt, a_log, ex, ex_t, states, dy_ssd, d_x, nb, seq)
    dproj1, g_conv_w, g_conv_b = _xconv_bwd(proj1, dxbc_c, conv_w, conv_b, dz, seq)
    dproj1, g_dt_bias = _dt_bwd(dt_raw, dt_bias, ddt, dproj1)
    dn1 = _matmul(dproj1, w_in1, mode="nt", out_dtype=BF16, bm=1024, bn=d, bk=1792, name="d_in_proj1")
    g_w_in1 = _matmul(n1, dproj1, mode="tn", out_dtype=F32, bm=d, bn=1792, bk=1024, name="dw_in_proj1")
    dh1, g_n1 = _rms_bwd(dn1, h1, n1_w, dh2, name="rms_bwd1")

    ro = D_INNER // nchip
    n1 = IN_ODD // nchip
    n1p = -(-n1 // LANE) * LANE
    g_w_out1c = g_w_out1.reshape(nchip, ro, d)
    dycat0, q_in1, q_out1 = _matmul(dh1, w_out0, mode="nt", out_dtype=BF16, bm=512, bn=1024, bk=d, name="d_out_proj0",
                                    comm_kinds=("pair", "pair"), comm_srcs=[g_w_in1, g_w_out1c])
    g_w_out0 = _matmul(ycat0, dh1, mode="tn", out_dtype=F32, bm=1024, bn=d, bk=1024, name="dw_out_proj0")
    dq, dk, dv, dga = _sba_bwd(proj0, o_att, dycat0, nb, seq)
    s_in1n = _half_add(g_w_in1, q_in1, axis=0, block=(128, IN_ODD_PAD), name="half_add_in1")
    s_in1 = jnp.stack([jnp.pad(s_in1n[:, p * n1:(p + 1) * n1], ((0, 0), (0, n1p - n1))) for p in range(nchip)])
    s_out1 = _half_add(g_w_out1c, q_out1, axis=1, block=(1, ro // 2, d), name="half_add_out1")
    dpc, g_dw_w, g_dw_b, g_ln_w, g_ln_b, l_in1, l_out1 = _conf_bwd(
        proj0, u2, dycat0, dw_w8, ev_ln_w, ev_ln_b, seq, ("slab", "slab"), [s_in1, s_out1])
    dproj0 = jnp.concatenate([dpc, dq, dk, dv, dga], axis=1)
    g_w_in0 = _matmul(n0, dproj0, mode="tn", out_dtype=F32, bm=d, bn=1792, bk=1024, name="dw_in_proj0")
    g_w_out0c = g_w_out0.reshape(nchip, ro, d)
    q_in0, q_out0 = _pair_swap([g_w_in0, g_w_out0c], name="pair_swap_l0")
    s_in0 = _half_add(g_w_in0, q_in0, axis=0, block=(128, IN_EVEN), name="half_add_in0")
    s_out0 = _half_add(g_w_out0c, q_out0, axis=1, block=(1, ro // 2, d), name="half_add_out0")
    dn0, l_in0, l_out0 = _matmul(dproj0, w_in0, mode="nt", out_dtype=BF16, bm=1024, bn=d, bk=1792, name="d_in_proj0",
                                 comm_kinds=("cols", "slab"), comm_srcs=[s_in0, s_out0])
    grad_x, g_n0 = _rms_bwd(dn0, xf, ev_norm_w, dh1, name="rms_bwd0")

    g_dw_w = g_dw_w.reshape(HALO, SUB, CONV_W).sum(axis=1)[0:CONF_K]
    g_dw_b, g_ln_w, g_ln_b = (a.sum(axis=0, keepdims=True) for a in (g_dw_b, g_ln_w, g_ln_b))
    g_conv_w = g_conv_w.reshape(PH, SUB, XBC).sum(axis=1)[0:SSM_K]
    g_conv_b = g_conv_b.sum(axis=0, keepdims=True)
    a_neg = -jnp.exp(od_a_log.reshape(-1))
    g_a_log = g_a[0, 0:SSM_H] * a_neg
    g_d = g_dx.reshape(SSM_H, SSM_P).sum(axis=1)

    def chip_slab_small(p):
        c0, c1, c2, c3 = CONV_W // nchip, d // nchip, XBC // nchip, D_INNER // nchip
        return _pack([g_dw_w[:, p * c0:(p + 1) * c0], g_n1[0, p * c1:(p + 1) * c1],
                      g_conv_w[:, p * c2:(p + 1) * c2], g_conv_b[0, p * c2:(p + 1) * c2],
                      g_gn[0, p * c3:(p + 1) * c3]], 8)

    gsmall = jnp.stack([chip_slab_small(p) for p in range(nchip)])
    rep_vec_shapes = [ev_norm_w.shape, ev_dw_b.shape, ev_ln_w.shape, ev_ln_b.shape, final_norm_w.shape]
    grep = _pack_rep([g_n0, g_dw_b, g_ln_w, g_ln_b, g_fn], [g_dt_bias[0, 0:SSM_H], g_a_log, g_d])

    ssmall, srep = _pair_exchange([], gsmall, grep)
    r_in0 = _chip_sum(s_in0, l_in0, own_block=(128, IN_EVEN // nchip), own_map=lambda i, j, p: (i, p[0]),
                      block=(128, IN_EVEN // nchip), name="chip_sum_in0")
    r_in1 = _chip_sum(s_in1, l_in1, own_block=(1, 256, n1p), own_map=lambda i, j, p: (p[0], i, 0),
                      block=(256, n1p), name="chip_sum_in1")
    r_out0 = _chip_sum(s_out0, l_out0, own_block=(1, ro // 2, d), own_map=lambda i, j, p: (p[0], 0, 0),
                       block=(ro // 2, d), name="chip_sum_out0")
    r_out1 = _chip_sum(s_out1, l_out1, own_block=(1, ro // 2, d), own_map=lambda i, j, p: (p[0], 0, 0),
                       block=(ro // 2, d), name="chip_sum_out1")
    big_r = [r_in0, r_in1, r_out0, r_out1]
    big_q = _pair_share(*big_r)

    big_m = [m_ev_w_in[0], m_od_w_in[0], m_ev_w_out[0], m_od_w_out[0]]
    big_v = [v_ev_w_in[0], v_od_w_in[0], v_ev_w_out[0], v_od_w_out[0]]
    big_names = ["adamw_in0", "adamw_in1", "adamw_out0", "adamw_out1"]
    out_bigs = [_adamw_nat(gm, gs, w, m, v, name=nm, tr=128)
                for gm, gs, w, m, v, nm in zip(big_r, big_q, big_w, big_m, big_v, big_names)]

    def upd(slots, ws, ms, vs, packer, name, tr):
        return _adamw(slots, packer(ws), packer(ms), packer(vs), name=name, tr=tr)

    small_m = [m_ev_dw_w[0], m_od_norm_w[0], m_od_conv_w[0], m_od_conv_b[0], m_od_gnorm_w[0]]
    small_v = [v_ev_dw_w[0], v_od_norm_w[0], v_od_conv_w[0], v_od_conv_b[0], v_od_gnorm_w[0]]
    out_small = upd(ssmall, small_w, small_m, small_v, lambda a: _pack(a, 8), "adamw_small", ssmall.shape[1])

    def rep_pack(a):
        return _pack_rep(a[0:5], a[5:8])

    rep_w = [ev_norm_w, ev_dw_b, ev_ln_w, ev_ln_b, final_norm_w, od_dt_bias, od_a_log, od_d]
    rep_m = [m_ev_norm_w, m_ev_dw_b, m_ev_ln_w, m_ev_ln_b, m_final_norm_w, m_od_dt_bias, m_od_a_log, m_od_d]
    rep_v = [v_ev_norm_w, v_ev_dw_b, v_ev_ln_w, v_ev_ln_b, v_final_norm_w, v_od_dt_bias, v_od_a_log, v_od_d]
    out_rep = upd(srep, rep_w, rep_m, rep_v, rep_pack, "adamw_rep", srep.shape[1])

    results = []
    for kind in range(4):
        bw = [o[kind].reshape((1,) + o[kind].shape) for o in out_bigs]
        sw = _unpack(out_small[kind], small_shapes)
        vecs, scal = _unpack_rep(out_rep[kind], rep_vec_shapes, od_dt_bias.shape)
        results.append([
            vecs[0], bw[0], sw[0].reshape(ev_dw_w.shape), vecs[1], vecs[2], vecs[3], bw[2],
            sw[1].reshape(od_norm_w.shape), bw[1], sw[2].reshape(od_conv_w.shape), sw[3].reshape(od_conv_b.shape),
            scal[0], scal[1], scal[2], sw[4].reshape(od_gnorm_w.shape), bw[3], vecs[4]])
    loss = lax.psum(loss_part[0, 0], ("x", "y", "c"))
    return (loss, grad_x.reshape(x.shape), *results[0], *results[1], *results[2], *results[3])
```

```python
import jax
import jax.numpy as jnp
from jax import lax
from jax.experimental import pallas as pl
from jax.experimental.pallas import tpu as pltpu

F32 = jnp.float32
BF16 = jnp.bfloat16

D_MODEL = 1024
CONV_W = 1024
ATT_W = 1024
HEAD_DIM = 128
N_HEADS = 8
CONF_K = 31
IN_EVEN = 7168
D_INNER = 2048
SSM_P = 64
SSM_H = 32
SSM_G = 4
SSM_R = SSM_H // SSM_G
SSM_N = 128
SSM_K = 4
CHUNK = 128
XBC = D_INNER + 2 * SSM_G * SSM_N
IN_ODD = D_INNER + XBC + SSM_H
IN_ODD_PAD = 5376
EPS = 1e-6
QB = 128
NEG_CUT = -100.0

ADAM_LR = 0.001
ADAM_B1 = 0.9
ADAM_B2 = 0.999
ADAM_EPS = 1e-08
ADAM_WD = 0.01
ADAM_STEP = 10

LANE = 128
VMEM_LIMIT = 56 * 1024 * 1024
MESH = pl.DeviceIdType.MESH

NN = (((1,), (0,)), ((), ()))
NT = (((1,), (1,)), ((), ()))
TN = (((0,), (0,)), ((), ()))


def _pallas(body, **kw):
    return pl.pallas_call(body, **kw)


def _params(n_axes):
    return pltpu.CompilerParams(dimension_semantics=("arbitrary",) * n_axes, vmem_limit_bytes=VMEM_LIMIT)


def _dot(a, b, dims=NN):
    return lax.dot_general(a.astype(BF16), b.astype(BF16), dims, preferred_element_type=F32)


def _parts(x):
    h = x.astype(BF16)
    r = x - h.astype(F32)
    m = r.astype(BF16)
    l = (r - m.astype(F32)).astype(BF16)
    return (h, m, l)


def _dotx(x, e01, dims=NN):
    acc = None
    for p in _parts(x):
        t = lax.dot_general(p, e01, dims, preferred_element_type=F32)
        acc = t if acc is None else acc + t
    return acc


def _dotx2(x, e01, dims=NN):
    h = x.astype(BF16)
    l = (x - h.astype(F32)).astype(BF16)
    return (lax.dot_general(h, e01, dims, preferred_element_type=F32)
            + lax.dot_general(l, e01, dims, preferred_element_type=F32))


def _xdot(e01, x, dims=NN):
    acc = None
    for p in _parts(x):
        t = lax.dot_general(e01, p, dims, preferred_element_type=F32)
        acc = t if acc is None else acc + t
    return acc


def _f32(x):
    return x.astype(F32)


def _sigmoid(x):
    return 1.0 / (1.0 + jnp.exp(-x))


def _dsilu(x, s):
    return s * (1.0 + x * (1.0 - s))


def _matmul(a, b, *, mode, out_dtype, bm, bn, bk, name, residual=None, n_major=False, comm_kinds=(), comm_srcs=()):
    if mode == "nn":
        (m, k), n = a.shape, b.shape[1]
        a_blk, a_map = (bm, bk), lambda i, j, kk: (i, kk)
        b_blk, b_map = (bk, bn), lambda i, j, kk: (kk, j)
        dims = NN
    elif mode == "nt":
        (m, k), n = a.shape, b.shape[0]
        a_blk, a_map = (bm, bk), lambda i, j, kk: (i, kk)
        b_blk, b_map = (bn, bk), lambda i, j, kk: (j, kk)
        dims = NT
    else:
        (k, m), n = a.shape, b.shape[1]
        a_blk, a_map = (bk, bm), lambda i, j, kk: (kk, i)
        b_blk, b_map = (bk, bn), lambda i, j, kk: (kk, j)
        dims = TN
    bm, bn, bk = min(bm, m), min(bn, n), min(bk, k)
    if mode != "nn":
        a_blk = (bm, bk) if mode == "nt" else (bk, bm)
        b_blk = (bn, bk) if mode == "nt" else (bk, bn)
    else:
        a_blk, b_blk = (bm, bk), (bk, bn)
    assert m % bm == 0 and n % bn == 0 and k % bk == 0, (name, m, n, k)
    nk = k // bk
    has_res = residual is not None

    def order(f):
        return (lambda j, i, kk: f(i, j, kk)) if n_major else f

    nw = len(comm_srcs)
    grid = (n // bn, m // bm, nk) if n_major else (m // bm, n // bn, nk)

    def body(*refs):
        a_ref, b_ref = refs[0], refs[1]
        r_ref = refs[2] if has_res else None
        n_in = 2 + has_res + nw
        o_ref = refs[n_in]
        n_out = n_in + 1 + nw

        def finish(r):
            if has_res:
                r = r + r_ref[...]
            o_ref[...] = r.astype(out_dtype)

        def compute():
            if nk == 1:
                finish(_dot(a_ref[...], b_ref[...], dims))
                return
            acc_ref = refs[n_out]
            kk = pl.program_id(2)

            @pl.when(kk == 0)
            def _():
                acc_ref[...] = jnp.zeros_like(acc_ref)

            acc_ref[...] += _dot(a_ref[...], b_ref[...], dims)

            @pl.when(kk == nk - 1)
            def _():
                finish(acc_ref[...])

        if not nw:
            compute()
            return
        start, done = _chip_plan(comm_kinds, refs[2 + has_res:n_in], refs[n_in + 1:n_out], refs[n_out + (nk > 1):])
        ids = [pl.program_id(ax) for ax in range(3)]

        @pl.when(jnp.logical_and(jnp.logical_and(ids[0] == 0, ids[1] == 0), ids[2] == 0))
        def _():
            start()

        compute()

        @pl.when(jnp.logical_and(jnp.logical_and(ids[0] == grid[0] - 1, ids[1] == grid[1] - 1), ids[2] == grid[2] - 1))
        def _():
            done()

    in_specs = [pl.BlockSpec(a_blk, order(a_map)), pl.BlockSpec(b_blk, order(b_map))]
    args = [a, b]
    out_map = order(lambda i, j, kk: (i, j))
    if has_res:
        in_specs.append(pl.BlockSpec((bm, bn), out_map))
        args.append(residual)
    any_spec = pl.BlockSpec(memory_space=pl.ANY)
    out_specs = [pl.BlockSpec((bm, bn), out_map)] + [any_spec] * nw
    out_shape = [jax.ShapeDtypeStruct((m, n), out_dtype)] + [_landing_shape(kd, s) for kd, s in zip(comm_kinds, comm_srcs)]
    res = _pallas(
        body, name=name, grid=grid, in_specs=in_specs + [any_spec] * nw, out_specs=out_specs, out_shape=out_shape,
        scratch_shapes=([pltpu.VMEM((bm, bn), F32)] if nk > 1 else []) + (_chip_sems(nw) if nw else []),
        compiler_params=_params(3),
    )(*args, *comm_srcs)
    return res if nw else res[0]


def _rms_fwd(x, w, *, name, tm=512):
    t, d = x.shape

    def body(x_ref, w_ref, o_ref):
        xv = x_ref[...]
        r = lax.rsqrt(jnp.mean(xv * xv, axis=1, keepdims=True) + EPS)
        o_ref[...] = (xv * r * w_ref[...]).astype(BF16)

    return _pallas(
        body, name=name, grid=(t // tm,),
        in_specs=[pl.BlockSpec((tm, d), lambda i: (i, 0)), pl.BlockSpec((1, d), lambda i: (0, 0))],
        out_specs=pl.BlockSpec((tm, d), lambda i: (i, 0)),
        out_shape=jax.ShapeDtypeStruct((t, d), BF16), compiler_params=_params(1),
    )(x, w)


def _rms_bwd(dn, x, w, dres, *, name, tm=512):
    t, d = x.shape

    def body(dn_ref, x_ref, w_ref, dr_ref, dx_ref, dw_ref):
        i = pl.program_id(0)
        xv = x_ref[...]
        r = lax.rsqrt(jnp.mean(xv * xv, axis=1, keepdims=True) + EPS)
        xh = xv * r
        dy = dn_ref[...].astype(F32)
        g = dy * w_ref[...]
        dx_ref[...] = dr_ref[...] + r * (g - xh * jnp.mean(g * xh, axis=1, keepdims=True))

        @pl.when(i == 0)
        def _():
            dw_ref[...] = jnp.zeros_like(dw_ref)

        dw_ref[...] += jnp.sum(dy * xh, axis=0, keepdims=True)

    row = pl.BlockSpec((tm, d), lambda i: (i, 0))
    vec = pl.BlockSpec((1, d), lambda i: (0, 0))
    return _pallas(
        body, name=name, grid=(t // tm,), in_specs=[row, row, vec, row], out_specs=[row, vec],
        out_shape=[jax.ShapeDtypeStruct((t, d), F32), jax.ShapeDtypeStruct((1, d), F32)],
        compiler_params=_params(1),
    )(dn, x, w, dres)


def _final_loss(h, w, target, *, tm=512):
    t, d = h.shape

    def body(h_ref, w_ref, t_ref, dh_ref, dw_ref, loss_ref):
        i = pl.program_id(0)
        xv = h_ref[...]
        r = lax.rsqrt(jnp.mean(xv * xv, axis=1, keepdims=True) + EPS)
        xh = xv * r
        wv = w_ref[...]
        err = xh * wv - t_ref[...]
        dy = err * (1.0 / d)
        g = dy * wv
        dh_ref[...] = r * (g - xh * jnp.mean(g * xh, axis=1, keepdims=True))

        @pl.when(i == 0)
        def _():
            dw_ref[...] = jnp.zeros_like(dw_ref)
            loss_ref[...] = jnp.zeros_like(loss_ref)

        dw_ref[...] += jnp.sum(dy * xh, axis=0, keepdims=True)
        part = jnp.sum(jnp.sum(err * err, axis=1, keepdims=True), axis=0, keepdims=True)
        loss_ref[...] += part * (0.5 / d)

    row = pl.BlockSpec((tm, d), lambda i: (i, 0))
    vec = pl.BlockSpec((1, d), lambda i: (0, 0))
    return _pallas(
        body, name="final_loss", grid=(t // tm,), in_specs=[row, vec, row],
        out_specs=[row, vec, pl.BlockSpec((1, LANE), lambda i: (0, 0))],
        out_shape=[jax.ShapeDtypeStruct((t, d), F32), jax.ShapeDtypeStruct((1, d), F32),
                   jax.ShapeDtypeStruct((1, LANE), F32)],
        compiler_params=_params(1),
    )(h, w, target)


HALO = 32


SUB = 8
RC = 16


def _make_shifts(sh_ref, rows, shifts=tuple(range(1, SUB))):
    for s in shifts:
        sh_ref[s, 0:rows, :] = sh_ref[0, s:s + rows, :]


def _shifted(sh_ref, r0, j, rows):
    return sh_ref[j % SUB, pl.ds(r0 + (j - j % SUB), rows), :]


def _taps(w8_ref, sh_ref, r0, first, step, init):
    accs = [init] * (RC // SUB)
    for k in range(CONF_K):
        wk = w8_ref[k * SUB:(k + 1) * SUB, :]
        x = _shifted(sh_ref, r0, first + step * k, RC)
        accs = [a + wk * x[q * SUB:(q + 1) * SUB] for q, a in enumerate(accs)]
    return jnp.concatenate(accs, axis=0)


def _conf_fwd(proj, dw_w, dw_b, ln_w, ln_b, seq, *, tm=256):
    t = proj.shape[0]
    c = CONV_W
    tps = seq // tm
    hb = tm // HALO

    def body(a_ref, b_ref, g_ref, ha_ref, hb_ref, w_ref, wb_ref, lw_ref, lb_ref, y_ref, u2_ref, sh_ref):
        i = pl.program_id(0)
        keep = jnp.where(i % tps == 0, 0.0, 1.0)
        sh_ref[0, 0:HALO, :] = _f32(ha_ref[...]) * _sigmoid(_f32(hb_ref[...])) * keep
        sh_ref[0, HALO:HALO + tm, :] = _f32(a_ref[...]) * _sigmoid(_f32(b_ref[...]))
        _make_shifts(sh_ref, tm + HALO - SUB)

        def chunk(ci, carry):
            r0 = pl.multiple_of(ci * RC, RC)
            acc = _taps(w_ref, sh_ref, r0, HALO - CONF_K + 1, 1, jnp.broadcast_to(wb_ref[...], (SUB, c)))
            u2_ref[pl.ds(r0, RC), :] = acc
            mu = jnp.mean(acc, axis=1, keepdims=True)
            xc = acc - mu
            rs = lax.rsqrt(jnp.mean(xc * xc, axis=1, keepdims=True) + EPS)
            u3 = xc * rs * lw_ref[...] + lb_ref[...]
            gv = _f32(g_ref[pl.ds(r0, RC), :])
            y_ref[pl.ds(r0, RC), :] = (u3 * _sigmoid(u3) * gv * _sigmoid(gv)).astype(BF16)
            return carry

        lax.fori_loop(0, tm // RC, chunk, 0, unroll=8)

    def col(j):
        return pl.BlockSpec((tm, c), lambda i: (i, j))

    def prev(j):
        return pl.BlockSpec((HALO, c), lambda i: (jnp.maximum(i * hb - 1, 0), j))

    vec = pl.BlockSpec((1, c), lambda i: (0, 0))
    return _pallas(
        body, name="conf_fwd", grid=(t // tm,),
        in_specs=[col(0), col(1), col(2), prev(0), prev(1),
                  pl.BlockSpec((HALO * SUB, c), lambda i: (0, 0)), vec, vec, vec],
        out_specs=[pl.BlockSpec((tm, c), lambda i: (i, 0)), pl.BlockSpec((tm, c), lambda i: (i, 0))],
        out_shape=[jax.ShapeDtypeStruct((t, c), BF16), jax.ShapeDtypeStruct((t, c), F32)],
        scratch_shapes=[pltpu.VMEM((SUB, tm + HALO, c), F32)], compiler_params=_params(1),
    )(proj, proj, proj, proj, proj, dw_w, dw_b, ln_w, ln_b)


def _conf_bwd(proj, u2, dycat, dw_w, ln_w, ln_b, seq, comm_kinds, comm_srcs, *, tm=256):
    t = proj.shape[0]
    c = CONV_W
    tps = seq // tm
    hb = tm // HALO
    nhb = t // HALO
    nw = len(comm_srcs)
    nsteps = t // tm

    def fold(v):
        out = v[0:SUB]
        for q in range(1, RC // SUB):
            out = out + v[q * SUB:(q + 1) * SUB]
        return out

    def body(*refs):
        (a_ref, b_ref, g_ref, pa_ref, pb_ref, ng_ref, u2_ref, nu2_ref, dy_ref, ndy_ref,
         w_ref, lw_ref, lb_ref) = refs[:13]
        dp_ref, dww_ref, dwb_ref, dlw_ref, dlb_ref = refs[13 + nw:18 + nw]
        su_ref, sd_ref = refs[18 + 2 * nw:20 + 2 * nw]
        comm_start, comm_finish = _chip_plan(comm_kinds, refs[13:13 + nw], refs[18 + nw:18 + 2 * nw],
                                             refs[20 + 2 * nw:])
        i = pl.program_id(0)
        first = i % tps == 0
        last = i % tps == tps - 1

        @pl.when(i == 0)
        def _():
            comm_start()
            dww_ref[...] = jnp.zeros_like(dww_ref)
            dwb_ref[...] = jnp.zeros_like(dwb_ref)
            dlw_ref[...] = jnp.zeros_like(dlw_ref)
            dlb_ref[...] = jnp.zeros_like(dlb_ref)

        su_ref[0, 0:HALO, :] = _f32(pa_ref[...]) * _sigmoid(_f32(pb_ref[...])) * jnp.where(first, 0.0, 1.0)
        su_ref[0, HALO:HALO + tm, :] = _f32(a_ref[...]) * _sigmoid(_f32(b_ref[...]))
        _make_shifts(su_ref, tm + HALO - SUB)

        def ln_back(u2c, gv, dy):
            mu = jnp.mean(u2c, axis=1, keepdims=True)
            xc = u2c - mu
            rs = lax.rsqrt(jnp.mean(xc * xc, axis=1, keepdims=True) + EPS)
            xh = xc * rs
            lw = lw_ref[...]
            u3 = xh * lw + lb_ref[...]
            s3 = _sigmoid(u3)
            sg = _sigmoid(gv)
            dgc = dy * (u3 * s3) * _dsilu(gv, sg)
            du3 = dy * gv * sg * _dsilu(u3, s3)
            dxh = du3 * lw
            du2 = rs * (dxh - jnp.mean(dxh, axis=1, keepdims=True)
                        - xh * jnp.mean(dxh * xh, axis=1, keepdims=True))
            return du2, dgc, du3, xh

        def tile_chunk(ci, carry):
            r0 = pl.multiple_of(ci * RC, RC)
            rows = pl.ds(r0, RC)
            du2, dgc, du3, xh = ln_back(u2_ref[rows, :], _f32(g_ref[rows, :]), _f32(dy_ref[rows, :]))
            sd_ref[0, rows, :] = du2
            dp_ref[rows, 2 * c:3 * c] = dgc.astype(BF16)
            dwb_ref[...] += fold(du2)
            dlw_ref[...] += fold(du3 * xh)
            dlb_ref[...] += fold(du3)
            return carry

        lax.fori_loop(0, tm // RC, tile_chunk, 0, unroll=4)
        live = jnp.where(last, 0.0, 1.0)
        for ci in range(HALO // RC):
            rows = slice(ci * RC, (ci + 1) * RC)
            du2, _, _, _ = ln_back(nu2_ref[rows, :], _f32(ng_ref[rows, :]), _f32(ndy_ref[rows, :]))
            sd_ref[0, tm + ci * RC:tm + (ci + 1) * RC, :] = du2 * live
        _make_shifts(sd_ref, tm + HALO - SUB)

        def tap_chunk(ci, carry):
            r0 = pl.multiple_of(ci * RC, RC)
            rows = pl.ds(r0, RC)
            du1 = _taps(w_ref, sd_ref, r0, CONF_K - 1, -1, jnp.zeros((SUB, c), F32))
            sb = _sigmoid(_f32(b_ref[rows, :]))
            dp_ref[rows, 0:c] = (du1 * sb).astype(BF16)
            dp_ref[rows, c:2 * c] = (du1 * _f32(a_ref[rows, :]) * sb * (1.0 - sb)).astype(BF16)
            du2 = sd_ref[0, rows, :]
            for k in range(CONF_K):
                dww_ref[k * SUB:(k + 1) * SUB, :] += fold(du2 * _shifted(su_ref, r0, HALO - CONF_K + 1 + k, RC))
            return carry

        lax.fori_loop(0, tm // RC, tap_chunk, 0, unroll=4)

        @pl.when(i == nsteps - 1)
        def _():
            comm_finish()

    def col(j):
        return pl.BlockSpec((tm, c), lambda i: (i, j))

    def prev(j):
        return pl.BlockSpec((HALO, c), lambda i: (jnp.maximum(i * hb - 1, 0), j))

    def nxt(j):
        return pl.BlockSpec((HALO, c), lambda i: (jnp.minimum((i + 1) * hb, nhb - 1), j))

    vec = pl.BlockSpec((1, c), lambda i: (0, 0))
    acc = pl.BlockSpec((SUB, c), lambda i: (0, 0))
    any_spec = pl.BlockSpec(memory_space=pl.ANY)
    return _pallas(
        body, name="conf_bwd", grid=(nsteps,),
        in_specs=[col(0), col(1), col(2), prev(0), prev(1), nxt(2), col(0), nxt(0), col(0), nxt(0),
                  pl.BlockSpec((HALO * SUB, c), lambda i: (0, 0)), vec, vec] + [any_spec] * nw,
        out_specs=[pl.BlockSpec((tm, 3 * c), lambda i: (i, 0)),
                   pl.BlockSpec((HALO * SUB, c), lambda i: (0, 0)), acc, acc, acc] + [any_spec] * nw,
        out_shape=[jax.ShapeDtypeStruct((t, 3 * c), BF16), jax.ShapeDtypeStruct((HALO * SUB, c), F32),
                   jax.ShapeDtypeStruct((SUB, c), F32), jax.ShapeDtypeStruct((SUB, c), F32),
                   jax.ShapeDtypeStruct((SUB, c), F32)]
        + [_landing_shape(kd, s) for kd, s in zip(comm_kinds, comm_srcs)],
        scratch_shapes=[pltpu.VMEM((SUB, tm + HALO, c), F32), pltpu.VMEM((SUB, tm + HALO, c), F32)] + _chip_sems(nw),
        compiler_params=_params(1),
    )(proj, proj, proj, proj, proj, proj, u2, u2, dycat, dycat, dw_w, ln_w, ln_b, *comm_srcs)


Q_COL = 3 * CONV_W // HEAD_DIM
K_COL = Q_COL + N_HEADS
V_COL = K_COL + N_HEADS
GA_COL = V_COL + N_HEADS


SBA_TQ = 256
SBA_WK = 4 * QB


def _sb_window(qs, kw, ws, limit, t0, carry):
    tq, wk = qs.shape[0], kw.shape[0]
    z = _dot(qs, kw, NT)
    sg = ws + lax.broadcasted_iota(jnp.int32, (tq, wk), 1)
    tg = t0 + lax.broadcasted_iota(jnp.int32, (tq, wk), 0)
    mask = sg < jnp.minimum(tg, limit)
    sp = jnp.log(1.0 + jnp.exp(-jnp.abs(z)))
    ls = jnp.minimum(z, 0.0) - sp
    lk = jnp.where(mask, ls - z, 0.0)
    jj = lax.broadcasted_iota(jnp.int32, (QB, QB), 0)
    ss = lax.broadcasted_iota(jnp.int32, (QB, QB), 1)
    ustrict = jnp.where(jj > ss, 1.0, 0.0).astype(BF16)
    laters = [None] * (wk // QB)
    for ch in reversed(range(wk // QB)):
        lkc = lk[:, ch * QB:(ch + 1) * QB]
        laters[ch] = carry + _dotx2(lkc, ustrict)
        carry = carry + jnp.sum(lkc, axis=1, keepdims=True)
    w = jnp.where(mask, jnp.exp(ls + jnp.concatenate(laters, axis=1)), 0.0)
    return mask, ls, w, carry


def _sba_fwd(proj, nb, seq, *, tq=SBA_TQ, wk=SBA_WK):
    t = proj.shape[0]
    wk = min(wk, seq)
    nq = seq // tq
    scale = HEAD_DIM ** -0.5

    def body(q_ref, k_ref, v_ref, g_ref, o_ref, y_ref):
        i = pl.program_id(2)
        t0 = i * tq
        qs = (_f32(q_ref[...]) * scale).astype(BF16)

        def window(ws, limit, carry, acc):
            ws = pl.multiple_of(ws, QB)
            _, _, w, carry = _sb_window(qs, k_ref[pl.ds(ws, wk), :], ws, limit, t0, carry)
            return carry, acc + _dot(w, v_ref[pl.ds(ws, wk), :])

        ws0 = jnp.maximum(t0 + tq - wk, 0)
        carry, acc = window(ws0, seq, jnp.zeros((tq, 1), F32), jnp.zeros((tq, HEAD_DIM), F32))

        def cond(st):
            return jnp.logical_and(st[0] > 0, jnp.max(st[1]) > NEG_CUT)

        def step(st):
            c2, a2 = window(jnp.maximum(st[0] - wk, 0), st[0], st[1], st[2])
            return jnp.maximum(st[0] - wk, 0), c2, a2

        _, _, acc = lax.while_loop(cond, step, (ws0, carry, acc))
        o_ref[...] = acc
        gv = _f32(g_ref[...])
        y_ref[...] = (acc * gv * _sigmoid(gv)).astype(BF16)

    def tile(c0):
        return pl.BlockSpec((tq, HEAD_DIM), lambda b, h, i: (b * nq + i, c0 + h))

    def whole(c0):
        return pl.BlockSpec((seq, HEAD_DIM), lambda b, h, i: (b, c0 + h))

    return _pallas(
        body, name="sba_fwd", grid=(nb, N_HEADS, nq),
        in_specs=[tile(Q_COL), whole(K_COL), whole(V_COL), tile(GA_COL)],
        out_specs=[tile(0), tile(0)],
        out_shape=[jax.ShapeDtypeStruct((t, ATT_W), F32), jax.ShapeDtypeStruct((t, ATT_W), BF16)],
        compiler_params=_params(3),
    )(proj, proj, proj, proj)


def _sba_bwd(proj, o, dycat, nb, seq, *, tq=SBA_TQ, wk=SBA_WK):
    t = proj.shape[0]
    wk = min(wk, seq)
    nq = seq // tq
    nwin = -(-seq // wk) + 1
    nch = wk // QB
    scale = HEAD_DIM ** -0.5

    def body(q_ref, k_ref, v_ref, g_ref, o_ref, dy_ref, dq_ref, dko_ref, dvo_ref, dg_ref, e_ref, sp_ref,
             dk_ref, dv_ref):
        i = pl.program_id(2)
        t0 = i * tq

        @pl.when(i == 0)
        def _():
            dk_ref[...] = jnp.zeros_like(dk_ref)
            dv_ref[...] = jnp.zeros_like(dv_ref)

        qs = (_f32(q_ref[...]) * scale).astype(BF16)
        gv = _f32(g_ref[...])
        sg = _sigmoid(gv)
        dy = _f32(dy_ref[...])
        do = (dy * gv * sg).astype(BF16)
        dg_ref[...] = (dy * o_ref[...] * _dsilu(gv, sg)).astype(BF16)

        def start_of(n):
            return pl.multiple_of(jnp.maximum(t0 + tq - (n + 1) * wk, 0), QB)

        def limit_of(n):
            return jnp.where(n == 0, seq, jnp.maximum(t0 + tq - n * wk, 0))

        def near(n, carry):
            ws = start_of(n)
            _, ls, w, carry = _sb_window(qs, k_ref[pl.ds(ws, wk), :], ws, limit_of(n), t0, carry)
            e_ref[n] = w * _dot(do, v_ref[pl.ds(ws, wk), :], NT)
            sp_ref[n] = jnp.exp(ls)
            dv_ref[pl.ds(ws, wk), :] += _dot(w, do, TN)
            return carry

        carry = near(0, jnp.zeros((tq, 1), F32))

        def cond(st):
            return jnp.logical_and(start_of(st[0] - 1) > 0, jnp.max(st[1]) > NEG_CUT)

        def step(st):
            return st[0] + 1, near(st[0], st[1])

        nvis, _ = lax.while_loop(cond, step, (1, carry))

        jj = lax.broadcasted_iota(jnp.int32, (QB, QB), 0)
        ss = lax.broadcasted_iota(jnp.int32, (QB, QB), 1)
        lstrict = jnp.where(jj < ss, 1.0, 0.0).astype(BF16)

        def far(r, st):
            pre, dq = st
            n = nvis - 1 - r
            ws = start_of(n)
            e = e_ref[n]
            spn = sp_ref[n]
            gs = []
            for ch in range(nch):
                ec = e[:, ch * QB:(ch + 1) * QB]
                gs.append(pre + _dotx2(ec, lstrict))
                pre = pre + jnp.sum(ec, axis=1, keepdims=True)
            sgl = ws + lax.broadcasted_iota(jnp.int32, (tq, wk), 1)
            tgl = t0 + lax.broadcasted_iota(jnp.int32, (tq, wk), 0)
            mask = sgl < jnp.minimum(tgl, limit_of(n))
            dz = jnp.where(mask, e * (1.0 - spn) - jnp.concatenate(gs, axis=1) * spn, 0.0).astype(BF16)
            dk_ref[pl.ds(ws, wk), :] += _dot(dz, qs, TN)
            return pre, dq + _dot(dz, k_ref[pl.ds(ws, wk), :])

        _, dq = lax.fori_loop(0, nvis, far, (jnp.zeros((tq, 1), F32), jnp.zeros((tq, HEAD_DIM), F32)))
        dq_ref[...] = (dq * scale).astype(BF16)

        @pl.when(i == nq - 1)
        def _():
            dko_ref[...] = dk_ref[...].astype(BF16)
            dvo_ref[...] = dv_ref[...].astype(BF16)

    def tile(c0):
        return pl.BlockSpec((tq, HEAD_DIM), lambda b, h, i: (b * nq + i, c0 + h))

    def whole(c0):
        return pl.BlockSpec((seq, HEAD_DIM), lambda b, h, i: (b, c0 + h))

    return _pallas(
        body, name="sba_bwd", grid=(nb, N_HEADS, nq),
        in_specs=[tile(Q_COL), whole(K_COL), whole(V_COL), tile(GA_COL), tile(0),
                  tile(CONV_W // HEAD_DIM)],
        out_specs=[tile(0), whole(0), whole(0), tile(0)],
        out_shape=[jax.ShapeDtypeStruct((t, ATT_W), BF16)] * 4,
        scratch_shapes=[pltpu.VMEM((nwin, tq, wk), F32), pltpu.VMEM((nwin, tq, wk), F32),
                        pltpu.VMEM((seq, HEAD_DIM), F32), pltpu.VMEM((seq, HEAD_DIM), F32)],
        compiler_params=_params(3),
    )(proj, proj, proj, proj, o, dycat)


CT = 512
PH = 8
XRC = 32
X_SHIFTS = tuple(s for s in range(PH - SSM_K + 1, PH))
D_SHIFTS = tuple(range(1, SSM_K))
XBC_BLK = D_INNER // CT


def _softplus(x):
    return jnp.maximum(x, 0.0) + jnp.log(1.0 + jnp.exp(-jnp.abs(x)))


def _dt_fwd(proj, dt_bias, *, tm=512):
    t = proj.shape[0]

    def body(p_ref, b_ref, o_ref):
        o_ref[...] = _softplus(p_ref[...] + b_ref[...])

    return _pallas(
        body, name="dt_fwd", grid=(t // tm,),
        in_specs=[pl.BlockSpec((tm, LANE), lambda i: (i, 0)), pl.BlockSpec((1, LANE), lambda i: (0, 0))],
        out_specs=pl.BlockSpec((tm, LANE), lambda i: (i, 0)),
        out_shape=jax.ShapeDtypeStruct((t, LANE), F32), compiler_params=_params(1),
    )(proj, dt_bias)


def _dt_bwd(proj, dt_bias, ddt, dproj, *, tm=512):
    t = proj.shape[0]
    wide = IN_ODD_PAD - D_INNER - XBC

    def body(p_ref, b_ref, d_ref, dp_any, o_ref, db_ref):
        i = pl.program_id(0)
        lanes = lax.broadcasted_iota(jnp.int32, (tm, LANE), 1)
        dr = jnp.where(lanes < SSM_H, d_ref[...] * _sigmoid(p_ref[...] + b_ref[...]), 0.0)
        o_ref[:, 0:LANE] = dr.astype(BF16)
        o_ref[:, LANE:wide] = jnp.zeros((tm, wide - LANE), BF16)

        @pl.when(i == 0)
        def _():
            db_ref[...] = jnp.zeros_like(db_ref)

        db_ref[...] += jnp.sum(dr, axis=0, keepdims=True)

    vec = pl.BlockSpec((1, LANE), lambda i: (0, 0))
    row = pl.BlockSpec((tm, LANE), lambda i: (i, 0))
    return _pallas(
        body, name="dt_bwd", grid=(t // tm,),
        in_specs=[pl.BlockSpec((tm, LANE), lambda i: (i, 0)), vec, row, pl.BlockSpec(memory_space=pl.ANY)],
        out_specs=[pl.BlockSpec((tm, wide), lambda i: (i, (D_INNER + XBC) // wide)), vec],
        out_shape=[jax.ShapeDtypeStruct(dproj.shape, dproj.dtype), jax.ShapeDtypeStruct((1, LANE), F32)],
        input_output_aliases={3: 0}, compiler_params=_params(1),
    )(proj, dt_bias, ddt, dproj)


def _xconv_fwd(proj, conv_w, conv_b, seq, *, tm=512):
    t = proj.shape[0]
    tps = seq // tm
    hb = tm // PH

    def body(x_ref, h_ref, w_ref, b_ref, o_ref, sh_ref):
        i = pl.program_id(1)
        sh_ref[0, 0:PH, :] = _f32(h_ref[...]) * jnp.where(i % tps == 0, 0.0, 1.0)
        sh_ref[0, PH:PH + tm, :] = _f32(x_ref[...])
        _make_shifts(sh_ref, tm, X_SHIFTS)

        def chunk(ci, carry):
            r0 = pl.multiple_of(ci * XRC, XRC)
            acc = jnp.zeros((XRC, CT), F32) + b_ref[...]
            for k in range(SSM_K):
                acc = acc + w_ref[k:k + 1, :] * _shifted(sh_ref, r0, PH - SSM_K + 1 + k, XRC)
            o_ref[pl.ds(r0, XRC), :] = acc * _sigmoid(acc)
            return carry

        lax.fori_loop(0, tm // XRC, chunk, 0, unroll=8)

    return _pallas(
        body, name="xconv_fwd", grid=(XBC // CT, t // tm),
        in_specs=[pl.BlockSpec((tm, CT), lambda j, i: (i, XBC_BLK + j)),
                  pl.BlockSpec((PH, CT), lambda j, i: (jnp.maximum(i * hb - 1, 0), XBC_BLK + j)),
                  pl.BlockSpec((PH, CT), lambda j, i: (0, j)),
                  pl.BlockSpec((1, CT), lambda j, i: (0, j))],
        out_specs=pl.BlockSpec((tm, CT), lambda j, i: (i, j)),
        out_shape=jax.ShapeDtypeStruct((t, XBC), F32),
        scratch_shapes=[pltpu.VMEM((SUB, tm + PH, CT), F32)], compiler_params=_params(2),
    )(proj, proj, conv_w, conv_b)


def _xconv_bwd(proj, dxc, conv_w, conv_b, dproj, seq, *, tm=512):
    t = proj.shape[0]
    tps = seq // tm
    hb = tm // PH
    nhb = t // PH
    te = tm + PH

    def fold(v):
        out = v[0:SUB]
        for q in range(1, v.shape[0] // SUB):
            out = out + v[q * SUB:(q + 1) * SUB]
        return out

    def body(x_ref, p_ref, n_ref, d_ref, nd_ref, w_ref, b_ref, dp_any, dx_ref, dw_ref, db_ref, sx_ref, sd_ref):
        i = pl.program_id(1)
        first = i % tps == 0
        last = i % tps == tps - 1

        @pl.when(i == 0)
        def _():
            dw_ref[...] = jnp.zeros_like(dw_ref)
            db_ref[...] = jnp.zeros_like(db_ref)

        sx_ref[0, 0:PH, :] = _f32(p_ref[...]) * jnp.where(first, 0.0, 1.0)
        sx_ref[0, PH:PH + tm, :] = _f32(x_ref[...])
        sx_ref[0, PH + tm:PH + te, :] = _f32(n_ref[...])
        _make_shifts(sx_ref, te, X_SHIFTS)

        def dv_of(r0, rows, dy):
            acc = jnp.zeros((rows, CT), F32) + b_ref[...]
            for k in range(SSM_K):
                acc = acc + w_ref[k:k + 1, :] * _shifted(sx_ref, r0, PH - SSM_K + 1 + k, rows)
            return dy * _dsilu(acc, _sigmoid(acc))

        def dv_chunk(ci, carry):
            r0 = pl.multiple_of(ci * XRC, XRC)
            dv = dv_of(r0, XRC, d_ref[pl.ds(r0, XRC), :])
            sd_ref[0, pl.ds(r0, XRC), :] = dv
            db_ref[...] += fold(dv)
            return carry

        lax.fori_loop(0, tm // XRC, dv_chunk, 0, unroll=8)
        sd_ref[0, tm:te, :] = dv_of(tm, PH, nd_ref[...]) * jnp.where(last, 0.0, 1.0)
        _make_shifts(sd_ref, tm, D_SHIFTS)

        def tap_chunk(ci, carry):
            r0 = pl.multiple_of(ci * XRC, XRC)
            dx = jnp.zeros((XRC, CT), F32)
            for k in range(SSM_K):
                dx = dx + w_ref[k:k + 1, :] * _shifted(sd_ref, r0, SSM_K - 1 - k, XRC)
            dx_ref[pl.ds(r0, XRC), :] = dx.astype(BF16)
            dv = sd_ref[0, pl.ds(r0, XRC), :]
            for k in range(SSM_K):
                dw_ref[k * SUB:(k + 1) * SUB, :] += fold(dv * _shifted(sx_ref, r0, PH - SSM_K + 1 + k, XRC))
            return carry

        lax.fori_loop(0, tm // XRC, tap_chunk, 0, unroll=8)

    return _pallas(
        body, name="xconv_bwd", grid=(XBC // CT, t // tm),
        in_specs=[pl.BlockSpec((tm, CT), lambda j, i: (i, XBC_BLK + j)),
                  pl.BlockSpec((PH, CT), lambda j, i: (jnp.maximum(i * hb - 1, 0), XBC_BLK + j)),
                  pl.BlockSpec((PH, CT), lambda j, i: (jnp.minimum((i + 1) * hb, nhb - 1), XBC_BLK + j)),
                  pl.BlockSpec((tm, CT), lambda j, i: (i, j)),
                  pl.BlockSpec((PH, CT), lambda j, i: (jnp.minimum((i + 1) * hb, nhb - 1), j)),
                  pl.BlockSpec((PH, CT), lambda j, i: (0, j)),
                  pl.BlockSpec((1, CT), lambda j, i: (0, j)),
                  pl.BlockSpec(memory_space=pl.ANY)],
        out_specs=[pl.BlockSpec((tm, CT), lambda j, i: (i, XBC_BLK + j)),
                   pl.BlockSpec((PH * SUB, CT), lambda j, i: (0, j)),
                   pl.BlockSpec((SUB, CT), lambda j, i: (0, j))],
        out_shape=[jax.ShapeDtypeStruct(dproj.shape, dproj.dtype), jax.ShapeDtypeStruct((PH * SUB, XBC), F32),
                   jax.ShapeDtypeStruct((SUB, XBC), F32)],
        scratch_shapes=[pltpu.VMEM((SUB, tm + 2 * PH, CT), F32), pltpu.VMEM((SUB, te, CT), F32)],
        input_output_aliases={7: 0}, compiler_params=_params(2),
    )(proj, proj, proj, dxc, dxc, conv_w, conv_b, dproj)


def _ssd_common(xbc, dt, alog, ex):
    L = CHUNK
    a = -jnp.exp(alog)
    la = dt * a
    li = lax.broadcasted_iota(jnp.int32, (L, L), 0)
    si = lax.broadcasted_iota(jnp.int32, (L, L), 1)
    lower = si <= li
    tri = jnp.where(lower, 1.0, 0.0).astype(BF16)
    cs = _xdot(tri, la)
    cst = _dotx(la, tri, (((0,), (1,)), ((), ())))
    csl = cs[L - 1:L, :]
    ecs_x = _dotx2(jnp.exp(cs)[:, 0:SSM_H], ex)
    tail_x = _dotx2(jnp.exp(csl - cs)[:, 0:SSM_H], ex)
    dt_x = _dotx2(dt[:, 0:SSM_H], ex)
    return a, la, lower, tri, cs, cst, ecs_x, tail_x, dt_x


def _ssd_fwd(xbc_c, dt, a_log, ex, nb, seq):
    t = xbc_c.shape[0]
    L = CHUNK
    nc = seq // L
    GW = SSM_R * SSM_P

    def body(x_ref, dt_ref, al_ref, ex_ref, y_ref, st_ref, state):
        c = pl.program_id(1)

        @pl.when(c == 0)
        def _():
            state[...] = jnp.zeros_like(state)

        st_ref[0] = state[...]
        xbc = x_ref[...]
        _, _, lower, _, cs, cst, ecs_x, tail_x, dt_x = _ssd_common(xbc, dt_ref[...], al_ref[...], ex_ref[...])
        xd = xbc[:, 0:D_INNER] * dt_x
        xdb = xd.astype(BF16)
        xt = (xd * tail_x).astype(BF16)
        el_x = ecs_x[L - 1:L, :]
        for g in range(SSM_G):
            bg = xbc[:, D_INNER + g * SSM_N:D_INNER + (g + 1) * SSM_N].astype(BF16)
            cg = xbc[:, D_INNER + (SSM_G + g) * SSM_N:D_INNER + (SSM_G + g + 1) * SSM_N].astype(BF16)
            cb = _dot(cg, bg, NT)
            sg = state[:, g * GW:(g + 1) * GW]
            ys = _dot(cg, sg) * ecs_x[:, g * GW:(g + 1) * GW]
            for r in range(SSM_R):
                h = g * SSM_R + r
                seg = cs[:, h:h + 1] - cst[h:h + 1, :]
                dec = jnp.exp(jnp.where(lower, seg, -1e30))
                yh = _dot(cb * dec, xdb[:, h * SSM_P:(h + 1) * SSM_P])
                y_ref[:, h * SSM_P:(h + 1) * SSM_P] = yh + ys[:, r * SSM_P:(r + 1) * SSM_P]
            state[:, g * GW:(g + 1) * GW] = sg * el_x[:, g * GW:(g + 1) * GW] + _dot(bg, xt[:, g * GW:(g + 1) * GW], TN)

    return _pallas(
        body, name="ssd_fwd", grid=(nb, nc),
        in_specs=[pl.BlockSpec((L, XBC), lambda b, c: (b * nc + c, 0)),
                  pl.BlockSpec((L, LANE), lambda b, c: (b * nc + c, 0)),
                  pl.BlockSpec((1, LANE), lambda b, c: (0, 0)),
                  pl.BlockSpec((SSM_H, D_INNER), lambda b, c: (0, 0))],
        out_specs=[pl.BlockSpec((L, D_INNER), lambda b, c: (b * nc + c, 0)),
                   pl.BlockSpec((1, SSM_N, D_INNER), lambda b, c: (b * nc + c, 0, 0))],
        out_shape=[jax.ShapeDtypeStruct((t, D_INNER), F32),
                   jax.ShapeDtypeStruct((nb * nc, SSM_N, D_INNER), F32)],
        scratch_shapes=[pltpu.VMEM((SSM_N, D_INNER), F32)], compiler_params=_params(2),
    )(xbc_c, dt, a_log, ex)


def _ssd_bwd(xbc_c, dt, a_log, ex, ext, states, dy, d_x, nb, seq):
    t = xbc_c.shape[0]
    L = CHUNK
    nc = seq // L
    GW = SSM_R * SSM_P

    def body(x_ref, dt_ref, al_ref, ex_ref, ext_ref, st_ref, dy_ref, sk_ref, dx_ref, ddt_ref, da_ref,
             dstate, dxd, yd, lastv):
        b = pl.program_id(0)
        c = pl.program_id(1)

        @pl.when(c == 0)
        def _():
            dstate[...] = jnp.zeros_like(dstate)

        @pl.when(jnp.logical_and(b == 0, c == 0))
        def _():
            da_ref[...] = jnp.zeros_like(da_ref)

        xbc = x_ref[...]
        dtv = dt_ref[...]
        ex_t = ext_ref[...]
        a, la, lower, tri, cs, cst, ecs_x, tail_x, dt_x = _ssd_common(xbc, dtv, al_ref[...], ex_ref[...])
        xs = xbc[:, 0:D_INNER]
        xd = xs * dt_x
        xdb = xd.astype(BF16)
        dyv = dy_ref[...]
        dyb = dyv.astype(BF16)
        dys = dyv * ecs_x
        xt = xd * tail_x
        el_x = ecs_x[L - 1:L, :]
        lane = lax.broadcasted_iota(jnp.int32, (L, LANE), 1)
        sub = lax.broadcasted_iota(jnp.int32, (LANE, L), 0)
        row_part = jnp.zeros((L, LANE), F32)
        col_part = jnp.zeros((LANE, L), F32)
        for g in range(SSM_G):
            gs = slice(g * GW, (g + 1) * GW)
            bcol = slice(D_INNER + g * SSM_N, D_INNER + (g + 1) * SSM_N)
            ccol = slice(D_INNER + (SSM_G + g) * SSM_N, D_INNER + (SSM_G + g + 1) * SSM_N)
            bg = xbc[:, bcol].astype(BF16)
            cg = xbc[:, ccol].astype(BF16)
            cb = _dot(cg, bg, NT)
            sg = st_ref[0, :, gs]
            dsg = dstate[:, gs]
            dc = _dot(dys[:, gs], sg, NT)
            db = _dot(xt[:, gs], dsg, NT)
            dx_state = tail_x[:, gs] * _dot(bg, dsg)
            tail_part = xd[:, gs] * dx_state
            yd[:, gs] = dys[:, gs] * _dot(cg, sg) - tail_part
            last = jnp.sum(tail_part, axis=0, keepdims=True) + el_x[:, gs] * jnp.sum(dsg * sg, axis=0, keepdims=True)
            lastv[:, gs] = jnp.broadcast_to(last, (8, GW))
            dcb = jnp.zeros((L, L), F32)
            for r in range(SSM_R):
                h = g * SSM_R + r
                hs = slice(h * SSM_P, (h + 1) * SSM_P)
                seg = cs[:, h:h + 1] - cst[h:h + 1, :]
                dec = jnp.exp(jnp.where(lower, seg, -1e30))
                m = cb * dec
                dm = _dot(dyb[:, hs], xdb[:, hs], NT)
                dcb = dcb + dm * dec
                e = dm * m
                row_part = row_part + jnp.where(lane == h, jnp.sum(e, axis=1, keepdims=True), 0.0)
                col_part = col_part + jnp.where(sub == h, jnp.sum(e, axis=0, keepdims=True), 0.0)
                dxd[:, hs] = _dot(m, dyb[:, hs], TN) + dx_state[:, r * SSM_P:(r + 1) * SSM_P]
            dx_ref[:, bcol] = db + _dot(dcb, cg, TN)
            dx_ref[:, ccol] = dc + _dot(dcb, bg)
            dstate[:, gs] = dsg * el_x[:, gs] + _dot(cg, dys[:, gs], TN)
        dxv = dxd[...]
        dx_ref[:, 0:D_INNER] = dxv * dt_x + dyv * sk_ref[...]
        ddt_x = _dotx(dxv * xs, ex_t)
        yst = _dotx(yd[...], ex_t)
        lst = _dotx(lastv[...], ex_t)[0:1, :]
        rows = lax.broadcasted_iota(jnp.int32, (L, LANE), 0)
        dcs = row_part - col_part.T + yst + jnp.where(rows == L - 1, lst, 0.0)
        li = lax.broadcasted_iota(jnp.int32, (L, L), 0)
        si = lax.broadcasted_iota(jnp.int32, (L, L), 1)
        upper = jnp.where(si >= li, 1.0, 0.0).astype(BF16)
        dla = _xdot(upper, dcs)
        ddt_ref[...] = dla * a + ddt_x
        da_ref[...] += jnp.sum(dla * dtv, axis=0, keepdims=True)

    def row(w):
        return pl.BlockSpec((L, w), lambda b, c: (b * nc + nc - 1 - c, 0))

    return _pallas(
        body, name="ssd_bwd", grid=(nb, nc),
        in_specs=[row(XBC), row(LANE), pl.BlockSpec((1, LANE), lambda b, c: (0, 0)),
                  pl.BlockSpec((SSM_H, D_INNER), lambda b, c: (0, 0)),
                  pl.BlockSpec((D_INNER, LANE), lambda b, c: (0, 0)),
                  pl.BlockSpec((1, SSM_N, D_INNER), lambda b, c: (b * nc + nc - 1 - c, 0, 0)),
                  row(D_INNER), pl.BlockSpec((1, D_INNER), lambda b, c: (0, 0))],
        out_specs=[row(XBC), row(LANE), pl.BlockSpec((1, LANE), lambda b, c: (0, 0))],
        out_shape=[jax.ShapeDtypeStruct((t, XBC), F32), jax.ShapeDtypeStruct((t, LANE), F32),
                   jax.ShapeDtypeStruct((1, LANE), F32)],
        scratch_shapes=[pltpu.VMEM((SSM_N, D_INNER), F32), pltpu.VMEM((L, D_INNER), F32),
                        pltpu.VMEM((L, D_INNER), F32), pltpu.VMEM((8, D_INNER), F32)],
        compiler_params=_params(2),
    )(xbc_c, dt, a_log, ex, ext, states, dy, d_x)


def _group_rms(y2):
    gw = D_INNER // SSM_G
    parts = []
    for g in range(SSM_G):
        v = y2[:, g * gw:(g + 1) * gw]
        r = lax.rsqrt(jnp.mean(v * v, axis=1, keepdims=True) + EPS)
        parts.append(jnp.broadcast_to(r, v.shape))
    return jnp.concatenate(parts, axis=1)


def _gate_fwd(y, xbc_c, proj, d_x, gn_w, *, tm=256):
    t = y.shape[0]

    def body(y_ref, x_ref, z_ref, d_ref, w_ref, o_ref):
        y1 = y_ref[...] + d_ref[...] * x_ref[...]
        zv = _f32(z_ref[...])
        y2 = y1 * zv * _sigmoid(zv)
        o_ref[...] = (y2 * _group_rms(y2) * w_ref[...]).astype(BF16)

    row = pl.BlockSpec((tm, D_INNER), lambda i: (i, 0))
    vec = pl.BlockSpec((1, D_INNER), lambda i: (0, 0))
    return _pallas(
        body, name="gate_fwd", grid=(t // tm,), in_specs=[row, row, row, vec, vec], out_specs=row,
        out_shape=jax.ShapeDtypeStruct((t, D_INNER), BF16), compiler_params=_params(1),
    )(y, xbc_c, proj, d_x, gn_w)


def _gate_bwd(dyg, y, xbc_c, proj, d_x, gn_w, *, tm=256):
    t = y.shape[0]
    gw = D_INNER // SSM_G

    def body(dg_ref, y_ref, x_ref, z_ref, d_ref, w_ref, dy_ref, dz_ref, dw_ref, dd_ref):
        i = pl.program_id(0)
        xv = x_ref[...]
        dxv = d_ref[...]
        y1 = y_ref[...] + dxv * xv
        zv = _f32(z_ref[...])
        sz = _sigmoid(zv)
        y2 = y1 * zv * sz
        rr = _group_rms(y2)
        xh = y2 * rr
        dg = _f32(dg_ref[...])
        gq = dg * w_ref[...]
        prod = gq * xh
        means = []
        for g in range(SSM_G):
            mg = jnp.mean(prod[:, g * gw:(g + 1) * gw], axis=1, keepdims=True)
            means.append(jnp.broadcast_to(mg, (tm, gw)))
        dy2 = rr * (gq - xh * jnp.concatenate(means, axis=1))
        dy1 = dy2 * zv * sz
        dy_ref[...] = dy1
        dz_ref[...] = (dy2 * y1 * _dsilu(zv, sz)).astype(BF16)

        @pl.when(i == 0)
        def _():
            dw_ref[...] = jnp.zeros_like(dw_ref)
            dd_ref[...] = jnp.zeros_like(dd_ref)

        dw_ref[...] += jnp.sum(dg * xh, axis=0, keepdims=True)
        dd_ref[...] += jnp.sum(dy1 * xv, axis=0, keepdims=True)

    row = pl.BlockSpec((tm, D_INNER), lambda i: (i, 0))
    vec = pl.BlockSpec((1, D_INNER), lambda i: (0, 0))
    return _pallas(
        body, name="gate_bwd", grid=(t // tm,), in_specs=[row, row, row, row, vec, vec],
        out_specs=[row, row, vec, vec],
        out_shape=[jax.ShapeDtypeStruct((t, D_INNER), F32),
                   jax.ShapeDtypeStruct((t, IN_ODD_PAD), BF16), jax.ShapeDtypeStruct((1, D_INNER), F32),
                   jax.ShapeDtypeStruct((1, D_INNER), F32)],
        compiler_params=_params(1),
    )(dyg, y, xbc_c, proj, d_x, gn_w)


ANY = pl.BlockSpec(memory_space=pl.ANY)


def _remote(src, dst, sems, k, to):
    send_sems, recv_sems = sems
    return pltpu.make_async_remote_copy(src_ref=src, dst_ref=dst, send_sem=send_sems.at[k], recv_sem=recv_sems.at[k],
                                        device_id=to, device_id_type=MESH)


NCHIP = 4


def _gathered_shape(kind, shard):
    r, n = shard.shape
    shape = {"cols": (r, NCHIP * n), "slab": (NCHIP, r, n), "rows": (NCHIP * r, n)}[kind]
    return jax.ShapeDtypeStruct(shape, shard.dtype)


def _gather_plan(kinds, shards, outs, sems, small=None):
    ici_s, ici_r, d2d_s, d2d_r = sems
    nw = len(shards)
    per = nw + (small is not None)

    def place():
        x, y, c = lax.axis_index("x"), lax.axis_index("y"), lax.axis_index("c")
        return 2 * x + y, c, (x, y, 1 - c), [(1 - x, y), (x, 1 - y), (1 - x, 1 - y)]

    def region(j, chip, half):
        r, n = shards[j].shape
        h = r // 2
        if kinds[j] == "cols":
            return outs[j].at[pl.ds(half * h, h), pl.ds(pl.multiple_of(chip * n, LANE), n)]
        if kinds[j] == "slab":
            return outs[j].at[chip, pl.ds(half * h, h), :]
        return outs[j].at[pl.ds(chip * r + half * h, h), :]

    def my_sends(me, c, peers):
        cps = []
        for k, (px, py) in enumerate(peers):
            for j in range(nw):
                h = shards[j].shape[0] // 2
                cps.append(_remote(shards[j].at[pl.ds(c * h, h), :], region(j, me, c), (ici_s, ici_r), per * k + j, (px, py, c)))
            if small is not None:
                cps.append(_remote(small[0], small[1].at[me], (ici_s, ici_r), per * k + nw, (px, py, c)))
        return cps

    def start():
        me, c, _, peers = place()
        for cp in my_sends(me, c, peers):
            cp.start()

    def finish():
        me, c, sib, peers = place()
        fwds = []
        for k, (px, py) in enumerate(peers):
            q = 2 * px + py
            for j in range(nw):
                d = region(j, q, c)
                _remote(d, d, (ici_s, ici_r), per * k + j, (px, py, c)).wait_recv()
                fwds.append(_remote(d, d, (d2d_s, d2d_r), nw * k + j, sib))
                fwds[-1].start()
            if small is not None:
                _remote(small[0], small[1].at[q], (ici_s, ici_r), per * k + nw, (px, py, c)).wait_recv()
        for k, (px, py) in enumerate(peers):
            for j in range(nw):
                d = region(j, 2 * px + py, 1 - c)
                _remote(d, d, (d2d_s, d2d_r), nw * k + j, sib).wait_recv()
        for cp in my_sends(me, c, peers) + fwds:
            cp.wait_send()

    return start, finish


def _gather_sems(nw, with_small):
    n_ici = 3 * (nw + with_small)
    return [pltpu.SemaphoreType.DMA((n_ici,)), pltpu.SemaphoreType.DMA((n_ici,)),
            pltpu.SemaphoreType.DMA((3 * nw,)), pltpu.SemaphoreType.DMA((3 * nw,))]


def _gather_shards(kinds, shards, small):
    nw = len(shards)

    def body(*refs):
        ins, sm, outs, osm, sems = refs[:nw], refs[nw], refs[nw + 1:2 * nw + 1], refs[2 * nw + 1], refs[2 * nw + 2:]
        start, finish = _gather_plan(kinds, ins, outs, sems, small=(sm, osm))
        start()
        finish()

    return _pallas(
        body, name="gather_shards", in_specs=[ANY] * (nw + 1), out_specs=[ANY] * (nw + 1),
        out_shape=[_gathered_shape(kd, s) for kd, s in zip(kinds, shards)]
        + [jax.ShapeDtypeStruct((NCHIP,) + small.shape, small.dtype)],
        scratch_shapes=_gather_sems(nw, 1),
    )(*shards, small)


def _place_cols(full, shard, *, name, tr=256):
    r, n = shard.shape

    def body(p_ref, full_any, s_ref, o_ref):
        o_ref[...] = s_ref[...]

    return _pallas(
        body, name=name,
        grid_spec=pltpu.PrefetchScalarGridSpec(
            num_scalar_prefetch=1, grid=(r // tr,),
            in_specs=[pl.BlockSpec(memory_space=pl.ANY), pl.BlockSpec((tr, n), lambda i, p: (i, 0))],
            out_specs=pl.BlockSpec((tr, n), lambda i, p: (i, p[0]))),
        out_shape=jax.ShapeDtypeStruct(full.shape, full.dtype), input_output_aliases={1: 0},
        compiler_params=_params(1),
    )(_chip_index(), full, shard)


def _matmul_with_gather(a, b, kinds, shards, *, out_dtype, bm, bn, name):
    (m, k), n = a.shape, b.shape[1]
    nw = len(shards)
    nj, ni = n // bn, m // bm

    def body(*refs):
        a_ref, b_ref, ins, o_ref = refs[0], refs[1], refs[2:2 + nw], refs[2 + nw]
        outs, sems = refs[3 + nw:3 + 2 * nw], refs[3 + 2 * nw:]
        start, finish = _gather_plan(kinds, ins, outs, sems)
        j, i = pl.program_id(0), pl.program_id(1)

        @pl.when(jnp.logical_and(j == 0, i == 0))
        def _():
            start()

        o_ref[...] = _dot(a_ref[...], b_ref[...]).astype(out_dtype)

        @pl.when(jnp.logical_and(j == nj - 1, i == ni - 1))
        def _():
            finish()

    return _pallas(
        body, name=name, grid=(nj, ni),
        in_specs=[pl.BlockSpec((bm, k), lambda j, i: (i, 0)), pl.BlockSpec((k, bn), lambda j, i: (0, j))] + [ANY] * nw,
        out_specs=[pl.BlockSpec((bm, bn), lambda j, i: (i, j))] + [ANY] * nw,
        out_shape=[jax.ShapeDtypeStruct((m, n), out_dtype)] + [_gathered_shape(kd, s) for kd, s in zip(kinds, shards)],
        scratch_shapes=_gather_sems(nw, 0), compiler_params=_params(2),
    )(a, b, *shards)


def _other_half(a, c):
    axis = a.ndim - 2
    h = a.shape[axis] // 2
    rows = pl.ds(pl.multiple_of((1 - c) * h, 8), h)
    return a.at[rows, :] if a.ndim == 2 else a.at[:, rows, :]


def _half_shape(a):
    axis = a.ndim - 2
    return jax.ShapeDtypeStruct(a.shape[:axis] + (a.shape[axis] // 2,) + a.shape[axis + 1:], a.dtype)


def _pair_swap(bigs, *, name):
    nb = len(bigs)

    def body(*refs):
        ins, outs, send_sems, recv_sems = refs[:nb], refs[nb:2 * nb], refs[2 * nb], refs[2 * nb + 1]
        x, y, c = lax.axis_index("x"), lax.axis_index("y"), lax.axis_index("c")
        pair = [_remote(_other_half(a, c), q, (send_sems, recv_sems), j, (x, y, 1 - c))
                for j, (a, q) in enumerate(zip(ins, outs))]
        for cp in pair:
            cp.start()
        for cp in pair:
            cp.wait()

    return _pallas(
        body, name=name, in_specs=[ANY] * nb, out_specs=[ANY] * nb, out_shape=[_half_shape(a) for a in bigs],
        scratch_shapes=[pltpu.SemaphoreType.DMA((nb,)), pltpu.SemaphoreType.DMA((nb,))],
    )(*bigs)


def _pair_exchange(bigs, gsmall, grep):
    nb = len(bigs)

    def body(*refs):
        ins, sm, rp = refs[:nb], refs[nb], refs[nb + 1]
        outs, osm, orp = refs[nb + 2:2 * nb + 2], refs[2 * nb + 2], refs[2 * nb + 3]
        pair_s, pair_r, send_sems, recv_sems, local_sems = refs[2 * nb + 4:]
        x, y, c = lax.axis_index("x"), lax.axis_index("y"), lax.axis_index("c")
        me = 4 * x + 2 * y + c
        chip = 2 * x + y
        sib = (x, y, 1 - c)
        pair = [_remote(_other_half(a, c), q, (pair_s, pair_r), j, sib) for j, (a, q) in enumerate(zip(ins, outs))]
        for cp in pair:
            cp.start()
        own = [pltpu.make_async_copy(sm.at[chip], osm.at[me], local_sems.at[0]),
               pltpu.make_async_copy(rp, orp.at[me], local_sems.at[1])]
        for cp in own:
            cp.start()
        peers = []
        for k in range(7):
            fx, fy, fc = ((k + 1) >> 2) & 1, ((k + 1) >> 1) & 1, (k + 1) & 1
            peers.append((1 - x if fx else x, 1 - y if fy else y, 1 - c if fc else c))
        sends = []
        for k, (px, py, pc) in enumerate(peers):
            sends.append(_remote(sm.at[2 * px + py], osm.at[me], (send_sems, recv_sems), 2 * k, (px, py, pc)))
            sends.append(_remote(rp, orp.at[me], (send_sems, recv_sems), 2 * k + 1, (px, py, pc)))
        for cp in sends:
            cp.start()
        for k, (px, py, pc) in enumerate(peers):
            slot = 4 * px + 2 * py + pc
            _remote(sm.at[chip], osm.at[slot], (send_sems, recv_sems), 2 * k, (px, py, pc)).wait_recv()
            _remote(rp, orp.at[slot], (send_sems, recv_sems), 2 * k + 1, (px, py, pc)).wait_recv()
        for cp in pair:
            cp.wait_recv()
        for cp in pair + sends:
            cp.wait_send()
        for cp in own:
            cp.wait()

    return _pallas(
        body, name="pair_exchange", in_specs=[ANY] * (nb + 2), out_specs=[ANY] * (nb + 2),
        out_shape=[_half_shape(a) for a in bigs]
        + [jax.ShapeDtypeStruct((8,) + gsmall.shape[1:], F32), jax.ShapeDtypeStruct((8,) + grep.shape, F32)],
        scratch_shapes=[pltpu.SemaphoreType.DMA((max(nb, 1),)), pltpu.SemaphoreType.DMA((max(nb, 1),)),
                        pltpu.SemaphoreType.DMA((14,)), pltpu.SemaphoreType.DMA((14,)),
                        pltpu.SemaphoreType.DMA((2,))],
    )(*bigs, gsmall, grep)


def _core_index():
    return lax.axis_index("c").astype(jnp.int32).reshape(1)


def _half_add(full, other, *, axis, block, name):
    nd = full.ndim
    nblk = other.shape[axis] // block[axis]
    grid = tuple(other.shape[d] // block[d] for d in range(nd))

    def body(c_ref, f_ref, o_ref, out_ref):
        out_ref[...] = (f_ref[...] + o_ref[...]).astype(BF16)

    def full_map(*idx):
        ids, c_ref = list(idx[:nd]), idx[nd]
        ids[axis] = ids[axis] + c_ref[0] * nblk
        return tuple(ids)

    def plain_map(*idx):
        return tuple(idx[:nd])

    return _pallas(
        body, name=name,
        grid_spec=pltpu.PrefetchScalarGridSpec(
            num_scalar_prefetch=1, grid=grid,
            in_specs=[pl.BlockSpec(block, full_map), pl.BlockSpec(block, plain_map)],
            out_specs=pl.BlockSpec(block, plain_map)),
        out_shape=jax.ShapeDtypeStruct(other.shape, BF16), compiler_params=_params(nd),
    )(_core_index(), full, other)


NPEER = 3


def _landing_shape(kind, src):
    if kind == "pair":
        return _half_shape(src)
    if kind == "cols":
        return jax.ShapeDtypeStruct((NPEER, src.shape[0], src.shape[1] // NCHIP), src.dtype)
    return jax.ShapeDtypeStruct((NPEER,) + src.shape[1:], src.dtype)


def _chip_plan(kinds, srcs, lands, sems):
    nw = len(srcs)
    chipwise = [j for j in range(nw) if kinds[j] != "pair"]
    pairwise = [j for j in range(nw) if kinds[j] == "pair"]

    def place():
        x, y, c = lax.axis_index("x"), lax.axis_index("y"), lax.axis_index("c")
        return 2 * x + y, c, [(1 - x, y), (x, 1 - y), (1 - x, 1 - y)]

    def pair_copies(c):
        sib = (lax.axis_index("x"), lax.axis_index("y"), 1 - c)
        return [_remote(_other_half(srcs[j], c), lands[j], sems, j, sib) for j in pairwise]

    def piece(j, chip):
        if kinds[j] == "cols":
            n = srcs[j].shape[1] // NCHIP
            return srcs[j].at[:, pl.ds(pl.multiple_of(chip * n, LANE), n)]
        return srcs[j].at[chip]

    def my_sends(c, peers):
        return pair_copies(c) + [_remote(piece(j, 2 * px + py), lands[j].at[k], sems, nw * k + j, (px, py, c))
                                 for k, (px, py) in enumerate(peers) for j in chipwise]

    def start():
        _, c, peers = place()
        for cp in my_sends(c, peers):
            cp.start()

    def finish():
        me, c, peers = place()
        for cp in pair_copies(c):
            cp.wait_recv()
        for k, (px, py) in enumerate(peers):
            for j in chipwise:
                _remote(piece(j, me), lands[j].at[k], sems, nw * k + j, (px, py, c)).wait_recv()
        for cp in my_sends(c, peers):
            cp.wait_send()

    return start, finish


def _chip_sems(nw):
    return [pltpu.SemaphoreType.DMA((NPEER * nw,)), pltpu.SemaphoreType.DMA((NPEER * nw,))]


def _chip_index():
    return (2 * lax.axis_index("x") + lax.axis_index("y")).astype(jnp.int32).reshape(1)


def _chip_sum(own, slots, *, own_block, own_map, block, name):
    npeer = slots.shape[0]
    shape = slots.shape[1:]
    grid = (shape[0] // block[0], shape[1] // block[1])

    def body(p_ref, own_ref, s_ref, o_ref):
        acc = own_ref[...].reshape(block).astype(F32)
        for q in range(npeer):
            acc = acc + s_ref[q].astype(F32)
        o_ref[...] = acc

    return _pallas(
        body, name=name,
        grid_spec=pltpu.PrefetchScalarGridSpec(
            num_scalar_prefetch=1, grid=grid,
            in_specs=[pl.BlockSpec(own_block, own_map),
                      pl.BlockSpec((npeer,) + block, lambda i, j, p: (0, i, j))],
            out_specs=pl.BlockSpec(block, lambda i, j, p: (i, j))),
        out_shape=jax.ShapeDtypeStruct(shape, F32), compiler_params=_params(2),
    )(_chip_index(), own, slots)


def _pair_share(r_in0, r_in1, r_out0, r_out1):
    def body(a0, a1, b0, b1, g0, g1, h0, h1, send_sems, recv_sems):
        x, y, c = lax.axis_index("x"), lax.axis_index("y"), lax.axis_index("c")
        sib = (x, y, 1 - c)
        sends = [_remote(s, d, (send_sems, recv_sems), j, sib)
                 for j, (s, d) in enumerate(zip([a0, a1, b0, b1], [g0, g1, h0, h1]))]
        for cp in sends:
            cp.start()
        for cp in sends:
            cp.wait()

    return _pallas(
        body, name="pair_share", in_specs=[ANY] * 4, out_specs=[ANY] * 4,
        out_shape=[jax.ShapeDtypeStruct(r.shape, F32) for r in (r_in0, r_in1, r_out0, r_out1)],
        scratch_shapes=[pltpu.SemaphoreType.DMA((4,)), pltpu.SemaphoreType.DMA((4,))],
    )(r_in0, r_in1, r_out0, r_out1)


def _adam_math(g, w, m, v):
    c1 = 1.0 - ADAM_B1 ** ADAM_STEP
    c2 = 1.0 - ADAM_B2 ** ADAM_STEP
    m2 = ADAM_B1 * m + (1.0 - ADAM_B1) * g
    v2 = ADAM_B2 * v + (1.0 - ADAM_B2) * (g * g)
    delta = -ADAM_LR * ((m2 / c1) / (jnp.sqrt(v2 / c2) + ADAM_EPS) + ADAM_WD * w)
    return delta, m2, v2


def _adamw_nat(g_mine, g_sib, w, m, v, *, name, tr):
    rows, cw = w.shape
    nt = g_mine.shape[0] // tr

    def body(c_ref, gm_ref, gs_ref, w_ref, m_ref, v_ref, go_ref, d_ref, nm_ref, nv_ref):
        mine = pl.program_id(0) // nt == c_ref[0]
        gv = jnp.where(mine, gm_ref[...], gs_ref[...])[:, 0:cw]
        delta, m2, v2 = _adam_math(gv, w_ref[...], m_ref[...], v_ref[...])
        go_ref[...] = gv
        d_ref[...] = delta
        nm_ref[...] = m2
        nv_ref[...] = v2

    def mine_map(i, c_ref):
        return (jnp.where(i // nt == c_ref[0], i % nt, 0), 0)

    def sib_map(i, c_ref):
        return (jnp.where(i // nt == c_ref[0], 0, i % nt), 0)

    row = pl.BlockSpec((tr, cw), lambda i, c_ref: (i, 0))
    gspec = (tr, g_mine.shape[1])
    out = jax.ShapeDtypeStruct((rows, cw), F32)
    return _pallas(
        body, name=name,
        grid_spec=pltpu.PrefetchScalarGridSpec(
            num_scalar_prefetch=1, grid=(rows // tr,),
            in_specs=[pl.BlockSpec(gspec, mine_map), pl.BlockSpec(gspec, sib_map), row, row, row],
            out_specs=[row, row, row, row]),
        out_shape=[out, out, out, out], compiler_params=_params(1),
    )(_core_index(), g_mine, g_sib, w, m, v)


def _adamw(slots, w, m, v, *, name, tr):
    nd, rows, _ = slots.shape
    c1 = 1.0 - ADAM_B1 ** ADAM_STEP
    c2 = 1.0 - ADAM_B2 ** ADAM_STEP

    def body(s_ref, w_ref, m_ref, v_ref, g_ref, d_ref, nm_ref, nv_ref):
        g = s_ref[0]
        for d in range(1, nd):
            g = g + s_ref[d]
        m2 = ADAM_B1 * m_ref[...] + (1.0 - ADAM_B1) * g
        v2 = ADAM_B2 * v_ref[...] + (1.0 - ADAM_B2) * (g * g)
        g_ref[...] = g
        nm_ref[...] = m2
        nv_ref[...] = v2
        d_ref[...] = -ADAM_LR * ((m2 / c1) / (jnp.sqrt(v2 / c2) + ADAM_EPS) + ADAM_WD * w_ref[...])

    row = pl.BlockSpec((tr, LANE), lambda i: (i, 0))
    out = jax.ShapeDtypeStruct((rows, LANE), F32)
    return _pallas(
        body, name=name, grid=(rows // tr,),
        in_specs=[pl.BlockSpec((nd, tr, LANE), lambda i: (0, i, 0)), row, row, row],
        out_specs=[row, row, row, row], out_shape=[out, out, out, out], compiler_params=_params(1),
    )(slots, w, m, v)


def _rows(a):
    return a.reshape(-1, LANE)


def _pad_rows(a, mult):
    pad = (-a.shape[0]) % mult
    return jnp.pad(a, ((0, pad), (0, 0))) if pad else a


def _pack(parts, mult):
    return _pad_rows(jnp.concatenate([_rows(p) for p in parts], axis=0), mult)


def _unpack(slab, shapes):
    out, r0 = [], 0
    for shp in shapes:
        n = 1
        for s in shp:
            n *= s
        r = n // LANE
        out.append(slab[r0:r0 + r].reshape(shp))
        r0 += r
    return out


def _pack_rep(vecs, scal):
    srow = jnp.concatenate([s.reshape(-1) for s in scal] + [jnp.zeros((LANE - 3 * SSM_H,), F32)]).reshape(1, LANE)
    return _pad_rows(jnp.concatenate([_rows(vv) for vv in vecs] + [srow], axis=0), 8)


def _unpack_rep(slab, vec_shapes, scal_shape):
    vecs, r0 = [], 0
    for shp in vec_shapes:
        vecs.append(slab[r0:r0 + 8].reshape(shp))
        r0 += 8
    srow = slab[r0]
    scal = [srow[i * SSM_H:(i + 1) * SSM_H].reshape(scal_shape) for i in range(3)]
    return vecs, scal


def kernel(x, ev_norm_w, ev_w_in, ev_dw_w, ev_dw_b, ev_ln_w, ev_ln_b, ev_w_out, od_norm_w, od_w_in, od_conv_w, od_conv_b, od_dt_bias, od_a_log, od_d, od_gnorm_w, od_w_out, final_norm_w, loss_target, m_ev_norm_w, m_ev_w_in, m_ev_dw_w, m_ev_dw_b, m_ev_ln_w, m_ev_ln_b, m_ev_w_out, m_od_norm_w, m_od_w_in, m_od_conv_w, m_od_conv_b, m_od_dt_bias, m_od_a_log, m_od_d, m_od_gnorm_w, m_od_w_out, m_final_norm_w, v_ev_norm_w, v_ev_w_in, v_ev_dw_w, v_ev_dw_b, v_ev_ln_w, v_ev_ln_b, v_ev_w_out, v_od_norm_w, v_od_w_in, v_od_conv_w, v_od_conv_b, v_od_dt_bias, v_od_a_log, v_od_d, v_od_gnorm_w, v_od_w_out, v_final_norm_w):
    nb, seq, d = x.shape
    t = nb * seq
    nchip = 4
    xf = x.reshape(t, d)
    tgt = loss_target.reshape(t, d)

    big_w = [ev_w_in[0], od_w_in[0], ev_w_out[0], od_w_out[0]]
    small_w = [ev_dw_w[0], od_norm_w[0], od_conv_w[0], od_conv_b[0], od_gnorm_w[0]]
    small_shapes = [a.shape for a in small_w]
    big_b = [a.astype(BF16) for a in big_w]
    small_slab = _pack(small_w, 8)
    w_in0, w_out0, gath_small = _gather_shards(("cols", "rows"), [big_b[0], big_b[2]], small_slab)
    chip = 2 * lax.axis_index("x") + lax.axis_index("y")
    w_in0 = _place_cols(w_in0, big_b[0], name="place_w_in0")
    w_out0 = lax.dynamic_update_slice(w_out0, big_b[2], (chip * big_b[2].shape[0], 0))
    gath_small = lax.dynamic_update_slice(gath_small, small_slab[None], (chip, 0, 0))
    per_chip = [_unpack(gath_small[p], small_shapes) for p in range(nchip)]

    def cat(idx, axis):
        return jnp.concatenate([per_chip[p][idx] for p in range(nchip)], axis=axis)

    dw_w = jnp.pad(cat(0, 1), ((0, HALO - CONF_K), (0, 0)))
    dw_w8 = jnp.repeat(dw_w, SUB, axis=0)
    n1_w = cat(1, 0).reshape(1, d)
    conv_w = jnp.pad(cat(2, 1), ((0, PH - SSM_K), (0, 0)))
    conv_b = cat(3, 0).reshape(1, XBC)
    gn_w = cat(4, 0).reshape(1, D_INNER)

    def lanes(a):
        return jnp.pad(a.reshape(1, -1), ((0, 0), (0, LANE - a.size)))

    dt_bias, a_log = lanes(od_dt_bias), lanes(od_a_log)
    d_x = jnp.repeat(od_d.reshape(-1), SSM_P).reshape(1, D_INNER)
    hid = lax.broadcasted_iota(jnp.int32, (SSM_H, D_INNER), 1) // SSM_P
    ex = (hid == lax.broadcasted_iota(jnp.int32, (SSM_H, D_INNER), 0)).astype(BF16)
    ex_t = jnp.pad(ex.T, ((0, 0), (0, LANE - SSM_H)))
    fn_w = final_norm_w.reshape(1, d)

    n0 = _rms_fwd(xf, ev_norm_w, name="rms_fwd0")
    proj0, w_in1g, w_out1 = _matmul_with_gather(n0, w_in0, ("slab", "rows"), [big_b[1], big_b[3]],
                                                out_dtype=BF16, bm=512, bn=1024, name="in_proj0")
    w_in1g = lax.dynamic_update_slice(w_in1g, big_b[1][None], (chip, 0, 0))
    w_out1 = lax.dynamic_update_slice(w_out1, big_b[3], (chip * big_b[3].shape[0], 0))
    w_in1 = jnp.pad(jnp.concatenate([w_in1g[p] for p in range(nchip)], axis=1),
                    ((0, 0), (0, IN_ODD_PAD - IN_ODD)))
    y_conv, u2 = _conf_fwd(proj0, dw_w8, ev_dw_b, ev_ln_w, ev_ln_b, seq)
    o_att, y_att = _sba_fwd(proj0, nb, seq)
    ycat0 = jnp.concatenate([y_conv, y_att], axis=1)
    h1 = _matmul(ycat0, w_out0, mode="nn", out_dtype=F32, bm=512, bn=d, bk=D_INNER, name="out_proj0", residual=xf)
    n1 = _rms_fwd(h1, n1_w, name="rms_fwd1")
    proj1 = _matmul(n1, w_in1, mode="nn", out_dtype=BF16, bm=512, bn=768, bk=d, name="in_proj1", n_major=True)
    dt_raw = _matmul(n1, w_in1[:, D_INNER + XBC:IN_ODD_PAD], mode="nn", out_dtype=F32, bm=512,
                     bn=IN_ODD_PAD - D_INNER - XBC, bk=d, name="in_proj1_dt")
    xbc_c = _xconv_fwd(proj1, conv_w, conv_b, seq)
    dt = _dt_fwd(dt_raw, dt_bias)
    y_ssd, states = _ssd_fwd(xbc_c, dt, a_log, ex, nb, seq)
    yg = _gate_fwd(y_ssd, xbc_c, proj1, d_x, gn_w)
    h2 = _matmul(yg, w_out1, mode="nn", out_dtype=F32, bm=512, bn=d, bk=D_INNER, name="out_proj1", residual=h1)
    dh2, g_fn, loss_part = _final_loss(h2, fn_w, tgt)

    dyg = _matmul(dh2, w_out1, mode="nt", out_dtype=BF16, bm=512, bn=1024, bk=d, name="d_out_proj1")
    g_w_out1 = _matmul(yg, dh2, mode="tn", out_dtype=F32, bm=1024, bn=d, bk=1024, name="dw_out_proj1")
    dy_ssd, dz, g_gn, g_dx = _gate_bwd(dyg, y_ssd, xbc_c, proj1, d_x, gn_w)
    dxbc_c, ddt, g_a = _ssd_bwd(xbc_c, dt, a_log, ex, ex_t, states, dy_ssd, d_x, nb, seq)
    dproj1, g_conv_w, g_conv_b = _xconv_bwd(proj1, dxbc_c, conv_w, conv_b, dz, seq)
    dproj1, g_dt_bias = _dt_bwd(dt_raw, dt_bias, ddt, dproj1)
    dn1 = _matmul(dproj1, w_in1, mode="nt", out_dtype=BF16, bm=1024, bn=d, bk=1792, name="d_in_proj1")
    g_w_in1 = _matmul(n1, dproj1, mode="tn", out_dtype=F32, bm=d, bn=1792, bk=1024, name="dw_in_proj1")
    dh1, g_n1 = _rms_bwd(dn1, h1, n1_w, dh2, name="rms_bwd1")

    ro = D_INNER // nchip
    n1 = IN_ODD // nchip
    n1p = -(-n1 // LANE) * LANE
    g_w_out1c = g_w_out1.reshape(nchip, ro, d)
    dycat0, q_in1, q_out1 = _matmul(dh1, w_out0, mode="nt", out_dtype=BF16, bm=512, bn=1024, bk=d, name="d_out_proj0",
                                    comm_kinds=("pair", "pair"), comm_srcs=[g_w_in1, g_w_out1c])
    g_w_out0 = _matmul(ycat0, dh1, mode="tn", out_dtype=F32, bm=1024, bn=d, bk=1024, name="dw_out_proj0")
    dq, dk, dv, dga = _sba_bwd(proj0, o_att, dycat0, nb, seq)
    s_in1n = _half_add(g_w_in1, q_in1, axis=0, block=(128, IN_ODD_PAD), name="half_add_in1")
    s_in1 = jnp.stack([jnp.pad(s_in1n[:, p * n1:(p + 1) * n1], ((0, 0), (0, n1p - n1))) for p in range(nchip)])
    s_out1 = _half_add(g_w_out1c, q_out1, axis=1, block=(1, ro // 2, d), name="half_add_out1")
    dpc, g_dw_w, g_dw_b, g_ln_w, g_ln_b, l_in1, l_out1 = _conf_bwd(
        proj0, u2, dycat0, dw_w8, ev_ln_w, ev_ln_b, seq, ("slab", "slab"), [s_in1, s_out1])
    dproj0 = jnp.concatenate([dpc, dq, dk, dv, dga], axis=1)
    g_w_in0 = _matmul(n0, dproj0, mode="tn", out_dtype=F32, bm=d, bn=1792, bk=1024, name="dw_in_proj0")
    g_w_out0c = g_w_out0.reshape(nchip, ro, d)
    q_in0, q_out0 = _pair_swap([g_w_in0, g_w_out0c], name="pair_swap_l0")
    s_in0 = _half_add(g_w_in0, q_in0, axis=0, block=(128, IN_EVEN), name="half_add_in0")
    s_out0 = _half_add(g_w_out0c, q_out0, axis=1, block=(1, ro // 2, d), name="half_add_out0")
    dn0, l_in0, l_out0 = _matmul(dproj0, w_in0, mode="nt", out_dtype=BF16, bm=1024, bn=d, bk=1792, name="d_in_proj0",
                                 comm_kinds=("cols", "slab"), comm_srcs=[s_in0, s_out0])
    grad_x, g_n0 = _rms_bwd(dn0, xf, ev_norm_w, dh1, name="rms_bwd0")

    g_dw_w = g_dw_w.reshape(HALO, SUB, CONV_W).sum(axis=1)[0:CONF_K]
    g_dw_b, g_ln_w, g_ln_b = (a.sum(axis=0, keepdims=True) for a in (g_dw_b, g_ln_w, g_ln_b))
    g_conv_w = g_conv_w.reshape(PH, SUB, XBC).sum(axis=1)[0:SSM_K]
    g_conv_b = g_conv_b.sum(axis=0, keepdims=True)
    a_neg = -jnp.exp(od_a_log.reshape(-1))
    g_a_log = g_a[0, 0:SSM_H] * a_neg
    g_d = g_dx.reshape(SSM_H, SSM_P).sum(axis=1)

    def chip_slab_small(p):
        c0, c1, c2, c3 = CONV_W // nchip, d // nchip, XBC // nchip, D_INNER // nchip
        return _pack([g_dw_w[:, p * c0:(p + 1) * c0], g_n1[0, p * c1:(p + 1) * c1],
                      g_conv_w[:, p * c2:(p + 1) * c2], g_conv_b[0, p * c2:(p + 1) * c2],
                      g_gn[0, p * c3:(p + 1) * c3]], 8)

    gsmall = jnp.stack([chip_slab_small(p) for p in range(nchip)])
    rep_vec_shapes = [ev_norm_w.shape, ev_dw_b.shape, ev_ln_w.shape, ev_ln_b.shape, final_norm_w.shape]
    grep = _pack_rep([g_n0, g_dw_b, g_ln_w, g_ln_b, g_fn], [g_dt_bias[0, 0:SSM_H], g_a_log, g_d])

    ssmall, srep = _pair_exchange([], gsmall, grep)
    r_in0 = _chip_sum(s_in0, l_in0, own_block=(128, IN_EVEN // nchip), own_map=lambda i, j, p: (i, p[0]),
                      block=(128, IN_EVEN // nchip), name="chip_sum_in0")
    r_in1 = _chip_sum(s_in1, l_in1, own_block=(1, 256, n1p), own_map=lambda i, j, p: (p[0], i, 0),
                      block=(256, n1p), name="chip_sum_in1")
    r_out0 = _chip_sum(s_out0, l_out0, own_block=(1, ro // 2, d), own_map=lambda i, j, p: (p[0], 0, 0),
                       block=(ro // 2, d), name="chip_sum_out0")
    r_out1 = _chip_sum(s_out1, l_out1, own_block=(1, ro // 2, d), own_map=lambda i, j, p: (p[0], 0, 0),
                       block=(ro // 2, d), name="chip_sum_out1")
    big_r = [r_in0, r_in1, r_out0, r_out1]
    big_q = _pair_share(*big_r)

    big_m = [m_ev_w_in[0], m_od_w_in[0], m_ev_w_out[0], m_od_w_out[0]]
    big_v = [v_ev_w_in[0], v_od_w_in[0], v_ev_w_out[0], v_od_w_out[0]]
    big_names = ["adamw_in0", "adamw_in1", "adamw_out0", "adamw_out1"]
    out_bigs = [_adamw_nat(gm, gs, w, m, v, name=nm, tr=128)
                for gm, gs, w, m, v, nm in zip(big_r, big_q, big_w, big_m, big_v, big_names)]

    def upd(slots, ws, ms, vs, packer, name, tr):
        return _adamw(slots, packer(ws), packer(ms), packer(vs), name=name, tr=tr)

    small_m = [m_ev_dw_w[0], m_od_norm_w[0], m_od_conv_w[0], m_od_conv_b[0], m_od_gnorm_w[0]]
    small_v = [v_ev_dw_w[0], v_od_norm_w[0], v_od_conv_w[0], v_od_conv_b[0], v_od_gnorm_w[0]]
    out_small = upd(ssmall, small_w, small_m, small_v, lambda a: _pack(a, 8), "adamw_small", ssmall.shape[1])

    def rep_pack(a):
        return _pack_rep(a[0:5], a[5:8])

    rep_w = [ev_norm_w, ev_dw_b, ev_ln_w, ev_ln_b, final_norm_w, od_dt_bias, od_a_log, od_d]
    rep_m = [m_ev_norm_w, m_ev_dw_b, m_ev_ln_w, m_ev_ln_b, m_final_norm_w, m_od_dt_bias, m_od_a_log, m_od_d]
    rep_v = [v_ev_norm_w, v_ev_dw_b, v_ev_ln_w, v_ev_ln_b, v_final_norm_w, v_od_dt_bias, v_od_a_log, v_od_d]
    out_rep = upd(srep, rep_w, rep_m, rep_v, rep_pack, "adamw_rep", srep.shape[1])

    results = []
    for kind in range(4):
        bw = [o[kind].reshape((1,) + o[kind].shape) for o in out_bigs]
        sw = _unpack(out_small[kind], small_shapes)
        vecs, scal = _unpack_rep(out_rep[kind], rep_vec_shapes, od_dt_bias.shape)
        results.append([
            vecs[0], bw[0], sw[0].reshape(ev_dw_w.shape), vecs[1], vecs[2], vecs[3], bw[2],
            sw[1].reshape(od_norm_w.shape), bw[1], sw[2].reshape(od_conv_w.shape), sw[3].reshape(od_conv_b.shape),
            scal[0], scal[1], scal[2], sw[4].reshape(od_gnorm_w.shape), bw[3], vecs[4]])
    loss = lax.psum(loss_part[0, 0], ("x", "y", "c"))
    return (loss, grad_x.reshape(x.shape), *results[0], *results[1], *results[2], *results[3])
```
